```python
import jax, jax.numpy as jnp
from jax import lax
import numpy as np

D_MODEL = 1024
BATCH = 8
SEQ = 8192
DEPTH = 4

CHUNK = 64
N_EVEN = (DEPTH + 1) // 2
N_ODD = DEPTH // 2
D_FF = 4 * D_MODEL
RMS_EPS = 1e-6
RET_WIDTH = D_MODEL // 2
RET_HEADS = 4
RET_HEAD_DIM = RET_WIDTH // RET_HEADS
RET_ROPE_BASE = 10000.0
GN_EPS = 1e-5
POOL_WIDTH = D_MODEL - RET_WIDTH
POOL_WINDOWS = (2, 4, 8, 16)
POOL_GROUPS = len(POOL_WINDOWS)
POOL_GROUP_DIM = POOL_WIDTH // POOL_GROUPS
AB_IN_WIDTH = 4 * RET_WIDTH + POOL_WIDTH
AB_OUT_WIDTH = RET_WIDTH + POOL_WIDTH
ATT_HEADS = 16
ATT_HEAD_DIM = D_MODEL // ATT_HEADS
LEFT_CHUNKS = 8
BAND = (LEFT_CHUNKS + 1) * CHUNK
REL_CLIP = 128
N_REL = 2 * REL_CLIP + 1
NEG_INF = -1e30

kernel_name = "hybrid_retention_pool_chunkattn_trunk"


def rms_norm(x, g):
    xf = x.astype(jnp.float32)
    y = xf * lax.rsqrt(jnp.mean(xf * xf, axis=-1, keepdims=True) + RMS_EPS)
    return (y * g.astype(jnp.float32)).astype(x.dtype)


def rotary(x):
    S, d = x.shape[1], x.shape[-1]
    inv_freq = 1.0 / (RET_ROPE_BASE ** jnp.linspace(0.0, 1.0, d // 2, dtype=jnp.float32))
    ang = jnp.arange(S, dtype=jnp.float32)[:, None] * inv_freq[None, :]
    cos = jnp.cos(ang)[None, :, None, :]
    sin = jnp.sin(ang)[None, :, None, :]
    x1, x2 = x[..., 0::2], x[..., 1::2]
    return jnp.stack([x1 * cos - x2 * sin, x1 * sin + x2 * cos], axis=-1).reshape(x.shape)


def retention(q, k, v):
    B, S, H, d = q.shape
    nc = S // CHUNK
    log_g = jnp.log1p(-jnp.power(2.0, -5.0 - jnp.arange(H, dtype=jnp.float32)))
    pos = jnp.arange(CHUNK, dtype=jnp.float32)
    intra_decay = jnp.exp(jnp.abs(pos[:, None] - pos[None, :])[None] * log_g[:, None, None])
    q_decay = jnp.exp((pos[:, None] + 1.0) * log_g[None, :])
    k_decay = jnp.exp((CHUNK - 1.0 - pos[:, None]) * log_g[None, :])
    chunk_decay = jnp.exp(CHUNK * log_g)

    qc = q.reshape(B, nc, CHUNK, H, d)
    kc = k.reshape(B, nc, CHUNK, H, d)
    vc = v.reshape(B, nc, CHUNK, H, d)
    scores = jnp.einsum('bcnhd,bcmhd->bchnm', qc, kc) * intra_decay
    intra = jnp.einsum('bchnm,bcmhe->bcnhe', scores, vc)

    def step(state, inp):
        q_i, k_i, v_i = inp
        cross = jnp.einsum('bnhd,bhde->bnhe', q_i * q_decay[:, :, None], state)
        state = state * chunk_decay[:, None, None] + jnp.einsum(
            'bmhd,bmhe->bhde', k_i * k_decay[:, :, None], v_i)
        return state, cross

    xs = (jnp.moveaxis(qc, 1, 0), jnp.moveaxis(kc, 1, 0), jnp.moveaxis(vc, 1, 0))
    state0 = jnp.zeros((B, H, d, d), jnp.float32)
    _, cross = lax.scan(step, state0, xs)
    out = intra + jnp.moveaxis(cross, 0, 1)
    return out.reshape(B, S, H, d)


def head_group_norm(o, g):
    B, S, H, d = o.shape
    mu = jnp.mean(o, axis=-1, keepdims=True)
    var = jnp.mean(jnp.square(o - mu), axis=-1, keepdims=True)
    y = (o - mu) * lax.rsqrt(var + GN_EPS)
    return y.reshape(B, S, H * d) * g


def multiscale_pool(p, w_group, scale):
    B, S, _ = p.shape
    pf = p.astype(jnp.float32).reshape(B, S, POOL_GROUPS, POOL_GROUP_DIM)
    csum = lax.cumsum(pf, axis=1)
    t = jnp.arange(S)
    outs = []
    for gi, w in enumerate(POOL_WINDOWS):
        c = csum[:, :, gi]
        lagged = jnp.pad(c, ((0, 0), (w, 0), (0, 0)))[:, :S]
        count = jnp.minimum(t + 1, w).astype(jnp.float32)[None, :, None]
        outs.append((c - lagged) / count - pf[:, :, gi])
    pooled = jnp.stack(outs, axis=2)
    mixed = jnp.einsum('bsgc,gce->bsge', pooled, w_group.astype(jnp.float32))
    return (mixed.reshape(B, S, POOL_WIDTH) * scale.astype(jnp.float32)).astype(p.dtype)


def retention_pool_mixer(h, w_in, gn_gain, w_pool, pool_scale, w_out):
    B, S, _ = h.shape
    z = h @ w_in
    q, k, v, g, p = jnp.split(
        z, [RET_WIDTH, 2 * RET_WIDTH, 3 * RET_WIDTH, 4 * RET_WIDTH], axis=-1)

    def heads(t):
        return t.astype(jnp.float32).reshape(B, S, RET_HEADS, RET_HEAD_DIM)

    qh = rotary(heads(q))
    kh = rotary(heads(k)) * (RET_HEAD_DIM ** -0.5)
    o = retention(qh, kh, heads(v))
    o = head_group_norm(o, gn_gain.astype(jnp.float32))
    ret_out = (jax.nn.silu(g.astype(jnp.float32)) * o).astype(h.dtype)
    pool_out = multiscale_pool(p, w_pool, pool_scale)
    return jnp.concatenate([ret_out, pool_out], axis=-1) @ w_out


def chunk_attention(h, w_qkv, rel_bias, w_out):
    B, S, _ = h.shape
    nc = S // CHUNK
    q, k, v = jnp.split(h @ w_qkv, 3, axis=-1)
    q = q.reshape(B, S, ATT_HEADS, ATT_HEAD_DIM) * (ATT_HEAD_DIM ** -0.5)
    pad = ((0, 0), (LEFT_CHUNKS * CHUNK, 0), (0, 0), (0, 0))
    k = jnp.pad(k.reshape(B, S, ATT_HEADS, ATT_HEAD_DIM), pad)
    v = jnp.pad(v.reshape(B, S, ATT_HEADS, ATT_HEAD_DIM), pad)
    n_idx = jnp.arange(CHUNK)[:, None]
    j_idx = jnp.arange(BAND)[None, :]
    rel = jnp.clip(n_idx + LEFT_CHUNKS * CHUNK - j_idx, -REL_CLIP, REL_CLIP) + REL_CLIP
    bias = rel_bias.astype(jnp.float32)[:, rel]
    band_pos = jnp.arange(BAND)

    def one_chunk(c):
        q_c = lax.dynamic_slice_in_dim(q, c * CHUNK, CHUNK, axis=1)
        k_b = lax.dynamic_slice_in_dim(k, c * CHUNK, BAND, axis=1)
        v_b = lax.dynamic_slice_in_dim(v, c * CHUNK, BAND, axis=1)
        s = jnp.einsum('bnhd,bjhd->bhnj', q_c, k_b).astype(jnp.float32) + bias
        valid = band_pos >= (LEFT_CHUNKS - c) * CHUNK
        s = jnp.where(valid, s, NEG_INF)
        pr = jax.nn.softmax(s, axis=-1).astype(v_b.dtype)
        return jnp.einsum('bhnj,bjhd->bnhd', pr, v_b)

    o = lax.map(one_chunk, jnp.arange(nc))
    o = jnp.moveaxis(o, 0, 1).reshape(B, S, D_MODEL)
    return o @ w_out


def squared_relu_mlp(h, w1, w2):
    return jnp.square(jax.nn.relu(h @ w1)) @ w2


def _fwd_setup_inputs(seed: int = 0) -> dict:
    key = jax.random.key(seed)
    ks = jax.random.split(key, 16)
    f32 = jnp.float32

    def nrm(k, shape, scale):
        return jax.random.normal(k, shape, f32) * scale

    return {
        "x": nrm(ks[0], (BATCH, SEQ, D_MODEL), 1.0),
        "mix_norm": 1.0 + nrm(ks[1], (DEPTH, D_MODEL), 0.05),
        "ffn_norm": 1.0 + nrm(ks[2], (DEPTH, D_MODEL), 0.05),
        "w_ffn_in": nrm(ks[3], (DEPTH, D_MODEL, D_FF), D_MODEL ** -0.5),
        "w_ffn_out": nrm(ks[4], (DEPTH, D_FF, D_MODEL), D_FF ** -0.5),
        "ab_w_in": nrm(ks[5], (N_EVEN, D_MODEL, AB_IN_WIDTH), D_MODEL ** -0.5),
        "ab_gn_gain": 1.0 + nrm(ks[6], (N_EVEN, RET_WIDTH), 0.05),
        "ab_w_pool": nrm(ks[7], (N_EVEN, POOL_GROUPS, POOL_GROUP_DIM, POOL_GROUP_DIM), POOL_GROUP_DIM ** -0.5),
        "ab_pool_scale": 1.0 + nrm(ks[8], (N_EVEN, POOL_WIDTH), 0.1),
        "ab_w_out": nrm(ks[9], (N_EVEN, AB_OUT_WIDTH, D_MODEL), AB_OUT_WIDTH ** -0.5),
        "c_w_qkv": nrm(ks[10], (N_ODD, D_MODEL, 3 * D_MODEL), D_MODEL ** -0.5),
        "c_rel_bias": nrm(ks[11], (N_ODD, ATT_HEADS, N_REL), 0.5),
        "c_w_out": nrm(ks[12], (N_ODD, D_MODEL, D_MODEL), D_MODEL ** -0.5),
        "final_norm": 1.0 + nrm(ks[13], (D_MODEL,), 0.05),
    }


def _fwd_reference(x, mix_norm, ffn_norm, w_ffn_in, w_ffn_out, ab_w_in, ab_gn_gain,
              ab_w_pool, ab_pool_scale, ab_w_out, c_w_qkv, c_rel_bias, c_w_out,
              final_norm):
    for layer in range(DEPTH):
        h = rms_norm(x, mix_norm[layer])
        i = layer // 2
        if layer % 2 == 0:
            x = x + retention_pool_mixer(h, ab_w_in[i], ab_gn_gain[i], ab_w_pool[i],
                                         ab_pool_scale[i], ab_w_out[i])
        else:
            x = x + chunk_attention(h, c_w_qkv[i], c_rel_bias[i], c_w_out[i])
        x = x + squared_relu_mlp(rms_norm(x, ffn_norm[layer]), w_ffn_in[layer], w_ffn_out[layer])
    return rms_norm(x, final_norm)


import jax as _jax
import jax.numpy as _jnp

TWIN_FORMAT = 'train_step'
FWD_PARAMS = ['x', 'mix_norm', 'ffn_norm', 'w_ffn_in', 'w_ffn_out', 'ab_w_in', 'ab_gn_gain', 'ab_w_pool', 'ab_pool_scale', 'ab_w_out', 'c_w_qkv', 'c_rel_bias', 'c_w_out', 'final_norm']
TWIN_WEIGHTS = ['mix_norm', 'ffn_norm', 'w_ffn_in', 'w_ffn_out', 'ab_w_in', 'ab_gn_gain', 'ab_w_pool', 'ab_pool_scale', 'ab_w_out', 'c_w_qkv', 'c_rel_bias', 'c_w_out', 'final_norm']
TWIN_DIFF_INPUT = 'x'
TWIN_INPUTS = ['x', 'mix_norm', 'ffn_norm', 'w_ffn_in', 'w_ffn_out', 'ab_w_in', 'ab_gn_gain', 'ab_w_pool', 'ab_pool_scale', 'ab_w_out', 'c_w_qkv', 'c_rel_bias', 'c_w_out', 'final_norm', 'loss_target', 'm_mix_norm', 'm_ffn_norm', 'm_w_ffn_in', 'm_w_ffn_out', 'm_ab_w_in', 'm_ab_gn_gain', 'm_ab_w_pool', 'm_ab_pool_scale', 'm_ab_w_out', 'm_c_w_qkv', 'm_c_rel_bias', 'm_c_w_out', 'm_final_norm', 'v_mix_norm', 'v_ffn_norm', 'v_w_ffn_in', 'v_w_ffn_out', 'v_ab_w_in', 'v_ab_gn_gain', 'v_ab_w_pool', 'v_ab_pool_scale', 'v_ab_w_out', 'v_c_w_qkv', 'v_c_rel_bias', 'v_c_w_out', 'v_final_norm']
TWIN_OUTPUTS = ['loss', 'grad_x', 'grad_mix_norm', 'grad_ffn_norm', 'grad_w_ffn_in', 'grad_w_ffn_out', 'grad_ab_w_in', 'grad_ab_gn_gain', 'grad_ab_w_pool', 'grad_ab_pool_scale', 'grad_ab_w_out', 'grad_c_w_qkv', 'grad_c_rel_bias', 'grad_c_w_out', 'grad_final_norm', 'delta_mix_norm', 'delta_ffn_norm', 'delta_w_ffn_in', 'delta_w_ffn_out', 'delta_ab_w_in', 'delta_ab_gn_gain', 'delta_ab_w_pool', 'delta_ab_pool_scale', 'delta_ab_w_out', 'delta_c_w_qkv', 'delta_c_rel_bias', 'delta_c_w_out', 'delta_final_norm', 'new_m_mix_norm', 'new_m_ffn_norm', 'new_m_w_ffn_in', 'new_m_w_ffn_out', 'new_m_ab_w_in', 'new_m_ab_gn_gain', 'new_m_ab_w_pool', 'new_m_ab_pool_scale', 'new_m_ab_w_out', 'new_m_c_w_qkv', 'new_m_c_rel_bias', 'new_m_c_w_out', 'new_m_final_norm', 'new_v_mix_norm', 'new_v_ffn_norm', 'new_v_w_ffn_in', 'new_v_w_ffn_out', 'new_v_ab_w_in', 'new_v_ab_gn_gain', 'new_v_ab_w_pool', 'new_v_ab_pool_scale', 'new_v_ab_w_out', 'new_v_c_w_qkv', 'new_v_c_rel_bias', 'new_v_c_w_out', 'new_v_final_norm']
TWIN_LEAF_KINDS = {'loss': 'loss', 'grad_x': 'grad_x', 'grad_mix_norm': 'grad_w', 'grad_ffn_norm': 'grad_w', 'grad_w_ffn_in': 'grad_w', 'grad_w_ffn_out': 'grad_w', 'grad_ab_w_in': 'grad_w', 'grad_ab_gn_gain': 'grad_w', 'grad_ab_w_pool': 'grad_w', 'grad_ab_pool_scale': 'grad_w', 'grad_ab_w_out': 'grad_w', 'grad_c_w_qkv': 'grad_w', 'grad_c_rel_bias': 'grad_w', 'grad_c_w_out': 'grad_w', 'grad_final_norm': 'grad_w', 'delta_mix_norm': 'delta_w', 'delta_ffn_norm': 'delta_w', 'delta_w_ffn_in': 'delta_w', 'delta_w_ffn_out': 'delta_w', 'delta_ab_w_in': 'delta_w', 'delta_ab_gn_gain': 'delta_w', 'delta_ab_w_pool': 'delta_w', 'delta_ab_pool_scale': 'delta_w', 'delta_ab_w_out': 'delta_w', 'delta_c_w_qkv': 'delta_w', 'delta_c_rel_bias': 'delta_w', 'delta_c_w_out': 'delta_w', 'delta_final_norm': 'delta_w', 'new_m_mix_norm': 'new_m', 'new_m_ffn_norm': 'new_m', 'new_m_w_ffn_in': 'new_m', 'new_m_w_ffn_out': 'new_m', 'new_m_ab_w_in': 'new_m', 'new_m_ab_gn_gain': 'new_m', 'new_m_ab_w_pool': 'new_m', 'new_m_ab_pool_scale': 'new_m', 'new_m_ab_w_out': 'new_m', 'new_m_c_w_qkv': 'new_m', 'new_m_c_rel_bias': 'new_m', 'new_m_c_w_out': 'new_m', 'new_m_final_norm': 'new_m', 'new_v_mix_norm': 'new_v', 'new_v_ffn_norm': 'new_v', 'new_v_w_ffn_in': 'new_v', 'new_v_w_ffn_out': 'new_v', 'new_v_ab_w_in': 'new_v', 'new_v_ab_gn_gain': 'new_v', 'new_v_ab_w_pool': 'new_v', 'new_v_ab_pool_scale': 'new_v', 'new_v_ab_w_out': 'new_v', 'new_v_c_w_qkv': 'new_v', 'new_v_c_rel_bias': 'new_v', 'new_v_c_w_out': 'new_v', 'new_v_final_norm': 'new_v'}


def _forward(args):
    return _fwd_reference(*[args[k] for k in FWD_PARAMS])


def _output_shape():
    out = _jax.eval_shape(lambda: _forward(_fwd_setup_inputs(0)))
    return out.shape, out.dtype

N_MICROBATCH = 1
ADAM_LR = 0.001
ADAM_B1 = 0.9
ADAM_B2 = 0.999
ADAM_EPS = 1e-08
ADAM_WD = 0.01
ADAM_STEP = 10
PER_EXAMPLE_BATCH_AXIS = {'x': 0, 'loss_target': 0}
SHARED_INPUTS = []
_WEIGHT_DTYPES = {'mix_norm': _jnp.float32, 'ffn_norm': _jnp.float32, 'w_ffn_in': _jnp.float32, 'w_ffn_out': _jnp.float32, 'ab_w_in': _jnp.float32, 'ab_gn_gain': _jnp.float32, 'ab_w_pool': _jnp.float32, 'ab_pool_scale': _jnp.float32, 'ab_w_out': _jnp.float32, 'c_w_qkv': _jnp.float32, 'c_rel_bias': _jnp.float32, 'c_w_out': _jnp.float32, 'final_norm': _jnp.float32}
MOMENT_SCALE = {'mix_norm': 2.112126e-01, 'ffn_norm': 2.243311e-01, 'w_ffn_in': 1.087019e-01, 'w_ffn_out': 3.664618e-01, 'ab_w_in': 1.353198e-01, 'ab_gn_gain': 1.259752e-01, 'ab_w_pool': 1.748354e-01, 'ab_pool_scale': 1.685966e-01, 'ab_w_out': 1.495308e-01, 'c_w_qkv': 1.299564e-01, 'c_rel_bias': 9.384993e-03, 'c_w_out': 2.314463e-01, 'final_norm': 6.585736e+01}


def _to_microbatches(a, axis):
    t = _jnp.moveaxis(a, axis, 0)
    t = t.reshape((N_MICROBATCH, t.shape[0] // N_MICROBATCH) + t.shape[1:])
    return _jnp.moveaxis(t, 1, axis + 1)


def setup_inputs(seed: int = 0) -> dict:
    inp = _fwd_setup_inputs(seed)
    key = _jax.random.fold_in(_jax.random.key(seed), 7919)
    shape, _ = _output_shape()
    out = dict(inp)
    out["loss_target"] = _jax.random.normal(_jax.random.fold_in(key, 0), shape, _jnp.float32)
    for i, name in enumerate(TWIN_WEIGHTS):
        w = inp[name].astype(_jnp.float32)
        if MOMENT_SCALE is None:
            s = _jnp.sqrt(_jnp.mean(_jnp.square(w)) + 1e-30)
        else:
            s = MOMENT_SCALE[name]
        km, kv = _jax.random.split(_jax.random.fold_in(key, i + 1))
        out[name] = w
        out["m_" + name] = s * _jax.random.normal(km, w.shape, _jnp.float32)
        out["v_" + name] = (s * s) * _jax.random.uniform(kv, w.shape, _jnp.float32, 0.5, 1.5)
    if N_MICROBATCH > 1:
        for name, axis in PER_EXAMPLE_BATCH_AXIS.items():
            out[name] = _to_microbatches(out[name], axis)
    return {'x': out['x'], 'mix_norm': out['mix_norm'], 'ffn_norm': out['ffn_norm'], 'w_ffn_in': out['w_ffn_in'], 'w_ffn_out': out['w_ffn_out'], 'ab_w_in': out['ab_w_in'], 'ab_gn_gain': out['ab_gn_gain'], 'ab_w_pool': out['ab_w_pool'], 'ab_pool_scale': out['ab_pool_scale'], 'ab_w_out': out['ab_w_out'], 'c_w_qkv': out['c_w_qkv'], 'c_rel_bias': out['c_rel_bias'], 'c_w_out': out['c_w_out'], 'final_norm': out['final_norm'], 'loss_target': out['loss_target'], 'm_mix_norm': out['m_mix_norm'], 'm_ffn_norm': out['m_ffn_norm'], 'm_w_ffn_in': out['m_w_ffn_in'], 'm_w_ffn_out': out['m_w_ffn_out'], 'm_ab_w_in': out['m_ab_w_in'], 'm_ab_gn_gain': out['m_ab_gn_gain'], 'm_ab_w_pool': out['m_ab_w_pool'], 'm_ab_pool_scale': out['m_ab_pool_scale'], 'm_ab_w_out': out['m_ab_w_out'], 'm_c_w_qkv': out['m_c_w_qkv'], 'm_c_rel_bias': out['m_c_rel_bias'], 'm_c_w_out': out['m_c_w_out'], 'm_final_norm': out['m_final_norm'], 'v_mix_norm': out['v_mix_norm'], 'v_ffn_norm': out['v_ffn_norm'], 'v_w_ffn_in': out['v_w_ffn_in'], 'v_w_ffn_out': out['v_w_ffn_out'], 'v_ab_w_in': out['v_ab_w_in'], 'v_ab_gn_gain': out['v_ab_gn_gain'], 'v_ab_w_pool': out['v_ab_w_pool'], 'v_ab_pool_scale': out['v_ab_pool_scale'], 'v_ab_w_out': out['v_ab_w_out'], 'v_c_w_qkv': out['v_c_w_qkv'], 'v_c_rel_bias': out['v_c_rel_bias'], 'v_c_w_out': out['v_c_w_out'], 'v_final_norm': out['v_final_norm']}


def _loss(weights, diff, rest, loss_target):
    with _jax.named_scope("forward"):
        args = {**rest, TWIN_DIFF_INPUT: diff, **{k: w.astype(_WEIGHT_DTYPES[k]) for k, w in weights.items()}}
        y = _forward(args)
    with _jax.named_scope("loss_head"):
        err = _jnp.square(y.astype(_jnp.float32) - loss_target)
        return 0.5 * _jnp.sum(_jnp.mean(err, axis=-1)) if err.ndim else 0.5 * err


def _adamw(w, g, m, v):
    m = ADAM_B1 * m + (1.0 - ADAM_B1) * g
    v = ADAM_B2 * v + (1.0 - ADAM_B2) * _jnp.square(g)
    m_hat = m / (1.0 - ADAM_B1 ** ADAM_STEP)
    v_hat = v / (1.0 - ADAM_B2 ** ADAM_STEP)
    delta = -ADAM_LR * (m_hat / (_jnp.sqrt(v_hat) + ADAM_EPS) + ADAM_WD * w)
    return delta, m, v


def reference(x, mix_norm, ffn_norm, w_ffn_in, w_ffn_out, ab_w_in, ab_gn_gain, ab_w_pool, ab_pool_scale, ab_w_out, c_w_qkv, c_rel_bias, c_w_out, final_norm, loss_target, m_mix_norm, m_ffn_norm, m_w_ffn_in, m_w_ffn_out, m_ab_w_in, m_ab_gn_gain, m_ab_w_pool, m_ab_pool_scale, m_ab_w_out, m_c_w_qkv, m_c_rel_bias, m_c_w_out, m_final_norm, v_mix_norm, v_ffn_norm, v_w_ffn_in, v_w_ffn_out, v_ab_w_in, v_ab_gn_gain, v_ab_w_pool, v_ab_pool_scale, v_ab_w_out, v_c_w_qkv, v_c_rel_bias, v_c_w_out, v_final_norm):
    given = dict(x=x, mix_norm=mix_norm, ffn_norm=ffn_norm, w_ffn_in=w_ffn_in, w_ffn_out=w_ffn_out, ab_w_in=ab_w_in, ab_gn_gain=ab_gn_gain, ab_w_pool=ab_w_pool, ab_pool_scale=ab_pool_scale, ab_w_out=ab_w_out, c_w_qkv=c_w_qkv, c_rel_bias=c_rel_bias, c_w_out=c_w_out, final_norm=final_norm, loss_target=loss_target, m_mix_norm=m_mix_norm, m_ffn_norm=m_ffn_norm, m_w_ffn_in=m_w_ffn_in, m_w_ffn_out=m_w_ffn_out, m_ab_w_in=m_ab_w_in, m_ab_gn_gain=m_ab_gn_gain, m_ab_w_pool=m_ab_w_pool, m_ab_pool_scale=m_ab_pool_scale, m_ab_w_out=m_ab_w_out, m_c_w_qkv=m_c_w_qkv, m_c_rel_bias=m_c_rel_bias, m_c_w_out=m_c_w_out, m_final_norm=m_final_norm, v_mix_norm=v_mix_norm, v_ffn_norm=v_ffn_norm, v_w_ffn_in=v_w_ffn_in, v_w_ffn_out=v_w_ffn_out, v_ab_w_in=v_ab_w_in, v_ab_gn_gain=v_ab_gn_gain, v_ab_w_pool=v_ab_w_pool, v_ab_pool_scale=v_ab_pool_scale, v_ab_w_out=v_ab_w_out, v_c_w_qkv=v_c_w_qkv, v_c_rel_bias=v_c_rel_bias, v_c_w_out=v_c_w_out, v_final_norm=v_final_norm)
    weights = {n: given[n] for n in TWIN_WEIGHTS}
    shared = {n: given[n] for n in SHARED_INPUTS}
    per_example = {n: given[n] for n in ['x']}
    grad_fn = _jax.value_and_grad(_loss, argnums=(0, 1))

    def one_microbatch(ex, loss_target):
        ex = dict(ex)
        diff = ex.pop(TWIN_DIFF_INPUT)
        return grad_fn(weights, diff, {**shared, **ex}, loss_target)

    if N_MICROBATCH == 1:
        loss, (grad_w, grad_x) = one_microbatch(per_example, given["loss_target"])
    else:
        def body(carry, xs):
            loss_sum, grad_sum = carry
            l_k, (gw_k, gx_k) = one_microbatch(xs[0], xs[1])
            with _jax.named_scope("update"):
                return (loss_sum + l_k, _jax.tree.map(_jnp.add, grad_sum, gw_k)), gx_k

        init = (_jnp.zeros((), _jnp.float32), _jax.tree.map(_jnp.zeros_like, weights))
        (loss, grad_w), grad_x = _jax.lax.scan(body, init, (per_example, given["loss_target"]))
    with _jax.named_scope("update"):
        delta_w, new_m, new_v = {}, {}, {}
        for n in TWIN_WEIGHTS:
            delta_w[n], new_m[n], new_v[n] = _adamw(weights[n], grad_w[n], given["m_" + n], given["v_" + n])
    return (loss, grad_x, *[grad_w[n] for n in TWIN_WEIGHTS], *[delta_w[n] for n in TWIN_WEIGHTS],
            *[new_m[n] for n in TWIN_WEIGHTS], *[new_v[n] for n in TWIN_WEIGHTS])
```

```python
import functools
import math

import numpy as np
import jax
import jax.numpy as jnp
from jax import lax
from jax.experimental import pallas as pl
from jax.experimental.pallas import tpu as pltpu

F32 = jnp.float32
BF16 = jnp.bfloat16

D_MODEL = 1024
D_FF = 4096
DEPTH = 4
CHUNK = 64
RMS_EPS = 1e-6
RET_WIDTH = 512
RET_HEADS = 4
RET_HEAD_DIM = 128
RET_ROPE_BASE = 10000.0
GN_EPS = 1e-5
POOL_WIDTH = 512
POOL_WINDOWS = (2, 4, 8, 16)
POOL_GROUP_DIM = 128
POOL_HALO = 16
AB_IN_WIDTH = 2560
ATT_HEADS = 16
ATT_HEAD_DIM = 64
LEFT_CHUNKS = 8
BAND = (LEFT_CHUNKS + 1) * CHUNK
REL_CLIP = 128
N_REL = 2 * REL_CLIP + 1
N_REL_PAD = 264
NEG_INF = -1e30
KSCALE = RET_HEAD_DIM ** -0.5
QSCALE = ATT_HEAD_DIM ** -0.5

ADAM_LR = 0.001
ADAM_B1 = 0.9
ADAM_B2 = 0.999
ADAM_EPS = 1e-08
ADAM_WD = 0.01
ADAM_STEP = 10

ATT_BLOCK = LEFT_CHUNKS * CHUNK
RET_BLOCK = 512
N_CHIPS = 4
N_DEV = 8
LANE = 128
VMEM_LIMIT = 52 * 1024 * 1024
MESH = pl.DeviceIdType.MESH


def _cp(sem, vmem=VMEM_LIMIT):
    return pltpu.CompilerParams(dimension_semantics=sem, vmem_limit_bytes=vmem)


def _dot(a, b):
    return lax.dot_general(a, b, (((1,), (0,)), ((), ())), preferred_element_type=F32)


def _dot_nt(a, b):
    return lax.dot_general(a, b, (((1,), (1,)), ((), ())), preferred_element_type=F32)


def _dot_tn(a, b):
    return lax.dot_general(a, b, (((0,), (0,)), ((), ())), preferred_element_type=F32)


def _mm(name, mode, a, b, *, la=None, lb=None, tm=1024, tn=1024, tk=1024, a_fn=None, b_fn=None,
        extras=(), epi=None, out_dtype=F32):
    a2, b2 = a.shape[-2:], b.shape[-2:]
    if mode == "nn":
        (M, K), (K2, N) = a2, b2
    elif mode == "nt":
        (M, K), (N, K2) = a2, b2
    else:
        (K, M), (K2, N) = a2, b2
    assert K == K2, (name, a.shape, b.shape)
    tm, tn, tk = min(tm, M), min(tn, N), min(tk, K)
    assert M % tm == 0 and N % tn == 0 and K % tk == 0, (name, M, N, K, tm, tn, tk)
    gm, gn, gk = M // tm, N // tn, K // tk

    def spec(block, idx, lead):
        if lead is None:
            return pl.BlockSpec(block, idx)
        return pl.BlockSpec((None,) + block, lambda i, j, k: (lead,) + idx(i, j, k))

    if mode == "nn":
        a_spec = spec((tm, tk), lambda i, j, k: (i, k), la)
        b_spec = spec((tk, tn), lambda i, j, k: (k, j), lb)
        dot = _dot
    elif mode == "nt":
        a_spec = spec((tm, tk), lambda i, j, k: (i, k), la)
        b_spec = spec((tn, tk), lambda i, j, k: (j, k), lb)
        dot = _dot_nt
    else:
        a_spec = spec((tk, tm), lambda i, j, k: (k, i), la)
        b_spec = spec((tk, tn), lambda i, j, k: (k, j), lb)
        dot = _dot_tn
    ex_specs = [pl.BlockSpec((tm, tn), lambda i, j, k: (i, j)) for _ in extras]
    n_ex = len(extras)

    def body(*refs):
        a_ref, b_ref = refs[0], refs[1]
        ex_refs = refs[2:2 + n_ex]
        o_ref = refs[2 + n_ex]
        av = a_ref[...]
        if a_fn is not None:
            av = a_fn(av)
        bv = b_ref[...]
        if b_fn is not None:
            bv = b_fn(bv)
        part = dot(av.astype(BF16), bv.astype(BF16))

        def finish(acc):
            if epi is not None:
                acc = epi(acc, *[r[...] for r in ex_refs])
            o_ref[...] = acc.astype(o_ref.dtype)

        if gk == 1:
            finish(part)
        else:
            acc_ref = refs[3 + n_ex]
            k = pl.program_id(2)

            @pl.when(k == 0)
            def _():
                acc_ref[...] = part

            @pl.when(k > 0)
            def _():
                acc_ref[...] += part

            @pl.when(k == gk - 1)
            def _():
                finish(acc_ref[...])

    return pl.pallas_call(
        body,
        name=name,
        grid=(gm, gn, gk),
        in_specs=[a_spec, b_spec] + ex_specs,
        out_specs=pl.BlockSpec((tm, tn), lambda i, j, k: (i, j)),
        out_shape=jax.ShapeDtypeStruct((M, N), out_dtype),
        scratch_shapes=[pltpu.VMEM((tm, tn), F32)] if gk > 1 else [],
        compiler_params=_cp(("parallel", "parallel", "arbitrary")),
    )(a, b, *extras)


def _relu2(u):
    r = jnp.maximum(u, 0)
    return r * r


def _rms_fwd(name, x, g):
    S, D = x.shape
    tq = min(1024, S)

    def body(x_ref, g_ref, o_ref):
        xv = x_ref[...]
        r = lax.rsqrt(jnp.mean(xv * xv, axis=-1, keepdims=True) + RMS_EPS)
        o_ref[...] = ((xv * r) * g_ref[...]).astype(o_ref.dtype)

    return pl.pallas_call(
        body,
        name=name,
        grid=(S // tq,),
        in_specs=[pl.BlockSpec((tq, D), lambda i: (i, 0)), pl.BlockSpec((1, D), lambda i: (0, 0))],
        out_specs=pl.BlockSpec((tq, D), lambda i: (i, 0)),
        out_shape=jax.ShapeDtypeStruct((S, D), BF16),
        compiler_params=_cp(("parallel",)),
    )(x, g)


def _rms_bwd(name, x, g, dh, dres):
    S, D = x.shape
    tq = min(512, S)
    n = S // tq

    def body(x_ref, g_ref, dh_ref, dres_ref, dx_ref, dg_ref, acc_ref):
        i = pl.program_id(0)
        xv = x_ref[...]
        r = lax.rsqrt(jnp.mean(xv * xv, axis=-1, keepdims=True) + RMS_EPS)
        xh = xv * r
        dh_v = dh_ref[...].astype(F32)
        dxh = dh_v * g_ref[...]
        dx = r * (dxh - xh * jnp.mean(dxh * xh, axis=-1, keepdims=True))
        dx_ref[...] = dres_ref[...] + dx
        part = jnp.sum((dh_v * xh).reshape(tq // 8, 8, D), axis=0)

        @pl.when(i == 0)
        def _():
            acc_ref[...] = part

        @pl.when(i > 0)
        def _():
            acc_ref[...] += part

        @pl.when(i == n - 1)
        def _():
            dg_ref[...] = jnp.sum(acc_ref[...], axis=0, keepdims=True)

    return pl.pallas_call(
        body,
        name=name,
        grid=(n,),
        in_specs=[pl.BlockSpec((tq, D), lambda i: (i, 0)), pl.BlockSpec((1, D), lambda i: (0, 0)),
                  pl.BlockSpec((tq, D), lambda i: (i, 0)), pl.BlockSpec((tq, D), lambda i: (i, 0))],
        out_specs=[pl.BlockSpec((tq, D), lambda i: (i, 0)), pl.BlockSpec((1, D), lambda i: (0, 0))],
        out_shape=[jax.ShapeDtypeStruct((S, D), F32), jax.ShapeDtypeStruct((1, D), F32)],
        scratch_shapes=[pltpu.VMEM((8, D), F32)],
        compiler_params=_cp(("arbitrary",)),
    )(x, g, dh, dres)


def _loss_head(x, g, t):
    S, D = x.shape
    tq = min(512, S)
    n = S // tq

    def body(x_ref, g_ref, t_ref, loss_ref, dx_ref, dg_ref, lacc_ref, gacc_ref):
        i = pl.program_id(0)
        xv = x_ref[...]
        gv = g_ref[...]
        r = lax.rsqrt(jnp.mean(xv * xv, axis=-1, keepdims=True) + RMS_EPS)
        xh = xv * r
        e = xh * gv - t_ref[...]
        dy = e * (1.0 / D)
        dxh = dy * gv
        dx_ref[...] = r * (dxh - xh * jnp.mean(dxh * xh, axis=-1, keepdims=True))
        lpart = jnp.sum((e * e).reshape(tq // 8, 8, D), axis=0)
        gpart = jnp.sum((dy * xh).reshape(tq // 8, 8, D), axis=0)

        @pl.when(i == 0)
        def _():
            lacc_ref[...] = lpart
            gacc_ref[...] = gpart

        @pl.when(i > 0)
        def _():
            lacc_ref[...] += lpart
            gacc_ref[...] += gpart

        @pl.when(i == n - 1)
        def _():
            dg_ref[...] = jnp.sum(gacc_ref[...], axis=0, keepdims=True)
            tot = jnp.sum(jnp.sum(lacc_ref[...], axis=0, keepdims=True), axis=1, keepdims=True)
            loss_ref[...] = jnp.broadcast_to(tot * (0.5 / D), (1, LANE))

    return pl.pallas_call(
        body,
        name="loss_head",
        grid=(n,),
        in_specs=[pl.BlockSpec((tq, D), lambda i: (i, 0)), pl.BlockSpec((1, D), lambda i: (0, 0)),
                  pl.BlockSpec((tq, D), lambda i: (i, 0))],
        out_specs=[pl.BlockSpec((1, LANE), lambda i: (0, 0)), pl.BlockSpec((tq, D), lambda i: (i, 0)),
                   pl.BlockSpec((1, D), lambda i: (0, 0))],
        out_shape=[jax.ShapeDtypeStruct((1, LANE), F32), jax.ShapeDtypeStruct((S, D), F32),
                   jax.ShapeDtypeStruct((1, D), F32)],
        scratch_shapes=[pltpu.VMEM((8, D), F32), pltpu.VMEM((8, D), F32)],
        compiler_params=_cp(("arbitrary",)),
    )(x, g, t)


def _ret_tables(S):
    T = min(RET_BLOCK, S)
    inv_freq = 1.0 / (RET_ROPE_BASE ** jnp.linspace(0.0, 1.0, RET_HEAD_DIM // 2, dtype=F32))
    ang = jnp.arange(S, dtype=F32)[:, None] * inv_freq[None, :]
    cos, sin = jnp.cos(ang), jnp.sin(ang)
    cosf = jnp.repeat(cos, 2, axis=-1)
    sins = jnp.stack([-sin, sin], axis=-1).reshape(S, RET_HEAD_DIM)
    log_g = np.log1p(-np.power(2.0, -5.0 - np.arange(RET_HEADS, dtype=np.float64)))
    pos = np.arange(T, dtype=np.float64)
    diff = pos[:, None] - pos[None, :]
    same = (pos[:, None] // CHUNK) == (pos[None, :] // CHUNK)
    seen = same | (diff > 0)
    dmat = np.where(seen[None], np.exp(np.abs(diff)[None] * log_g[:, None, None]), 0.0)
    aq = np.exp((pos[None, :] + 1.0) * log_g[:, None])
    ak = np.exp((T - 1.0 - pos[None, :]) * log_g[:, None])
    lam = np.exp(T * log_g)
    bc = lambda v: jnp.asarray(np.broadcast_to(v[..., None], v.shape + (LANE,)), F32)
    return dict(cos=cosf, sin=sins, dmat=jnp.asarray(dmat, F32), aq=bc(aq), ak=bc(ak),
                lam=jnp.asarray(np.broadcast_to(lam[:, None, None], (RET_HEADS, 1, LANE)), F32))


def _rot(x, cos, sin_s, even):
    sw = jnp.where(even, pltpu.roll(x, LANE - 1, 1), pltpu.roll(x, 1, 1))
    return x * cos + sw * sin_s


def _rot_t(dy, cos, sin_s, even):
    t = dy * sin_s
    return dy * cos + jnp.where(even, pltpu.roll(t, LANE - 1, 1), pltpu.roll(t, 1, 1))


def _ret_specs(T, rev_nb=None):
    blk = (lambda b: b) if rev_nb is None else (lambda b: rev_nb - 1 - b)
    hd = RET_HEADS
    z_specs = [pl.BlockSpec((T, LANE), functools.partial(lambda h, b, o: (blk(b), o * hd + h), o=o))
               for o in range(4)]
    tab = [pl.BlockSpec((T, LANE), lambda h, b: (blk(b), 0)),
           pl.BlockSpec((T, LANE), lambda h, b: (blk(b), 0)),
           pl.BlockSpec((None, T, T), lambda h, b: (h, 0, 0)),
           pl.BlockSpec((None, T, LANE), lambda h, b: (h, 0, 0)),
           pl.BlockSpec((None, T, LANE), lambda h, b: (h, 0, 0)),
           pl.BlockSpec((None, 1, LANE), lambda h, b: (h, 0, 0)),
           pl.BlockSpec((1, LANE), lambda h, b: (0, h))]
    return z_specs + tab, blk


def _ret_fwd(name, z, tb, gain):
    S = z.shape[0]
    T = min(RET_BLOCK, S)
    nb = S // T
    specs, blk = _ret_specs(T)

    def body(zq, zk, zv, zg, cos_r, sin_r, d_r, aq_r, ak_r, lam_r, gain_r, ret_o, opre_o, st_o, state):
        b = pl.program_id(1)

        @pl.when(b == 0)
        def _():
            state[...] = jnp.zeros_like(state)

        even = (lax.broadcasted_iota(jnp.int32, (T, LANE), 1) & 1) == 0
        c, s = cos_r[...], sin_r[...]
        q = _rot(zq[...], c, s, even)
        k = _rot(zk[...], c, s, even) * KSCALE
        qb, kb, vb = q.astype(BF16), k.astype(BF16), zv[...].astype(BF16)
        p = (_dot_nt(qb, kb) * d_r[...]).astype(BF16)
        st = state[...]
        st_o[...] = st
        o = _dot(p, vb) + _dot((q * aq_r[...]).astype(BF16), st.astype(BF16))
        state[...] = st * lam_r[...] + _dot_tn((k * ak_r[...]).astype(BF16), vb)
        opre_o[...] = o
        mu = jnp.mean(o, axis=-1, keepdims=True)
        d = o - mu
        y = d * lax.rsqrt(jnp.mean(d * d, axis=-1, keepdims=True) + GN_EPS)
        g = zg[...]
        ret_o[...] = ((g * jax.nn.sigmoid(g)) * (y * gain_r[...])).astype(ret_o.dtype)

    out_blk = pl.BlockSpec((T, LANE), lambda h, b: (b, h))
    return pl.pallas_call(
        body,
        name=name,
        grid=(RET_HEADS, nb),
        in_specs=specs,
        out_specs=[out_blk, out_blk, pl.BlockSpec((None, None, LANE, LANE), lambda h, b: (h, b, 0, 0))],
        out_shape=[jax.ShapeDtypeStruct((S, RET_WIDTH), BF16), jax.ShapeDtypeStruct((S, RET_WIDTH), F32),
                   jax.ShapeDtypeStruct((RET_HEADS, nb, LANE, LANE), F32)],
        scratch_shapes=[pltpu.VMEM((LANE, LANE), F32)],
        compiler_params=_cp(("parallel", "arbitrary")),
    )(z, z, z, z, tb["cos"], tb["sin"], tb["dmat"], tb["aq"], tb["ak"], tb["lam"], gain)


def _ret_bwd(name, z, tb, gain, opre, states, dcat):
    S = z.shape[0]
    T = min(RET_BLOCK, S)
    nb = S // T
    specs, blk = _ret_specs(T, rev_nb=nb)
    tok = pl.BlockSpec((T, LANE), lambda h, b: (blk(b), h))

    def body(zq, zk, zv, zg, cos_r, sin_r, d_r, aq_r, ak_r, lam_r, gain_r, opre_r, st_r, dret_r,
             dq_o, dk_o, dv_o, dg_o, dgain_o, dstate):
        b = pl.program_id(1)

        @pl.when(b == 0)
        def _():
            dstate[...] = jnp.zeros_like(dstate)
            dgain_o[...] = jnp.zeros_like(dgain_o)

        even = (lax.broadcasted_iota(jnp.int32, (T, LANE), 1) & 1) == 0
        c, s = cos_r[...], sin_r[...]
        aq, ak, dm = aq_r[...], ak_r[...], d_r[...]
        q = _rot(zq[...], c, s, even)
        k = _rot(zk[...], c, s, even) * KSCALE
        qb, kb, vb = q.astype(BF16), k.astype(BF16), zv[...].astype(BF16)
        pb = (_dot_nt(qb, kb) * dm).astype(BF16)
        g = zg[...]
        sig = jax.nn.sigmoid(g)
        o = opre_r[...]
        mu = jnp.mean(o, axis=-1, keepdims=True)
        d = o - mu
        rstd = lax.rsqrt(jnp.mean(d * d, axis=-1, keepdims=True) + GN_EPS)
        y = d * rstd
        gain_v = gain_r[...]
        dret = dret_r[...].astype(F32)
        dyg = dret * (g * sig)
        dg_o[...] = (dret * (y * gain_v) * (sig * (1.0 + g * (1.0 - sig)))).astype(dg_o.dtype)
        dgain_o[...] += jnp.sum(dyg * y, axis=0, keepdims=True)
        dy = dyg * gain_v
        do = rstd * (dy - jnp.mean(dy, axis=-1, keepdims=True) - y * jnp.mean(dy * y, axis=-1, keepdims=True))
        dob = do.astype(BF16)
        stb = st_r[...].astype(BF16)
        dsn = dstate[...]
        dsnb = dsn.astype(BF16)
        dpb = (_dot_nt(dob, vb) * dm).astype(BF16)
        dq = _dot(dpb, kb) + _dot_nt(dob, stb) * aq
        dk = _dot_tn(dpb, qb) + _dot_nt(vb, dsnb) * ak
        dv = _dot_tn(pb, dob) + _dot((k * ak).astype(BF16), dsnb)
        dstate[...] = dsn * lam_r[...] + _dot_tn((q * aq).astype(BF16), dob)
        dq_o[...] = _rot_t(dq, c, s, even).astype(dq_o.dtype)
        dk_o[...] = _rot_t(dk * KSCALE, c, s, even).astype(dk_o.dtype)
        dv_o[...] = dv.astype(dv_o.dtype)

    part = jax.ShapeDtypeStruct((S, RET_WIDTH), BF16)
    return pl.pallas_call(
        body,
        name=name,
        grid=(RET_HEADS, nb),
        in_specs=specs + [tok, pl.BlockSpec((None, None, LANE, LANE), lambda h, b: (h, blk(b), 0, 0)), tok],
        out_specs=[tok, tok, tok, tok, pl.BlockSpec((1, LANE), lambda h, b: (0, h))],
        out_shape=[part, part, part, part, jax.ShapeDtypeStruct((1, RET_WIDTH), F32)],
        scratch_shapes=[pltpu.VMEM((LANE, LANE), F32)],
        compiler_params=_cp(("parallel", "arbitrary")),
    )(z, z, z, z, tb["cos"], tb["sin"], tb["dmat"], tb["aq"], tb["ak"], tb["lam"], gain, opre, states, dcat)


def _pool_counts(t0, rows):
    t = t0 + lax.broadcasted_iota(jnp.int32, (rows, POOL_WIDTH), 0)
    grp = lax.broadcasted_iota(jnp.int32, (rows, POOL_WIDTH), 1) >> 7
    win = jnp.where(grp == 0, POOL_WINDOWS[0], jnp.where(grp == 1, POOL_WINDOWS[1],
                    jnp.where(grp == 2, POOL_WINDOWS[2], POOL_WINDOWS[3])))
    return jnp.maximum(jnp.minimum(t + 1, win), 1).astype(F32), grp


def _window_sums(ext, grp, sign):
    n = ext.shape[0]
    sh = lambda v, k: pltpu.roll(v, k % n if sign > 0 else (n - k) % n, 0)
    s2 = ext + sh(ext, 1)
    s4 = s2 + sh(s2, 2)
    s8 = s4 + sh(s4, 4)
    s16 = s8 + sh(s8, 8)
    return jnp.where(grp == 0, s2, jnp.where(grp == 1, s4, jnp.where(grp == 2, s8, s16)))


def _pool_fwd(name, z, w_pool, scale):
    S = z.shape[0]
    T = min(512, S)
    nb = S // T
    pcol = AB_IN_WIDTH // POOL_WIDTH - 1
    hb = T // POOL_HALO

    def body(p_ref, halo_ref, w_ref, sc_ref, out_ref, pooled_ref):
        b = pl.program_id(0)
        cur = p_ref[...]
        halo = jnp.where(b > 0, halo_ref[...], 0.0)
        ext = jnp.concatenate([halo, cur], axis=0)
        cnt, grp = _pool_counts(b * T - POOL_HALO, T + POOL_HALO)
        sums = _window_sums(ext, grp, +1)
        pooled = (sums / cnt)[POOL_HALO:] - cur
        pb = pooled.astype(BF16)
        pooled_ref[...] = pb
        for gi in range(len(POOL_WINDOWS)):
            cs = slice(gi * POOL_GROUP_DIM, (gi + 1) * POOL_GROUP_DIM)
            mixed = _dot(pb[:, cs], w_ref[gi].astype(BF16))
            out_ref[:, cs] = (mixed * sc_ref[:, cs]).astype(out_ref.dtype)

    outs = jax.ShapeDtypeStruct((S, POOL_WIDTH), BF16)
    return pl.pallas_call(
        body,
        name=name,
        grid=(nb,),
        in_specs=[pl.BlockSpec((T, POOL_WIDTH), lambda b: (b, pcol)),
                  pl.BlockSpec((POOL_HALO, POOL_WIDTH), lambda b: (jnp.maximum(b * hb - 1, 0), pcol)),
                  pl.BlockSpec((4, POOL_GROUP_DIM, POOL_GROUP_DIM), lambda b: (0, 0, 0)),
                  pl.BlockSpec((1, POOL_WIDTH), lambda b: (0, 0))],
        out_specs=[pl.BlockSpec((T, POOL_WIDTH), lambda b: (b, 0)), pl.BlockSpec((T, POOL_WIDTH), lambda b: (b, 0))],
        out_shape=[outs, outs],
        compiler_params=_cp(("parallel",)),
    )(z, z, w_pool, scale)


def _pool_bwd(name, pooled, w_pool, scale, dcat):
    S = pooled.shape[0]
    T = min(512, S)
    nb = S // T
    hb = T // POOL_HALO
    last_h = S // POOL_HALO - 1

    def body(d_ref, dn_ref, pooled_ref, w_ref, sc_ref, dp_ref, dw_ref, dsc_ref):
        b = pl.program_id(0)

        @pl.when(b == 0)
        def _():
            dw_ref[...] = jnp.zeros_like(dw_ref)
            dsc_ref[...] = jnp.zeros_like(dsc_ref)

        sc = sc_ref[...]
        dout = d_ref[...].astype(F32)
        dnext = jnp.where(b < nb - 1, dn_ref[...].astype(F32), 0.0)
        dmix = jnp.concatenate([dout, dnext], axis=0) * sc
        dmb = dmix.astype(BF16)
        pb = pooled_ref[...]
        dpooled = []
        for gi in range(len(POOL_WINDOWS)):
            cs = slice(gi * POOL_GROUP_DIM, (gi + 1) * POOL_GROUP_DIM)
            wb = w_ref[gi].astype(BF16)
            dpooled.append(_dot_nt(dmb[:, cs], wb))
            dw_ref[gi] += _dot_tn(pb[:, cs], dmb[:T, cs])
            mixed = _dot(pb[:, cs], wb)
            dsc_ref[:, cs] += jnp.sum(dout[:, cs] * mixed, axis=0, keepdims=True)
        dpl = jnp.concatenate(dpooled, axis=1)
        cnt, grp = _pool_counts(b * T, T + POOL_HALO)
        sums = _window_sums(dpl / cnt, grp, -1)
        dp_ref[...] = (sums[:T] - dpl[:T]).astype(dp_ref.dtype)

    return pl.pallas_call(
        body,
        name=name,
        grid=(nb,),
        in_specs=[pl.BlockSpec((T, POOL_WIDTH), lambda b: (b, 1)),
                  pl.BlockSpec((POOL_HALO, POOL_WIDTH), lambda b: (jnp.minimum((b + 1) * hb, last_h), 1)),
                  pl.BlockSpec((T, POOL_WIDTH), lambda b: (b, 0)),
                  pl.BlockSpec((4, POOL_GROUP_DIM, POOL_GROUP_DIM), lambda b: (0, 0, 0)),
                  pl.BlockSpec((1, POOL_WIDTH), lambda b: (0, 0))],
        out_specs=[pl.BlockSpec((T, POOL_WIDTH), lambda b: (b, 0)),
                   pl.BlockSpec((4, POOL_GROUP_DIM, POOL_GROUP_DIM), lambda b: (0, 0, 0)),
                   pl.BlockSpec((1, POOL_WIDTH), lambda b: (0, 0))],
        out_shape=[jax.ShapeDtypeStruct((S, POOL_WIDTH), BF16),
                   jax.ShapeDtypeStruct((4, POOL_GROUP_DIM, POOL_GROUP_DIM), F32),
                   jax.ShapeDtypeStruct((1, POOL_WIDTH), F32)],
        compiler_params=_cp(("arbitrary",)),
    )(dcat, dcat, pooled, w_pool, scale)


_REL_COLS = CHUNK * BAND
_REL_TILE = _REL_COLS // 8


def _rel_index():
    n = np.arange(CHUNK)[:, None]
    j = np.arange(BAND)[None, :]
    rel = np.clip(n + LEFT_CHUNKS * CHUNK - j, -REL_CLIP, REL_CLIP) + REL_CLIP
    return jnp.asarray(rel.reshape(1, _REL_COLS), jnp.int32)


def _bias_table(name, rel_bias, rel_idx):
    rb = jnp.pad(rel_bias, ((0, 0), (0, N_REL_PAD - N_REL)))

    def body(rb_ref, idx_ref, o_ref):
        r = lax.broadcasted_iota(jnp.int32, (N_REL_PAD, _REL_TILE), 0)
        onehot = (r == idx_ref[...]).astype(F32)
        o_ref[...] = jnp.dot(rb_ref[...], onehot, precision=lax.Precision.HIGHEST, preferred_element_type=F32)

    return pl.pallas_call(
        body,
        name=name,
        grid=(_REL_COLS // _REL_TILE,),
        in_specs=[pl.BlockSpec((ATT_HEADS, N_REL_PAD), lambda i: (0, 0)), pl.BlockSpec((1, _REL_TILE), lambda i: (0, i))],
        out_specs=pl.BlockSpec((ATT_HEADS, _REL_TILE), lambda i: (0, i)),
        out_shape=jax.ShapeDtypeStruct((ATT_HEADS, _REL_COLS), F32),
        compiler_params=_cp(("parallel",)),
    )(rb, rel_idx)


def _bias_grad(name, ds_sum, rel_idx):
    n = _REL_COLS // _REL_TILE

    def body(ds_ref, idx_ref, o_ref):
        i = pl.program_id(0)
        r = lax.broadcasted_iota(jnp.int32, (N_REL_PAD, _REL_TILE), 0)
        onehot = (r == idx_ref[...]).astype(F32)
        part = lax.dot_general(ds_ref[...], onehot, (((1,), (1,)), ((), ())),
                               precision=lax.Precision.HIGHEST, preferred_element_type=F32)

        @pl.when(i == 0)
        def _():
            o_ref[...] = part

        @pl.when(i > 0)
        def _():
            o_ref[...] += part

    out = pl.pallas_call(
        body,
        name=name,
        grid=(n,),
        in_specs=[pl.BlockSpec((ATT_HEADS, _REL_TILE), lambda i: (0, i)), pl.BlockSpec((1, _REL_TILE), lambda i: (0, i))],
        out_specs=pl.BlockSpec((ATT_HEADS, N_REL_PAD), lambda i: (0, 0)),
        out_shape=jax.ShapeDtypeStruct((ATT_HEADS, N_REL_PAD), F32),
        compiler_params=_cp(("arbitrary",)),
    )(ds_sum, rel_idx)
    return out[:, :N_REL]


def _attn_probs(q_ref, kw_ref, bias_ref, e, i, first_block, col):
    hs = slice(e * ATT_HEAD_DIM, (e + 1) * ATT_HEAD_DIM)
    qi = q_ref[i * CHUNK:(i + 1) * CHUNK, hs] * QSCALE
    kb = kw_ref[i * CHUNK:i * CHUNK + BAND, hs]
    s = _dot_nt(qi, kb) + bias_ref[e]
    s = jnp.where(col >= first_block * (ATT_BLOCK - i * CHUNK), s, NEG_INF)
    p = jnp.exp(s - jnp.max(s, axis=-1, keepdims=True))
    return p / jnp.sum(p, axis=-1, keepdims=True), qi, kb


def _attn_in_specs(nb):
    T = ATT_BLOCK
    hp = ATT_HEADS // 2
    cur = lambda off: pl.BlockSpec((T, LANE), lambda h, b: (jnp.minimum(b, nb - 1), off + h))
    prev = lambda off: pl.BlockSpec((T, LANE), lambda h, b: (jnp.clip(b - 1, 0, nb - 1), off + h))
    return [cur(0), prev(hp), cur(hp), prev(2 * hp), cur(2 * hp),
            pl.BlockSpec((None, 2, CHUNK, BAND), lambda h, b: (h, 0, 0, 0))]


def _attn_fwd(name, qkv, bias):
    S = qkv.shape[0]
    T = ATT_BLOCK
    nb = S // T

    def body(q_ref, kp_ref, kc_ref, vp_ref, vc_ref, bias_ref, o_ref, kw_ref, vw_ref):
        b = pl.program_id(1)
        kw_ref[0:T] = kp_ref[...]
        kw_ref[T:2 * T] = kc_ref[...]
        vw_ref[0:T] = vp_ref[...]
        vw_ref[T:2 * T] = vc_ref[...]
        col = lax.broadcasted_iota(jnp.int32, (CHUNK, BAND), 1)
        first = jnp.where(b == 0, 1, 0)
        for e in range(2):
            hs = slice(e * ATT_HEAD_DIM, (e + 1) * ATT_HEAD_DIM)
            for i in range(T // CHUNK):
                p, _, _ = _attn_probs(q_ref, kw_ref, bias_ref, e, i, first, col)
                vb = vw_ref[i * CHUNK:i * CHUNK + BAND, hs]
                o_ref[i * CHUNK:(i + 1) * CHUNK, hs] = _dot(p.astype(BF16), vb).astype(o_ref.dtype)

    return pl.pallas_call(
        body,
        name=name,
        grid=(ATT_HEADS // 2, nb),
        in_specs=_attn_in_specs(nb),
        out_specs=pl.BlockSpec((T, LANE), lambda h, b: (b, h)),
        out_shape=jax.ShapeDtypeStruct((S, D_MODEL), BF16),
        scratch_shapes=[pltpu.VMEM((2 * T, LANE), BF16), pltpu.VMEM((2 * T, LANE), BF16)],
        compiler_params=_cp(("parallel", "arbitrary")),
    )(qkv, qkv, qkv, qkv, qkv, bias)


def _attn_bwd(name, qkv, bias, do):
    S = qkv.shape[0]
    T = ATT_BLOCK
    nb = S // T

    def body(q_ref, kp_ref, kc_ref, vp_ref, vc_ref, bias_ref, do_ref,
             dq_ref, dk_ref, dv_ref, dbias_ref, kw_ref, vw_ref, dkw_ref, dvw_ref):
        b = pl.program_id(1)

        @pl.when(b == 0)
        def _():
            dbias_ref[...] = jnp.zeros_like(dbias_ref)
            dkw_ref[T:2 * T] = jnp.zeros((T, LANE), F32)
            dvw_ref[T:2 * T] = jnp.zeros((T, LANE), F32)

        dkw_ref[0:T] = dkw_ref[T:2 * T]
        dvw_ref[0:T] = dvw_ref[T:2 * T]
        dkw_ref[T:2 * T] = jnp.zeros((T, LANE), F32)
        dvw_ref[T:2 * T] = jnp.zeros((T, LANE), F32)

        @pl.when(b < nb)
        def _():
            kw_ref[0:T] = kp_ref[...]
            kw_ref[T:2 * T] = kc_ref[...]
            vw_ref[0:T] = vp_ref[...]
            vw_ref[T:2 * T] = vc_ref[...]
            col = lax.broadcasted_iota(jnp.int32, (CHUNK, BAND), 1)
            first = jnp.where(b == 0, 1, 0)
            for e in range(2):
                hs = slice(e * ATT_HEAD_DIM, (e + 1) * ATT_HEAD_DIM)
                for i in range(T // CHUNK):
                    rows = slice(i * CHUNK, (i + 1) * CHUNK)
                    win = slice(i * CHUNK, i * CHUNK + BAND)
                    p, qi, kb = _attn_probs(q_ref, kw_ref, bias_ref, e, i, first, col)
                    vb = vw_ref[win, hs]
                    doi = do_ref[rows, hs]
                    dp = _dot_nt(doi, vb)
                    ds = p * (dp - jnp.sum(p * dp, axis=-1, keepdims=True))
                    dbias_ref[e] += ds
                    dsb = ds.astype(BF16)
                    dq_ref[rows, hs] = (_dot(dsb, kb) * QSCALE).astype(dq_ref.dtype)
                    dkw_ref[win, hs] += _dot_tn(dsb, qi)
                    dvw_ref[win, hs] += _dot_tn(p.astype(BF16), doi)

        @pl.when(b > 0)
        def _():
            dk_ref[...] = dkw_ref[0:T].astype(dk_ref.dtype)
            dv_ref[...] = dvw_ref[0:T].astype(dv_ref.dtype)

    tok = jax.ShapeDtypeStruct((S, D_MODEL), BF16)
    prev_out = pl.BlockSpec((T, LANE), lambda h, b: (jnp.maximum(b - 1, 0), h))
    return pl.pallas_call(
        body,
        name=name,
        grid=(ATT_HEADS // 2, nb + 1),
        in_specs=_attn_in_specs(nb) + [pl.BlockSpec((T, LANE), lambda h, b: (jnp.minimum(b, nb - 1), h))],
        out_specs=[pl.BlockSpec((T, LANE), lambda h, b: (jnp.minimum(b, nb - 1), h)), prev_out, prev_out,
                   pl.BlockSpec((None, 2, CHUNK, BAND), lambda h, b: (h, 0, 0, 0))],
        out_shape=[tok, tok, tok, jax.ShapeDtypeStruct((ATT_HEADS // 2, 2, CHUNK, BAND), F32)],
        scratch_shapes=[pltpu.VMEM((2 * T, LANE), BF16), pltpu.VMEM((2 * T, LANE), BF16),
                        pltpu.VMEM((2 * T, LANE), F32), pltpu.VMEM((2 * T, LANE), F32)],
        compiler_params=_cp(("parallel", "arbitrary")),
    )(qkv, qkv, qkv, qkv, qkv, bias, do)


def _local_step(x, target, small, big):
    S = x.shape[0]
    tb = _ret_tables(S)
    rel_idx = _rel_index()
    saved = []
    for layer in range(DEPTH):
        i = layer // 2
        st = {"x_in": x}
        h = _rms_fwd(f"mix_norm_fwd{layer}", x, small["mix_norm"][layer:layer + 1])
        st["h"] = h
        if layer % 2 == 0:
            z = _mm(f"ab_in_fwd{layer}", "nn", h, big["ab_w_in"], lb=i, tn=640, out_dtype=F32)
            gain = small["ab_gn_gain"][i:i + 1]
            ret, opre, states = _ret_fwd(f"ret_fwd{layer}", z, tb, gain)
            pout, pooled = _pool_fwd(f"pool_fwd{layer}", z, small["ab_w_pool"][i], small["ab_pool_scale"][i:i + 1])
            cat = jnp.concatenate([ret, pout], axis=1)
            st.update(z=z, opre=opre, states=states, pooled=pooled, cat=cat)
            x = _mm(f"ab_out_fwd{layer}", "nn", cat, big["ab_w_out"], lb=i, extras=(x,), epi=lambda acc, r: acc + r)
        else:
            qkv = _mm(f"qkv_fwd{layer}", "nn", h, big["c_w_qkv"], lb=i, out_dtype=BF16)
            bias = _bias_table(f"bias_table{layer}", small["c_rel_bias"][i], rel_idx)
            bias = bias.reshape(ATT_HEADS // 2, 2, CHUNK, BAND)
            att = _attn_fwd(f"attn_fwd{layer}", qkv, bias)
            st.update(qkv=qkv, bias=bias, att=att)
            x = _mm(f"c_out_fwd{layer}", "nn", att, big["c_w_out"], lb=i, extras=(x,), epi=lambda acc, r: acc + r)
        st["x_mid"] = x
        hn = _rms_fwd(f"ffn_norm_fwd{layer}", x, small["ffn_norm"][layer:layer + 1])
        u = _mm(f"ffn_in_fwd{layer}", "nn", hn, big["w_ffn_in"], lb=layer, out_dtype=BF16)
        x = _mm(f"ffn_out_fwd{layer}", "nn", u, big["w_ffn_out"], lb=layer, a_fn=_relu2, extras=(x,),
                epi=lambda acc, r: acc + r)
        st.update(hn=hn, u=u)
        saved.append(st)

    loss, dx, d_final = _loss_head(x, small["final_norm"].reshape(1, D_MODEL), target)

    gs = {k: [None] * v.shape[0] for k, v in small.items() if k != "final_norm"}
    gb = {k: [None] * v.shape[0] for k, v in big.items()}
    for layer in reversed(range(DEPTH)):
        i = layer // 2
        st = saved[layer]
        du = _mm(f"ffn_out_bwd{layer}", "nt", dx, big["w_ffn_out"], lb=layer, extras=(st["u"],),
                 epi=lambda acc, u: acc * (2.0 * jnp.maximum(u, 0).astype(F32)), out_dtype=BF16)
        gb["w_ffn_out"][layer] = _mm(f"ffn_out_dw{layer}", "tn", st["u"], dx, a_fn=_relu2)
        dhn = _mm(f"ffn_in_bwd{layer}", "nt", du, big["w_ffn_in"], lb=layer)
        gb["w_ffn_in"][layer] = _mm(f"ffn_in_dw{layer}", "tn", st["hn"], du)
        dx, gs["ffn_norm"][layer] = _rms_bwd(f"ffn_norm_bwd{layer}", st["x_mid"], small["ffn_norm"][layer:layer + 1], dhn, dx)
        if layer % 2 == 0:
            dcat = _mm(f"ab_out_bwd{layer}", "nt", dx, big["ab_w_out"], lb=i, out_dtype=BF16)
            gb["ab_w_out"][i] = _mm(f"ab_out_dw{layer}", "tn", st["cat"], dx)
            gain = small["ab_gn_gain"][i:i + 1]
            dq, dk, dv, dg, gs["ab_gn_gain"][i] = _ret_bwd(f"ret_bwd{layer}", st["z"], tb, gain, st["opre"], st["states"], dcat)
            dp, gs["ab_w_pool"][i], gs["ab_pool_scale"][i] = _pool_bwd(
                f"pool_bwd{layer}", st["pooled"], small["ab_w_pool"][i], small["ab_pool_scale"][i:i + 1], dcat)
            dz = jnp.concatenate([dq, dk, dv, dg, dp], axis=1)
            dh = _mm(f"ab_in_bwd{layer}", "nt", dz, big["ab_w_in"], lb=i, tk=1280)
            gb["ab_w_in"][i] = _mm(f"ab_in_dw{layer}", "tn", st["h"], dz, tn=640)
        else:
            datt = _mm(f"c_out_bwd{layer}", "nt", dx, big["c_w_out"], lb=i, out_dtype=BF16)
            gb["c_w_out"][i] = _mm(f"c_out_dw{layer}", "tn", st["att"], dx)
            dq, dk, dv, dbias = _attn_bwd(f"attn_bwd{layer}", st["qkv"], st["bias"], datt)
            gs["c_rel_bias"][i] = _bias_grad(f"bias_grad{layer}", dbias.reshape(ATT_HEADS, _REL_COLS), rel_idx)
            dqkv = jnp.concatenate([dq, dk, dv], axis=1)
            dh = _mm(f"qkv_bwd{layer}", "nt", dqkv, big["c_w_qkv"], lb=i)
            gb["c_w_qkv"][i] = _mm(f"qkv_dw{layer}", "tn", st["h"], dqkv)
        dx, gs["mix_norm"][layer] = _rms_bwd(f"mix_norm_bwd{layer}", st["x_in"], small["mix_norm"][layer:layer + 1], dh, dx)

    g_small = {
        "mix_norm": jnp.concatenate(gs["mix_norm"], axis=0),
        "ffn_norm": jnp.concatenate(gs["ffn_norm"], axis=0),
        "ab_gn_gain": jnp.concatenate(gs["ab_gn_gain"], axis=0),
        "ab_w_pool": jnp.stack(gs["ab_w_pool"], axis=0),
        "ab_pool_scale": jnp.concatenate(gs["ab_pool_scale"], axis=0),
        "c_rel_bias": jnp.stack(gs["c_rel_bias"], axis=0),
        "final_norm": d_final.reshape(D_MODEL),
    }
    g_big = {k: jnp.stack(v, axis=0) for k, v in gb.items()}
    return loss, dx, g_small, g_big


_BIG = ("w_ffn_in", "w_ffn_out", "ab_w_in", "ab_w_out", "c_w_qkv", "c_w_out")
_SHARD_AXIS = {"w_ffn_in": 2, "w_ffn_out": 1, "ab_w_in": 2, "ab_w_out": 1, "c_w_qkv": 2, "c_w_out": 1}
_SMALL = ("mix_norm", "ffn_norm", "ab_gn_gain", "ab_w_pool", "ab_pool_scale", "c_rel_bias", "final_norm")
_ANY = pl.BlockSpec(memory_space=pl.ANY)


def _place():
    x, y, c = lax.axis_index("x"), lax.axis_index("y"), lax.axis_index("c")
    chips = [(1 - x, y), (x, 1 - y), (1 - x, 1 - y)]
    return x, y, c, chips


def _sub(ref, axis, start, size):
    idx = [slice(None)] * 3
    idx[axis] = pl.ds(pl.multiple_of(start, LANE), size)
    return ref.at[tuple(idx)]


def _gather_weights(shards):
    n = len(_BIG)
    axes = [_SHARD_AXIS[k] for k in _BIG]
    sizes = [s.shape[a] for s, a in zip(shards, axes)]

    def body(*refs):
        ins, outs = refs[:n], refs[n:2 * n]
        send_sems, recv_sems, local_sems = refs[2 * n:]
        x, y, c, chips = _place()
        me = 2 * x + y
        sends, recvs, locs = [], [], []
        for w in range(n):
            mine = _sub(outs[w], axes[w], me * sizes[w], sizes[w])
            loc = pltpu.make_async_copy(ins[w], mine, local_sems.at[w])
            loc.start()
            locs.append(loc)
            for k, (px, py) in enumerate(chips):
                s = w * 3 + k
                cp = pltpu.make_async_remote_copy(src_ref=ins[w], dst_ref=mine, send_sem=send_sems.at[s],
                                                  recv_sem=recv_sems.at[s], device_id=(px, py, c), device_id_type=MESH)
                cp.start()
                sends.append(cp)
                theirs = _sub(outs[w], axes[w], (2 * px + py) * sizes[w], sizes[w])
                recvs.append(pltpu.make_async_remote_copy(src_ref=ins[w], dst_ref=theirs, send_sem=send_sems.at[s],
                                                          recv_sem=recv_sems.at[s], device_id=(px, py, c), device_id_type=MESH))
        for cp in recvs:
            cp.wait_recv()
        for cp in sends:
            cp.wait_send()
        for loc in locs:
            loc.wait()

    def full(s, a):
        shape = list(s.shape)
        shape[a] *= N_CHIPS
        return jax.ShapeDtypeStruct(tuple(shape), s.dtype)

    return pl.pallas_call(
        body,
        name="gather_weights",
        in_specs=[_ANY] * n,
        out_specs=[_ANY] * n,
        out_shape=[full(s, a) for s, a in zip(shards, axes)],
        scratch_shapes=[pltpu.SemaphoreType.DMA((3 * n,)), pltpu.SemaphoreType.DMA((3 * n,)), pltpu.SemaphoreType.DMA((n,))],
        compiler_params=pltpu.CompilerParams(has_side_effects=True),
    )(*shards)


def _half_axis(name):
    return 3 - _SHARD_AXIS[name]


def _pair_exchange(grads):
    n = len(_BIG)
    haxes = [_half_axis(k) for k in _BIG]
    halves = [g.shape[a] // 2 for g, a in zip(grads, haxes)]

    def body(*refs):
        ins, outs = refs[:n], refs[n:2 * n]
        send_sems, recv_sems = refs[2 * n:]
        x, y, c, _ = _place()
        cps = []
        for w in range(n):
            src = _sub(ins[w], haxes[w], (1 - c) * halves[w], halves[w])
            cp = pltpu.make_async_remote_copy(src_ref=src, dst_ref=outs[w], send_sem=send_sems.at[w],
                                              recv_sem=recv_sems.at[w], device_id=(x, y, 1 - c), device_id_type=MESH)
            cp.start()
            cps.append(cp)
        for cp in cps:
            cp.wait_recv()
        for cp in cps:
            cp.wait_send()

    def half(g, a):
        shape = list(g.shape)
        shape[a] //= 2
        return jax.ShapeDtypeStruct(tuple(shape), g.dtype)

    return pl.pallas_call(
        body,
        name="pair_exchange",
        in_specs=[_ANY] * n,
        out_specs=[_ANY] * n,
        out_shape=[half(g, a) for g, a in zip(grads, haxes)],
        scratch_shapes=[pltpu.SemaphoreType.DMA((n,)), pltpu.SemaphoreType.DMA((n,))],
        compiler_params=pltpu.CompilerParams(has_side_effects=True),
    )(*grads)


def _rows_tile(rows, cols):
    tr = rows
    while tr * cols > (1 << 19) and tr % 16 == 0:
        tr //= 2
    return tr


def _pair_sum(name, g, recv, c_arr, haxis):
    L, R, C = recv.shape
    tr = _rows_tile(R, C)
    nr = R // tr
    if haxis == 1:
        g_idx = lambda l, i, s: (l, s[0] * nr + i, 0)
    else:
        g_idx = lambda l, i, s: (l, i, s[0])

    def body(s_ref, g_ref, r_ref, o_ref):
        o_ref[...] = g_ref[...] + r_ref[...]

    return pl.pallas_call(
        body,
        name=name,
        grid_spec=pltpu.PrefetchScalarGridSpec(
            num_scalar_prefetch=1,
            grid=(L, nr),
            in_specs=[pl.BlockSpec((None, tr, C), g_idx), pl.BlockSpec((None, tr, C), lambda l, i, s: (l, i, 0))],
            out_specs=pl.BlockSpec((None, tr, C), lambda l, i, s: (l, i, 0)),
        ),
        out_shape=jax.ShapeDtypeStruct((L, R, C), F32),
        compiler_params=_cp(("parallel", "parallel")),
    )(c_arr, g, recv)


def _chip_exchange(psums):
    n = len(_BIG)
    axes = [_SHARD_AXIS[k] for k in _BIG]
    sizes = [p.shape[a] // N_CHIPS for p, a in zip(psums, axes)]

    def body(*refs):
        ins, outs = refs[:n], refs[n:2 * n]
        send_sems, recv_sems = refs[2 * n:]
        x, y, c, chips = _place()
        cps = []
        for w in range(n):
            for k, (px, py) in enumerate(chips):
                s = w * 3 + k
                src = _sub(ins[w], axes[w], (2 * px + py) * sizes[w], sizes[w])
                cp = pltpu.make_async_remote_copy(src_ref=src, dst_ref=outs[w].at[k], send_sem=send_sems.at[s],
                                                  recv_sem=recv_sems.at[s], device_id=(px, py, c), device_id_type=MESH)
                cp.start()
                cps.append(cp)
        for cp in cps:
            cp.wait_recv()
        for cp in cps:
            cp.wait_send()

    def landing(p, a):
        shape = list(p.shape)
        shape[a] //= N_CHIPS
        return jax.ShapeDtypeStruct((3,) + tuple(shape), p.dtype)

    return pl.pallas_call(
        body,
        name="chip_exchange",
        in_specs=[_ANY] * n,
        out_specs=[_ANY] * n,
        out_shape=[landing(p, a) for p, a in zip(psums, axes)],
        scratch_shapes=[pltpu.SemaphoreType.DMA((3 * n,)), pltpu.SemaphoreType.DMA((3 * n,))],
        compiler_params=pltpu.CompilerParams(has_side_effects=True),
    )(*psums)


def _chip_sum(name, psum, landed, pos, saxis):
    L = psum.shape[0]
    _, _, R, C = landed.shape
    tr = _rows_tile(R, C)
    nr = R // tr
    if saxis == 2:
        p_idx = lambda l, i, s: (l, i, s[0])
        o_idx = lambda l, i, s: (l, s[1] * nr + i, 0)
        o_shape = (L, 2 * R, C)
    else:
        p_idx = lambda l, i, s: (l, s[0] * nr + i, 0)
        o_idx = lambda l, i, s: (l, i, s[1])
        o_shape = (L, R, 2 * C)

    def body(s_ref, p_ref, l_ref, o_ref):
        o_ref[...] = ((p_ref[...] + l_ref[0]) + l_ref[1]) + l_ref[2]

    return pl.pallas_call(
        body,
        name=name,
        grid_spec=pltpu.PrefetchScalarGridSpec(
            num_scalar_prefetch=1,
            grid=(L, nr),
            in_specs=[pl.BlockSpec((None, tr, C), p_idx), pl.BlockSpec((3, None, tr, C), lambda l, i, s: (0, l, i, 0))],
            out_specs=pl.BlockSpec((None, tr, C), o_idx),
        ),
        out_shape=jax.ShapeDtypeStruct(o_shape, F32),
        compiler_params=_cp(("parallel", "parallel")),
    )(pos, psum, landed)


def _sibling_fill(halves):
    n = len(_BIG)
    haxes = [_half_axis(k) for k in _BIG]
    sizes = [h.shape[a] // 2 for h, a in zip(halves, haxes)]

    def body(*refs):
        outs = refs[n:2 * n]
        send_sems, recv_sems = refs[2 * n:]
        x, y, c, _ = _place()
        sends, recvs = [], []
        for w in range(n):
            mine = _sub(outs[w], haxes[w], c * sizes[w], sizes[w])
            theirs = _sub(outs[w], haxes[w], (1 - c) * sizes[w], sizes[w])
            cp = pltpu.make_async_remote_copy(src_ref=mine, dst_ref=mine, send_sem=send_sems.at[w],
                                              recv_sem=recv_sems.at[w], device_id=(x, y, 1 - c), device_id_type=MESH)
            cp.start()
            sends.append(cp)
            recvs.append(pltpu.make_async_remote_copy(src_ref=theirs, dst_ref=theirs, send_sem=send_sems.at[w],
                                                      recv_sem=recv_sems.at[w], device_id=(x, y, 1 - c), device_id_type=MESH))
        for cp in recvs:
            cp.wait_recv()
        for cp in sends:
            cp.wait_send()

    return pl.pallas_call(
        body,
        name="sibling_fill",
        in_specs=[_ANY] * n,
        out_specs=[_ANY] * n,
        out_shape=[jax.ShapeDtypeStruct(h.shape, h.dtype) for h in halves],
        input_output_aliases={w: w for w in range(n)},
        scratch_shapes=[pltpu.SemaphoreType.DMA((n,)), pltpu.SemaphoreType.DMA((n,))],
        compiler_params=pltpu.CompilerParams(has_side_effects=True),
    )(*halves)


def _all_reduce_small(packed):
    R = packed.shape[0]

    def body(p_ref, o_ref, land_ref, send_sems, recv_sems):
        x, y, c, _ = _place()
        me = 4 * x + 2 * y + c
        sends, recvs = [], []
        for r in range(1, N_DEV):
            px, py, pc = x ^ (r >> 2), y ^ ((r >> 1) & 1), c ^ (r & 1)
            cp = pltpu.make_async_remote_copy(src_ref=p_ref, dst_ref=land_ref.at[me], send_sem=send_sems.at[r - 1],
                                              recv_sem=recv_sems.at[r - 1], device_id=(px, py, pc), device_id_type=MESH)
            cp.start()
            sends.append(cp)
            recvs.append(pltpu.make_async_remote_copy(src_ref=p_ref, dst_ref=land_ref.at[4 * px + 2 * py + pc],
                                                      send_sem=send_sems.at[r - 1], recv_sem=recv_sems.at[r - 1],
                                                      device_id=(px, py, pc), device_id_type=MESH))
        land_ref[me] = p_ref[...]
        for cp in recvs:
            cp.wait_recv()
        for cp in sends:
            cp.wait_send()
        acc = land_ref[0]
        for d in range(1, N_DEV):
            acc = acc + land_ref[d]
        o_ref[...] = acc

    vm = pl.BlockSpec(memory_space=pltpu.VMEM)
    return pl.pallas_call(
        body,
        name="all_reduce_small",
        in_specs=[vm],
        out_specs=vm,
        out_shape=jax.ShapeDtypeStruct((R, LANE), F32),
        scratch_shapes=[pltpu.VMEM((N_DEV, R, LANE), F32), pltpu.SemaphoreType.DMA((N_DEV - 1,)),
                        pltpu.SemaphoreType.DMA((N_DEV - 1,))],
        compiler_params=pltpu.CompilerParams(has_side_effects=True, vmem_limit_bytes=VMEM_LIMIT),
    )(packed)


def _adamw(name, w, g, m, v):
    R, C = w.shape
    tr = _rows_tile(R, C)
    c1 = 1.0 - ADAM_B1 ** ADAM_STEP
    c2 = 1.0 - ADAM_B2 ** ADAM_STEP

    def body(w_ref, g_ref, m_ref, v_ref, d_ref, nm_ref, nv_ref):
        gv = g_ref[...]
        nm = ADAM_B1 * m_ref[...] + (1.0 - ADAM_B1) * gv
        nv = ADAM_B2 * v_ref[...] + (1.0 - ADAM_B2) * (gv * gv)
        nm_ref[...] = nm
        nv_ref[...] = nv
        d_ref[...] = -ADAM_LR * ((nm / c1) / (jnp.sqrt(nv / c2) + ADAM_EPS) + ADAM_WD * w_ref[...])

    blk = pl.BlockSpec((tr, C), lambda i: (i, 0))
    out = jax.ShapeDtypeStruct((R, C), F32)
    return pl.pallas_call(
        body,
        name=name,
        grid=(R // tr,),
        in_specs=[blk] * 4,
        out_specs=[blk] * 3,
        out_shape=[out] * 3,
        compiler_params=_cp(("parallel",)),
    )(w, g, m, v)


def _pack(parts):
    rows = []
    for p in parts:
        flat = p.reshape(-1).astype(F32)
        n = -(-flat.shape[0] // (8 * LANE)) * (8 * LANE)
        rows.append(jnp.pad(flat, (0, n - flat.shape[0])).reshape(n // LANE, LANE))
    return jnp.concatenate(rows, axis=0)


def _unpack(packed, like):
    out, r = [], 0
    for p in like:
        size = int(np.prod(p.shape))
        n = -(-size // (8 * LANE)) * 8
        out.append(packed[r:r + n].reshape(-1)[:size].reshape(p.shape))
        r += n
    return out


def kernel(x, mix_norm, ffn_norm, w_ffn_in, w_ffn_out, ab_w_in, ab_gn_gain, ab_w_pool, ab_pool_scale, ab_w_out, c_w_qkv, c_rel_bias, c_w_out, final_norm, loss_target, m_mix_norm, m_ffn_norm, m_w_ffn_in, m_w_ffn_out, m_ab_w_in, m_ab_gn_gain, m_ab_w_pool, m_ab_pool_scale, m_ab_w_out, m_c_w_qkv, m_c_rel_bias, m_c_w_out, m_final_norm, v_mix_norm, v_ffn_norm, v_w_ffn_in, v_w_ffn_out, v_ab_w_in, v_ab_gn_gain, v_ab_w_pool, v_ab_pool_scale, v_ab_w_out, v_c_w_qkv, v_c_rel_bias, v_c_w_out, v_final_norm):
    w = dict(mix_norm=mix_norm, ffn_norm=ffn_norm, w_ffn_in=w_ffn_in, w_ffn_out=w_ffn_out, ab_w_in=ab_w_in,
             ab_gn_gain=ab_gn_gain, ab_w_pool=ab_w_pool, ab_pool_scale=ab_pool_scale, ab_w_out=ab_w_out,
             c_w_qkv=c_w_qkv, c_rel_bias=c_rel_bias, c_w_out=c_w_out, final_norm=final_norm)
    m = dict(mix_norm=m_mix_norm, ffn_norm=m_ffn_norm, w_ffn_in=m_w_ffn_in, w_ffn_out=m_w_ffn_out, ab_w_in=m_ab_w_in,
             ab_gn_gain=m_ab_gn_gain, ab_w_pool=m_ab_w_pool, ab_pool_scale=m_ab_pool_scale, ab_w_out=m_ab_w_out,
             c_w_qkv=m_c_w_qkv, c_rel_bias=m_c_rel_bias, c_w_out=m_c_w_out, final_norm=m_final_norm)
    v = dict(mix_norm=v_mix_norm, ffn_norm=v_ffn_norm, w_ffn_in=v_w_ffn_in, w_ffn_out=v_w_ffn_out, ab_w_in=v_ab_w_in,
             ab_gn_gain=v_ab_gn_gain, ab_w_pool=v_ab_w_pool, ab_pool_scale=v_ab_pool_scale, ab_w_out=v_ab_w_out,
             c_w_qkv=v_c_w_qkv, c_rel_bias=v_c_rel_bias, c_w_out=v_c_w_out, final_norm=v_final_norm)
    S = x.shape[1]
    cx, cy, cc = lax.axis_index("x"), lax.axis_index("y"), lax.axis_index("c")
    core = jnp.reshape(cc, (1,)).astype(jnp.int32)
    pos = jnp.stack([2 * cx + cy, cc]).astype(jnp.int32)

    full = _gather_weights([w[k].astype(BF16) for k in _BIG])
    big = dict(zip(_BIG, full))
    small = {k: w[k] for k in _SMALL}
    loss, grad_x, g_small, g_big = _local_step(x.reshape(S, D_MODEL), loss_target.reshape(S, D_MODEL), small, big)

    grads = [g_big[k] for k in _BIG]
    from_sibling = _pair_exchange(grads)
    psums = [_pair_sum(f"pair_sum_{k}", g, r, core, _half_axis(k)) for k, g, r in zip(_BIG, grads, from_sibling)]
    landed = _chip_exchange(psums)
    halves = [_chip_sum(f"chip_sum_{k}", p, l, pos, _SHARD_AXIS[k]) for k, p, l in zip(_BIG, psums, landed)]
    reduced = dict(zip(_BIG, _sibling_fill(halves)))

    packed = _all_reduce_small(_pack([g_small[k] for k in _SMALL] + [loss]))
    small_like = [w[k] for k in _SMALL]
    g_red = dict(zip(_SMALL, _unpack(packed, small_like)))
    loss_row = packed.shape[0] - 8
    loss_out = packed[loss_row, 0]

    grad, delta, new_m, new_v = {}, {}, {}, {}
    for k in _BIG:
        shp = w[k].shape
        two = (shp[0] * shp[1], shp[2])
        d, nm, nv = _adamw(f"adamw_{k}", w[k].reshape(two), reduced[k].reshape(two), m[k].reshape(two), v[k].reshape(two))
        grad[k], delta[k], new_m[k], new_v[k] = reduced[k], d.reshape(shp), nm.reshape(shp), nv.reshape(shp)
    d, nm, nv = _adamw("adamw_small", _pack(small_like), packed[:loss_row], _pack([m[k] for k in _SMALL]),
                       _pack([v[k] for k in _SMALL]))
    for k, dk, mk, vk in zip(_SMALL, _unpack(d, small_like), _unpack(nm, small_like), _unpack(nv, small_like)):
        grad[k], delta[k], new_m[k], new_v[k] = g_red[k], dk, mk, vk

    order = ("mix_norm", "ffn_norm", "w_ffn_in", "w_ffn_out", "ab_w_in", "ab_gn_gain", "ab_w_pool", "ab_pool_scale",
             "ab_w_out", "c_w_qkv", "c_rel_bias", "c_w_out", "final_norm")
    return (loss_out, grad_x.reshape(x.shape), *[grad[k] for k in order], *[delta[k] for k in order],
            *[new_m[k] for k in order], *[new_v[k] for k in order])
```

```python
import functools
import math

import numpy as np
import jax
import jax.numpy as jnp
from jax import lax
from jax.experimental import pallas as pl
from jax.experimental.pallas import tpu as pltpu

F32 = jnp.float32
BF16 = jnp.bfloat16

D_MODEL = 1024
D_FF = 4096
DEPTH = 4
CHUNK = 64
RMS_EPS = 1e-6
RET_WIDTH = 512
RET_HEADS = 4
RET_HEAD_DIM = 128
RET_ROPE_BASE = 10000.0
GN_EPS = 1e-5
POOL_WIDTH = 512
POOL_WINDOWS = (2, 4, 8, 16)
POOL_GROUP_DIM = 128
POOL_HALO = 16
AB_IN_WIDTH = 2560
ATT_HEADS = 16
ATT_HEAD_DIM = 64
LEFT_CHUNKS = 8
BAND = (LEFT_CHUNKS + 1) * CHUNK
REL_CLIP = 128
N_REL = 2 * REL_CLIP + 1
N_REL_PAD = 264
NEG_INF = -1e30
KSCALE = RET_HEAD_DIM ** -0.5
QSCALE = ATT_HEAD_DIM ** -0.5

ADAM_LR = 0.001
ADAM_B1 = 0.9
ADAM_B2 = 0.999
ADAM_EPS = 1e-08
ADAM_WD = 0.01
ADAM_STEP = 10

ATT_BLOCK = LEFT_CHUNKS * CHUNK
RET_BLOCK = 512
N_CHIPS = 4
N_DEV = 8
LANE = 128
VMEM_LIMIT = 52 * 1024 * 1024
MESH = pl.DeviceIdType.MESH


def _cp(sem, vmem=VMEM_LIMIT):
    return pltpu.CompilerParams(dimension_semantics=sem, vmem_limit_bytes=vmem)


def _dot(a, b):
    return lax.dot_general(a, b, (((1,), (0,)), ((), ())), preferred_element_type=F32)


def _dot_nt(a, b):
    return lax.dot_general(a, b, (((1,), (1,)), ((), ())), preferred_element_type=F32)


def _dot_tn(a, b):
    return lax.dot_general(a, b, (((0,), (0,)), ((), ())), preferred_element_type=F32)


def _mm(name, mode, a, b, *, la=None, lb=None, tm=1024, tn=1024, tk=1024, a_fn=None, b_fn=None,
        extras=(), epi=None, out_dtype=F32, stack=None):
    a2, b2 = a.shape[-2:], b.shape[-2:]
    if mode == "nn":
        (M, K), (K2, N) = a2, b2
    elif mode == "nt":
        (M, K), (N, K2) = a2, b2
    else:
        (K, M), (K2, N) = a2, b2
    assert K == K2, (name, a.shape, b.shape)
    tm, tn, tk = min(tm, M), min(tn, N), min(tk, K)
    assert M % tm == 0 and N % tn == 0 and K % tk == 0, (name, M, N, K, tm, tn, tk)
    gm, gn, gk = M // tm, N // tn, K // tk

    def spec(block, idx, lead):
        if lead is None:
            return pl.BlockSpec(block, idx)
        return pl.BlockSpec((None,) + block, lambda i, j, k: (lead,) + idx(i, j, k))

    if mode == "nn":
        a_spec = spec((tm, tk), lambda i, j, k: (i, k), la)
        b_spec = spec((tk, tn), lambda i, j, k: (k, j), lb)
        dot = _dot
    elif mode == "nt":
        a_spec = spec((tm, tk), lambda i, j, k: (i, k), la)
        b_spec = spec((tn, tk), lambda i, j, k: (j, k), lb)
        dot = _dot_nt
    else:
        a_spec = spec((tk, tm), lambda i, j, k: (k, i), la)
        b_spec = spec((tk, tn), lambda i, j, k: (k, j), lb)
        dot = _dot_tn
    ex_specs = [pl.BlockSpec((tm, tn), lambda i, j, k: (i, j)) for _ in extras]
    n_ex = len(extras)

    def body(*refs):
        a_ref, b_ref = refs[0], refs[1]
        ex_refs = refs[2:2 + n_ex]
        o_ref = refs[2 + n_ex]
        av = a_ref[...]
        if a_fn is not None:
            av = a_fn(av)
        bv = b_ref[...]
        if b_fn is not None:
            bv = b_fn(bv)
        part = dot(av.astype(BF16), bv.astype(BF16))

        def finish(acc):
            if epi is not None:
                acc = epi(acc, *[r[...] for r in ex_refs])
            o_ref[...] = acc.astype(o_ref.dtype)

        if gk == 1:
            finish(part)
        else:
            acc_ref = refs[3 + n_ex]
            k = pl.program_id(2)

            @pl.when(k == 0)
            def _():
                acc_ref[...] = part

            @pl.when(k > 0)
            def _():
                acc_ref[...] += part

            @pl.when(k == gk - 1)
            def _():
                finish(acc_ref[...])

    operands = [a, b, *extras]
    in_specs = [a_spec, b_spec] + ex_specs
    aliases = {}
    if stack is None:
        out_spec = pl.BlockSpec((tm, tn), lambda i, j, k: (i, j))
        out_shape = jax.ShapeDtypeStruct((M, N), out_dtype)
    else:
        n_layers, layer, prev = stack
        out_spec = pl.BlockSpec((None, tm, tn), lambda i, j, k: (layer, i, j))
        out_shape = jax.ShapeDtypeStruct((n_layers, M, N), out_dtype)
        if prev is not None:
            aliases = {len(operands): 0}
            operands.append(prev)
            in_specs.append(_ANY)

    def call_body(*refs):
        if aliases:
            refs = refs[:2 + n_ex] + refs[3 + n_ex:]
        body(*refs)

    return pl.pallas_call(
        call_body,
        name=name,
        grid=(gm, gn, gk),
        in_specs=in_specs,
        out_specs=out_spec,
        out_shape=out_shape,
        input_output_aliases=aliases,
        scratch_shapes=[pltpu.VMEM((tm, tn), F32)] if gk > 1 else [],
        compiler_params=_cp(("parallel", "parallel", "arbitrary")),
    )(*operands)


def _relu2(u):
    r = jnp.maximum(u, 0)
    return r * r


def _rms_fwd(name, x, g):
    S, D = x.shape
    tq = min(1024, S)

    def body(x_ref, g_ref, o_ref):
        xv = x_ref[...]
        r = lax.rsqrt(jnp.mean(xv * xv, axis=-1, keepdims=True) + RMS_EPS)
        o_ref[...] = ((xv * r) * g_ref[...]).astype(o_ref.dtype)

    return pl.pallas_call(
        body,
        name=name,
        grid=(S // tq,),
        in_specs=[pl.BlockSpec((tq, D), lambda i: (i, 0)), pl.BlockSpec((1, D), lambda i: (0, 0))],
        out_specs=pl.BlockSpec((tq, D), lambda i: (i, 0)),
        out_shape=jax.ShapeDtypeStruct((S, D), BF16),
        compiler_params=_cp(("parallel",)),
    )(x, g)


def _rms_bwd(name, x, g, dh, dres):
    S, D = x.shape
    tq = min(512, S)
    n = S // tq

    def body(x_ref, g_ref, dh_ref, dres_ref, dx_ref, dg_ref, acc_ref):
        i = pl.program_id(0)
        xv = x_ref[...]
        r = lax.rsqrt(jnp.mean(xv * xv, axis=-1, keepdims=True) + RMS_EPS)
        xh = xv * r
        dh_v = dh_ref[...].astype(F32)
        dxh = dh_v * g_ref[...]
        dx = r * (dxh - xh * jnp.mean(dxh * xh, axis=-1, keepdims=True))
        dx_ref[...] = dres_ref[...] + dx
        part = jnp.sum((dh_v * xh).reshape(tq // 8, 8, D), axis=0)

        @pl.when(i == 0)
        def _():
            acc_ref[...] = part

        @pl.when(i > 0)
        def _():
            acc_ref[...] += part

        @pl.when(i == n - 1)
        def _():
            dg_ref[...] = jnp.sum(acc_ref[...], axis=0, keepdims=True)

    return pl.pallas_call(
        body,
        name=name,
        grid=(n,),
        in_specs=[pl.BlockSpec((tq, D), lambda i: (i, 0)), pl.BlockSpec((1, D), lambda i: (0, 0)),
                  pl.BlockSpec((tq, D), lambda i: (i, 0)), pl.BlockSpec((tq, D), lambda i: (i, 0))],
        out_specs=[pl.BlockSpec((tq, D), lambda i: (i, 0)), pl.BlockSpec((1, D), lambda i: (0, 0))],
        out_shape=[jax.ShapeDtypeStruct((S, D), F32), jax.ShapeDtypeStruct((1, D), F32)],
        scratch_shapes=[pltpu.VMEM((8, D), F32)],
        compiler_params=_cp(("arbitrary",)),
    )(x, g, dh, dres)


def _loss_head(x, g, t):
    S, D = x.shape
    tq = min(512, S)
    n = S // tq

    def body(x_ref, g_ref, t_ref, loss_ref, dx_ref, dg_ref, lacc_ref, gacc_ref):
        i = pl.program_id(0)
        xv = x_ref[...]
        gv = g_ref[...]
        r = lax.rsqrt(jnp.mean(xv * xv, axis=-1, keepdims=True) + RMS_EPS)
        xh = xv * r
        e = xh * gv - t_ref[...]
        dy = e * (1.0 / D)
        dxh = dy * gv
        dx_ref[...] = r * (dxh - xh * jnp.mean(dxh * xh, axis=-1, keepdims=True))
        lpart = jnp.sum((e * e).reshape(tq // 8, 8, D), axis=0)
        gpart = jnp.sum((dy * xh).reshape(tq // 8, 8, D), axis=0)

        @pl.when(i == 0)
        def _():
            lacc_ref[...] = lpart
            gacc_ref[...] = gpart

        @pl.when(i > 0)
        def _():
            lacc_ref[...] += lpart
            gacc_ref[...] += gpart

        @pl.when(i == n - 1)
        def _():
            dg_ref[...] = jnp.sum(gacc_ref[...], axis=0, keepdims=True)
            tot = jnp.sum(jnp.sum(lacc_ref[...], axis=0, keepdims=True), axis=1, keepdims=True)
            loss_ref[...] = jnp.broadcast_to(tot * (0.5 / D), (1, LANE))

    return pl.pallas_call(
        body,
        name="loss_head",
        grid=(n,),
        in_specs=[pl.BlockSpec((tq, D), lambda i: (i, 0)), pl.BlockSpec((1, D), lambda i: (0, 0)),
                  pl.BlockSpec((tq, D), lambda i: (i, 0))],
        out_specs=[pl.BlockSpec((1, LANE), lambda i: (0, 0)), pl.BlockSpec((tq, D), lambda i: (i, 0)),
                   pl.BlockSpec((1, D), lambda i: (0, 0))],
        out_shape=[jax.ShapeDtypeStruct((1, LANE), F32), jax.ShapeDtypeStruct((S, D), F32),
                   jax.ShapeDtypeStruct((1, D), F32)],
        scratch_shapes=[pltpu.VMEM((8, D), F32), pltpu.VMEM((8, D), F32)],
        compiler_params=_cp(("arbitrary",)),
    )(x, g, t)


def _ret_tables(S):
    T = min(RET_BLOCK, S)
    inv_freq = 1.0 / (RET_ROPE_BASE ** jnp.linspace(0.0, 1.0, RET_HEAD_DIM // 2, dtype=F32))
    ang = jnp.arange(S, dtype=F32)[:, None] * inv_freq[None, :]
    cos, sin = jnp.cos(ang), jnp.sin(ang)
    cosf = jnp.repeat(cos, 2, axis=-1)
    sins = jnp.stack([-sin, sin], axis=-1).reshape(S, RET_HEAD_DIM)
    log_g = np.log1p(-np.power(2.0, -5.0 - np.arange(RET_HEADS, dtype=np.float64)))
    pos = np.arange(T, dtype=np.float64)
    diff = pos[:, None] - pos[None, :]
    same = (pos[:, None] // CHUNK) == (pos[None, :] // CHUNK)
    seen = same | (diff > 0)
    dmat = np.where(seen[None], np.exp(np.abs(diff)[None] * log_g[:, None, None]), 0.0)
    aq = np.exp((pos[None, :] + 1.0) * log_g[:, None])
    ak = np.exp((T - 1.0 - pos[None, :]) * log_g[:, None])
    lam = np.exp(T * log_g)
    bc = lambda v: jnp.asarray(np.broadcast_to(v[..., None], v.shape + (LANE,)), F32)
    return dict(cos=cosf, sin=sins, dmat=jnp.asarray(dmat, F32), aq=bc(aq), ak=bc(ak),
                lam=jnp.asarray(np.broadcast_to(lam[:, None, None], (RET_HEADS, 1, LANE)), F32))


def _rot(x, cos, sin_s, even):
    sw = jnp.where(even, pltpu.roll(x, LANE - 1, 1), pltpu.roll(x, 1, 1))
    return x * cos + sw * sin_s


def _rot_t(dy, cos, sin_s, even):
    t = dy * sin_s
    return dy * cos + jnp.where(even, pltpu.roll(t, LANE - 1, 1), pltpu.roll(t, 1, 1))


def _ret_specs(T, rev_nb=None):
    blk = (lambda b: b) if rev_nb is None else (lambda b: rev_nb - 1 - b)
    hd = RET_HEADS
    z_specs = [pl.BlockSpec((T, LANE), functools.partial(lambda h, b, o: (blk(b), o * hd + h), o=o))
               for o in range(4)]
    tab = [pl.BlockSpec((T, LANE), lambda h, b: (blk(b), 0)),
           pl.BlockSpec((T, LANE), lambda h, b: (blk(b), 0)),
           pl.BlockSpec((None, T, T), lambda h, b: (h, 0, 0)),
           pl.BlockSpec((None, T, LANE), lambda h, b: (h, 0, 0)),
           pl.BlockSpec((None, T, LANE), lambda h, b: (h, 0, 0)),
           pl.BlockSpec((None, 1, LANE), lambda h, b: (h, 0, 0)),
           pl.BlockSpec((1, LANE), lambda h, b: (0, h))]
    return z_specs + tab, blk


def _ret_fwd(name, z, tb, gain):
    S = z.shape[0]
    T = min(RET_BLOCK, S)
    nb = S // T
    specs, blk = _ret_specs(T)

    def body(zq, zk, zv, zg, cos_r, sin_r, d_r, aq_r, ak_r, lam_r, gain_r, ret_o, opre_o, st_o, state):
        b = pl.program_id(1)

        @pl.when(b == 0)
        def _():
            state[...] = jnp.zeros_like(state)

        even = (lax.broadcasted_iota(jnp.int32, (T, LANE), 1) & 1) == 0
        c, s = cos_r[...], sin_r[...]
        q = _rot(zq[...], c, s, even)
        k = _rot(zk[...], c, s, even) * KSCALE
        qb, kb, vb = q.astype(BF16), k.astype(BF16), zv[...].astype(BF16)
        p = (_dot_nt(qb, kb) * d_r[...]).astype(BF16)
        st = state[...]
        st_o[...] = st
        o = _dot(p, vb) + _dot((q * aq_r[...]).astype(BF16), st.astype(BF16))
        state[...] = st * lam_r[...] + _dot_tn((k * ak_r[...]).astype(BF16), vb)
        opre_o[...] = o
        mu = jnp.mean(o, axis=-1, keepdims=True)
        d = o - mu
        y = d * lax.rsqrt(jnp.mean(d * d, axis=-1, keepdims=True) + GN_EPS)
        g = zg[...]
        ret_o[...] = ((g * jax.nn.sigmoid(g)) * (y * gain_r[...])).astype(ret_o.dtype)

    out_blk = pl.BlockSpec((T, LANE), lambda h, b: (b, h))
    return pl.pallas_call(
        body,
        name=name,
        grid=(RET_HEADS, nb),
        in_specs=specs,
        out_specs=[out_blk, out_blk, pl.BlockSpec((None, None, LANE, LANE), lambda h, b: (h, b, 0, 0))],
        out_shape=[jax.ShapeDtypeStruct((S, RET_WIDTH), BF16), jax.ShapeDtypeStruct((S, RET_WIDTH), F32),
                   jax.ShapeDtypeStruct((RET_HEADS, nb, LANE, LANE), F32)],
        scratch_shapes=[pltpu.VMEM((LANE, LANE), F32)],
        compiler_params=_cp(("parallel", "arbitrary")),
    )(z, z, z, z, tb["cos"], tb["sin"], tb["dmat"], tb["aq"], tb["ak"], tb["lam"], gain)


def _ret_bwd(name, z, tb, gain, opre, states, dcat):
    S = z.shape[0]
    T = min(RET_BLOCK, S)
    nb = S // T
    specs, blk = _ret_specs(T, rev_nb=nb)
    tok = pl.BlockSpec((T, LANE), lambda h, b: (blk(b), h))

    def body(zq, zk, zv, zg, cos_r, sin_r, d_r, aq_r, ak_r, lam_r, gain_r, opre_r, st_r, dret_r,
             dq_o, dk_o, dv_o, dg_o, dgain_o, dstate):
        b = pl.program_id(1)

        @pl.when(b == 0)
        def _():
            dstate[...] = jnp.zeros_like(dstate)
            dgain_o[...] = jnp.zeros_like(dgain_o)

        even = (lax.broadcasted_iota(jnp.int32, (T, LANE), 1) & 1) == 0
        c, s = cos_r[...], sin_r[...]
        aq, ak, dm = aq_r[...], ak_r[...], d_r[...]
        q = _rot(zq[...], c, s, even)
        k = _rot(zk[...], c, s, even) * KSCALE
        qb, kb, vb = q.astype(BF16), k.astype(BF16), zv[...].astype(BF16)
        pb = (_dot_nt(qb, kb) * dm).astype(BF16)
        g = zg[...]
        sig = jax.nn.sigmoid(g)
        o = opre_r[...]
        mu = jnp.mean(o, axis=-1, keepdims=True)
        d = o - mu
        rstd = lax.rsqrt(jnp.mean(d * d, axis=-1, keepdims=True) + GN_EPS)
        y = d * rstd
        gain_v = gain_r[...]
        dret = dret_r[...].astype(F32)
        dyg = dret * (g * sig)
        dg_o[...] = (dret * (y * gain_v) * (sig * (1.0 + g * (1.0 - sig)))).astype(dg_o.dtype)
        dgain_o[...] += jnp.sum(dyg * y, axis=0, keepdims=True)
        dy = dyg * gain_v
        do = rstd * (dy - jnp.mean(dy, axis=-1, keepdims=True) - y * jnp.mean(dy * y, axis=-1, keepdims=True))
        dob = do.astype(BF16)
        stb = st_r[...].astype(BF16)
        dsn = dstate[...]
        dsnb = dsn.astype(BF16)
        dpb = (_dot_nt(dob, vb) * dm).astype(BF16)
        dq = _dot(dpb, kb) + _dot_nt(dob, stb) * aq
        dk = _dot_tn(dpb, qb) + _dot_nt(vb, dsnb) * ak
        dv = _dot_tn(pb, dob) + _dot((k * ak).astype(BF16), dsnb)
        dstate[...] = dsn * lam_r[...] + _dot_tn((q * aq).astype(BF16), dob)
        dq_o[...] = _rot_t(dq, c, s, even).astype(dq_o.dtype)
        dk_o[...] = _rot_t(dk * KSCALE, c, s, even).astype(dk_o.dtype)
        dv_o[...] = dv.astype(dv_o.dtype)

    part = jax.ShapeDtypeStruct((S, RET_WIDTH), BF16)
    return pl.pallas_call(
        body,
        name=name,
        grid=(RET_HEADS, nb),
        in_specs=specs + [tok, pl.BlockSpec((None, None, LANE, LANE), lambda h, b: (h, blk(b), 0, 0)), tok],
        out_specs=[tok, tok, tok, tok, pl.BlockSpec((1, LANE), lambda h, b: (0, h))],
        out_shape=[part, part, part, part, jax.ShapeDtypeStruct((1, RET_WIDTH), F32)],
        scratch_shapes=[pltpu.VMEM((LANE, LANE), F32)],
        compiler_params=_cp(("parallel", "arbitrary")),
    )(z, z, z, z, tb["cos"], tb["sin"], tb["dmat"], tb["aq"], tb["ak"], tb["lam"], gain, opre, states, dcat)


def _pool_counts(t0, rows):
    t = t0 + lax.broadcasted_iota(jnp.int32, (rows, POOL_WIDTH), 0)
    grp = lax.broadcasted_iota(jnp.int32, (rows, POOL_WIDTH), 1) >> 7
    win = jnp.where(grp == 0, POOL_WINDOWS[0], jnp.where(grp == 1, POOL_WINDOWS[1],
                    jnp.where(grp == 2, POOL_WINDOWS[2], POOL_WINDOWS[3])))
    return jnp.maximum(jnp.minimum(t + 1, win), 1).astype(F32), grp


def _window_sums(ext, grp, sign):
    n = ext.shape[0]
    sh = lambda v, k: pltpu.roll(v, k % n if sign > 0 else (n - k) % n, 0)
    s2 = ext + sh(ext, 1)
    s4 = s2 + sh(s2, 2)
    s8 = s4 + sh(s4, 4)
    s16 = s8 + sh(s8, 8)
    return jnp.where(grp == 0, s2, jnp.where(grp == 1, s4, jnp.where(grp == 2, s8, s16)))


def _pool_fwd(name, z, w_pool, scale):
    S = z.shape[0]
    T = min(512, S)
    nb = S // T
    pcol = AB_IN_WIDTH // POOL_WIDTH - 1
    hb = T // POOL_HALO

    def body(p_ref, halo_ref, w_ref, sc_ref, out_ref, pooled_ref):
        b = pl.program_id(0)
        cur = p_ref[...]
        halo = jnp.where(b > 0, halo_ref[...], 0.0)
        ext = jnp.concatenate([halo, cur], axis=0)
        cnt, grp = _pool_counts(b * T - POOL_HALO, T + POOL_HALO)
        sums = _window_sums(ext, grp, +1)
        pooled = (sums / cnt)[POOL_HALO:] - cur
        pb = pooled.astype(BF16)
        pooled_ref[...] = pb
        for gi in range(len(POOL_WINDOWS)):
            cs = slice(gi * POOL_GROUP_DIM, (gi + 1) * POOL_GROUP_DIM)
            mixed = _dot(pb[:, cs], w_ref[gi].astype(BF16))
            out_ref[:, cs] = (mixed * sc_ref[:, cs]).astype(out_ref.dtype)

    outs = jax.ShapeDtypeStruct((S, POOL_WIDTH), BF16)
    return pl.pallas_call(
        body,
        name=name,
        grid=(nb,),
        in_specs=[pl.BlockSpec((T, POOL_WIDTH), lambda b: (b, pcol)),
                  pl.BlockSpec((POOL_HALO, POOL_WIDTH), lambda b: (jnp.maximum(b * hb - 1, 0), pcol)),
                  pl.BlockSpec((4, POOL_GROUP_DIM, POOL_GROUP_DIM), lambda b: (0, 0, 0)),
                  pl.BlockSpec((1, POOL_WIDTH), lambda b: (0, 0))],
        out_specs=[pl.BlockSpec((T, POOL_WIDTH), lambda b: (b, 0)), pl.BlockSpec((T, POOL_WIDTH), lambda b: (b, 0))],
        out_shape=[outs, outs],
        compiler_params=_cp(("parallel",)),
    )(z, z, w_pool, scale)


def _pool_bwd(name, pooled, w_pool, scale, dcat):
    S = pooled.shape[0]
    T = min(512, S)
    nb = S // T
    hb = T // POOL_HALO
    last_h = S // POOL_HALO - 1

    def body(d_ref, dn_ref, pooled_ref, w_ref, sc_ref, dp_ref, dw_ref, dsc_ref):
        b = pl.program_id(0)

        @pl.when(b == 0)
        def _():
            dw_ref[...] = jnp.zeros_like(dw_ref)
            dsc_ref[...] = jnp.zeros_like(dsc_ref)

        sc = sc_ref[...]
        dout = d_ref[...].astype(F32)
        dnext = jnp.where(b < nb - 1, dn_ref[...].astype(F32), 0.0)
        dmix = jnp.concatenate([dout, dnext], axis=0) * sc
        dmb = dmix.astype(BF16)
        pb = pooled_ref[...]
        dpooled = []
        for gi in range(len(POOL_WINDOWS)):
            cs = slice(gi * POOL_GROUP_DIM, (gi + 1) * POOL_GROUP_DIM)
            wb = w_ref[gi].astype(BF16)
            dpooled.append(_dot_nt(dmb[:, cs], wb))
            dw_ref[gi] += _dot_tn(pb[:, cs], dmb[:T, cs])
            mixed = _dot(pb[:, cs], wb)
            dsc_ref[:, cs] += jnp.sum(dout[:, cs] * mixed, axis=0, keepdims=True)
        dpl = jnp.concatenate(dpooled, axis=1)
        cnt, grp = _pool_counts(b * T, T + POOL_HALO)
        sums = _window_sums(dpl / cnt, grp, -1)
        dp_ref[...] = (sums[:T] - dpl[:T]).astype(dp_ref.dtype)

    return pl.pallas_call(
        body,
        name=name,
        grid=(nb,),
        in_specs=[pl.BlockSpec((T, POOL_WIDTH), lambda b: (b, 1)),
                  pl.BlockSpec((POOL_HALO, POOL_WIDTH), lambda b: (jnp.minimum((b + 1) * hb, last_h), 1)),
                  pl.BlockSpec((T, POOL_WIDTH), lambda b: (b, 0)),
                  pl.BlockSpec((4, POOL_GROUP_DIM, POOL_GROUP_DIM), lambda b: (0, 0, 0)),
                  pl.BlockSpec((1, POOL_WIDTH), lambda b: (0, 0))],
        out_specs=[pl.BlockSpec((T, POOL_WIDTH), lambda b: (b, 0)),
                   pl.BlockSpec((4, POOL_GROUP_DIM, POOL_GROUP_DIM), lambda b: (0, 0, 0)),
                   pl.BlockSpec((1, POOL_WIDTH), lambda b: (0, 0))],
        out_shape=[jax.ShapeDtypeStruct((S, POOL_WIDTH), BF16),
                   jax.ShapeDtypeStruct((4, POOL_GROUP_DIM, POOL_GROUP_DIM), F32),
                   jax.ShapeDtypeStruct((1, POOL_WIDTH), F32)],
        compiler_params=_cp(("arbitrary",)),
    )(dcat, dcat, pooled, w_pool, scale)


ATT_Q = 256
ATT_W = ATT_Q + LEFT_CHUNKS * CHUNK
_REL_COLS = ATT_Q * ATT_W
_REL_TILE = 8192


def _rel_index():
    n = np.arange(ATT_Q)[:, None]
    k = np.arange(ATT_W)[None, :]
    j = k - (n // CHUNK) * CHUNK
    rel = np.clip(n % CHUNK + LEFT_CHUNKS * CHUNK - j, -REL_CLIP, REL_CLIP) + REL_CLIP
    rel = np.where((j >= 0) & (j < BAND), rel, N_REL)
    return jnp.asarray(rel.reshape(1, _REL_COLS), jnp.int32)


def _bias_table(name, rel_bias, rel_idx):
    rb = jnp.concatenate([rel_bias, jnp.full((ATT_HEADS, 1), NEG_INF, F32),
                          jnp.zeros((ATT_HEADS, N_REL_PAD - N_REL - 1), F32)], axis=1)

    def body(rb_ref, idx_ref, o_ref):
        r = lax.broadcasted_iota(jnp.int32, (N_REL_PAD, _REL_TILE), 0)
        onehot = (r == idx_ref[...]).astype(F32)
        o_ref[...] = jnp.dot(rb_ref[...], onehot, precision=lax.Precision.HIGHEST, preferred_element_type=F32)

    return pl.pallas_call(
        body,
        name=name,
        grid=(_REL_COLS // _REL_TILE,),
        in_specs=[pl.BlockSpec((ATT_HEADS, N_REL_PAD), lambda i: (0, 0)), pl.BlockSpec((1, _REL_TILE), lambda i: (0, i))],
        out_specs=pl.BlockSpec((ATT_HEADS, _REL_TILE), lambda i: (0, i)),
        out_shape=jax.ShapeDtypeStruct((ATT_HEADS, _REL_COLS), F32),
        compiler_params=_cp(("parallel",)),
    )(rb, rel_idx)


def _bias_grad(name, ds_sum, rel_idx):
    n = _REL_COLS // _REL_TILE

    def body(ds_ref, idx_ref, o_ref):
        i = pl.program_id(0)
        r = lax.broadcasted_iota(jnp.int32, (N_REL_PAD, _REL_TILE), 0)
        onehot = (r == idx_ref[...]).astype(F32)
        part = lax.dot_general(ds_ref[...], onehot, (((1,), (1,)), ((), ())),
                               precision=lax.Precision.HIGHEST, preferred_element_type=F32)

        @pl.when(i == 0)
        def _():
            o_ref[...] = part

        @pl.when(i > 0)
        def _():
            o_ref[...] += part

    out = pl.pallas_call(
        body,
        name=name,
        grid=(n,),
        in_specs=[pl.BlockSpec((ATT_HEADS, _REL_TILE), lambda i: (0, i)), pl.BlockSpec((1, _REL_TILE), lambda i: (0, i))],
        out_specs=pl.BlockSpec((ATT_HEADS, N_REL_PAD), lambda i: (0, 0)),
        out_shape=jax.ShapeDtypeStruct((ATT_HEADS, N_REL_PAD), F32),
        compiler_params=_cp(("arbitrary",)),
    )(ds_sum, rel_idx)
    return out[:, :N_REL]


def _attn_unit(q_ref, kw_ref, bias_ref, e, u, first_block, col, lane):
    mine = (lane < ATT_HEAD_DIM) if e == 0 else (lane >= ATT_HEAD_DIM)
    qm = jnp.where(mine, q_ref[u * ATT_Q:(u + 1) * ATT_Q, :] * QSCALE, 0)
    kw = kw_ref[u * ATT_Q:u * ATT_Q + ATT_W, :]
    s = _dot_nt(qm, kw) + bias_ref[e]
    s = jnp.where(col >= first_block * (ATT_BLOCK - u * ATT_Q), s, NEG_INF)
    p = jnp.exp(s - jnp.max(s, axis=-1, keepdims=True))
    return p, 1.0 / jnp.sum(p, axis=-1, keepdims=True), qm, kw, mine


def _attn_in_specs(nb):
    T = ATT_BLOCK
    hp = ATT_HEADS // 2
    cur = lambda off: pl.BlockSpec((T, LANE), lambda h, b: (jnp.minimum(b, nb - 1), off + h))
    prev = lambda off: pl.BlockSpec((T, LANE), lambda h, b: (jnp.clip(b - 1, 0, nb - 1), off + h))
    return [cur(0), prev(hp), cur(hp), prev(2 * hp), cur(2 * hp),
            pl.BlockSpec((None, 2, ATT_Q, ATT_W), lambda h, b: (h, 0, 0, 0))]


def _attn_fwd(name, qkv, bias):
    S = qkv.shape[0]
    T = ATT_BLOCK
    nb = S // T

    def body(q_ref, kp_ref, kc_ref, vp_ref, vc_ref, bias_ref, o_ref, kw_ref, vw_ref):
        b = pl.program_id(1)
        kw_ref[0:T] = kp_ref[...]
        kw_ref[T:2 * T] = kc_ref[...]
        vw_ref[0:T] = vp_ref[...]
        vw_ref[T:2 * T] = vc_ref[...]
        col = lax.broadcasted_iota(jnp.int32, (ATT_Q, ATT_W), 1)
        lane = lax.broadcasted_iota(jnp.int32, (ATT_Q, LANE), 1)
        first = jnp.where(b == 0, 1, 0)
        for u in range(T // ATT_Q):
            vw = vw_ref[u * ATT_Q:u * ATT_Q + ATT_W, :]
            outs = []
            for e in range(2):
                p, inv, _, _, _ = _attn_unit(q_ref, kw_ref, bias_ref, e, u, first, col, lane)
                outs.append(_dot(p.astype(BF16), vw) * inv)
            o_ref[u * ATT_Q:(u + 1) * ATT_Q, :] = jnp.where(lane < ATT_HEAD_DIM, outs[0], outs[1]).astype(o_ref.dtype)

    return pl.pallas_call(
        body,
        name=name,
        grid=(ATT_HEADS // 2, nb),
        in_specs=_attn_in_specs(nb),
        out_specs=pl.BlockSpec((T, LANE), lambda h, b: (b, h)),
        out_shape=jax.ShapeDtypeStruct((S, D_MODEL), BF16),
        scratch_shapes=[pltpu.VMEM((2 * T, LANE), BF16), pltpu.VMEM((2 * T, LANE), BF16)],
        compiler_params=_cp(("parallel", "arbitrary")),
    )(qkv, qkv, qkv, qkv, qkv, bias)


def _attn_bwd(name, qkv, bias, do):
    S = qkv.shape[0]
    T = ATT_BLOCK
    nb = S // T

    def body(q_ref, kp_ref, kc_ref, vp_ref, vc_ref, bias_ref, do_ref,
             dq_ref, dk_ref, dv_ref, dbias_ref, kw_ref, vw_ref, dkw_ref, dvw_ref):
        b = pl.program_id(1)

        @pl.when(b == 0)
        def _():
            dbias_ref[...] = jnp.zeros_like(dbias_ref)
            dkw_ref[T:2 * T] = jnp.zeros((T, LANE), F32)
            dvw_ref[T:2 * T] = jnp.zeros((T, LANE), F32)

        dkw_ref[0:T] = dkw_ref[T:2 * T]
        dvw_ref[0:T] = dvw_ref[T:2 * T]
        dkw_ref[T:2 * T] = jnp.zeros((T, LANE), F32)
        dvw_ref[T:2 * T] = jnp.zeros((T, LANE), F32)

        @pl.when(b < nb)
        def _():
            kw_ref[0:T] = kp_ref[...]
            kw_ref[T:2 * T] = kc_ref[...]
            vw_ref[0:T] = vp_ref[...]
            vw_ref[T:2 * T] = vc_ref[...]
            col = lax.broadcasted_iota(jnp.int32, (ATT_Q, ATT_W), 1)
            lane = lax.broadcasted_iota(jnp.int32, (ATT_Q, LANE), 1)
            first = jnp.where(b == 0, 1, 0)
            for u in range(T // ATT_Q):
                rows = slice(u * ATT_Q, (u + 1) * ATT_Q)
                win = slice(u * ATT_Q, u * ATT_Q + ATT_W)
                vw = vw_ref[win, :]
                do2 = do_ref[rows, :]
                dqs, dk, dv = [], None, None
                for e in range(2):
                    p, inv, qm, kw, mine = _attn_unit(q_ref, kw_ref, bias_ref, e, u, first, col, lane)
                    dom = jnp.where(mine, do2, 0)
                    dp = _dot_nt(dom, vw)
                    delta = jnp.sum(p * dp, axis=-1, keepdims=True) * inv
                    ds = p * ((dp - delta) * inv)
                    dbias_ref[e] += ds
                    dsb = ds.astype(BF16)
                    dqs.append(_dot(dsb, kw))
                    dk_e = _dot_tn(dsb, qm)
                    dv_e = _dot_tn((p * inv).astype(BF16), dom)
                    dk = dk_e if dk is None else dk + dk_e
                    dv = dv_e if dv is None else dv + dv_e
                dq_ref[rows, :] = (jnp.where(lane < ATT_HEAD_DIM, dqs[0], dqs[1]) * QSCALE).astype(dq_ref.dtype)
                dkw_ref[win, :] += dk
                dvw_ref[win, :] += dv

        @pl.when(b > 0)
        def _():
            dk_ref[...] = dkw_ref[0:T].astype(dk_ref.dtype)
            dv_ref[...] = dvw_ref[0:T].astype(dv_ref.dtype)

    tok = jax.ShapeDtypeStruct((S, D_MODEL), BF16)
    prev_out = pl.BlockSpec((T, LANE), lambda h, b: (jnp.maximum(b - 1, 0), h))
    return pl.pallas_call(
        body,
        name=name,
        grid=(ATT_HEADS // 2, nb + 1),
        in_specs=_attn_in_specs(nb) + [pl.BlockSpec((T, LANE), lambda h, b: (jnp.minimum(b, nb - 1), h))],
        out_specs=[pl.BlockSpec((T, LANE), lambda h, b: (jnp.minimum(b, nb - 1), h)), prev_out, prev_out,
                   pl.BlockSpec((None, 2, ATT_Q, ATT_W), lambda h, b: (h, 0, 0, 0))],
        out_shape=[tok, tok, tok, jax.ShapeDtypeStruct((ATT_HEADS // 2, 2, ATT_Q, ATT_W), F32)],
        scratch_shapes=[pltpu.VMEM((2 * T, LANE), BF16), pltpu.VMEM((2 * T, LANE), BF16),
                        pltpu.VMEM((2 * T, LANE), F32), pltpu.VMEM((2 * T, LANE), F32)],
        compiler_params=_cp(("parallel", "arbitrary")),
    )(qkv, qkv, qkv, qkv, qkv, bias, do)


def _local_step(x, target, small, big):
    S = x.shape[0]
    tb = _ret_tables(S)
    rel_idx = _rel_index()
    saved = []
    for layer in range(DEPTH):
        i = layer // 2
        st = {"x_in": x}
        h = _rms_fwd(f"mix_norm_fwd{layer}", x, small["mix_norm"][layer:layer + 1])
        st["h"] = h
        if layer % 2 == 0:
            z = _mm(f"ab_in_fwd{layer}", "nn", h, big["ab_w_in"], lb=i, tn=640, out_dtype=F32)
            gain = small["ab_gn_gain"][i:i + 1]
            ret, opre, states = _ret_fwd(f"ret_fwd{layer}", z, tb, gain)
            pout, pooled = _pool_fwd(f"pool_fwd{layer}", z, small["ab_w_pool"][i], small["ab_pool_scale"][i:i + 1])
            cat = jnp.concatenate([ret, pout], axis=1)
            st.update(z=z, opre=opre, states=states, pooled=pooled, cat=cat)
            x = _mm(f"ab_out_fwd{layer}", "nn", cat, big["ab_w_out"], lb=i, extras=(x,), epi=lambda acc, r: acc + r)
        else:
            qkv = _mm(f"qkv_fwd{layer}", "nn", h, big["c_w_qkv"], lb=i, out_dtype=BF16)
            bias = _bias_table(f"bias_table{layer}", small["c_rel_bias"][i], rel_idx)
            bias = bias.reshape(ATT_HEADS // 2, 2, ATT_Q, ATT_W)
            att = _attn_fwd(f"attn_fwd{layer}", qkv, bias)
            st.update(qkv=qkv, bias=bias, att=att)
            x = _mm(f"c_out_fwd{layer}", "nn", att, big["c_w_out"], lb=i, extras=(x,), epi=lambda acc, r: acc + r)
        st["x_mid"] = x
        hn = _rms_fwd(f"ffn_norm_fwd{layer}", x, small["ffn_norm"][layer:layer + 1])
        u = _mm(f"ffn_in_fwd{layer}", "nn", hn, big["w_ffn_in"], lb=layer, out_dtype=BF16)
        x = _mm(f"ffn_out_fwd{layer}", "nn", u, big["w_ffn_out"], lb=layer, a_fn=_relu2, extras=(x,),
                epi=lambda acc, r: acc + r)
        st.update(hn=hn, u=u)
        saved.append(st)

    loss, dx, d_final = _loss_head(x, small["final_norm"].reshape(1, D_MODEL), target)

    gs = {k: [None] * v.shape[0] for k, v in small.items() if k != "final_norm"}
    gb = {k: None for k in big}

    def dw(name, key, idx, a, b, **kw):
        gb[key] = _mm(name, "tn", a, b, stack=(big[key].shape[0], idx, gb[key]), **kw)

    for layer in reversed(range(DEPTH)):
        i = layer // 2
        st = saved[layer]
        du = _mm(f"ffn_out_bwd{layer}", "nt", dx, big["w_ffn_out"], lb=layer, extras=(st["u"],),
                 epi=lambda acc, u: acc * (2.0 * jnp.maximum(u, 0).astype(F32)), out_dtype=BF16)
        dw(f"ffn_out_dw{layer}", "w_ffn_out", layer, st["u"], dx, a_fn=_relu2)
        dhn = _mm(f"ffn_in_bwd{layer}", "nt", du, big["w_ffn_in"], lb=layer)
        dw(f"ffn_in_dw{layer}", "w_ffn_in", layer, st["hn"], du)
        dx, gs["ffn_norm"][layer] = _rms_bwd(f"ffn_norm_bwd{layer}", st["x_mid"], small["ffn_norm"][layer:layer + 1], dhn, dx)
        if layer % 2 == 0:
            dcat = _mm(f"ab_out_bwd{layer}", "nt", dx, big["ab_w_out"], lb=i, out_dtype=BF16)
            dw(f"ab_out_dw{layer}", "ab_w_out", i, st["cat"], dx)
            gain = small["ab_gn_gain"][i:i + 1]
            dq, dk, dv, dg, gs["ab_gn_gain"][i] = _ret_bwd(f"ret_bwd{layer}", st["z"], tb, gain, st["opre"], st["states"], dcat)
            dp, gs["ab_w_pool"][i], gs["ab_pool_scale"][i] = _pool_bwd(
                f"pool_bwd{layer}", st["pooled"], small["ab_w_pool"][i], small["ab_pool_scale"][i:i + 1], dcat)
            dz = jnp.concatenate([dq, dk, dv, dg, dp], axis=1)
            dh = _mm(f"ab_in_bwd{layer}", "nt", dz, big["ab_w_in"], lb=i, tk=1280)
            dw(f"ab_in_dw{layer}", "ab_w_in", i, st["h"], dz, tn=640)
        else:
            datt = _mm(f"c_out_bwd{layer}", "nt", dx, big["c_w_out"], lb=i, out_dtype=BF16)
            dw(f"c_out_dw{layer}", "c_w_out", i, st["att"], dx)
            dq, dk, dv, dbias = _attn_bwd(f"attn_bwd{layer}", st["qkv"], st["bias"], datt)
            gs["c_rel_bias"][i] = _bias_grad(f"bias_grad{layer}", dbias.reshape(ATT_HEADS, _REL_COLS), rel_idx)
            dqkv = jnp.concatenate([dq, dk, dv], axis=1)
            dh = _mm(f"qkv_bwd{layer}", "nt", dqkv, big["c_w_qkv"], lb=i)
            dw(f"qkv_dw{layer}", "c_w_qkv", i, st["h"], dqkv)
        dx, gs["mix_norm"][layer] = _rms_bwd(f"mix_norm_bwd{layer}", st["x_in"], small["mix_norm"][layer:layer + 1], dh, dx)

    g_small = {
        "mix_norm": jnp.concatenate(gs["mix_norm"], axis=0),
        "ffn_norm": jnp.concatenate(gs["ffn_norm"], axis=0),
        "ab_gn_gain": jnp.concatenate(gs["ab_gn_gain"], axis=0),
        "ab_w_pool": jnp.stack(gs["ab_w_pool"], axis=0),
        "ab_pool_scale": jnp.concatenate(gs["ab_pool_scale"], axis=0),
        "c_rel_bias": jnp.stack(gs["c_rel_bias"], axis=0),
        "final_norm": d_final.reshape(D_MODEL),
    }
    return loss, dx, g_small, gb


_BIG = ("w_ffn_in", "w_ffn_out", "ab_w_in", "ab_w_out", "c_w_qkv", "c_w_out")
_SHARD_AXIS = {"w_ffn_in": 2, "w_ffn_out": 1, "ab_w_in": 2, "ab_w_out": 1, "c_w_qkv": 2, "c_w_out": 1}
_SMALL = ("mix_norm", "ffn_norm", "ab_gn_gain", "ab_w_pool", "ab_pool_scale", "c_rel_bias", "final_norm")
_ANY = pl.BlockSpec(memory_space=pl.ANY)


def _place():
    x, y, c = lax.axis_index("x"), lax.axis_index("y"), lax.axis_index("c")
    chips = [(1 - x, y), (x, 1 - y), (1 - x, 1 - y)]
    return x, y, c, chips


def _sub(ref, axis, start, size):
    idx = [slice(None)] * 3
    idx[axis] = pl.ds(pl.multiple_of(start, LANE), size)
    return ref.at[tuple(idx)]


def _gather_weights(shards):
    n = len(_BIG)
    axes = [_SHARD_AXIS[k] for k in _BIG]
    sizes = [s.shape[a] for s, a in zip(shards, axes)]

    haxes = [_half_axis(k) for k in _BIG]
    hsizes = [s.shape[a] // 2 for s, a in zip(shards, haxes)]

    def body(*refs):
        ins, outs = refs[:n], refs[n:2 * n]
        send_sems, recv_sems, local_sems = refs[2 * n:]
        x, y, c, chips = _place()
        me = 2 * x + y
        sibling = (x, y, 1 - c)

        def remote(src, dst, s, to):
            return pltpu.make_async_remote_copy(src_ref=src, dst_ref=dst, send_sem=send_sems.at[s],
                                                recv_sem=recv_sems.at[s], device_id=to, device_id_type=MESH)

        def half(ref, w, chip, core):
            return _sub(_sub(ref, axes[w], chip * sizes[w], sizes[w]), haxes[w], core * hsizes[w], hsizes[w])

        first, passed, locs = [], [], []
        for w in range(n):
            loc = pltpu.make_async_copy(ins[w], _sub(outs[w], axes[w], me * sizes[w], sizes[w]), local_sems.at[w])
            loc.start()
            locs.append(loc)
            for k, (px, py) in enumerate(chips):
                cp = remote(_sub(ins[w], haxes[w], c * hsizes[w], hsizes[w]), half(outs[w], w, me, c), w * 6 + k, (px, py, c))
                cp.start()
                first.append(cp)
        for w in range(n):
            for k, (px, py) in enumerate(chips):
                landed = half(outs[w], w, 2 * px + py, c)
                remote(landed, landed, w * 6 + k, (px, py, c)).wait_recv()
                cp = remote(landed, landed, w * 6 + 3 + k, sibling)
                cp.start()
                passed.append(cp)
        for w in range(n):
            for k, (px, py) in enumerate(chips):
                theirs = half(outs[w], w, 2 * px + py, 1 - c)
                remote(theirs, theirs, w * 6 + 3 + k, sibling).wait_recv()
        for cp in first + passed:
            cp.wait_send()
        for loc in locs:
            loc.wait()

    def full(s, a):
        shape = list(s.shape)
        shape[a] *= N_CHIPS
        return jax.ShapeDtypeStruct(tuple(shape), s.dtype)

    return pl.pallas_call(
        body,
        name="gather_weights",
        in_specs=[_ANY] * n,
        out_specs=[_ANY] * n,
        out_shape=[full(s, a) for s, a in zip(shards, axes)],
        scratch_shapes=[pltpu.SemaphoreType.DMA((6 * n,)), pltpu.SemaphoreType.DMA((6 * n,)), pltpu.SemaphoreType.DMA((n,))],
        compiler_params=pltpu.CompilerParams(has_side_effects=True),
    )(*shards)


def _half_axis(name):
    return 3 - _SHARD_AXIS[name]


def _pair_exchange(grads):
    n = len(_BIG)
    haxes = [_half_axis(k) for k in _BIG]
    halves = [g.shape[a] // 2 for g, a in zip(grads, haxes)]

    def body(*refs):
        ins, outs = refs[:n], refs[n:2 * n]
        send_sems, recv_sems = refs[2 * n:]
        x, y, c, _ = _place()
        cps = []
        for w in range(n):
            src = _sub(ins[w], haxes[w], (1 - c) * halves[w], halves[w])
            cp = pltpu.make_async_remote_copy(src_ref=src, dst_ref=outs[w], send_sem=send_sems.at[w],
                                              recv_sem=recv_sems.at[w], device_id=(x, y, 1 - c), device_id_type=MESH)
            cp.start()
            cps.append(cp)
        for cp in cps:
            cp.wait_recv()
        for cp in cps:
            cp.wait_send()

    def half(g, a):
        shape = list(g.shape)
        shape[a] //= 2
        return jax.ShapeDtypeStruct(tuple(shape), g.dtype)

    return pl.pallas_call(
        body,
        name="pair_exchange",
        in_specs=[_ANY] * n,
        out_specs=[_ANY] * n,
        out_shape=[half(g, a) for g, a in zip(grads, haxes)],
        scratch_shapes=[pltpu.SemaphoreType.DMA((n,)), pltpu.SemaphoreType.DMA((n,))],
        compiler_params=pltpu.CompilerParams(has_side_effects=True),
    )(*grads)


def _rows_tile(rows, cols):
    tr = rows
    while tr * cols > (1 << 19) and tr % 16 == 0:
        tr //= 2
    return tr


def _pair_sum(name, g, recv, c_arr, haxis):
    L, R, C = recv.shape
    tr = _rows_tile(R, C)
    nr = R // tr
    if haxis == 1:
        g_idx = lambda l, i, s: (l, s[0] * nr + i, 0)
    else:
        g_idx = lambda l, i, s: (l, i, s[0])

    def body(s_ref, g_ref, r_ref, o_ref):
        o_ref[...] = (g_ref[...] + r_ref[...]).astype(o_ref.dtype)

    return pl.pallas_call(
        body,
        name=name,
        grid_spec=pltpu.PrefetchScalarGridSpec(
            num_scalar_prefetch=1,
            grid=(L, nr),
            in_specs=[pl.BlockSpec((None, tr, C), g_idx), pl.BlockSpec((None, tr, C), lambda l, i, s: (l, i, 0))],
            out_specs=pl.BlockSpec((None, tr, C), lambda l, i, s: (l, i, 0)),
        ),
        out_shape=jax.ShapeDtypeStruct((L, R, C), BF16),
        compiler_params=_cp(("parallel", "parallel")),
    )(c_arr, g, recv)


def _chip_exchange(psums):
    n = len(_BIG)
    axes = [_SHARD_AXIS[k] for k in _BIG]
    sizes = [p.shape[a] // N_CHIPS for p, a in zip(psums, axes)]

    def body(*refs):
        ins, outs = refs[:n], refs[n:2 * n]
        send_sems, recv_sems = refs[2 * n:]
        x, y, c, chips = _place()
        cps = []
        for w in range(n):
            for k, (px, py) in enumerate(chips):
                s = w * 3 + k
                src = _sub(ins[w], axes[w], (2 * px + py) * sizes[w], sizes[w])
                cp = pltpu.make_async_remote_copy(src_ref=src, dst_ref=outs[w].at[k], send_sem=send_sems.at[s],
                                                  recv_sem=recv_sems.at[s], device_id=(px, py, c), device_id_type=MESH)
                cp.start()
                cps.append(cp)
        for cp in cps:
            cp.wait_recv()
        for cp in cps:
            cp.wait_send()

    def landing(p, a):
        shape = list(p.shape)
        shape[a] //= N_CHIPS
        return jax.ShapeDtypeStruct((3,) + tuple(shape), p.dtype)

    return pl.pallas_call(
        body,
        name="chip_exchange",
        in_specs=[_ANY] * n,
        out_specs=[_ANY] * n,
        out_shape=[landing(p, a) for p, a in zip(psums, axes)],
        scratch_shapes=[pltpu.SemaphoreType.DMA((3 * n,)), pltpu.SemaphoreType.DMA((3 * n,))],
        compiler_params=pltpu.CompilerParams(has_side_effects=True),
    )(*psums)


def _chip_sum(name, psum, landed, pos, saxis):
    L = psum.shape[0]
    _, _, R, C = landed.shape
    tr = _rows_tile(R, C)
    nr = R // tr
    if saxis == 2:
        p_idx = lambda l, i, s: (l, i, s[0])
        o_idx = lambda l, i, s: (l, s[1] * nr + i, 0)
        o_shape = (L, 2 * R, C)
    else:
        p_idx = lambda l, i, s: (l, s[0] * nr + i, 0)
        o_idx = lambda l, i, s: (l, i, s[1])
        o_shape = (L, R, 2 * C)

    def body(s_ref, p_ref, l_ref, o_ref):
        o_ref[...] = ((p_ref[...].astype(F32) + l_ref[0].astype(F32)) + l_ref[1].astype(F32)) + l_ref[2].astype(F32)

    return pl.pallas_call(
        body,
        name=name,
        grid_spec=pltpu.PrefetchScalarGridSpec(
            num_scalar_prefetch=1,
            grid=(L, nr),
            in_specs=[pl.BlockSpec((None, tr, C), p_idx), pl.BlockSpec((3, None, tr, C), lambda l, i, s: (0, l, i, 0))],
            out_specs=pl.BlockSpec((None, tr, C), o_idx),
        ),
        out_shape=jax.ShapeDtypeStruct(o_shape, F32),
        compiler_params=_cp(("parallel", "parallel")),
    )(pos, psum, landed)


def _sibling_fill(halves):
    n = len(_BIG)
    haxes = [_half_axis(k) for k in _BIG]
    sizes = [h.shape[a] // 2 for h, a in zip(halves, haxes)]

    def body(*refs):
        outs = refs[n:2 * n]
        send_sems, recv_sems = refs[2 * n:]
        x, y, c, _ = _place()
        sends, recvs = [], []
        for w in range(n):
            mine = _sub(outs[w], haxes[w], c * sizes[w], sizes[w])
            theirs = _sub(outs[w], haxes[w], (1 - c) * sizes[w], sizes[w])
            cp = pltpu.make_async_remote_copy(src_ref=mine, dst_ref=mine, send_sem=send_sems.at[w],
                                              recv_sem=recv_sems.at[w], device_id=(x, y, 1 - c), device_id_type=MESH)
            cp.start()
            sends.append(cp)
            recvs.append(pltpu.make_async_remote_copy(src_ref=theirs, dst_ref=theirs, send_sem=send_sems.at[w],
                                                      recv_sem=recv_sems.at[w], device_id=(x, y, 1 - c), device_id_type=MESH))
        for cp in recvs:
            cp.wait_recv()
        for cp in sends:
            cp.wait_send()

    return pl.pallas_call(
        body,
        name="sibling_fill",
        in_specs=[_ANY] * n,
        out_specs=[_ANY] * n,
        out_shape=[jax.ShapeDtypeStruct(h.shape, h.dtype) for h in halves],
        input_output_aliases={w: w for w in range(n)},
        scratch_shapes=[pltpu.SemaphoreType.DMA((n,)), pltpu.SemaphoreType.DMA((n,))],
        compiler_params=pltpu.CompilerParams(has_side_effects=True),
    )(*halves)


def _all_reduce_small(packed):
    R = packed.shape[0]

    def body(p_ref, o_ref, land_ref, send_sems, recv_sems):
        x, y, c, _ = _place()
        me = 4 * x + 2 * y + c
        sends, recvs = [], []
        for r in range(1, N_DEV):
            px, py, pc = x ^ (r >> 2), y ^ ((r >> 1) & 1), c ^ (r & 1)
            cp = pltpu.make_async_remote_copy(src_ref=p_ref, dst_ref=land_ref.at[me], send_sem=send_sems.at[r - 1],
                                              recv_sem=recv_sems.at[r - 1], device_id=(px, py, pc), device_id_type=MESH)
            cp.start()
            sends.append(cp)
            recvs.append(pltpu.make_async_remote_copy(src_ref=p_ref, dst_ref=land_ref.at[4 * px + 2 * py + pc],
                                                      send_sem=send_sems.at[r - 1], recv_sem=recv_sems.at[r - 1],
                                                      device_id=(px, py, pc), device_id_type=MESH))
        land_ref[me] = p_ref[...]
        for cp in recvs:
            cp.wait_recv()
        for cp in sends:
            cp.wait_send()
        acc = land_ref[0]
        for d in range(1, N_DEV):
            acc = acc + land_ref[d]
        o_ref[...] = acc

    vm = pl.BlockSpec(memory_space=pltpu.VMEM)
    return pl.pallas_call(
        body,
        name="all_reduce_small",
        in_specs=[vm],
        out_specs=vm,
        out_shape=jax.ShapeDtypeStruct((R, LANE), F32),
        scratch_shapes=[pltpu.VMEM((N_DEV, R, LANE), F32), pltpu.SemaphoreType.DMA((N_DEV - 1,)),
                        pltpu.SemaphoreType.DMA((N_DEV - 1,))],
        compiler_params=pltpu.CompilerParams(has_side_effects=True, vmem_limit_bytes=VMEM_LIMIT),
    )(packed)


def _adamw(name, w, g, m, v):
    R, C = w.shape
    tr = _rows_tile(R, C)
    c1 = 1.0 - ADAM_B1 ** ADAM_STEP
    c2 = 1.0 - ADAM_B2 ** ADAM_STEP

    def body(w_ref, g_ref, m_ref, v_ref, d_ref, nm_ref, nv_ref):
        gv = g_ref[...]
        nm = ADAM_B1 * m_ref[...] + (1.0 - ADAM_B1) * gv
        nv = ADAM_B2 * v_ref[...] + (1.0 - ADAM_B2) * (gv * gv)
        nm_ref[...] = nm
        nv_ref[...] = nv
        d_ref[...] = -ADAM_LR * ((nm / c1) / (jnp.sqrt(nv / c2) + ADAM_EPS) + ADAM_WD * w_ref[...])

    blk = pl.BlockSpec((tr, C), lambda i: (i, 0))
    out = jax.ShapeDtypeStruct((R, C), F32)
    return pl.pallas_call(
        body,
        name=name,
        grid=(R // tr,),
        in_specs=[blk] * 4,
        out_specs=[blk] * 3,
        out_shape=[out] * 3,
        compiler_params=_cp(("parallel",)),
    )(w, g, m, v)


def _pack(parts):
    rows = []
    for p in parts:
        flat = p.reshape(-1).astype(F32)
        n = -(-flat.shape[0] // (8 * LANE)) * (8 * LANE)
        rows.append(jnp.pad(flat, (0, n - flat.shape[0])).reshape(n // LANE, LANE))
    return jnp.concatenate(rows, axis=0)


def _unpack(packed, like):
    out, r = [], 0
    for p in like:
        size = int(np.prod(p.shape))
        n = -(-size // (8 * LANE)) * 8
        out.append(packed[r:r + n].reshape(-1)[:size].reshape(p.shape))
        r += n
    return out


def kernel(x, mix_norm, ffn_norm, w_ffn_in, w_ffn_out, ab_w_in, ab_gn_gain, ab_w_pool, ab_pool_scale, ab_w_out, c_w_qkv, c_rel_bias, c_w_out, final_norm, loss_target, m_mix_norm, m_ffn_norm, m_w_ffn_in, m_w_ffn_out, m_ab_w_in, m_ab_gn_gain, m_ab_w_pool, m_ab_pool_scale, m_ab_w_out, m_c_w_qkv, m_c_rel_bias, m_c_w_out, m_final_norm, v_mix_norm, v_ffn_norm, v_w_ffn_in, v_w_ffn_out, v_ab_w_in, v_ab_gn_gain, v_ab_w_pool, v_ab_pool_scale, v_ab_w_out, v_c_w_qkv, v_c_rel_bias, v_c_w_out, v_final_norm):
    w = dict(mix_norm=mix_norm, ffn_norm=ffn_norm, w_ffn_in=w_ffn_in, w_ffn_out=w_ffn_out, ab_w_in=ab_w_in,
             ab_gn_gain=ab_gn_gain, ab_w_pool=ab_w_pool, ab_pool_scale=ab_pool_scale, ab_w_out=ab_w_out,
             c_w_qkv=c_w_qkv, c_rel_bias=c_rel_bias, c_w_out=c_w_out, final_norm=final_norm)
    m = dict(mix_norm=m_mix_norm, ffn_norm=m_ffn_norm, w_ffn_in=m_w_ffn_in, w_ffn_out=m_w_ffn_out, ab_w_in=m_ab_w_in,
             ab_gn_gain=m_ab_gn_gain, ab_w_pool=m_ab_w_pool, ab_pool_scale=m_ab_pool_scale, ab_w_out=m_ab_w_out,
             c_w_qkv=m_c_w_qkv, c_rel_bias=m_c_rel_bias, c_w_out=m_c_w_out, final_norm=m_final_norm)
    v = dict(mix_norm=v_mix_norm, ffn_norm=v_ffn_norm, w_ffn_in=v_w_ffn_in, w_ffn_out=v_w_ffn_out, ab_w_in=v_ab_w_in,
             ab_gn_gain=v_ab_gn_gain, ab_w_pool=v_ab_w_pool, ab_pool_scale=v_ab_pool_scale, ab_w_out=v_ab_w_out,
             c_w_qkv=v_c_w_qkv, c_rel_bias=v_c_rel_bias, c_w_out=v_c_w_out, final_norm=v_final_norm)
    S = x.shape[1]
    cx, cy, cc = lax.axis_index("x"), lax.axis_index("y"), lax.axis_index("c")
    core = jnp.reshape(cc, (1,)).astype(jnp.int32)
    pos = jnp.stack([2 * cx + cy, cc]).astype(jnp.int32)

    full = _gather_weights([w[k].astype(BF16) for k in _BIG])
    big = dict(zip(_BIG, full))
    small = {k: w[k] for k in _SMALL}
    loss, grad_x, g_small, g_big = _local_step(x.reshape(S, D_MODEL), loss_target.reshape(S, D_MODEL), small, big)

    grads = [g_big[k] for k in _BIG]
    from_sibling = _pair_exchange(grads)
    psums = [_pair_sum(f"pair_sum_{k}", g, r, core, _half_axis(k)) for k, g, r in zip(_BIG, grads, from_sibling)]
    landed = _chip_exchange(psums)
    halves = [_chip_sum(f"chip_sum_{k}", p, l, pos, _SHARD_AXIS[k]) for k, p, l in zip(_BIG, psums, landed)]
    reduced = dict(zip(_BIG, _sibling_fill(halves)))

    packed = _all_reduce_small(_pack([g_small[k] for k in _SMALL] + [loss]))
    small_like = [w[k] for k in _SMALL]
    g_red = dict(zip(_SMALL, _unpack(packed, small_like)))
    loss_row = packed.shape[0] - 8
    loss_out = packed[loss_row, 0]

    grad, delta, new_m, new_v = {}, {}, {}, {}
    for k in _BIG:
        shp = w[k].shape
        two = (shp[0] * shp[1], shp[2])
        d, nm, nv = _adamw(f"adamw_{k}", w[k].reshape(two), reduced[k].reshape(two), m[k].reshape(two), v[k].reshape(two))
        grad[k], delta[k], new_m[k], new_v[k] = reduced[k], d.reshape(shp), nm.reshape(shp), nv.reshape(shp)
    d, nm, nv = _adamw("adamw_small", _pack(small_like), packed[:loss_row], _pack([m[k] for k in _SMALL]),
                       _pack([v[k] for k in _SMALL]))
    for k, dk, mk, vk in zip(_SMALL, _unpack(d, small_like), _unpack(nm, small_like), _unpack(nv, small_like)):
        grad[k], delta[k], new_m[k], new_v[k] = g_red[k], dk, mk, vk

    order = ("mix_norm", "ffn_norm", "w_ffn_in", "w_ffn_out", "ab_w_in", "ab_gn_gain", "ab_w_pool", "ab_pool_scale",
             "ab_w_out", "c_w_qkv", "c_rel_bias", "c_w_out", "final_norm")
    return (loss_out, grad_x.reshape(x.shape), *[grad[k] for k in order], *[delta[k] for k in order],
            *[new_m[k] for k in order], *[new_v[k] for k in order])
```

```python
import functools
from typing import Callable, NamedTuple

import numpy as np
import jax
import jax.numpy as jnp
from jax import lax
from jax.experimental import pallas as pl
from jax.experimental.pallas import tpu as pltpu

F32 = jnp.float32
BF16 = jnp.bfloat16

D_MODEL = 1024
D_FF = 4096
DEPTH = 4
CHUNK = 64
RMS_EPS = 1e-6
RET_WIDTH = 512
RET_HEADS = 4
RET_HEAD_DIM = 128
RET_ROPE_BASE = 10000.0
GN_EPS = 1e-5
POOL_WIDTH = 512
POOL_WINDOWS = (2, 4, 8, 16)
POOL_GROUP_DIM = 128
POOL_HALO = 16
AB_IN_WIDTH = 2560
ATT_HEADS = 16
ATT_HEAD_DIM = 64
LEFT_CHUNKS = 8
BAND = (LEFT_CHUNKS + 1) * CHUNK
REL_CLIP = 128
N_REL = 2 * REL_CLIP + 1
N_REL_PAD = 264
NEG_INF = -1e30
KSCALE = RET_HEAD_DIM ** -0.5
QSCALE = ATT_HEAD_DIM ** -0.5

ADAM_LR = 0.001
ADAM_B1 = 0.9
ADAM_B2 = 0.999
ADAM_EPS = 1e-08
ADAM_WD = 0.01
ADAM_STEP = 10

ATT_BLOCK = LEFT_CHUNKS * CHUNK
RET_BLOCK = 512
N_CHIPS = 4
N_DEV = 8
LANE = 128
VMEM_LIMIT = 52 * 1024 * 1024
MESH = pl.DeviceIdType.MESH


def _cp(sem, vmem=VMEM_LIMIT):
    return pltpu.CompilerParams(dimension_semantics=sem, vmem_limit_bytes=vmem)


def _dot(a, b):
    return lax.dot_general(a, b, (((1,), (0,)), ((), ())), preferred_element_type=F32)


def _dot_nt(a, b):
    return lax.dot_general(a, b, (((1,), (1,)), ((), ())), preferred_element_type=F32)


def _dot_tn(a, b):
    return lax.dot_general(a, b, (((0,), (0,)), ((), ())), preferred_element_type=F32)


_ANY = pl.BlockSpec(memory_space=pl.ANY)


class _Rider(NamedTuple):
    operands: tuple
    out_shapes: tuple
    n_sems: int
    start: Callable
    finish: Callable


def _mm(name, mode, a, b, *, la=None, lb=None, tm=1024, tn=1024, tk=1024, a_fn=None, b_fn=None,
        extras=(), epi=None, out_dtype=F32, stack=None, rider=None):
    a_parts = list(a) if isinstance(a, (list, tuple)) else [a]
    b_parts = list(b) if isinstance(b, (list, tuple)) else [b]
    na, nbp = len(a_parts), len(b_parts)
    a2, b2 = list(a_parts[0].shape[-2:]), list(b_parts[0].shape[-2:])
    a2[1] *= na
    b2[1] *= nbp
    if mode == "nn":
        (M, K), (K2, N) = a2, b2
    elif mode == "nt":
        (M, K), (N, K2) = a2, b2
    else:
        (K, M), (K2, N) = a2, b2
    assert K == K2, (name, a2, b2)
    tm, tn, tk = min(tm, M), min(tn, N), min(tk, K)
    assert M % tm == 0 and N % tn == 0 and K % tk == 0, (name, M, N, K, tm, tn, tk)
    gm, gn, gk = M // tm, N // tn, K // tk

    def specs(parts, block, idx, lead):
        per = parts[0].shape[-1] // block[1]
        assert parts[0].shape[-1] % block[1] == 0, (name, parts[0].shape, block)
        out = []
        for p in range(len(parts)):
            def f(i, j, k, p=p):
                r, c = idx(i, j, k)
                if len(parts) > 1:
                    c = jnp.clip(c - p * per, 0, per - 1)
                return (r, c) if lead is None else (lead, r, c)
            out.append(pl.BlockSpec(block if lead is None else (None,) + block, f))
        return out, per

    if mode == "nn":
        a_specs, a_per = specs(a_parts, (tm, tk), lambda i, j, k: (i, k), la)
        b_specs, b_per = specs(b_parts, (tk, tn), lambda i, j, k: (k, j), lb)
        a_axis, b_axis, dot = 2, 1, _dot
    elif mode == "nt":
        a_specs, a_per = specs(a_parts, (tm, tk), lambda i, j, k: (i, k), la)
        b_specs, b_per = specs(b_parts, (tn, tk), lambda i, j, k: (j, k), lb)
        a_axis, b_axis, dot = 2, 2, _dot_nt
    else:
        a_specs, a_per = specs(a_parts, (tk, tm), lambda i, j, k: (k, i), la)
        b_specs, b_per = specs(b_parts, (tk, tn), lambda i, j, k: (k, j), lb)
        a_axis, b_axis, dot = 0, 1, _dot_tn
    ex_specs = [pl.BlockSpec((tm, tn), lambda i, j, k: (i, j)) for _ in extras]
    n_ex = len(extras)

    operands = a_parts + b_parts + list(extras)
    in_specs = a_specs + b_specs + ex_specs
    aliases = {}
    if stack is None:
        out_specs = [pl.BlockSpec((tm, tn), lambda i, j, k: (i, j))]
        out_shapes = [jax.ShapeDtypeStruct((M, N), out_dtype)]
    else:
        n_layers, layer, prev = stack
        out_specs = [pl.BlockSpec((None, tm, tn), lambda i, j, k: (layer, i, j))]
        out_shapes = [jax.ShapeDtypeStruct((n_layers, M, N), out_dtype)]
        if prev is not None:
            aliases = {len(operands): 0}
            operands.append(prev)
            in_specs.append(_ANY)
    n_prev = len(aliases)
    scratch = [pltpu.VMEM((tm, tn), F32)] if gk > 1 else []
    n_rin = n_rout = 0
    if rider is not None:
        n_rin, n_rout = len(rider.operands), len(rider.out_shapes)
        operands += list(rider.operands)
        in_specs += [_ANY] * n_rin
        out_specs += [_ANY] * n_rout
        out_shapes += list(rider.out_shapes)
        scratch += [pltpu.SemaphoreType.DMA((rider.n_sems,)), pltpu.SemaphoreType.DMA((rider.n_sems,))]

    def pick(refs, per, axis):
        if len(refs) == 1:
            return refs[0][...]
        return lax.switch(pl.program_id(axis) // per, [functools.partial(lambda r: r[...], r) for r in refs])

    def body(*refs):
        a_refs, b_refs = refs[:na], refs[na:na + nbp]
        ex_refs = refs[na + nbp:na + nbp + n_ex]
        n_in = na + nbp + n_ex + n_prev
        rin = refs[n_in:n_in + n_rin]
        o_ref = refs[n_in + n_rin]
        rout = refs[n_in + n_rin + 1:n_in + n_rin + 1 + n_rout]
        rest = refs[n_in + n_rin + 1 + n_rout:]
        i, j, k = pl.program_id(0), pl.program_id(1), pl.program_id(2)
        if rider is not None:
            sems = rest[-2:]

            @pl.when(jnp.logical_and(i == 0, jnp.logical_and(j == 0, k == 0)))
            def _():
                rider.start(rin, rout, *sems)

        av = pick(a_refs, a_per, a_axis)
        if a_fn is not None:
            av = a_fn(av)
        bv = pick(b_refs, b_per, b_axis)
        if b_fn is not None:
            bv = b_fn(bv)
        part = dot(av.astype(BF16), bv.astype(BF16))

        def finish(acc):
            if epi is not None:
                acc = epi(acc, *[r[...] for r in ex_refs])
            o_ref[...] = acc.astype(o_ref.dtype)

        if gk == 1:
            finish(part)
        else:
            acc_ref = rest[0]

            @pl.when(k == 0)
            def _():
                acc_ref[...] = part

            @pl.when(k > 0)
            def _():
                acc_ref[...] += part

            @pl.when(k == gk - 1)
            def _():
                finish(acc_ref[...])

        if rider is not None:
            @pl.when(jnp.logical_and(i == gm - 1, jnp.logical_and(j == gn - 1, k == gk - 1)))
            def _():
                rider.finish(rin, rout, *sems)

    sem = ("arbitrary",) * 3 if rider is not None else ("parallel", "parallel", "arbitrary")
    outs = pl.pallas_call(
        body,
        name=name,
        grid=(gm, gn, gk),
        in_specs=in_specs,
        out_specs=out_specs,
        out_shape=out_shapes,
        input_output_aliases=aliases,
        scratch_shapes=scratch,
        compiler_params=_cp(sem),
    )(*operands)
    return outs[0] if rider is None else (outs[0], list(outs[1:]))


def _relu2(u):
    r = jnp.maximum(u, 0)
    return r * r


def _rms_fwd(name, x, g):
    S, D = x.shape
    tq = min(1024, S)

    def body(x_ref, g_ref, o_ref):
        xv = x_ref[...]
        r = lax.rsqrt(jnp.mean(xv * xv, axis=-1, keepdims=True) + RMS_EPS)
        o_ref[...] = ((xv * r) * g_ref[...]).astype(o_ref.dtype)

    return pl.pallas_call(
        body,
        name=name,
        grid=(S // tq,),
        in_specs=[pl.BlockSpec((tq, D), lambda i: (i, 0)), pl.BlockSpec((1, D), lambda i: (0, 0))],
        out_specs=pl.BlockSpec((tq, D), lambda i: (i, 0)),
        out_shape=jax.ShapeDtypeStruct((S, D), BF16),
        compiler_params=_cp(("parallel",)),
    )(x, g)


def _rms_bwd(name, x, g, dh, dres):
    S, D = x.shape
    tq = min(512, S)
    n = S // tq

    def body(x_ref, g_ref, dh_ref, dres_ref, dx_ref, dg_ref, acc_ref):
        i = pl.program_id(0)
        xv = x_ref[...]
        r = lax.rsqrt(jnp.mean(xv * xv, axis=-1, keepdims=True) + RMS_EPS)
        xh = xv * r
        dh_v = dh_ref[...].astype(F32)
        dxh = dh_v * g_ref[...]
        dx = r * (dxh - xh * jnp.mean(dxh * xh, axis=-1, keepdims=True))
        dx_ref[...] = dres_ref[...] + dx
        part = jnp.sum((dh_v * xh).reshape(tq // 8, 8, D), axis=0)

        @pl.when(i == 0)
        def _():
            acc_ref[...] = part

        @pl.when(i > 0)
        def _():
            acc_ref[...] += part

        @pl.when(i == n - 1)
        def _():
            dg_ref[...] = jnp.sum(acc_ref[...], axis=0, keepdims=True)

    return pl.pallas_call(
        body,
        name=name,
        grid=(n,),
        in_specs=[pl.BlockSpec((tq, D), lambda i: (i, 0)), pl.BlockSpec((1, D), lambda i: (0, 0)),
                  pl.BlockSpec((tq, D), lambda i: (i, 0)), pl.BlockSpec((tq, D), lambda i: (i, 0))],
        out_specs=[pl.BlockSpec((tq, D), lambda i: (i, 0)), pl.BlockSpec((1, D), lambda i: (0, 0))],
        out_shape=[jax.ShapeDtypeStruct((S, D), F32), jax.ShapeDtypeStruct((1, D), F32)],
        scratch_shapes=[pltpu.VMEM((8, D), F32)],
        compiler_params=_cp(("arbitrary",)),
    )(x, g, dh, dres)


def _loss_head(x, g, t):
    S, D = x.shape
    tq = min(512, S)
    n = S // tq

    def body(x_ref, g_ref, t_ref, loss_ref, dx_ref, dg_ref, lacc_ref, gacc_ref):
        i = pl.program_id(0)
        xv = x_ref[...]
        gv = g_ref[...]
        r = lax.rsqrt(jnp.mean(xv * xv, axis=-1, keepdims=True) + RMS_EPS)
        xh = xv * r
        e = xh * gv - t_ref[...]
        dy = e * (1.0 / D)
        dxh = dy * gv
        dx_ref[...] = r * (dxh - xh * jnp.mean(dxh * xh, axis=-1, keepdims=True))
        lpart = jnp.sum((e * e).reshape(tq // 8, 8, D), axis=0)
        gpart = jnp.sum((dy * xh).reshape(tq // 8, 8, D), axis=0)

        @pl.when(i == 0)
        def _():
            lacc_ref[...] = lpart
            gacc_ref[...] = gpart

        @pl.when(i > 0)
        def _():
            lacc_ref[...] += lpart
            gacc_ref[...] += gpart

        @pl.when(i == n - 1)
        def _():
            dg_ref[...] = jnp.sum(gacc_ref[...], axis=0, keepdims=True)
            tot = jnp.sum(jnp.sum(lacc_ref[...], axis=0, keepdims=True), axis=1, keepdims=True)
            loss_ref[...] = jnp.broadcast_to(tot * (0.5 / D), (1, LANE))

    return pl.pallas_call(
        body,
        name="loss_head",
        grid=(n,),
        in_specs=[pl.BlockSpec((tq, D), lambda i: (i, 0)), pl.BlockSpec((1, D), lambda i: (0, 0)),
                  pl.BlockSpec((tq, D), lambda i: (i, 0))],
        out_specs=[pl.BlockSpec((1, LANE), lambda i: (0, 0)), pl.BlockSpec((tq, D), lambda i: (i, 0)),
                   pl.BlockSpec((1, D), lambda i: (0, 0))],
        out_shape=[jax.ShapeDtypeStruct((1, LANE), F32), jax.ShapeDtypeStruct((S, D), F32),
                   jax.ShapeDtypeStruct((1, D), F32)],
        scratch_shapes=[pltpu.VMEM((8, D), F32), pltpu.VMEM((8, D), F32)],
        compiler_params=_cp(("arbitrary",)),
    )(x, g, t)


def _ret_tables(S):
    T = min(RET_BLOCK, S)
    inv_freq = 1.0 / (RET_ROPE_BASE ** jnp.linspace(0.0, 1.0, RET_HEAD_DIM // 2, dtype=F32))
    ang = jnp.arange(S, dtype=F32)[:, None] * inv_freq[None, :]
    cos, sin = jnp.cos(ang), jnp.sin(ang)
    cosf = jnp.repeat(cos, 2, axis=-1)
    sins = jnp.stack([-sin, sin], axis=-1).reshape(S, RET_HEAD_DIM)
    log_g = np.log1p(-np.power(2.0, -5.0 - np.arange(RET_HEADS, dtype=np.float64)))
    pos = np.arange(T, dtype=np.float64)
    diff = pos[:, None] - pos[None, :]
    same = (pos[:, None] // CHUNK) == (pos[None, :] // CHUNK)
    seen = same | (diff > 0)
    dmat = np.where(seen[None], np.exp(np.abs(diff)[None] * log_g[:, None, None]), 0.0)
    aq = np.exp((pos[None, :] + 1.0) * log_g[:, None])
    ak = np.exp((T - 1.0 - pos[None, :]) * log_g[:, None])
    lam = np.exp(T * log_g)
    bc = lambda v: jnp.asarray(np.broadcast_to(v[..., None], v.shape + (LANE,)), F32)
    return dict(cos=cosf, sin=sins, dmat=jnp.asarray(dmat, F32), aq=bc(aq), ak=bc(ak),
                lam=jnp.asarray(np.broadcast_to(lam[:, None, None], (RET_HEADS, 1, LANE)), F32))


def _rot(x, cos, sin_s, even):
    sw = jnp.where(even, pltpu.roll(x, LANE - 1, 1), pltpu.roll(x, 1, 1))
    return x * cos + sw * sin_s


def _rot_t(dy, cos, sin_s, even):
    t = dy * sin_s
    return dy * cos + jnp.where(even, pltpu.roll(t, LANE - 1, 1), pltpu.roll(t, 1, 1))


def _ret_specs(T, rev_nb=None):
    blk = (lambda b: b) if rev_nb is None else (lambda b: rev_nb - 1 - b)
    whole = lambda shape: pl.BlockSpec(shape, lambda b: (0,) * len(shape))
    specs = [pl.BlockSpec((T, AB_IN_WIDTH), lambda b: (blk(b), 0)),
             pl.BlockSpec((T, LANE), lambda b: (blk(b), 0)),
             pl.BlockSpec((T, LANE), lambda b: (blk(b), 0)),
             whole((RET_HEADS, T, T)), whole((RET_HEADS, T, LANE)), whole((RET_HEADS, T, LANE)),
             whole((RET_HEADS, 1, LANE)), whole((1, RET_WIDTH))]
    return specs, blk


def _head_views(h, z_ref, tabs, token_refs, head_refs):
    zs = [z_ref.at[:, (o * RET_HEADS + h) * LANE:(o * RET_HEADS + h + 1) * LANE] for o in range(4)]
    hs = slice(h * LANE, (h + 1) * LANE)
    return zs, [t.at[h] for t in tabs], [r.at[:, hs] for r in token_refs], [r.at[h] for r in head_refs]


def _ret_fwd(name, z, tb, gain):
    S = z.shape[0]
    T = min(RET_BLOCK, S)
    nb = S // T
    specs, blk = _ret_specs(T)

    def body(z_ref, cos_r, sin_r, d_all, aq_all, ak_all, lam_all, gain_all, cat_all, opre_all, st_all, state_all):
        @pl.when(pl.program_id(0) == 0)
        def _():
            state_all[...] = jnp.zeros_like(state_all)

        for h in range(RET_HEADS):
            zs, tabs, toks, heads = _head_views(h, z_ref, (d_all, aq_all, ak_all, lam_all),
                                                (gain_all, cat_all, opre_all), (st_all, state_all))
            head(*zs, cos_r, sin_r, *tabs, *toks, *heads)

    def head(zq, zk, zv, zg, cos_r, sin_r, d_r, aq_r, ak_r, lam_r, gain_r, ret_o, opre_o, st_o, state):
        even = (lax.broadcasted_iota(jnp.int32, (T, LANE), 1) & 1) == 0
        c, s = cos_r[...], sin_r[...]
        q = _rot(zq[...], c, s, even)
        k = _rot(zk[...], c, s, even) * KSCALE
        qb, kb, vb = q.astype(BF16), k.astype(BF16), zv[...].astype(BF16)
        p = (_dot_nt(qb, kb) * d_r[...]).astype(BF16)
        st = state[...]
        st_o[...] = st
        o = _dot(p, vb) + _dot((q * aq_r[...]).astype(BF16), st.astype(BF16))
        state[...] = st * lam_r[...] + _dot_tn((k * ak_r[...]).astype(BF16), vb)
        opre_o[...] = o
        mu = jnp.mean(o, axis=-1, keepdims=True)
        d = o - mu
        y = d * lax.rsqrt(jnp.mean(d * d, axis=-1, keepdims=True) + GN_EPS)
        g = zg[...]
        ret_o[...] = ((g * jax.nn.sigmoid(g)) * (y * gain_r[...])).astype(ret_o.dtype)

    out_blk = pl.BlockSpec((T, RET_WIDTH), lambda b: (b, 0))
    return pl.pallas_call(
        body,
        name=name,
        grid=(nb,),
        in_specs=specs,
        out_specs=[out_blk, out_blk, pl.BlockSpec((RET_HEADS, None, LANE, LANE), lambda b: (0, b, 0, 0))],
        out_shape=[jax.ShapeDtypeStruct((S, D_MODEL), BF16), jax.ShapeDtypeStruct((S, RET_WIDTH), F32),
                   jax.ShapeDtypeStruct((RET_HEADS, nb, LANE, LANE), F32)],
        scratch_shapes=[pltpu.VMEM((RET_HEADS, LANE, LANE), F32)],
        compiler_params=_cp(("arbitrary",)),
    )(z, tb["cos"], tb["sin"], tb["dmat"], tb["aq"], tb["ak"], tb["lam"], gain)


def _ret_bwd(name, z, tb, gain, opre, states, dcat):
    S = z.shape[0]
    T = min(RET_BLOCK, S)
    nb = S // T
    specs, blk = _ret_specs(T, rev_nb=nb)
    tok = pl.BlockSpec((T, RET_WIDTH), lambda b: (blk(b), 0))

    def body(z_ref, cos_r, sin_r, d_all, aq_all, ak_all, lam_all, gain_all, opre_all, st_all, dret_all,
             dz_ref, dgain_all, dstate_all):
        @pl.when(pl.program_id(0) == 0)
        def _():
            dstate_all[...] = jnp.zeros_like(dstate_all)
            dgain_all[...] = jnp.zeros_like(dgain_all)

        for h in range(RET_HEADS):
            zs, tabs, toks, heads = _head_views(h, z_ref, (d_all, aq_all, ak_all, lam_all),
                                                (gain_all, opre_all, dret_all, dgain_all), (st_all, dstate_all))
            dzs, _, _, _ = _head_views(h, dz_ref, (), (), ())
            gain_r, opre_r, dret_r, dgain_o = toks
            head(*zs, cos_r, sin_r, *tabs, gain_r, opre_r, heads[0], dret_r, *dzs, dgain_o, heads[1])

    def head(zq, zk, zv, zg, cos_r, sin_r, d_r, aq_r, ak_r, lam_r, gain_r, opre_r, st_r, dret_r,
             dq_o, dk_o, dv_o, dg_o, dgain_o, dstate):
        even = (lax.broadcasted_iota(jnp.int32, (T, LANE), 1) & 1) == 0
        c, s = cos_r[...], sin_r[...]
        aq, ak, dm = aq_r[...], ak_r[...], d_r[...]
        q = _rot(zq[...], c, s, even)
        k = _rot(zk[...], c, s, even) * KSCALE
        qb, kb, vb = q.astype(BF16), k.astype(BF16), zv[...].astype(BF16)
        pb = (_dot_nt(qb, kb) * dm).astype(BF16)
        g = zg[...]
        sig = jax.nn.sigmoid(g)
        o = opre_r[...]
        mu = jnp.mean(o, axis=-1, keepdims=True)
        d = o - mu
        rstd = lax.rsqrt(jnp.mean(d * d, axis=-1, keepdims=True) + GN_EPS)
        y = d * rstd
        gain_v = gain_r[...]
        dret = dret_r[...].astype(F32)
        dyg = dret * (g * sig)
        dg_o[...] = (dret * (y * gain_v) * (sig * (1.0 + g * (1.0 - sig)))).astype(dg_o.dtype)
        dgain_o[...] += jnp.sum(dyg * y, axis=0, keepdims=True)
        dy = dyg * gain_v
        do = rstd * (dy - jnp.mean(dy, axis=-1, keepdims=True) - y * jnp.mean(dy * y, axis=-1, keepdims=True))
        dob = do.astype(BF16)
        stb = st_r[...].astype(BF16)
        dsn = dstate[...]
        dsnb = dsn.astype(BF16)
        dpb = (_dot_nt(dob, vb) * dm).astype(BF16)
        dq = _dot(dpb, kb) + _dot_nt(dob, stb) * aq
        dk = _dot_tn(dpb, qb) + _dot_nt(vb, dsnb) * ak
        dv = _dot_tn(pb, dob) + _dot((k * ak).astype(BF16), dsnb)
        dstate[...] = dsn * lam_r[...] + _dot_tn((q * aq).astype(BF16), dob)
        dq_o[...] = _rot_t(dq, c, s, even).astype(dq_o.dtype)
        dk_o[...] = _rot_t(dk * KSCALE, c, s, even).astype(dk_o.dtype)
        dv_o[...] = dv.astype(dv_o.dtype)

    return pl.pallas_call(
        body,
        name=name,
        grid=(nb,),
        in_specs=specs + [tok, pl.BlockSpec((RET_HEADS, None, LANE, LANE), lambda b: (0, blk(b), 0, 0)), tok],
        out_specs=[pl.BlockSpec((T, 4 * RET_WIDTH), lambda b: (blk(b), 0)), pl.BlockSpec((1, RET_WIDTH), lambda b: (0, 0))],
        out_shape=[jax.ShapeDtypeStruct((S, AB_IN_WIDTH), BF16), jax.ShapeDtypeStruct((1, RET_WIDTH), F32)],
        scratch_shapes=[pltpu.VMEM((RET_HEADS, LANE, LANE), F32)],
        compiler_params=_cp(("arbitrary",)),
    )(z, tb["cos"], tb["sin"], tb["dmat"], tb["aq"], tb["ak"], tb["lam"], gain, opre, states, dcat)


def _pool_counts(t0, rows):
    t = t0 + lax.broadcasted_iota(jnp.int32, (rows, POOL_WIDTH), 0)
    grp = lax.broadcasted_iota(jnp.int32, (rows, POOL_WIDTH), 1) >> 7
    win = jnp.where(grp == 0, POOL_WINDOWS[0], jnp.where(grp == 1, POOL_WINDOWS[1],
                    jnp.where(grp == 2, POOL_WINDOWS[2], POOL_WINDOWS[3])))
    return jnp.maximum(jnp.minimum(t + 1, win), 1).astype(F32), grp


def _window_sums(ext, grp, sign):
    n = ext.shape[0]
    sh = lambda v, k: pltpu.roll(v, k % n if sign > 0 else (n - k) % n, 0)
    s2 = ext + sh(ext, 1)
    s4 = s2 + sh(s2, 2)
    s8 = s4 + sh(s4, 4)
    s16 = s8 + sh(s8, 8)
    return jnp.where(grp == 0, s2, jnp.where(grp == 1, s4, jnp.where(grp == 2, s8, s16)))


def _pool_fwd(name, z, w_pool, scale, cat):
    S = z.shape[0]
    T = min(512, S)
    nb = S // T
    pcol = AB_IN_WIDTH // POOL_WIDTH - 1
    hb = T // POOL_HALO

    def body(p_ref, halo_ref, w_ref, sc_ref, cat_in, out_ref, pooled_ref):
        b = pl.program_id(0)
        cur = p_ref[...]
        halo = jnp.where(b > 0, halo_ref[...], 0.0)
        ext = jnp.concatenate([halo, cur], axis=0)
        cnt, grp = _pool_counts(b * T - POOL_HALO, T + POOL_HALO)
        sums = _window_sums(ext, grp, +1)
        pooled = (sums / cnt)[POOL_HALO:] - cur
        pb = pooled.astype(BF16)
        pooled_ref[...] = pb
        for gi in range(len(POOL_WINDOWS)):
            cs = slice(gi * POOL_GROUP_DIM, (gi + 1) * POOL_GROUP_DIM)
            mixed = _dot(pb[:, cs], w_ref[gi].astype(BF16))
            out_ref[:, cs] = (mixed * sc_ref[:, cs]).astype(out_ref.dtype)

    return pl.pallas_call(
        body,
        name=name,
        grid=(nb,),
        in_specs=[pl.BlockSpec((T, POOL_WIDTH), lambda b: (b, pcol)),
                  pl.BlockSpec((POOL_HALO, POOL_WIDTH), lambda b: (jnp.maximum(b * hb - 1, 0), pcol)),
                  pl.BlockSpec((4, POOL_GROUP_DIM, POOL_GROUP_DIM), lambda b: (0, 0, 0)),
                  pl.BlockSpec((1, POOL_WIDTH), lambda b: (0, 0)), _ANY],
        out_specs=[pl.BlockSpec((T, POOL_WIDTH), lambda b: (b, 1)), pl.BlockSpec((T, POOL_WIDTH), lambda b: (b, 0))],
        out_shape=[jax.ShapeDtypeStruct(cat.shape, cat.dtype), jax.ShapeDtypeStruct((S, POOL_WIDTH), BF16)],
        input_output_aliases={4: 0},
        compiler_params=_cp(("parallel",)),
    )(z, z, w_pool, scale, cat)


def _pool_bwd(name, pooled, w_pool, scale, dcat, dz):
    S = pooled.shape[0]
    T = min(512, S)
    nb = S // T
    hb = T // POOL_HALO
    last_h = S // POOL_HALO - 1
    pcol = AB_IN_WIDTH // POOL_WIDTH - 1

    def body(d_ref, dn_ref, pooled_ref, w_ref, sc_ref, dz_in, dp_ref, dw_ref, dsc_ref):
        b = pl.program_id(0)

        @pl.when(b == 0)
        def _():
            dw_ref[...] = jnp.zeros_like(dw_ref)
            dsc_ref[...] = jnp.zeros_like(dsc_ref)

        sc = sc_ref[...]
        dout = d_ref[...].astype(F32)
        dnext = jnp.where(b < nb - 1, dn_ref[...].astype(F32), 0.0)
        dmix = jnp.concatenate([dout, dnext], axis=0) * sc
        dmb = dmix.astype(BF16)
        pb = pooled_ref[...]
        dpooled = []
        for gi in range(len(POOL_WINDOWS)):
            cs = slice(gi * POOL_GROUP_DIM, (gi + 1) * POOL_GROUP_DIM)
            wb = w_ref[gi].astype(BF16)
            dpooled.append(_dot_nt(dmb[:, cs], wb))
            dw_ref[gi] += _dot_tn(pb[:, cs], dmb[:T, cs])
            mixed = _dot(pb[:, cs], wb)
            dsc_ref[:, cs] += jnp.sum(dout[:, cs] * mixed, axis=0, keepdims=True)
        dpl = jnp.concatenate(dpooled, axis=1)
        cnt, grp = _pool_counts(b * T, T + POOL_HALO)
        sums = _window_sums(dpl / cnt, grp, -1)
        dp_ref[...] = (sums[:T] - dpl[:T]).astype(dp_ref.dtype)

    return pl.pallas_call(
        body,
        name=name,
        grid=(nb,),
        in_specs=[pl.BlockSpec((T, POOL_WIDTH), lambda b: (b, 1)),
                  pl.BlockSpec((POOL_HALO, POOL_WIDTH), lambda b: (jnp.minimum((b + 1) * hb, last_h), 1)),
                  pl.BlockSpec((T, POOL_WIDTH), lambda b: (b, 0)),
                  pl.BlockSpec((4, POOL_GROUP_DIM, POOL_GROUP_DIM), lambda b: (0, 0, 0)),
                  pl.BlockSpec((1, POOL_WIDTH), lambda b: (0, 0)), _ANY],
        out_specs=[pl.BlockSpec((T, POOL_WIDTH), lambda b: (b, pcol)),
                   pl.BlockSpec((4, POOL_GROUP_DIM, POOL_GROUP_DIM), lambda b: (0, 0, 0)),
                   pl.BlockSpec((1, POOL_WIDTH), lambda b: (0, 0))],
        out_shape=[jax.ShapeDtypeStruct(dz.shape, dz.dtype),
                   jax.ShapeDtypeStruct((4, POOL_GROUP_DIM, POOL_GROUP_DIM), F32),
                   jax.ShapeDtypeStruct((1, POOL_WIDTH), F32)],
        input_output_aliases={5: 0},
        compiler_params=_cp(("arbitrary",)),
    )(dcat, dcat, pooled, w_pool, scale, dz)


ATT_Q = 256
ATT_W = ATT_Q + LEFT_CHUNKS * CHUNK
_REL_COLS = CHUNK * ATT_W
_REL_TILE = _REL_COLS // 8


def _rel_index():
    n = np.arange(CHUNK)[:, None]
    j = np.arange(ATT_W)[None, :]
    rel = np.clip(n + LEFT_CHUNKS * CHUNK - j, -REL_CLIP, REL_CLIP) + REL_CLIP
    rel = np.where(j < BAND, rel, N_REL)
    return jnp.asarray(rel.reshape(1, _REL_COLS), jnp.int32)


def _bias_table(name, rel_bias, rel_idx):
    rb = jnp.concatenate([rel_bias, jnp.full((ATT_HEADS, 1), NEG_INF, F32),
                          jnp.zeros((ATT_HEADS, N_REL_PAD - N_REL - 1), F32)], axis=1)

    def body(rb_ref, idx_ref, o_ref):
        r = lax.broadcasted_iota(jnp.int32, (N_REL_PAD, _REL_TILE), 0)
        onehot = (r == idx_ref[...]).astype(F32)
        o_ref[...] = jnp.dot(rb_ref[...], onehot, precision=lax.Precision.HIGHEST, preferred_element_type=F32)

    return pl.pallas_call(
        body,
        name=name,
        grid=(_REL_COLS // _REL_TILE,),
        in_specs=[pl.BlockSpec((ATT_HEADS, N_REL_PAD), lambda i: (0, 0)), pl.BlockSpec((1, _REL_TILE), lambda i: (0, i))],
        out_specs=pl.BlockSpec((ATT_HEADS, _REL_TILE), lambda i: (0, i)),
        out_shape=jax.ShapeDtypeStruct((ATT_HEADS, _REL_COLS), F32),
        compiler_params=_cp(("parallel",)),
    )(rb, rel_idx)


def _bias_grad(name, ds_sum, rel_idx):
    n = _REL_COLS // _REL_TILE

    def body(ds_ref, idx_ref, o_ref):
        i = pl.program_id(0)
        r = lax.broadcasted_iota(jnp.int32, (N_REL_PAD, _REL_TILE), 0)
        onehot = (r == idx_ref[...]).astype(F32)
        part = lax.dot_general(ds_ref[...], onehot, (((1,), (1,)), ((), ())),
                               precision=lax.Precision.HIGHEST, preferred_element_type=F32)

        @pl.when(i == 0)
        def _():
            o_ref[...] = part

        @pl.when(i > 0)
        def _():
            o_ref[...] += part

    out = pl.pallas_call(
        body,
        name=name,
        grid=(n,),
        in_specs=[pl.BlockSpec((ATT_HEADS, _REL_TILE), lambda i: (0, i)), pl.BlockSpec((1, _REL_TILE), lambda i: (0, i))],
        out_specs=pl.BlockSpec((ATT_HEADS, N_REL_PAD), lambda i: (0, 0)),
        out_shape=jax.ShapeDtypeStruct((ATT_HEADS, N_REL_PAD), F32),
        compiler_params=_cp(("arbitrary",)),
    )(ds_sum, rel_idx)
    return out[:, :N_REL]


def _attn_unit(q_ref, kw_ref, bias_ref, e, u, first_block, col, lane):
    mine = (lane < ATT_HEAD_DIM) if e == 0 else (lane >= ATT_HEAD_DIM)
    qm = jnp.where(mine, q_ref[u * ATT_Q:(u + 1) * ATT_Q, :] * QSCALE, 0)
    kw = kw_ref[u * ATT_Q:u * ATT_Q + ATT_W, :]
    s = _dot_nt(qm, kw) + bias_ref[e]
    s = jnp.where(col >= first_block * (ATT_BLOCK - u * ATT_Q), s, NEG_INF)
    p = jnp.exp(s - jnp.max(s, axis=-1, keepdims=True))
    return p, 1.0 / jnp.sum(p, axis=-1, keepdims=True), qm, kw, mine


def _attn_in_specs(nb):
    T = ATT_BLOCK
    hp = ATT_HEADS // 2
    cur = lambda off: pl.BlockSpec((T, LANE), lambda h, b: (jnp.minimum(b, nb - 1), off + h))
    prev = lambda off: pl.BlockSpec((T, LANE), lambda h, b: (jnp.clip(b - 1, 0, nb - 1), off + h))
    return [cur(0), prev(hp), cur(hp), prev(2 * hp), cur(2 * hp),
            pl.BlockSpec((None, 2, CHUNK, ATT_W), lambda h, b: (h, 0, 0, 0))]


def _spread_bias(bias_ref, bm_ref):
    for e in range(2):
        for j in range(ATT_Q // CHUNK):
            bm_ref[e, j * CHUNK:(j + 1) * CHUNK, :] = pltpu.roll(bias_ref[e], j * CHUNK, 1)


def _attn_fwd(name, qkv, bias):
    S = qkv.shape[0]
    T = ATT_BLOCK
    nb = S // T

    def body(q_ref, kp_ref, kc_ref, vp_ref, vc_ref, band_ref, o_ref, kw_ref, vw_ref, bias_ref):
        b = pl.program_id(1)

        @pl.when(b == 0)
        def _():
            _spread_bias(band_ref, bias_ref)

        kw_ref[0:T] = kp_ref[...]
        kw_ref[T:2 * T] = kc_ref[...]
        vw_ref[0:T] = vp_ref[...]
        vw_ref[T:2 * T] = vc_ref[...]
        col = lax.broadcasted_iota(jnp.int32, (ATT_Q, ATT_W), 1)
        lane = lax.broadcasted_iota(jnp.int32, (ATT_Q, LANE), 1)
        first = jnp.where(b == 0, 1, 0)
        for u in range(T // ATT_Q):
            vw = vw_ref[u * ATT_Q:u * ATT_Q + ATT_W, :]
            outs = []
            for e in range(2):
                p, inv, _, _, _ = _attn_unit(q_ref, kw_ref, bias_ref, e, u, first, col, lane)
                outs.append(_dot(p.astype(BF16), vw) * inv)
            o_ref[u * ATT_Q:(u + 1) * ATT_Q, :] = jnp.where(lane < ATT_HEAD_DIM, outs[0], outs[1]).astype(o_ref.dtype)

    return pl.pallas_call(
        body,
        name=name,
        grid=(ATT_HEADS // 2, nb),
        in_specs=_attn_in_specs(nb),
        out_specs=pl.BlockSpec((T, LANE), lambda h, b: (b, h)),
        out_shape=jax.ShapeDtypeStruct((S, D_MODEL), BF16),
        scratch_shapes=[pltpu.VMEM((2 * T, LANE), BF16), pltpu.VMEM((2 * T, LANE), BF16),
                        pltpu.VMEM((2, ATT_Q, ATT_W), F32)],
        compiler_params=_cp(("parallel", "arbitrary")),
    )(qkv, qkv, qkv, qkv, qkv, bias)


def _attn_bwd(name, qkv, bias, do):
    S = qkv.shape[0]
    T = ATT_BLOCK
    nb = S // T

    def body(q_ref, kp_ref, kc_ref, vp_ref, vc_ref, band_ref, do_ref,
             dq_ref, dk_ref, dv_ref, dband_ref, kw_ref, vw_ref, dkw_ref, dvw_ref, bias_ref, dbias_ref):
        b = pl.program_id(1)

        @pl.when(b == 0)
        def _():
            _spread_bias(band_ref, bias_ref)
            dbias_ref[...] = jnp.zeros_like(dbias_ref)
            dkw_ref[T:2 * T] = jnp.zeros((T, LANE), F32)
            dvw_ref[T:2 * T] = jnp.zeros((T, LANE), F32)

        dkw_ref[0:T] = dkw_ref[T:2 * T]
        dvw_ref[0:T] = dvw_ref[T:2 * T]
        dkw_ref[T:2 * T] = jnp.zeros((T, LANE), F32)
        dvw_ref[T:2 * T] = jnp.zeros((T, LANE), F32)

        @pl.when(b < nb)
        def _():
            kw_ref[0:T] = kp_ref[...]
            kw_ref[T:2 * T] = kc_ref[...]
            vw_ref[0:T] = vp_ref[...]
            vw_ref[T:2 * T] = vc_ref[...]
            col = lax.broadcasted_iota(jnp.int32, (ATT_Q, ATT_W), 1)
            lane = lax.broadcasted_iota(jnp.int32, (ATT_Q, LANE), 1)
            first = jnp.where(b == 0, 1, 0)
            for u in range(T // ATT_Q):
                rows = slice(u * ATT_Q, (u + 1) * ATT_Q)
                win = slice(u * ATT_Q, u * ATT_Q + ATT_W)
                vw = vw_ref[win, :]
                do2 = do_ref[rows, :]
                dqs, dk, dv = [], None, None
                for e in range(2):
                    p, inv, qm, kw, mine = _attn_unit(q_ref, kw_ref, bias_ref, e, u, first, col, lane)
                    dom = jnp.where(mine, do2, 0)
                    dp = _dot_nt(dom, vw)
                    delta = jnp.sum(p * dp, axis=-1, keepdims=True) * inv
                    ds = p * ((dp - delta) * inv)
                    dbias_ref[e] += ds
                    dsb = ds.astype(BF16)
                    dqs.append(_dot(dsb, kw))
                    dk_e = _dot_tn(dsb, qm)
                    dv_e = _dot_tn((p * inv).astype(BF16), dom)
                    dk = dk_e if dk is None else dk + dk_e
                    dv = dv_e if dv is None else dv + dv_e
                dq_ref[rows, :] = (jnp.where(lane < ATT_HEAD_DIM, dqs[0], dqs[1]) * QSCALE).astype(dq_ref.dtype)
                dkw_ref[win, :] += dk
                dvw_ref[win, :] += dv

        @pl.when(b > 0)
        def _():
            dk_ref[...] = dkw_ref[0:T].astype(dk_ref.dtype)
            dv_ref[...] = dvw_ref[0:T].astype(dv_ref.dtype)

        @pl.when(b == nb)
        def _():
            for e in range(2):
                acc = dbias_ref[e, 0:CHUNK, :]
                for j in range(1, ATT_Q // CHUNK):
                    acc = acc + pltpu.roll(dbias_ref[e, j * CHUNK:(j + 1) * CHUNK, :], ATT_W - j * CHUNK, 1)
                dband_ref[e] = acc

    tok = jax.ShapeDtypeStruct((S, D_MODEL), BF16)
    prev_out = pl.BlockSpec((T, LANE), lambda h, b: (jnp.maximum(b - 1, 0), h))
    return pl.pallas_call(
        body,
        name=name,
        grid=(ATT_HEADS // 2, nb + 1),
        in_specs=_attn_in_specs(nb) + [pl.BlockSpec((T, LANE), lambda h, b: (jnp.minimum(b, nb - 1), h))],
        out_specs=[pl.BlockSpec((T, LANE), lambda h, b: (jnp.minimum(b, nb - 1), h)), prev_out, prev_out,
                   pl.BlockSpec((None, 2, CHUNK, ATT_W), lambda h, b: (h, 0, 0, 0))],
        out_shape=[tok, tok, tok, jax.ShapeDtypeStruct((ATT_HEADS // 2, 2, CHUNK, ATT_W), F32)],
        scratch_shapes=[pltpu.VMEM((2 * T, LANE), BF16), pltpu.VMEM((2 * T, LANE), BF16),
                        pltpu.VMEM((2 * T, LANE), F32), pltpu.VMEM((2 * T, LANE), F32),
                        pltpu.VMEM((2, ATT_Q, ATT_W), F32), pltpu.VMEM((2, ATT_Q, ATT_W), F32)],
        compiler_params=_cp(("parallel", "arbitrary")),
    )(qkv, qkv, qkv, qkv, qkv, bias, do)


def _local_step(x, target, small, W):
    S = x.shape[0]
    tb = _ret_tables(S)
    rel_idx = _rel_index()
    saved = []
    for layer in range(DEPTH):
        i = layer // 2
        st = {"x_in": x}
        h = _rms_fwd(f"mix_norm_fwd{layer}", x, small["mix_norm"][layer:layer + 1])
        st["h"] = h
        if layer % 2 == 0:
            z = W.mm(f"ab_in_fwd{layer}", "nn", h, W.get("ab_w_in", i), tn=640, out_dtype=F32)
            gain = small["ab_gn_gain"][i:i + 1]
            cat, opre, states = _ret_fwd(f"ret_fwd{layer}", z, tb, gain)
            cat, pooled = _pool_fwd(f"pool_fwd{layer}", z, small["ab_w_pool"][i], small["ab_pool_scale"][i:i + 1], cat)
            st.update(z=z, opre=opre, states=states, pooled=pooled, cat=cat)
            x = W.mm(f"ab_out_fwd{layer}", "nn", cat, W.get("ab_w_out", i), extras=(x,), epi=lambda acc, r: acc + r)
        else:
            qkv = W.mm(f"qkv_fwd{layer}", "nn", h, W.get("c_w_qkv", i), out_dtype=BF16)
            bias = _bias_table(f"bias_table{layer}", small["c_rel_bias"][i], rel_idx)
            bias = bias.reshape(ATT_HEADS // 2, 2, CHUNK, ATT_W)
            att = _attn_fwd(f"attn_fwd{layer}", qkv, bias)
            st.update(qkv=qkv, bias=bias, att=att)
            x = W.mm(f"c_out_fwd{layer}", "nn", att, W.get("c_w_out", i), extras=(x,), epi=lambda acc, r: acc + r)
        st["x_mid"] = x
        hn = _rms_fwd(f"ffn_norm_fwd{layer}", x, small["ffn_norm"][layer:layer + 1])
        u = W.mm(f"ffn_in_fwd{layer}", "nn", hn, W.get("w_ffn_in", layer), out_dtype=BF16)
        x = W.mm(f"ffn_out_fwd{layer}", "nn", u, W.get("w_ffn_out", layer), a_fn=_relu2, extras=(x,),
                epi=lambda acc, r: acc + r)
        st.update(hn=hn, u=u)
        saved.append(st)

    loss, dx, d_final = _loss_head(x, small["final_norm"].reshape(1, D_MODEL), target)

    gs = {k: [None] * v.shape[0] for k, v in small.items() if k != "final_norm"}
    gb = {k: None for k in W.n_layers}

    def dw(name, key, idx, a, b, **kw):
        gb[key] = _mm(name, "tn", a, b, stack=(W.n_layers[key], idx, gb[key]), **kw)

    for layer in reversed(range(DEPTH)):
        i = layer // 2
        st = saved[layer]
        du = _mm(f"ffn_out_bwd{layer}", "nt", dx, W.get("w_ffn_out", layer), extras=(st["u"],),
                 epi=lambda acc, u: acc * (2.0 * jnp.maximum(u, 0).astype(F32)), out_dtype=BF16)
        dw(f"ffn_out_dw{layer}", "w_ffn_out", layer, st["u"], dx, a_fn=_relu2)
        dhn = _mm(f"ffn_in_bwd{layer}", "nt", du, W.get("w_ffn_in", layer))
        dw(f"ffn_in_dw{layer}", "w_ffn_in", layer, st["hn"], du)
        dx, gs["ffn_norm"][layer] = _rms_bwd(f"ffn_norm_bwd{layer}", st["x_mid"], small["ffn_norm"][layer:layer + 1], dhn, dx)
        if layer % 2 == 0:
            dcat = _mm(f"ab_out_bwd{layer}", "nt", dx, W.get("ab_w_out", i), out_dtype=BF16)
            dw(f"ab_out_dw{layer}", "ab_w_out", i, st["cat"], dx)
            gain = small["ab_gn_gain"][i:i + 1]
            dz, gs["ab_gn_gain"][i] = _ret_bwd(f"ret_bwd{layer}", st["z"], tb, gain, st["opre"], st["states"], dcat)
            dz, gs["ab_w_pool"][i], gs["ab_pool_scale"][i] = _pool_bwd(
                f"pool_bwd{layer}", st["pooled"], small["ab_w_pool"][i], small["ab_pool_scale"][i:i + 1], dcat, dz)
            dh = _mm(f"ab_in_bwd{layer}", "nt", dz, W.get("ab_w_in", i), tk=1280)
            dw(f"ab_in_dw{layer}", "ab_w_in", i, st["h"], dz, tn=640)
        else:
            datt = _mm(f"c_out_bwd{layer}", "nt", dx, W.get("c_w_out", i), out_dtype=BF16)
            dw(f"c_out_dw{layer}", "c_w_out", i, st["att"], dx)
            dq, dk, dv, dbias = _attn_bwd(f"attn_bwd{layer}", st["qkv"], st["bias"], datt)
            gs["c_rel_bias"][i] = _bias_grad(f"bias_grad{layer}", dbias.reshape(ATT_HEADS, _REL_COLS), rel_idx)
            dqkv = [dq, dk, dv]
            dh = _mm(f"qkv_bwd{layer}", "nt", dqkv, W.get("c_w_qkv", i))
            dw(f"qkv_dw{layer}", "c_w_qkv", i, st["h"], dqkv)
        dx, gs["mix_norm"][layer] = _rms_bwd(f"mix_norm_bwd{layer}", st["x_in"], small["mix_norm"][layer:layer + 1], dh, dx)

    g_small = {
        "mix_norm": jnp.concatenate(gs["mix_norm"], axis=0),
        "ffn_norm": jnp.concatenate(gs["ffn_norm"], axis=0),
        "ab_gn_gain": jnp.concatenate(gs["ab_gn_gain"], axis=0),
        "ab_w_pool": jnp.stack(gs["ab_w_pool"], axis=0),
        "ab_pool_scale": jnp.concatenate(gs["ab_pool_scale"], axis=0),
        "c_rel_bias": jnp.stack(gs["c_rel_bias"], axis=0),
        "final_norm": d_final.reshape(D_MODEL),
    }
    return loss, dx, g_small, gb


_BIG = ("w_ffn_in", "w_ffn_out", "ab_w_in", "ab_w_out", "c_w_qkv", "c_w_out")
_SHARD_AXIS = {"w_ffn_in": 2, "w_ffn_out": 1, "ab_w_in": 2, "ab_w_out": 1, "c_w_qkv": 2, "c_w_out": 1}
_SMALL = ("mix_norm", "ffn_norm", "ab_gn_gain", "ab_w_pool", "ab_pool_scale", "c_rel_bias", "final_norm")


def _place():
    x, y, c = lax.axis_index("x"), lax.axis_index("y"), lax.axis_index("c")
    chips = [(1 - x, y), (x, 1 - y), (1 - x, 1 - y)]
    return x, y, c, chips


def _sub(ref, axis, start, size):
    idx = [slice(None)] * len(ref.shape)
    idx[axis] = pl.ds(pl.multiple_of(start, LANE), size)
    return ref.at[tuple(idx)]


def _gather_rider(items, shards):
    keys = sorted({k for k, _ in items})
    n = len(items)
    axes = [_SHARD_AXIS[k] - 1 for k, _ in items]
    sizes = [shards[k].shape[a + 1] for (k, _), a in zip(items, axes)]
    hsizes = [shards[k].shape[2 - a] // 2 for (k, _), a in zip(items, axes)]

    def views(ins, outs, send_sems, recv_sems):
        x, y, c, chips = _place()
        srcs = [ins[keys.index(k)].at[l] for k, l in items]

        def remote(src, dst, s, to):
            return pltpu.make_async_remote_copy(src_ref=src, dst_ref=dst, send_sem=send_sems.at[s],
                                                recv_sem=recv_sems.at[s], device_id=to, device_id_type=MESH)

        def half(w, chip, core):
            return _sub(_sub(outs[w], axes[w], chip * sizes[w], sizes[w]), 1 - axes[w], core * hsizes[w], hsizes[w])

        me = 2 * x + y
        local = [pltpu.make_async_copy(srcs[w], _sub(outs[w], axes[w], me * sizes[w], sizes[w]), send_sems.at[6 * n + w])
                 for w in range(n)]
        first = [remote(_sub(srcs[w], 1 - axes[w], c * hsizes[w], hsizes[w]), half(w, me, c), w * 6 + k, (px, py, c))
                 for w in range(n) for k, (px, py) in enumerate(chips)]
        return x, y, c, chips, remote, half, local, first

    def start(ins, outs, send_sems, recv_sems):
        *_, local, first = views(ins, outs, send_sems, recv_sems)
        for cp in local + first:
            cp.start()

    def finish(ins, outs, send_sems, recv_sems):
        x, y, c, chips, remote, half, local, first = views(ins, outs, send_sems, recv_sems)
        sibling = (x, y, 1 - c)
        passed = []
        for w in range(n):
            for k, (px, py) in enumerate(chips):
                landed = half(w, 2 * px + py, c)
                remote(landed, landed, w * 6 + k, (px, py, c)).wait_recv()
                cp = remote(landed, landed, w * 6 + 3 + k, sibling)
                cp.start()
                passed.append(cp)
        for w in range(n):
            for k, (px, py) in enumerate(chips):
                theirs = half(w, 2 * px + py, 1 - c)
                remote(theirs, theirs, w * 6 + 3 + k, sibling).wait_recv()
        for cp in first + passed:
            cp.wait_send()
        for cp in local:
            cp.wait()

    def full(k, a):
        shape = list(shards[k].shape[1:])
        shape[a] *= N_CHIPS
        return jax.ShapeDtypeStruct(tuple(shape), shards[k].dtype)

    return _Rider(tuple(shards[k] for k in keys), tuple(full(k, a) for (k, _), a in zip(items, axes)), 7 * n, start, finish)


def _mixer_items(layer):
    names = ("ab_w_in", "ab_w_out") if layer % 2 == 0 else ("c_w_qkv", "c_w_out")
    return [(k, layer // 2) for k in names]


class _Weights:
    def __init__(self, shards):
        self.shards = shards
        self.n_layers = {k: shards[k].shape[0] for k in _BIG}
        self.full = {}
        first = _mixer_items(0)
        self._take(first, _run_rider("gather_first", _gather_rider(first, shards)))
        self.plan = {"ab_in_fwd0": [("w_ffn_in", 0)], "ab_out_fwd0": [("w_ffn_out", 0)]}
        for layer in range(1, DEPTH):
            proj = "ab_in" if layer % 2 == 0 else "qkv"
            self.plan[f"ffn_in_fwd{layer - 1}"] = _mixer_items(layer)
            self.plan[f"ffn_out_fwd{layer - 1}"] = [("w_ffn_in", layer)]
            self.plan[f"{proj}_fwd{layer}"] = [("w_ffn_out", layer)]

    def _take(self, items, outs):
        self.full.update(zip(items, outs))

    def get(self, name, layer):
        return self.full[(name, layer)]

    def mm(self, name, *args, **kw):
        items = self.plan.get(name)
        if items is None:
            return _mm(name, *args, **kw)
        res, outs = _mm(name, *args, rider=_gather_rider(items, self.shards), **kw)
        self._take(items, outs)
        return res


def _run_rider(name, rider):
    n_in, n_out = len(rider.operands), len(rider.out_shapes)

    def body(*refs):
        ins, outs, sems = refs[:n_in], refs[n_in:n_in + n_out], refs[n_in + n_out:]
        rider.start(ins, outs, *sems)
        rider.finish(ins, outs, *sems)

    return pl.pallas_call(
        body,
        name=name,
        in_specs=[_ANY] * n_in,
        out_specs=[_ANY] * n_out,
        out_shape=list(rider.out_shapes),
        scratch_shapes=[pltpu.SemaphoreType.DMA((rider.n_sems,)), pltpu.SemaphoreType.DMA((rider.n_sems,))],
        compiler_params=pltpu.CompilerParams(has_side_effects=True),
    )(*rider.operands)


def _half_axis(name):
    return 3 - _SHARD_AXIS[name]


def _pair_exchange(grads):
    n = len(_BIG)
    haxes = [_half_axis(k) for k in _BIG]
    halves = [g.shape[a] // 2 for g, a in zip(grads, haxes)]

    def body(*refs):
        ins, outs = refs[:n], refs[n:2 * n]
        send_sems, recv_sems = refs[2 * n:]
        x, y, c, _ = _place()
        cps = []
        for w in range(n):
            src = _sub(ins[w], haxes[w], (1 - c) * halves[w], halves[w])
            cp = pltpu.make_async_remote_copy(src_ref=src, dst_ref=outs[w], send_sem=send_sems.at[w],
                                              recv_sem=recv_sems.at[w], device_id=(x, y, 1 - c), device_id_type=MESH)
            cp.start()
            cps.append(cp)
        for cp in cps:
            cp.wait_recv()
        for cp in cps:
            cp.wait_send()

    def half(g, a):
        shape = list(g.shape)
        shape[a] //= 2
        return jax.ShapeDtypeStruct(tuple(shape), g.dtype)

    return pl.pallas_call(
        body,
        name="pair_exchange",
        in_specs=[_ANY] * n,
        out_specs=[_ANY] * n,
        out_shape=[half(g, a) for g, a in zip(grads, haxes)],
        scratch_shapes=[pltpu.SemaphoreType.DMA((n,)), pltpu.SemaphoreType.DMA((n,))],
        compiler_params=pltpu.CompilerParams(has_side_effects=True),
    )(*grads)


def _rows_tile(rows, cols):
    tr = rows
    while tr * cols > (1 << 19) and tr % 16 == 0:
        tr //= 2
    return tr


def _pair_sum(name, g, recv, c_arr, haxis):
    L, R, C = recv.shape
    tr = _rows_tile(R, C)
    nr = R // tr
    if haxis == 1:
        g_idx = lambda l, i, s: (l, s[0] * nr + i, 0)
    else:
        g_idx = lambda l, i, s: (l, i, s[0])

    def body(s_ref, g_ref, r_ref, o_ref):
        o_ref[...] = (g_ref[...] + r_ref[...]).astype(o_ref.dtype)

    return pl.pallas_call(
        body,
        name=name,
        grid_spec=pltpu.PrefetchScalarGridSpec(
            num_scalar_prefetch=1,
            grid=(L, nr),
            in_specs=[pl.BlockSpec((None, tr, C), g_idx), pl.BlockSpec((None, tr, C), lambda l, i, s: (l, i, 0))],
            out_specs=pl.BlockSpec((None, tr, C), lambda l, i, s: (l, i, 0)),
        ),
        out_shape=jax.ShapeDtypeStruct((L, R, C), BF16),
        compiler_params=_cp(("parallel", "parallel")),
    )(c_arr, g, recv)


def _chip_exchange(psums):
    n = len(_BIG)
    axes = [_SHARD_AXIS[k] for k in _BIG]
    sizes = [p.shape[a] // N_CHIPS for p, a in zip(psums, axes)]

    def body(*refs):
        ins, outs = refs[:n], refs[n:2 * n]
        send_sems, recv_sems = refs[2 * n:]
        x, y, c, chips = _place()
        cps = []
        for w in range(n):
            for k, (px, py) in enumerate(chips):
                s = w * 3 + k
                src = _sub(ins[w], axes[w], (2 * px + py) * sizes[w], sizes[w])
                cp = pltpu.make_async_remote_copy(src_ref=src, dst_ref=outs[w].at[k], send_sem=send_sems.at[s],
                                                  recv_sem=recv_sems.at[s], device_id=(px, py, c), device_id_type=MESH)
                cp.start()
                cps.append(cp)
        for cp in cps:
            cp.wait_recv()
        for cp in cps:
            cp.wait_send()

    def landing(p, a):
        shape = list(p.shape)
        shape[a] //= N_CHIPS
        return jax.ShapeDtypeStruct((3,) + tuple(shape), p.dtype)

    return pl.pallas_call(
        body,
        name="chip_exchange",
        in_specs=[_ANY] * n,
        out_specs=[_ANY] * n,
        out_shape=[landing(p, a) for p, a in zip(psums, axes)],
        scratch_shapes=[pltpu.SemaphoreType.DMA((3 * n,)), pltpu.SemaphoreType.DMA((3 * n,))],
        compiler_params=pltpu.CompilerParams(has_side_effects=True),
    )(*psums)


def _chip_sum(name, psum, landed, pos, saxis):
    L = psum.shape[0]
    _, _, R, C = landed.shape
    tr = _rows_tile(R, C)
    nr = R // tr
    if saxis == 2:
        p_idx = lambda l, i, s: (l, i, s[0])
        o_idx = lambda l, i, s: (l, s[1] * nr + i, 0)
        o_shape = (L, 2 * R, C)
    else:
        p_idx = lambda l, i, s: (l, s[0] * nr + i, 0)
        o_idx = lambda l, i, s: (l, i, s[1])
        o_shape = (L, R, 2 * C)

    def body(s_ref, p_ref, l_ref, o_ref):
        o_ref[...] = ((p_ref[...].astype(F32) + l_ref[0].astype(F32)) + l_ref[1].astype(F32)) + l_ref[2].astype(F32)

    return pl.pallas_call(
        body,
        name=name,
        grid_spec=pltpu.PrefetchScalarGridSpec(
            num_scalar_prefetch=1,
            grid=(L, nr),
            in_specs=[pl.BlockSpec((None, tr, C), p_idx), pl.BlockSpec((3, None, tr, C), lambda l, i, s: (0, l, i, 0))],
            out_specs=pl.BlockSpec((None, tr, C), o_idx),
        ),
        out_shape=jax.ShapeDtypeStruct(o_shape, F32),
        compiler_params=_cp(("parallel", "parallel")),
    )(pos, psum, landed)


def _sibling_fill(halves):
    n = len(_BIG)
    haxes = [_half_axis(k) for k in _BIG]
    sizes = [h.shape[a] // 2 for h, a in zip(halves, haxes)]

    def body(*refs):
        outs = refs[n:2 * n]
        send_sems, recv_sems = refs[2 * n:]
        x, y, c, _ = _place()
        sends, recvs = [], []
        for w in range(n):
            mine = _sub(outs[w], haxes[w], c * sizes[w], sizes[w])
            theirs = _sub(outs[w], haxes[w], (1 - c) * sizes[w], sizes[w])
            cp = pltpu.make_async_remote_copy(src_ref=mine, dst_ref=mine, send_sem=send_sems.at[w],
                                              recv_sem=recv_sems.at[w], device_id=(x, y, 1 - c), device_id_type=MESH)
            cp.start()
            sends.append(cp)
            recvs.append(pltpu.make_async_remote_copy(src_ref=theirs, dst_ref=theirs, send_sem=send_sems.at[w],
                                                      recv_sem=recv_sems.at[w], device_id=(x, y, 1 - c), device_id_type=MESH))
        for cp in recvs:
            cp.wait_recv()
        for cp in sends:
            cp.wait_send()

    return pl.pallas_call(
        body,
        name="sibling_fill",
        in_specs=[_ANY] * n,
        out_specs=[_ANY] * n,
        out_shape=[jax.ShapeDtypeStruct(h.shape, h.dtype) for h in halves],
        input_output_aliases={w: w for w in range(n)},
        scratch_shapes=[pltpu.SemaphoreType.DMA((n,)), pltpu.SemaphoreType.DMA((n,))],
        compiler_params=pltpu.CompilerParams(has_side_effects=True),
    )(*halves)


def _all_reduce_small(packed):
    R = packed.shape[0]

    def body(p_ref, o_ref, land_ref, send_sems, recv_sems):
        x, y, c, _ = _place()
        me = 4 * x + 2 * y + c
        sends, recvs = [], []
        for r in range(1, N_DEV):
            px, py, pc = x ^ (r >> 2), y ^ ((r >> 1) & 1), c ^ (r & 1)
            cp = pltpu.make_async_remote_copy(src_ref=p_ref, dst_ref=land_ref.at[me], send_sem=send_sems.at[r - 1],
                                              recv_sem=recv_sems.at[r - 1], device_id=(px, py, pc), device_id_type=MESH)
            cp.start()
            sends.append(cp)
            recvs.append(pltpu.make_async_remote_copy(src_ref=p_ref, dst_ref=land_ref.at[4 * px + 2 * py + pc],
                                                      send_sem=send_sems.at[r - 1], recv_sem=recv_sems.at[r - 1],
                                                      device_id=(px, py, pc), device_id_type=MESH))
        land_ref[me] = p_ref[...]
        for cp in recvs:
            cp.wait_recv()
        for cp in sends:
            cp.wait_send()
        acc = land_ref[0]
        for d in range(1, N_DEV):
            acc = acc + land_ref[d]
        o_ref[...] = acc

    vm = pl.BlockSpec(memory_space=pltpu.VMEM)
    return pl.pallas_call(
        body,
        name="all_reduce_small",
        in_specs=[vm],
        out_specs=vm,
        out_shape=jax.ShapeDtypeStruct((R, LANE), F32),
        scratch_shapes=[pltpu.VMEM((N_DEV, R, LANE), F32), pltpu.SemaphoreType.DMA((N_DEV - 1,)),
                        pltpu.SemaphoreType.DMA((N_DEV - 1,))],
        compiler_params=pltpu.CompilerParams(has_side_effects=True, vmem_limit_bytes=VMEM_LIMIT),
    )(packed)


def _adamw(name, w, g, m, v):
    R, C = w.shape
    tr = _rows_tile(R, C)
    c1 = 1.0 - ADAM_B1 ** ADAM_STEP
    c2 = 1.0 - ADAM_B2 ** ADAM_STEP

    def body(w_ref, g_ref, m_ref, v_ref, d_ref, nm_ref, nv_ref):
        gv = g_ref[...]
        nm = ADAM_B1 * m_ref[...] + (1.0 - ADAM_B1) * gv
        nv = ADAM_B2 * v_ref[...] + (1.0 - ADAM_B2) * (gv * gv)
        nm_ref[...] = nm
        nv_ref[...] = nv
        d_ref[...] = -ADAM_LR * ((nm / c1) / (jnp.sqrt(nv / c2) + ADAM_EPS) + ADAM_WD * w_ref[...])

    blk = pl.BlockSpec((tr, C), lambda i: (i, 0))
    out = jax.ShapeDtypeStruct((R, C), F32)
    return pl.pallas_call(
        body,
        name=name,
        grid=(R // tr,),
        in_specs=[blk] * 4,
        out_specs=[blk] * 3,
        out_shape=[out] * 3,
        compiler_params=_cp(("parallel",)),
    )(w, g, m, v)


def _pack(parts):
    rows = []
    for p in parts:
        flat = p.reshape(-1).astype(F32)
        n = -(-flat.shape[0] // (8 * LANE)) * (8 * LANE)
        rows.append(jnp.pad(flat, (0, n - flat.shape[0])).reshape(n // LANE, LANE))
    return jnp.concatenate(rows, axis=0)


def _unpack(packed, like):
    out, r = [], 0
    for p in like:
        size = int(np.prod(p.shape))
        n = -(-size // (8 * LANE)) * 8
        out.append(packed[r:r + n].reshape(-1)[:size].reshape(p.shape))
        r += n
    return out


def kernel(x, mix_norm, ffn_norm, w_ffn_in, w_ffn_out, ab_w_in, ab_gn_gain, ab_w_pool, ab_pool_scale, ab_w_out, c_w_qkv, c_rel_bias, c_w_out, final_norm, loss_target, m_mix_norm, m_ffn_norm, m_w_ffn_in, m_w_ffn_out, m_ab_w_in, m_ab_gn_gain, m_ab_w_pool, m_ab_pool_scale, m_ab_w_out, m_c_w_qkv, m_c_rel_bias, m_c_w_out, m_final_norm, v_mix_norm, v_ffn_norm, v_w_ffn_in, v_w_ffn_out, v_ab_w_in, v_ab_gn_gain, v_ab_w_pool, v_ab_pool_scale, v_ab_w_out, v_c_w_qkv, v_c_rel_bias, v_c_w_out, v_final_norm):
    w = dict(mix_norm=mix_norm, ffn_norm=ffn_norm, w_ffn_in=w_ffn_in, w_ffn_out=w_ffn_out, ab_w_in=ab_w_in,
             ab_gn_gain=ab_gn_gain, ab_w_pool=ab_w_pool, ab_pool_scale=ab_pool_scale, ab_w_out=ab_w_out,
             c_w_qkv=c_w_qkv, c_rel_bias=c_rel_bias, c_w_out=c_w_out, final_norm=final_norm)
    m = dict(mix_norm=m_mix_norm, ffn_norm=m_ffn_norm, w_ffn_in=m_w_ffn_in, w_ffn_out=m_w_ffn_out, ab_w_in=m_ab_w_in,
             ab_gn_gain=m_ab_gn_gain, ab_w_pool=m_ab_w_pool, ab_pool_scale=m_ab_pool_scale, ab_w_out=m_ab_w_out,
             c_w_qkv=m_c_w_qkv, c_rel_bias=m_c_rel_bias, c_w_out=m_c_w_out, final_norm=m_final_norm)
    v = dict(mix_norm=v_mix_norm, ffn_norm=v_ffn_norm, w_ffn_in=v_w_ffn_in, w_ffn_out=v_w_ffn_out, ab_w_in=v_ab_w_in,
             ab_gn_gain=v_ab_gn_gain, ab_w_pool=v_ab_w_pool, ab_pool_scale=v_ab_pool_scale, ab_w_out=v_ab_w_out,
             c_w_qkv=v_c_w_qkv, c_rel_bias=v_c_rel_bias, c_w_out=v_c_w_out, final_norm=v_final_norm)
    S = x.shape[1]
    cx, cy, cc = lax.axis_index("x"), lax.axis_index("y"), lax.axis_index("c")
    core = jnp.reshape(cc, (1,)).astype(jnp.int32)
    pos = jnp.stack([2 * cx + cy, cc]).astype(jnp.int32)

    big = _Weights({k: w[k].astype(BF16) for k in _BIG})
    small = {k: w[k] for k in _SMALL}
    loss, grad_x, g_small, g_big = _local_step(x.reshape(S, D_MODEL), loss_target.reshape(S, D_MODEL), small, big)

    grads = [g_big[k] for k in _BIG]
    from_sibling = _pair_exchange(grads)
    psums = [_pair_sum(f"pair_sum_{k}", g, r, core, _half_axis(k)) for k, g, r in zip(_BIG, grads, from_sibling)]
    landed = _chip_exchange(psums)
    halves = [_chip_sum(f"chip_sum_{k}", p, l, pos, _SHARD_AXIS[k]) for k, p, l in zip(_BIG, psums, landed)]
    reduced = dict(zip(_BIG, _sibling_fill(halves)))

    packed = _all_reduce_small(_pack([g_small[k] for k in _SMALL] + [loss]))
    small_like = [w[k] for k in _SMALL]
    g_red = dict(zip(_SMALL, _unpack(packed, small_like)))
    loss_row = packed.shape[0] - 8
    loss_out = packed[loss_row, 0]

    grad, delta, new_m, new_v = {}, {}, {}, {}
    for k in _BIG:
        shp = w[k].shape
        two = (shp[0] * shp[1], shp[2])
        d, nm, nv = _adamw(f"adamw_{k}", w[k].reshape(two), reduced[k].reshape(two), m[k].reshape(two), v[k].reshape(two))
        grad[k], delta[k], new_m[k], new_v[k] = reduced[k], d.reshape(shp), nm.reshape(shp), nv.reshape(shp)
    d, nm, nv = _adamw("adamw_small", _pack(small_like), packed[:loss_row], _pack([m[k] for k in _SMALL]),
                       _pack([v[k] for k in _SMALL]))
    for k, dk, mk, vk in zip(_SMALL, _unpack(d, small_like), _unpack(nm, small_like), _unpack(nv, small_like)):
        grad[k], delta[k], new_m[k], new_v[k] = g_red[k], dk, mk, vk

    order = ("mix_norm", "ffn_norm", "w_ffn_in", "w_ffn_out", "ab_w_in", "ab_gn_gain", "ab_w_pool", "ab_pool_scale",
             "ab_w_out", "c_w_qkv", "c_rel_bias", "c_w_out", "final_norm")
    return (loss_out, grad_x.reshape(x.shape), *[grad[k] for k in order], *[delta[k] for k in order],
            *[new_m[k] for k in order], *[new_v[k] for k in order])
```

```python
import functools
from typing import Callable, NamedTuple

import numpy as np
import jax
import jax.numpy as jnp
from jax import lax
from jax.experimental import pallas as pl
from jax.experimental.pallas import tpu as pltpu

F32 = jnp.float32
BF16 = jnp.bfloat16

D_MODEL = 1024
D_FF = 4096
DEPTH = 4
CHUNK = 64
RMS_EPS = 1e-6
RET_WIDTH = 512
RET_HEADS = 4
RET_HEAD_DIM = 128
RET_ROPE_BASE = 10000.0
GN_EPS = 1e-5
POOL_WIDTH = 512
POOL_WINDOWS = (2, 4, 8, 16)
POOL_GROUP_DIM = 128
POOL_HALO = 16
AB_IN_WIDTH = 2560
ATT_HEADS = 16
ATT_HEAD_DIM = 64
LEFT_CHUNKS = 8
BAND = (LEFT_CHUNKS + 1) * CHUNK
REL_CLIP = 128
N_REL = 2 * REL_CLIP + 1
N_REL_PAD = 264
NEG_INF = -1e30
KSCALE = RET_HEAD_DIM ** -0.5
QSCALE = ATT_HEAD_DIM ** -0.5

ADAM_LR = 0.001
ADAM_B1 = 0.9
ADAM_B2 = 0.999
ADAM_EPS = 1e-08
ADAM_WD = 0.01
ADAM_STEP = 10

ATT_BLOCK = LEFT_CHUNKS * CHUNK
RET_BLOCK = 512
N_CHIPS = 4
N_DEV = 8
LANE = 128
VMEM_LIMIT = 52 * 1024 * 1024
MESH = pl.DeviceIdType.MESH


def _cp(sem, vmem=VMEM_LIMIT):
    return pltpu.CompilerParams(dimension_semantics=sem, vmem_limit_bytes=vmem)


def _dot(a, b):
    return lax.dot_general(a, b, (((1,), (0,)), ((), ())), preferred_element_type=F32)


def _dot_nt(a, b):
    return lax.dot_general(a, b, (((1,), (1,)), ((), ())), preferred_element_type=F32)


def _dot_tn(a, b):
    return lax.dot_general(a, b, (((0,), (0,)), ((), ())), preferred_element_type=F32)


_ANY = pl.BlockSpec(memory_space=pl.ANY)


class _Rider(NamedTuple):
    operands: tuple
    out_shapes: tuple
    n_sems: int
    start: Callable
    finish: Callable
    aliases: tuple = ()


def _mm(name, mode, a, b, *, la=None, lb=None, tm=1024, tn=1024, tk=1024, a_fn=None, b_fn=None,
        extras=(), epi=None, out_dtype=F32, stack=None, rider=None):
    a_parts = list(a) if isinstance(a, (list, tuple)) else [a]
    b_parts = list(b) if isinstance(b, (list, tuple)) else [b]
    na, nbp = len(a_parts), len(b_parts)
    a2, b2 = list(a_parts[0].shape[-2:]), list(b_parts[0].shape[-2:])
    a2[1] *= na
    b2[1] *= nbp
    if mode == "nn":
        (M, K), (K2, N) = a2, b2
    elif mode == "nt":
        (M, K), (N, K2) = a2, b2
    else:
        (K, M), (K2, N) = a2, b2
    assert K == K2, (name, a2, b2)
    tm, tn, tk = min(tm, M), min(tn, N), min(tk, K)
    assert M % tm == 0 and N % tn == 0 and K % tk == 0, (name, M, N, K, tm, tn, tk)
    gm, gn, gk = M // tm, N // tn, K // tk

    def specs(parts, block, idx, lead):
        per = parts[0].shape[-1] // block[1]
        assert parts[0].shape[-1] % block[1] == 0, (name, parts[0].shape, block)
        out = []
        for p in range(len(parts)):
            def f(i, j, k, p=p):
                r, c = idx(i, j, k)
                if len(parts) > 1:
                    c = jnp.clip(c - p * per, 0, per - 1)
                return (r, c) if lead is None else (lead, r, c)
            out.append(pl.BlockSpec(block if lead is None else (None,) + block, f))
        return out, per

    if mode == "nn":
        a_specs, a_per = specs(a_parts, (tm, tk), lambda i, j, k: (i, k), la)
        b_specs, b_per = specs(b_parts, (tk, tn), lambda i, j, k: (k, j), lb)
        a_axis, b_axis, dot = 2, 1, _dot
    elif mode == "nt":
        a_specs, a_per = specs(a_parts, (tm, tk), lambda i, j, k: (i, k), la)
        b_specs, b_per = specs(b_parts, (tn, tk), lambda i, j, k: (j, k), lb)
        a_axis, b_axis, dot = 2, 2, _dot_nt
    else:
        a_specs, a_per = specs(a_parts, (tk, tm), lambda i, j, k: (k, i), la)
        b_specs, b_per = specs(b_parts, (tk, tn), lambda i, j, k: (k, j), lb)
        a_axis, b_axis, dot = 0, 1, _dot_tn
    ex_specs = [pl.BlockSpec((tm, tn), lambda i, j, k: (i, j)) for _ in extras]
    n_ex = len(extras)

    operands = a_parts + b_parts + list(extras)
    in_specs = a_specs + b_specs + ex_specs
    aliases = {}
    if stack is None:
        out_specs = [pl.BlockSpec((tm, tn), lambda i, j, k: (i, j))]
        out_shapes = [jax.ShapeDtypeStruct((M, N), out_dtype)]
    else:
        n_layers, layer, prev = stack
        out_specs = [pl.BlockSpec((None, tm, tn), lambda i, j, k: (layer, i, j))]
        out_shapes = [jax.ShapeDtypeStruct((n_layers, M, N), out_dtype)]
        if prev is not None:
            aliases = {len(operands): 0}
            operands.append(prev)
            in_specs.append(_ANY)
    n_prev = len(aliases)
    scratch = [pltpu.VMEM((tm, tn), F32)] if gk > 1 else []
    n_rin = n_rout = 0
    if rider is not None:
        n_rin, n_rout = len(rider.operands), len(rider.out_shapes)
        for src, dst in rider.aliases:
            aliases[len(operands) + src] = 1 + dst
        operands += list(rider.operands)
        in_specs += [_ANY] * n_rin
        out_specs += [_ANY] * n_rout
        out_shapes += list(rider.out_shapes)
        scratch += [pltpu.SemaphoreType.DMA((rider.n_sems,)), pltpu.SemaphoreType.DMA((rider.n_sems,))]
    assert na == 1 or nbp == 1, name

    def body(*refs):
        a_refs, b_refs = refs[:na], refs[na:na + nbp]
        ex_refs = refs[na + nbp:na + nbp + n_ex]
        n_in = na + nbp + n_ex + n_prev
        rin = refs[n_in:n_in + n_rin]
        o_ref = refs[n_in + n_rin]
        rout = refs[n_in + n_rin + 1:n_in + n_rin + 1 + n_rout]
        rest = refs[n_in + n_rin + 1 + n_rout:]
        i, j, k = pl.program_id(0), pl.program_id(1), pl.program_id(2)
        if rider is not None:
            sems = rest[-2:]

            @pl.when(jnp.logical_and(i == 0, jnp.logical_and(j == 0, k == 0)))
            def _():
                rider.start(rin, rout, *sems)

        def finish(acc):
            if epi is not None:
                acc = epi(acc, *[r[...] for r in ex_refs])
            o_ref[...] = acc.astype(o_ref.dtype)

        def step(a_ref, b_ref):
            av, bv = a_ref[...], b_ref[...]
            if a_fn is not None:
                av = a_fn(av)
            if b_fn is not None:
                bv = b_fn(bv)
            part = dot(av.astype(BF16), bv.astype(BF16))
            if gk == 1:
                finish(part)
                return
            acc_ref = rest[0]

            @pl.when(k == 0)
            def _():
                acc_ref[...] = part

            @pl.when(k > 0)
            def _():
                acc_ref[...] += part

        if na > 1:
            sel = pl.program_id(a_axis) // a_per
            for p in range(na):
                pl.when(sel == p)(functools.partial(step, a_refs[p], b_refs[0]))
        elif nbp > 1:
            sel = pl.program_id(b_axis) // b_per
            for p in range(nbp):
                pl.when(sel == p)(functools.partial(step, a_refs[0], b_refs[p]))
        else:
            step(a_refs[0], b_refs[0])
        if gk > 1:
            @pl.when(k == gk - 1)
            def _():
                finish(rest[0][...])

        if rider is not None:
            @pl.when(jnp.logical_and(i == gm - 1, jnp.logical_and(j == gn - 1, k == gk - 1)))
            def _():
                rider.finish(rin, rout, *sems)

    sem = ("arbitrary",) * 3 if rider is not None else ("parallel", "parallel", "arbitrary")
    outs = pl.pallas_call(
        body,
        name=name,
        grid=(gm, gn, gk),
        in_specs=in_specs,
        out_specs=out_specs,
        out_shape=out_shapes,
        input_output_aliases=aliases,
        scratch_shapes=scratch,
        compiler_params=_cp(sem),
    )(*operands)
    return outs[0] if rider is None else (outs[0], list(outs[1:]))


def _relu2(u):
    r = jnp.maximum(u, 0)
    return r * r


def _rms_fwd(name, x, g):
    S, D = x.shape
    tq = min(1024, S)

    def body(x_ref, g_ref, o_ref):
        xv = x_ref[...]
        r = lax.rsqrt(jnp.mean(xv * xv, axis=-1, keepdims=True) + RMS_EPS)
        o_ref[...] = ((xv * r) * g_ref[...]).astype(o_ref.dtype)

    return pl.pallas_call(
        body,
        name=name,
        grid=(S // tq,),
        in_specs=[pl.BlockSpec((tq, D), lambda i: (i, 0)), pl.BlockSpec((1, D), lambda i: (0, 0))],
        out_specs=pl.BlockSpec((tq, D), lambda i: (i, 0)),
        out_shape=jax.ShapeDtypeStruct((S, D), BF16),
        compiler_params=_cp(("parallel",)),
    )(x, g)


def _rms_bwd(name, x, g, dh, dres):
    S, D = x.shape
    tq = min(512, S)
    n = S // tq

    def body(x_ref, g_ref, dh_ref, dres_ref, dx_ref, dg_ref, acc_ref):
        i = pl.program_id(0)
        xv = x_ref[...]
        r = lax.rsqrt(jnp.mean(xv * xv, axis=-1, keepdims=True) + RMS_EPS)
        xh = xv * r
        dh_v = dh_ref[...].astype(F32)
        dxh = dh_v * g_ref[...]
        dx = r * (dxh - xh * jnp.mean(dxh * xh, axis=-1, keepdims=True))
        dx_ref[...] = dres_ref[...] + dx
        part = jnp.sum((dh_v * xh).reshape(tq // 8, 8, D), axis=0)

        @pl.when(i == 0)
        def _():
            acc_ref[...] = part

        @pl.when(i > 0)
        def _():
            acc_ref[...] += part

        @pl.when(i == n - 1)
        def _():
            dg_ref[...] = jnp.sum(acc_ref[...], axis=0, keepdims=True)

    return pl.pallas_call(
        body,
        name=name,
        grid=(n,),
        in_specs=[pl.BlockSpec((tq, D), lambda i: (i, 0)), pl.BlockSpec((1, D), lambda i: (0, 0)),
                  pl.BlockSpec((tq, D), lambda i: (i, 0)), pl.BlockSpec((tq, D), lambda i: (i, 0))],
        out_specs=[pl.BlockSpec((tq, D), lambda i: (i, 0)), pl.BlockSpec((1, D), lambda i: (0, 0))],
        out_shape=[jax.ShapeDtypeStruct((S, D), F32), jax.ShapeDtypeStruct((1, D), F32)],
        scratch_shapes=[pltpu.VMEM((8, D), F32)],
        compiler_params=_cp(("arbitrary",)),
    )(x, g, dh, dres)


def _loss_head(x, g, t):
    S, D = x.shape
    tq = min(512, S)
    n = S // tq

    def body(x_ref, g_ref, t_ref, loss_ref, dx_ref, dg_ref, lacc_ref, gacc_ref):
        i = pl.program_id(0)
        xv = x_ref[...]
        gv = g_ref[...]
        r = lax.rsqrt(jnp.mean(xv * xv, axis=-1, keepdims=True) + RMS_EPS)
        xh = xv * r
        e = xh * gv - t_ref[...]
        dy = e * (1.0 / D)
        dxh = dy * gv
        dx_ref[...] = r * (dxh - xh * jnp.mean(dxh * xh, axis=-1, keepdims=True))
        lpart = jnp.sum((e * e).reshape(tq // 8, 8, D), axis=0)
        gpart = jnp.sum((dy * xh).reshape(tq // 8, 8, D), axis=0)

        @pl.when(i == 0)
        def _():
            lacc_ref[...] = lpart
            gacc_ref[...] = gpart

        @pl.when(i > 0)
        def _():
            lacc_ref[...] += lpart
            gacc_ref[...] += gpart

        @pl.when(i == n - 1)
        def _():
            dg_ref[...] = jnp.sum(gacc_ref[...], axis=0, keepdims=True)
            tot = jnp.sum(jnp.sum(lacc_ref[...], axis=0, keepdims=True), axis=1, keepdims=True)
            loss_ref[...] = jnp.broadcast_to(tot * (0.5 / D), (1, LANE))

    return pl.pallas_call(
        body,
        name="loss_head",
        grid=(n,),
        in_specs=[pl.BlockSpec((tq, D), lambda i: (i, 0)), pl.BlockSpec((1, D), lambda i: (0, 0)),
                  pl.BlockSpec((tq, D), lambda i: (i, 0))],
        out_specs=[pl.BlockSpec((1, LANE), lambda i: (0, 0)), pl.BlockSpec((tq, D), lambda i: (i, 0)),
                   pl.BlockSpec((1, D), lambda i: (0, 0))],
        out_shape=[jax.ShapeDtypeStruct((1, LANE), F32), jax.ShapeDtypeStruct((S, D), F32),
                   jax.ShapeDtypeStruct((1, D), F32)],
        scratch_shapes=[pltpu.VMEM((8, D), F32), pltpu.VMEM((8, D), F32)],
        compiler_params=_cp(("arbitrary",)),
    )(x, g, t)


def _ret_tables(S):
    T = min(RET_BLOCK, S)
    inv_freq = 1.0 / (RET_ROPE_BASE ** jnp.linspace(0.0, 1.0, RET_HEAD_DIM // 2, dtype=F32))
    ang = jnp.arange(S, dtype=F32)[:, None] * inv_freq[None, :]
    cos, sin = jnp.cos(ang), jnp.sin(ang)
    cosf = jnp.repeat(cos, 2, axis=-1)
    sins = jnp.stack([-sin, sin], axis=-1).reshape(S, RET_HEAD_DIM)
    log_g = np.log1p(-np.power(2.0, -5.0 - np.arange(RET_HEADS, dtype=np.float64)))
    pos = np.arange(T, dtype=np.float64)
    diff = pos[:, None] - pos[None, :]
    same = (pos[:, None] // CHUNK) == (pos[None, :] // CHUNK)
    seen = same | (diff > 0)
    dmat = np.where(seen[None], np.exp(np.abs(diff)[None] * log_g[:, None, None]), 0.0)
    aq = np.exp((pos[None, :] + 1.0) * log_g[:, None])
    ak = np.exp((T - 1.0 - pos[None, :]) * log_g[:, None])
    lam = np.exp(T * log_g)
    bc = lambda v: jnp.asarray(np.broadcast_to(v[..., None], v.shape + (LANE,)), F32)
    return dict(cos=cosf, sin=sins, dmat=jnp.asarray(dmat, F32), aq=bc(aq), ak=bc(ak),
                lam=jnp.asarray(np.broadcast_to(lam[:, None, None], (RET_HEADS, 1, LANE)), F32))


def _rot(x, cos, sin_s, even):
    sw = jnp.where(even, pltpu.roll(x, LANE - 1, 1), pltpu.roll(x, 1, 1))
    return x * cos + sw * sin_s


def _rot_t(dy, cos, sin_s, even):
    t = dy * sin_s
    return dy * cos + jnp.where(even, pltpu.roll(t, LANE - 1, 1), pltpu.roll(t, 1, 1))


def _ret_specs(T, rev_nb=None):
    blk = (lambda b: b) if rev_nb is None else (lambda b: rev_nb - 1 - b)
    whole = lambda shape: pl.BlockSpec(shape, lambda b: (0,) * len(shape))
    specs = [pl.BlockSpec((T, AB_IN_WIDTH), lambda b: (blk(b), 0)),
             pl.BlockSpec((T, LANE), lambda b: (blk(b), 0)),
             pl.BlockSpec((T, LANE), lambda b: (blk(b), 0)),
             whole((RET_HEADS, T, T)), whole((RET_HEADS, T, LANE)), whole((RET_HEADS, T, LANE)),
             whole((RET_HEADS, 1, LANE)), whole((1, RET_WIDTH))]
    return specs, blk


def _head_views(h, z_ref, tabs, token_refs, head_refs):
    zs = [z_ref.at[:, (o * RET_HEADS + h) * LANE:(o * RET_HEADS + h + 1) * LANE] for o in range(4)]
    hs = slice(h * LANE, (h + 1) * LANE)
    return zs, [t.at[h] for t in tabs], [r.at[:, hs] for r in token_refs], [r.at[h] for r in head_refs]


def _ret_fwd(name, z, tb, gain):
    S = z.shape[0]
    T = min(RET_BLOCK, S)
    nb = S // T
    specs, blk = _ret_specs(T)

    def body(z_ref, cos_r, sin_r, d_all, aq_all, ak_all, lam_all, gain_all, cat_all, opre_all, st_all, state_all):
        @pl.when(pl.program_id(0) == 0)
        def _():
            state_all[...] = jnp.zeros_like(state_all)

        for h in range(RET_HEADS):
            zs, tabs, toks, heads = _head_views(h, z_ref, (d_all, aq_all, ak_all, lam_all),
                                                (gain_all, cat_all, opre_all), (st_all, state_all))
            head(*zs, cos_r, sin_r, *tabs, *toks, *heads)

    def head(zq, zk, zv, zg, cos_r, sin_r, d_r, aq_r, ak_r, lam_r, gain_r, ret_o, opre_o, st_o, state):
        even = (lax.broadcasted_iota(jnp.int32, (T, LANE), 1) & 1) == 0
        c, s = cos_r[...], sin_r[...]
        q = _rot(zq[...], c, s, even)
        k = _rot(zk[...], c, s, even) * KSCALE
        qb, kb, vb = q.astype(BF16), k.astype(BF16), zv[...].astype(BF16)
        p = (_dot_nt(qb, kb) * d_r[...]).astype(BF16)
        st = state[...]
        st_o[...] = st
        o = _dot(p, vb) + _dot((q * aq_r[...]).astype(BF16), st.astype(BF16))
        state[...] = st * lam_r[...] + _dot_tn((k * ak_r[...]).astype(BF16), vb)
        opre_o[...] = o
        mu = jnp.mean(o, axis=-1, keepdims=True)
        d = o - mu
        y = d * lax.rsqrt(jnp.mean(d * d, axis=-1, keepdims=True) + GN_EPS)
        g = zg[...]
        ret_o[...] = ((g * jax.nn.sigmoid(g)) * (y * gain_r[...])).astype(ret_o.dtype)

    out_blk = pl.BlockSpec((T, RET_WIDTH), lambda b: (b, 0))
    return pl.pallas_call(
        body,
        name=name,
        grid=(nb,),
        in_specs=specs,
        out_specs=[out_blk, out_blk, pl.BlockSpec((RET_HEADS, None, LANE, LANE), lambda b: (0, b, 0, 0))],
        out_shape=[jax.ShapeDtypeStruct((S, D_MODEL), BF16), jax.ShapeDtypeStruct((S, RET_WIDTH), F32),
                   jax.ShapeDtypeStruct((RET_HEADS, nb, LANE, LANE), F32)],
        scratch_shapes=[pltpu.VMEM((RET_HEADS, LANE, LANE), F32)],
        compiler_params=_cp(("arbitrary",)),
    )(z, tb["cos"], tb["sin"], tb["dmat"], tb["aq"], tb["ak"], tb["lam"], gain)


def _ret_bwd(name, z, tb, gain, opre, states, dcat):
    S = z.shape[0]
    T = min(RET_BLOCK, S)
    nb = S // T
    specs, blk = _ret_specs(T, rev_nb=nb)
    tok = pl.BlockSpec((T, RET_WIDTH), lambda b: (blk(b), 0))

    def body(z_ref, cos_r, sin_r, d_all, aq_all, ak_all, lam_all, gain_all, opre_all, st_all, dret_all,
             dz_ref, dgain_all, dstate_all):
        @pl.when(pl.program_id(0) == 0)
        def _():
            dstate_all[...] = jnp.zeros_like(dstate_all)
            dgain_all[...] = jnp.zeros_like(dgain_all)

        for h in range(RET_HEADS):
            zs, tabs, toks, heads = _head_views(h, z_ref, (d_all, aq_all, ak_all, lam_all),
                                                (gain_all, opre_all, dret_all, dgain_all), (st_all, dstate_all))
            dzs, _, _, _ = _head_views(h, dz_ref, (), (), ())
            gain_r, opre_r, dret_r, dgain_o = toks
            head(*zs, cos_r, sin_r, *tabs, gain_r, opre_r, heads[0], dret_r, *dzs, dgain_o, heads[1])

    def head(zq, zk, zv, zg, cos_r, sin_r, d_r, aq_r, ak_r, lam_r, gain_r, opre_r, st_r, dret_r,
             dq_o, dk_o, dv_o, dg_o, dgain_o, dstate):
        even = (lax.broadcasted_iota(jnp.int32, (T, LANE), 1) & 1) == 0
        c, s = cos_r[...], sin_r[...]
        aq, ak, dm = aq_r[...], ak_r[...], d_r[...]
        q = _rot(zq[...], c, s, even)
        k = _rot(zk[...], c, s, even) * KSCALE
        qb, kb, vb = q.astype(BF16), k.astype(BF16), zv[...].astype(BF16)
        pb = (_dot_nt(qb, kb) * dm).astype(BF16)
        g = zg[...]
        sig = jax.nn.sigmoid(g)
        o = opre_r[...]
        mu = jnp.mean(o, axis=-1, keepdims=True)
        d = o - mu
        rstd = lax.rsqrt(jnp.mean(d * d, axis=-1, keepdims=True) + GN_EPS)
        y = d * rstd
        gain_v = gain_r[...]
        dret = dret_r[...].astype(F32)
        dyg = dret * (g * sig)
        dg_o[...] = (dret * (y * gain_v) * (sig * (1.0 + g * (1.0 - sig)))).astype(dg_o.dtype)
        dgain_o[...] += jnp.sum(dyg * y, axis=0, keepdims=True)
        dy = dyg * gain_v
        do = rstd * (dy - jnp.mean(dy, axis=-1, keepdims=True) - y * jnp.mean(dy * y, axis=-1, keepdims=True))
        dob = do.astype(BF16)
        stb = st_r[...].astype(BF16)
        dsn = dstate[...]
        dsnb = dsn.astype(BF16)
        dpb = (_dot_nt(dob, vb) * dm).astype(BF16)
        dq = _dot(dpb, kb) + _dot_nt(dob, stb) * aq
        dk = _dot_tn(dpb, qb) + _dot_nt(vb, dsnb) * ak
        dv = _dot_tn(pb, dob) + _dot((k * ak).astype(BF16), dsnb)
        dstate[...] = dsn * lam_r[...] + _dot_tn((q * aq).astype(BF16), dob)
        dq_o[...] = _rot_t(dq, c, s, even).astype(dq_o.dtype)
        dk_o[...] = _rot_t(dk * KSCALE, c, s, even).astype(dk_o.dtype)
        dv_o[...] = dv.astype(dv_o.dtype)

    return pl.pallas_call(
        body,
        name=name,
        grid=(nb,),
        in_specs=specs + [tok, pl.BlockSpec((RET_HEADS, None, LANE, LANE), lambda b: (0, blk(b), 0, 0)), tok],
        out_specs=[pl.BlockSpec((T, 4 * RET_WIDTH), lambda b: (blk(b), 0)), pl.BlockSpec((1, RET_WIDTH), lambda b: (0, 0))],
        out_shape=[jax.ShapeDtypeStruct((S, AB_IN_WIDTH), BF16), jax.ShapeDtypeStruct((1, RET_WIDTH), F32)],
        scratch_shapes=[pltpu.VMEM((RET_HEADS, LANE, LANE), F32)],
        compiler_params=_cp(("arbitrary",)),
    )(z, tb["cos"], tb["sin"], tb["dmat"], tb["aq"], tb["ak"], tb["lam"], gain, opre, states, dcat)


def _pool_counts(t0, rows):
    t = t0 + lax.broadcasted_iota(jnp.int32, (rows, POOL_WIDTH), 0)
    grp = lax.broadcasted_iota(jnp.int32, (rows, POOL_WIDTH), 1) >> 7
    win = jnp.where(grp == 0, POOL_WINDOWS[0], jnp.where(grp == 1, POOL_WINDOWS[1],
                    jnp.where(grp == 2, POOL_WINDOWS[2], POOL_WINDOWS[3])))
    return jnp.maximum(jnp.minimum(t + 1, win), 1).astype(F32), grp


def _window_sums(ext, grp, sign):
    n = ext.shape[0]
    sh = lambda v, k: pltpu.roll(v, k % n if sign > 0 else (n - k) % n, 0)
    s2 = ext + sh(ext, 1)
    s4 = s2 + sh(s2, 2)
    s8 = s4 + sh(s4, 4)
    s16 = s8 + sh(s8, 8)
    return jnp.where(grp == 0, s2, jnp.where(grp == 1, s4, jnp.where(grp == 2, s8, s16)))


def _pool_fwd(name, z, w_pool, scale, cat):
    S = z.shape[0]
    T = min(512, S)
    nb = S // T
    pcol = AB_IN_WIDTH // POOL_WIDTH - 1
    hb = T // POOL_HALO

    def body(p_ref, halo_ref, w_ref, sc_ref, cat_in, out_ref, pooled_ref):
        b = pl.program_id(0)
        cur = p_ref[...]
        halo = jnp.where(b > 0, halo_ref[...], 0.0)
        ext = jnp.concatenate([halo, cur], axis=0)
        cnt, grp = _pool_counts(b * T - POOL_HALO, T + POOL_HALO)
        sums = _window_sums(ext, grp, +1)
        pooled = (sums / cnt)[POOL_HALO:] - cur
        pb = pooled.astype(BF16)
        pooled_ref[...] = pb
        for gi in range(len(POOL_WINDOWS)):
            cs = slice(gi * POOL_GROUP_DIM, (gi + 1) * POOL_GROUP_DIM)
            mixed = _dot(pb[:, cs], w_ref[gi].astype(BF16))
            out_ref[:, cs] = (mixed * sc_ref[:, cs]).astype(out_ref.dtype)

    return pl.pallas_call(
        body,
        name=name,
        grid=(nb,),
        in_specs=[pl.BlockSpec((T, POOL_WIDTH), lambda b: (b, pcol)),
                  pl.BlockSpec((POOL_HALO, POOL_WIDTH), lambda b: (jnp.maximum(b * hb - 1, 0), pcol)),
                  pl.BlockSpec((4, POOL_GROUP_DIM, POOL_GROUP_DIM), lambda b: (0, 0, 0)),
                  pl.BlockSpec((1, POOL_WIDTH), lambda b: (0, 0)), _ANY],
        out_specs=[pl.BlockSpec((T, POOL_WIDTH), lambda b: (b, 1)), pl.BlockSpec((T, POOL_WIDTH), lambda b: (b, 0))],
        out_shape=[jax.ShapeDtypeStruct(cat.shape, cat.dtype), jax.ShapeDtypeStruct((S, POOL_WIDTH), BF16)],
        input_output_aliases={4: 0},
        compiler_params=_cp(("parallel",)),
    )(z, z, w_pool, scale, cat)


def _pool_bwd(name, pooled, w_pool, scale, dcat, dz):
    S = pooled.shape[0]
    T = min(512, S)
    nb = S // T
    hb = T // POOL_HALO
    last_h = S // POOL_HALO - 1
    pcol = AB_IN_WIDTH // POOL_WIDTH - 1

    def body(d_ref, dn_ref, pooled_ref, w_ref, sc_ref, dz_in, dp_ref, dw_ref, dsc_ref):
        b = pl.program_id(0)

        @pl.when(b == 0)
        def _():
            dw_ref[...] = jnp.zeros_like(dw_ref)
            dsc_ref[...] = jnp.zeros_like(dsc_ref)

        sc = sc_ref[...]
        dout = d_ref[...].astype(F32)
        dnext = jnp.where(b < nb - 1, dn_ref[...].astype(F32), 0.0)
        dmix = jnp.concatenate([dout, dnext], axis=0) * sc
        dmb = dmix.astype(BF16)
        pb = pooled_ref[...]
        dpooled = []
        for gi in range(len(POOL_WINDOWS)):
            cs = slice(gi * POOL_GROUP_DIM, (gi + 1) * POOL_GROUP_DIM)
            wb = w_ref[gi].astype(BF16)
            dpooled.append(_dot_nt(dmb[:, cs], wb))
            dw_ref[gi] += _dot_tn(pb[:, cs], dmb[:T, cs])
            mixed = _dot(pb[:, cs], wb)
            dsc_ref[:, cs] += jnp.sum(dout[:, cs] * mixed, axis=0, keepdims=True)
        dpl = jnp.concatenate(dpooled, axis=1)
        cnt, grp = _pool_counts(b * T, T + POOL_HALO)
        sums = _window_sums(dpl / cnt, grp, -1)
        dp_ref[...] = (sums[:T] - dpl[:T]).astype(dp_ref.dtype)

    return pl.pallas_call(
        body,
        name=name,
        grid=(nb,),
        in_specs=[pl.BlockSpec((T, POOL_WIDTH), lambda b: (b, 1)),
                  pl.BlockSpec((POOL_HALO, POOL_WIDTH), lambda b: (jnp.minimum((b + 1) * hb, last_h), 1)),
                  pl.BlockSpec((T, POOL_WIDTH), lambda b: (b, 0)),
                  pl.BlockSpec((4, POOL_GROUP_DIM, POOL_GROUP_DIM), lambda b: (0, 0, 0)),
                  pl.BlockSpec((1, POOL_WIDTH), lambda b: (0, 0)), _ANY],
        out_specs=[pl.BlockSpec((T, POOL_WIDTH), lambda b: (b, pcol)),
                   pl.BlockSpec((4, POOL_GROUP_DIM, POOL_GROUP_DIM), lambda b: (0, 0, 0)),
                   pl.BlockSpec((1, POOL_WIDTH), lambda b: (0, 0))],
        out_shape=[jax.ShapeDtypeStruct(dz.shape, dz.dtype),
                   jax.ShapeDtypeStruct((4, POOL_GROUP_DIM, POOL_GROUP_DIM), F32),
                   jax.ShapeDtypeStruct((1, POOL_WIDTH), F32)],
        input_output_aliases={5: 0},
        compiler_params=_cp(("arbitrary",)),
    )(dcat, dcat, pooled, w_pool, scale, dz)


ATT_Q = 256
ATT_W = ATT_Q + LEFT_CHUNKS * CHUNK
_REL_COLS = CHUNK * ATT_W
_REL_TILE = _REL_COLS // 8


def _rel_index():
    n = np.arange(CHUNK)[:, None]
    j = np.arange(ATT_W)[None, :]
    rel = np.clip(n + LEFT_CHUNKS * CHUNK - j, -REL_CLIP, REL_CLIP) + REL_CLIP
    rel = np.where(j < BAND, rel, N_REL)
    return jnp.asarray(rel.reshape(1, _REL_COLS), jnp.int32)


def _bias_table(name, rel_bias, rel_idx):
    rb = jnp.concatenate([rel_bias, jnp.full((ATT_HEADS, 1), NEG_INF, F32),
                          jnp.zeros((ATT_HEADS, N_REL_PAD - N_REL - 1), F32)], axis=1)

    def body(rb_ref, idx_ref, o_ref):
        r = lax.broadcasted_iota(jnp.int32, (N_REL_PAD, _REL_TILE), 0)
        onehot = (r == idx_ref[...]).astype(F32)
        o_ref[...] = jnp.dot(rb_ref[...], onehot, precision=lax.Precision.HIGHEST, preferred_element_type=F32)

    return pl.pallas_call(
        body,
        name=name,
        grid=(_REL_COLS // _REL_TILE,),
        in_specs=[pl.BlockSpec((ATT_HEADS, N_REL_PAD), lambda i: (0, 0)), pl.BlockSpec((1, _REL_TILE), lambda i: (0, i))],
        out_specs=pl.BlockSpec((ATT_HEADS, _REL_TILE), lambda i: (0, i)),
        out_shape=jax.ShapeDtypeStruct((ATT_HEADS, _REL_COLS), F32),
        compiler_params=_cp(("parallel",)),
    )(rb, rel_idx)


def _bias_grad(name, ds_sum, rel_idx):
    n = _REL_COLS // _REL_TILE

    def body(ds_ref, idx_ref, o_ref):
        i = pl.program_id(0)
        r = lax.broadcasted_iota(jnp.int32, (N_REL_PAD, _REL_TILE), 0)
        onehot = (r == idx_ref[...]).astype(F32)
        part = lax.dot_general(ds_ref[...], onehot, (((1,), (1,)), ((), ())),
                               precision=lax.Precision.HIGHEST, preferred_element_type=F32)

        @pl.when(i == 0)
        def _():
            o_ref[...] = part

        @pl.when(i > 0)
        def _():
            o_ref[...] += part

    out = pl.pallas_call(
        body,
        name=name,
        grid=(n,),
        in_specs=[pl.BlockSpec((ATT_HEADS, _REL_TILE), lambda i: (0, i)), pl.BlockSpec((1, _REL_TILE), lambda i: (0, i))],
        out_specs=pl.BlockSpec((ATT_HEADS, N_REL_PAD), lambda i: (0, 0)),
        out_shape=jax.ShapeDtypeStruct((ATT_HEADS, N_REL_PAD), F32),
        compiler_params=_cp(("arbitrary",)),
    )(ds_sum, rel_idx)
    return out[:, :N_REL]


def _attn_unit(q_ref, kw_ref, bias_ref, e, u, first_block, col, lane):
    mine = (lane < ATT_HEAD_DIM) if e == 0 else (lane >= ATT_HEAD_DIM)
    qm = jnp.where(mine, q_ref[u * ATT_Q:(u + 1) * ATT_Q, :] * QSCALE, 0)
    kw = kw_ref[u * ATT_Q:u * ATT_Q + ATT_W, :]
    s = _dot_nt(qm, kw) + bias_ref[e]
    s = jnp.where(col >= first_block * (ATT_BLOCK - u * ATT_Q), s, NEG_INF)
    p = jnp.exp(s - jnp.max(s, axis=-1, keepdims=True))
    return p, 1.0 / jnp.sum(p, axis=-1, keepdims=True), qm, kw, mine


def _attn_in_specs(nb):
    T = ATT_BLOCK
    hp = ATT_HEADS // 2
    cur = lambda off: pl.BlockSpec((T, LANE), lambda h, b: (jnp.minimum(b, nb - 1), off + h))
    prev = lambda off: pl.BlockSpec((T, LANE), lambda h, b: (jnp.clip(b - 1, 0, nb - 1), off + h))
    return [cur(0), prev(hp), cur(hp), prev(2 * hp), cur(2 * hp),
            pl.BlockSpec((None, 2, CHUNK, ATT_W), lambda h, b: (h, 0, 0, 0))]


def _spread_bias(bias_ref, bm_ref):
    for e in range(2):
        for j in range(ATT_Q // CHUNK):
            bm_ref[e, j * CHUNK:(j + 1) * CHUNK, :] = pltpu.roll(bias_ref[e], j * CHUNK, 1)


def _attn_fwd(name, qkv, bias):
    S = qkv.shape[0]
    T = ATT_BLOCK
    nb = S // T

    def body(q_ref, kp_ref, kc_ref, vp_ref, vc_ref, band_ref, o_ref, kw_ref, vw_ref, bias_ref):
        b = pl.program_id(1)

        @pl.when(b == 0)
        def _():
            _spread_bias(band_ref, bias_ref)

        kw_ref[0:T] = kp_ref[...]
        kw_ref[T:2 * T] = kc_ref[...]
        vw_ref[0:T] = vp_ref[...]
        vw_ref[T:2 * T] = vc_ref[...]
        col = lax.broadcasted_iota(jnp.int32, (ATT_Q, ATT_W), 1)
        lane = lax.broadcasted_iota(jnp.int32, (ATT_Q, LANE), 1)
        first = jnp.where(b == 0, 1, 0)
        for u in range(T // ATT_Q):
            vw = vw_ref[u * ATT_Q:u * ATT_Q + ATT_W, :]
            outs = []
            for e in range(2):
                p, inv, _, _, _ = _attn_unit(q_ref, kw_ref, bias_ref, e, u, first, col, lane)
                outs.append(_dot(p.astype(BF16), vw) * inv)
            o_ref[u * ATT_Q:(u + 1) * ATT_Q, :] = jnp.where(lane < ATT_HEAD_DIM, outs[0], outs[1]).astype(o_ref.dtype)

    return pl.pallas_call(
        body,
        name=name,
        grid=(ATT_HEADS // 2, nb),
        in_specs=_attn_in_specs(nb),
        out_specs=pl.BlockSpec((T, LANE), lambda h, b: (b, h)),
        out_shape=jax.ShapeDtypeStruct((S, D_MODEL), BF16),
        scratch_shapes=[pltpu.VMEM((2 * T, LANE), BF16), pltpu.VMEM((2 * T, LANE), BF16),
                        pltpu.VMEM((2, ATT_Q, ATT_W), F32)],
        compiler_params=_cp(("parallel", "arbitrary")),
    )(qkv, qkv, qkv, qkv, qkv, bias)


def _attn_bwd(name, qkv, bias, do):
    S = qkv.shape[0]
    T = ATT_BLOCK
    nb = S // T

    def body(q_ref, kp_ref, kc_ref, vp_ref, vc_ref, band_ref, do_ref,
             dq_ref, dk_ref, dv_ref, dband_ref, kw_ref, vw_ref, dkw_ref, dvw_ref, bias_ref, dbias_ref):
        b = pl.program_id(1)

        @pl.when(b == 0)
        def _():
            _spread_bias(band_ref, bias_ref)
            dbias_ref[...] = jnp.zeros_like(dbias_ref)
            dkw_ref[T:2 * T] = jnp.zeros((T, LANE), F32)
            dvw_ref[T:2 * T] = jnp.zeros((T, LANE), F32)

        dkw_ref[0:T] = dkw_ref[T:2 * T]
        dvw_ref[0:T] = dvw_ref[T:2 * T]
        dkw_ref[T:2 * T] = jnp.zeros((T, LANE), F32)
        dvw_ref[T:2 * T] = jnp.zeros((T, LANE), F32)

        @pl.when(b < nb)
        def _():
            kw_ref[0:T] = kp_ref[...]
            kw_ref[T:2 * T] = kc_ref[...]
            vw_ref[0:T] = vp_ref[...]
            vw_ref[T:2 * T] = vc_ref[...]
            col = lax.broadcasted_iota(jnp.int32, (ATT_Q, ATT_W), 1)
            lane = lax.broadcasted_iota(jnp.int32, (ATT_Q, LANE), 1)
            first = jnp.where(b == 0, 1, 0)
            for u in range(T // ATT_Q):
                rows = slice(u * ATT_Q, (u + 1) * ATT_Q)
                win = slice(u * ATT_Q, u * ATT_Q + ATT_W)
                vw = vw_ref[win, :]
                do2 = do_ref[rows, :]
                dqs, dk, dv = [], None, None
                for e in range(2):
                    p, inv, qm, kw, mine = _attn_unit(q_ref, kw_ref, bias_ref, e, u, first, col, lane)
                    dom = jnp.where(mine, do2, 0)
                    dp = _dot_nt(dom, vw)
                    delta = jnp.sum(p * dp, axis=-1, keepdims=True) * inv
                    ds = p * ((dp - delta) * inv)
                    dbias_ref[e] += ds
                    dsb = ds.astype(BF16)
                    dqs.append(_dot(dsb, kw))
                    dk_e = _dot_tn(dsb, qm)
                    dv_e = _dot_tn((p * inv).astype(BF16), dom)
                    dk = dk_e if dk is None else dk + dk_e
                    dv = dv_e if dv is None else dv + dv_e
                dq_ref[rows, :] = (jnp.where(lane < ATT_HEAD_DIM, dqs[0], dqs[1]) * QSCALE).astype(dq_ref.dtype)
                dkw_ref[win, :] += dk
                dvw_ref[win, :] += dv

        @pl.when(b > 0)
        def _():
            dk_ref[...] = dkw_ref[0:T].astype(dk_ref.dtype)
            dv_ref[...] = dvw_ref[0:T].astype(dv_ref.dtype)

        @pl.when(b == nb)
        def _():
            for e in range(2):
                acc = dbias_ref[e, 0:CHUNK, :]
                for j in range(1, ATT_Q // CHUNK):
                    acc = acc + pltpu.roll(dbias_ref[e, j * CHUNK:(j + 1) * CHUNK, :], ATT_W - j * CHUNK, 1)
                dband_ref[e] = acc

    tok = jax.ShapeDtypeStruct((S, D_MODEL), BF16)
    prev_out = pl.BlockSpec((T, LANE), lambda h, b: (jnp.maximum(b - 1, 0), h))
    return pl.pallas_call(
        body,
        name=name,
        grid=(ATT_HEADS // 2, nb + 1),
        in_specs=_attn_in_specs(nb) + [pl.BlockSpec((T, LANE), lambda h, b: (jnp.minimum(b, nb - 1), h))],
        out_specs=[pl.BlockSpec((T, LANE), lambda h, b: (jnp.minimum(b, nb - 1), h)), prev_out, prev_out,
                   pl.BlockSpec((None, 2, CHUNK, ATT_W), lambda h, b: (h, 0, 0, 0))],
        out_shape=[tok, tok, tok, jax.ShapeDtypeStruct((ATT_HEADS // 2, 2, CHUNK, ATT_W), F32)],
        scratch_shapes=[pltpu.VMEM((2 * T, LANE), BF16), pltpu.VMEM((2 * T, LANE), BF16),
                        pltpu.VMEM((2 * T, LANE), F32), pltpu.VMEM((2 * T, LANE), F32),
                        pltpu.VMEM((2, ATT_Q, ATT_W), F32), pltpu.VMEM((2, ATT_Q, ATT_W), F32)],
        compiler_params=_cp(("parallel", "arbitrary")),
    )(qkv, qkv, qkv, qkv, qkv, bias, do)


def _local_step(x, target, small, W):
    S = x.shape[0]
    tb = _ret_tables(S)
    rel_idx = _rel_index()
    saved = []
    for layer in range(DEPTH):
        i = layer // 2
        st = {"x_in": x}
        h = _rms_fwd(f"mix_norm_fwd{layer}", x, small["mix_norm"][layer:layer + 1])
        st["h"] = h
        if layer % 2 == 0:
            z = W.mm(f"ab_in_fwd{layer}", "nn", h, W.get("ab_w_in", i), tn=640, out_dtype=F32)
            gain = small["ab_gn_gain"][i:i + 1]
            cat, opre, states = _ret_fwd(f"ret_fwd{layer}", z, tb, gain)
            cat, pooled = _pool_fwd(f"pool_fwd{layer}", z, small["ab_w_pool"][i], small["ab_pool_scale"][i:i + 1], cat)
            st.update(z=z, opre=opre, states=states, pooled=pooled, cat=cat)
            x = W.mm(f"ab_out_fwd{layer}", "nn", cat, W.get("ab_w_out", i), extras=(x,), epi=lambda acc, r: acc + r)
        else:
            qkv = W.mm(f"qkv_fwd{layer}", "nn", h, W.get("c_w_qkv", i), out_dtype=BF16)
            bias = _bias_table(f"bias_table{layer}", small["c_rel_bias"][i], rel_idx)
            bias = bias.reshape(ATT_HEADS // 2, 2, CHUNK, ATT_W)
            att = _attn_fwd(f"attn_fwd{layer}", qkv, bias)
            st.update(qkv=qkv, bias=bias, att=att)
            x = W.mm(f"c_out_fwd{layer}", "nn", att, W.get("c_w_out", i), extras=(x,), epi=lambda acc, r: acc + r)
        st["x_mid"] = x
        hn = _rms_fwd(f"ffn_norm_fwd{layer}", x, small["ffn_norm"][layer:layer + 1])
        u = W.mm(f"ffn_in_fwd{layer}", "nn", hn, W.get("w_ffn_in", layer), out_dtype=BF16)
        x = W.mm(f"ffn_out_fwd{layer}", "nn", u, W.get("w_ffn_out", layer), a_fn=_relu2, extras=(x,),
                epi=lambda acc, r: acc + r)
        st.update(hn=hn, u=u)
        saved.append(st)

    loss, dx, d_final = _loss_head(x, small["final_norm"].reshape(1, D_MODEL), target)

    gs = {k: [None] * v.shape[0] for k, v in small.items() if k != "final_norm"}
    gb = {k: None for k in W.n_layers}
    landed = {k: None for k in W.n_layers}
    pending = []

    def host(name, *args, **kw):
        if not pending:
            return _mm(name, *args, **kw)
        key, idx = pending.pop(0)
        res, outs = _mm(name, *args, rider=_grad_rider(key, idx, gb[key], landed[key]), **kw)
        landed[key] = outs[0]
        return res

    def dw(name, key, idx, a, b, call=_mm, **kw):
        gb[key] = call(name, "tn", a, b, stack=(W.n_layers[key], idx, gb[key]), out_dtype=BF16, **kw)
        pending.append((key, idx))

    for layer in reversed(range(DEPTH)):
        i = layer // 2
        st = saved[layer]
        du = host(f"ffn_out_bwd{layer}", "nt", dx, W.get("w_ffn_out", layer), extras=(st["u"],),
                  epi=lambda acc, u: acc * (2.0 * jnp.maximum(u, 0).astype(F32)), out_dtype=BF16)
        dw(f"ffn_out_dw{layer}", "w_ffn_out", layer, st["u"], dx, a_fn=_relu2)
        dhn = host(f"ffn_in_bwd{layer}", "nt", du, W.get("w_ffn_in", layer))
        dw(f"ffn_in_dw{layer}", "w_ffn_in", layer, st["hn"], du)
        dx, gs["ffn_norm"][layer] = _rms_bwd(f"ffn_norm_bwd{layer}", st["x_mid"], small["ffn_norm"][layer:layer + 1], dhn, dx)
        if layer % 2 == 0:
            dcat = _mm(f"ab_out_bwd{layer}", "nt", dx, W.get("ab_w_out", i), out_dtype=BF16)
            dw(f"ab_out_dw{layer}", "ab_w_out", i, st["cat"], dx)
            gain = small["ab_gn_gain"][i:i + 1]
            dz, gs["ab_gn_gain"][i] = _ret_bwd(f"ret_bwd{layer}", st["z"], tb, gain, st["opre"], st["states"], dcat)
            dz, gs["ab_w_pool"][i], gs["ab_pool_scale"][i] = _pool_bwd(
                f"pool_bwd{layer}", st["pooled"], small["ab_w_pool"][i], small["ab_pool_scale"][i:i + 1], dcat, dz)
            dh = host(f"ab_in_bwd{layer}", "nt", dz, W.get("ab_w_in", i), tk=1280)
            dw(f"ab_in_dw{layer}", "ab_w_in", i, st["h"], dz, call=host, tn=640)
        else:
            datt = _mm(f"c_out_bwd{layer}", "nt", dx, W.get("c_w_out", i), out_dtype=BF16)
            dw(f"c_out_dw{layer}", "c_w_out", i, st["att"], dx)
            dq, dk, dv, dbias = _attn_bwd(f"attn_bwd{layer}", st["qkv"], st["bias"], datt)
            gs["c_rel_bias"][i] = _bias_grad(f"bias_grad{layer}", dbias.reshape(ATT_HEADS, _REL_COLS), rel_idx)
            dqkv = [dq, dk, dv]
            dh = host(f"qkv_bwd{layer}", "nt", dqkv, W.get("c_w_qkv", i))
            dw(f"qkv_dw{layer}", "c_w_qkv", i, st["h"], dqkv, call=host)
        dx, gs["mix_norm"][layer] = _rms_bwd(f"mix_norm_bwd{layer}", st["x_in"], small["mix_norm"][layer:layer + 1], dh, dx)
    for key, idx in pending:
        landed[key], = _run_rider(f"grad_exchange_{key}{idx}", _grad_rider(key, idx, gb[key], landed[key]))

    g_small = {
        "mix_norm": jnp.concatenate(gs["mix_norm"], axis=0),
        "ffn_norm": jnp.concatenate(gs["ffn_norm"], axis=0),
        "ab_gn_gain": jnp.concatenate(gs["ab_gn_gain"], axis=0),
        "ab_w_pool": jnp.stack(gs["ab_w_pool"], axis=0),
        "ab_pool_scale": jnp.concatenate(gs["ab_pool_scale"], axis=0),
        "c_rel_bias": jnp.stack(gs["c_rel_bias"], axis=0),
        "final_norm": d_final.reshape(D_MODEL),
    }
    return loss, dx, g_small, gb, landed


_BIG = ("w_ffn_in", "w_ffn_out", "ab_w_in", "ab_w_out", "c_w_qkv", "c_w_out")
_SHARD_AXIS = {"w_ffn_in": 2, "w_ffn_out": 1, "ab_w_in": 2, "ab_w_out": 1, "c_w_qkv": 2, "c_w_out": 1}
_SMALL = ("mix_norm", "ffn_norm", "ab_gn_gain", "ab_w_pool", "ab_pool_scale", "c_rel_bias", "final_norm")


def _place():
    x, y, c = lax.axis_index("x"), lax.axis_index("y"), lax.axis_index("c")
    chips = [(1 - x, y), (x, 1 - y), (1 - x, 1 - y)]
    return x, y, c, chips


def _sub(ref, axis, start, size):
    idx = [slice(None)] * len(ref.shape)
    idx[axis] = pl.ds(pl.multiple_of(start, LANE), size)
    return ref.at[tuple(idx)]


def _gather_rider(items, shards):
    keys = sorted({k for k, _ in items})
    n = len(items)
    axes = [_SHARD_AXIS[k] - 1 for k, _ in items]
    sizes = [shards[k].shape[a + 1] for (k, _), a in zip(items, axes)]
    hsizes = [shards[k].shape[2 - a] // 2 for (k, _), a in zip(items, axes)]

    def views(ins, outs, send_sems, recv_sems):
        x, y, c, chips = _place()
        srcs = [ins[keys.index(k)].at[l] for k, l in items]

        def remote(src, dst, s, to):
            return pltpu.make_async_remote_copy(src_ref=src, dst_ref=dst, send_sem=send_sems.at[s],
                                                recv_sem=recv_sems.at[s], device_id=to, device_id_type=MESH)

        def half(w, chip, core):
            return _sub(_sub(outs[w], axes[w], chip * sizes[w], sizes[w]), 1 - axes[w], core * hsizes[w], hsizes[w])

        me = 2 * x + y
        local = [pltpu.make_async_copy(srcs[w], _sub(outs[w], axes[w], me * sizes[w], sizes[w]), send_sems.at[6 * n + w])
                 for w in range(n)]
        first = [remote(_sub(srcs[w], 1 - axes[w], c * hsizes[w], hsizes[w]), half(w, me, c), w * 6 + k, (px, py, c))
                 for w in range(n) for k, (px, py) in enumerate(chips)]
        return x, y, c, chips, remote, half, local, first

    def start(ins, outs, send_sems, recv_sems):
        *_, local, first = views(ins, outs, send_sems, recv_sems)
        for cp in local + first:
            cp.start()

    def finish(ins, outs, send_sems, recv_sems):
        x, y, c, chips, remote, half, local, first = views(ins, outs, send_sems, recv_sems)
        sibling = (x, y, 1 - c)
        passed = []
        for w in range(n):
            for k, (px, py) in enumerate(chips):
                landed = half(w, 2 * px + py, c)
                remote(landed, landed, w * 6 + k, (px, py, c)).wait_recv()
                cp = remote(landed, landed, w * 6 + 3 + k, sibling)
                cp.start()
                passed.append(cp)
        for w in range(n):
            for k, (px, py) in enumerate(chips):
                theirs = half(w, 2 * px + py, 1 - c)
                remote(theirs, theirs, w * 6 + 3 + k, sibling).wait_recv()
        for cp in first + passed:
            cp.wait_send()
        for cp in local:
            cp.wait()

    def full(k, a):
        shape = list(shards[k].shape[1:])
        shape[a] *= N_CHIPS
        return jax.ShapeDtypeStruct(tuple(shape), shards[k].dtype)

    return _Rider(tuple(shards[k] for k in keys), tuple(full(k, a) for (k, _), a in zip(items, axes)), 7 * n, start, finish)


def _mixer_items(layer):
    names = ("ab_w_in", "ab_w_out") if layer % 2 == 0 else ("c_w_qkv", "c_w_out")
    return [(k, layer // 2) for k in names]


class _Weights:
    def __init__(self, shards):
        self.shards = shards
        self.n_layers = {k: shards[k].shape[0] for k in _BIG}
        self.full = {}
        first = _mixer_items(0)
        self._take(first, _run_rider("gather_first", _gather_rider(first, shards)))
        self.plan = {"ab_in_fwd0": [("w_ffn_in", 0)], "ab_out_fwd0": [("w_ffn_out", 0)]}
        for layer in range(1, DEPTH):
            proj = "ab_in" if layer % 2 == 0 else "qkv"
            self.plan[f"ffn_in_fwd{layer - 1}"] = _mixer_items(layer)
            self.plan[f"ffn_out_fwd{layer - 1}"] = [("w_ffn_in", layer)]
            self.plan[f"{proj}_fwd{layer}"] = [("w_ffn_out", layer)]

    def _take(self, items, outs):
        self.full.update(zip(items, outs))

    def get(self, name, layer):
        return self.full[(name, layer)]

    def mm(self, name, *args, **kw):
        items = self.plan.get(name)
        if items is None:
            return _mm(name, *args, **kw)
        res, outs = _mm(name, *args, rider=_gather_rider(items, self.shards), **kw)
        self._take(items, outs)
        return res


def _run_rider(name, rider):
    n_in, n_out = len(rider.operands), len(rider.out_shapes)

    def body(*refs):
        ins, outs, sems = refs[:n_in], refs[n_in:n_in + n_out], refs[n_in + n_out:]
        rider.start(ins, outs, *sems)
        rider.finish(ins, outs, *sems)

    return pl.pallas_call(
        body,
        name=name,
        in_specs=[_ANY] * n_in,
        out_specs=[_ANY] * n_out,
        out_shape=list(rider.out_shapes),
        input_output_aliases=dict(rider.aliases),
        scratch_shapes=[pltpu.SemaphoreType.DMA((rider.n_sems,)), pltpu.SemaphoreType.DMA((rider.n_sems,))],
        compiler_params=pltpu.CompilerParams(has_side_effects=True),
    )(*rider.operands)


def _grad_rider(name, layer, grad, landing):
    axis = _SHARD_AXIS[name] - 1
    L, R, C = grad.shape
    shard = (R // N_CHIPS, C) if axis == 0 else (R, C // N_CHIPS)
    size = shard[axis]

    def copies(ins, outs, send_sems, recv_sems):
        x, y, c, chips = _place()
        return [pltpu.make_async_remote_copy(
            src_ref=_sub(ins[0].at[layer], axis, (2 * px + py) * size, size), dst_ref=outs[0].at[layer, k],
            send_sem=send_sems.at[k], recv_sem=recv_sems.at[k], device_id=(px, py, c), device_id_type=MESH)
            for k, (px, py) in enumerate(chips)]

    def start(ins, outs, send_sems, recv_sems):
        for cp in copies(ins, outs, send_sems, recv_sems):
            cp.start()

    def finish(ins, outs, send_sems, recv_sems):
        cps = copies(ins, outs, send_sems, recv_sems)
        for cp in cps:
            cp.wait_recv()
        for cp in cps:
            cp.wait_send()

    out = jax.ShapeDtypeStruct((L, 3) + shard, grad.dtype)
    if landing is None:
        return _Rider((grad,), (out,), 3, start, finish)
    return _Rider((grad, landing), (out,), 3, start, finish, aliases=((1, 0),))


def _pair_swap(sums):
    n = len(sums)

    def body(*refs):
        ins, outs = refs[:n], refs[n:2 * n]
        send_sems, recv_sems = refs[2 * n:]
        x, y, c, _ = _place()
        cps = [pltpu.make_async_remote_copy(src_ref=ins[w], dst_ref=outs[w], send_sem=send_sems.at[w],
                                            recv_sem=recv_sems.at[w], device_id=(x, y, 1 - c), device_id_type=MESH)
               for w in range(n)]
        for cp in cps:
            cp.start()
        for cp in cps:
            cp.wait_recv()
        for cp in cps:
            cp.wait_send()

    return pl.pallas_call(
        body,
        name="pair_swap",
        in_specs=[_ANY] * n,
        out_specs=[_ANY] * n,
        out_shape=[jax.ShapeDtypeStruct(s.shape, s.dtype) for s in sums],
        scratch_shapes=[pltpu.SemaphoreType.DMA((n,)), pltpu.SemaphoreType.DMA((n,))],
        compiler_params=pltpu.CompilerParams(has_side_effects=True),
    )(*sums)


def _rows_tile(rows, cols):
    tr = rows
    while tr * cols > (1 << 19) and tr % 16 == 0:
        tr //= 2
    return tr


def _chip_sum(name, grad, landed, chip, saxis):
    L = grad.shape[0]
    _, _, R, C = landed.shape
    tr = _rows_tile(R, C)
    nr = R // tr
    if saxis == 2:
        g_idx = lambda l, i, s: (l, i, s[0])
    else:
        g_idx = lambda l, i, s: (l, s[0] * nr + i, 0)

    def body(s_ref, g_ref, l_ref, o_ref):
        tot = ((g_ref[...].astype(F32) + l_ref[0].astype(F32)) + l_ref[1].astype(F32)) + l_ref[2].astype(F32)
        o_ref[...] = tot.astype(o_ref.dtype)

    return pl.pallas_call(
        body,
        name=name,
        grid_spec=pltpu.PrefetchScalarGridSpec(
            num_scalar_prefetch=1,
            grid=(L, nr),
            in_specs=[pl.BlockSpec((None, tr, C), g_idx), pl.BlockSpec((None, 3, tr, C), lambda l, i, s: (l, 0, i, 0))],
            out_specs=pl.BlockSpec((None, tr, C), lambda l, i, s: (l, i, 0)),
        ),
        out_shape=jax.ShapeDtypeStruct((L, R, C), BF16),
        compiler_params=_cp(("parallel", "parallel")),
    )(chip, grad, landed)


def _all_reduce_small(packed):
    R = packed.shape[0]

    def body(p_ref, o_ref, land_ref, send_sems, recv_sems):
        x, y, c, _ = _place()
        me = 4 * x + 2 * y + c
        sends, recvs = [], []
        for r in range(1, N_DEV):
            px, py, pc = x ^ (r >> 2), y ^ ((r >> 1) & 1), c ^ (r & 1)
            cp = pltpu.make_async_remote_copy(src_ref=p_ref, dst_ref=land_ref.at[me], send_sem=send_sems.at[r - 1],
                                              recv_sem=recv_sems.at[r - 1], device_id=(px, py, pc), device_id_type=MESH)
            cp.start()
            sends.append(cp)
            recvs.append(pltpu.make_async_remote_copy(src_ref=p_ref, dst_ref=land_ref.at[4 * px + 2 * py + pc],
                                                      send_sem=send_sems.at[r - 1], recv_sem=recv_sems.at[r - 1],
                                                      device_id=(px, py, pc), device_id_type=MESH))
        land_ref[me] = p_ref[...]
        for cp in recvs:
            cp.wait_recv()
        for cp in sends:
            cp.wait_send()
        acc = land_ref[0]
        for d in range(1, N_DEV):
            acc = acc + land_ref[d]
        o_ref[...] = acc

    vm = pl.BlockSpec(memory_space=pltpu.VMEM)
    return pl.pallas_call(
        body,
        name="all_reduce_small",
        in_specs=[vm],
        out_specs=vm,
        out_shape=jax.ShapeDtypeStruct((R, LANE), F32),
        scratch_shapes=[pltpu.VMEM((N_DEV, R, LANE), F32), pltpu.SemaphoreType.DMA((N_DEV - 1,)),
                        pltpu.SemaphoreType.DMA((N_DEV - 1,))],
        compiler_params=pltpu.CompilerParams(has_side_effects=True, vmem_limit_bytes=VMEM_LIMIT),
    )(packed)


def _adamw(name, w, m, v, grads):
    R, C = w.shape
    tr = _rows_tile(R, C)
    c1 = 1.0 - ADAM_B1 ** ADAM_STEP
    c2 = 1.0 - ADAM_B2 ** ADAM_STEP
    ng = len(grads)

    def body(*refs):
        w_ref, m_ref, v_ref = refs[:3]
        g_refs = refs[3:3 + ng]
        g_ref, d_ref, nm_ref, nv_ref = refs[3 + ng:]
        gv = g_refs[0][...].astype(F32)
        for r in g_refs[1:]:
            gv = gv + r[...].astype(F32)
        g_ref[...] = gv
        nm = ADAM_B1 * m_ref[...] + (1.0 - ADAM_B1) * gv
        nv = ADAM_B2 * v_ref[...] + (1.0 - ADAM_B2) * (gv * gv)
        nm_ref[...] = nm
        nv_ref[...] = nv
        d_ref[...] = -ADAM_LR * ((nm / c1) / (jnp.sqrt(nv / c2) + ADAM_EPS) + ADAM_WD * w_ref[...])

    blk = pl.BlockSpec((tr, C), lambda i: (i, 0))
    out = jax.ShapeDtypeStruct((R, C), F32)
    return pl.pallas_call(
        body,
        name=name,
        grid=(R // tr,),
        in_specs=[blk] * (3 + ng),
        out_specs=[blk] * 4,
        out_shape=[out] * 4,
        compiler_params=_cp(("parallel",)),
    )(w, m, v, *grads)


def _pack(parts):
    rows = []
    for p in parts:
        flat = p.reshape(-1).astype(F32)
        n = -(-flat.shape[0] // (8 * LANE)) * (8 * LANE)
        rows.append(jnp.pad(flat, (0, n - flat.shape[0])).reshape(n // LANE, LANE))
    return jnp.concatenate(rows, axis=0)


def _unpack(packed, like):
    out, r = [], 0
    for p in like:
        size = int(np.prod(p.shape))
        n = -(-size // (8 * LANE)) * 8
        out.append(packed[r:r + n].reshape(-1)[:size].reshape(p.shape))
        r += n
    return out


def kernel(x, mix_norm, ffn_norm, w_ffn_in, w_ffn_out, ab_w_in, ab_gn_gain, ab_w_pool, ab_pool_scale, ab_w_out, c_w_qkv, c_rel_bias, c_w_out, final_norm, loss_target, m_mix_norm, m_ffn_norm, m_w_ffn_in, m_w_ffn_out, m_ab_w_in, m_ab_gn_gain, m_ab_w_pool, m_ab_pool_scale, m_ab_w_out, m_c_w_qkv, m_c_rel_bias, m_c_w_out, m_final_norm, v_mix_norm, v_ffn_norm, v_w_ffn_in, v_w_ffn_out, v_ab_w_in, v_ab_gn_gain, v_ab_w_pool, v_ab_pool_scale, v_ab_w_out, v_c_w_qkv, v_c_rel_bias, v_c_w_out, v_final_norm):
    w = dict(mix_norm=mix_norm, ffn_norm=ffn_norm, w_ffn_in=w_ffn_in, w_ffn_out=w_ffn_out, ab_w_in=ab_w_in,
             ab_gn_gain=ab_gn_gain, ab_w_pool=ab_w_pool, ab_pool_scale=ab_pool_scale, ab_w_out=ab_w_out,
             c_w_qkv=c_w_qkv, c_rel_bias=c_rel_bias, c_w_out=c_w_out, final_norm=final_norm)
    m = dict(mix_norm=m_mix_norm, ffn_norm=m_ffn_norm, w_ffn_in=m_w_ffn_in, w_ffn_out=m_w_ffn_out, ab_w_in=m_ab_w_in,
             ab_gn_gain=m_ab_gn_gain, ab_w_pool=m_ab_w_pool, ab_pool_scale=m_ab_pool_scale, ab_w_out=m_ab_w_out,
             c_w_qkv=m_c_w_qkv, c_rel_bias=m_c_rel_bias, c_w_out=m_c_w_out, final_norm=m_final_norm)
    v = dict(mix_norm=v_mix_norm, ffn_norm=v_ffn_norm, w_ffn_in=v_w_ffn_in, w_ffn_out=v_w_ffn_out, ab_w_in=v_ab_w_in,
             ab_gn_gain=v_ab_gn_gain, ab_w_pool=v_ab_w_pool, ab_pool_scale=v_ab_pool_scale, ab_w_out=v_ab_w_out,
             c_w_qkv=v_c_w_qkv, c_rel_bias=v_c_rel_bias, c_w_out=v_c_w_out, final_norm=v_final_norm)
    S = x.shape[1]
    cx, cy, cc = lax.axis_index("x"), lax.axis_index("y"), lax.axis_index("c")
    chip = jnp.reshape(2 * cx + cy, (1,)).astype(jnp.int32)

    big = _Weights({k: w[k].astype(BF16) for k in _BIG})
    small = {k: w[k] for k in _SMALL}
    loss, grad_x, g_small, g_big, landed = _local_step(x.reshape(S, D_MODEL), loss_target.reshape(S, D_MODEL), small, big)

    sums = [_chip_sum(f"chip_sum_{k}", g_big[k], landed[k], chip, _SHARD_AXIS[k]) for k in _BIG]
    siblings = _pair_swap(sums)

    packed = _all_reduce_small(_pack([g_small[k] for k in _SMALL] + [loss]))
    small_like = [w[k] for k in _SMALL]
    g_red = dict(zip(_SMALL, _unpack(packed, small_like)))
    loss_row = packed.shape[0] - 8
    loss_out = packed[loss_row, 0]

    grad, delta, new_m, new_v = {}, {}, {}, {}
    for k, mine, theirs in zip(_BIG, sums, siblings):
        shp = w[k].shape
        two = (shp[0] * shp[1], shp[2])
        outs = _adamw(f"adamw_{k}", w[k].reshape(two), m[k].reshape(two), v[k].reshape(two),
                      (mine.reshape(two), theirs.reshape(two)))
        grad[k], delta[k], new_m[k], new_v[k] = [o.reshape(shp) for o in outs]
    _, d, nm, nv = _adamw("adamw_small", _pack(small_like), _pack([m[k] for k in _SMALL]), _pack([v[k] for k in _SMALL]),
                          (packed[:loss_row],))
    for k, dk, mk, vk in zip(_SMALL, _unpack(d, small_like), _unpack(nm, small_like), _unpack(nv, small_like)):
        grad[k], delta[k], new_m[k], new_v[k] = g_red[k], dk, mk, vk

    order = ("mix_norm", "ffn_norm", "w_ffn_in", "w_ffn_out", "ab_w_in", "ab_gn_gain", "ab_w_pool", "ab_pool_scale",
             "ab_w_out", "c_w_qkv", "c_rel_bias", "c_w_out", "final_norm")
    return (loss_out, grad_x.reshape(x.shape), *[grad[k] for k in order], *[delta[k] for k in order],
            *[new_m[k] for k in order], *[new_v[k] for k in order])
```

```python
import functools
from typing import Callable, NamedTuple

import numpy as np
import jax
import jax.numpy as jnp
from jax import lax
from jax.experimental import pallas as pl
from jax.experimental.pallas import tpu as pltpu

F32 = jnp.float32
BF16 = jnp.bfloat16

D_MODEL = 1024
D_FF = 4096
DEPTH = 4
CHUNK = 64
RMS_EPS = 1e-6
RET_WIDTH = 512
RET_HEADS = 4
RET_HEAD_DIM = 128
RET_ROPE_BASE = 10000.0
GN_EPS = 1e-5
POOL_WIDTH = 512
POOL_WINDOWS = (2, 4, 8, 16)
POOL_GROUP_DIM = 128
POOL_HALO = 16
AB_IN_WIDTH = 2560
ATT_HEADS = 16
ATT_HEAD_DIM = 64
LEFT_CHUNKS = 8
BAND = (LEFT_CHUNKS + 1) * CHUNK
REL_CLIP = 128
N_REL = 2 * REL_CLIP + 1
N_REL_PAD = 264
NEG_INF = -1e30
KSCALE = RET_HEAD_DIM ** -0.5
QSCALE = ATT_HEAD_DIM ** -0.5

ADAM_LR = 0.001
ADAM_B1 = 0.9
ADAM_B2 = 0.999
ADAM_EPS = 1e-08
ADAM_WD = 0.01
ADAM_STEP = 10

ATT_BLOCK = LEFT_CHUNKS * CHUNK
RET_BLOCK = 512
N_CHIPS = 4
N_DEV = 8
LANE = 128
VMEM_LIMIT = 52 * 1024 * 1024
MESH = pl.DeviceIdType.MESH


def _cp(sem, vmem=VMEM_LIMIT):
    return pltpu.CompilerParams(dimension_semantics=sem, vmem_limit_bytes=vmem)


def _dot(a, b):
    return lax.dot_general(a, b, (((1,), (0,)), ((), ())), preferred_element_type=F32)


def _dot_nt(a, b):
    return lax.dot_general(a, b, (((1,), (1,)), ((), ())), preferred_element_type=F32)


def _dot_tn(a, b):
    return lax.dot_general(a, b, (((0,), (0,)), ((), ())), preferred_element_type=F32)


_ANY = pl.BlockSpec(memory_space=pl.ANY)


class _Rider(NamedTuple):
    operands: tuple
    out_shapes: tuple
    n_sems: int
    start: Callable
    finish: Callable
    aliases: tuple = ()


def _mm(name, mode, a, b, *, la=None, lb=None, tm=1024, tn=1024, tk=1024, a_fn=None, b_fn=None,
        extras=(), aux=(), sides=(), epi=None, out_dtype=F32, stack=None, rider=None):
    a_parts = list(a) if isinstance(a, (list, tuple)) else [a]
    b_parts = list(b) if isinstance(b, (list, tuple)) else [b]
    na, nbp = len(a_parts), len(b_parts)
    a2, b2 = list(a_parts[0].shape[-2:]), list(b_parts[0].shape[-2:])
    a2[1] *= na
    b2[1] *= nbp
    if mode == "nn":
        (M, K), (K2, N) = a2, b2
    elif mode == "nt":
        (M, K), (N, K2) = a2, b2
    else:
        (K, M), (K2, N) = a2, b2
    assert K == K2, (name, a2, b2)
    tm, tn, tk = min(tm, M), min(tn, N), min(tk, K)
    assert M % tm == 0 and N % tn == 0 and K % tk == 0, (name, M, N, K, tm, tn, tk)
    gm, gn, gk = M // tm, N // tn, K // tk

    def specs(parts, block, idx, lead):
        per = parts[0].shape[-1] // block[1]
        assert parts[0].shape[-1] % block[1] == 0, (name, parts[0].shape, block)
        out = []
        for p in range(len(parts)):
            def f(i, j, k, p=p):
                r, c = idx(i, j, k)
                if len(parts) > 1:
                    c = jnp.clip(c - p * per, 0, per - 1)
                return (r, c) if lead is None else (lead, r, c)
            out.append(pl.BlockSpec(block if lead is None else (None,) + block, f))
        return out, per

    if mode == "nn":
        a_specs, a_per = specs(a_parts, (tm, tk), lambda i, j, k: (i, k), la)
        b_specs, b_per = specs(b_parts, (tk, tn), lambda i, j, k: (k, j), lb)
        a_axis, b_axis, dot = 2, 1, _dot
    elif mode == "nt":
        a_specs, a_per = specs(a_parts, (tm, tk), lambda i, j, k: (i, k), la)
        b_specs, b_per = specs(b_parts, (tn, tk), lambda i, j, k: (j, k), lb)
        a_axis, b_axis, dot = 2, 2, _dot_nt
    else:
        a_specs, a_per = specs(a_parts, (tk, tm), lambda i, j, k: (k, i), la)
        b_specs, b_per = specs(b_parts, (tk, tn), lambda i, j, k: (k, j), lb)
        a_axis, b_axis, dot = 0, 1, _dot_tn
    ex_specs = [pl.BlockSpec((tm, tn), lambda i, j, k: (i, j)) for _ in extras]
    n_ex = len(extras)

    n_aux, n_side = len(aux), len(sides)
    operands = a_parts + b_parts + list(extras) + list(aux)
    in_specs = a_specs + b_specs + ex_specs + [pl.BlockSpec(v.shape, lambda i, j, k, nd=v.ndim: (0,) * nd) for v in aux]
    aliases = {}
    if stack is None:
        out_specs = [pl.BlockSpec((tm, tn), lambda i, j, k: (i, j))]
        out_shapes = [jax.ShapeDtypeStruct((M, N), out_dtype)]
    else:
        n_layers, layer, prev = stack
        out_specs = [pl.BlockSpec((None, tm, tn), lambda i, j, k: (layer, i, j))]
        out_shapes = [jax.ShapeDtypeStruct((n_layers, M, N), out_dtype)]
        if prev is not None:
            aliases = {len(operands): 0}
            operands.append(prev)
            in_specs.append(_ANY)
    for kind, dtype in sides:
        if kind == "tile":
            out_specs.append(pl.BlockSpec((tm, tn), lambda i, j, k: (i, j)))
            out_shapes.append(jax.ShapeDtypeStruct((M, N), dtype))
        else:
            assert gn == 1, name
            out_specs.append(pl.BlockSpec((8, tn), lambda i, j, k: (0, 0)))
            out_shapes.append(jax.ShapeDtypeStruct((8, N), dtype))
    n_prev = len(aliases)
    scratch = [pltpu.VMEM((tm, tn), F32)] if gk > 1 else []
    n_rin = n_rout = 0
    if rider is not None:
        n_rin, n_rout = len(rider.operands), len(rider.out_shapes)
        for src, dst in rider.aliases:
            aliases[len(operands) + src] = 1 + n_side + dst
        operands += list(rider.operands)
        in_specs += [_ANY] * n_rin
        out_specs += [_ANY] * n_rout
        out_shapes += list(rider.out_shapes)
        scratch += [pltpu.SemaphoreType.DMA((rider.n_sems,)), pltpu.SemaphoreType.DMA((rider.n_sems,))]
    assert na == 1 or nbp == 1, name

    def body(*refs):
        a_refs, b_refs = refs[:na], refs[na:na + nbp]
        ex_refs = refs[na + nbp:na + nbp + n_ex + n_aux]
        n_in = na + nbp + n_ex + n_aux + n_prev
        rin = refs[n_in:n_in + n_rin]
        o_ref = refs[n_in + n_rin]
        side_refs = refs[n_in + n_rin + 1:n_in + n_rin + 1 + n_side]
        rout = refs[n_in + n_rin + 1 + n_side:n_in + n_rin + 1 + n_side + n_rout]
        rest = refs[n_in + n_rin + 1 + n_side + n_rout:]
        i, j, k = pl.program_id(0), pl.program_id(1), pl.program_id(2)
        if rider is not None:
            sems = rest[-2:]

            @pl.when(jnp.logical_and(i == 0, jnp.logical_and(j == 0, k == 0)))
            def _():
                rider.start(rin, rout, *sems)

        def finish(acc):
            if epi is not None:
                acc = epi(acc, *[r[...] for r in ex_refs])
            if n_side:
                acc, *side_vals = acc
                for (kind, _), ref, val in zip(sides, side_refs, side_vals):
                    if kind == "tile":
                        ref[...] = val.astype(ref.dtype)
                    else:
                        @pl.when(i == 0)
                        def _(ref=ref, val=val):
                            ref[...] = val

                        @pl.when(i > 0)
                        def _(ref=ref, val=val):
                            ref[...] += val

                        @pl.when(i == gm - 1)
                        def _(ref=ref):
                            ref[0:1, :] = jnp.sum(ref[...], axis=0, keepdims=True)
            o_ref[...] = acc.astype(o_ref.dtype)

        def step(a_ref, b_ref):
            av, bv = a_ref[...], b_ref[...]
            if a_fn is not None:
                av = a_fn(av)
            if b_fn is not None:
                bv = b_fn(bv)
            part = dot(av.astype(BF16), bv.astype(BF16))
            if gk == 1:
                finish(part)
                return
            acc_ref = rest[0]

            @pl.when(k == 0)
            def _():
                acc_ref[...] = part

            @pl.when(k > 0)
            def _():
                acc_ref[...] += part

        if na > 1:
            sel = pl.program_id(a_axis) // a_per
            for p in range(na):
                pl.when(sel == p)(functools.partial(step, a_refs[p], b_refs[0]))
        elif nbp > 1:
            sel = pl.program_id(b_axis) // b_per
            for p in range(nbp):
                pl.when(sel == p)(functools.partial(step, a_refs[0], b_refs[p]))
        else:
            step(a_refs[0], b_refs[0])
        if gk > 1:
            @pl.when(k == gk - 1)
            def _():
                finish(rest[0][...])

        if rider is not None:
            @pl.when(jnp.logical_and(i == gm - 1, jnp.logical_and(j == gn - 1, k == gk - 1)))
            def _():
                rider.finish(rin, rout, *sems)

    sequential = rider is not None or any(kind == "colsum" for kind, _ in sides)
    sem = ("arbitrary",) * 3 if sequential else ("parallel", "parallel", "arbitrary")
    outs = pl.pallas_call(
        body,
        name=name,
        grid=(gm, gn, gk),
        in_specs=in_specs,
        out_specs=out_specs,
        out_shape=out_shapes,
        input_output_aliases=aliases,
        scratch_shapes=scratch,
        compiler_params=_cp(sem),
    )(*operands)
    res = outs[0] if not sides else tuple(outs[:1 + n_side])
    return res if rider is None else (res, list(outs[1 + n_side:]))


def _relu2(u):
    r = jnp.maximum(u, 0)
    return r * r


def _epi_residual(acc, res):
    return acc + res


def _epi_residual_norm(acc, res, g):
    xn = acc + res
    r = lax.rsqrt(jnp.mean(xn * xn, axis=-1, keepdims=True) + RMS_EPS)
    return xn, (xn * r) * g


def _epi_rms_bwd(dh, x, dres, g):
    r = lax.rsqrt(jnp.mean(x * x, axis=-1, keepdims=True) + RMS_EPS)
    xh = x * r
    dxh = dh * g
    dx = dres + r * (dxh - xh * jnp.mean(dxh * xh, axis=-1, keepdims=True))
    return dx, jnp.sum((dh * xh).reshape(dh.shape[0] // 8, 8, dh.shape[1]), axis=0)


def _rms_fwd(name, x, g):
    S, D = x.shape
    tq = min(1024, S)

    def body(x_ref, g_ref, o_ref):
        xv = x_ref[...]
        r = lax.rsqrt(jnp.mean(xv * xv, axis=-1, keepdims=True) + RMS_EPS)
        o_ref[...] = ((xv * r) * g_ref[...]).astype(o_ref.dtype)

    return pl.pallas_call(
        body,
        name=name,
        grid=(S // tq,),
        in_specs=[pl.BlockSpec((tq, D), lambda i: (i, 0)), pl.BlockSpec((1, D), lambda i: (0, 0))],
        out_specs=pl.BlockSpec((tq, D), lambda i: (i, 0)),
        out_shape=jax.ShapeDtypeStruct((S, D), BF16),
        compiler_params=_cp(("parallel",)),
    )(x, g)


def _loss_head(x, g, t):
    S, D = x.shape
    tq = min(512, S)
    n = S // tq

    def body(x_ref, g_ref, t_ref, loss_ref, dx_ref, dg_ref, lacc_ref, gacc_ref):
        i = pl.program_id(0)
        xv = x_ref[...]
        gv = g_ref[...]
        r = lax.rsqrt(jnp.mean(xv * xv, axis=-1, keepdims=True) + RMS_EPS)
        xh = xv * r
        e = xh * gv - t_ref[...]
        dy = e * (1.0 / D)
        dxh = dy * gv
        dx_ref[...] = r * (dxh - xh * jnp.mean(dxh * xh, axis=-1, keepdims=True))
        lpart = jnp.sum((e * e).reshape(tq // 8, 8, D), axis=0)
        gpart = jnp.sum((dy * xh).reshape(tq // 8, 8, D), axis=0)

        @pl.when(i == 0)
        def _():
            lacc_ref[...] = lpart
            gacc_ref[...] = gpart

        @pl.when(i > 0)
        def _():
            lacc_ref[...] += lpart
            gacc_ref[...] += gpart

        @pl.when(i == n - 1)
        def _():
            dg_ref[...] = jnp.sum(gacc_ref[...], axis=0, keepdims=True)
            tot = jnp.sum(jnp.sum(lacc_ref[...], axis=0, keepdims=True), axis=1, keepdims=True)
            loss_ref[...] = jnp.broadcast_to(tot * (0.5 / D), (1, LANE))

    return pl.pallas_call(
        body,
        name="loss_head",
        grid=(n,),
        in_specs=[pl.BlockSpec((tq, D), lambda i: (i, 0)), pl.BlockSpec((1, D), lambda i: (0, 0)),
                  pl.BlockSpec((tq, D), lambda i: (i, 0))],
        out_specs=[pl.BlockSpec((1, LANE), lambda i: (0, 0)), pl.BlockSpec((tq, D), lambda i: (i, 0)),
                   pl.BlockSpec((1, D), lambda i: (0, 0))],
        out_shape=[jax.ShapeDtypeStruct((1, LANE), F32), jax.ShapeDtypeStruct((S, D), F32),
                   jax.ShapeDtypeStruct((1, D), F32)],
        scratch_shapes=[pltpu.VMEM((8, D), F32), pltpu.VMEM((8, D), F32)],
        compiler_params=_cp(("arbitrary",)),
    )(x, g, t)


def _ret_tables(S):
    T = min(RET_BLOCK, S)
    inv_freq = 1.0 / (RET_ROPE_BASE ** jnp.linspace(0.0, 1.0, RET_HEAD_DIM // 2, dtype=F32))
    ang = jnp.arange(S, dtype=F32)[:, None] * inv_freq[None, :]
    cos, sin = jnp.cos(ang), jnp.sin(ang)
    cosf = jnp.repeat(cos, 2, axis=-1)
    sins = jnp.stack([-sin, sin], axis=-1).reshape(S, RET_HEAD_DIM)
    log_g = np.log1p(-np.power(2.0, -5.0 - np.arange(RET_HEADS, dtype=np.float64)))
    pos = np.arange(T, dtype=np.float64)
    diff = pos[:, None] - pos[None, :]
    same = (pos[:, None] // CHUNK) == (pos[None, :] // CHUNK)
    seen = same | (diff > 0)
    dmat = np.where(seen[None], np.exp(np.abs(diff)[None] * log_g[:, None, None]), 0.0)
    aq = np.exp((pos[None, :] + 1.0) * log_g[:, None])
    ak = np.exp((T - 1.0 - pos[None, :]) * log_g[:, None])
    lam = np.exp(T * log_g)
    bc = lambda v: jnp.asarray(np.broadcast_to(v[..., None], v.shape + (LANE,)), F32)
    return dict(cos=cosf, sin=sins, dmat=jnp.asarray(dmat, F32), aq=bc(aq), ak=bc(ak),
                lam=jnp.asarray(np.broadcast_to(lam[:, None, None], (RET_HEADS, 1, LANE)), F32))


def _rot(x, cos, sin_s, even):
    sw = jnp.where(even, pltpu.roll(x, LANE - 1, 1), pltpu.roll(x, 1, 1))
    return x * cos + sw * sin_s


def _rot_t(dy, cos, sin_s, even):
    t = dy * sin_s
    return dy * cos + jnp.where(even, pltpu.roll(t, LANE - 1, 1), pltpu.roll(t, 1, 1))


def _ret_specs(T, rev_nb=None):
    blk = (lambda b: b) if rev_nb is None else (lambda b: rev_nb - 1 - b)
    whole = lambda shape: pl.BlockSpec(shape, lambda b: (0,) * len(shape))
    specs = [pl.BlockSpec((T, AB_IN_WIDTH), lambda b: (blk(b), 0)),
             pl.BlockSpec((T, LANE), lambda b: (blk(b), 0)),
             pl.BlockSpec((T, LANE), lambda b: (blk(b), 0)),
             whole((RET_HEADS, T, T)), whole((RET_HEADS, T, LANE)), whole((RET_HEADS, T, LANE)),
             whole((RET_HEADS, 1, LANE)), whole((1, RET_WIDTH))]
    return specs, blk


def _head_views(h, z_ref, tabs, token_refs, head_refs):
    zs = [z_ref.at[:, (o * RET_HEADS + h) * LANE:(o * RET_HEADS + h + 1) * LANE] for o in range(4)]
    hs = slice(h * LANE, (h + 1) * LANE)
    return zs, [t.at[h] for t in tabs], [r.at[:, hs] for r in token_refs], [r.at[h] for r in head_refs]


def _ret_fwd(name, z, tb, gain):
    S = z.shape[0]
    T = min(RET_BLOCK, S)
    nb = S // T
    specs, blk = _ret_specs(T)

    def body(z_ref, cos_r, sin_r, d_all, aq_all, ak_all, lam_all, gain_all, cat_all, opre_all, st_all, state_all):
        @pl.when(pl.program_id(0) == 0)
        def _():
            state_all[...] = jnp.zeros_like(state_all)

        for h in range(RET_HEADS):
            zs, tabs, toks, heads = _head_views(h, z_ref, (d_all, aq_all, ak_all, lam_all),
                                                (gain_all, cat_all, opre_all), (st_all, state_all))
            head(*zs, cos_r, sin_r, *tabs, *toks, *heads)

    def head(zq, zk, zv, zg, cos_r, sin_r, d_r, aq_r, ak_r, lam_r, gain_r, ret_o, opre_o, st_o, state):
        even = (lax.broadcasted_iota(jnp.int32, (T, LANE), 1) & 1) == 0
        c, s = cos_r[...], sin_r[...]
        q = _rot(zq[...], c, s, even)
        k = _rot(zk[...], c, s, even) * KSCALE
        qb, kb, vb = q.astype(BF16), k.astype(BF16), zv[...].astype(BF16)
        p = (_dot_nt(qb, kb) * d_r[...]).astype(BF16)
        st = state[...]
        st_o[...] = st
        o = _dot(p, vb) + _dot((q * aq_r[...]).astype(BF16), st.astype(BF16))
        state[...] = st * lam_r[...] + _dot_tn((k * ak_r[...]).astype(BF16), vb)
        opre_o[...] = o
        mu = jnp.mean(o, axis=-1, keepdims=True)
        d = o - mu
        y = d * lax.rsqrt(jnp.mean(d * d, axis=-1, keepdims=True) + GN_EPS)
        g = zg[...]
        ret_o[...] = ((g * jax.nn.sigmoid(g)) * (y * gain_r[...])).astype(ret_o.dtype)

    out_blk = pl.BlockSpec((T, RET_WIDTH), lambda b: (b, 0))
    return pl.pallas_call(
        body,
        name=name,
        grid=(nb,),
        in_specs=specs,
        out_specs=[out_blk, out_blk, pl.BlockSpec((RET_HEADS, None, LANE, LANE), lambda b: (0, b, 0, 0))],
        out_shape=[jax.ShapeDtypeStruct((S, D_MODEL), BF16), jax.ShapeDtypeStruct((S, RET_WIDTH), F32),
                   jax.ShapeDtypeStruct((RET_HEADS, nb, LANE, LANE), F32)],
        scratch_shapes=[pltpu.VMEM((RET_HEADS, LANE, LANE), F32)],
        compiler_params=_cp(("arbitrary",)),
    )(z, tb["cos"], tb["sin"], tb["dmat"], tb["aq"], tb["ak"], tb["lam"], gain)


def _ret_bwd(name, z, tb, gain, opre, states, dcat):
    S = z.shape[0]
    T = min(RET_BLOCK, S)
    nb = S // T
    specs, blk = _ret_specs(T, rev_nb=nb)
    tok = pl.BlockSpec((T, RET_WIDTH), lambda b: (blk(b), 0))

    def body(z_ref, cos_r, sin_r, d_all, aq_all, ak_all, lam_all, gain_all, opre_all, st_all, dret_all,
             dz_ref, dgain_all, dstate_all):
        @pl.when(pl.program_id(0) == 0)
        def _():
            dstate_all[...] = jnp.zeros_like(dstate_all)
            dgain_all[...] = jnp.zeros_like(dgain_all)

        for h in range(RET_HEADS):
            zs, tabs, toks, heads = _head_views(h, z_ref, (d_all, aq_all, ak_all, lam_all),
                                                (gain_all, opre_all, dret_all, dgain_all), (st_all, dstate_all))
            dzs, _, _, _ = _head_views(h, dz_ref, (), (), ())
            gain_r, opre_r, dret_r, dgain_o = toks
            head(*zs, cos_r, sin_r, *tabs, gain_r, opre_r, heads[0], dret_r, *dzs, dgain_o, heads[1])

    def head(zq, zk, zv, zg, cos_r, sin_r, d_r, aq_r, ak_r, lam_r, gain_r, opre_r, st_r, dret_r,
             dq_o, dk_o, dv_o, dg_o, dgain_o, dstate):
        even = (lax.broadcasted_iota(jnp.int32, (T, LANE), 1) & 1) == 0
        c, s = cos_r[...], sin_r[...]
        aq, ak, dm = aq_r[...], ak_r[...], d_r[...]
        q = _rot(zq[...], c, s, even)
        k = _rot(zk[...], c, s, even) * KSCALE
        qb, kb, vb = q.astype(BF16), k.astype(BF16), zv[...].astype(BF16)
        pb = (_dot_nt(qb, kb) * dm).astype(BF16)
        g = zg[...]
        sig = jax.nn.sigmoid(g)
        o = opre_r[...]
        mu = jnp.mean(o, axis=-1, keepdims=True)
        d = o - mu
        rstd = lax.rsqrt(jnp.mean(d * d, axis=-1, keepdims=True) + GN_EPS)
        y = d * rstd
        gain_v = gain_r[...]
        dret = dret_r[...].astype(F32)
        dyg = dret * (g * sig)
        dg_o[...] = (dret * (y * gain_v) * (sig * (1.0 + g * (1.0 - sig)))).astype(dg_o.dtype)
        dgain_o[...] += jnp.sum(dyg * y, axis=0, keepdims=True)
        dy = dyg * gain_v
        do = rstd * (dy - jnp.mean(dy, axis=-1, keepdims=True) - y * jnp.mean(dy * y, axis=-1, keepdims=True))
        dob = do.astype(BF16)
        stb = st_r[...].astype(BF16)
        dsn = dstate[...]
        dsnb = dsn.astype(BF16)
        dpb = (_dot_nt(dob, vb) * dm).astype(BF16)
        dq = _dot(dpb, kb) + _dot_nt(dob, stb) * aq
        dk = _dot_tn(dpb, qb) + _dot_nt(vb, dsnb) * ak
        dv = _dot_tn(pb, dob) + _dot((k * ak).astype(BF16), dsnb)
        dstate[...] = dsn * lam_r[...] + _dot_tn((q * aq).astype(BF16), dob)
        dq_o[...] = _rot_t(dq, c, s, even).astype(dq_o.dtype)
        dk_o[...] = _rot_t(dk * KSCALE, c, s, even).astype(dk_o.dtype)
        dv_o[...] = dv.astype(dv_o.dtype)

    return pl.pallas_call(
        body,
        name=name,
        grid=(nb,),
        in_specs=specs + [tok, pl.BlockSpec((RET_HEADS, None, LANE, LANE), lambda b: (0, blk(b), 0, 0)), tok],
        out_specs=[pl.BlockSpec((T, 4 * RET_WIDTH), lambda b: (blk(b), 0)), pl.BlockSpec((1, RET_WIDTH), lambda b: (0, 0))],
        out_shape=[jax.ShapeDtypeStruct((S, AB_IN_WIDTH), BF16), jax.ShapeDtypeStruct((1, RET_WIDTH), F32)],
        scratch_shapes=[pltpu.VMEM((RET_HEADS, LANE, LANE), F32)],
        compiler_params=_cp(("arbitrary",)),
    )(z, tb["cos"], tb["sin"], tb["dmat"], tb["aq"], tb["ak"], tb["lam"], gain, opre, states, dcat)


def _pool_counts(t0, rows):
    t = t0 + lax.broadcasted_iota(jnp.int32, (rows, POOL_WIDTH), 0)
    grp = lax.broadcasted_iota(jnp.int32, (rows, POOL_WIDTH), 1) >> 7
    win = jnp.where(grp == 0, POOL_WINDOWS[0], jnp.where(grp == 1, POOL_WINDOWS[1],
                    jnp.where(grp == 2, POOL_WINDOWS[2], POOL_WINDOWS[3])))
    return jnp.maximum(jnp.minimum(t + 1, win), 1).astype(F32), grp


def _window_sums(ext, grp, sign):
    n = ext.shape[0]
    sh = lambda v, k: pltpu.roll(v, k % n if sign > 0 else (n - k) % n, 0)
    s2 = ext + sh(ext, 1)
    s4 = s2 + sh(s2, 2)
    s8 = s4 + sh(s4, 4)
    s16 = s8 + sh(s8, 8)
    return jnp.where(grp == 0, s2, jnp.where(grp == 1, s4, jnp.where(grp == 2, s8, s16)))


def _pool_fwd(name, z, w_pool, scale, cat):
    S = z.shape[0]
    T = min(512, S)
    nb = S // T
    pcol = AB_IN_WIDTH // POOL_WIDTH - 1
    hb = T // POOL_HALO

    def body(p_ref, halo_ref, w_ref, sc_ref, cat_in, out_ref, pooled_ref):
        b = pl.program_id(0)
        cur = p_ref[...]
        halo = jnp.where(b > 0, halo_ref[...], 0.0)
        ext = jnp.concatenate([halo, cur], axis=0)
        cnt, grp = _pool_counts(b * T - POOL_HALO, T + POOL_HALO)
        sums = _window_sums(ext, grp, +1)
        pooled = (sums / cnt)[POOL_HALO:] - cur
        pb = pooled.astype(BF16)
        pooled_ref[...] = pb
        for gi in range(len(POOL_WINDOWS)):
            cs = slice(gi * POOL_GROUP_DIM, (gi + 1) * POOL_GROUP_DIM)
            mixed = _dot(pb[:, cs], w_ref[gi].astype(BF16))
            out_ref[:, cs] = (mixed * sc_ref[:, cs]).astype(out_ref.dtype)

    return pl.pallas_call(
        body,
        name=name,
        grid=(nb,),
        in_specs=[pl.BlockSpec((T, POOL_WIDTH), lambda b: (b, pcol)),
                  pl.BlockSpec((POOL_HALO, POOL_WIDTH), lambda b: (jnp.maximum(b * hb - 1, 0), pcol)),
                  pl.BlockSpec((4, POOL_GROUP_DIM, POOL_GROUP_DIM), lambda b: (0, 0, 0)),
                  pl.BlockSpec((1, POOL_WIDTH), lambda b: (0, 0)), _ANY],
        out_specs=[pl.BlockSpec((T, POOL_WIDTH), lambda b: (b, 1)), pl.BlockSpec((T, POOL_WIDTH), lambda b: (b, 0))],
        out_shape=[jax.ShapeDtypeStruct(cat.shape, cat.dtype), jax.ShapeDtypeStruct((S, POOL_WIDTH), BF16)],
        input_output_aliases={4: 0},
        compiler_params=_cp(("parallel",)),
    )(z, z, w_pool, scale, cat)


def _pool_bwd(name, pooled, w_pool, scale, dcat, dz):
    S = pooled.shape[0]
    T = min(512, S)
    nb = S // T
    hb = T // POOL_HALO
    last_h = S // POOL_HALO - 1
    pcol = AB_IN_WIDTH // POOL_WIDTH - 1

    def body(d_ref, dn_ref, pooled_ref, w_ref, sc_ref, dz_in, dp_ref, dw_ref, dsc_ref):
        b = pl.program_id(0)

        @pl.when(b == 0)
        def _():
            dw_ref[...] = jnp.zeros_like(dw_ref)
            dsc_ref[...] = jnp.zeros_like(dsc_ref)

        sc = sc_ref[...]
        dout = d_ref[...].astype(F32)
        dnext = jnp.where(b < nb - 1, dn_ref[...].astype(F32), 0.0)
        dmix = jnp.concatenate([dout, dnext], axis=0) * sc
        dmb = dmix.astype(BF16)
        pb = pooled_ref[...]
        dpooled = []
        for gi in range(len(POOL_WINDOWS)):
            cs = slice(gi * POOL_GROUP_DIM, (gi + 1) * POOL_GROUP_DIM)
            wb = w_ref[gi].astype(BF16)
            dpooled.append(_dot_nt(dmb[:, cs], wb))
            dw_ref[gi] += _dot_tn(pb[:, cs], dmb[:T, cs])
            mixed = _dot(pb[:, cs], wb)
            dsc_ref[:, cs] += jnp.sum(dout[:, cs] * mixed, axis=0, keepdims=True)
        dpl = jnp.concatenate(dpooled, axis=1)
        cnt, grp = _pool_counts(b * T, T + POOL_HALO)
        sums = _window_sums(dpl / cnt, grp, -1)
        dp_ref[...] = (sums[:T] - dpl[:T]).astype(dp_ref.dtype)

    return pl.pallas_call(
        body,
        name=name,
        grid=(nb,),
        in_specs=[pl.BlockSpec((T, POOL_WIDTH), lambda b: (b, 1)),
                  pl.BlockSpec((POOL_HALO, POOL_WIDTH), lambda b: (jnp.minimum((b + 1) * hb, last_h), 1)),
                  pl.BlockSpec((T, POOL_WIDTH), lambda b: (b, 0)),
                  pl.BlockSpec((4, POOL_GROUP_DIM, POOL_GROUP_DIM), lambda b: (0, 0, 0)),
                  pl.BlockSpec((1, POOL_WIDTH), lambda b: (0, 0)), _ANY],
        out_specs=[pl.BlockSpec((T, POOL_WIDTH), lambda b: (b, pcol)),
                   pl.BlockSpec((4, POOL_GROUP_DIM, POOL_GROUP_DIM), lambda b: (0, 0, 0)),
                   pl.BlockSpec((1, POOL_WIDTH), lambda b: (0, 0))],
        out_shape=[jax.ShapeDtypeStruct(dz.shape, dz.dtype),
                   jax.ShapeDtypeStruct((4, POOL_GROUP_DIM, POOL_GROUP_DIM), F32),
                   jax.ShapeDtypeStruct((1, POOL_WIDTH), F32)],
        input_output_aliases={5: 0},
        compiler_params=_cp(("arbitrary",)),
    )(dcat, dcat, pooled, w_pool, scale, dz)


ATT_Q = 256
ATT_W = ATT_Q + LEFT_CHUNKS * CHUNK
_REL_COLS = CHUNK * ATT_W
_REL_TILE = _REL_COLS // 8


def _rel_index():
    n = np.arange(CHUNK)[:, None]
    j = np.arange(ATT_W)[None, :]
    rel = np.clip(n + LEFT_CHUNKS * CHUNK - j, -REL_CLIP, REL_CLIP) + REL_CLIP
    rel = np.where(j < BAND, rel, N_REL)
    return jnp.asarray(rel.reshape(1, _REL_COLS), jnp.int32)


def _bias_table(name, rel_bias, rel_idx):
    rb = jnp.concatenate([rel_bias, jnp.full((ATT_HEADS, 1), NEG_INF, F32),
                          jnp.zeros((ATT_HEADS, N_REL_PAD - N_REL - 1), F32)], axis=1)

    def body(rb_ref, idx_ref, o_ref):
        r = lax.broadcasted_iota(jnp.int32, (N_REL_PAD, _REL_TILE), 0)
        onehot = (r == idx_ref[...]).astype(F32)
        o_ref[...] = jnp.dot(rb_ref[...], onehot, precision=lax.Precision.HIGHEST, preferred_element_type=F32)

    return pl.pallas_call(
        body,
        name=name,
        grid=(_REL_COLS // _REL_TILE,),
        in_specs=[pl.BlockSpec((ATT_HEADS, N_REL_PAD), lambda i: (0, 0)), pl.BlockSpec((1, _REL_TILE), lambda i: (0, i))],
        out_specs=pl.BlockSpec((ATT_HEADS, _REL_TILE), lambda i: (0, i)),
        out_shape=jax.ShapeDtypeStruct((ATT_HEADS, _REL_COLS), F32),
        compiler_params=_cp(("parallel",)),
    )(rb, rel_idx)


def _bias_grad(name, ds_sum, rel_idx):
    n = _REL_COLS // _REL_TILE

    def body(ds_ref, idx_ref, o_ref):
        i = pl.program_id(0)
        r = lax.broadcasted_iota(jnp.int32, (N_REL_PAD, _REL_TILE), 0)
        onehot = (r == idx_ref[...]).astype(F32)
        part = lax.dot_general(ds_ref[...], onehot, (((1,), (1,)), ((), ())),
                               precision=lax.Precision.HIGHEST, preferred_element_type=F32)

        @pl.when(i == 0)
        def _():
            o_ref[...] = part

        @pl.when(i > 0)
        def _():
            o_ref[...] += part

    out = pl.pallas_call(
        body,
        name=name,
        grid=(n,),
        in_specs=[pl.BlockSpec((ATT_HEADS, _REL_TILE), lambda i: (0, i)), pl.BlockSpec((1, _REL_TILE), lambda i: (0, i))],
        out_specs=pl.BlockSpec((ATT_HEADS, N_REL_PAD), lambda i: (0, 0)),
        out_shape=jax.ShapeDtypeStruct((ATT_HEADS, N_REL_PAD), F32),
        compiler_params=_cp(("arbitrary",)),
    )(ds_sum, rel_idx)
    return out[:, :N_REL]


def _attn_unit(q_ref, kw_ref, bias_ref, e, u, lane):
    mine = (lane < ATT_HEAD_DIM) if e == 0 else (lane >= ATT_HEAD_DIM)
    qm = jnp.where(mine, q_ref[u * ATT_Q:(u + 1) * ATT_Q, :] * QSCALE, 0)
    kw = kw_ref[u * ATT_Q:u * ATT_Q + ATT_W, :]
    s = _dot_nt(qm, kw) + bias_ref[u, e]
    p = jnp.exp(s - jnp.max(s, axis=-1, keepdims=True))
    return p, 1.0 / jnp.sum(p, axis=-1, keepdims=True), qm, kw, mine


def _attn_in_specs(nb):
    T = ATT_BLOCK
    hp = ATT_HEADS // 2
    cur = lambda off: pl.BlockSpec((T, LANE), lambda h, b: (jnp.minimum(b, nb - 1), off + h))
    prev = lambda off: pl.BlockSpec((T, LANE), lambda h, b: (jnp.clip(b - 1, 0, nb - 1), off + h))
    return [cur(0), prev(hp), cur(hp), prev(2 * hp), cur(2 * hp),
            pl.BlockSpec((None, 2, CHUNK, ATT_W), lambda h, b: (h, 0, 0, 0))]


def _spread_bias(bias_ref, bm_ref, block):
    col = lax.broadcasted_iota(jnp.int32, (CHUNK, ATT_W), 1)
    for first in (True, False):
        @pl.when(block == (0 if first else 1))
        def _(first=first):
            for u in range(ATT_BLOCK // ATT_Q):
                for e in range(2):
                    for j in range(ATT_Q // CHUNK):
                        rows = pltpu.roll(bias_ref[e], j * CHUNK, 1)
                        if first:
                            rows = jnp.where(col >= ATT_BLOCK - u * ATT_Q, rows, NEG_INF)
                        bm_ref[u, e, j * CHUNK:(j + 1) * CHUNK, :] = rows


def _attn_fwd(name, qkv, bias):
    S = qkv.shape[0]
    T = ATT_BLOCK
    nb = S // T

    def body(q_ref, kp_ref, kc_ref, vp_ref, vc_ref, band_ref, o_ref, kw_ref, vw_ref, bias_ref):
        _spread_bias(band_ref, bias_ref, pl.program_id(1))
        kw_ref[0:T] = kp_ref[...]
        kw_ref[T:2 * T] = kc_ref[...]
        vw_ref[0:T] = vp_ref[...]
        vw_ref[T:2 * T] = vc_ref[...]
        lane = lax.broadcasted_iota(jnp.int32, (ATT_Q, LANE), 1)
        for u in range(T // ATT_Q):
            vw = vw_ref[u * ATT_Q:u * ATT_Q + ATT_W, :]
            outs = []
            for e in range(2):
                p, inv, _, _, _ = _attn_unit(q_ref, kw_ref, bias_ref, e, u, lane)
                outs.append(_dot(p.astype(BF16), vw) * inv)
            o_ref[u * ATT_Q:(u + 1) * ATT_Q, :] = jnp.where(lane < ATT_HEAD_DIM, outs[0], outs[1]).astype(o_ref.dtype)

    return pl.pallas_call(
        body,
        name=name,
        grid=(ATT_HEADS // 2, nb),
        in_specs=_attn_in_specs(nb),
        out_specs=pl.BlockSpec((T, LANE), lambda h, b: (b, h)),
        out_shape=jax.ShapeDtypeStruct((S, D_MODEL), BF16),
        scratch_shapes=[pltpu.VMEM((2 * T, LANE), BF16), pltpu.VMEM((2 * T, LANE), BF16),
                        pltpu.VMEM((T // ATT_Q, 2, ATT_Q, ATT_W), F32)],
        compiler_params=_cp(("parallel", "arbitrary")),
    )(qkv, qkv, qkv, qkv, qkv, bias)


def _attn_bwd(name, qkv, bias, do):
    S = qkv.shape[0]
    T = ATT_BLOCK
    nb = S // T

    def body(q_ref, kp_ref, kc_ref, vp_ref, vc_ref, band_ref, do_ref,
             dq_ref, dk_ref, dv_ref, dband_ref, kw_ref, vw_ref, dkw_ref, dvw_ref, bias_ref, dbias_ref):
        b = pl.program_id(1)

        _spread_bias(band_ref, bias_ref, b)

        @pl.when(b == 0)
        def _():
            dbias_ref[...] = jnp.zeros_like(dbias_ref)
            dkw_ref[T:2 * T] = jnp.zeros((T, LANE), F32)
            dvw_ref[T:2 * T] = jnp.zeros((T, LANE), F32)

        dkw_ref[0:T] = dkw_ref[T:2 * T]
        dvw_ref[0:T] = dvw_ref[T:2 * T]
        dkw_ref[T:2 * T] = jnp.zeros((T, LANE), F32)
        dvw_ref[T:2 * T] = jnp.zeros((T, LANE), F32)

        @pl.when(b < nb)
        def _():
            kw_ref[0:T] = kp_ref[...]
            kw_ref[T:2 * T] = kc_ref[...]
            vw_ref[0:T] = vp_ref[...]
            vw_ref[T:2 * T] = vc_ref[...]
            lane = lax.broadcasted_iota(jnp.int32, (ATT_Q, LANE), 1)
            for u in range(T // ATT_Q):
                rows = slice(u * ATT_Q, (u + 1) * ATT_Q)
                win = slice(u * ATT_Q, u * ATT_Q + ATT_W)
                vw = vw_ref[win, :]
                do2 = do_ref[rows, :]
                dqs, dk, dv = [], None, None
                for e in range(2):
                    p, inv, qm, kw, mine = _attn_unit(q_ref, kw_ref, bias_ref, e, u, lane)
                    dom = jnp.where(mine, do2, 0)
                    dp = _dot_nt(dom, vw)
                    delta = jnp.sum(p * dp, axis=-1, keepdims=True) * inv
                    ds = p * ((dp - delta) * inv)
                    dbias_ref[e] += ds
                    dsb = ds.astype(BF16)
                    dqs.append(_dot(dsb, kw))
                    dk_e = _dot_tn(dsb, qm)
                    dv_e = _dot_tn((p * inv).astype(BF16), dom)
                    dk = dk_e if dk is None else dk + dk_e
                    dv = dv_e if dv is None else dv + dv_e
                dq_ref[rows, :] = (jnp.where(lane < ATT_HEAD_DIM, dqs[0], dqs[1]) * QSCALE).astype(dq_ref.dtype)
                dkw_ref[win, :] += dk
                dvw_ref[win, :] += dv

        @pl.when(b > 0)
        def _():
            dk_ref[...] = dkw_ref[0:T].astype(dk_ref.dtype)
            dv_ref[...] = dvw_ref[0:T].astype(dv_ref.dtype)

        @pl.when(b == nb)
        def _():
            for e in range(2):
                acc = dbias_ref[e, 0:CHUNK, :]
                for j in range(1, ATT_Q // CHUNK):
                    acc = acc + pltpu.roll(dbias_ref[e, j * CHUNK:(j + 1) * CHUNK, :], ATT_W - j * CHUNK, 1)
                dband_ref[e] = acc

    tok = jax.ShapeDtypeStruct((S, D_MODEL), BF16)
    prev_out = pl.BlockSpec((T, LANE), lambda h, b: (jnp.maximum(b - 1, 0), h))
    return pl.pallas_call(
        body,
        name=name,
        grid=(ATT_HEADS // 2, nb + 1),
        in_specs=_attn_in_specs(nb) + [pl.BlockSpec((T, LANE), lambda h, b: (jnp.minimum(b, nb - 1), h))],
        out_specs=[pl.BlockSpec((T, LANE), lambda h, b: (jnp.minimum(b, nb - 1), h)), prev_out, prev_out,
                   pl.BlockSpec((None, 2, CHUNK, ATT_W), lambda h, b: (h, 0, 0, 0))],
        out_shape=[tok, tok, tok, jax.ShapeDtypeStruct((ATT_HEADS // 2, 2, CHUNK, ATT_W), F32)],
        scratch_shapes=[pltpu.VMEM((2 * T, LANE), BF16), pltpu.VMEM((2 * T, LANE), BF16),
                        pltpu.VMEM((2 * T, LANE), F32), pltpu.VMEM((2 * T, LANE), F32),
                        pltpu.VMEM((T // ATT_Q, 2, ATT_Q, ATT_W), F32), pltpu.VMEM((2, ATT_Q, ATT_W), F32)],
        compiler_params=_cp(("parallel", "arbitrary")),
    )(qkv, qkv, qkv, qkv, qkv, bias, do)


def _local_step(x, target, small, W):
    S = x.shape[0]
    tb = _ret_tables(S)
    rel_idx = _rel_index()
    saved = []
    normed = (("tile", BF16),)
    h = _rms_fwd("mix_norm_fwd0", x, small["mix_norm"][0:1])
    for layer in range(DEPTH):
        i = layer // 2
        st = {"x_in": x, "h": h}
        g_ffn = small["ffn_norm"][layer:layer + 1]
        if layer % 2 == 0:
            z = W.mm(f"ab_in_fwd{layer}", "nn", h, W.get("ab_w_in", i), tn=640, out_dtype=F32)
            gain = small["ab_gn_gain"][i:i + 1]
            cat, opre, states = _ret_fwd(f"ret_fwd{layer}", z, tb, gain)
            cat, pooled = _pool_fwd(f"pool_fwd{layer}", z, small["ab_w_pool"][i], small["ab_pool_scale"][i:i + 1], cat)
            st.update(z=z, opre=opre, states=states, pooled=pooled, cat=cat)
            x, hn = W.mm(f"ab_out_fwd{layer}", "nn", cat, W.get("ab_w_out", i), extras=(x,), aux=(g_ffn,), sides=normed,
                         epi=_epi_residual_norm)
        else:
            qkv = W.mm(f"qkv_fwd{layer}", "nn", h, W.get("c_w_qkv", i), out_dtype=BF16)
            bias = _bias_table(f"bias_table{layer}", small["c_rel_bias"][i], rel_idx)
            bias = bias.reshape(ATT_HEADS // 2, 2, CHUNK, ATT_W)
            att = _attn_fwd(f"attn_fwd{layer}", qkv, bias)
            st.update(qkv=qkv, bias=bias, att=att)
            x, hn = W.mm(f"c_out_fwd{layer}", "nn", att, W.get("c_w_out", i), extras=(x,), aux=(g_ffn,), sides=normed,
                         epi=_epi_residual_norm)
        st["x_mid"] = x
        u = W.mm(f"ffn_in_fwd{layer}", "nn", hn, W.get("w_ffn_in", layer), out_dtype=BF16)
        if layer + 1 < DEPTH:
            x, h = W.mm(f"ffn_out_fwd{layer}", "nn", u, W.get("w_ffn_out", layer), a_fn=_relu2, extras=(x,),
                        aux=(small["mix_norm"][layer + 1:layer + 2],), sides=normed, epi=_epi_residual_norm)
        else:
            x = W.mm(f"ffn_out_fwd{layer}", "nn", u, W.get("w_ffn_out", layer), a_fn=_relu2, extras=(x,),
                     epi=_epi_residual)
        st.update(hn=hn, u=u)
        saved.append(st)

    loss, dx, d_final = _loss_head(x, small["final_norm"].reshape(1, D_MODEL), target)

    gs = {k: [None] * v.shape[0] for k, v in small.items() if k != "final_norm"}
    gb = {k: None for k in W.n_layers}
    landed = {k: None for k in W.n_layers}
    pending = []

    def host(name, *args, **kw):
        if not pending:
            return _mm(name, *args, **kw)
        key, idx = pending.pop(0)
        res, outs = _mm(name, *args, rider=_grad_rider(key, idx, gb[key], landed[key]), **kw)
        landed[key] = outs[0]
        return res

    def dw(name, key, idx, a, b, call=_mm, **kw):
        gb[key] = call(name, "tn", a, b, stack=(W.n_layers[key], idx, gb[key]), out_dtype=BF16, **kw)
        pending.append((key, idx))

    gain_sums = (("colsum", F32),)
    for layer in reversed(range(DEPTH)):
        i = layer // 2
        st = saved[layer]
        du = host(f"ffn_out_bwd{layer}", "nt", dx, W.get("w_ffn_out", layer), extras=(st["u"],),
                  epi=lambda acc, u: acc * (2.0 * jnp.maximum(u, 0).astype(F32)), out_dtype=BF16)
        dw(f"ffn_out_dw{layer}", "w_ffn_out", layer, st["u"], dx, a_fn=_relu2)
        dx, dgain = host(f"ffn_in_bwd{layer}", "nt", du, W.get("w_ffn_in", layer), tm=512, extras=(st["x_mid"], dx),
                         aux=(small["ffn_norm"][layer:layer + 1],), sides=gain_sums, epi=_epi_rms_bwd)
        gs["ffn_norm"][layer] = dgain[0:1]
        dw(f"ffn_in_dw{layer}", "w_ffn_in", layer, st["hn"], du)
        norm_bwd = dict(extras=(st["x_in"], dx), aux=(small["mix_norm"][layer:layer + 1],), sides=gain_sums,
                        epi=_epi_rms_bwd)
        if layer % 2 == 0:
            dcat = _mm(f"ab_out_bwd{layer}", "nt", dx, W.get("ab_w_out", i), out_dtype=BF16)
            dw(f"ab_out_dw{layer}", "ab_w_out", i, st["cat"], dx)
            gain = small["ab_gn_gain"][i:i + 1]
            dz, gs["ab_gn_gain"][i] = _ret_bwd(f"ret_bwd{layer}", st["z"], tb, gain, st["opre"], st["states"], dcat)
            dz, gs["ab_w_pool"][i], gs["ab_pool_scale"][i] = _pool_bwd(
                f"pool_bwd{layer}", st["pooled"], small["ab_w_pool"][i], small["ab_pool_scale"][i:i + 1], dcat, dz)
            dx, dgain = host(f"ab_in_bwd{layer}", "nt", dz, W.get("ab_w_in", i), tm=512, tk=1280, **norm_bwd)
            dw(f"ab_in_dw{layer}", "ab_w_in", i, st["h"], dz, call=host, tn=640)
        else:
            datt = _mm(f"c_out_bwd{layer}", "nt", dx, W.get("c_w_out", i), out_dtype=BF16)
            dw(f"c_out_dw{layer}", "c_w_out", i, st["att"], dx)
            dq, dk, dv, dbias = _attn_bwd(f"attn_bwd{layer}", st["qkv"], st["bias"], datt)
            gs["c_rel_bias"][i] = _bias_grad(f"bias_grad{layer}", dbias.reshape(ATT_HEADS, _REL_COLS), rel_idx)
            dqkv = [dq, dk, dv]
            dx, dgain = host(f"qkv_bwd{layer}", "nt", dqkv, W.get("c_w_qkv", i), tm=512, **norm_bwd)
            dw(f"qkv_dw{layer}", "c_w_qkv", i, st["h"], dqkv, call=host)
        gs["mix_norm"][layer] = dgain[0:1]
    for key, idx in pending:
        landed[key], = _run_rider(f"grad_exchange_{key}{idx}", _grad_rider(key, idx, gb[key], landed[key]))

    g_small = {
        "mix_norm": jnp.concatenate(gs["mix_norm"], axis=0),
        "ffn_norm": jnp.concatenate(gs["ffn_norm"], axis=0),
        "ab_gn_gain": jnp.concatenate(gs["ab_gn_gain"], axis=0),
        "ab_w_pool": jnp.stack(gs["ab_w_pool"], axis=0),
        "ab_pool_scale": jnp.concatenate(gs["ab_pool_scale"], axis=0),
        "c_rel_bias": jnp.stack(gs["c_rel_bias"], axis=0),
        "final_norm": d_final.reshape(D_MODEL),
    }
    return loss, dx, g_small, gb, landed


_BIG = ("w_ffn_in", "w_ffn_out", "ab_w_in", "ab_w_out", "c_w_qkv", "c_w_out")
_SHARD_AXIS = {"w_ffn_in": 2, "w_ffn_out": 1, "ab_w_in": 2, "ab_w_out": 1, "c_w_qkv": 2, "c_w_out": 1}
_SMALL = ("mix_norm", "ffn_norm", "ab_gn_gain", "ab_w_pool", "ab_pool_scale", "c_rel_bias", "final_norm")


def _place():
    x, y, c = lax.axis_index("x"), lax.axis_index("y"), lax.axis_index("c")
    chips = [(1 - x, y), (x, 1 - y), (1 - x, 1 - y)]
    return x, y, c, chips


def _sub(ref, axis, start, size):
    idx = [slice(None)] * len(ref.shape)
    idx[axis] = pl.ds(pl.multiple_of(start, LANE), size)
    return ref.at[tuple(idx)]


def _gather_rider(items, shards):
    keys = sorted({k for k, _ in items})
    n = len(items)
    axes = [_SHARD_AXIS[k] - 1 for k, _ in items]
    sizes = [shards[k].shape[a + 1] for (k, _), a in zip(items, axes)]
    hsizes = [shards[k].shape[2 - a] // 2 for (k, _), a in zip(items, axes)]

    def views(ins, outs, send_sems, recv_sems):
        x, y, c, chips = _place()
        srcs = [ins[keys.index(k)].at[l] for k, l in items]

        def remote(src, dst, s, to):
            return pltpu.make_async_remote_copy(src_ref=src, dst_ref=dst, send_sem=send_sems.at[s],
                                                recv_sem=recv_sems.at[s], device_id=to, device_id_type=MESH)

        def half(w, chip, core):
            return _sub(_sub(outs[w], axes[w], chip * sizes[w], sizes[w]), 1 - axes[w], core * hsizes[w], hsizes[w])

        me = 2 * x + y
        local = [pltpu.make_async_copy(srcs[w], _sub(outs[w], axes[w], me * sizes[w], sizes[w]), send_sems.at[6 * n + w])
                 for w in range(n)]
        first = [remote(_sub(srcs[w], 1 - axes[w], c * hsizes[w], hsizes[w]), half(w, me, c), w * 6 + k, (px, py, c))
                 for w in range(n) for k, (px, py) in enumerate(chips)]
        return x, y, c, chips, remote, half, local, first

    def start(ins, outs, send_sems, recv_sems):
        *_, local, first = views(ins, outs, send_sems, recv_sems)
        for cp in local + first:
            cp.start()

    def finish(ins, outs, send_sems, recv_sems):
        x, y, c, chips, remote, half, local, first = views(ins, outs, send_sems, recv_sems)
        sibling = (x, y, 1 - c)
        passed = []
        for w in range(n):
            for k, (px, py) in enumerate(chips):
                landed = half(w, 2 * px + py, c)
                remote(landed, landed, w * 6 + k, (px, py, c)).wait_recv()
                cp = remote(landed, landed, w * 6 + 3 + k, sibling)
                cp.start()
                passed.append(cp)
        for w in range(n):
            for k, (px, py) in enumerate(chips):
                theirs = half(w, 2 * px + py, 1 - c)
                remote(theirs, theirs, w * 6 + 3 + k, sibling).wait_recv()
        for cp in first + passed:
            cp.wait_send()
        for cp in local:
            cp.wait()

    def full(k, a):
        shape = list(shards[k].shape[1:])
        shape[a] *= N_CHIPS
        return jax.ShapeDtypeStruct(tuple(shape), shards[k].dtype)

    return _Rider(tuple(shards[k] for k in keys), tuple(full(k, a) for (k, _), a in zip(items, axes)), 7 * n, start, finish)


def _mixer_items(layer):
    names = ("ab_w_in", "ab_w_out") if layer % 2 == 0 else ("c_w_qkv", "c_w_out")
    return [(k, layer // 2) for k in names]


class _Weights:
    def __init__(self, shards):
        self.shards = shards
        self.n_layers = {k: shards[k].shape[0] for k in _BIG}
        self.full = {}
        first = _mixer_items(0)
        self._take(first, _run_rider("gather_first", _gather_rider(first, shards)))
        self.plan = {"ab_in_fwd0": [("w_ffn_in", 0)], "ab_out_fwd0": [("w_ffn_out", 0)]}
        for layer in range(1, DEPTH):
            proj = "ab_in" if layer % 2 == 0 else "qkv"
            self.plan[f"ffn_in_fwd{layer - 1}"] = _mixer_items(layer)
            self.plan[f"ffn_out_fwd{layer - 1}"] = [("w_ffn_in", layer)]
            self.plan[f"{proj}_fwd{layer}"] = [("w_ffn_out", layer)]

    def _take(self, items, outs):
        self.full.update(zip(items, outs))

    def get(self, name, layer):
        return self.full[(name, layer)]

    def mm(self, name, *args, **kw):
        items = self.plan.get(name)
        if items is None:
            return _mm(name, *args, **kw)
        res, outs = _mm(name, *args, rider=_gather_rider(items, self.shards), **kw)
        self._take(items, outs)
        return res


def _run_rider(name, rider):
    n_in, n_out = len(rider.operands), len(rider.out_shapes)

    def body(*refs):
        ins, outs, sems = refs[:n_in], refs[n_in:n_in + n_out], refs[n_in + n_out:]
        rider.start(ins, outs, *sems)
        rider.finish(ins, outs, *sems)

    return pl.pallas_call(
        body,
        name=name,
        in_specs=[_ANY] * n_in,
        out_specs=[_ANY] * n_out,
        out_shape=list(rider.out_shapes),
        input_output_aliases=dict(rider.aliases),
        scratch_shapes=[pltpu.SemaphoreType.DMA((rider.n_sems,)), pltpu.SemaphoreType.DMA((rider.n_sems,))],
        compiler_params=pltpu.CompilerParams(has_side_effects=True),
    )(*rider.operands)


def _grad_rider(name, layer, grad, landing):
    axis = _SHARD_AXIS[name] - 1
    L, R, C = grad.shape
    shard = (R // N_CHIPS, C) if axis == 0 else (R, C // N_CHIPS)
    size = shard[axis]

    def copies(ins, outs, send_sems, recv_sems):
        x, y, c, chips = _place()
        return [pltpu.make_async_remote_copy(
            src_ref=_sub(ins[0].at[layer], axis, (2 * px + py) * size, size), dst_ref=outs[0].at[layer, k],
            send_sem=send_sems.at[k], recv_sem=recv_sems.at[k], device_id=(px, py, c), device_id_type=MESH)
            for k, (px, py) in enumerate(chips)]

    def start(ins, outs, send_sems, recv_sems):
        for cp in copies(ins, outs, send_sems, recv_sems):
            cp.start()

    def finish(ins, outs, send_sems, recv_sems):
        cps = copies(ins, outs, send_sems, recv_sems)
        for cp in cps:
            cp.wait_recv()
        for cp in cps:
            cp.wait_send()

    out = jax.ShapeDtypeStruct((L, 3) + shard, grad.dtype)
    if landing is None:
        return _Rider((grad,), (out,), 3, start, finish)
    return _Rider((grad, landing), (out,), 3, start, finish, aliases=((1, 0),))


def _pair_swap(sums):
    n = len(sums)

    def body(*refs):
        ins, outs = refs[:n], refs[n:2 * n]
        send_sems, recv_sems = refs[2 * n:]
        x, y, c, _ = _place()
        cps = [pltpu.make_async_remote_copy(src_ref=ins[w], dst_ref=outs[w], send_sem=send_sems.at[w],
                                            recv_sem=recv_sems.at[w], device_id=(x, y, 1 - c), device_id_type=MESH)
               for w in range(n)]
        for cp in cps:
            cp.start()
        for cp in cps:
            cp.wait_recv()
        for cp in cps:
            cp.wait_send()

    return pl.pallas_call(
        body,
        name="pair_swap",
        in_specs=[_ANY] * n,
        out_specs=[_ANY] * n,
        out_shape=[jax.ShapeDtypeStruct(s.shape, s.dtype) for s in sums],
        scratch_shapes=[pltpu.SemaphoreType.DMA((n,)), pltpu.SemaphoreType.DMA((n,))],
        compiler_params=pltpu.CompilerParams(has_side_effects=True),
    )(*sums)


def _rows_tile(rows, cols):
    tr = rows
    while tr * cols > (1 << 19) and tr % 16 == 0:
        tr //= 2
    return tr


def _chip_sum(name, grad, landed, chip, saxis):
    L = grad.shape[0]
    _, _, R, C = landed.shape
    tr = _rows_tile(R, C)
    nr = R // tr
    if saxis == 2:
        g_idx = lambda l, i, s: (l, i, s[0])
    else:
        g_idx = lambda l, i, s: (l, s[0] * nr + i, 0)

    def body(s_ref, g_ref, l_ref, o_ref):
        tot = ((g_ref[...].astype(F32) + l_ref[0].astype(F32)) + l_ref[1].astype(F32)) + l_ref[2].astype(F32)
        o_ref[...] = tot.astype(o_ref.dtype)

    return pl.pallas_call(
        body,
        name=name,
        grid_spec=pltpu.PrefetchScalarGridSpec(
            num_scalar_prefetch=1,
            grid=(L, nr),
            in_specs=[pl.BlockSpec((None, tr, C), g_idx), pl.BlockSpec((None, 3, tr, C), lambda l, i, s: (l, 0, i, 0))],
            out_specs=pl.BlockSpec((None, tr, C), lambda l, i, s: (l, i, 0)),
        ),
        out_shape=jax.ShapeDtypeStruct((L, R, C), BF16),
        compiler_params=_cp(("parallel", "parallel")),
    )(chip, grad, landed)


def _all_reduce_small(packed):
    R = packed.shape[0]

    def body(p_ref, o_ref, land_ref, send_sems, recv_sems):
        x, y, c, _ = _place()
        me = 4 * x + 2 * y + c
        sends, recvs = [], []
        for r in range(1, N_DEV):
            px, py, pc = x ^ (r >> 2), y ^ ((r >> 1) & 1), c ^ (r & 1)
            cp = pltpu.make_async_remote_copy(src_ref=p_ref, dst_ref=land_ref.at[me], send_sem=send_sems.at[r - 1],
                                              recv_sem=recv_sems.at[r - 1], device_id=(px, py, pc), device_id_type=MESH)
            cp.start()
            sends.append(cp)
            recvs.append(pltpu.make_async_remote_copy(src_ref=p_ref, dst_ref=land_ref.at[4 * px + 2 * py + pc],
                                                      send_sem=send_sems.at[r - 1], recv_sem=recv_sems.at[r - 1],
                                                      device_id=(px, py, pc), device_id_type=MESH))
        land_ref[me] = p_ref[...]
        for cp in recvs:
            cp.wait_recv()
        for cp in sends:
            cp.wait_send()
        acc = land_ref[0]
        for d in range(1, N_DEV):
            acc = acc + land_ref[d]
        o_ref[...] = acc

    vm = pl.BlockSpec(memory_space=pltpu.VMEM)
    return pl.pallas_call(
        body,
        name="all_reduce_small",
        in_specs=[vm],
        out_specs=vm,
        out_shape=jax.ShapeDtypeStruct((R, LANE), F32),
        scratch_shapes=[pltpu.VMEM((N_DEV, R, LANE), F32), pltpu.SemaphoreType.DMA((N_DEV - 1,)),
                        pltpu.SemaphoreType.DMA((N_DEV - 1,))],
        compiler_params=pltpu.CompilerParams(has_side_effects=True, vmem_limit_bytes=VMEM_LIMIT),
    )(packed)


def _adamw(name, w, m, v, grads):
    R, C = w.shape
    tr = _rows_tile(R, C)
    c1 = 1.0 - ADAM_B1 ** ADAM_STEP
    c2 = 1.0 - ADAM_B2 ** ADAM_STEP
    ng = len(grads)

    def body(*refs):
        w_ref, m_ref, v_ref = refs[:3]
        g_refs = refs[3:3 + ng]
        g_ref, d_ref, nm_ref, nv_ref = refs[3 + ng:]
        gv = g_refs[0][...].astype(F32)
        for r in g_refs[1:]:
            gv = gv + r[...].astype(F32)
        g_ref[...] = gv
        nm = ADAM_B1 * m_ref[...] + (1.0 - ADAM_B1) * gv
        nv = ADAM_B2 * v_ref[...] + (1.0 - ADAM_B2) * (gv * gv)
        nm_ref[...] = nm
        nv_ref[...] = nv
        d_ref[...] = -ADAM_LR * ((nm / c1) / (jnp.sqrt(nv / c2) + ADAM_EPS) + ADAM_WD * w_ref[...])

    blk = pl.BlockSpec((tr, C), lambda i: (i, 0))
    out = jax.ShapeDtypeStruct((R, C), F32)
    return pl.pallas_call(
        body,
        name=name,
        grid=(R // tr,),
        in_specs=[blk] * (3 + ng),
        out_specs=[blk] * 4,
        out_shape=[out] * 4,
        compiler_params=_cp(("parallel",)),
    )(w, m, v, *grads)


def _pack(parts):
    rows = []
    for p in parts:
        flat = p.reshape(-1).astype(F32)
        n = -(-flat.shape[0] // (8 * LANE)) * (8 * LANE)
        rows.append(jnp.pad(flat, (0, n - flat.shape[0])).reshape(n // LANE, LANE))
    return jnp.concatenate(rows, axis=0)


def _unpack(packed, like):
    out, r = [], 0
    for p in like:
        size = int(np.prod(p.shape))
        n = -(-size // (8 * LANE)) * 8
        out.append(packed[r:r + n].reshape(-1)[:size].reshape(p.shape))
        r += n
    return out


def kernel(x, mix_norm, ffn_norm, w_ffn_in, w_ffn_out, ab_w_in, ab_gn_gain, ab_w_pool, ab_pool_scale, ab_w_out, c_w_qkv, c_rel_bias, c_w_out, final_norm, loss_target, m_mix_norm, m_ffn_norm, m_w_ffn_in, m_w_ffn_out, m_ab_w_in, m_ab_gn_gain, m_ab_w_pool, m_ab_pool_scale, m_ab_w_out, m_c_w_qkv, m_c_rel_bias, m_c_w_out, m_final_norm, v_mix_norm, v_ffn_norm, v_w_ffn_in, v_w_ffn_out, v_ab_w_in, v_ab_gn_gain, v_ab_w_pool, v_ab_pool_scale, v_ab_w_out, v_c_w_qkv, v_c_rel_bias, v_c_w_out, v_final_norm):
    w = dict(mix_norm=mix_norm, ffn_norm=ffn_norm, w_ffn_in=w_ffn_in, w_ffn_out=w_ffn_out, ab_w_in=ab_w_in,
             ab_gn_gain=ab_gn_gain, ab_w_pool=ab_w_pool, ab_pool_scale=ab_pool_scale, ab_w_out=ab_w_out,
             c_w_qkv=c_w_qkv, c_rel_bias=c_rel_bias, c_w_out=c_w_out, final_norm=final_norm)
    m = dict(mix_norm=m_mix_norm, ffn_norm=m_ffn_norm, w_ffn_in=m_w_ffn_in, w_ffn_out=m_w_ffn_out, ab_w_in=m_ab_w_in,
             ab_gn_gain=m_ab_gn_gain, ab_w_pool=m_ab_w_pool, ab_pool_scale=m_ab_pool_scale, ab_w_out=m_ab_w_out,
             c_w_qkv=m_c_w_qkv, c_rel_bias=m_c_rel_bias, c_w_out=m_c_w_out, final_norm=m_final_norm)
    v = dict(mix_norm=v_mix_norm, ffn_norm=v_ffn_norm, w_ffn_in=v_w_ffn_in, w_ffn_out=v_w_ffn_out, ab_w_in=v_ab_w_in,
             ab_gn_gain=v_ab_gn_gain, ab_w_pool=v_ab_w_pool, ab_pool_scale=v_ab_pool_scale, ab_w_out=v_ab_w_out,
             c_w_qkv=v_c_w_qkv, c_rel_bias=v_c_rel_bias, c_w_out=v_c_w_out, final_norm=v_final_norm)
    S = x.shape[1]
    cx, cy, cc = lax.axis_index("x"), lax.axis_index("y"), lax.axis_index("c")
    chip = jnp.reshape(2 * cx + cy, (1,)).astype(jnp.int32)

    big = _Weights({k: w[k].astype(BF16) for k in _BIG})
    small = {k: w[k] for k in _SMALL}
    loss, grad_x, g_small, g_big, landed = _local_step(x.reshape(S, D_MODEL), loss_target.reshape(S, D_MODEL), small, big)

    sums = [_chip_sum(f"chip_sum_{k}", g_big[k], landed[k], chip, _SHARD_AXIS[k]) for k in _BIG]
    siblings = _pair_swap(sums)

    packed = _all_reduce_small(_pack([g_small[k] for k in _SMALL] + [loss]))
    small_like = [w[k] for k in _SMALL]
    g_red = dict(zip(_SMALL, _unpack(packed, small_like)))
    loss_row = packed.shape[0] - 8
    loss_out = packed[loss_row, 0]

    grad, delta, new_m, new_v = {}, {}, {}, {}
    for k, mine, theirs in zip(_BIG, sums, siblings):
        shp = w[k].shape
        two = (shp[0] * shp[1], shp[2])
        outs = _adamw(f"adamw_{k}", w[k].reshape(two), m[k].reshape(two), v[k].reshape(two),
                      (mine.reshape(two), theirs.reshape(two)))
        grad[k], delta[k], new_m[k], new_v[k] = [o.reshape(shp) for o in outs]
    _, d, nm, nv = _adamw("adamw_small", _pack(small_like), _pack([m[k] for k in _SMALL]), _pack([v[k] for k in _SMALL]),
                          (packed[:loss_row],))
    for k, dk, mk, vk in zip(_SMALL, _unpack(d, small_like), _unpack(nm, small_like), _unpack(nv, small_like)):
        grad[k], delta[k], new_m[k], new_v[k] = g_red[k], dk, mk, vk

    order = ("mix_norm", "ffn_norm", "w_ffn_in", "w_ffn_out", "ab_w_in", "ab_gn_gain", "ab_w_pool", "ab_pool_scale",
             "ab_w_out", "c_w_qkv", "c_rel_bias", "c_w_out", "final_norm")
    return (loss_out, grad_x.reshape(x.shape), *[grad[k] for k in order], *[delta[k] for k in order],
            *[new_m[k] for k in order], *[new_v[k] for k in order])
```

```python
import functools
from typing import Callable, NamedTuple

import numpy as np
import jax
import jax.numpy as jnp
from jax import lax
from jax.experimental import pallas as pl
from jax.experimental.pallas import tpu as pltpu

F32 = jnp.float32
BF16 = jnp.bfloat16

D_MODEL = 1024
D_FF = 4096
DEPTH = 4
CHUNK = 64
RMS_EPS = 1e-6
RET_WIDTH = 512
RET_HEADS = 4
RET_HEAD_DIM = 128
RET_ROPE_BASE = 10000.0
GN_EPS = 1e-5
POOL_WIDTH = 512
POOL_WINDOWS = (2, 4, 8, 16)
POOL_GROUP_DIM = 128
POOL_HALO = 16
AB_IN_WIDTH = 2560
ATT_HEADS = 16
ATT_HEAD_DIM = 64
LEFT_CHUNKS = 8
BAND = (LEFT_CHUNKS + 1) * CHUNK
REL_CLIP = 128
N_REL = 2 * REL_CLIP + 1
N_REL_PAD = 264
NEG_INF = -1e30
KSCALE = RET_HEAD_DIM ** -0.5
QSCALE = ATT_HEAD_DIM ** -0.5

ADAM_LR = 0.001
ADAM_B1 = 0.9
ADAM_B2 = 0.999
ADAM_EPS = 1e-08
ADAM_WD = 0.01
ADAM_STEP = 10

ATT_BLOCK = LEFT_CHUNKS * CHUNK
RET_BLOCK = 512
N_CHIPS = 4
N_DEV = 8
LANE = 128
VMEM_LIMIT = 52 * 1024 * 1024
EPI_ROWS = 256
MESH = pl.DeviceIdType.MESH


def _cp(sem, vmem=VMEM_LIMIT):
    return pltpu.CompilerParams(dimension_semantics=sem, vmem_limit_bytes=vmem)


def _dot(a, b):
    return lax.dot_general(a, b, (((1,), (0,)), ((), ())), preferred_element_type=F32)


def _dot_nt(a, b):
    return lax.dot_general(a, b, (((1,), (1,)), ((), ())), preferred_element_type=F32)


def _dot_tn(a, b):
    return lax.dot_general(a, b, (((0,), (0,)), ((), ())), preferred_element_type=F32)


_ANY = pl.BlockSpec(memory_space=pl.ANY)


class _Rider(NamedTuple):
    operands: tuple
    out_shapes: tuple
    n_sems: int
    start: Callable
    finish: Callable
    aliases: tuple = ()


def _mm(name, mode, a, b, *, la=None, lb=None, tm=1024, tn=1024, tk=1024, a_fn=None, b_fn=None,
        extras=(), aux=(), sides=(), epi=None, out_dtype=F32, stack=None, rider=None):
    a_parts = list(a) if isinstance(a, (list, tuple)) else [a]
    b_parts = list(b) if isinstance(b, (list, tuple)) else [b]
    na, nbp = len(a_parts), len(b_parts)
    a2, b2 = list(a_parts[0].shape[-2:]), list(b_parts[0].shape[-2:])
    a2[1] *= na
    b2[1] *= nbp
    if mode == "nn":
        (M, K), (K2, N) = a2, b2
    elif mode == "nt":
        (M, K), (N, K2) = a2, b2
    else:
        (K, M), (K2, N) = a2, b2
    assert K == K2, (name, a2, b2)
    tm, tn, tk = min(tm, M), min(tn, N), min(tk, K)
    assert M % tm == 0 and N % tn == 0 and K % tk == 0, (name, M, N, K, tm, tn, tk)
    gm, gn, gk = M // tm, N // tn, K // tk

    def specs(parts, block, idx, lead):
        per = parts[0].shape[-1] // block[1]
        assert parts[0].shape[-1] % block[1] == 0, (name, parts[0].shape, block)
        out = []
        for p in range(len(parts)):
            def f(i, j, k, p=p):
                r, c = idx(i, j, k)
                if len(parts) > 1:
                    c = jnp.clip(c - p * per, 0, per - 1)
                return (r, c) if lead is None else (lead, r, c)
            out.append(pl.BlockSpec(block if lead is None else (None,) + block, f))
        return out, per

    if mode == "nn":
        a_specs, a_per = specs(a_parts, (tm, tk), lambda i, j, k: (i, k), la)
        b_specs, b_per = specs(b_parts, (tk, tn), lambda i, j, k: (k, j), lb)
        a_axis, b_axis, dot = 2, 1, _dot
    elif mode == "nt":
        a_specs, a_per = specs(a_parts, (tm, tk), lambda i, j, k: (i, k), la)
        b_specs, b_per = specs(b_parts, (tn, tk), lambda i, j, k: (j, k), lb)
        a_axis, b_axis, dot = 2, 2, _dot_nt
    else:
        a_specs, a_per = specs(a_parts, (tk, tm), lambda i, j, k: (k, i), la)
        b_specs, b_per = specs(b_parts, (tk, tn), lambda i, j, k: (k, j), lb)
        a_axis, b_axis, dot = 0, 1, _dot_tn
    ex_specs = [pl.BlockSpec((tm, tn), lambda i, j, k: (i, j)) for _ in extras]
    n_ex = len(extras)

    n_aux, n_side = len(aux), len(sides)
    operands = a_parts + b_parts + list(extras) + list(aux)
    in_specs = a_specs + b_specs + ex_specs + [pl.BlockSpec(v.shape, lambda i, j, k, nd=v.ndim: (0,) * nd) for v in aux]
    aliases = {}
    if stack is None:
        out_specs = [pl.BlockSpec((tm, tn), lambda i, j, k: (i, j))]
        out_shapes = [jax.ShapeDtypeStruct((M, N), out_dtype)]
    else:
        n_layers, layer, prev = stack
        out_specs = [pl.BlockSpec((None, tm, tn), lambda i, j, k: (layer, i, j))]
        out_shapes = [jax.ShapeDtypeStruct((n_layers, M, N), out_dtype)]
        if prev is not None:
            aliases = {len(operands): 0}
            operands.append(prev)
            in_specs.append(_ANY)
    for kind, dtype in sides:
        if kind == "tile":
            out_specs.append(pl.BlockSpec((tm, tn), lambda i, j, k: (i, j)))
            out_shapes.append(jax.ShapeDtypeStruct((M, N), dtype))
        else:
            assert gn == 1, name
            out_specs.append(pl.BlockSpec((8, tn), lambda i, j, k: (0, 0)))
            out_shapes.append(jax.ShapeDtypeStruct((8, N), dtype))
    n_prev = len(aliases)
    scratch = [pltpu.VMEM((tm, tn), F32)] if gk > 1 else []
    n_rin = n_rout = 0
    if rider is not None:
        n_rin, n_rout = len(rider.operands), len(rider.out_shapes)
        for src, dst in rider.aliases:
            aliases[len(operands) + src] = 1 + n_side + dst
        operands += list(rider.operands)
        in_specs += [_ANY] * n_rin
        out_specs += [_ANY] * n_rout
        out_shapes += list(rider.out_shapes)
        scratch += [pltpu.SemaphoreType.DMA((rider.n_sems,)), pltpu.SemaphoreType.DMA((rider.n_sems,))]
    assert na == 1 or nbp == 1, name

    def body(*refs):
        a_refs, b_refs = refs[:na], refs[na:na + nbp]
        ex_refs = refs[na + nbp:na + nbp + n_ex + n_aux]
        n_in = na + nbp + n_ex + n_aux + n_prev
        rin = refs[n_in:n_in + n_rin]
        o_ref = refs[n_in + n_rin]
        side_refs = refs[n_in + n_rin + 1:n_in + n_rin + 1 + n_side]
        rout = refs[n_in + n_rin + 1 + n_side:n_in + n_rin + 1 + n_side + n_rout]
        rest = refs[n_in + n_rin + 1 + n_side + n_rout:]
        i, j, k = pl.program_id(0), pl.program_id(1), pl.program_id(2)
        if rider is not None:
            sems = rest[-2:]

            @pl.when(jnp.logical_and(i == 0, jnp.logical_and(j == 0, k == 0)))
            def _():
                rider.start(rin, rout, *sems)

        def finish(acc):
            if epi is None:
                o_ref[...] = acc[...].astype(o_ref.dtype)
                return
            strip = min(tm, EPI_ROWS)
            colsums = [None] * n_side
            for r0 in range(0, tm, strip):
                rows = slice(r0, r0 + strip)
                res = epi(acc[rows, :], *[r[rows, :] for r in ex_refs[:n_ex]], *[r[...] for r in ex_refs[n_ex:]])
                if n_side:
                    res, *side_vals = res
                    for s, ((kind, _), ref, val) in enumerate(zip(sides, side_refs, side_vals)):
                        if kind == "tile":
                            ref[rows, :] = val.astype(ref.dtype)
                        else:
                            colsums[s] = val if colsums[s] is None else colsums[s] + val
                o_ref[rows, :] = res.astype(o_ref.dtype)
            for (kind, _), ref, val in zip(sides, side_refs, colsums):
                if kind == "colsum":
                    @pl.when(i == 0)
                    def _(ref=ref, val=val):
                        ref[...] = val

                    @pl.when(i > 0)
                    def _(ref=ref, val=val):
                        ref[...] += val

                    @pl.when(i == gm - 1)
                    def _(ref=ref):
                        ref[0:1, :] = jnp.sum(ref[...], axis=0, keepdims=True)

        def step(a_ref, b_ref):
            av, bv = a_ref[...], b_ref[...]
            if a_fn is not None:
                av = a_fn(av)
            if b_fn is not None:
                bv = b_fn(bv)
            part = dot(av.astype(BF16), bv.astype(BF16))
            if gk == 1:
                finish(part)
                return
            acc_ref = rest[0]

            @pl.when(k == 0)
            def _():
                acc_ref[...] = part

            @pl.when(k > 0)
            def _():
                acc_ref[...] += part

        if na > 1:
            sel = pl.program_id(a_axis) // a_per
            for p in range(na):
                pl.when(sel == p)(functools.partial(step, a_refs[p], b_refs[0]))
        elif nbp > 1:
            sel = pl.program_id(b_axis) // b_per
            for p in range(nbp):
                pl.when(sel == p)(functools.partial(step, a_refs[0], b_refs[p]))
        else:
            step(a_refs[0], b_refs[0])
        if gk > 1:
            @pl.when(k == gk - 1)
            def _():
                finish(rest[0])

        if rider is not None:
            @pl.when(jnp.logical_and(i == gm - 1, jnp.logical_and(j == gn - 1, k == gk - 1)))
            def _():
                rider.finish(rin, rout, *sems)

    sequential = rider is not None or any(kind == "colsum" for kind, _ in sides)
    sem = ("arbitrary",) * 3 if sequential else ("parallel", "parallel", "arbitrary")
    outs = pl.pallas_call(
        body,
        name=name,
        grid=(gm, gn, gk),
        in_specs=in_specs,
        out_specs=out_specs,
        out_shape=out_shapes,
        input_output_aliases=aliases,
        scratch_shapes=scratch,
        compiler_params=_cp(sem),
    )(*operands)
    res = outs[0] if not sides else tuple(outs[:1 + n_side])
    return res if rider is None else (res, list(outs[1 + n_side:]))


def _relu2(u):
    r = jnp.maximum(u, 0)
    return r * r


def _epi_residual(acc, res):
    return acc + res


def _epi_residual_norm(acc, res, g):
    xn = acc + res
    r = lax.rsqrt(jnp.mean(xn * xn, axis=-1, keepdims=True) + RMS_EPS)
    return xn, (xn * r) * g


def _epi_rms_bwd(dh, x, dres, g):
    r = lax.rsqrt(jnp.mean(x * x, axis=-1, keepdims=True) + RMS_EPS)
    xh = x * r
    dxh = dh * g
    dx = dres + r * (dxh - xh * jnp.mean(dxh * xh, axis=-1, keepdims=True))
    return dx, jnp.sum((dh * xh).reshape(dh.shape[0] // 8, 8, dh.shape[1]), axis=0)


def _rms_fwd(name, x, g):
    S, D = x.shape
    tq = min(1024, S)

    def body(x_ref, g_ref, o_ref):
        xv = x_ref[...]
        r = lax.rsqrt(jnp.mean(xv * xv, axis=-1, keepdims=True) + RMS_EPS)
        o_ref[...] = ((xv * r) * g_ref[...]).astype(o_ref.dtype)

    return pl.pallas_call(
        body,
        name=name,
        grid=(S // tq,),
        in_specs=[pl.BlockSpec((tq, D), lambda i: (i, 0)), pl.BlockSpec((1, D), lambda i: (0, 0))],
        out_specs=pl.BlockSpec((tq, D), lambda i: (i, 0)),
        out_shape=jax.ShapeDtypeStruct((S, D), BF16),
        compiler_params=_cp(("parallel",)),
    )(x, g)


def _loss_head(x, g, t):
    S, D = x.shape
    tq = min(512, S)
    n = S // tq

    def body(x_ref, g_ref, t_ref, loss_ref, dx_ref, dg_ref, lacc_ref, gacc_ref):
        i = pl.program_id(0)
        xv = x_ref[...]
        gv = g_ref[...]
        r = lax.rsqrt(jnp.mean(xv * xv, axis=-1, keepdims=True) + RMS_EPS)
        xh = xv * r
        e = xh * gv - t_ref[...]
        dy = e * (1.0 / D)
        dxh = dy * gv
        dx_ref[...] = r * (dxh - xh * jnp.mean(dxh * xh, axis=-1, keepdims=True))
        lpart = jnp.sum((e * e).reshape(tq // 8, 8, D), axis=0)
        gpart = jnp.sum((dy * xh).reshape(tq // 8, 8, D), axis=0)

        @pl.when(i == 0)
        def _():
            lacc_ref[...] = lpart
            gacc_ref[...] = gpart

        @pl.when(i > 0)
        def _():
            lacc_ref[...] += lpart
            gacc_ref[...] += gpart

        @pl.when(i == n - 1)
        def _():
            dg_ref[...] = jnp.sum(gacc_ref[...], axis=0, keepdims=True)
            tot = jnp.sum(jnp.sum(lacc_ref[...], axis=0, keepdims=True), axis=1, keepdims=True)
            loss_ref[...] = jnp.broadcast_to(tot * (0.5 / D), (1, LANE))

    return pl.pallas_call(
        body,
        name="loss_head",
        grid=(n,),
        in_specs=[pl.BlockSpec((tq, D), lambda i: (i, 0)), pl.BlockSpec((1, D), lambda i: (0, 0)),
                  pl.BlockSpec((tq, D), lambda i: (i, 0))],
        out_specs=[pl.BlockSpec((1, LANE), lambda i: (0, 0)), pl.BlockSpec((tq, D), lambda i: (i, 0)),
                   pl.BlockSpec((1, D), lambda i: (0, 0))],
        out_shape=[jax.ShapeDtypeStruct((1, LANE), F32), jax.ShapeDtypeStruct((S, D), F32),
                   jax.ShapeDtypeStruct((1, D), F32)],
        scratch_shapes=[pltpu.VMEM((8, D), F32), pltpu.VMEM((8, D), F32)],
        compiler_params=_cp(("arbitrary",)),
    )(x, g, t)


def _ret_tables(S):
    T = min(RET_BLOCK, S)
    inv_freq = 1.0 / (RET_ROPE_BASE ** jnp.linspace(0.0, 1.0, RET_HEAD_DIM // 2, dtype=F32))
    ang = jnp.arange(S, dtype=F32)[:, None] * inv_freq[None, :]
    cos, sin = jnp.cos(ang), jnp.sin(ang)
    cosf = jnp.repeat(cos, 2, axis=-1)
    sins = jnp.stack([-sin, sin], axis=-1).reshape(S, RET_HEAD_DIM)
    log_g = np.log1p(-np.power(2.0, -5.0 - np.arange(RET_HEADS, dtype=np.float64)))
    pos = np.arange(T, dtype=np.float64)
    diff = pos[:, None] - pos[None, :]
    same = (pos[:, None] // CHUNK) == (pos[None, :] // CHUNK)
    seen = same | (diff > 0)
    dmat = np.where(seen[None], np.exp(np.abs(diff)[None] * log_g[:, None, None]), 0.0)
    aq = np.exp((pos[None, :] + 1.0) * log_g[:, None])
    ak = np.exp((T - 1.0 - pos[None, :]) * log_g[:, None])
    lam = np.exp(T * log_g)
    bc = lambda v: jnp.asarray(np.broadcast_to(v[..., None], v.shape + (LANE,)), F32)
    return dict(cos=cosf, sin=sins, dmat=jnp.asarray(dmat, F32), aq=bc(aq), ak=bc(ak),
                lam=jnp.asarray(np.broadcast_to(lam[:, None, None], (RET_HEADS, 1, LANE)), F32))


def _rot(x, cos, sin_s, even):
    sw = jnp.where(even, pltpu.roll(x, LANE - 1, 1), pltpu.roll(x, 1, 1))
    return x * cos + sw * sin_s


def _rot_t(dy, cos, sin_s, even):
    t = dy * sin_s
    return dy * cos + jnp.where(even, pltpu.roll(t, LANE - 1, 1), pltpu.roll(t, 1, 1))


def _ret_specs(T, rev_nb=None):
    blk = (lambda b: b) if rev_nb is None else (lambda b: rev_nb - 1 - b)
    whole = lambda shape: pl.BlockSpec(shape, lambda b: (0,) * len(shape))
    specs = [pl.BlockSpec((T, AB_IN_WIDTH), lambda b: (blk(b), 0)),
             pl.BlockSpec((T, LANE), lambda b: (blk(b), 0)),
             pl.BlockSpec((T, LANE), lambda b: (blk(b), 0)),
             whole((RET_HEADS, T, T)), whole((RET_HEADS, T, LANE)), whole((RET_HEADS, T, LANE)),
             whole((RET_HEADS, 1, LANE)), whole((1, RET_WIDTH))]
    return specs, blk


def _head_views(h, z_ref, tabs, token_refs, head_refs):
    zs = [z_ref.at[:, (o * RET_HEADS + h) * LANE:(o * RET_HEADS + h + 1) * LANE] for o in range(4)]
    hs = slice(h * LANE, (h + 1) * LANE)
    return zs, [t.at[h] for t in tabs], [r.at[:, hs] for r in token_refs], [r.at[h] for r in head_refs]


def _ret_fwd(name, z, tb, gain):
    S = z.shape[0]
    T = min(RET_BLOCK, S)
    nb = S // T
    specs, blk = _ret_specs(T)

    def body(z_ref, cos_r, sin_r, d_all, aq_all, ak_all, lam_all, gain_all, cat_all, opre_all, st_all, state_all):
        @pl.when(pl.program_id(0) == 0)
        def _():
            state_all[...] = jnp.zeros_like(state_all)

        for h in range(RET_HEADS):
            zs, tabs, toks, heads = _head_views(h, z_ref, (d_all, aq_all, ak_all, lam_all),
                                                (gain_all, cat_all, opre_all), (st_all, state_all))
            head(*zs, cos_r, sin_r, *tabs, *toks, *heads)

    def head(zq, zk, zv, zg, cos_r, sin_r, d_r, aq_r, ak_r, lam_r, gain_r, ret_o, opre_o, st_o, state):
        even = (lax.broadcasted_iota(jnp.int32, (T, LANE), 1) & 1) == 0
        c, s = cos_r[...], sin_r[...]
        q = _rot(zq[...], c, s, even)
        k = _rot(zk[...], c, s, even) * KSCALE
        qb, kb, vb = q.astype(BF16), k.astype(BF16), zv[...].astype(BF16)
        p = (_dot_nt(qb, kb) * d_r[...]).astype(BF16)
        st = state[...]
        st_o[...] = st
        o = _dot(p, vb) + _dot((q * aq_r[...]).astype(BF16), st.astype(BF16))
        state[...] = st * lam_r[...] + _dot_tn((k * ak_r[...]).astype(BF16), vb)
        opre_o[...] = o
        mu = jnp.mean(o, axis=-1, keepdims=True)
        d = o - mu
        y = d * lax.rsqrt(jnp.mean(d * d, axis=-1, keepdims=True) + GN_EPS)
        g = zg[...]
        ret_o[...] = ((g * jax.nn.sigmoid(g)) * (y * gain_r[...])).astype(ret_o.dtype)

    out_blk = pl.BlockSpec((T, RET_WIDTH), lambda b: (b, 0))
    return pl.pallas_call(
        body,
        name=name,
        grid=(nb,),
        in_specs=specs,
        out_specs=[out_blk, out_blk, pl.BlockSpec((RET_HEADS, None, LANE, LANE), lambda b: (0, b, 0, 0))],
        out_shape=[jax.ShapeDtypeStruct((S, D_MODEL), BF16), jax.ShapeDtypeStruct((S, RET_WIDTH), F32),
                   jax.ShapeDtypeStruct((RET_HEADS, nb, LANE, LANE), F32)],
        scratch_shapes=[pltpu.VMEM((RET_HEADS, LANE, LANE), F32)],
        compiler_params=_cp(("arbitrary",)),
    )(z, tb["cos"], tb["sin"], tb["dmat"], tb["aq"], tb["ak"], tb["lam"], gain)


def _ret_bwd(name, z, tb, gain, opre, states, dcat):
    S = z.shape[0]
    T = min(RET_BLOCK, S)
    nb = S // T
    specs, blk = _ret_specs(T, rev_nb=nb)
    tok = pl.BlockSpec((T, RET_WIDTH), lambda b: (blk(b), 0))

    def body(z_ref, cos_r, sin_r, d_all, aq_all, ak_all, lam_all, gain_all, opre_all, st_all, dret_all,
             dz_ref, dgain_all, dstate_all):
        @pl.when(pl.program_id(0) == 0)
        def _():
            dstate_all[...] = jnp.zeros_like(dstate_all)
            dgain_all[...] = jnp.zeros_like(dgain_all)

        for h in range(RET_HEADS):
            zs, tabs, toks, heads = _head_views(h, z_ref, (d_all, aq_all, ak_all, lam_all),
                                                (gain_all, opre_all, dret_all, dgain_all), (st_all, dstate_all))
            dzs, _, _, _ = _head_views(h, dz_ref, (), (), ())
            gain_r, opre_r, dret_r, dgain_o = toks
            head(*zs, cos_r, sin_r, *tabs, gain_r, opre_r, heads[0], dret_r, *dzs, dgain_o, heads[1])

    def head(zq, zk, zv, zg, cos_r, sin_r, d_r, aq_r, ak_r, lam_r, gain_r, opre_r, st_r, dret_r,
             dq_o, dk_o, dv_o, dg_o, dgain_o, dstate):
        even = (lax.broadcasted_iota(jnp.int32, (T, LANE), 1) & 1) == 0
        c, s = cos_r[...], sin_r[...]
        aq, ak, dm = aq_r[...], ak_r[...], d_r[...]
        q = _rot(zq[...], c, s, even)
        k = _rot(zk[...], c, s, even) * KSCALE
        qb, kb, vb = q.astype(BF16), k.astype(BF16), zv[...].astype(BF16)
        pb = (_dot_nt(qb, kb) * dm).astype(BF16)
        g = zg[...]
        sig = jax.nn.sigmoid(g)
        o = opre_r[...]
        mu = jnp.mean(o, axis=-1, keepdims=True)
        d = o - mu
        rstd = lax.rsqrt(jnp.mean(d * d, axis=-1, keepdims=True) + GN_EPS)
        y = d * rstd
        gain_v = gain_r[...]
        dret = dret_r[...].astype(F32)
        dyg = dret * (g * sig)
        dg_o[...] = (dret * (y * gain_v) * (sig * (1.0 + g * (1.0 - sig)))).astype(dg_o.dtype)
        dgain_o[...] += jnp.sum(dyg * y, axis=0, keepdims=True)
        dy = dyg * gain_v
        do = rstd * (dy - jnp.mean(dy, axis=-1, keepdims=True) - y * jnp.mean(dy * y, axis=-1, keepdims=True))
        dob = do.astype(BF16)
        stb = st_r[...].astype(BF16)
        dsn = dstate[...]
        dsnb = dsn.astype(BF16)
        dpb = (_dot_nt(dob, vb) * dm).astype(BF16)
        dq = _dot(dpb, kb) + _dot_nt(dob, stb) * aq
        dk = _dot_tn(dpb, qb) + _dot_nt(vb, dsnb) * ak
        dv = _dot_tn(pb, dob) + _dot((k * ak).astype(BF16), dsnb)
        dstate[...] = dsn * lam_r[...] + _dot_tn((q * aq).astype(BF16), dob)
        dq_o[...] = _rot_t(dq, c, s, even).astype(dq_o.dtype)
        dk_o[...] = _rot_t(dk * KSCALE, c, s, even).astype(dk_o.dtype)
        dv_o[...] = dv.astype(dv_o.dtype)

    return pl.pallas_call(
        body,
        name=name,
        grid=(nb,),
        in_specs=specs + [tok, pl.BlockSpec((RET_HEADS, None, LANE, LANE), lambda b: (0, blk(b), 0, 0)), tok],
        out_specs=[pl.BlockSpec((T, 4 * RET_WIDTH), lambda b: (blk(b), 0)), pl.BlockSpec((1, RET_WIDTH), lambda b: (0, 0))],
        out_shape=[jax.ShapeDtypeStruct((S, AB_IN_WIDTH), BF16), jax.ShapeDtypeStruct((1, RET_WIDTH), F32)],
        scratch_shapes=[pltpu.VMEM((RET_HEADS, LANE, LANE), F32)],
        compiler_params=_cp(("arbitrary",)),
    )(z, tb["cos"], tb["sin"], tb["dmat"], tb["aq"], tb["ak"], tb["lam"], gain, opre, states, dcat)


def _pool_counts(t0, rows):
    t = t0 + lax.broadcasted_iota(jnp.int32, (rows, POOL_WIDTH), 0)
    grp = lax.broadcasted_iota(jnp.int32, (rows, POOL_WIDTH), 1) >> 7
    win = jnp.where(grp == 0, POOL_WINDOWS[0], jnp.where(grp == 1, POOL_WINDOWS[1],
                    jnp.where(grp == 2, POOL_WINDOWS[2], POOL_WINDOWS[3])))
    return jnp.maximum(jnp.minimum(t + 1, win), 1).astype(F32), grp


def _window_sums(ext, grp, sign):
    n = ext.shape[0]
    sh = lambda v, k: pltpu.roll(v, k % n if sign > 0 else (n - k) % n, 0)
    s2 = ext + sh(ext, 1)
    s4 = s2 + sh(s2, 2)
    s8 = s4 + sh(s4, 4)
    s16 = s8 + sh(s8, 8)
    return jnp.where(grp == 0, s2, jnp.where(grp == 1, s4, jnp.where(grp == 2, s8, s16)))


def _pool_fwd(name, z, w_pool, scale, cat):
    S = z.shape[0]
    T = min(512, S)
    nb = S // T
    pcol = AB_IN_WIDTH // POOL_WIDTH - 1
    hb = T // POOL_HALO

    def body(p_ref, halo_ref, w_ref, sc_ref, cat_in, out_ref, pooled_ref):
        b = pl.program_id(0)
        cur = p_ref[...]
        halo = jnp.where(b > 0, halo_ref[...], 0.0)
        ext = jnp.concatenate([halo, cur], axis=0)
        cnt, grp = _pool_counts(b * T - POOL_HALO, T + POOL_HALO)
        sums = _window_sums(ext, grp, +1)
        pooled = (sums / cnt)[POOL_HALO:] - cur
        pb = pooled.astype(BF16)
        pooled_ref[...] = pb
        for gi in range(len(POOL_WINDOWS)):
            cs = slice(gi * POOL_GROUP_DIM, (gi + 1) * POOL_GROUP_DIM)
            mixed = _dot(pb[:, cs], w_ref[gi].astype(BF16))
            out_ref[:, cs] = (mixed * sc_ref[:, cs]).astype(out_ref.dtype)

    return pl.pallas_call(
        body,
        name=name,
        grid=(nb,),
        in_specs=[pl.BlockSpec((T, POOL_WIDTH), lambda b: (b, pcol)),
                  pl.BlockSpec((POOL_HALO, POOL_WIDTH), lambda b: (jnp.maximum(b * hb - 1, 0), pcol)),
                  pl.BlockSpec((4, POOL_GROUP_DIM, POOL_GROUP_DIM), lambda b: (0, 0, 0)),
                  pl.BlockSpec((1, POOL_WIDTH), lambda b: (0, 0)), _ANY],
        out_specs=[pl.BlockSpec((T, POOL_WIDTH), lambda b: (b, 1)), pl.BlockSpec((T, POOL_WIDTH), lambda b: (b, 0))],
        out_shape=[jax.ShapeDtypeStruct(cat.shape, cat.dtype), jax.ShapeDtypeStruct((S, POOL_WIDTH), BF16)],
        input_output_aliases={4: 0},
        compiler_params=_cp(("parallel",)),
    )(z, z, w_pool, scale, cat)


def _pool_bwd(name, pooled, w_pool, scale, dcat, dz):
    S = pooled.shape[0]
    T = min(512, S)
    nb = S // T
    hb = T // POOL_HALO
    last_h = S // POOL_HALO - 1
    pcol = AB_IN_WIDTH // POOL_WIDTH - 1

    def body(d_ref, dn_ref, pooled_ref, w_ref, sc_ref, dz_in, dp_ref, dw_ref, dsc_ref):
        b = pl.program_id(0)

        @pl.when(b == 0)
        def _():
            dw_ref[...] = jnp.zeros_like(dw_ref)
            dsc_ref[...] = jnp.zeros_like(dsc_ref)

        sc = sc_ref[...]
        dout = d_ref[...].astype(F32)
        dnext = jnp.where(b < nb - 1, dn_ref[...].astype(F32), 0.0)
        dmix = jnp.concatenate([dout, dnext], axis=0) * sc
        dmb = dmix.astype(BF16)
        pb = pooled_ref[...]
        dpooled = []
        for gi in range(len(POOL_WINDOWS)):
            cs = slice(gi * POOL_GROUP_DIM, (gi + 1) * POOL_GROUP_DIM)
            wb = w_ref[gi].astype(BF16)
            dpooled.append(_dot_nt(dmb[:, cs], wb))
            dw_ref[gi] += _dot_tn(pb[:, cs], dmb[:T, cs])
            mixed = _dot(pb[:, cs], wb)
            dsc_ref[:, cs] += jnp.sum(dout[:, cs] * mixed, axis=0, keepdims=True)
        dpl = jnp.concatenate(dpooled, axis=1)
        cnt, grp = _pool_counts(b * T, T + POOL_HALO)
        sums = _window_sums(dpl / cnt, grp, -1)
        dp_ref[...] = (sums[:T] - dpl[:T]).astype(dp_ref.dtype)

    return pl.pallas_call(
        body,
        name=name,
        grid=(nb,),
        in_specs=[pl.BlockSpec((T, POOL_WIDTH), lambda b: (b, 1)),
                  pl.BlockSpec((POOL_HALO, POOL_WIDTH), lambda b: (jnp.minimum((b + 1) * hb, last_h), 1)),
                  pl.BlockSpec((T, POOL_WIDTH), lambda b: (b, 0)),
                  pl.BlockSpec((4, POOL_GROUP_DIM, POOL_GROUP_DIM), lambda b: (0, 0, 0)),
                  pl.BlockSpec((1, POOL_WIDTH), lambda b: (0, 0)), _ANY],
        out_specs=[pl.BlockSpec((T, POOL_WIDTH), lambda b: (b, pcol)),
                   pl.BlockSpec((4, POOL_GROUP_DIM, POOL_GROUP_DIM), lambda b: (0, 0, 0)),
                   pl.BlockSpec((1, POOL_WIDTH), lambda b: (0, 0))],
        out_shape=[jax.ShapeDtypeStruct(dz.shape, dz.dtype),
                   jax.ShapeDtypeStruct((4, POOL_GROUP_DIM, POOL_GROUP_DIM), F32),
                   jax.ShapeDtypeStruct((1, POOL_WIDTH), F32)],
        input_output_aliases={5: 0},
        compiler_params=_cp(("arbitrary",)),
    )(dcat, dcat, pooled, w_pool, scale, dz)


ATT_STRIP = 32
ATT_Q = 256
ATT_W = ATT_Q + LEFT_CHUNKS * CHUNK


def _rel_index():
    j = np.arange(ATT_W)
    rel = np.clip(LEFT_CHUNKS * CHUNK - j, -REL_CLIP, REL_CLIP) + REL_CLIP
    fwd = np.where(j < BAND, rel, N_REL)
    bwd = np.where(j <= ATT_W - CHUNK, fwd, 2 * REL_CLIP)
    return tuple(jnp.asarray(v.reshape(1, ATT_W), jnp.int32) for v in (fwd, bwd))


def _bias_table(name, rel_bias, rel_idx):
    rb = jnp.concatenate([rel_bias, jnp.full((ATT_HEADS, 1), NEG_INF, F32),
                          jnp.zeros((ATT_HEADS, N_REL_PAD - N_REL - 1), F32)], axis=1)

    def body(rb_ref, idx_ref, o_ref, row0_ref):
        r = lax.broadcasted_iota(jnp.int32, (N_REL_PAD, ATT_W), 0)
        onehot = (r == idx_ref[...]).astype(F32)
        row0_ref[...] = jnp.dot(rb_ref[...], onehot, precision=lax.Precision.HIGHEST, preferred_element_type=F32)
        col = lax.broadcasted_iota(jnp.int32, (CHUNK, ATT_W), 1)
        row = lax.broadcasted_iota(jnp.int32, (CHUNK, ATT_W), 0)
        for h in range(ATT_HEADS):
            same = jnp.broadcast_to(row0_ref[pl.ds(h, 1), :], (CHUNK, ATT_W))
            turned = pltpu.roll(same, 0, 1, stride=1, stride_axis=0)
            o_ref[h] = jnp.where(col >= BAND, NEG_INF, jnp.where(col < row, same, turned))

    return pl.pallas_call(
        body,
        name=name,
        out_shape=jax.ShapeDtypeStruct((ATT_HEADS, CHUNK, ATT_W), F32),
        scratch_shapes=[pltpu.VMEM((ATT_HEADS, ATT_W), F32)],
        compiler_params=pltpu.CompilerParams(vmem_limit_bytes=VMEM_LIMIT),
    )(rb, rel_idx)


def _bias_grad(name, dband, rel_idx):
    def body(d_ref, idx_ref, o_ref, sums_ref):
        row = lax.broadcasted_iota(jnp.int32, (CHUNK, ATT_W), 0)
        for h in range(ATT_HEADS):
            back = d_ref[h]
            for bit in range(CHUNK.bit_length() - 1):
                back = jnp.where(((row >> bit) & 1) == 1, pltpu.roll(back, ATT_W - (1 << bit), 1), back)
            sums_ref[pl.ds(h, 1), :] = jnp.sum(back, axis=0, keepdims=True)
        r = lax.broadcasted_iota(jnp.int32, (N_REL_PAD, ATT_W), 0)
        onehot = (r == idx_ref[...]).astype(F32)
        o_ref[...] = lax.dot_general(sums_ref[...], onehot, (((1,), (1,)), ((), ())),
                                     precision=lax.Precision.HIGHEST, preferred_element_type=F32)

    out = pl.pallas_call(
        body,
        name=name,
        out_shape=jax.ShapeDtypeStruct((ATT_HEADS, N_REL_PAD), F32),
        scratch_shapes=[pltpu.VMEM((ATT_HEADS, ATT_W), F32)],
        compiler_params=pltpu.CompilerParams(vmem_limit_bytes=VMEM_LIMIT),
    )(dband, rel_idx)
    return out[:, :N_REL]


def _attn_unit(q_ref, kw_ref, bias_ref, e, u, lane):
    mine = (lane < ATT_HEAD_DIM) if e == 0 else (lane >= ATT_HEAD_DIM)
    qm = jnp.where(mine, q_ref[u * ATT_Q:(u + 1) * ATT_Q, :] * QSCALE, 0)
    kw = kw_ref[u * ATT_Q:u * ATT_Q + ATT_W, :]
    s = _dot_nt(qm, kw) + bias_ref[u, e]
    p = jnp.exp(s - jnp.max(s, axis=-1, keepdims=True))
    return p, 1.0 / jnp.sum(p, axis=-1, keepdims=True), qm, kw, mine


def _attn_in_specs(nb):
    T = ATT_BLOCK
    hp = ATT_HEADS // 2
    cur = lambda off: pl.BlockSpec((T, LANE), lambda h, b: (jnp.minimum(b, nb - 1), off + h))
    prev = lambda off: pl.BlockSpec((T, LANE), lambda h, b: (jnp.clip(b - 1, 0, nb - 1), off + h))
    return [cur(0), prev(hp), cur(hp), prev(2 * hp), cur(2 * hp),
            pl.BlockSpec((None, 2, CHUNK, ATT_W), lambda h, b: (h, 0, 0, 0))]


def _spread_bias(bias_ref, bm_ref, block):
    col = lax.broadcasted_iota(jnp.int32, (CHUNK, ATT_W), 1)
    for first in (True, False):
        @pl.when(block == (0 if first else 1))
        def _(first=first):
            for u in range(ATT_BLOCK // ATT_Q):
                for e in range(2):
                    for j in range(ATT_Q // CHUNK):
                        rows = pltpu.roll(bias_ref[e], j * CHUNK, 1)
                        if first:
                            rows = jnp.where(col >= ATT_BLOCK - u * ATT_Q, rows, NEG_INF)
                        bm_ref[u, e, j * CHUNK:(j + 1) * CHUNK, :] = rows


def _attn_fwd(name, qkv, bias):
    S = qkv.shape[0]
    T = ATT_BLOCK
    nb = S // T

    def body(q_ref, kp_ref, kc_ref, vp_ref, vc_ref, band_ref, o_ref, kw_ref, vw_ref, bias_ref, s_ref, p_ref, inv_ref):
        _spread_bias(band_ref, bias_ref, pl.program_id(1))
        kw_ref[0:T] = kp_ref[...]
        kw_ref[T:2 * T] = kc_ref[...]
        vw_ref[0:T] = vp_ref[...]
        vw_ref[T:2 * T] = vc_ref[...]
        lane = lax.broadcasted_iota(jnp.int32, (ATT_Q, LANE), 1)
        for u in range(T // ATT_Q):
            vw = vw_ref[u * ATT_Q:u * ATT_Q + ATT_W, :]
            kw = kw_ref[u * ATT_Q:u * ATT_Q + ATT_W, :]
            outs = []
            for e in range(2):
                mine = (lane < ATT_HEAD_DIM) if e == 0 else (lane >= ATT_HEAD_DIM)
                qm = jnp.where(mine, q_ref[u * ATT_Q:(u + 1) * ATT_Q, :] * QSCALE, 0)
                s_ref[e] = _dot_nt(qm, kw)
                for r in range(ATT_Q // ATT_STRIP):
                    rows = slice(r * ATT_STRIP, (r + 1) * ATT_STRIP)
                    s = s_ref[e, rows, :] + bias_ref[u, e, rows, :]
                    p = jnp.exp(s - jnp.max(s, axis=-1, keepdims=True))
                    inv_ref[e, rows, :] = jnp.broadcast_to(1.0 / jnp.sum(p, axis=-1, keepdims=True), (ATT_STRIP, LANE))
                    p_ref[e, rows, :] = p.astype(BF16)
                outs.append(_dot(p_ref[e], vw) * inv_ref[e])
            o_ref[u * ATT_Q:(u + 1) * ATT_Q, :] = jnp.where(lane < ATT_HEAD_DIM, outs[0], outs[1]).astype(o_ref.dtype)

    return pl.pallas_call(
        body,
        name=name,
        grid=(ATT_HEADS // 2, nb),
        in_specs=_attn_in_specs(nb),
        out_specs=pl.BlockSpec((T, LANE), lambda h, b: (b, h)),
        out_shape=jax.ShapeDtypeStruct((S, D_MODEL), BF16),
        scratch_shapes=[pltpu.VMEM((2 * T, LANE), BF16), pltpu.VMEM((2 * T, LANE), BF16),
                        pltpu.VMEM((T // ATT_Q, 2, ATT_Q, ATT_W), F32), pltpu.VMEM((2, ATT_Q, ATT_W), F32),
                        pltpu.VMEM((2, ATT_Q, ATT_W), BF16), pltpu.VMEM((2, ATT_Q, LANE), F32)],
        compiler_params=_cp(("parallel", "arbitrary")),
    )(qkv, qkv, qkv, qkv, qkv, bias)


def _attn_bwd(name, qkv, bias, do):
    S = qkv.shape[0]
    T = ATT_BLOCK
    nb = S // T

    def body(q_ref, kp_ref, kc_ref, vp_ref, vc_ref, band_ref, do_ref,
             dq_ref, dk_ref, dv_ref, dband_ref, kw_ref, vw_ref, dkw_ref, dvw_ref, bias_ref, dbias_ref):
        b = pl.program_id(1)

        _spread_bias(band_ref, bias_ref, b)

        @pl.when(b == 0)
        def _():
            dbias_ref[...] = jnp.zeros_like(dbias_ref)
            dkw_ref[T:2 * T] = jnp.zeros((T, LANE), F32)
            dvw_ref[T:2 * T] = jnp.zeros((T, LANE), F32)

        dkw_ref[0:T] = dkw_ref[T:2 * T]
        dvw_ref[0:T] = dvw_ref[T:2 * T]
        dkw_ref[T:2 * T] = jnp.zeros((T, LANE), F32)
        dvw_ref[T:2 * T] = jnp.zeros((T, LANE), F32)

        @pl.when(b < nb)
        def _():
            kw_ref[0:T] = kp_ref[...]
            kw_ref[T:2 * T] = kc_ref[...]
            vw_ref[0:T] = vp_ref[...]
            vw_ref[T:2 * T] = vc_ref[...]
            lane = lax.broadcasted_iota(jnp.int32, (ATT_Q, LANE), 1)
            for u in range(T // ATT_Q):
                rows = slice(u * ATT_Q, (u + 1) * ATT_Q)
                win = slice(u * ATT_Q, u * ATT_Q + ATT_W)
                vw = vw_ref[win, :]
                do2 = do_ref[rows, :]
                dqs, dk, dv = [], None, None
                for e in range(2):
                    p, inv, qm, kw, mine = _attn_unit(q_ref, kw_ref, bias_ref, e, u, lane)
                    dom = jnp.where(mine, do2, 0)
                    dp = _dot_nt(dom, vw)
                    delta = jnp.sum(p * dp, axis=-1, keepdims=True) * inv
                    ds = p * ((dp - delta) * inv)
                    dbias_ref[e] += ds
                    dsb = ds.astype(BF16)
                    dqs.append(_dot(dsb, kw))
                    dk_e = _dot_tn(dsb, qm)
                    dv_e = _dot_tn((p * inv).astype(BF16), dom)
                    dk = dk_e if dk is None else dk + dk_e
                    dv = dv_e if dv is None else dv + dv_e
                dq_ref[rows, :] = (jnp.where(lane < ATT_HEAD_DIM, dqs[0], dqs[1]) * QSCALE).astype(dq_ref.dtype)
                dkw_ref[win, :] += dk
                dvw_ref[win, :] += dv

        @pl.when(b > 0)
        def _():
            dk_ref[...] = dkw_ref[0:T].astype(dk_ref.dtype)
            dv_ref[...] = dvw_ref[0:T].astype(dv_ref.dtype)

        @pl.when(b == nb)
        def _():
            for e in range(2):
                acc = dbias_ref[e, 0:CHUNK, :]
                for j in range(1, ATT_Q // CHUNK):
                    acc = acc + pltpu.roll(dbias_ref[e, j * CHUNK:(j + 1) * CHUNK, :], ATT_W - j * CHUNK, 1)
                dband_ref[e] = acc

    tok = jax.ShapeDtypeStruct((S, D_MODEL), BF16)
    prev_out = pl.BlockSpec((T, LANE), lambda h, b: (jnp.maximum(b - 1, 0), h))
    return pl.pallas_call(
        body,
        name=name,
        grid=(ATT_HEADS // 2, nb + 1),
        in_specs=_attn_in_specs(nb) + [pl.BlockSpec((T, LANE), lambda h, b: (jnp.minimum(b, nb - 1), h))],
        out_specs=[pl.BlockSpec((T, LANE), lambda h, b: (jnp.minimum(b, nb - 1), h)), prev_out, prev_out,
                   pl.BlockSpec((None, 2, CHUNK, ATT_W), lambda h, b: (h, 0, 0, 0))],
        out_shape=[tok, tok, tok, jax.ShapeDtypeStruct((ATT_HEADS // 2, 2, CHUNK, ATT_W), F32)],
        scratch_shapes=[pltpu.VMEM((2 * T, LANE), BF16), pltpu.VMEM((2 * T, LANE), BF16),
                        pltpu.VMEM((2 * T, LANE), F32), pltpu.VMEM((2 * T, LANE), F32),
                        pltpu.VMEM((T // ATT_Q, 2, ATT_Q, ATT_W), F32), pltpu.VMEM((2, ATT_Q, ATT_W), F32)],
        compiler_params=_cp(("parallel", "arbitrary")),
    )(qkv, qkv, qkv, qkv, qkv, bias, do)


def _local_step(x, target, small, W):
    S = x.shape[0]
    tb = _ret_tables(S)
    rel_fwd, rel_bwd = _rel_index()
    saved = []
    normed = (("tile", BF16),)
    h = _rms_fwd("mix_norm_fwd0", x, small["mix_norm"][0:1])
    for layer in range(DEPTH):
        i = layer // 2
        st = {"x_in": x, "h": h}
        g_ffn = small["ffn_norm"][layer:layer + 1]
        if layer % 2 == 0:
            z = W.mm(f"ab_in_fwd{layer}", "nn", h, W.get("ab_w_in", i), tn=640, out_dtype=F32)
            gain = small["ab_gn_gain"][i:i + 1]
            cat, opre, states = _ret_fwd(f"ret_fwd{layer}", z, tb, gain)
            cat, pooled = _pool_fwd(f"pool_fwd{layer}", z, small["ab_w_pool"][i], small["ab_pool_scale"][i:i + 1], cat)
            st.update(z=z, opre=opre, states=states, pooled=pooled, cat=cat)
            x, hn = W.mm(f"ab_out_fwd{layer}", "nn", cat, W.get("ab_w_out", i), extras=(x,), aux=(g_ffn,), sides=normed,
                         epi=_epi_residual_norm)
        else:
            qkv = W.mm(f"qkv_fwd{layer}", "nn", h, W.get("c_w_qkv", i), out_dtype=BF16)
            bias = _bias_table(f"bias_table{layer}", small["c_rel_bias"][i], rel_fwd)
            bias = bias.reshape(ATT_HEADS // 2, 2, CHUNK, ATT_W)
            att = _attn_fwd(f"attn_fwd{layer}", qkv, bias)
            st.update(qkv=qkv, bias=bias, att=att)
            x, hn = W.mm(f"c_out_fwd{layer}", "nn", att, W.get("c_w_out", i), extras=(x,), aux=(g_ffn,), sides=normed,
                         epi=_epi_residual_norm)
        st["x_mid"] = x
        u = W.mm(f"ffn_in_fwd{layer}", "nn", hn, W.get("w_ffn_in", layer), out_dtype=BF16)
        if layer + 1 < DEPTH:
            x, h = W.mm(f"ffn_out_fwd{layer}", "nn", u, W.get("w_ffn_out", layer), a_fn=_relu2, extras=(x,),
                        aux=(small["mix_norm"][layer + 1:layer + 2],), sides=normed, epi=_epi_residual_norm)
        else:
            x = W.mm(f"ffn_out_fwd{layer}", "nn", u, W.get("w_ffn_out", layer), a_fn=_relu2, extras=(x,),
                     epi=_epi_residual)
        st.update(hn=hn, u=u)
        saved.append(st)

    loss, dx, d_final = _loss_head(x, small["final_norm"].reshape(1, D_MODEL), target)

    gs = {k: [None] * v.shape[0] for k, v in small.items() if k != "final_norm"}
    gb = {k: None for k in W.n_layers}
    landed = {k: None for k in W.n_layers}
    pending = []

    def host(name, *args, take=1, **kw):
        items = [pending.pop(0) for _ in range(min(take, len(pending)))]
        if not items:
            return _mm(name, *args, **kw)
        riders = [_grad_rider(key, idx, gb[key], landed[key]) for key, idx in items]
        res, outs = _mm(name, *args, rider=_join_riders(riders), **kw)
        for (key, _), out in zip(items, outs):
            landed[key] = out
        return res

    def dw(name, key, idx, a, b, call=_mm, **kw):
        gb[key] = call(name, "tn", a, b, stack=(W.n_layers[key], idx, gb[key]), out_dtype=BF16, **kw)
        pending.append((key, idx))

    gain_sums = (("colsum", F32),)
    for layer in reversed(range(DEPTH)):
        i = layer // 2
        st = saved[layer]
        du = host(f"ffn_out_bwd{layer}", "nt", dx, W.get("w_ffn_out", layer), extras=(st["u"],),
                  epi=lambda acc, u: acc * (2.0 * jnp.maximum(u, 0).astype(F32)), out_dtype=BF16)
        dw(f"ffn_out_dw{layer}", "w_ffn_out", layer, st["u"], dx, a_fn=_relu2)
        dx, dgain = host(f"ffn_in_bwd{layer}", "nt", du, W.get("w_ffn_in", layer), extras=(st["x_mid"], dx),
                         aux=(small["ffn_norm"][layer:layer + 1],), sides=gain_sums, epi=_epi_rms_bwd)
        gs["ffn_norm"][layer] = dgain[0:1]
        dw(f"ffn_in_dw{layer}", "w_ffn_in", layer, st["hn"], du)
        norm_bwd = dict(extras=(st["x_in"], dx), aux=(small["mix_norm"][layer:layer + 1],), sides=gain_sums,
                        epi=_epi_rms_bwd)
        if layer % 2 == 0:
            dcat = _mm(f"ab_out_bwd{layer}", "nt", dx, W.get("ab_w_out", i), out_dtype=BF16)
            dw(f"ab_out_dw{layer}", "ab_w_out", i, st["cat"], dx)
            gain = small["ab_gn_gain"][i:i + 1]
            dz, gs["ab_gn_gain"][i] = _ret_bwd(f"ret_bwd{layer}", st["z"], tb, gain, st["opre"], st["states"], dcat)
            dz, gs["ab_w_pool"][i], gs["ab_pool_scale"][i] = _pool_bwd(
                f"pool_bwd{layer}", st["pooled"], small["ab_w_pool"][i], small["ab_pool_scale"][i:i + 1], dcat, dz)
            if layer == 0:
                dw(f"ab_in_dw{layer}", "ab_w_in", i, st["h"], dz, call=host, tn=640)
            dx, dgain = host(f"ab_in_bwd{layer}", "nt", dz, W.get("ab_w_in", i), tm=512, tk=1280,
                             take=len(pending) if layer == 0 else 1, **norm_bwd)
            if layer > 0:
                dw(f"ab_in_dw{layer}", "ab_w_in", i, st["h"], dz, call=host, tn=640)
        else:
            datt = _mm(f"c_out_bwd{layer}", "nt", dx, W.get("c_w_out", i), out_dtype=BF16)
            dw(f"c_out_dw{layer}", "c_w_out", i, st["att"], dx)
            dq, dk, dv, dbias = _attn_bwd(f"attn_bwd{layer}", st["qkv"], st["bias"], datt)
            gs["c_rel_bias"][i] = _bias_grad(f"bias_grad{layer}", dbias.reshape(ATT_HEADS, CHUNK, ATT_W), rel_bwd)
            dqkv = [dq, dk, dv]
            dx, dgain = host(f"qkv_bwd{layer}", "nt", dqkv, W.get("c_w_qkv", i), tm=512, **norm_bwd)
            dw(f"qkv_dw{layer}", "c_w_qkv", i, st["h"], dqkv, call=host)
        gs["mix_norm"][layer] = dgain[0:1]
    for key, idx in pending:
        landed[key], = _run_rider(f"grad_exchange_{key}{idx}", _grad_rider(key, idx, gb[key], landed[key]))

    g_small = {
        "mix_norm": jnp.concatenate(gs["mix_norm"], axis=0),
        "ffn_norm": jnp.concatenate(gs["ffn_norm"], axis=0),
        "ab_gn_gain": jnp.concatenate(gs["ab_gn_gain"], axis=0),
        "ab_w_pool": jnp.stack(gs["ab_w_pool"], axis=0),
        "ab_pool_scale": jnp.concatenate(gs["ab_pool_scale"], axis=0),
        "c_rel_bias": jnp.stack(gs["c_rel_bias"], axis=0),
        "final_norm": d_final.reshape(D_MODEL),
    }
    return loss, dx, g_small, gb, landed


_BIG = ("w_ffn_in", "w_ffn_out", "ab_w_in", "ab_w_out", "c_w_qkv", "c_w_out")
_SHARD_AXIS = {"w_ffn_in": 2, "w_ffn_out": 1, "ab_w_in": 2, "ab_w_out": 1, "c_w_qkv": 2, "c_w_out": 1}
_SMALL = ("mix_norm", "ffn_norm", "ab_gn_gain", "ab_w_pool", "ab_pool_scale", "c_rel_bias", "final_norm")


def _place():
    x, y, c = lax.axis_index("x"), lax.axis_index("y"), lax.axis_index("c")
    chips = [(1 - x, y), (x, 1 - y), (1 - x, 1 - y)]
    return x, y, c, chips


def _sub(ref, axis, start, size):
    idx = [slice(None)] * len(ref.shape)
    idx[axis] = pl.ds(pl.multiple_of(start, LANE), size)
    return ref.at[tuple(idx)]


def _gather_rider(items, shards):
    keys = sorted({k for k, _ in items})
    n = len(items)
    axes = [_SHARD_AXIS[k] - 1 for k, _ in items]
    sizes = [shards[k].shape[a + 1] for (k, _), a in zip(items, axes)]
    hsizes = [shards[k].shape[2 - a] // 2 for (k, _), a in zip(items, axes)]

    def views(ins, outs, send_sems, recv_sems):
        x, y, c, chips = _place()
        srcs = [ins[keys.index(k)].at[l] for k, l in items]

        def remote(src, dst, s, to):
            return pltpu.make_async_remote_copy(src_ref=src, dst_ref=dst, send_sem=send_sems.at[s],
                                                recv_sem=recv_sems.at[s], device_id=to, device_id_type=MESH)

        def half(w, chip, core):
            return _sub(_sub(outs[w], axes[w], chip * sizes[w], sizes[w]), 1 - axes[w], core * hsizes[w], hsizes[w])

        me = 2 * x + y
        local = [pltpu.make_async_copy(srcs[w], _sub(outs[w], axes[w], me * sizes[w], sizes[w]), send_sems.at[6 * n + w])
                 for w in range(n)]
        first = [remote(_sub(srcs[w], 1 - axes[w], c * hsizes[w], hsizes[w]), half(w, me, c), w * 6 + k, (px, py, c))
                 for w in range(n) for k, (px, py) in enumerate(chips)]
        return x, y, c, chips, remote, half, local, first

    def start(ins, outs, send_sems, recv_sems):
        *_, local, first = views(ins, outs, send_sems, recv_sems)
        for cp in local + first:
            cp.start()

    def finish(ins, outs, send_sems, recv_sems):
        x, y, c, chips, remote, half, local, first = views(ins, outs, send_sems, recv_sems)
        sibling = (x, y, 1 - c)
        passed = []
        for w in range(n):
            for k, (px, py) in enumerate(chips):
                landed = half(w, 2 * px + py, c)
                remote(landed, landed, w * 6 + k, (px, py, c)).wait_recv()
                cp = remote(landed, landed, w * 6 + 3 + k, sibling)
                cp.start()
                passed.append(cp)
        for w in range(n):
            for k, (px, py) in enumerate(chips):
                theirs = half(w, 2 * px + py, 1 - c)
                remote(theirs, theirs, w * 6 + 3 + k, sibling).wait_recv()
        for cp in first + passed:
            cp.wait_send()
        for cp in local:
            cp.wait()

    def full(k, a):
        shape = list(shards[k].shape[1:])
        shape[a] *= N_CHIPS
        return jax.ShapeDtypeStruct(tuple(shape), shards[k].dtype)

    return _Rider(tuple(shards[k] for k in keys), tuple(full(k, a) for (k, _), a in zip(items, axes)), 7 * n, start, finish)


def _mixer_items(layer):
    names = ("ab_w_in", "ab_w_out") if layer % 2 == 0 else ("c_w_qkv", "c_w_out")
    return [(k, layer // 2) for k in names]


class _Weights:
    def __init__(self, shards):
        self.shards = shards
        self.n_layers = {k: shards[k].shape[0] for k in _BIG}
        self.full = {}
        first, second = _mixer_items(0)
        self._take([first], _run_rider("gather_first", _gather_rider([first], shards)))
        self.plan = {"ab_in_fwd0": [second, ("w_ffn_in", 0)], "ab_out_fwd0": [("w_ffn_out", 0)]}
        for layer in range(1, DEPTH):
            proj = "ab_in" if layer % 2 == 0 else "qkv"
            self.plan[f"ffn_in_fwd{layer - 1}"] = _mixer_items(layer)
            self.plan[f"ffn_out_fwd{layer - 1}"] = [("w_ffn_in", layer)]
            self.plan[f"{proj}_fwd{layer}"] = [("w_ffn_out", layer)]

    def _take(self, items, outs):
        self.full.update(zip(items, outs))

    def get(self, name, layer):
        return self.full[(name, layer)]

    def mm(self, name, *args, **kw):
        items = self.plan.get(name)
        if items is None:
            return _mm(name, *args, **kw)
        res, outs = _mm(name, *args, rider=_gather_rider(items, self.shards), **kw)
        self._take(items, outs)
        return res


def _run_rider(name, rider):
    n_in, n_out = len(rider.operands), len(rider.out_shapes)

    def body(*refs):
        ins, outs, sems = refs[:n_in], refs[n_in:n_in + n_out], refs[n_in + n_out:]
        rider.start(ins, outs, *sems)
        rider.finish(ins, outs, *sems)

    return pl.pallas_call(
        body,
        name=name,
        in_specs=[_ANY] * n_in,
        out_specs=[_ANY] * n_out,
        out_shape=list(rider.out_shapes),
        input_output_aliases=dict(rider.aliases),
        scratch_shapes=[pltpu.SemaphoreType.DMA((rider.n_sems,)), pltpu.SemaphoreType.DMA((rider.n_sems,))],
        compiler_params=pltpu.CompilerParams(has_side_effects=True),
    )(*rider.operands)


def _grad_rider(name, layer, grad, landing):
    axis = _SHARD_AXIS[name] - 1
    L, R, C = grad.shape
    shard = (R // N_CHIPS, C) if axis == 0 else (R, C // N_CHIPS)
    size = shard[axis]

    def copies(ins, outs, send_sems, recv_sems):
        x, y, c, chips = _place()
        return [pltpu.make_async_remote_copy(
            src_ref=_sub(ins[0].at[layer], axis, (2 * px + py) * size, size), dst_ref=outs[0].at[layer, k],
            send_sem=send_sems.at[k], recv_sem=recv_sems.at[k], device_id=(px, py, c), device_id_type=MESH)
            for k, (px, py) in enumerate(chips)]

    def start(ins, outs, send_sems, recv_sems):
        for cp in copies(ins, outs, send_sems, recv_sems):
            cp.start()

    def finish(ins, outs, send_sems, recv_sems):
        cps = copies(ins, outs, send_sems, recv_sems)
        for cp in cps:
            cp.wait_recv()
        for cp in cps:
            cp.wait_send()

    out = jax.ShapeDtypeStruct((L, 3) + shard, grad.dtype)
    if landing is None:
        return _Rider((grad,), (out,), 3, start, finish)
    return _Rider((grad, landing), (out,), 3, start, finish, aliases=((1, 0),))


def _join_riders(riders):
    if len(riders) == 1:
        return riders[0]

    def parts(ins, outs, send_sems, recv_sems):
        i0 = o0 = s0 = 0
        for r in riders:
            ni, no = len(r.operands), len(r.out_shapes)
            yield (r, ins[i0:i0 + ni], outs[o0:o0 + no], send_sems.at[pl.ds(s0, r.n_sems)],
                   recv_sems.at[pl.ds(s0, r.n_sems)])
            i0, o0, s0 = i0 + ni, o0 + no, s0 + r.n_sems

    def start(*refs):
        for r, *own in parts(*refs):
            r.start(*own)

    def finish(*refs):
        for r, *own in parts(*refs):
            r.finish(*own)

    aliases, i0, o0 = [], 0, 0
    for r in riders:
        aliases += [(i0 + src, o0 + dst) for src, dst in r.aliases]
        i0, o0 = i0 + len(r.operands), o0 + len(r.out_shapes)
    return _Rider(tuple(x for r in riders for x in r.operands), tuple(x for r in riders for x in r.out_shapes),
                  sum(r.n_sems for r in riders), start, finish, tuple(aliases))


def _pair_swap(sums):
    n = len(sums)

    def body(*refs):
        ins, outs = refs[:n], refs[n:2 * n]
        send_sems, recv_sems = refs[2 * n:]
        x, y, c, _ = _place()
        cps = [pltpu.make_async_remote_copy(src_ref=ins[w], dst_ref=outs[w], send_sem=send_sems.at[w],
                                            recv_sem=recv_sems.at[w], device_id=(x, y, 1 - c), device_id_type=MESH)
               for w in range(n)]
        for cp in cps:
            cp.start()
        for cp in cps:
            cp.wait_recv()
        for cp in cps:
            cp.wait_send()

    return pl.pallas_call(
        body,
        name="pair_swap",
        in_specs=[_ANY] * n,
        out_specs=[_ANY] * n,
        out_shape=[jax.ShapeDtypeStruct(s.shape, s.dtype) for s in sums],
        scratch_shapes=[pltpu.SemaphoreType.DMA((n,)), pltpu.SemaphoreType.DMA((n,))],
        compiler_params=pltpu.CompilerParams(has_side_effects=True),
    )(*sums)


def _rows_tile(rows, cols):
    tr = rows
    while tr * cols > (1 << 19) and tr % 16 == 0:
        tr //= 2
    return tr


def _chip_sum(name, grad, landed, chip, saxis):
    L = grad.shape[0]
    _, _, R, C = landed.shape
    tr = _rows_tile(R, C)
    nr = R // tr
    if saxis == 2:
        g_idx = lambda l, i, s: (l, i, s[0])
    else:
        g_idx = lambda l, i, s: (l, s[0] * nr + i, 0)

    def body(s_ref, g_ref, l_ref, o_ref):
        tot = ((g_ref[...].astype(F32) + l_ref[0].astype(F32)) + l_ref[1].astype(F32)) + l_ref[2].astype(F32)
        o_ref[...] = tot.astype(o_ref.dtype)

    return pl.pallas_call(
        body,
        name=name,
        grid_spec=pltpu.PrefetchScalarGridSpec(
            num_scalar_prefetch=1,
            grid=(L, nr),
            in_specs=[pl.BlockSpec((None, tr, C), g_idx), pl.BlockSpec((None, 3, tr, C), lambda l, i, s: (l, 0, i, 0))],
            out_specs=pl.BlockSpec((None, tr, C), lambda l, i, s: (l, i, 0)),
        ),
        out_shape=jax.ShapeDtypeStruct((L, R, C), BF16),
        compiler_params=_cp(("parallel", "parallel")),
    )(chip, grad, landed)


def _all_reduce_small(packed):
    R = packed.shape[0]

    def body(p_ref, o_ref, land_ref, send_sems, recv_sems):
        x, y, c, _ = _place()
        me = 4 * x + 2 * y + c
        sends, recvs = [], []
        for r in range(1, N_DEV):
            px, py, pc = x ^ (r >> 2), y ^ ((r >> 1) & 1), c ^ (r & 1)
            cp = pltpu.make_async_remote_copy(src_ref=p_ref, dst_ref=land_ref.at[me], send_sem=send_sems.at[r - 1],
                                              recv_sem=recv_sems.at[r - 1], device_id=(px, py, pc), device_id_type=MESH)
            cp.start()
            sends.append(cp)
            recvs.append(pltpu.make_async_remote_copy(src_ref=p_ref, dst_ref=land_ref.at[4 * px + 2 * py + pc],
                                                      send_sem=send_sems.at[r - 1], recv_sem=recv_sems.at[r - 1],
                                                      device_id=(px, py, pc), device_id_type=MESH))
        land_ref[me] = p_ref[...]
        for cp in recvs:
            cp.wait_recv()
        for cp in sends:
            cp.wait_send()
        acc = land_ref[0]
        for d in range(1, N_DEV):
            acc = acc + land_ref[d]
        o_ref[...] = acc

    vm = pl.BlockSpec(memory_space=pltpu.VMEM)
    return pl.pallas_call(
        body,
        name="all_reduce_small",
        in_specs=[vm],
        out_specs=vm,
        out_shape=jax.ShapeDtypeStruct((R, LANE), F32),
        scratch_shapes=[pltpu.VMEM((N_DEV, R, LANE), F32), pltpu.SemaphoreType.DMA((N_DEV - 1,)),
                        pltpu.SemaphoreType.DMA((N_DEV - 1,))],
        compiler_params=pltpu.CompilerParams(has_side_effects=True, vmem_limit_bytes=VMEM_LIMIT),
    )(packed)


def _adamw(name, w, m, v, grads):
    R, C = w.shape
    tr = _rows_tile(R, C)
    c1 = 1.0 - ADAM_B1 ** ADAM_STEP
    c2 = 1.0 - ADAM_B2 ** ADAM_STEP
    ng = len(grads)

    def body(*refs):
        w_ref, m_ref, v_ref = refs[:3]
        g_refs = refs[3:3 + ng]
        g_ref, d_ref, nm_ref, nv_ref = refs[3 + ng:]
        gv = g_refs[0][...].astype(F32)
        for r in g_refs[1:]:
            gv = gv + r[...].astype(F32)
        g_ref[...] = gv
        nm = ADAM_B1 * m_ref[...] + (1.0 - ADAM_B1) * gv
        nv = ADAM_B2 * v_ref[...] + (1.0 - ADAM_B2) * (gv * gv)
        nm_ref[...] = nm
        nv_ref[...] = nv
        d_ref[...] = -ADAM_LR * ((nm / c1) / (jnp.sqrt(nv / c2) + ADAM_EPS) + ADAM_WD * w_ref[...])

    blk = pl.BlockSpec((tr, C), lambda i: (i, 0))
    out = jax.ShapeDtypeStruct((R, C), F32)
    return pl.pallas_call(
        body,
        name=name,
        grid=(R // tr,),
        in_specs=[blk] * (3 + ng),
        out_specs=[blk] * 4,
        out_shape=[out] * 4,
        compiler_params=_cp(("parallel",)),
    )(w, m, v, *grads)


def _pack(parts):
    rows = []
    for p in parts:
        flat = p.reshape(-1).astype(F32)
        n = -(-flat.shape[0] // (8 * LANE)) * (8 * LANE)
        rows.append(jnp.pad(flat, (0, n - flat.shape[0])).reshape(n // LANE, LANE))
    return jnp.concatenate(rows, axis=0)


def _unpack(packed, like):
    out, r = [], 0
    for p in like:
        size = int(np.prod(p.shape))
        n = -(-size // (8 * LANE)) * 8
        out.append(packed[r:r + n].reshape(-1)[:size].reshape(p.shape))
        r += n
    return out


def kernel(x, mix_norm, ffn_norm, w_ffn_in, w_ffn_out, ab_w_in, ab_gn_gain, ab_w_pool, ab_pool_scale, ab_w_out, c_w_qkv, c_rel_bias, c_w_out, final_norm, loss_target, m_mix_norm, m_ffn_norm, m_w_ffn_in, m_w_ffn_out, m_ab_w_in, m_ab_gn_gain, m_ab_w_pool, m_ab_pool_scale, m_ab_w_out, m_c_w_qkv, m_c_rel_bias, m_c_w_out, m_final_norm, v_mix_norm, v_ffn_norm, v_w_ffn_in, v_w_ffn_out, v_ab_w_in, v_ab_gn_gain, v_ab_w_pool, v_ab_pool_scale, v_ab_w_out, v_c_w_qkv, v_c_rel_bias, v_c_w_out, v_final_norm):
    w = dict(mix_norm=mix_norm, ffn_norm=ffn_norm, w_ffn_in=w_ffn_in, w_ffn_out=w_ffn_out, ab_w_in=ab_w_in,
             ab_gn_gain=ab_gn_gain, ab_w_pool=ab_w_pool, ab_pool_scale=ab_pool_scale, ab_w_out=ab_w_out,
             c_w_qkv=c_w_qkv, c_rel_bias=c_rel_bias, c_w_out=c_w_out, final_norm=final_norm)
    m = dict(mix_norm=m_mix_norm, ffn_norm=m_ffn_norm, w_ffn_in=m_w_ffn_in, w_ffn_out=m_w_ffn_out, ab_w_in=m_ab_w_in,
             ab_gn_gain=m_ab_gn_gain, ab_w_pool=m_ab_w_pool, ab_pool_scale=m_ab_pool_scale, ab_w_out=m_ab_w_out,
             c_w_qkv=m_c_w_qkv, c_rel_bias=m_c_rel_bias, c_w_out=m_c_w_out, final_norm=m_final_norm)
    v = dict(mix_norm=v_mix_norm, ffn_norm=v_ffn_norm, w_ffn_in=v_w_ffn_in, w_ffn_out=v_w_ffn_out, ab_w_in=v_ab_w_in,
             ab_gn_gain=v_ab_gn_gain, ab_w_pool=v_ab_w_pool, ab_pool_scale=v_ab_pool_scale, ab_w_out=v_ab_w_out,
             c_w_qkv=v_c_w_qkv, c_rel_bias=v_c_rel_bias, c_w_out=v_c_w_out, final_norm=v_final_norm)
    S = x.shape[1]
    cx, cy, cc = lax.axis_index("x"), lax.axis_index("y"), lax.axis_index("c")
    chip = jnp.reshape(2 * cx + cy, (1,)).astype(jnp.int32)

    big = _Weights({k: w[k].astype(BF16) for k in _BIG})
    small = {k: w[k] for k in _SMALL}
    loss, grad_x, g_small, g_big, landed = _local_step(x.reshape(S, D_MODEL), loss_target.reshape(S, D_MODEL), small, big)

    sums = [_chip_sum(f"chip_sum_{k}", g_big[k], landed[k], chip, _SHARD_AXIS[k]) for k in _BIG]
    siblings = _pair_swap(sums)

    packed = _all_reduce_small(_pack([g_small[k] for k in _SMALL] + [loss]))
    small_like = [w[k] for k in _SMALL]
    g_red = dict(zip(_SMALL, _unpack(packed, small_like)))
    loss_row = packed.shape[0] - 8
    loss_out = packed[loss_row, 0]

    grad, delta, new_m, new_v = {}, {}, {}, {}
    for k, mine, theirs in zip(_BIG, sums, siblings):
        shp = w[k].shape
        two = (shp[0] * shp[1], shp[2])
        outs = _adamw(f"adamw_{k}", w[k].reshape(two), m[k].reshape(two), v[k].reshape(two),
                      (mine.reshape(two), theirs.reshape(two)))
        grad[k], delta[k], new_m[k], new_v[k] = [o.reshape(shp) for o in outs]
    _, d, nm, nv = _adamw("adamw_small", _pack(small_like), _pack([m[k] for k in _SMALL]), _pack([v[k] for k in _SMALL]),
                          (packed[:loss_row],))
    for k, dk, mk, vk in zip(_SMALL, _unpack(d, small_like), _unpack(nm, small_like), _unpack(nv, small_like)):
        grad[k], delta[k], new_m[k], new_v[k] = g_red[k], dk, mk, vk

    order = ("mix_norm", "ffn_norm", "w_ffn_in", "w_ffn_out", "ab_w_in", "ab_gn_gain", "ab_w_pool", "ab_pool_scale",
             "ab_w_out", "c_w_qkv", "c_rel_bias", "c_w_out", "final_norm")
    return (loss_out, grad_x.reshape(x.shape), *[grad[k] for k in order], *[delta[k] for k in order],
            *[new_m[k] for k in order], *[new_v[k] for k in order])
```

```python
import functools
from typing import Callable, NamedTuple

import numpy as np
import jax
import jax.numpy as jnp
from jax import lax
from jax.experimental import pallas as pl
from jax.experimental.pallas import tpu as pltpu

F32 = jnp.float32
BF16 = jnp.bfloat16

D_MODEL = 1024
D_FF = 4096
DEPTH = 4
CHUNK = 64
RMS_EPS = 1e-6
RET_WIDTH = 512
RET_HEADS = 4
RET_HEAD_DIM = 128
RET_ROPE_BASE = 10000.0
GN_EPS = 1e-5
POOL_WIDTH = 512
POOL_WINDOWS = (2, 4, 8, 16)
POOL_GROUP_DIM = 128
POOL_HALO = 16
AB_IN_WIDTH = 2560
ATT_HEADS = 16
ATT_HEAD_DIM = 64
LEFT_CHUNKS = 8
BAND = (LEFT_CHUNKS + 1) * CHUNK
REL_CLIP = 128
N_REL = 2 * REL_CLIP + 1
N_REL_PAD = 264
NEG_INF = -1e30
KSCALE = RET_HEAD_DIM ** -0.5
QSCALE = ATT_HEAD_DIM ** -0.5

ADAM_LR = 0.001
ADAM_B1 = 0.9
ADAM_B2 = 0.999
ADAM_EPS = 1e-08
ADAM_WD = 0.01
ADAM_STEP = 10

ATT_BLOCK = LEFT_CHUNKS * CHUNK
RET_BLOCK = 512
N_CHIPS = 4
N_DEV = 8
LANE = 128
VMEM_LIMIT = 52 * 1024 * 1024
EPI_ROWS = 256
MESH = pl.DeviceIdType.MESH


def _cp(sem, vmem=VMEM_LIMIT):
    return pltpu.CompilerParams(dimension_semantics=sem, vmem_limit_bytes=vmem)


def _dot(a, b):
    return lax.dot_general(a, b, (((1,), (0,)), ((), ())), preferred_element_type=F32)


def _dot_nt(a, b):
    return lax.dot_general(a, b, (((1,), (1,)), ((), ())), preferred_element_type=F32)


def _dot_tn(a, b):
    return lax.dot_general(a, b, (((0,), (0,)), ((), ())), preferred_element_type=F32)


_ANY = pl.BlockSpec(memory_space=pl.ANY)


class _Rider(NamedTuple):
    operands: tuple
    out_shapes: tuple
    n_sems: int
    start: Callable
    finish: Callable
    aliases: tuple = ()
    relay: Callable = None


def _mm(name, mode, a, b, *, la=None, lb=None, tm=1024, tn=1024, tk=1024, a_fn=None, b_fn=None,
        extras=(), aux=(), sides=(), epi=None, out_dtype=F32, stack=None, rider=None):
    a_parts = list(a) if isinstance(a, (list, tuple)) else [a]
    b_parts = list(b) if isinstance(b, (list, tuple)) else [b]
    na, nbp = len(a_parts), len(b_parts)
    a2, b2 = list(a_parts[0].shape[-2:]), list(b_parts[0].shape[-2:])
    a2[1] *= na
    b2[1] *= nbp
    if mode == "nn":
        (M, K), (K2, N) = a2, b2
    elif mode == "nt":
        (M, K), (N, K2) = a2, b2
    else:
        (K, M), (K2, N) = a2, b2
    assert K == K2, (name, a2, b2)
    tm, tn, tk = min(tm, M), min(tn, N), min(tk, K)
    assert M % tm == 0 and N % tn == 0 and K % tk == 0, (name, M, N, K, tm, tn, tk)
    gm, gn, gk = M // tm, N // tn, K // tk
    fold = mode == "nt" and na > 1 and gk == 1

    def specs(parts, block, idx, lead):
        per = parts[0].shape[-1] // block[1]
        assert parts[0].shape[-1] % block[1] == 0, (name, parts[0].shape, block)
        out = []
        for p in range(len(parts)):
            def f(i, j, k, p=p):
                r, c = idx(i, j, k)
                if len(parts) > 1:
                    c = jnp.clip(c - p * per, 0, per - 1)
                return (r, c) if lead is None else (lead, r, c)
            out.append(pl.BlockSpec(block if lead is None else (None,) + block, f))
        return out, per

    if mode == "nn":
        a_specs, a_per = specs(a_parts, (tm, tk), lambda i, j, k: (i, k), la)
        b_specs, b_per = specs(b_parts, (tk, tn), lambda i, j, k: (k, j), lb)
        a_axis, b_axis, dot = 2, 1, _dot
    elif mode == "nt":
        if fold:
            a_specs, a_per = [pl.BlockSpec((tm, K // na), lambda i, j, k: (i, 0)) for _ in a_parts], 1
        else:
            a_specs, a_per = specs(a_parts, (tm, tk), lambda i, j, k: (i, k), la)
        b_specs, b_per = specs(b_parts, (tn, tk), lambda i, j, k: (j, k), lb)
        a_axis, b_axis, dot = 2, 2, _dot_nt
    else:
        a_specs, a_per = specs(a_parts, (tk, tm), lambda i, j, k: (k, i), la)
        b_specs, b_per = specs(b_parts, (tk, tn), lambda i, j, k: (k, j), lb)
        a_axis, b_axis, dot = 0, 1, _dot_tn
    ex_specs = [pl.BlockSpec((tm, tn), lambda i, j, k: (i, j)) for _ in extras]
    n_ex = len(extras)

    n_aux, n_side = len(aux), len(sides)
    operands = a_parts + b_parts + list(extras) + list(aux)
    in_specs = a_specs + b_specs + ex_specs + [pl.BlockSpec(v.shape, lambda i, j, k, nd=v.ndim: (0,) * nd) for v in aux]
    aliases = {}
    if stack is None:
        out_specs = [pl.BlockSpec((tm, tn), lambda i, j, k: (i, j))]
        out_shapes = [jax.ShapeDtypeStruct((M, N), out_dtype)]
    else:
        n_layers, layer, prev = stack
        out_specs = [pl.BlockSpec((None, tm, tn), lambda i, j, k: (layer, i, j))]
        out_shapes = [jax.ShapeDtypeStruct((n_layers, M, N), out_dtype)]
        if prev is not None:
            aliases = {len(operands): 0}
            operands.append(prev)
            in_specs.append(_ANY)
    for kind, dtype in sides:
        if kind == "tile":
            out_specs.append(pl.BlockSpec((tm, tn), lambda i, j, k: (i, j)))
            out_shapes.append(jax.ShapeDtypeStruct((M, N), dtype))
        else:
            assert gn == 1, name
            out_specs.append(pl.BlockSpec((8, tn), lambda i, j, k: (0, 0)))
            out_shapes.append(jax.ShapeDtypeStruct((8, N), dtype))
    n_prev = len(aliases)
    scratch = [pltpu.VMEM((tm, tn), F32)] if gk > 1 else []
    n_rin = n_rout = 0
    if rider is not None:
        n_rin, n_rout = len(rider.operands), len(rider.out_shapes)
        for src, dst in rider.aliases:
            aliases[len(operands) + src] = 1 + n_side + dst
        operands += list(rider.operands)
        in_specs += [_ANY] * n_rin
        out_specs += [_ANY] * n_rout
        out_shapes += list(rider.out_shapes)
        scratch += [pltpu.SemaphoreType.DMA((rider.n_sems,)), pltpu.SemaphoreType.DMA((rider.n_sems,))]
    assert na == 1 or nbp == 1, name

    def body(*refs):
        a_refs, b_refs = refs[:na], refs[na:na + nbp]
        ex_refs = refs[na + nbp:na + nbp + n_ex + n_aux]
        n_in = na + nbp + n_ex + n_aux + n_prev
        rin = refs[n_in:n_in + n_rin]
        o_ref = refs[n_in + n_rin]
        side_refs = refs[n_in + n_rin + 1:n_in + n_rin + 1 + n_side]
        rout = refs[n_in + n_rin + 1 + n_side:n_in + n_rin + 1 + n_side + n_rout]
        rest = refs[n_in + n_rin + 1 + n_side + n_rout:]
        i, j, k = pl.program_id(0), pl.program_id(1), pl.program_id(2)
        if rider is not None:
            sems = rest[-2:]

            @pl.when(jnp.logical_and(i == 0, jnp.logical_and(j == 0, k == 0)))
            def _():
                rider.start(rin, rout, *sems)

        def finish(acc):
            if epi is None:
                o_ref[...] = acc[...].astype(o_ref.dtype)
                return
            strip = min(tm, EPI_ROWS)
            colsums = [None] * n_side
            for r0 in range(0, tm, strip):
                rows = slice(r0, r0 + strip)
                res = epi(acc[rows, :], *[r[rows, :] for r in ex_refs[:n_ex]], *[r[...] for r in ex_refs[n_ex:]])
                if n_side:
                    res, *side_vals = res
                    for s, ((kind, _), ref, val) in enumerate(zip(sides, side_refs, side_vals)):
                        if kind == "tile":
                            ref[rows, :] = val.astype(ref.dtype)
                        else:
                            colsums[s] = val if colsums[s] is None else colsums[s] + val
                o_ref[rows, :] = res.astype(o_ref.dtype)
            for (kind, _), ref, val in zip(sides, side_refs, colsums):
                if kind == "colsum":
                    @pl.when(i == 0)
                    def _(ref=ref, val=val):
                        ref[...] = val

                    @pl.when(i > 0)
                    def _(ref=ref, val=val):
                        ref[...] += val

                    @pl.when(i == gm - 1)
                    def _(ref=ref):
                        ref[0:1, :] = jnp.sum(ref[...], axis=0, keepdims=True)

        def step(a_ref, b_ref):
            av, bv = a_ref[...], b_ref[...]
            if a_fn is not None:
                av = a_fn(av)
            if b_fn is not None:
                bv = b_fn(bv)
            part = dot(av.astype(BF16), bv.astype(BF16))
            if gk == 1:
                finish(part)
                return
            acc_ref = rest[0]

            @pl.when(k == 0)
            def _():
                acc_ref[...] = part

            @pl.when(k > 0)
            def _():
                acc_ref[...] += part

        if fold:
            kp = K // na
            finish(sum(dot(a_refs[p][...].astype(BF16), b_refs[0][:, p * kp:(p + 1) * kp].astype(BF16))
                       for p in range(na)))
        elif na > 1:
            sel = pl.program_id(a_axis) // a_per
            for p in range(na):
                pl.when(sel == p)(functools.partial(step, a_refs[p], b_refs[0]))
        elif nbp > 1:
            sel = pl.program_id(b_axis) // b_per
            for p in range(nbp):
                pl.when(sel == p)(functools.partial(step, a_refs[0], b_refs[p]))
        else:
            step(a_refs[0], b_refs[0])
        if gk > 1:
            @pl.when(k == gk - 1)
            def _():
                finish(rest[0])

        if rider is not None:
            steps = gm * gn * gk
            step_no = (i * gn + j) * gk + k
            if rider.relay is not None:
                assert steps >= 3, name

                @pl.when(step_no == steps - 2)
                def _():
                    rider.relay(rin, rout, *sems)

            @pl.when(step_no == steps - 1)
            def _():
                rider.finish(rin, rout, *sems)

    sequential = rider is not None or any(kind == "colsum" for kind, _ in sides)
    sem = ("arbitrary",) * 3 if sequential else ("parallel", "parallel", "arbitrary")
    outs = pl.pallas_call(
        body,
        name=name,
        grid=(gm, gn, gk),
        in_specs=in_specs,
        out_specs=out_specs,
        out_shape=out_shapes,
        input_output_aliases=aliases,
        scratch_shapes=scratch,
        compiler_params=_cp(sem),
    )(*operands)
    res = outs[0] if not sides else tuple(outs[:1 + n_side])
    return res if rider is None else (res, list(outs[1 + n_side:]))


def _relu2(u):
    r = jnp.maximum(u, 0)
    return r * r


def _epi_residual(acc, res):
    return acc + res


def _epi_residual_norm(acc, res, g):
    xn = acc + res
    r = lax.rsqrt(jnp.mean(xn * xn, axis=-1, keepdims=True) + RMS_EPS)
    return xn, (xn * r) * g


def _epi_rms_bwd(dh, x, dres, g):
    r = lax.rsqrt(jnp.mean(x * x, axis=-1, keepdims=True) + RMS_EPS)
    xh = x * r
    dxh = dh * g
    dx = dres + r * (dxh - xh * jnp.mean(dxh * xh, axis=-1, keepdims=True))
    return dx, jnp.sum((dh * xh).reshape(dh.shape[0] // 8, 8, dh.shape[1]), axis=0)


def _rms_fwd(name, x, g):
    S, D = x.shape
    tq = min(1024, S)

    def body(x_ref, g_ref, o_ref):
        xv = x_ref[...]
        r = lax.rsqrt(jnp.mean(xv * xv, axis=-1, keepdims=True) + RMS_EPS)
        o_ref[...] = ((xv * r) * g_ref[...]).astype(o_ref.dtype)

    return pl.pallas_call(
        body,
        name=name,
        grid=(S // tq,),
        in_specs=[pl.BlockSpec((tq, D), lambda i: (i, 0)), pl.BlockSpec((1, D), lambda i: (0, 0))],
        out_specs=pl.BlockSpec((tq, D), lambda i: (i, 0)),
        out_shape=jax.ShapeDtypeStruct((S, D), BF16),
        compiler_params=_cp(("parallel",)),
    )(x, g)


def _loss_head(x, g, t):
    S, D = x.shape
    tq = min(512, S)
    n = S // tq

    def body(x_ref, g_ref, t_ref, loss_ref, dx_ref, dg_ref, lacc_ref, gacc_ref):
        i = pl.program_id(0)
        xv = x_ref[...]
        gv = g_ref[...]
        r = lax.rsqrt(jnp.mean(xv * xv, axis=-1, keepdims=True) + RMS_EPS)
        xh = xv * r
        e = xh * gv - t_ref[...]
        dy = e * (1.0 / D)
        dxh = dy * gv
        dx_ref[...] = r * (dxh - xh * jnp.mean(dxh * xh, axis=-1, keepdims=True))
        lpart = jnp.sum((e * e).reshape(tq // 8, 8, D), axis=0)
        gpart = jnp.sum((dy * xh).reshape(tq // 8, 8, D), axis=0)

        @pl.when(i == 0)
        def _():
            lacc_ref[...] = lpart
            gacc_ref[...] = gpart

        @pl.when(i > 0)
        def _():
            lacc_ref[...] += lpart
            gacc_ref[...] += gpart

        @pl.when(i == n - 1)
        def _():
            dg_ref[...] = jnp.sum(gacc_ref[...], axis=0, keepdims=True)
            tot = jnp.sum(jnp.sum(lacc_ref[...], axis=0, keepdims=True), axis=1, keepdims=True)
            loss_ref[...] = jnp.broadcast_to(tot * (0.5 / D), (1, LANE))

    return pl.pallas_call(
        body,
        name="loss_head",
        grid=(n,),
        in_specs=[pl.BlockSpec((tq, D), lambda i: (i, 0)), pl.BlockSpec((1, D), lambda i: (0, 0)),
                  pl.BlockSpec((tq, D), lambda i: (i, 0))],
        out_specs=[pl.BlockSpec((1, LANE), lambda i: (0, 0)), pl.BlockSpec((tq, D), lambda i: (i, 0)),
                   pl.BlockSpec((1, D), lambda i: (0, 0))],
        out_shape=[jax.ShapeDtypeStruct((1, LANE), F32), jax.ShapeDtypeStruct((S, D), F32),
                   jax.ShapeDtypeStruct((1, D), F32)],
        scratch_shapes=[pltpu.VMEM((8, D), F32), pltpu.VMEM((8, D), F32)],
        compiler_params=_cp(("arbitrary",)),
    )(x, g, t)


def _ret_tables(S):
    T = min(RET_BLOCK, S)
    inv_freq = 1.0 / (RET_ROPE_BASE ** jnp.linspace(0.0, 1.0, RET_HEAD_DIM // 2, dtype=F32))
    ang = jnp.arange(S, dtype=F32)[:, None] * inv_freq[None, :]
    cos, sin = jnp.cos(ang), jnp.sin(ang)
    cosf = jnp.repeat(cos, 2, axis=-1)
    sins = jnp.stack([-sin, sin], axis=-1).reshape(S, RET_HEAD_DIM)
    log_g = np.log1p(-np.power(2.0, -5.0 - np.arange(RET_HEADS, dtype=np.float64)))
    pos = np.arange(T, dtype=np.float64)
    diff = pos[:, None] - pos[None, :]
    same = (pos[:, None] // CHUNK) == (pos[None, :] // CHUNK)
    seen = same | (diff > 0)
    dmat = np.where(seen[None], np.exp(np.abs(diff)[None] * log_g[:, None, None]), 0.0)
    aq = np.exp((pos[None, :] + 1.0) * log_g[:, None])
    ak = np.exp((T - 1.0 - pos[None, :]) * log_g[:, None])
    lam = np.exp(T * log_g)
    bc = lambda v: jnp.asarray(np.broadcast_to(v[..., None], v.shape + (LANE,)), F32)
    return dict(cos=cosf, sin=sins, dmat=jnp.asarray(dmat, F32), aq=bc(aq), ak=bc(ak),
                lam=jnp.asarray(np.broadcast_to(lam[:, None, None], (RET_HEADS, 1, LANE)), F32))


def _rot(x, cos, sin_s, even):
    sw = jnp.where(even, pltpu.roll(x, LANE - 1, 1), pltpu.roll(x, 1, 1))
    return x * cos + sw * sin_s


def _rot_t(dy, cos, sin_s, even):
    t = dy * sin_s
    return dy * cos + jnp.where(even, pltpu.roll(t, LANE - 1, 1), pltpu.roll(t, 1, 1))


def _ret_specs(T, rev_nb=None):
    blk = (lambda b: b) if rev_nb is None else (lambda b: rev_nb - 1 - b)
    whole = lambda shape: pl.BlockSpec(shape, lambda b: (0,) * len(shape))
    specs = [pl.BlockSpec((T, AB_IN_WIDTH), lambda b: (blk(b), 0)),
             pl.BlockSpec((T, LANE), lambda b: (blk(b), 0)),
             pl.BlockSpec((T, LANE), lambda b: (blk(b), 0)),
             whole((RET_HEADS, T, T)), whole((RET_HEADS, T, LANE)), whole((RET_HEADS, T, LANE)),
             whole((RET_HEADS, 1, LANE)), whole((1, RET_WIDTH))]
    return specs, blk


def _head_views(h, z_ref, tabs, token_refs, head_refs):
    zs = [z_ref.at[:, (o * RET_HEADS + h) * LANE:(o * RET_HEADS + h + 1) * LANE] for o in range(4)]
    hs = slice(h * LANE, (h + 1) * LANE)
    return zs, [t.at[h] for t in tabs], [r.at[:, hs] for r in token_refs], [r.at[h] for r in head_refs]


def _ret_fwd(name, z, tb, gain):
    S = z.shape[0]
    T = min(RET_BLOCK, S)
    nb = S // T
    specs, blk = _ret_specs(T)

    def body(z_ref, cos_r, sin_r, d_all, aq_all, ak_all, lam_all, gain_all, cat_all, opre_all, st_all, state_all):
        @pl.when(pl.program_id(0) == 0)
        def _():
            state_all[...] = jnp.zeros_like(state_all)

        for h in range(RET_HEADS):
            zs, tabs, toks, heads = _head_views(h, z_ref, (d_all, aq_all, ak_all, lam_all),
                                                (gain_all, cat_all, opre_all), (st_all, state_all))
            head(*zs, cos_r, sin_r, *tabs, *toks, *heads)

    def head(zq, zk, zv, zg, cos_r, sin_r, d_r, aq_r, ak_r, lam_r, gain_r, ret_o, opre_o, st_o, state):
        even = (lax.broadcasted_iota(jnp.int32, (T, LANE), 1) & 1) == 0
        c, s = cos_r[...], sin_r[...]
        q = _rot(zq[...], c, s, even)
        k = _rot(zk[...], c, s, even) * KSCALE
        qb, kb, vb = q.astype(BF16), k.astype(BF16), zv[...].astype(BF16)
        p = (_dot_nt(qb, kb) * d_r[...]).astype(BF16)
        st = state[...]
        st_o[...] = st
        o = _dot(p, vb) + _dot((q * aq_r[...]).astype(BF16), st.astype(BF16))
        state[...] = st * lam_r[...] + _dot_tn((k * ak_r[...]).astype(BF16), vb)
        opre_o[...] = o
        mu = jnp.mean(o, axis=-1, keepdims=True)
        d = o - mu
        y = d * lax.rsqrt(jnp.mean(d * d, axis=-1, keepdims=True) + GN_EPS)
        g = zg[...]
        ret_o[...] = ((g * jax.nn.sigmoid(g)) * (y * gain_r[...])).astype(ret_o.dtype)

    out_blk = pl.BlockSpec((T, RET_WIDTH), lambda b: (b, 0))
    return pl.pallas_call(
        body,
        name=name,
        grid=(nb,),
        in_specs=specs,
        out_specs=[out_blk, out_blk, pl.BlockSpec((RET_HEADS, None, LANE, LANE), lambda b: (0, b, 0, 0))],
        out_shape=[jax.ShapeDtypeStruct((S, D_MODEL), BF16), jax.ShapeDtypeStruct((S, RET_WIDTH), F32),
                   jax.ShapeDtypeStruct((RET_HEADS, nb, LANE, LANE), F32)],
        scratch_shapes=[pltpu.VMEM((RET_HEADS, LANE, LANE), F32)],
        compiler_params=_cp(("arbitrary",)),
    )(z, tb["cos"], tb["sin"], tb["dmat"], tb["aq"], tb["ak"], tb["lam"], gain)


def _ret_bwd(name, z, tb, gain, opre, states, dcat):
    S = z.shape[0]
    T = min(RET_BLOCK, S)
    nb = S // T
    specs, blk = _ret_specs(T, rev_nb=nb)
    tok = pl.BlockSpec((T, RET_WIDTH), lambda b: (blk(b), 0))

    def body(z_ref, cos_r, sin_r, d_all, aq_all, ak_all, lam_all, gain_all, opre_all, st_all, dret_all,
             dz_ref, dgain_all, dstate_all):
        @pl.when(pl.program_id(0) == 0)
        def _():
            dstate_all[...] = jnp.zeros_like(dstate_all)
            dgain_all[...] = jnp.zeros_like(dgain_all)

        for h in range(RET_HEADS):
            zs, tabs, toks, heads = _head_views(h, z_ref, (d_all, aq_all, ak_all, lam_all),
                                                (gain_all, opre_all, dret_all, dgain_all), (st_all, dstate_all))
            dzs, _, _, _ = _head_views(h, dz_ref, (), (), ())
            gain_r, opre_r, dret_r, dgain_o = toks
            head(*zs, cos_r, sin_r, *tabs, gain_r, opre_r, heads[0], dret_r, *dzs, dgain_o, heads[1])

    def head(zq, zk, zv, zg, cos_r, sin_r, d_r, aq_r, ak_r, lam_r, gain_r, opre_r, st_r, dret_r,
             dq_o, dk_o, dv_o, dg_o, dgain_o, dstate):
        even = (lax.broadcasted_iota(jnp.int32, (T, LANE), 1) & 1) == 0
        c, s = cos_r[...], sin_r[...]
        aq, ak, dm = aq_r[...], ak_r[...], d_r[...]
        q = _rot(zq[...], c, s, even)
        k = _rot(zk[...], c, s, even) * KSCALE
        qb, kb, vb = q.astype(BF16), k.astype(BF16), zv[...].astype(BF16)
        pb = (_dot_nt(qb, kb) * dm).astype(BF16)
        g = zg[...]
        sig = jax.nn.sigmoid(g)
        o = opre_r[...]
        mu = jnp.mean(o, axis=-1, keepdims=True)
        d = o - mu
        rstd = lax.rsqrt(jnp.mean(d * d, axis=-1, keepdims=True) + GN_EPS)
        y = d * rstd
        gain_v = gain_r[...]
        dret = dret_r[...].astype(F32)
        dyg = dret * (g * sig)
        dg_o[...] = (dret * (y * gain_v) * (sig * (1.0 + g * (1.0 - sig)))).astype(dg_o.dtype)
        dgain_o[...] += jnp.sum(dyg * y, axis=0, keepdims=True)
        dy = dyg * gain_v
        do = rstd * (dy - jnp.mean(dy, axis=-1, keepdims=True) - y * jnp.mean(dy * y, axis=-1, keepdims=True))
        dob = do.astype(BF16)
        stb = st_r[...].astype(BF16)
        dsn = dstate[...]
        dsnb = dsn.astype(BF16)
        dpb = (_dot_nt(dob, vb) * dm).astype(BF16)
        dq = _dot(dpb, kb) + _dot_nt(dob, stb) * aq
        dk = _dot_tn(dpb, qb) + _dot_nt(vb, dsnb) * ak
        dv = _dot_tn(pb, dob) + _dot((k * ak).astype(BF16), dsnb)
        dstate[...] = dsn * lam_r[...] + _dot_tn((q * aq).astype(BF16), dob)
        dq_o[...] = _rot_t(dq, c, s, even).astype(dq_o.dtype)
        dk_o[...] = _rot_t(dk * KSCALE, c, s, even).astype(dk_o.dtype)
        dv_o[...] = dv.astype(dv_o.dtype)

    return pl.pallas_call(
        body,
        name=name,
        grid=(nb,),
        in_specs=specs + [tok, pl.BlockSpec((RET_HEADS, None, LANE, LANE), lambda b: (0, blk(b), 0, 0)), tok],
        out_specs=[pl.BlockSpec((T, 4 * RET_WIDTH), lambda b: (blk(b), 0)), pl.BlockSpec((1, RET_WIDTH), lambda b: (0, 0))],
        out_shape=[jax.ShapeDtypeStruct((S, AB_IN_WIDTH), BF16), jax.ShapeDtypeStruct((1, RET_WIDTH), F32)],
        scratch_shapes=[pltpu.VMEM((RET_HEADS, LANE, LANE), F32)],
        compiler_params=_cp(("arbitrary",)),
    )(z, tb["cos"], tb["sin"], tb["dmat"], tb["aq"], tb["ak"], tb["lam"], gain, opre, states, dcat)


def _pool_counts(t0, rows):
    t = t0 + lax.broadcasted_iota(jnp.int32, (rows, POOL_WIDTH), 0)
    grp = lax.broadcasted_iota(jnp.int32, (rows, POOL_WIDTH), 1) >> 7
    win = jnp.where(grp == 0, POOL_WINDOWS[0], jnp.where(grp == 1, POOL_WINDOWS[1],
                    jnp.where(grp == 2, POOL_WINDOWS[2], POOL_WINDOWS[3])))
    return jnp.maximum(jnp.minimum(t + 1, win), 1).astype(F32), grp


def _window_sums(ext, grp, sign):
    n = ext.shape[0]
    sh = lambda v, k: pltpu.roll(v, k % n if sign > 0 else (n - k) % n, 0)
    s2 = ext + sh(ext, 1)
    s4 = s2 + sh(s2, 2)
    s8 = s4 + sh(s4, 4)
    s16 = s8 + sh(s8, 8)
    return jnp.where(grp == 0, s2, jnp.where(grp == 1, s4, jnp.where(grp == 2, s8, s16)))


def _pool_fwd(name, z, w_pool, scale, cat):
    S = z.shape[0]
    T = min(512, S)
    nb = S // T
    pcol = AB_IN_WIDTH // POOL_WIDTH - 1
    hb = T // POOL_HALO

    def body(p_ref, halo_ref, w_ref, sc_ref, cat_in, out_ref, pooled_ref):
        b = pl.program_id(0)
        cur = p_ref[...]
        halo = jnp.where(b > 0, halo_ref[...], 0.0)
        ext = jnp.concatenate([halo, cur], axis=0)
        cnt, grp = _pool_counts(b * T - POOL_HALO, T + POOL_HALO)
        sums = _window_sums(ext, grp, +1)
        pooled = (sums / cnt)[POOL_HALO:] - cur
        pb = pooled.astype(BF16)
        pooled_ref[...] = pb
        for gi in range(len(POOL_WINDOWS)):
            cs = slice(gi * POOL_GROUP_DIM, (gi + 1) * POOL_GROUP_DIM)
            mixed = _dot(pb[:, cs], w_ref[gi].astype(BF16))
            out_ref[:, cs] = (mixed * sc_ref[:, cs]).astype(out_ref.dtype)

    return pl.pallas_call(
        body,
        name=name,
        grid=(nb,),
        in_specs=[pl.BlockSpec((T, POOL_WIDTH), lambda b: (b, pcol)),
                  pl.BlockSpec((POOL_HALO, POOL_WIDTH), lambda b: (jnp.maximum(b * hb - 1, 0), pcol)),
                  pl.BlockSpec((4, POOL_GROUP_DIM, POOL_GROUP_DIM), lambda b: (0, 0, 0)),
                  pl.BlockSpec((1, POOL_WIDTH), lambda b: (0, 0)), _ANY],
        out_specs=[pl.BlockSpec((T, POOL_WIDTH), lambda b: (b, 1)), pl.BlockSpec((T, POOL_WIDTH), lambda b: (b, 0))],
        out_shape=[jax.ShapeDtypeStruct(cat.shape, cat.dtype), jax.ShapeDtypeStruct((S, POOL_WIDTH), BF16)],
        input_output_aliases={4: 0},
        compiler_params=_cp(("parallel",)),
    )(z, z, w_pool, scale, cat)


def _pool_bwd(name, pooled, w_pool, scale, dcat, dz):
    S = pooled.shape[0]
    T = min(512, S)
    nb = S // T
    hb = T // POOL_HALO
    last_h = S // POOL_HALO - 1
    pcol = AB_IN_WIDTH // POOL_WIDTH - 1

    def body(d_ref, dn_ref, pooled_ref, w_ref, sc_ref, dz_in, dp_ref, dw_ref, dsc_ref):
        b = pl.program_id(0)

        @pl.when(b == 0)
        def _():
            dw_ref[...] = jnp.zeros_like(dw_ref)
            dsc_ref[...] = jnp.zeros_like(dsc_ref)

        sc = sc_ref[...]
        dout = d_ref[...].astype(F32)
        dnext = jnp.where(b < nb - 1, dn_ref[...].astype(F32), 0.0)
        dmix = jnp.concatenate([dout, dnext], axis=0) * sc
        dmb = dmix.astype(BF16)
        pb = pooled_ref[...]
        dpooled = []
        for gi in range(len(POOL_WINDOWS)):
            cs = slice(gi * POOL_GROUP_DIM, (gi + 1) * POOL_GROUP_DIM)
            wb = w_ref[gi].astype(BF16)
            dpooled.append(_dot_nt(dmb[:, cs], wb))
            dw_ref[gi] += _dot_tn(pb[:, cs], dmb[:T, cs])
            mixed = _dot(pb[:, cs], wb)
            dsc_ref[:, cs] += jnp.sum(dout[:, cs] * mixed, axis=0, keepdims=True)
        dpl = jnp.concatenate(dpooled, axis=1)
        cnt, grp = _pool_counts(b * T, T + POOL_HALO)
        sums = _window_sums(dpl / cnt, grp, -1)
        dp_ref[...] = (sums[:T] - dpl[:T]).astype(dp_ref.dtype)

    return pl.pallas_call(
        body,
        name=name,
        grid=(nb,),
        in_specs=[pl.BlockSpec((T, POOL_WIDTH), lambda b: (b, 1)),
                  pl.BlockSpec((POOL_HALO, POOL_WIDTH), lambda b: (jnp.minimum((b + 1) * hb, last_h), 1)),
                  pl.BlockSpec((T, POOL_WIDTH), lambda b: (b, 0)),
                  pl.BlockSpec((4, POOL_GROUP_DIM, POOL_GROUP_DIM), lambda b: (0, 0, 0)),
                  pl.BlockSpec((1, POOL_WIDTH), lambda b: (0, 0)), _ANY],
        out_specs=[pl.BlockSpec((T, POOL_WIDTH), lambda b: (b, pcol)),
                   pl.BlockSpec((4, POOL_GROUP_DIM, POOL_GROUP_DIM), lambda b: (0, 0, 0)),
                   pl.BlockSpec((1, POOL_WIDTH), lambda b: (0, 0))],
        out_shape=[jax.ShapeDtypeStruct(dz.shape, dz.dtype),
                   jax.ShapeDtypeStruct((4, POOL_GROUP_DIM, POOL_GROUP_DIM), F32),
                   jax.ShapeDtypeStruct((1, POOL_WIDTH), F32)],
        input_output_aliases={5: 0},
        compiler_params=_cp(("arbitrary",)),
    )(dcat, dcat, pooled, w_pool, scale, dz)


ATT_STRIP = 32
ATT_Q = 256
ATT_W = ATT_Q + LEFT_CHUNKS * CHUNK


def _rel_index():
    j = np.arange(ATT_W)
    rel = np.clip(LEFT_CHUNKS * CHUNK - j, -REL_CLIP, REL_CLIP) + REL_CLIP
    fwd = np.where(j < BAND, rel, N_REL)
    bwd = np.where(j <= ATT_W - CHUNK, fwd, 2 * REL_CLIP)
    return tuple(jnp.asarray(v.reshape(1, ATT_W), jnp.int32) for v in (fwd, bwd))


def _bias_table(name, rel_bias, rel_idx):
    rb = jnp.concatenate([rel_bias, jnp.full((ATT_HEADS, 1), NEG_INF, F32),
                          jnp.zeros((ATT_HEADS, N_REL_PAD - N_REL - 1), F32)], axis=1)

    def body(rb_ref, idx_ref, o_ref, row0_ref):
        r = lax.broadcasted_iota(jnp.int32, (N_REL_PAD, ATT_W), 0)
        onehot = (r == idx_ref[...]).astype(F32)
        row0_ref[...] = jnp.dot(rb_ref[...], onehot, precision=lax.Precision.HIGHEST, preferred_element_type=F32)
        col = lax.broadcasted_iota(jnp.int32, (CHUNK, ATT_W), 1)
        row = lax.broadcasted_iota(jnp.int32, (CHUNK, ATT_W), 0)
        for h in range(ATT_HEADS):
            same = jnp.broadcast_to(row0_ref[pl.ds(h, 1), :], (CHUNK, ATT_W))
            turned = pltpu.roll(same, 0, 1, stride=1, stride_axis=0)
            o_ref[h] = jnp.where(col >= BAND, NEG_INF, jnp.where(col < row, same, turned))

    return pl.pallas_call(
        body,
        name=name,
        out_shape=jax.ShapeDtypeStruct((ATT_HEADS, CHUNK, ATT_W), F32),
        scratch_shapes=[pltpu.VMEM((ATT_HEADS, ATT_W), F32)],
        compiler_params=pltpu.CompilerParams(vmem_limit_bytes=VMEM_LIMIT),
    )(rb, rel_idx)


def _bias_grad(name, dband, rel_idx):
    def body(d_ref, idx_ref, o_ref, sums_ref):
        row = lax.broadcasted_iota(jnp.int32, (CHUNK, ATT_W), 0)
        for h in range(ATT_HEADS):
            back = d_ref[h]
            for bit in range(CHUNK.bit_length() - 1):
                back = jnp.where(((row >> bit) & 1) == 1, pltpu.roll(back, ATT_W - (1 << bit), 1), back)
            sums_ref[pl.ds(h, 1), :] = jnp.sum(back, axis=0, keepdims=True)
        r = lax.broadcasted_iota(jnp.int32, (N_REL_PAD, ATT_W), 0)
        onehot = (r == idx_ref[...]).astype(F32)
        o_ref[...] = lax.dot_general(sums_ref[...], onehot, (((1,), (1,)), ((), ())),
                                     precision=lax.Precision.HIGHEST, preferred_element_type=F32)

    out = pl.pallas_call(
        body,
        name=name,
        out_shape=jax.ShapeDtypeStruct((ATT_HEADS, N_REL_PAD), F32),
        scratch_shapes=[pltpu.VMEM((ATT_HEADS, ATT_W), F32)],
        compiler_params=pltpu.CompilerParams(vmem_limit_bytes=VMEM_LIMIT),
    )(dband, rel_idx)
    return out[:, :N_REL]


def _attn_unit(q_ref, kw_ref, bias_ref, e, u, lane):
    mine = (lane < ATT_HEAD_DIM) if e == 0 else (lane >= ATT_HEAD_DIM)
    qm = jnp.where(mine, q_ref[u * ATT_Q:(u + 1) * ATT_Q, :] * QSCALE, 0)
    kw = kw_ref[u * ATT_Q:u * ATT_Q + ATT_W, :]
    s = _dot_nt(qm, kw) + bias_ref[u, e]
    p = jnp.exp(s - jnp.max(s, axis=-1, keepdims=True))
    return p, 1.0 / jnp.sum(p, axis=-1, keepdims=True), qm, kw, mine


def _attn_in_specs(nb):
    T = ATT_BLOCK
    hp = ATT_HEADS // 2
    cur = lambda off: pl.BlockSpec((T, LANE), lambda h, b: (jnp.minimum(b, nb - 1), off + h))
    prev = lambda off: pl.BlockSpec((T, LANE), lambda h, b: (jnp.clip(b - 1, 0, nb - 1), off + h))
    return [cur(0), prev(hp), cur(hp), prev(2 * hp), cur(2 * hp),
            pl.BlockSpec((None, 2, CHUNK, ATT_W), lambda h, b: (h, 0, 0, 0))]


def _spread_bias(bias_ref, bm_ref, block):
    col = lax.broadcasted_iota(jnp.int32, (CHUNK, ATT_W), 1)
    for first in (True, False):
        @pl.when(block == (0 if first else 1))
        def _(first=first):
            for u in range(ATT_BLOCK // ATT_Q):
                for e in range(2):
                    for j in range(ATT_Q // CHUNK):
                        rows = pltpu.roll(bias_ref[e], j * CHUNK, 1)
                        if first:
                            rows = jnp.where(col >= ATT_BLOCK - u * ATT_Q, rows, NEG_INF)
                        bm_ref[u, e, j * CHUNK:(j + 1) * CHUNK, :] = rows


def _attn_fwd(name, qkv, bias):
    S = qkv.shape[0]
    T = ATT_BLOCK
    nb = S // T

    def body(q_ref, kp_ref, kc_ref, vp_ref, vc_ref, band_ref, o_ref, kw_ref, vw_ref, bias_ref, s_ref, p_ref, inv_ref):
        _spread_bias(band_ref, bias_ref, pl.program_id(1))
        kw_ref[0:T] = kp_ref[...]
        kw_ref[T:2 * T] = kc_ref[...]
        vw_ref[0:T] = vp_ref[...]
        vw_ref[T:2 * T] = vc_ref[...]
        lane = lax.broadcasted_iota(jnp.int32, (ATT_Q, LANE), 1)
        for u in range(T // ATT_Q):
            vw = vw_ref[u * ATT_Q:u * ATT_Q + ATT_W, :]
            kw = kw_ref[u * ATT_Q:u * ATT_Q + ATT_W, :]
            outs = []
            for e in range(2):
                mine = (lane < ATT_HEAD_DIM) if e == 0 else (lane >= ATT_HEAD_DIM)
                qm = jnp.where(mine, q_ref[u * ATT_Q:(u + 1) * ATT_Q, :] * QSCALE, 0)
                s_ref[e] = _dot_nt(qm, kw)
                for r in range(ATT_Q // ATT_STRIP):
                    rows = slice(r * ATT_STRIP, (r + 1) * ATT_STRIP)
                    s = s_ref[e, rows, :] + bias_ref[u, e, rows, :]
                    p = jnp.exp(s - jnp.max(s, axis=-1, keepdims=True))
                    inv_ref[e, rows, :] = jnp.broadcast_to(1.0 / jnp.sum(p, axis=-1, keepdims=True), (ATT_STRIP, LANE))
                    p_ref[e, rows, :] = p.astype(BF16)
                outs.append(_dot(p_ref[e], vw) * inv_ref[e])
            o_ref[u * ATT_Q:(u + 1) * ATT_Q, :] = jnp.where(lane < ATT_HEAD_DIM, outs[0], outs[1]).astype(o_ref.dtype)

    return pl.pallas_call(
        body,
        name=name,
        grid=(ATT_HEADS // 2, nb),
        in_specs=_attn_in_specs(nb),
        out_specs=pl.BlockSpec((T, LANE), lambda h, b: (b, h)),
        out_shape=jax.ShapeDtypeStruct((S, D_MODEL), BF16),
        scratch_shapes=[pltpu.VMEM((2 * T, LANE), BF16), pltpu.VMEM((2 * T, LANE), BF16),
                        pltpu.VMEM((T // ATT_Q, 2, ATT_Q, ATT_W), F32), pltpu.VMEM((2, ATT_Q, ATT_W), F32),
                        pltpu.VMEM((2, ATT_Q, ATT_W), BF16), pltpu.VMEM((2, ATT_Q, LANE), F32)],
        compiler_params=_cp(("parallel", "arbitrary")),
    )(qkv, qkv, qkv, qkv, qkv, bias)


def _attn_bwd(name, qkv, bias, do):
    S = qkv.shape[0]
    T = ATT_BLOCK
    nb = S // T

    def body(q_ref, kp_ref, kc_ref, vp_ref, vc_ref, band_ref, do_ref,
             dq_ref, dk_ref, dv_ref, dband_ref, kw_ref, vw_ref, dkw_ref, dvw_ref, bias_ref, dbias_ref):
        b = pl.program_id(1)

        _spread_bias(band_ref, bias_ref, b)

        @pl.when(b == 0)
        def _():
            dbias_ref[...] = jnp.zeros_like(dbias_ref)
            dkw_ref[T:2 * T] = jnp.zeros((T, LANE), F32)
            dvw_ref[T:2 * T] = jnp.zeros((T, LANE), F32)

        dkw_ref[0:T] = dkw_ref[T:2 * T]
        dvw_ref[0:T] = dvw_ref[T:2 * T]
        dkw_ref[T:2 * T] = jnp.zeros((T, LANE), F32)
        dvw_ref[T:2 * T] = jnp.zeros((T, LANE), F32)

        @pl.when(b < nb)
        def _():
            kw_ref[0:T] = kp_ref[...]
            kw_ref[T:2 * T] = kc_ref[...]
            vw_ref[0:T] = vp_ref[...]
            vw_ref[T:2 * T] = vc_ref[...]
            lane = lax.broadcasted_iota(jnp.int32, (ATT_Q, LANE), 1)
            for u in range(T // ATT_Q):
                rows = slice(u * ATT_Q, (u + 1) * ATT_Q)
                win = slice(u * ATT_Q, u * ATT_Q + ATT_W)
                vw = vw_ref[win, :]
                do2 = do_ref[rows, :]
                dqs, dk, dv = [], None, None
                for e in range(2):
                    p, inv, qm, kw, mine = _attn_unit(q_ref, kw_ref, bias_ref, e, u, lane)
                    dom = jnp.where(mine, do2, 0)
                    dp = _dot_nt(dom, vw)
                    delta = jnp.sum(p * dp, axis=-1, keepdims=True) * inv
                    ds = p * ((dp - delta) * inv)
                    dbias_ref[e] += ds
                    dsb = ds.astype(BF16)
                    dqs.append(_dot(dsb, kw))
                    dk_e = _dot_tn(dsb, qm)
                    dv_e = _dot_tn((p * inv).astype(BF16), dom)
                    dk = dk_e if dk is None else dk + dk_e
                    dv = dv_e if dv is None else dv + dv_e
                dq_ref[rows, :] = (jnp.where(lane < ATT_HEAD_DIM, dqs[0], dqs[1]) * QSCALE).astype(dq_ref.dtype)
                dkw_ref[win, :] += dk
                dvw_ref[win, :] += dv

        @pl.when(b > 0)
        def _():
            dk_ref[...] = dkw_ref[0:T].astype(dk_ref.dtype)
            dv_ref[...] = dvw_ref[0:T].astype(dv_ref.dtype)

        @pl.when(b == nb)
        def _():
            for e in range(2):
                acc = dbias_ref[e, 0:CHUNK, :]
                for j in range(1, ATT_Q // CHUNK):
                    acc = acc + pltpu.roll(dbias_ref[e, j * CHUNK:(j + 1) * CHUNK, :], ATT_W - j * CHUNK, 1)
                dband_ref[e] = acc

    tok = jax.ShapeDtypeStruct((S, D_MODEL), BF16)
    prev_out = pl.BlockSpec((T, LANE), lambda h, b: (jnp.maximum(b - 1, 0), h))
    return pl.pallas_call(
        body,
        name=name,
        grid=(ATT_HEADS // 2, nb + 1),
        in_specs=_attn_in_specs(nb) + [pl.BlockSpec((T, LANE), lambda h, b: (jnp.minimum(b, nb - 1), h))],
        out_specs=[pl.BlockSpec((T, LANE), lambda h, b: (jnp.minimum(b, nb - 1), h)), prev_out, prev_out,
                   pl.BlockSpec((None, 2, CHUNK, ATT_W), lambda h, b: (h, 0, 0, 0))],
        out_shape=[tok, tok, tok, jax.ShapeDtypeStruct((ATT_HEADS // 2, 2, CHUNK, ATT_W), F32)],
        scratch_shapes=[pltpu.VMEM((2 * T, LANE), BF16), pltpu.VMEM((2 * T, LANE), BF16),
                        pltpu.VMEM((2 * T, LANE), F32), pltpu.VMEM((2 * T, LANE), F32),
                        pltpu.VMEM((T // ATT_Q, 2, ATT_Q, ATT_W), F32), pltpu.VMEM((2, ATT_Q, ATT_W), F32)],
        compiler_params=_cp(("parallel", "arbitrary")),
    )(qkv, qkv, qkv, qkv, qkv, bias, do)


def _local_step(x, target, small, W):
    S = x.shape[0]
    tb = _ret_tables(S)
    rel_fwd, rel_bwd = _rel_index()
    saved = []
    normed = (("tile", BF16),)
    deep = dict(tm=512, tk=D_FF)
    h = _rms_fwd("mix_norm_fwd0", x, small["mix_norm"][0:1])
    for layer in range(DEPTH):
        i = layer // 2
        st = {"x_in": x, "h": h}
        g_ffn = small["ffn_norm"][layer:layer + 1]
        if layer % 2 == 0:
            z = W.mm(f"ab_in_fwd{layer}", "nn", h, W.get("ab_w_in", i), tn=640, out_dtype=F32)
            gain = small["ab_gn_gain"][i:i + 1]
            cat, opre, states = _ret_fwd(f"ret_fwd{layer}", z, tb, gain)
            cat, pooled = _pool_fwd(f"pool_fwd{layer}", z, small["ab_w_pool"][i], small["ab_pool_scale"][i:i + 1], cat)
            st.update(z=z, opre=opre, states=states, pooled=pooled, cat=cat)
            x, hn = W.mm(f"ab_out_fwd{layer}", "nn", cat, W.get("ab_w_out", i), extras=(x,), aux=(g_ffn,), sides=normed,
                         epi=_epi_residual_norm)
        else:
            qkv = W.mm(f"qkv_fwd{layer}", "nn", h, W.get("c_w_qkv", i), out_dtype=BF16)
            bias = _bias_table(f"bias_table{layer}", small["c_rel_bias"][i], rel_fwd)
            bias = bias.reshape(ATT_HEADS // 2, 2, CHUNK, ATT_W)
            att = _attn_fwd(f"attn_fwd{layer}", qkv, bias)
            st.update(qkv=qkv, bias=bias, att=att)
            x, hn = W.mm(f"c_out_fwd{layer}", "nn", att, W.get("c_w_out", i), extras=(x,), aux=(g_ffn,), sides=normed,
                         epi=_epi_residual_norm)
        st["x_mid"] = x
        u = W.mm(f"ffn_in_fwd{layer}", "nn", hn, W.get("w_ffn_in", layer), out_dtype=BF16)
        if layer + 1 < DEPTH:
            x, h = W.mm(f"ffn_out_fwd{layer}", "nn", u, W.get("w_ffn_out", layer), a_fn=_relu2, extras=(x,),
                        aux=(small["mix_norm"][layer + 1:layer + 2],), sides=normed, epi=_epi_residual_norm, **deep)
        else:
            x = W.mm(f"ffn_out_fwd{layer}", "nn", u, W.get("w_ffn_out", layer), a_fn=_relu2, extras=(x,),
                     epi=_epi_residual, **deep)
        st.update(hn=hn, u=u)
        saved.append(st)

    loss, dx, d_final = _loss_head(x, small["final_norm"].reshape(1, D_MODEL), target)

    gs = {k: [None] * v.shape[0] for k, v in small.items() if k != "final_norm"}
    gb = {k: None for k in W.n_layers}
    landed = {k: None for k in W.n_layers}
    pending = []

    def host(name, *args, take=1, **kw):
        items = [pending.pop(0) for _ in range(min(take, len(pending)))]
        if not items:
            return _mm(name, *args, **kw)
        riders = [_grad_rider(key, idx, gb[key], landed[key]) for key, idx in items]
        res, outs = _mm(name, *args, rider=_join_riders(riders), **kw)
        for (key, _), out in zip(items, outs):
            landed[key] = out
        return res

    def dw(name, key, idx, a, b, call=_mm, **kw):
        gb[key] = call(name, "tn", a, b, stack=(W.n_layers[key], idx, gb[key]), out_dtype=BF16, **kw)
        pending.append((key, idx))

    gain_sums = (("colsum", F32),)
    for layer in reversed(range(DEPTH)):
        i = layer // 2
        st = saved[layer]
        du = host(f"ffn_out_bwd{layer}", "nt", dx, W.get("w_ffn_out", layer), extras=(st["u"],),
                  epi=lambda acc, u: acc * (2.0 * jnp.maximum(u, 0).astype(F32)), out_dtype=BF16)
        dw(f"ffn_out_dw{layer}", "w_ffn_out", layer, st["u"], dx, a_fn=_relu2)
        dx, dgain = host(f"ffn_in_bwd{layer}", "nt", du, W.get("w_ffn_in", layer), extras=(st["x_mid"], dx),
                         aux=(small["ffn_norm"][layer:layer + 1],), sides=gain_sums, epi=_epi_rms_bwd, **deep)
        gs["ffn_norm"][layer] = dgain[0:1]
        dw(f"ffn_in_dw{layer}", "w_ffn_in", layer, st["hn"], du, tk=2048)
        norm_bwd = dict(extras=(st["x_in"], dx), aux=(small["mix_norm"][layer:layer + 1],), sides=gain_sums,
                        epi=_epi_rms_bwd)
        if layer % 2 == 0:
            dcat = _mm(f"ab_out_bwd{layer}", "nt", dx, W.get("ab_w_out", i), out_dtype=BF16)
            dw(f"ab_out_dw{layer}", "ab_w_out", i, st["cat"], dx)
            gain = small["ab_gn_gain"][i:i + 1]
            dz, gs["ab_gn_gain"][i] = _ret_bwd(f"ret_bwd{layer}", st["z"], tb, gain, st["opre"], st["states"], dcat)
            dz, gs["ab_w_pool"][i], gs["ab_pool_scale"][i] = _pool_bwd(
                f"pool_bwd{layer}", st["pooled"], small["ab_w_pool"][i], small["ab_pool_scale"][i:i + 1], dcat, dz)
            if layer == 0:
                dw(f"ab_in_dw{layer}", "ab_w_in", i, st["h"], dz, call=host, tn=640, tk=2048)
            dx, dgain = host(f"ab_in_bwd{layer}", "nt", dz, W.get("ab_w_in", i), tm=512, tk=AB_IN_WIDTH,
                             take=len(pending) if layer == 0 else 1, **norm_bwd)
            if layer > 0:
                dw(f"ab_in_dw{layer}", "ab_w_in", i, st["h"], dz, call=host, tn=640, tk=2048)
        else:
            datt = _mm(f"c_out_bwd{layer}", "nt", dx, W.get("c_w_out", i), out_dtype=BF16)
            dw(f"c_out_dw{layer}", "c_w_out", i, st["att"], dx)
            dq, dk, dv, dbias = _attn_bwd(f"attn_bwd{layer}", st["qkv"], st["bias"], datt)
            gs["c_rel_bias"][i] = _bias_grad(f"bias_grad{layer}", dbias.reshape(ATT_HEADS, CHUNK, ATT_W), rel_bwd)
            dqkv = [dq, dk, dv]
            dx, dgain = host(f"qkv_bwd{layer}", "nt", dqkv, W.get("c_w_qkv", i), tm=512, tk=3 * D_MODEL, **norm_bwd)
            dw(f"qkv_dw{layer}", "c_w_qkv", i, st["h"], dqkv, call=host, tk=2048)
        gs["mix_norm"][layer] = dgain[0:1]
    for key, idx in pending:
        landed[key], = _run_rider(f"grad_exchange_{key}{idx}", _grad_rider(key, idx, gb[key], landed[key]))

    g_small = {
        "mix_norm": jnp.concatenate(gs["mix_norm"], axis=0),
        "ffn_norm": jnp.concatenate(gs["ffn_norm"], axis=0),
        "ab_gn_gain": jnp.concatenate(gs["ab_gn_gain"], axis=0),
        "ab_w_pool": jnp.stack(gs["ab_w_pool"], axis=0),
        "ab_pool_scale": jnp.concatenate(gs["ab_pool_scale"], axis=0),
        "c_rel_bias": jnp.stack(gs["c_rel_bias"], axis=0),
        "final_norm": d_final.reshape(D_MODEL),
    }
    return loss, dx, g_small, gb, landed


_BIG = ("w_ffn_in", "w_ffn_out", "ab_w_in", "ab_w_out", "c_w_qkv", "c_w_out")
_SHARD_AXIS = {"w_ffn_in": 2, "w_ffn_out": 1, "ab_w_in": 2, "ab_w_out": 1, "c_w_qkv": 2, "c_w_out": 1}
_SMALL = ("mix_norm", "ffn_norm", "ab_gn_gain", "ab_w_pool", "ab_pool_scale", "c_rel_bias", "final_norm")


def _place():
    x, y, c = lax.axis_index("x"), lax.axis_index("y"), lax.axis_index("c")
    chips = [(1 - x, y), (x, 1 - y), (1 - x, 1 - y)]
    return x, y, c, chips


def _sub(ref, axis, start, size):
    idx = [slice(None)] * len(ref.shape)
    idx[axis] = pl.ds(pl.multiple_of(start, LANE), size)
    return ref.at[tuple(idx)]


def _gather_rider(items, shards):
    keys = sorted({k for k, _ in items})
    n = len(items)
    axes = [_SHARD_AXIS[k] - 1 for k, _ in items]
    sizes = [shards[k].shape[a + 1] for (k, _), a in zip(items, axes)]
    hsizes = [shards[k].shape[2 - a] // 2 for (k, _), a in zip(items, axes)]

    def views(ins, outs, send_sems, recv_sems):
        x, y, c, chips = _place()
        srcs = [ins[keys.index(k)].at[l] for k, l in items]

        def remote(src, dst, s, to):
            return pltpu.make_async_remote_copy(src_ref=src, dst_ref=dst, send_sem=send_sems.at[s],
                                                recv_sem=recv_sems.at[s], device_id=to, device_id_type=MESH)

        def half(w, chip, core):
            return _sub(_sub(outs[w], axes[w], chip * sizes[w], sizes[w]), 1 - axes[w], core * hsizes[w], hsizes[w])

        me = 2 * x + y
        local = [pltpu.make_async_copy(srcs[w], _sub(outs[w], axes[w], me * sizes[w], sizes[w]), send_sems.at[6 * n + w])
                 for w in range(n)]
        first = [remote(_sub(srcs[w], 1 - axes[w], c * hsizes[w], hsizes[w]), half(w, me, c), w * 6 + k, (px, py, c))
                 for w in range(n) for k, (px, py) in enumerate(chips)]
        return x, y, c, chips, remote, half, local, first

    def start(ins, outs, send_sems, recv_sems):
        *_, local, first = views(ins, outs, send_sems, recv_sems)
        for cp in local + first:
            cp.start()

    def passes(x, y, c, chips, remote, half):
        return [remote(half(w, 2 * px + py, c), half(w, 2 * px + py, c), w * 6 + 3 + k, (x, y, 1 - c))
                for w in range(n) for k, (px, py) in enumerate(chips)]

    def relay(ins, outs, send_sems, recv_sems):
        x, y, c, chips, remote, half, _, _ = views(ins, outs, send_sems, recv_sems)
        for w in range(n):
            for k, (px, py) in enumerate(chips):
                landed = half(w, 2 * px + py, c)
                remote(landed, landed, w * 6 + k, (px, py, c)).wait_recv()
        for cp in passes(x, y, c, chips, remote, half):
            cp.start()

    def finish(ins, outs, send_sems, recv_sems):
        x, y, c, chips, remote, half, local, first = views(ins, outs, send_sems, recv_sems)
        for w in range(n):
            for k, (px, py) in enumerate(chips):
                theirs = half(w, 2 * px + py, 1 - c)
                remote(theirs, theirs, w * 6 + 3 + k, (x, y, 1 - c)).wait_recv()
        for cp in first + passes(x, y, c, chips, remote, half):
            cp.wait_send()
        for cp in local:
            cp.wait()

    def full(k, a):
        shape = list(shards[k].shape[1:])
        shape[a] *= N_CHIPS
        return jax.ShapeDtypeStruct(tuple(shape), shards[k].dtype)

    return _Rider(tuple(shards[k] for k in keys), tuple(full(k, a) for (k, _), a in zip(items, axes)), 7 * n, start, finish,
                  relay=relay)


def _mixer_items(layer):
    names = ("ab_w_in", "ab_w_out") if layer % 2 == 0 else ("c_w_qkv", "c_w_out")
    return [(k, layer // 2) for k in names]


class _Weights:
    def __init__(self, shards):
        self.shards = shards
        self.n_layers = {k: shards[k].shape[0] for k in _BIG}
        self.full = {}
        first, second = _mixer_items(0)
        self._take([first], _run_rider("gather_first", _gather_rider([first], shards)))
        self.plan = {"ab_in_fwd0": [second, ("w_ffn_in", 0)], "ab_out_fwd0": [("w_ffn_out", 0)]}
        for layer in range(1, DEPTH):
            proj = "ab_in" if layer % 2 == 0 else "qkv"
            self.plan[f"ffn_in_fwd{layer - 1}"] = _mixer_items(layer)
            self.plan[f"ffn_out_fwd{layer - 1}"] = [("w_ffn_in", layer)]
            self.plan[f"{proj}_fwd{layer}"] = [("w_ffn_out", layer)]

    def _take(self, items, outs):
        self.full.update(zip(items, outs))

    def get(self, name, layer):
        return self.full[(name, layer)]

    def mm(self, name, *args, **kw):
        items = self.plan.get(name)
        if items is None:
            return _mm(name, *args, **kw)
        res, outs = _mm(name, *args, rider=_gather_rider(items, self.shards), **kw)
        self._take(items, outs)
        return res


def _run_rider(name, rider):
    n_in, n_out = len(rider.operands), len(rider.out_shapes)

    def body(*refs):
        ins, outs, sems = refs[:n_in], refs[n_in:n_in + n_out], refs[n_in + n_out:]
        rider.start(ins, outs, *sems)
        if rider.relay is not None:
            rider.relay(ins, outs, *sems)
        rider.finish(ins, outs, *sems)

    return pl.pallas_call(
        body,
        name=name,
        in_specs=[_ANY] * n_in,
        out_specs=[_ANY] * n_out,
        out_shape=list(rider.out_shapes),
        input_output_aliases=dict(rider.aliases),
        scratch_shapes=[pltpu.SemaphoreType.DMA((rider.n_sems,)), pltpu.SemaphoreType.DMA((rider.n_sems,))],
        compiler_params=pltpu.CompilerParams(has_side_effects=True),
    )(*rider.operands)


def _grad_rider(name, layer, grad, landing):
    axis = _SHARD_AXIS[name] - 1
    L, R, C = grad.shape
    shard = (R // N_CHIPS, C) if axis == 0 else (R, C // N_CHIPS)
    size = shard[axis]

    def copies(ins, outs, send_sems, recv_sems):
        x, y, c, chips = _place()
        return [pltpu.make_async_remote_copy(
            src_ref=_sub(ins[0].at[layer], axis, (2 * px + py) * size, size), dst_ref=outs[0].at[layer, k],
            send_sem=send_sems.at[k], recv_sem=recv_sems.at[k], device_id=(px, py, c), device_id_type=MESH)
            for k, (px, py) in enumerate(chips)]

    def start(ins, outs, send_sems, recv_sems):
        for cp in copies(ins, outs, send_sems, recv_sems):
            cp.start()

    def finish(ins, outs, send_sems, recv_sems):
        cps = copies(ins, outs, send_sems, recv_sems)
        for cp in cps:
            cp.wait_recv()
        for cp in cps:
            cp.wait_send()

    out = jax.ShapeDtypeStruct((L, 3) + shard, grad.dtype)
    if landing is None:
        return _Rider((grad,), (out,), 3, start, finish)
    return _Rider((grad, landing), (out,), 3, start, finish, aliases=((1, 0),))


def _join_riders(riders):
    if len(riders) == 1:
        return riders[0]

    def parts(ins, outs, send_sems, recv_sems):
        i0 = o0 = s0 = 0
        for r in riders:
            ni, no = len(r.operands), len(r.out_shapes)
            yield (r, ins[i0:i0 + ni], outs[o0:o0 + no], send_sems.at[pl.ds(s0, r.n_sems)],
                   recv_sems.at[pl.ds(s0, r.n_sems)])
            i0, o0, s0 = i0 + ni, o0 + no, s0 + r.n_sems

    def start(*refs):
        for r, *own in parts(*refs):
            r.start(*own)

    def finish(*refs):
        for r, *own in parts(*refs):
            r.finish(*own)

    aliases, i0, o0 = [], 0, 0
    for r in riders:
        aliases += [(i0 + src, o0 + dst) for src, dst in r.aliases]
        i0, o0 = i0 + len(r.operands), o0 + len(r.out_shapes)
    return _Rider(tuple(x for r in riders for x in r.operands), tuple(x for r in riders for x in r.out_shapes),
                  sum(r.n_sems for r in riders), start, finish, tuple(aliases))


def _pair_swap(sums):
    n = len(sums)

    def body(*refs):
        ins, outs = refs[:n], refs[n:2 * n]
        send_sems, recv_sems = refs[2 * n:]
        x, y, c, _ = _place()
        cps = [pltpu.make_async_remote_copy(src_ref=ins[w], dst_ref=outs[w], send_sem=send_sems.at[w],
                                            recv_sem=recv_sems.at[w], device_id=(x, y, 1 - c), device_id_type=MESH)
               for w in range(n)]
        for cp in cps:
            cp.start()
        for cp in cps:
            cp.wait_recv()
        for cp in cps:
            cp.wait_send()

    return pl.pallas_call(
        body,
        name="pair_swap",
        in_specs=[_ANY] * n,
        out_specs=[_ANY] * n,
        out_shape=[jax.ShapeDtypeStruct(s.shape, s.dtype) for s in sums],
        scratch_shapes=[pltpu.SemaphoreType.DMA((n,)), pltpu.SemaphoreType.DMA((n,))],
        compiler_params=pltpu.CompilerParams(has_side_effects=True),
    )(*sums)


def _rows_tile(rows, cols):
    tr = rows
    while tr * cols > (1 << 19) and tr % 16 == 0:
        tr //= 2
    return tr


def _chip_sum(name, grad, landed, chip, saxis):
    L = grad.shape[0]
    _, _, R, C = landed.shape
    tr = _rows_tile(R, C)
    nr = R // tr
    if saxis == 2:
        g_idx = lambda l, i, s: (l, i, s[0])
    else:
        g_idx = lambda l, i, s: (l, s[0] * nr + i, 0)

    def body(s_ref, g_ref, l_ref, o_ref):
        tot = ((g_ref[...].astype(F32) + l_ref[0].astype(F32)) + l_ref[1].astype(F32)) + l_ref[2].astype(F32)
        o_ref[...] = tot.astype(o_ref.dtype)

    return pl.pallas_call(
        body,
        name=name,
        grid_spec=pltpu.PrefetchScalarGridSpec(
            num_scalar_prefetch=1,
            grid=(L, nr),
            in_specs=[pl.BlockSpec((None, tr, C), g_idx), pl.BlockSpec((None, 3, tr, C), lambda l, i, s: (l, 0, i, 0))],
            out_specs=pl.BlockSpec((None, tr, C), lambda l, i, s: (l, i, 0)),
        ),
        out_shape=jax.ShapeDtypeStruct((L, R, C), BF16),
        compiler_params=_cp(("parallel", "parallel")),
    )(chip, grad, landed)


def _all_reduce_small(packed):
    R = packed.shape[0]

    def body(p_ref, o_ref, land_ref, send_sems, recv_sems):
        x, y, c, _ = _place()
        me = 4 * x + 2 * y + c
        sends, recvs = [], []
        for r in range(1, N_DEV):
            px, py, pc = x ^ (r >> 2), y ^ ((r >> 1) & 1), c ^ (r & 1)
            cp = pltpu.make_async_remote_copy(src_ref=p_ref, dst_ref=land_ref.at[me], send_sem=send_sems.at[r - 1],
                                              recv_sem=recv_sems.at[r - 1], device_id=(px, py, pc), device_id_type=MESH)
            cp.start()
            sends.append(cp)
            recvs.append(pltpu.make_async_remote_copy(src_ref=p_ref, dst_ref=land_ref.at[4 * px + 2 * py + pc],
                                                      send_sem=send_sems.at[r - 1], recv_sem=recv_sems.at[r - 1],
                                                      device_id=(px, py, pc), device_id_type=MESH))
        land_ref[me] = p_ref[...]
        for cp in recvs:
            cp.wait_recv()
        for cp in sends:
            cp.wait_send()
        acc = land_ref[0]
        for d in range(1, N_DEV):
            acc = acc + land_ref[d]
        o_ref[...] = acc

    vm = pl.BlockSpec(memory_space=pltpu.VMEM)
    return pl.pallas_call(
        body,
        name="all_reduce_small",
        in_specs=[vm],
        out_specs=vm,
        out_shape=jax.ShapeDtypeStruct((R, LANE), F32),
        scratch_shapes=[pltpu.VMEM((N_DEV, R, LANE), F32), pltpu.SemaphoreType.DMA((N_DEV - 1,)),
                        pltpu.SemaphoreType.DMA((N_DEV - 1,))],
        compiler_params=pltpu.CompilerParams(has_side_effects=True, vmem_limit_bytes=VMEM_LIMIT),
    )(packed)


def _adamw(name, w, m, v, grads):
    R, C = w.shape
    tr = _rows_tile(R, C)
    c1 = 1.0 - ADAM_B1 ** ADAM_STEP
    c2 = 1.0 - ADAM_B2 ** ADAM_STEP
    ng = len(grads)

    def body(*refs):
        w_ref, m_ref, v_ref = refs[:3]
        g_refs = refs[3:3 + ng]
        g_ref, d_ref, nm_ref, nv_ref = refs[3 + ng:]
        gv = g_refs[0][...].astype(F32)
        for r in g_refs[1:]:
            gv = gv + r[...].astype(F32)
        g_ref[...] = gv
        nm = ADAM_B1 * m_ref[...] + (1.0 - ADAM_B1) * gv
        nv = ADAM_B2 * v_ref[...] + (1.0 - ADAM_B2) * (gv * gv)
        nm_ref[...] = nm
        nv_ref[...] = nv
        d_ref[...] = -ADAM_LR * ((nm / c1) / (jnp.sqrt(nv / c2) + ADAM_EPS) + ADAM_WD * w_ref[...])

    blk = pl.BlockSpec((tr, C), lambda i: (i, 0))
    out = jax.ShapeDtypeStruct((R, C), F32)
    return pl.pallas_call(
        body,
        name=name,
        grid=(R // tr,),
        in_specs=[blk] * (3 + ng),
        out_specs=[blk] * 4,
        out_shape=[out] * 4,
        compiler_params=_cp(("parallel",)),
    )(w, m, v, *grads)


def _pack(parts):
    rows = []
    for p in parts:
        flat = p.reshape(-1).astype(F32)
        n = -(-flat.shape[0] // (8 * LANE)) * (8 * LANE)
        rows.append(jnp.pad(flat, (0, n - flat.shape[0])).reshape(n // LANE, LANE))
    return jnp.concatenate(rows, axis=0)


def _unpack(packed, like):
    out, r = [], 0
    for p in like:
        size = int(np.prod(p.shape))
        n = -(-size // (8 * LANE)) * 8
        out.append(packed[r:r + n].reshape(-1)[:size].reshape(p.shape))
        r += n
    return out


def kernel(x, mix_norm, ffn_norm, w_ffn_in, w_ffn_out, ab_w_in, ab_gn_gain, ab_w_pool, ab_pool_scale, ab_w_out, c_w_qkv, c_rel_bias, c_w_out, final_norm, loss_target, m_mix_norm, m_ffn_norm, m_w_ffn_in, m_w_ffn_out, m_ab_w_in, m_ab_gn_gain, m_ab_w_pool, m_ab_pool_scale, m_ab_w_out, m_c_w_qkv, m_c_rel_bias, m_c_w_out, m_final_norm, v_mix_norm, v_ffn_norm, v_w_ffn_in, v_w_ffn_out, v_ab_w_in, v_ab_gn_gain, v_ab_w_pool, v_ab_pool_scale, v_ab_w_out, v_c_w_qkv, v_c_rel_bias, v_c_w_out, v_final_norm):
    w = dict(mix_norm=mix_norm, ffn_norm=ffn_norm, w_ffn_in=w_ffn_in, w_ffn_out=w_ffn_out, ab_w_in=ab_w_in,
             ab_gn_gain=ab_gn_gain, ab_w_pool=ab_w_pool, ab_pool_scale=ab_pool_scale, ab_w_out=ab_w_out,
             c_w_qkv=c_w_qkv, c_rel_bias=c_rel_bias, c_w_out=c_w_out, final_norm=final_norm)
    m = dict(mix_norm=m_mix_norm, ffn_norm=m_ffn_norm, w_ffn_in=m_w_ffn_in, w_ffn_out=m_w_ffn_out, ab_w_in=m_ab_w_in,
             ab_gn_gain=m_ab_gn_gain, ab_w_pool=m_ab_w_pool, ab_pool_scale=m_ab_pool_scale, ab_w_out=m_ab_w_out,
             c_w_qkv=m_c_w_qkv, c_rel_bias=m_c_rel_bias, c_w_out=m_c_w_out, final_norm=m_final_norm)
    v = dict(mix_norm=v_mix_norm, ffn_norm=v_ffn_norm, w_ffn_in=v_w_ffn_in, w_ffn_out=v_w_ffn_out, ab_w_in=v_ab_w_in,
             ab_gn_gain=v_ab_gn_gain, ab_w_pool=v_ab_w_pool, ab_pool_scale=v_ab_pool_scale, ab_w_out=v_ab_w_out,
             c_w_qkv=v_c_w_qkv, c_rel_bias=v_c_rel_bias, c_w_out=v_c_w_out, final_norm=v_final_norm)
    S = x.shape[1]
    cx, cy, cc = lax.axis_index("x"), lax.axis_index("y"), lax.axis_index("c")
    chip = jnp.reshape(2 * cx + cy, (1,)).astype(jnp.int32)

    big = _Weights({k: w[k].astype(BF16) for k in _BIG})
    small = {k: w[k] for k in _SMALL}
    loss, grad_x, g_small, g_big, landed = _local_step(x.reshape(S, D_MODEL), loss_target.reshape(S, D_MODEL), small, big)

    sums = [_chip_sum(f"chip_sum_{k}", g_big[k], landed[k], chip, _SHARD_AXIS[k]) for k in _BIG]
    siblings = _pair_swap(sums)

    packed = _all_reduce_small(_pack([g_small[k] for k in _SMALL] + [loss]))
    small_like = [w[k] for k in _SMALL]
    g_red = dict(zip(_SMALL, _unpack(packed, small_like)))
    loss_row = packed.shape[0] - 8
    loss_out = packed[loss_row, 0]

    grad, delta, new_m, new_v = {}, {}, {}, {}
    for k, mine, theirs in zip(_BIG, sums, siblings):
        shp = w[k].shape
        two = (shp[0] * shp[1], shp[2])
        outs = _adamw(f"adamw_{k}", w[k].reshape(two), m[k].reshape(two), v[k].reshape(two),
                      (mine.reshape(two), theirs.reshape(two)))
        grad[k], delta[k], new_m[k], new_v[k] = [o.reshape(shp) for o in outs]
    _, d, nm, nv = _adamw("adamw_small", _pack(small_like), _pack([m[k] for k in _SMALL]), _pack([v[k] for k in _SMALL]),
                          (packed[:loss_row],))
    for k, dk, mk, vk in zip(_SMALL, _unpack(d, small_like), _unpack(nm, small_like), _unpack(nv, small_like)):
        grad[k], delta[k], new_m[k], new_v[k] = g_red[k], dk, mk, vk

    order = ("mix_norm", "ffn_norm", "w_ffn_in", "w_ffn_out", "ab_w_in", "ab_gn_gain", "ab_w_pool", "ab_pool_scale",
             "ab_w_out", "c_w_qkv", "c_rel_bias", "c_w_out", "final_norm")
    return (loss_out, grad_x.reshape(x.shape), *[grad[k] for k in order], *[delta[k] for k in order],
            *[new_m[k] for k in order], *[new_v[k] for k in order])
```

```python
import functools
from typing import Callable, NamedTuple

import numpy as np
import jax
import jax.numpy as jnp
from jax import lax
from jax.experimental import pallas as pl
from jax.experimental.pallas import tpu as pltpu

F32 = jnp.float32
BF16 = jnp.bfloat16

D_MODEL = 1024
D_FF = 4096
DEPTH = 4
CHUNK = 64
RMS_EPS = 1e-6
RET_WIDTH = 512
RET_HEADS = 4
RET_HEAD_DIM = 128
RET_ROPE_BASE = 10000.0
GN_EPS = 1e-5
POOL_WIDTH = 512
POOL_WINDOWS = (2, 4, 8, 16)
POOL_GROUP_DIM = 128
POOL_HALO = 16
AB_IN_WIDTH = 2560
ATT_HEADS = 16
ATT_HEAD_DIM = 64
LEFT_CHUNKS = 8
BAND = (LEFT_CHUNKS + 1) * CHUNK
REL_CLIP = 128
N_REL = 2 * REL_CLIP + 1
N_REL_PAD = 264
NEG_INF = -1e30
KSCALE = RET_HEAD_DIM ** -0.5
QSCALE = ATT_HEAD_DIM ** -0.5

ADAM_LR = 0.001
ADAM_B1 = 0.9
ADAM_B2 = 0.999
ADAM_EPS = 1e-08
ADAM_WD = 0.01
ADAM_STEP = 10

ATT_BLOCK = LEFT_CHUNKS * CHUNK
RET_BLOCK = 512
N_CHIPS = 4
N_DEV = 8
LANE = 128
VMEM_LIMIT = 52 * 1024 * 1024
EPI_ROWS = 256
MESH = pl.DeviceIdType.MESH


def _cp(sem, vmem=VMEM_LIMIT):
    return pltpu.CompilerParams(dimension_semantics=sem, vmem_limit_bytes=vmem)


def _dot(a, b):
    return lax.dot_general(a, b, (((1,), (0,)), ((), ())), preferred_element_type=F32)


def _dot_nt(a, b):
    return lax.dot_general(a, b, (((1,), (1,)), ((), ())), preferred_element_type=F32)


def _dot_tn(a, b):
    return lax.dot_general(a, b, (((0,), (0,)), ((), ())), preferred_element_type=F32)


_ANY = pl.BlockSpec(memory_space=pl.ANY)


class _Rider(NamedTuple):
    operands: tuple
    out_shapes: tuple
    n_sems: int
    start: Callable
    finish: Callable
    aliases: tuple = ()
    relay: Callable = None


def _mm(name, mode, a, b, *, la=None, lb=None, tm=1024, tn=1024, tk=1024, a_fn=None, b_fn=None,
        extras=(), aux=(), sides=(), epi=None, out_dtype=F32, stack=None, rider=None):
    a_parts = list(a) if isinstance(a, (list, tuple)) else [a]
    b_parts = list(b) if isinstance(b, (list, tuple)) else [b]
    na, nbp = len(a_parts), len(b_parts)
    a2, b2 = list(a_parts[0].shape[-2:]), list(b_parts[0].shape[-2:])
    a2[1] *= na
    b2[1] *= nbp
    if mode == "nn":
        (M, K), (K2, N) = a2, b2
    elif mode == "nt":
        (M, K), (N, K2) = a2, b2
    else:
        (K, M), (K2, N) = a2, b2
    assert K == K2, (name, a2, b2)
    tm, tn, tk = min(tm, M), min(tn, N), min(tk, K)
    assert M % tm == 0 and N % tn == 0 and K % tk == 0, (name, M, N, K, tm, tn, tk)
    gm, gn, gk = M // tm, N // tn, K // tk
    fold = mode == "nt" and na > 1 and gk == 1

    def specs(parts, block, idx, lead):
        per = parts[0].shape[-1] // block[1]
        assert parts[0].shape[-1] % block[1] == 0, (name, parts[0].shape, block)
        out = []
        for p in range(len(parts)):
            def f(i, j, k, p=p):
                r, c = idx(i, j, k)
                if len(parts) > 1:
                    c = jnp.clip(c - p * per, 0, per - 1)
                return (r, c) if lead is None else (lead, r, c)
            out.append(pl.BlockSpec(block if lead is None else (None,) + block, f))
        return out, per

    if mode == "nn":
        a_specs, a_per = specs(a_parts, (tm, tk), lambda i, j, k: (i, k), la)
        b_specs, b_per = specs(b_parts, (tk, tn), lambda i, j, k: (k, j), lb)
        a_axis, b_axis, dot = 2, 1, _dot
    elif mode == "nt":
        if fold:
            a_specs, a_per = [pl.BlockSpec((tm, K // na), lambda i, j, k: (i, 0)) for _ in a_parts], 1
        else:
            a_specs, a_per = specs(a_parts, (tm, tk), lambda i, j, k: (i, k), la)
        b_specs, b_per = specs(b_parts, (tn, tk), lambda i, j, k: (j, k), lb)
        a_axis, b_axis, dot = 2, 2, _dot_nt
    else:
        a_specs, a_per = specs(a_parts, (tk, tm), lambda i, j, k: (k, i), la)
        b_specs, b_per = specs(b_parts, (tk, tn), lambda i, j, k: (k, j), lb)
        a_axis, b_axis, dot = 0, 1, _dot_tn
    ex_specs = [pl.BlockSpec((tm, tn), lambda i, j, k: (i, j)) for _ in extras]
    n_ex = len(extras)

    n_aux, n_side = len(aux), len(sides)
    operands = a_parts + b_parts + list(extras) + list(aux)
    in_specs = a_specs + b_specs + ex_specs + [pl.BlockSpec(v.shape, lambda i, j, k, nd=v.ndim: (0,) * nd) for v in aux]
    aliases = {}
    if stack is None:
        out_specs = [pl.BlockSpec((tm, tn), lambda i, j, k: (i, j))]
        out_shapes = [jax.ShapeDtypeStruct((M, N), out_dtype)]
    else:
        n_layers, layer, prev = stack
        out_specs = [pl.BlockSpec((None, tm, tn), lambda i, j, k: (layer, i, j))]
        out_shapes = [jax.ShapeDtypeStruct((n_layers, M, N), out_dtype)]
        if prev is not None:
            aliases = {len(operands): 0}
            operands.append(prev)
            in_specs.append(_ANY)
    for kind, dtype in sides:
        if kind == "tile":
            out_specs.append(pl.BlockSpec((tm, tn), lambda i, j, k: (i, j)))
            out_shapes.append(jax.ShapeDtypeStruct((M, N), dtype))
        else:
            assert gn == 1, name
            out_specs.append(pl.BlockSpec((8, tn), lambda i, j, k: (0, 0)))
            out_shapes.append(jax.ShapeDtypeStruct((8, N), dtype))
    n_prev = len(aliases)
    scratch = [pltpu.VMEM((tm, tn), F32)] if gk > 1 else []
    n_rin = n_rout = 0
    if rider is not None:
        n_rin, n_rout = len(rider.operands), len(rider.out_shapes)
        for src, dst in rider.aliases:
            aliases[len(operands) + src] = 1 + n_side + dst
        operands += list(rider.operands)
        in_specs += [_ANY] * n_rin
        out_specs += [_ANY] * n_rout
        out_shapes += list(rider.out_shapes)
        scratch += [pltpu.SemaphoreType.DMA((rider.n_sems,)), pltpu.SemaphoreType.DMA((rider.n_sems,))]
    assert na == 1 or nbp == 1, name

    def body(*refs):
        a_refs, b_refs = refs[:na], refs[na:na + nbp]
        ex_refs = refs[na + nbp:na + nbp + n_ex + n_aux]
        n_in = na + nbp + n_ex + n_aux + n_prev
        rin = refs[n_in:n_in + n_rin]
        o_ref = refs[n_in + n_rin]
        side_refs = refs[n_in + n_rin + 1:n_in + n_rin + 1 + n_side]
        rout = refs[n_in + n_rin + 1 + n_side:n_in + n_rin + 1 + n_side + n_rout]
        rest = refs[n_in + n_rin + 1 + n_side + n_rout:]
        i, j, k = pl.program_id(0), pl.program_id(1), pl.program_id(2)
        if rider is not None:
            sems = rest[-2:]

            @pl.when(jnp.logical_and(i == 0, jnp.logical_and(j == 0, k == 0)))
            def _():
                rider.start(rin, rout, *sems)

        def finish(acc):
            if epi is None:
                o_ref[...] = acc[...].astype(o_ref.dtype)
                return
            strip = min(tm, EPI_ROWS)
            colsums = [None] * n_side
            for r0 in range(0, tm, strip):
                rows = slice(r0, r0 + strip)
                res = epi(acc[rows, :], *[r[rows, :] for r in ex_refs[:n_ex]], *[r[...] for r in ex_refs[n_ex:]])
                if n_side:
                    res, *side_vals = res
                    for s, ((kind, _), ref, val) in enumerate(zip(sides, side_refs, side_vals)):
                        if kind == "tile":
                            ref[rows, :] = val.astype(ref.dtype)
                        else:
                            colsums[s] = val if colsums[s] is None else colsums[s] + val
                o_ref[rows, :] = res.astype(o_ref.dtype)
            for (kind, _), ref, val in zip(sides, side_refs, colsums):
                if kind == "colsum":
                    @pl.when(i == 0)
                    def _(ref=ref, val=val):
                        ref[...] = val

                    @pl.when(i > 0)
                    def _(ref=ref, val=val):
                        ref[...] += val

                    @pl.when(i == gm - 1)
                    def _(ref=ref):
                        ref[0:1, :] = jnp.sum(ref[...], axis=0, keepdims=True)

        def step(a_ref, b_ref):
            av, bv = a_ref[...], b_ref[...]
            if a_fn is not None:
                av = a_fn(av)
            if b_fn is not None:
                bv = b_fn(bv)
            part = dot(av.astype(BF16), bv.astype(BF16))
            if gk == 1:
                finish(part)
                return
            acc_ref = rest[0]

            @pl.when(k == 0)
            def _():
                acc_ref[...] = part

            @pl.when(k > 0)
            def _():
                acc_ref[...] += part

        if fold:
            kp = K // na
            finish(sum(dot(a_refs[p][...].astype(BF16), b_refs[0][:, p * kp:(p + 1) * kp].astype(BF16))
                       for p in range(na)))
        elif na > 1:
            sel = pl.program_id(a_axis) // a_per
            for p in range(na):
                pl.when(sel == p)(functools.partial(step, a_refs[p], b_refs[0]))
        elif nbp > 1:
            sel = pl.program_id(b_axis) // b_per
            for p in range(nbp):
                pl.when(sel == p)(functools.partial(step, a_refs[0], b_refs[p]))
        else:
            step(a_refs[0], b_refs[0])
        if gk > 1:
            @pl.when(k == gk - 1)
            def _():
                finish(rest[0])

        if rider is not None:
            steps = gm * gn * gk
            step_no = (i * gn + j) * gk + k
            if rider.relay is not None:
                assert steps >= 3, name

                @pl.when(step_no == steps - 2)
                def _():
                    rider.relay(rin, rout, *sems)

            @pl.when(step_no == steps - 1)
            def _():
                rider.finish(rin, rout, *sems)

    sequential = rider is not None or any(kind == "colsum" for kind, _ in sides)
    sem = ("arbitrary",) * 3 if sequential else ("parallel", "parallel", "arbitrary")
    outs = pl.pallas_call(
        body,
        name=name,
        grid=(gm, gn, gk),
        in_specs=in_specs,
        out_specs=out_specs,
        out_shape=out_shapes,
        input_output_aliases=aliases,
        scratch_shapes=scratch,
        compiler_params=_cp(sem),
    )(*operands)
    res = outs[0] if not sides else tuple(outs[:1 + n_side])
    return res if rider is None else (res, list(outs[1 + n_side:]))


def _relu2(u):
    r = jnp.maximum(u, 0)
    return r * r


def _epi_residual(acc, res):
    return acc + res


def _epi_residual_norm(acc, res, g):
    xn = acc + res
    r = lax.rsqrt(jnp.mean(xn * xn, axis=-1, keepdims=True) + RMS_EPS)
    return xn, (xn * r) * g


def _epi_rms_bwd(dh, x, dres, g):
    r = lax.rsqrt(jnp.mean(x * x, axis=-1, keepdims=True) + RMS_EPS)
    xh = x * r
    dxh = dh * g
    dx = dres + r * (dxh - xh * jnp.mean(dxh * xh, axis=-1, keepdims=True))
    return dx, dx, jnp.sum((dh * xh).reshape(dh.shape[0] // 8, 8, dh.shape[1]), axis=0)


def _rms_fwd(name, x, g):
    S, D = x.shape
    tq = min(1024, S)

    def body(x_ref, g_ref, o_ref):
        xv = x_ref[...]
        r = lax.rsqrt(jnp.mean(xv * xv, axis=-1, keepdims=True) + RMS_EPS)
        o_ref[...] = ((xv * r) * g_ref[...]).astype(o_ref.dtype)

    return pl.pallas_call(
        body,
        name=name,
        grid=(S // tq,),
        in_specs=[pl.BlockSpec((tq, D), lambda i: (i, 0)), pl.BlockSpec((1, D), lambda i: (0, 0))],
        out_specs=pl.BlockSpec((tq, D), lambda i: (i, 0)),
        out_shape=jax.ShapeDtypeStruct((S, D), BF16),
        compiler_params=_cp(("parallel",)),
    )(x, g)


def _loss_head(x, g, t):
    S, D = x.shape
    tq = min(512, S)
    n = S // tq

    def body(x_ref, g_ref, t_ref, loss_ref, dx_ref, dxb_ref, dg_ref, lacc_ref, gacc_ref):
        i = pl.program_id(0)
        xv = x_ref[...]
        gv = g_ref[...]
        r = lax.rsqrt(jnp.mean(xv * xv, axis=-1, keepdims=True) + RMS_EPS)
        xh = xv * r
        e = xh * gv - t_ref[...]
        dy = e * (1.0 / D)
        dxh = dy * gv
        dx = r * (dxh - xh * jnp.mean(dxh * xh, axis=-1, keepdims=True))
        dx_ref[...] = dx
        dxb_ref[...] = dx.astype(dxb_ref.dtype)
        lpart = jnp.sum((e * e).reshape(tq // 8, 8, D), axis=0)
        gpart = jnp.sum((dy * xh).reshape(tq // 8, 8, D), axis=0)

        @pl.when(i == 0)
        def _():
            lacc_ref[...] = lpart
            gacc_ref[...] = gpart

        @pl.when(i > 0)
        def _():
            lacc_ref[...] += lpart
            gacc_ref[...] += gpart

        @pl.when(i == n - 1)
        def _():
            dg_ref[...] = jnp.sum(gacc_ref[...], axis=0, keepdims=True)
            tot = jnp.sum(jnp.sum(lacc_ref[...], axis=0, keepdims=True), axis=1, keepdims=True)
            loss_ref[...] = jnp.broadcast_to(tot * (0.5 / D), (1, LANE))

    return pl.pallas_call(
        body,
        name="loss_head",
        grid=(n,),
        in_specs=[pl.BlockSpec((tq, D), lambda i: (i, 0)), pl.BlockSpec((1, D), lambda i: (0, 0)),
                  pl.BlockSpec((tq, D), lambda i: (i, 0))],
        out_specs=[pl.BlockSpec((1, LANE), lambda i: (0, 0)), pl.BlockSpec((tq, D), lambda i: (i, 0)),
                   pl.BlockSpec((tq, D), lambda i: (i, 0)), pl.BlockSpec((1, D), lambda i: (0, 0))],
        out_shape=[jax.ShapeDtypeStruct((1, LANE), F32), jax.ShapeDtypeStruct((S, D), F32),
                   jax.ShapeDtypeStruct((S, D), BF16), jax.ShapeDtypeStruct((1, D), F32)],
        scratch_shapes=[pltpu.VMEM((8, D), F32), pltpu.VMEM((8, D), F32)],
        compiler_params=_cp(("arbitrary",)),
    )(x, g, t)


def _ret_tables(S):
    T = min(RET_BLOCK, S)
    inv_freq = 1.0 / (RET_ROPE_BASE ** jnp.linspace(0.0, 1.0, RET_HEAD_DIM // 2, dtype=F32))
    ang = jnp.arange(S, dtype=F32)[:, None] * inv_freq[None, :]
    cos, sin = jnp.cos(ang), jnp.sin(ang)
    cosf = jnp.repeat(cos, 2, axis=-1)
    sins = jnp.stack([-sin, sin], axis=-1).reshape(S, RET_HEAD_DIM)
    log_g = np.log1p(-np.power(2.0, -5.0 - np.arange(RET_HEADS, dtype=np.float64)))
    pos = np.arange(T, dtype=np.float64)
    diff = pos[:, None] - pos[None, :]
    same = (pos[:, None] // CHUNK) == (pos[None, :] // CHUNK)
    seen = same | (diff > 0)
    dmat = np.where(seen[None], np.exp(np.abs(diff)[None] * log_g[:, None, None]), 0.0)
    aq = np.exp((pos[None, :] + 1.0) * log_g[:, None])
    ak = np.exp((T - 1.0 - pos[None, :]) * log_g[:, None])
    lam = np.exp(T * log_g)
    bc = lambda v: jnp.asarray(np.broadcast_to(v[..., None], v.shape + (LANE,)), F32)
    return dict(cos=cosf, sin=sins, dmat=jnp.asarray(dmat, F32), aq=bc(aq), ak=bc(ak),
                lam=jnp.asarray(np.broadcast_to(lam[:, None, None], (RET_HEADS, 1, LANE)), F32))


def _rot(x, cos, sin_s, even):
    sw = jnp.where(even, pltpu.roll(x, LANE - 1, 1), pltpu.roll(x, 1, 1))
    return x * cos + sw * sin_s


def _rot_t(dy, cos, sin_s, even):
    t = dy * sin_s
    return dy * cos + jnp.where(even, pltpu.roll(t, LANE - 1, 1), pltpu.roll(t, 1, 1))


def _ret_specs(T, rev_nb=None):
    blk = (lambda b: b) if rev_nb is None else (lambda b: rev_nb - 1 - b)
    whole = lambda shape: pl.BlockSpec(shape, lambda b: (0,) * len(shape))
    specs = [pl.BlockSpec((T, AB_IN_WIDTH), lambda b: (blk(b), 0)),
             pl.BlockSpec((T, LANE), lambda b: (blk(b), 0)),
             pl.BlockSpec((T, LANE), lambda b: (blk(b), 0)),
             whole((RET_HEADS, T, T)), whole((RET_HEADS, T, LANE)), whole((RET_HEADS, T, LANE)),
             whole((RET_HEADS, 1, LANE)), whole((1, RET_WIDTH))]
    return specs, blk


def _head_views(h, z_ref, tabs, token_refs, head_refs):
    zs = [z_ref.at[:, (o * RET_HEADS + h) * LANE:(o * RET_HEADS + h + 1) * LANE] for o in range(4)]
    hs = slice(h * LANE, (h + 1) * LANE)
    return zs, [t.at[h] for t in tabs], [r.at[:, hs] for r in token_refs], [r.at[h] for r in head_refs]


def _ret_fwd(name, z, tb, gain):
    S = z.shape[0]
    T = min(RET_BLOCK, S)
    nb = S // T
    specs, blk = _ret_specs(T)

    def body(z_ref, cos_r, sin_r, d_all, aq_all, ak_all, lam_all, gain_all, cat_all, opre_all, st_all, state_all):
        @pl.when(pl.program_id(0) == 0)
        def _():
            state_all[...] = jnp.zeros_like(state_all)

        for h in range(RET_HEADS):
            zs, tabs, toks, heads = _head_views(h, z_ref, (d_all, aq_all, ak_all, lam_all),
                                                (gain_all, cat_all, opre_all), (st_all, state_all))
            head(*zs, cos_r, sin_r, *tabs, *toks, *heads)

    def head(zq, zk, zv, zg, cos_r, sin_r, d_r, aq_r, ak_r, lam_r, gain_r, ret_o, opre_o, st_o, state):
        even = (lax.broadcasted_iota(jnp.int32, (T, LANE), 1) & 1) == 0
        c, s = cos_r[...], sin_r[...]
        q = _rot(zq[...], c, s, even)
        k = _rot(zk[...], c, s, even) * KSCALE
        qb, kb, vb = q.astype(BF16), k.astype(BF16), zv[...].astype(BF16)
        p = (_dot_nt(qb, kb) * d_r[...]).astype(BF16)
        st = state[...]
        st_o[...] = st
        o = _dot(p, vb) + _dot((q * aq_r[...]).astype(BF16), st.astype(BF16))
        state[...] = st * lam_r[...] + _dot_tn((k * ak_r[...]).astype(BF16), vb)
        opre_o[...] = o
        mu = jnp.mean(o, axis=-1, keepdims=True)
        d = o - mu
        y = d * lax.rsqrt(jnp.mean(d * d, axis=-1, keepdims=True) + GN_EPS)
        g = zg[...]
        ret_o[...] = ((g * jax.nn.sigmoid(g)) * (y * gain_r[...])).astype(ret_o.dtype)

    out_blk = pl.BlockSpec((T, RET_WIDTH), lambda b: (b, 0))
    return pl.pallas_call(
        body,
        name=name,
        grid=(nb,),
        in_specs=specs,
        out_specs=[out_blk, out_blk, pl.BlockSpec((RET_HEADS, None, LANE, LANE), lambda b: (0, b, 0, 0))],
        out_shape=[jax.ShapeDtypeStruct((S, D_MODEL), BF16), jax.ShapeDtypeStruct((S, RET_WIDTH), F32),
                   jax.ShapeDtypeStruct((RET_HEADS, nb, LANE, LANE), F32)],
        scratch_shapes=[pltpu.VMEM((RET_HEADS, LANE, LANE), F32)],
        compiler_params=_cp(("arbitrary",)),
    )(z, tb["cos"], tb["sin"], tb["dmat"], tb["aq"], tb["ak"], tb["lam"], gain)


def _ret_bwd(name, z, tb, gain, opre, states, dcat):
    S = z.shape[0]
    T = min(RET_BLOCK, S)
    nb = S // T
    specs, blk = _ret_specs(T, rev_nb=nb)
    tok = pl.BlockSpec((T, RET_WIDTH), lambda b: (blk(b), 0))

    def body(z_ref, cos_r, sin_r, d_all, aq_all, ak_all, lam_all, gain_all, opre_all, st_all, dret_all,
             dz_ref, dgain_all, dstate_all):
        @pl.when(pl.program_id(0) == 0)
        def _():
            dstate_all[...] = jnp.zeros_like(dstate_all)
            dgain_all[...] = jnp.zeros_like(dgain_all)

        for h in range(RET_HEADS):
            zs, tabs, toks, heads = _head_views(h, z_ref, (d_all, aq_all, ak_all, lam_all),
                                                (gain_all, opre_all, dret_all, dgain_all), (st_all, dstate_all))
            dzs, _, _, _ = _head_views(h, dz_ref, (), (), ())
            gain_r, opre_r, dret_r, dgain_o = toks
            head(*zs, cos_r, sin_r, *tabs, gain_r, opre_r, heads[0], dret_r, *dzs, dgain_o, heads[1])

    def head(zq, zk, zv, zg, cos_r, sin_r, d_r, aq_r, ak_r, lam_r, gain_r, opre_r, st_r, dret_r,
             dq_o, dk_o, dv_o, dg_o, dgain_o, dstate):
        even = (lax.broadcasted_iota(jnp.int32, (T, LANE), 1) & 1) == 0
        c, s = cos_r[...], sin_r[...]
        aq, ak, dm = aq_r[...], ak_r[...], d_r[...]
        q = _rot(zq[...], c, s, even)
        k = _rot(zk[...], c, s, even) * KSCALE
        qb, kb, vb = q.astype(BF16), k.astype(BF16), zv[...].astype(BF16)
        pb = (_dot_nt(qb, kb) * dm).astype(BF16)
        g = zg[...]
        sig = jax.nn.sigmoid(g)
        o = opre_r[...]
        mu = jnp.mean(o, axis=-1, keepdims=True)
        d = o - mu
        rstd = lax.rsqrt(jnp.mean(d * d, axis=-1, keepdims=True) + GN_EPS)
        y = d * rstd
        gain_v = gain_r[...]
        dret = dret_r[...].astype(F32)
        dyg = dret * (g * sig)
        dg_o[...] = (dret * (y * gain_v) * (sig * (1.0 + g * (1.0 - sig)))).astype(dg_o.dtype)
        dgain_o[...] += jnp.sum(dyg * y, axis=0, keepdims=True)
        dy = dyg * gain_v
        do = rstd * (dy - jnp.mean(dy, axis=-1, keepdims=True) - y * jnp.mean(dy * y, axis=-1, keepdims=True))
        dob = do.astype(BF16)
        stb = st_r[...].astype(BF16)
        dsn = dstate[...]
        dsnb = dsn.astype(BF16)
        dpb = (_dot_nt(dob, vb) * dm).astype(BF16)
        dq = _dot(dpb, kb) + _dot_nt(dob, stb) * aq
        dk = _dot_tn(dpb, qb) + _dot_nt(vb, dsnb) * ak
        dv = _dot_tn(pb, dob) + _dot((k * ak).astype(BF16), dsnb)
        dstate[...] = dsn * lam_r[...] + _dot_tn((q * aq).astype(BF16), dob)
        dq_o[...] = _rot_t(dq, c, s, even).astype(dq_o.dtype)
        dk_o[...] = _rot_t(dk * KSCALE, c, s, even).astype(dk_o.dtype)
        dv_o[...] = dv.astype(dv_o.dtype)

    return pl.pallas_call(
        body,
        name=name,
        grid=(nb,),
        in_specs=specs + [tok, pl.BlockSpec((RET_HEADS, None, LANE, LANE), lambda b: (0, blk(b), 0, 0)), tok],
        out_specs=[pl.BlockSpec((T, 4 * RET_WIDTH), lambda b: (blk(b), 0)), pl.BlockSpec((1, RET_WIDTH), lambda b: (0, 0))],
        out_shape=[jax.ShapeDtypeStruct((S, AB_IN_WIDTH), BF16), jax.ShapeDtypeStruct((1, RET_WIDTH), F32)],
        scratch_shapes=[pltpu.VMEM((RET_HEADS, LANE, LANE), F32)],
        compiler_params=_cp(("arbitrary",)),
    )(z, tb["cos"], tb["sin"], tb["dmat"], tb["aq"], tb["ak"], tb["lam"], gain, opre, states, dcat)


def _pool_counts(t0, rows):
    t = t0 + lax.broadcasted_iota(jnp.int32, (rows, POOL_WIDTH), 0)
    grp = lax.broadcasted_iota(jnp.int32, (rows, POOL_WIDTH), 1) >> 7
    win = jnp.where(grp == 0, POOL_WINDOWS[0], jnp.where(grp == 1, POOL_WINDOWS[1],
                    jnp.where(grp == 2, POOL_WINDOWS[2], POOL_WINDOWS[3])))
    return jnp.maximum(jnp.minimum(t + 1, win), 1).astype(F32), grp


def _window_sums(ext, grp, sign):
    n = ext.shape[0]
    sh = lambda v, k: pltpu.roll(v, k % n if sign > 0 else (n - k) % n, 0)
    s2 = ext + sh(ext, 1)
    s4 = s2 + sh(s2, 2)
    s8 = s4 + sh(s4, 4)
    s16 = s8 + sh(s8, 8)
    return jnp.where(grp == 0, s2, jnp.where(grp == 1, s4, jnp.where(grp == 2, s8, s16)))


def _pool_fwd(name, z, w_pool, scale, cat):
    S = z.shape[0]
    T = min(512, S)
    nb = S // T
    pcol = AB_IN_WIDTH // POOL_WIDTH - 1
    hb = T // POOL_HALO

    def body(p_ref, halo_ref, w_ref, sc_ref, cat_in, out_ref, pooled_ref):
        b = pl.program_id(0)
        cur = p_ref[...]
        halo = jnp.where(b > 0, halo_ref[...], 0.0)
        ext = jnp.concatenate([halo, cur], axis=0)
        cnt, grp = _pool_counts(b * T - POOL_HALO, T + POOL_HALO)
        sums = _window_sums(ext, grp, +1)
        pooled = (sums / cnt)[POOL_HALO:] - cur
        pb = pooled.astype(BF16)
        pooled_ref[...] = pb
        for gi in range(len(POOL_WINDOWS)):
            cs = slice(gi * POOL_GROUP_DIM, (gi + 1) * POOL_GROUP_DIM)
            mixed = _dot(pb[:, cs], w_ref[gi].astype(BF16))
            out_ref[:, cs] = (mixed * sc_ref[:, cs]).astype(out_ref.dtype)

    return pl.pallas_call(
        body,
        name=name,
        grid=(nb,),
        in_specs=[pl.BlockSpec((T, POOL_WIDTH), lambda b: (b, pcol)),
                  pl.BlockSpec((POOL_HALO, POOL_WIDTH), lambda b: (jnp.maximum(b * hb - 1, 0), pcol)),
                  pl.BlockSpec((4, POOL_GROUP_DIM, POOL_GROUP_DIM), lambda b: (0, 0, 0)),
                  pl.BlockSpec((1, POOL_WIDTH), lambda b: (0, 0)), _ANY],
        out_specs=[pl.BlockSpec((T, POOL_WIDTH), lambda b: (b, 1)), pl.BlockSpec((T, POOL_WIDTH), lambda b: (b, 0))],
        out_shape=[jax.ShapeDtypeStruct(cat.shape, cat.dtype), jax.ShapeDtypeStruct((S, POOL_WIDTH), BF16)],
        input_output_aliases={4: 0},
        compiler_params=_cp(("parallel",)),
    )(z, z, w_pool, scale, cat)


def _pool_bwd(name, pooled, w_pool, scale, dcat, dz):
    S = pooled.shape[0]
    T = min(512, S)
    nb = S // T
    hb = T // POOL_HALO
    last_h = S // POOL_HALO - 1
    pcol = AB_IN_WIDTH // POOL_WIDTH - 1

    def body(d_ref, dn_ref, pooled_ref, w_ref, sc_ref, dz_in, dp_ref, dw_ref, dsc_ref):
        b = pl.program_id(0)

        @pl.when(b == 0)
        def _():
            dw_ref[...] = jnp.zeros_like(dw_ref)
            dsc_ref[...] = jnp.zeros_like(dsc_ref)

        sc = sc_ref[...]
        dout = d_ref[...].astype(F32)
        dnext = jnp.where(b < nb - 1, dn_ref[...].astype(F32), 0.0)
        dmix = jnp.concatenate([dout, dnext], axis=0) * sc
        dmb = dmix.astype(BF16)
        pb = pooled_ref[...]
        dpooled = []
        for gi in range(len(POOL_WINDOWS)):
            cs = slice(gi * POOL_GROUP_DIM, (gi + 1) * POOL_GROUP_DIM)
            wb = w_ref[gi].astype(BF16)
            dpooled.append(_dot_nt(dmb[:, cs], wb))
            dw_ref[gi] += _dot_tn(pb[:, cs], dmb[:T, cs])
            mixed = _dot(pb[:, cs], wb)
            dsc_ref[:, cs] += jnp.sum(dout[:, cs] * mixed, axis=0, keepdims=True)
        dpl = jnp.concatenate(dpooled, axis=1)
        cnt, grp = _pool_counts(b * T, T + POOL_HALO)
        sums = _window_sums(dpl / cnt, grp, -1)
        dp_ref[...] = (sums[:T] - dpl[:T]).astype(dp_ref.dtype)

    return pl.pallas_call(
        body,
        name=name,
        grid=(nb,),
        in_specs=[pl.BlockSpec((T, POOL_WIDTH), lambda b: (b, 1)),
                  pl.BlockSpec((POOL_HALO, POOL_WIDTH), lambda b: (jnp.minimum((b + 1) * hb, last_h), 1)),
                  pl.BlockSpec((T, POOL_WIDTH), lambda b: (b, 0)),
                  pl.BlockSpec((4, POOL_GROUP_DIM, POOL_GROUP_DIM), lambda b: (0, 0, 0)),
                  pl.BlockSpec((1, POOL_WIDTH), lambda b: (0, 0)), _ANY],
        out_specs=[pl.BlockSpec((T, POOL_WIDTH), lambda b: (b, pcol)),
                   pl.BlockSpec((4, POOL_GROUP_DIM, POOL_GROUP_DIM), lambda b: (0, 0, 0)),
                   pl.BlockSpec((1, POOL_WIDTH), lambda b: (0, 0))],
        out_shape=[jax.ShapeDtypeStruct(dz.shape, dz.dtype),
                   jax.ShapeDtypeStruct((4, POOL_GROUP_DIM, POOL_GROUP_DIM), F32),
                   jax.ShapeDtypeStruct((1, POOL_WIDTH), F32)],
        input_output_aliases={5: 0},
        compiler_params=_cp(("arbitrary",)),
    )(dcat, dcat, pooled, w_pool, scale, dz)


ATT_STRIP = 32
ATT_Q = 256
ATT_W = ATT_Q + LEFT_CHUNKS * CHUNK


def _rel_index():
    j = np.arange(ATT_W)
    rel = np.clip(LEFT_CHUNKS * CHUNK - j, -REL_CLIP, REL_CLIP) + REL_CLIP
    fwd = np.where(j < BAND, rel, N_REL)
    bwd = np.where(j <= ATT_W - CHUNK, fwd, 2 * REL_CLIP)
    return tuple(jnp.asarray(v.reshape(1, ATT_W), jnp.int32) for v in (fwd, bwd))


def _bias_table(name, rel_bias, rel_idx):
    rb = jnp.concatenate([rel_bias, jnp.full((ATT_HEADS, 1), NEG_INF, F32),
                          jnp.zeros((ATT_HEADS, N_REL_PAD - N_REL - 1), F32)], axis=1)

    def body(rb_ref, idx_ref, o_ref, row0_ref):
        r = lax.broadcasted_iota(jnp.int32, (N_REL_PAD, ATT_W), 0)
        onehot = (r == idx_ref[...]).astype(F32)
        row0_ref[...] = jnp.dot(rb_ref[...], onehot, precision=lax.Precision.HIGHEST, preferred_element_type=F32)
        col = lax.broadcasted_iota(jnp.int32, (CHUNK, ATT_W), 1)
        row = lax.broadcasted_iota(jnp.int32, (CHUNK, ATT_W), 0)
        for h in range(ATT_HEADS):
            same = jnp.broadcast_to(row0_ref[pl.ds(h, 1), :], (CHUNK, ATT_W))
            turned = pltpu.roll(same, 0, 1, stride=1, stride_axis=0)
            o_ref[h] = jnp.where(col >= BAND, NEG_INF, jnp.where(col < row, same, turned))

    return pl.pallas_call(
        body,
        name=name,
        out_shape=jax.ShapeDtypeStruct((ATT_HEADS, CHUNK, ATT_W), F32),
        scratch_shapes=[pltpu.VMEM((ATT_HEADS, ATT_W), F32)],
        compiler_params=pltpu.CompilerParams(vmem_limit_bytes=VMEM_LIMIT),
    )(rb, rel_idx)


def _bias_grad(name, dband, rel_idx):
    def body(d_ref, idx_ref, o_ref, sums_ref):
        row = lax.broadcasted_iota(jnp.int32, (CHUNK, ATT_W), 0)
        for h in range(ATT_HEADS):
            back = d_ref[h]
            for bit in range(CHUNK.bit_length() - 1):
                back = jnp.where(((row >> bit) & 1) == 1, pltpu.roll(back, ATT_W - (1 << bit), 1), back)
            sums_ref[pl.ds(h, 1), :] = jnp.sum(back, axis=0, keepdims=True)
        r = lax.broadcasted_iota(jnp.int32, (N_REL_PAD, ATT_W), 0)
        onehot = (r == idx_ref[...]).astype(F32)
        o_ref[...] = lax.dot_general(sums_ref[...], onehot, (((1,), (1,)), ((), ())),
                                     precision=lax.Precision.HIGHEST, preferred_element_type=F32)

    out = pl.pallas_call(
        body,
        name=name,
        out_shape=jax.ShapeDtypeStruct((ATT_HEADS, N_REL_PAD), F32),
        scratch_shapes=[pltpu.VMEM((ATT_HEADS, ATT_W), F32)],
        compiler_params=pltpu.CompilerParams(vmem_limit_bytes=VMEM_LIMIT),
    )(dband, rel_idx)
    return out[:, :N_REL]


def _attn_unit(q_ref, kw_ref, bias_ref, e, u, lane):
    mine = (lane < ATT_HEAD_DIM) if e == 0 else (lane >= ATT_HEAD_DIM)
    qm = jnp.where(mine, q_ref[u * ATT_Q:(u + 1) * ATT_Q, :] * QSCALE, 0)
    kw = kw_ref[u * ATT_Q:u * ATT_Q + ATT_W, :]
    s = _dot_nt(qm, kw) + bias_ref[u, e]
    p = jnp.exp(s - jnp.max(s, axis=-1, keepdims=True))
    return p, 1.0 / jnp.sum(p, axis=-1, keepdims=True), qm, kw, mine


def _attn_in_specs(nb):
    T = ATT_BLOCK
    hp = ATT_HEADS // 2
    cur = lambda off: pl.BlockSpec((T, LANE), lambda h, b: (jnp.minimum(b, nb - 1), off + h))
    prev = lambda off: pl.BlockSpec((T, LANE), lambda h, b: (jnp.clip(b - 1, 0, nb - 1), off + h))
    return [cur(0), prev(hp), cur(hp), prev(2 * hp), cur(2 * hp),
            pl.BlockSpec((None, 2, CHUNK, ATT_W), lambda h, b: (h, 0, 0, 0))]


def _spread_bias(bias_ref, bm_ref, block):
    col = lax.broadcasted_iota(jnp.int32, (CHUNK, ATT_W), 1)
    for first in (True, False):
        @pl.when(block == (0 if first else 1))
        def _(first=first):
            for u in range(ATT_BLOCK // ATT_Q):
                for e in range(2):
                    for j in range(ATT_Q // CHUNK):
                        rows = pltpu.roll(bias_ref[e], j * CHUNK, 1)
                        if first:
                            rows = jnp.where(col >= ATT_BLOCK - u * ATT_Q, rows, NEG_INF)
                        bm_ref[u, e, j * CHUNK:(j + 1) * CHUNK, :] = rows


def _attn_fwd(name, qkv, bias):
    S = qkv.shape[0]
    T = ATT_BLOCK
    nb = S // T

    def body(q_ref, kp_ref, kc_ref, vp_ref, vc_ref, band_ref, o_ref, kw_ref, vw_ref, bias_ref, s_ref, p_ref, inv_ref):
        _spread_bias(band_ref, bias_ref, pl.program_id(1))
        kw_ref[0:T] = kp_ref[...]
        kw_ref[T:2 * T] = kc_ref[...]
        vw_ref[0:T] = vp_ref[...]
        vw_ref[T:2 * T] = vc_ref[...]
        lane = lax.broadcasted_iota(jnp.int32, (ATT_Q, LANE), 1)
        for u in range(T // ATT_Q):
            vw = vw_ref[u * ATT_Q:u * ATT_Q + ATT_W, :]
            kw = kw_ref[u * ATT_Q:u * ATT_Q + ATT_W, :]
            outs = []
            for e in range(2):
                mine = (lane < ATT_HEAD_DIM) if e == 0 else (lane >= ATT_HEAD_DIM)
                qm = jnp.where(mine, q_ref[u * ATT_Q:(u + 1) * ATT_Q, :] * QSCALE, 0)
                s_ref[e] = _dot_nt(qm, kw)
                for r in range(ATT_Q // ATT_STRIP):
                    rows = slice(r * ATT_STRIP, (r + 1) * ATT_STRIP)
                    s = s_ref[e, rows, :] + bias_ref[u, e, rows, :]
                    p = jnp.exp(s - jnp.max(s, axis=-1, keepdims=True))
                    inv_ref[e, rows, :] = jnp.broadcast_to(1.0 / jnp.sum(p, axis=-1, keepdims=True), (ATT_STRIP, LANE))
                    p_ref[e, rows, :] = p.astype(BF16)
                outs.append(_dot(p_ref[e], vw) * inv_ref[e])
            o_ref[u * ATT_Q:(u + 1) * ATT_Q, :] = jnp.where(lane < ATT_HEAD_DIM, outs[0], outs[1]).astype(o_ref.dtype)

    return pl.pallas_call(
        body,
        name=name,
        grid=(ATT_HEADS // 2, nb),
        in_specs=_attn_in_specs(nb),
        out_specs=pl.BlockSpec((T, LANE), lambda h, b: (b, h)),
        out_shape=jax.ShapeDtypeStruct((S, D_MODEL), BF16),
        scratch_shapes=[pltpu.VMEM((2 * T, LANE), BF16), pltpu.VMEM((2 * T, LANE), BF16),
                        pltpu.VMEM((T // ATT_Q, 2, ATT_Q, ATT_W), F32), pltpu.VMEM((2, ATT_Q, ATT_W), F32),
                        pltpu.VMEM((2, ATT_Q, ATT_W), BF16), pltpu.VMEM((2, ATT_Q, LANE), F32)],
        compiler_params=_cp(("parallel", "arbitrary")),
    )(qkv, qkv, qkv, qkv, qkv, bias)


def _attn_bwd(name, qkv, bias, do):
    S = qkv.shape[0]
    T = ATT_BLOCK
    nb = S // T

    def body(q_ref, kp_ref, kc_ref, vp_ref, vc_ref, band_ref, do_ref,
             dq_ref, dk_ref, dv_ref, dband_ref, kw_ref, vw_ref, dkw_ref, dvw_ref, bias_ref, dbias_ref):
        b = pl.program_id(1)

        _spread_bias(band_ref, bias_ref, b)

        @pl.when(b == 0)
        def _():
            dbias_ref[...] = jnp.zeros_like(dbias_ref)
            dkw_ref[T:2 * T] = jnp.zeros((T, LANE), F32)
            dvw_ref[T:2 * T] = jnp.zeros((T, LANE), F32)

        dkw_ref[0:T] = dkw_ref[T:2 * T]
        dvw_ref[0:T] = dvw_ref[T:2 * T]
        dkw_ref[T:2 * T] = jnp.zeros((T, LANE), F32)
        dvw_ref[T:2 * T] = jnp.zeros((T, LANE), F32)

        @pl.when(b < nb)
        def _():
            kw_ref[0:T] = kp_ref[...]
            kw_ref[T:2 * T] = kc_ref[...]
            vw_ref[0:T] = vp_ref[...]
            vw_ref[T:2 * T] = vc_ref[...]
            lane = lax.broadcasted_iota(jnp.int32, (ATT_Q, LANE), 1)
            for u in range(T // ATT_Q):
                rows = slice(u * ATT_Q, (u + 1) * ATT_Q)
                win = slice(u * ATT_Q, u * ATT_Q + ATT_W)
                vw = vw_ref[win, :]
                do2 = do_ref[rows, :]
                dqs, dk, dv = [], None, None
                for e in range(2):
                    p, inv, qm, kw, mine = _attn_unit(q_ref, kw_ref, bias_ref, e, u, lane)
                    dom = jnp.where(mine, do2, 0)
                    dp = _dot_nt(dom, vw)
                    delta = jnp.sum(p * dp, axis=-1, keepdims=True) * inv
                    ds = p * ((dp - delta) * inv)
                    dbias_ref[e] += ds
                    dsb = ds.astype(BF16)
                    dqs.append(_dot(dsb, kw))
                    dk_e = _dot_tn(dsb, qm)
                    dv_e = _dot_tn((p * inv).astype(BF16), dom)
                    dk = dk_e if dk is None else dk + dk_e
                    dv = dv_e if dv is None else dv + dv_e
                dq_ref[rows, :] = (jnp.where(lane < ATT_HEAD_DIM, dqs[0], dqs[1]) * QSCALE).astype(dq_ref.dtype)
                dkw_ref[win, :] += dk
                dvw_ref[win, :] += dv

        @pl.when(b > 0)
        def _():
            dk_ref[...] = dkw_ref[0:T].astype(dk_ref.dtype)
            dv_ref[...] = dvw_ref[0:T].astype(dv_ref.dtype)

        @pl.when(b == nb)
        def _():
            for e in range(2):
                acc = dbias_ref[e, 0:CHUNK, :]
                for j in range(1, ATT_Q // CHUNK):
                    acc = acc + pltpu.roll(dbias_ref[e, j * CHUNK:(j + 1) * CHUNK, :], ATT_W - j * CHUNK, 1)
                dband_ref[e] = acc

    tok = jax.ShapeDtypeStruct((S, D_MODEL), BF16)
    prev_out = pl.BlockSpec((T, LANE), lambda h, b: (jnp.maximum(b - 1, 0), h))
    return pl.pallas_call(
        body,
        name=name,
        grid=(ATT_HEADS // 2, nb + 1),
        in_specs=_attn_in_specs(nb) + [pl.BlockSpec((T, LANE), lambda h, b: (jnp.minimum(b, nb - 1), h))],
        out_specs=[pl.BlockSpec((T, LANE), lambda h, b: (jnp.minimum(b, nb - 1), h)), prev_out, prev_out,
                   pl.BlockSpec((None, 2, CHUNK, ATT_W), lambda h, b: (h, 0, 0, 0))],
        out_shape=[tok, tok, tok, jax.ShapeDtypeStruct((ATT_HEADS // 2, 2, CHUNK, ATT_W), F32)],
        scratch_shapes=[pltpu.VMEM((2 * T, LANE), BF16), pltpu.VMEM((2 * T, LANE), BF16),
                        pltpu.VMEM((2 * T, LANE), F32), pltpu.VMEM((2 * T, LANE), F32),
                        pltpu.VMEM((T // ATT_Q, 2, ATT_Q, ATT_W), F32), pltpu.VMEM((2, ATT_Q, ATT_W), F32)],
        compiler_params=_cp(("parallel", "arbitrary")),
    )(qkv, qkv, qkv, qkv, qkv, bias, do)


def _local_step(x, target, small, W):
    S = x.shape[0]
    tb = _ret_tables(S)
    rel_fwd, rel_bwd = _rel_index()
    saved = []
    normed = (("tile", BF16),)
    deep = dict(tm=512, tk=D_FF)
    h = _rms_fwd("mix_norm_fwd0", x, small["mix_norm"][0:1])
    for layer in range(DEPTH):
        i = layer // 2
        st = {"x_in": x, "h": h}
        g_ffn = small["ffn_norm"][layer:layer + 1]
        if layer % 2 == 0:
            z = W.mm(f"ab_in_fwd{layer}", "nn", h, W.get("ab_w_in", i), tn=640, out_dtype=F32)
            gain = small["ab_gn_gain"][i:i + 1]
            cat, opre, states = _ret_fwd(f"ret_fwd{layer}", z, tb, gain)
            cat, pooled = _pool_fwd(f"pool_fwd{layer}", z, small["ab_w_pool"][i], small["ab_pool_scale"][i:i + 1], cat)
            st.update(z=z, opre=opre, states=states, pooled=pooled, cat=cat)
            x, hn = W.mm(f"ab_out_fwd{layer}", "nn", cat, W.get("ab_w_out", i), extras=(x,), aux=(g_ffn,), sides=normed,
                         epi=_epi_residual_norm)
        else:
            qkv = W.mm(f"qkv_fwd{layer}", "nn", h, W.get("c_w_qkv", i), out_dtype=BF16)
            bias = _bias_table(f"bias_table{layer}", small["c_rel_bias"][i], rel_fwd)
            bias = bias.reshape(ATT_HEADS // 2, 2, CHUNK, ATT_W)
            att = _attn_fwd(f"attn_fwd{layer}", qkv, bias)
            st.update(qkv=qkv, bias=bias, att=att)
            x, hn = W.mm(f"c_out_fwd{layer}", "nn", att, W.get("c_w_out", i), extras=(x,), aux=(g_ffn,), sides=normed,
                         epi=_epi_residual_norm)
        st["x_mid"] = x
        u = W.mm(f"ffn_in_fwd{layer}", "nn", hn, W.get("w_ffn_in", layer), out_dtype=BF16, tm=2048)
        if layer + 1 < DEPTH:
            x, h = W.mm(f"ffn_out_fwd{layer}", "nn", u, W.get("w_ffn_out", layer), a_fn=_relu2, extras=(x,),
                        aux=(small["mix_norm"][layer + 1:layer + 2],), sides=normed, epi=_epi_residual_norm, **deep)
        else:
            x = W.mm(f"ffn_out_fwd{layer}", "nn", u, W.get("w_ffn_out", layer), a_fn=_relu2, extras=(x,),
                     epi=_epi_residual, **deep)
        st.update(hn=hn, u=u)
        saved.append(st)

    loss, dx, dxb, d_final = _loss_head(x, small["final_norm"].reshape(1, D_MODEL), target)

    gs = {k: [None] * v.shape[0] for k, v in small.items() if k != "final_norm"}
    gb = {k: None for k in W.n_layers}
    landed = {k: None for k in W.n_layers}
    pending = []

    def host(name, *args, take=1, **kw):
        items = [pending.pop(0) for _ in range(min(take, len(pending)))]
        if not items:
            return _mm(name, *args, **kw)
        riders = [_grad_rider(key, idx, gb[key], landed[key]) for key, idx in items]
        res, outs = _mm(name, *args, rider=_join_riders(riders), **kw)
        for (key, _), out in zip(items, outs):
            landed[key] = out
        return res

    def dw(name, key, idx, a, b, call=_mm, **kw):
        gb[key] = call(name, "tn", a, b, stack=(W.n_layers[key], idx, gb[key]), out_dtype=BF16, **kw)
        pending.append((key, idx))

    gain_sums = (("tile", BF16), ("colsum", F32))
    for layer in reversed(range(DEPTH)):
        i = layer // 2
        st = saved[layer]
        du = host(f"ffn_out_bwd{layer}", "nt", dxb, W.get("w_ffn_out", layer), extras=(st["u"],),
                  epi=lambda acc, u: acc * (2.0 * jnp.maximum(u, 0).astype(F32)), out_dtype=BF16, tm=2048)
        dw(f"ffn_out_dw{layer}", "w_ffn_out", layer, st["u"], dxb, a_fn=_relu2, tk=2048)
        dx, dxb, dgain = host(f"ffn_in_bwd{layer}", "nt", du, W.get("w_ffn_in", layer), extras=(st["x_mid"], dx),
                         aux=(small["ffn_norm"][layer:layer + 1],), sides=gain_sums, epi=_epi_rms_bwd, **deep)
        gs["ffn_norm"][layer] = dgain[0:1]
        dw(f"ffn_in_dw{layer}", "w_ffn_in", layer, st["hn"], du, tk=2048)
        norm_bwd = dict(extras=(st["x_in"], dx), aux=(small["mix_norm"][layer:layer + 1],), sides=gain_sums,
                        epi=_epi_rms_bwd)
        if layer % 2 == 0:
            dcat = _mm(f"ab_out_bwd{layer}", "nt", dxb, W.get("ab_w_out", i), out_dtype=BF16)
            dw(f"ab_out_dw{layer}", "ab_w_out", i, st["cat"], dxb, tk=2048)
            gain = small["ab_gn_gain"][i:i + 1]
            dz, gs["ab_gn_gain"][i] = _ret_bwd(f"ret_bwd{layer}", st["z"], tb, gain, st["opre"], st["states"], dcat)
            dz, gs["ab_w_pool"][i], gs["ab_pool_scale"][i] = _pool_bwd(
                f"pool_bwd{layer}", st["pooled"], small["ab_w_pool"][i], small["ab_pool_scale"][i:i + 1], dcat, dz)
            if layer == 0:
                dw(f"ab_in_dw{layer}", "ab_w_in", i, st["h"], dz, call=host, tn=640, tk=2048)
            dx, dxb, dgain = host(f"ab_in_bwd{layer}", "nt", dz, W.get("ab_w_in", i), tm=512, tk=AB_IN_WIDTH,
                             take=len(pending) if layer == 0 else 1, **norm_bwd)
            if layer > 0:
                dw(f"ab_in_dw{layer}", "ab_w_in", i, st["h"], dz, call=host, tn=640, tk=2048)
        else:
            datt = _mm(f"c_out_bwd{layer}", "nt", dxb, W.get("c_w_out", i), out_dtype=BF16)
            dw(f"c_out_dw{layer}", "c_w_out", i, st["att"], dxb, tk=2048)
            dq, dk, dv, dbias = _attn_bwd(f"attn_bwd{layer}", st["qkv"], st["bias"], datt)
            gs["c_rel_bias"][i] = _bias_grad(f"bias_grad{layer}", dbias.reshape(ATT_HEADS, CHUNK, ATT_W), rel_bwd)
            dqkv = [dq, dk, dv]
            dx, dxb, dgain = host(f"qkv_bwd{layer}", "nt", dqkv, W.get("c_w_qkv", i), tm=512, tk=3 * D_MODEL, **norm_bwd)
            dw(f"qkv_dw{layer}", "c_w_qkv", i, st["h"], dqkv, call=host, tk=2048)
        gs["mix_norm"][layer] = dgain[0:1]
    for key, idx in pending:
        landed[key], = _run_rider(f"grad_exchange_{key}{idx}", _grad_rider(key, idx, gb[key], landed[key]))

    g_small = {
        "mix_norm": jnp.concatenate(gs["mix_norm"], axis=0),
        "ffn_norm": jnp.concatenate(gs["ffn_norm"], axis=0),
        "ab_gn_gain": jnp.concatenate(gs["ab_gn_gain"], axis=0),
        "ab_w_pool": jnp.stack(gs["ab_w_pool"], axis=0),
        "ab_pool_scale": jnp.concatenate(gs["ab_pool_scale"], axis=0),
        "c_rel_bias": jnp.stack(gs["c_rel_bias"], axis=0),
        "final_norm": d_final.reshape(D_MODEL),
    }
    return loss, dx, g_small, gb, landed


_BIG = ("w_ffn_in", "w_ffn_out", "ab_w_in", "ab_w_out", "c_w_qkv", "c_w_out")
_SHARD_AXIS = {"w_ffn_in": 2, "w_ffn_out": 1, "ab_w_in": 2, "ab_w_out": 1, "c_w_qkv": 2, "c_w_out": 1}
_SMALL = ("mix_norm", "ffn_norm", "ab_gn_gain", "ab_w_pool", "ab_pool_scale", "c_rel_bias", "final_norm")


def _place():
    x, y, c = lax.axis_index("x"), lax.axis_index("y"), lax.axis_index("c")
    chips = [(1 - x, y), (x, 1 - y), (1 - x, 1 - y)]
    return x, y, c, chips


def _sub(ref, axis, start, size):
    idx = [slice(None)] * len(ref.shape)
    idx[axis] = pl.ds(pl.multiple_of(start, LANE), size)
    return ref.at[tuple(idx)]


def _gather_rider(items, shards):
    keys = sorted({k for k, _ in items})
    n = len(items)
    axes = [_SHARD_AXIS[k] - 1 for k, _ in items]
    sizes = [shards[k].shape[a + 1] for (k, _), a in zip(items, axes)]
    hsizes = [shards[k].shape[2 - a] // 2 for (k, _), a in zip(items, axes)]

    def views(ins, outs, send_sems, recv_sems):
        x, y, c, chips = _place()
        srcs = [ins[keys.index(k)].at[l] for k, l in items]

        def remote(src, dst, s, to):
            return pltpu.make_async_remote_copy(src_ref=src, dst_ref=dst, send_sem=send_sems.at[s],
                                                recv_sem=recv_sems.at[s], device_id=to, device_id_type=MESH)

        def half(w, chip, core):
            return _sub(_sub(outs[w], axes[w], chip * sizes[w], sizes[w]), 1 - axes[w], core * hsizes[w], hsizes[w])

        me = 2 * x + y
        local = [pltpu.make_async_copy(srcs[w], _sub(outs[w], axes[w], me * sizes[w], sizes[w]), send_sems.at[6 * n + w])
                 for w in range(n)]
        first = [remote(_sub(srcs[w], 1 - axes[w], c * hsizes[w], hsizes[w]), half(w, me, c), w * 6 + k, (px, py, c))
                 for w in range(n) for k, (px, py) in enumerate(chips)]
        return x, y, c, chips, remote, half, local, first

    def start(ins, outs, send_sems, recv_sems):
        *_, local, first = views(ins, outs, send_sems, recv_sems)
        for cp in local + first:
            cp.start()

    def passes(x, y, c, chips, remote, half):
        return [remote(half(w, 2 * px + py, c), half(w, 2 * px + py, c), w * 6 + 3 + k, (x, y, 1 - c))
                for w in range(n) for k, (px, py) in enumerate(chips)]

    def relay(ins, outs, send_sems, recv_sems):
        x, y, c, chips, remote, half, _, _ = views(ins, outs, send_sems, recv_sems)
        for w in range(n):
            for k, (px, py) in enumerate(chips):
                landed = half(w, 2 * px + py, c)
                remote(landed, landed, w * 6 + k, (px, py, c)).wait_recv()
        for cp in passes(x, y, c, chips, remote, half):
            cp.start()

    def finish(ins, outs, send_sems, recv_sems):
        x, y, c, chips, remote, half, local, first = views(ins, outs, send_sems, recv_sems)
        for w in range(n):
            for k, (px, py) in enumerate(chips):
                theirs = half(w, 2 * px + py, 1 - c)
                remote(theirs, theirs, w * 6 + 3 + k, (x, y, 1 - c)).wait_recv()
        for cp in first + passes(x, y, c, chips, remote, half):
            cp.wait_send()
        for cp in local:
            cp.wait()

    def full(k, a):
        shape = list(shards[k].shape[1:])
        shape[a] *= N_CHIPS
        return jax.ShapeDtypeStruct(tuple(shape), shards[k].dtype)

    return _Rider(tuple(shards[k] for k in keys), tuple(full(k, a) for (k, _), a in zip(items, axes)), 7 * n, start, finish,
                  relay=relay)


def _mixer_items(layer):
    names = ("ab_w_in", "ab_w_out") if layer % 2 == 0 else ("c_w_qkv", "c_w_out")
    return [(k, layer // 2) for k in names]


class _Weights:
    def __init__(self, shards):
        self.shards = shards
        self.n_layers = {k: shards[k].shape[0] for k in _BIG}
        self.full = {}
        first, second = _mixer_items(0)
        self._take([first], _run_rider("gather_first", _gather_rider([first], shards)))
        self.plan = {"ab_in_fwd0": [second, ("w_ffn_in", 0)], "ab_out_fwd0": [("w_ffn_out", 0)]}
        for layer in range(1, DEPTH):
            proj = "ab_in" if layer % 2 == 0 else "qkv"
            self.plan[f"ffn_in_fwd{layer - 1}"] = _mixer_items(layer)
            self.plan[f"ffn_out_fwd{layer - 1}"] = [("w_ffn_in", layer)]
            self.plan[f"{proj}_fwd{layer}"] = [("w_ffn_out", layer)]

    def _take(self, items, outs):
        self.full.update(zip(items, outs))

    def get(self, name, layer):
        return self.full[(name, layer)]

    def mm(self, name, *args, **kw):
        items = self.plan.get(name)
        if items is None:
            return _mm(name, *args, **kw)
        res, outs = _mm(name, *args, rider=_gather_rider(items, self.shards), **kw)
        self._take(items, outs)
        return res


def _run_rider(name, rider):
    n_in, n_out = len(rider.operands), len(rider.out_shapes)

    def body(*refs):
        ins, outs, sems = refs[:n_in], refs[n_in:n_in + n_out], refs[n_in + n_out:]
        rider.start(ins, outs, *sems)
        if rider.relay is not None:
            rider.relay(ins, outs, *sems)
        rider.finish(ins, outs, *sems)

    return pl.pallas_call(
        body,
        name=name,
        in_specs=[_ANY] * n_in,
        out_specs=[_ANY] * n_out,
        out_shape=list(rider.out_shapes),
        input_output_aliases=dict(rider.aliases),
        scratch_shapes=[pltpu.SemaphoreType.DMA((rider.n_sems,)), pltpu.SemaphoreType.DMA((rider.n_sems,))],
        compiler_params=pltpu.CompilerParams(has_side_effects=True),
    )(*rider.operands)


def _grad_rider(name, layer, grad, landing):
    axis = _SHARD_AXIS[name] - 1
    L, R, C = grad.shape
    shard = (R // N_CHIPS, C) if axis == 0 else (R, C // N_CHIPS)
    size = shard[axis]

    def copies(ins, outs, send_sems, recv_sems):
        x, y, c, chips = _place()
        return [pltpu.make_async_remote_copy(
            src_ref=_sub(ins[0].at[layer], axis, (2 * px + py) * size, size), dst_ref=outs[0].at[layer, k],
            send_sem=send_sems.at[k], recv_sem=recv_sems.at[k], device_id=(px, py, c), device_id_type=MESH)
            for k, (px, py) in enumerate(chips)]

    def start(ins, outs, send_sems, recv_sems):
        for cp in copies(ins, outs, send_sems, recv_sems):
            cp.start()

    def finish(ins, outs, send_sems, recv_sems):
        cps = copies(ins, outs, send_sems, recv_sems)
        for cp in cps:
            cp.wait_recv()
        for cp in cps:
            cp.wait_send()

    out = jax.ShapeDtypeStruct((L, 3) + shard, grad.dtype)
    if landing is None:
        return _Rider((grad,), (out,), 3, start, finish)
    return _Rider((grad, landing), (out,), 3, start, finish, aliases=((1, 0),))


def _join_riders(riders):
    if len(riders) == 1:
        return riders[0]

    def parts(ins, outs, send_sems, recv_sems):
        i0 = o0 = s0 = 0
        for r in riders:
            ni, no = len(r.operands), len(r.out_shapes)
            yield (r, ins[i0:i0 + ni], outs[o0:o0 + no], send_sems.at[pl.ds(s0, r.n_sems)],
                   recv_sems.at[pl.ds(s0, r.n_sems)])
            i0, o0, s0 = i0 + ni, o0 + no, s0 + r.n_sems

    def start(*refs):
        for r, *own in parts(*refs):
            r.start(*own)

    def finish(*refs):
        for r, *own in parts(*refs):
            r.finish(*own)

    aliases, i0, o0 = [], 0, 0
    for r in riders:
        aliases += [(i0 + src, o0 + dst) for src, dst in r.aliases]
        i0, o0 = i0 + len(r.operands), o0 + len(r.out_shapes)
    return _Rider(tuple(x for r in riders for x in r.operands), tuple(x for r in riders for x in r.out_shapes),
                  sum(r.n_sems for r in riders), start, finish, tuple(aliases))


def _pair_swap(sums):
    n = len(sums)

    def body(*refs):
        ins, outs = refs[:n], refs[n:2 * n]
        send_sems, recv_sems = refs[2 * n:]
        x, y, c, _ = _place()
        cps = [pltpu.make_async_remote_copy(src_ref=ins[w], dst_ref=outs[w], send_sem=send_sems.at[w],
                                            recv_sem=recv_sems.at[w], device_id=(x, y, 1 - c), device_id_type=MESH)
               for w in range(n)]
        for cp in cps:
            cp.start()
        for cp in cps:
            cp.wait_recv()
        for cp in cps:
            cp.wait_send()

    return pl.pallas_call(
        body,
        name="pair_swap",
        in_specs=[_ANY] * n,
        out_specs=[_ANY] * n,
        out_shape=[jax.ShapeDtypeStruct(s.shape, s.dtype) for s in sums],
        scratch_shapes=[pltpu.SemaphoreType.DMA((n,)), pltpu.SemaphoreType.DMA((n,))],
        compiler_params=pltpu.CompilerParams(has_side_effects=True),
    )(*sums)


def _rows_tile(rows, cols):
    tr = rows
    while tr * cols > (1 << 19) and tr % 16 == 0:
        tr //= 2
    return tr


def _chip_sum(name, grad, landed, chip, saxis):
    L = grad.shape[0]
    _, _, R, C = landed.shape
    tr = _rows_tile(R, C)
    nr = R // tr
    if saxis == 2:
        g_idx = lambda l, i, s: (l, i, s[0])
    else:
        g_idx = lambda l, i, s: (l, s[0] * nr + i, 0)

    def body(s_ref, g_ref, l_ref, o_ref):
        tot = ((g_ref[...].astype(F32) + l_ref[0].astype(F32)) + l_ref[1].astype(F32)) + l_ref[2].astype(F32)
        o_ref[...] = tot.astype(o_ref.dtype)

    return pl.pallas_call(
        body,
        name=name,
        grid_spec=pltpu.PrefetchScalarGridSpec(
            num_scalar_prefetch=1,
            grid=(L, nr),
            in_specs=[pl.BlockSpec((None, tr, C), g_idx), pl.BlockSpec((None, 3, tr, C), lambda l, i, s: (l, 0, i, 0))],
            out_specs=pl.BlockSpec((None, tr, C), lambda l, i, s: (l, i, 0)),
        ),
        out_shape=jax.ShapeDtypeStruct((L, R, C), BF16),
        compiler_params=_cp(("parallel", "parallel")),
    )(chip, grad, landed)


def _all_reduce_small(packed):
    R = packed.shape[0]

    def body(p_ref, o_ref, land_ref, send_sems, recv_sems):
        x, y, c, _ = _place()
        me = 4 * x + 2 * y + c
        sends, recvs = [], []
        for r in range(1, N_DEV):
            px, py, pc = x ^ (r >> 2), y ^ ((r >> 1) & 1), c ^ (r & 1)
            cp = pltpu.make_async_remote_copy(src_ref=p_ref, dst_ref=land_ref.at[me], send_sem=send_sems.at[r - 1],
                                              recv_sem=recv_sems.at[r - 1], device_id=(px, py, pc), device_id_type=MESH)
            cp.start()
            sends.append(cp)
            recvs.append(pltpu.make_async_remote_copy(src_ref=p_ref, dst_ref=land_ref.at[4 * px + 2 * py + pc],
                                                      send_sem=send_sems.at[r - 1], recv_sem=recv_sems.at[r - 1],
                                                      device_id=(px, py, pc), device_id_type=MESH))
        land_ref[me] = p_ref[...]
        for cp in recvs:
            cp.wait_recv()
        for cp in sends:
            cp.wait_send()
        acc = land_ref[0]
        for d in range(1, N_DEV):
            acc = acc + land_ref[d]
        o_ref[...] = acc

    vm = pl.BlockSpec(memory_space=pltpu.VMEM)
    return pl.pallas_call(
        body,
        name="all_reduce_small",
        in_specs=[vm],
        out_specs=vm,
        out_shape=jax.ShapeDtypeStruct((R, LANE), F32),
        scratch_shapes=[pltpu.VMEM((N_DEV, R, LANE), F32), pltpu.SemaphoreType.DMA((N_DEV - 1,)),
                        pltpu.SemaphoreType.DMA((N_DEV - 1,))],
        compiler_params=pltpu.CompilerParams(has_side_effects=True, vmem_limit_bytes=VMEM_LIMIT),
    )(packed)


def _adamw(name, w, m, v, grads):
    R, C = w.shape
    tr = _rows_tile(R, C)
    c1 = 1.0 - ADAM_B1 ** ADAM_STEP
    c2 = 1.0 - ADAM_B2 ** ADAM_STEP
    ng = len(grads)

    def body(*refs):
        w_ref, m_ref, v_ref = refs[:3]
        g_refs = refs[3:3 + ng]
        g_ref, d_ref, nm_ref, nv_ref = refs[3 + ng:]
        gv = g_refs[0][...].astype(F32)
        for r in g_refs[1:]:
            gv = gv + r[...].astype(F32)
        g_ref[...] = gv
        nm = ADAM_B1 * m_ref[...] + (1.0 - ADAM_B1) * gv
        nv = ADAM_B2 * v_ref[...] + (1.0 - ADAM_B2) * (gv * gv)
        nm_ref[...] = nm
        nv_ref[...] = nv
        d_ref[...] = -ADAM_LR * ((nm / c1) / (jnp.sqrt(nv / c2) + ADAM_EPS) + ADAM_WD * w_ref[...])

    blk = pl.BlockSpec((tr, C), lambda i: (i, 0))
    out = jax.ShapeDtypeStruct((R, C), F32)
    return pl.pallas_call(
        body,
        name=name,
        grid=(R // tr,),
        in_specs=[blk] * (3 + ng),
        out_specs=[blk] * 4,
        out_shape=[out] * 4,
        compiler_params=_cp(("parallel",)),
    )(w, m, v, *grads)


def _pack(parts):
    rows = []
    for p in parts:
        flat = p.reshape(-1).astype(F32)
        n = -(-flat.shape[0] // (8 * LANE)) * (8 * LANE)
        rows.append(jnp.pad(flat, (0, n - flat.shape[0])).reshape(n // LANE, LANE))
    return jnp.concatenate(rows, axis=0)


def _unpack(packed, like):
    out, r = [], 0
    for p in like:
        size = int(np.prod(p.shape))
        n = -(-size // (8 * LANE)) * 8
        out.append(packed[r:r + n].reshape(-1)[:size].reshape(p.shape))
        r += n
    return out


def kernel(x, mix_norm, ffn_norm, w_ffn_in, w_ffn_out, ab_w_in, ab_gn_gain, ab_w_pool, ab_pool_scale, ab_w_out, c_w_qkv, c_rel_bias, c_w_out, final_norm, loss_target, m_mix_norm, m_ffn_norm, m_w_ffn_in, m_w_ffn_out, m_ab_w_in, m_ab_gn_gain, m_ab_w_pool, m_ab_pool_scale, m_ab_w_out, m_c_w_qkv, m_c_rel_bias, m_c_w_out, m_final_norm, v_mix_norm, v_ffn_norm, v_w_ffn_in, v_w_ffn_out, v_ab_w_in, v_ab_gn_gain, v_ab_w_pool, v_ab_pool_scale, v_ab_w_out, v_c_w_qkv, v_c_rel_bias, v_c_w_out, v_final_norm):
    w = dict(mix_norm=mix_norm, ffn_norm=ffn_norm, w_ffn_in=w_ffn_in, w_ffn_out=w_ffn_out, ab_w_in=ab_w_in,
             ab_gn_gain=ab_gn_gain, ab_w_pool=ab_w_pool, ab_pool_scale=ab_pool_scale, ab_w_out=ab_w_out,
             c_w_qkv=c_w_qkv, c_rel_bias=c_rel_bias, c_w_out=c_w_out, final_norm=final_norm)
    m = dict(mix_norm=m_mix_norm, ffn_norm=m_ffn_norm, w_ffn_in=m_w_ffn_in, w_ffn_out=m_w_ffn_out, ab_w_in=m_ab_w_in,
             ab_gn_gain=m_ab_gn_gain, ab_w_pool=m_ab_w_pool, ab_pool_scale=m_ab_pool_scale, ab_w_out=m_ab_w_out,
             c_w_qkv=m_c_w_qkv, c_rel_bias=m_c_rel_bias, c_w_out=m_c_w_out, final_norm=m_final_norm)
    v = dict(mix_norm=v_mix_norm, ffn_norm=v_ffn_norm, w_ffn_in=v_w_ffn_in, w_ffn_out=v_w_ffn_out, ab_w_in=v_ab_w_in,
             ab_gn_gain=v_ab_gn_gain, ab_w_pool=v_ab_w_pool, ab_pool_scale=v_ab_pool_scale, ab_w_out=v_ab_w_out,
             c_w_qkv=v_c_w_qkv, c_rel_bias=v_c_rel_bias, c_w_out=v_c_w_out, final_norm=v_final_norm)
    S = x.shape[1]
    cx, cy, cc = lax.axis_index("x"), lax.axis_index("y"), lax.axis_index("c")
    chip = jnp.reshape(2 * cx + cy, (1,)).astype(jnp.int32)

    big = _Weights({k: w[k].astype(BF16) for k in _BIG})
    small = {k: w[k] for k in _SMALL}
    loss, grad_x, g_small, g_big, landed = _local_step(x.reshape(S, D_MODEL), loss_target.reshape(S, D_MODEL), small, big)

    sums = [_chip_sum(f"chip_sum_{k}", g_big[k], landed[k], chip, _SHARD_AXIS[k]) for k in _BIG]
    siblings = _pair_swap(sums)

    packed = _all_reduce_small(_pack([g_small[k] for k in _SMALL] + [loss]))
    small_like = [w[k] for k in _SMALL]
    g_red = dict(zip(_SMALL, _unpack(packed, small_like)))
    loss_row = packed.shape[0] - 8
    loss_out = packed[loss_row, 0]

    grad, delta, new_m, new_v = {}, {}, {}, {}
    for k, mine, theirs in zip(_BIG, sums, siblings):
        shp = w[k].shape
        two = (shp[0] * shp[1], shp[2])
        outs = _adamw(f"adamw_{k}", w[k].reshape(two), m[k].reshape(two), v[k].reshape(two),
                      (mine.reshape(two), theirs.reshape(two)))
        grad[k], delta[k], new_m[k], new_v[k] = [o.reshape(shp) for o in outs]
    _, d, nm, nv = _adamw("adamw_small", _pack(small_like), _pack([m[k] for k in _SMALL]), _pack([v[k] for k in _SMALL]),
                          (packed[:loss_row],))
    for k, dk, mk, vk in zip(_SMALL, _unpack(d, small_like), _unpack(nm, small_like), _unpack(nv, small_like)):
        grad[k], delta[k], new_m[k], new_v[k] = g_red[k], dk, mk, vk

    order = ("mix_norm", "ffn_norm", "w_ffn_in", "w_ffn_out", "ab_w_in", "ab_gn_gain", "ab_w_pool", "ab_pool_scale",
             "ab_w_out", "c_w_qkv", "c_rel_bias", "c_w_out", "final_norm")
    return (loss_out, grad_x.reshape(x.shape), *[grad[k] for k in order], *[delta[k] for k in order],
            *[new_m[k] for k in order], *[new_v[k] for k in order])
```

```python
import functools
from typing import Callable, NamedTuple

import numpy as np
import jax
import jax.numpy as jnp
from jax import lax
from jax.experimental import pallas as pl
from jax.experimental.pallas import tpu as pltpu

F32 = jnp.float32
BF16 = jnp.bfloat16

D_MODEL = 1024
D_FF = 4096
DEPTH = 4
CHUNK = 64
RMS_EPS = 1e-6
RET_WIDTH = 512
RET_HEADS = 4
RET_HEAD_DIM = 128
RET_ROPE_BASE = 10000.0
GN_EPS = 1e-5
POOL_WIDTH = 512
POOL_WINDOWS = (2, 4, 8, 16)
POOL_GROUP_DIM = 128
POOL_HALO = 16
AB_IN_WIDTH = 2560
ATT_HEADS = 16
ATT_HEAD_DIM = 64
LEFT_CHUNKS = 8
BAND = (LEFT_CHUNKS + 1) * CHUNK
REL_CLIP = 128
N_REL = 2 * REL_CLIP + 1
N_REL_PAD = 264
NEG_INF = -1e30
KSCALE = RET_HEAD_DIM ** -0.5
QSCALE = ATT_HEAD_DIM ** -0.5

ADAM_LR = 0.001
ADAM_B1 = 0.9
ADAM_B2 = 0.999
ADAM_EPS = 1e-08
ADAM_WD = 0.01
ADAM_STEP = 10

ATT_BLOCK = LEFT_CHUNKS * CHUNK
RET_BLOCK = 512
N_CHIPS = 4
N_DEV = 8
LANE = 128
VMEM_LIMIT = 52 * 1024 * 1024
EPI_ROWS = 256
MESH = pl.DeviceIdType.MESH


def _cp(sem, vmem=VMEM_LIMIT):
    return pltpu.CompilerParams(dimension_semantics=sem, vmem_limit_bytes=vmem)


def _dot(a, b):
    return lax.dot_general(a, b, (((1,), (0,)), ((), ())), preferred_element_type=F32)


def _dot_nt(a, b):
    return lax.dot_general(a, b, (((1,), (1,)), ((), ())), preferred_element_type=F32)


def _dot_tn(a, b):
    return lax.dot_general(a, b, (((0,), (0,)), ((), ())), preferred_element_type=F32)


_ANY = pl.BlockSpec(memory_space=pl.ANY)


class _Rider(NamedTuple):
    operands: tuple
    out_shapes: tuple
    n_sems: int
    start: Callable
    finish: Callable
    aliases: tuple = ()
    relay: Callable = None


def _mm(name, mode, a, b, *, la=None, lb=None, tm=1024, tn=1024, tk=1024, a_fn=None, b_fn=None,
        extras=(), aux=(), sides=(), epi=None, out_dtype=F32, stack=None, rider=None):
    a_parts = list(a) if isinstance(a, (list, tuple)) else [a]
    b_parts = list(b) if isinstance(b, (list, tuple)) else [b]
    na, nbp = len(a_parts), len(b_parts)
    a2, b2 = list(a_parts[0].shape[-2:]), list(b_parts[0].shape[-2:])
    a2[1] *= na
    b2[1] *= nbp
    if mode == "nn":
        (M, K), (K2, N) = a2, b2
    elif mode == "nt":
        (M, K), (N, K2) = a2, b2
    else:
        (K, M), (K2, N) = a2, b2
    assert K == K2, (name, a2, b2)
    tm, tn, tk = min(tm, M), min(tn, N), min(tk, K)
    assert M % tm == 0 and N % tn == 0 and K % tk == 0, (name, M, N, K, tm, tn, tk)
    gm, gn, gk = M // tm, N // tn, K // tk
    fold = mode == "nt" and na > 1 and gk == 1

    def specs(parts, block, idx, lead):
        per = parts[0].shape[-1] // block[1]
        assert parts[0].shape[-1] % block[1] == 0, (name, parts[0].shape, block)
        out = []
        for p in range(len(parts)):
            def f(i, j, k, p=p):
                r, c = idx(i, j, k)
                if len(parts) > 1:
                    c = jnp.clip(c - p * per, 0, per - 1)
                return (r, c) if lead is None else (lead, r, c)
            out.append(pl.BlockSpec(block if lead is None else (None,) + block, f))
        return out, per

    if mode == "nn":
        a_specs, a_per = specs(a_parts, (tm, tk), lambda i, j, k: (i, k), la)
        b_specs, b_per = specs(b_parts, (tk, tn), lambda i, j, k: (k, j), lb)
        a_axis, b_axis, dot = 2, 1, _dot
    elif mode == "nt":
        if fold:
            a_specs, a_per = [pl.BlockSpec((tm, K // na), lambda i, j, k: (i, 0)) for _ in a_parts], 1
        else:
            a_specs, a_per = specs(a_parts, (tm, tk), lambda i, j, k: (i, k), la)
        b_specs, b_per = specs(b_parts, (tn, tk), lambda i, j, k: (j, k), lb)
        a_axis, b_axis, dot = 2, 2, _dot_nt
    else:
        a_specs, a_per = specs(a_parts, (tk, tm), lambda i, j, k: (k, i), la)
        b_specs, b_per = specs(b_parts, (tk, tn), lambda i, j, k: (k, j), lb)
        a_axis, b_axis, dot = 0, 1, _dot_tn
    ex_specs = [pl.BlockSpec((tm, tn), lambda i, j, k: (i, j)) for _ in extras]
    n_ex = len(extras)

    n_aux, n_side = len(aux), len(sides)
    operands = a_parts + b_parts + list(extras) + list(aux)
    in_specs = a_specs + b_specs + ex_specs + [pl.BlockSpec(v.shape, lambda i, j, k, nd=v.ndim: (0,) * nd) for v in aux]
    aliases = {}
    if stack is None:
        out_specs = [pl.BlockSpec((tm, tn), lambda i, j, k: (i, j))]
        out_shapes = [jax.ShapeDtypeStruct((M, N), out_dtype)]
    else:
        n_layers, layer, prev = stack
        out_specs = [pl.BlockSpec((None, tm, tn), lambda i, j, k: (layer, i, j))]
        out_shapes = [jax.ShapeDtypeStruct((n_layers, M, N), out_dtype)]
        if prev is not None:
            aliases = {len(operands): 0}
            operands.append(prev)
            in_specs.append(_ANY)
    for kind, dtype in sides:
        if kind == "tile":
            out_specs.append(pl.BlockSpec((tm, tn), lambda i, j, k: (i, j)))
            out_shapes.append(jax.ShapeDtypeStruct((M, N), dtype))
        else:
            assert gn == 1, name
            out_specs.append(pl.BlockSpec((8, tn), lambda i, j, k: (0, 0)))
            out_shapes.append(jax.ShapeDtypeStruct((8, N), dtype))
    n_prev = len(aliases)
    scratch = [pltpu.VMEM((tm, tn), F32)] if gk > 1 else []
    n_rin = n_rout = 0
    if rider is not None:
        n_rin, n_rout = len(rider.operands), len(rider.out_shapes)
        for src, dst in rider.aliases:
            aliases[len(operands) + src] = 1 + n_side + dst
        operands += list(rider.operands)
        in_specs += [_ANY] * n_rin
        out_specs += [_ANY] * n_rout
        out_shapes += list(rider.out_shapes)
        scratch += [pltpu.SemaphoreType.DMA((rider.n_sems,)), pltpu.SemaphoreType.DMA((rider.n_sems,))]
    assert na == 1 or nbp == 1, name

    def body(*refs):
        a_refs, b_refs = refs[:na], refs[na:na + nbp]
        ex_refs = refs[na + nbp:na + nbp + n_ex + n_aux]
        n_in = na + nbp + n_ex + n_aux + n_prev
        rin = refs[n_in:n_in + n_rin]
        o_ref = refs[n_in + n_rin]
        side_refs = refs[n_in + n_rin + 1:n_in + n_rin + 1 + n_side]
        rout = refs[n_in + n_rin + 1 + n_side:n_in + n_rin + 1 + n_side + n_rout]
        rest = refs[n_in + n_rin + 1 + n_side + n_rout:]
        i, j, k = pl.program_id(0), pl.program_id(1), pl.program_id(2)
        if rider is not None:
            sems = rest[-2:]

            @pl.when(jnp.logical_and(i == 0, jnp.logical_and(j == 0, k == 0)))
            def _():
                rider.start(rin, rout, *sems)

        def finish(acc):
            if epi is None:
                o_ref[...] = acc[...].astype(o_ref.dtype)
                return
            strip = min(tm, EPI_ROWS)
            colsums = [None] * n_side
            for r0 in range(0, tm, strip):
                rows = slice(r0, r0 + strip)
                res = epi(acc[rows, :], *[r[rows, :] for r in ex_refs[:n_ex]], *[r[...] for r in ex_refs[n_ex:]])
                if n_side:
                    res, *side_vals = res
                    for s, ((kind, _), ref, val) in enumerate(zip(sides, side_refs, side_vals)):
                        if kind == "tile":
                            ref[rows, :] = val.astype(ref.dtype)
                        else:
                            colsums[s] = val if colsums[s] is None else colsums[s] + val
                o_ref[rows, :] = res.astype(o_ref.dtype)
            for (kind, _), ref, val in zip(sides, side_refs, colsums):
                if kind == "colsum":
                    @pl.when(i == 0)
                    def _(ref=ref, val=val):
                        ref[...] = val

                    @pl.when(i > 0)
                    def _(ref=ref, val=val):
                        ref[...] += val

                    @pl.when(i == gm - 1)
                    def _(ref=ref):
                        ref[0:1, :] = jnp.sum(ref[...], axis=0, keepdims=True)

        def step(a_ref, b_ref):
            av, bv = a_ref[...], b_ref[...]
            if a_fn is not None:
                av = a_fn(av)
            if b_fn is not None:
                bv = b_fn(bv)
            part = dot(av.astype(BF16), bv.astype(BF16))
            if gk == 1:
                finish(part)
                return
            acc_ref = rest[0]

            @pl.when(k == 0)
            def _():
                acc_ref[...] = part

            @pl.when(k > 0)
            def _():
                acc_ref[...] += part

        if fold:
            kp = K // na
            finish(sum(dot(a_refs[p][...].astype(BF16), b_refs[0][:, p * kp:(p + 1) * kp].astype(BF16))
                       for p in range(na)))
        elif na > 1:
            sel = pl.program_id(a_axis) // a_per
            for p in range(na):
                pl.when(sel == p)(functools.partial(step, a_refs[p], b_refs[0]))
        elif nbp > 1:
            sel = pl.program_id(b_axis) // b_per
            for p in range(nbp):
                pl.when(sel == p)(functools.partial(step, a_refs[0], b_refs[p]))
        else:
            step(a_refs[0], b_refs[0])
        if gk > 1:
            @pl.when(k == gk - 1)
            def _():
                finish(rest[0])

        if rider is not None:
            steps = gm * gn * gk
            step_no = (i * gn + j) * gk + k
            if rider.relay is not None:
                assert steps >= 3, name

                @pl.when(step_no == steps - 2)
                def _():
                    rider.relay(rin, rout, *sems)

            @pl.when(step_no == steps - 1)
            def _():
                rider.finish(rin, rout, *sems)

    sequential = rider is not None or any(kind == "colsum" for kind, _ in sides)
    sem = ("arbitrary",) * 3 if sequential else ("parallel", "parallel", "arbitrary")
    outs = pl.pallas_call(
        body,
        name=name,
        grid=(gm, gn, gk),
        in_specs=in_specs,
        out_specs=out_specs,
        out_shape=out_shapes,
        input_output_aliases=aliases,
        scratch_shapes=scratch,
        compiler_params=_cp(sem),
    )(*operands)
    res = outs[0] if not sides else tuple(outs[:1 + n_side])
    return res if rider is None else (res, list(outs[1 + n_side:]))


def _relu2(u):
    r = jnp.maximum(u, 0)
    return r * r


def _epi_residual(acc, res):
    return acc + res


def _epi_residual_norm(acc, res, g):
    xn = acc + res
    r = lax.rsqrt(jnp.mean(xn * xn, axis=-1, keepdims=True) + RMS_EPS)
    return xn, (xn * r) * g


def _epi_rms_bwd(dh, x, dres, g):
    r = lax.rsqrt(jnp.mean(x * x, axis=-1, keepdims=True) + RMS_EPS)
    xh = x * r
    dxh = dh * g
    dx = dres + r * (dxh - xh * jnp.mean(dxh * xh, axis=-1, keepdims=True))
    return dx, dx, jnp.sum((dh * xh).reshape(dh.shape[0] // 8, 8, dh.shape[1]), axis=0)


def _rms_fwd(name, x, g):
    S, D = x.shape
    tq = min(1024, S)

    def body(x_ref, g_ref, o_ref):
        xv = x_ref[...]
        r = lax.rsqrt(jnp.mean(xv * xv, axis=-1, keepdims=True) + RMS_EPS)
        o_ref[...] = ((xv * r) * g_ref[...]).astype(o_ref.dtype)

    return pl.pallas_call(
        body,
        name=name,
        grid=(S // tq,),
        in_specs=[pl.BlockSpec((tq, D), lambda i: (i, 0)), pl.BlockSpec((1, D), lambda i: (0, 0))],
        out_specs=pl.BlockSpec((tq, D), lambda i: (i, 0)),
        out_shape=jax.ShapeDtypeStruct((S, D), BF16),
        compiler_params=_cp(("parallel",)),
    )(x, g)


def _loss_head(x, g, t):
    S, D = x.shape
    tq = min(512, S)
    n = S // tq

    def body(x_ref, g_ref, t_ref, loss_ref, dx_ref, dxb_ref, dg_ref, lacc_ref, gacc_ref):
        i = pl.program_id(0)
        xv = x_ref[...]
        gv = g_ref[...]
        r = lax.rsqrt(jnp.mean(xv * xv, axis=-1, keepdims=True) + RMS_EPS)
        xh = xv * r
        e = xh * gv - t_ref[...]
        dy = e * (1.0 / D)
        dxh = dy * gv
        dx = r * (dxh - xh * jnp.mean(dxh * xh, axis=-1, keepdims=True))
        dx_ref[...] = dx
        dxb_ref[...] = dx.astype(dxb_ref.dtype)
        lpart = jnp.sum((e * e).reshape(tq // 8, 8, D), axis=0)
        gpart = jnp.sum((dy * xh).reshape(tq // 8, 8, D), axis=0)

        @pl.when(i == 0)
        def _():
            lacc_ref[...] = lpart
            gacc_ref[...] = gpart

        @pl.when(i > 0)
        def _():
            lacc_ref[...] += lpart
            gacc_ref[...] += gpart

        @pl.when(i == n - 1)
        def _():
            dg_ref[...] = jnp.sum(gacc_ref[...], axis=0, keepdims=True)
            tot = jnp.sum(jnp.sum(lacc_ref[...], axis=0, keepdims=True), axis=1, keepdims=True)
            loss_ref[...] = jnp.broadcast_to(tot * (0.5 / D), (1, LANE))

    return pl.pallas_call(
        body,
        name="loss_head",
        grid=(n,),
        in_specs=[pl.BlockSpec((tq, D), lambda i: (i, 0)), pl.BlockSpec((1, D), lambda i: (0, 0)),
                  pl.BlockSpec((tq, D), lambda i: (i, 0))],
        out_specs=[pl.BlockSpec((1, LANE), lambda i: (0, 0)), pl.BlockSpec((tq, D), lambda i: (i, 0)),
                   pl.BlockSpec((tq, D), lambda i: (i, 0)), pl.BlockSpec((1, D), lambda i: (0, 0))],
        out_shape=[jax.ShapeDtypeStruct((1, LANE), F32), jax.ShapeDtypeStruct((S, D), F32),
                   jax.ShapeDtypeStruct((S, D), BF16), jax.ShapeDtypeStruct((1, D), F32)],
        scratch_shapes=[pltpu.VMEM((8, D), F32), pltpu.VMEM((8, D), F32)],
        compiler_params=_cp(("arbitrary",)),
    )(x, g, t)


def _ret_tables(S):
    T = min(RET_BLOCK, S)
    inv_freq = 1.0 / (RET_ROPE_BASE ** jnp.linspace(0.0, 1.0, RET_HEAD_DIM // 2, dtype=F32))
    ang = jnp.arange(S, dtype=F32)[:, None] * inv_freq[None, :]
    cos, sin = jnp.cos(ang), jnp.sin(ang)
    cosf = jnp.repeat(cos, 2, axis=-1)
    sins = jnp.stack([-sin, sin], axis=-1).reshape(S, RET_HEAD_DIM)
    log_g = np.log1p(-np.power(2.0, -5.0 - np.arange(RET_HEADS, dtype=np.float64)))
    pos = np.arange(T, dtype=np.float64)
    diff = pos[:, None] - pos[None, :]
    same = (pos[:, None] // CHUNK) == (pos[None, :] // CHUNK)
    seen = same | (diff > 0)
    dmat = np.where(seen[None], np.exp(np.abs(diff)[None] * log_g[:, None, None]), 0.0)
    aq = np.exp((pos[None, :] + 1.0) * log_g[:, None])
    ak = np.exp((T - 1.0 - pos[None, :]) * log_g[:, None])
    lam = np.exp(T * log_g)
    bc = lambda v: jnp.asarray(np.broadcast_to(v[..., None], v.shape + (LANE,)), F32)
    return dict(cos=cosf, sin=sins, dmat=jnp.asarray(dmat, F32), aq=bc(aq), ak=bc(ak),
                lam=jnp.asarray(np.broadcast_to(lam[:, None, None], (RET_HEADS, 1, LANE)), F32))


def _rot(x, cos, sin_s, even):
    sw = jnp.where(even, pltpu.roll(x, LANE - 1, 1), pltpu.roll(x, 1, 1))
    return x * cos + sw * sin_s


def _rot_t(dy, cos, sin_s, even):
    t = dy * sin_s
    return dy * cos + jnp.where(even, pltpu.roll(t, LANE - 1, 1), pltpu.roll(t, 1, 1))


def _ret_specs(T, rev_nb=None):
    blk = (lambda b: b) if rev_nb is None else (lambda b: rev_nb - 1 - b)
    whole = lambda shape: pl.BlockSpec(shape, lambda b: (0,) * len(shape))
    specs = [pl.BlockSpec((T, AB_IN_WIDTH), lambda b: (blk(b), 0)),
             pl.BlockSpec((T, LANE), lambda b: (blk(b), 0)),
             pl.BlockSpec((T, LANE), lambda b: (blk(b), 0)),
             whole((RET_HEADS, T, T)), whole((RET_HEADS, T, LANE)), whole((RET_HEADS, T, LANE)),
             whole((RET_HEADS, 1, LANE)), whole((1, RET_WIDTH))]
    return specs, blk


def _head_views(h, z_ref, tabs, token_refs, head_refs):
    zs = [z_ref.at[:, (o * RET_HEADS + h) * LANE:(o * RET_HEADS + h + 1) * LANE] for o in range(4)]
    hs = slice(h * LANE, (h + 1) * LANE)
    return zs, [t.at[h] for t in tabs], [r.at[:, hs] for r in token_refs], [r.at[h] for r in head_refs]


def _ret_fwd(name, z, tb, gain):
    S = z.shape[0]
    T = min(RET_BLOCK, S)
    nb = S // T
    specs, blk = _ret_specs(T)

    def body(z_ref, cos_r, sin_r, d_all, aq_all, ak_all, lam_all, gain_all, cat_all, opre_all, st_all, state_all):
        @pl.when(pl.program_id(0) == 0)
        def _():
            state_all[...] = jnp.zeros_like(state_all)

        for h in range(RET_HEADS):
            zs, tabs, toks, heads = _head_views(h, z_ref, (d_all, aq_all, ak_all, lam_all),
                                                (gain_all, cat_all, opre_all), (st_all, state_all))
            head(*zs, cos_r, sin_r, *tabs, *toks, *heads)

    def head(zq, zk, zv, zg, cos_r, sin_r, d_r, aq_r, ak_r, lam_r, gain_r, ret_o, opre_o, st_o, state):
        even = (lax.broadcasted_iota(jnp.int32, (T, LANE), 1) & 1) == 0
        c, s = cos_r[...], sin_r[...]
        q = _rot(zq[...].astype(F32), c, s, even)
        k = _rot(zk[...].astype(F32), c, s, even) * KSCALE
        qb, kb, vb = q.astype(BF16), k.astype(BF16), zv[...].astype(BF16)
        p = (_dot_nt(qb, kb) * d_r[...]).astype(BF16)
        st = state[...]
        st_o[...] = st
        o = _dot(p, vb) + _dot((q * aq_r[...]).astype(BF16), st.astype(BF16))
        state[...] = st * lam_r[...] + _dot_tn((k * ak_r[...]).astype(BF16), vb)
        opre_o[...] = o
        mu = jnp.mean(o, axis=-1, keepdims=True)
        d = o - mu
        y = d * lax.rsqrt(jnp.mean(d * d, axis=-1, keepdims=True) + GN_EPS)
        g = zg[...].astype(F32)
        ret_o[...] = ((g * jax.nn.sigmoid(g)) * (y * gain_r[...])).astype(ret_o.dtype)

    out_blk = pl.BlockSpec((T, RET_WIDTH), lambda b: (b, 0))
    return pl.pallas_call(
        body,
        name=name,
        grid=(nb,),
        in_specs=specs,
        out_specs=[out_blk, out_blk, pl.BlockSpec((RET_HEADS, None, LANE, LANE), lambda b: (0, b, 0, 0))],
        out_shape=[jax.ShapeDtypeStruct((S, D_MODEL), BF16), jax.ShapeDtypeStruct((S, RET_WIDTH), F32),
                   jax.ShapeDtypeStruct((RET_HEADS, nb, LANE, LANE), F32)],
        scratch_shapes=[pltpu.VMEM((RET_HEADS, LANE, LANE), F32)],
        compiler_params=_cp(("arbitrary",)),
    )(z, tb["cos"], tb["sin"], tb["dmat"], tb["aq"], tb["ak"], tb["lam"], gain)


def _ret_bwd(name, z, tb, gain, opre, states, dcat):
    S = z.shape[0]
    T = min(RET_BLOCK, S)
    nb = S // T
    specs, blk = _ret_specs(T, rev_nb=nb)
    tok = pl.BlockSpec((T, RET_WIDTH), lambda b: (blk(b), 0))

    def body(z_ref, cos_r, sin_r, d_all, aq_all, ak_all, lam_all, gain_all, opre_all, st_all, dret_all,
             dz_ref, dgain_all, dstate_all):
        @pl.when(pl.program_id(0) == 0)
        def _():
            dstate_all[...] = jnp.zeros_like(dstate_all)
            dgain_all[...] = jnp.zeros_like(dgain_all)

        for h in range(RET_HEADS):
            zs, tabs, toks, heads = _head_views(h, z_ref, (d_all, aq_all, ak_all, lam_all),
                                                (gain_all, opre_all, dret_all, dgain_all), (st_all, dstate_all))
            dzs, _, _, _ = _head_views(h, dz_ref, (), (), ())
            gain_r, opre_r, dret_r, dgain_o = toks
            head(*zs, cos_r, sin_r, *tabs, gain_r, opre_r, heads[0], dret_r, *dzs, dgain_o, heads[1])

    def head(zq, zk, zv, zg, cos_r, sin_r, d_r, aq_r, ak_r, lam_r, gain_r, opre_r, st_r, dret_r,
             dq_o, dk_o, dv_o, dg_o, dgain_o, dstate):
        even = (lax.broadcasted_iota(jnp.int32, (T, LANE), 1) & 1) == 0
        c, s = cos_r[...], sin_r[...]
        aq, ak, dm = aq_r[...], ak_r[...], d_r[...]
        q = _rot(zq[...].astype(F32), c, s, even)
        k = _rot(zk[...].astype(F32), c, s, even) * KSCALE
        qb, kb, vb = q.astype(BF16), k.astype(BF16), zv[...].astype(BF16)
        pb = (_dot_nt(qb, kb) * dm).astype(BF16)
        g = zg[...].astype(F32)
        sig = jax.nn.sigmoid(g)
        o = opre_r[...]
        mu = jnp.mean(o, axis=-1, keepdims=True)
        d = o - mu
        rstd = lax.rsqrt(jnp.mean(d * d, axis=-1, keepdims=True) + GN_EPS)
        y = d * rstd
        gain_v = gain_r[...]
        dret = dret_r[...].astype(F32)
        dyg = dret * (g * sig)
        dg_o[...] = (dret * (y * gain_v) * (sig * (1.0 + g * (1.0 - sig)))).astype(dg_o.dtype)
        dgain_o[...] += jnp.sum(dyg * y, axis=0, keepdims=True)
        dy = dyg * gain_v
        do = rstd * (dy - jnp.mean(dy, axis=-1, keepdims=True) - y * jnp.mean(dy * y, axis=-1, keepdims=True))
        dob = do.astype(BF16)
        stb = st_r[...].astype(BF16)
        dsn = dstate[...]
        dsnb = dsn.astype(BF16)
        dpb = (_dot_nt(dob, vb) * dm).astype(BF16)
        dq = _dot(dpb, kb) + _dot_nt(dob, stb) * aq
        dk = _dot_tn(dpb, qb) + _dot_nt(vb, dsnb) * ak
        dv = _dot_tn(pb, dob) + _dot((k * ak).astype(BF16), dsnb)
        dstate[...] = dsn * lam_r[...] + _dot_tn((q * aq).astype(BF16), dob)
        dq_o[...] = _rot_t(dq, c, s, even).astype(dq_o.dtype)
        dk_o[...] = _rot_t(dk * KSCALE, c, s, even).astype(dk_o.dtype)
        dv_o[...] = dv.astype(dv_o.dtype)

    return pl.pallas_call(
        body,
        name=name,
        grid=(nb,),
        in_specs=specs + [tok, pl.BlockSpec((RET_HEADS, None, LANE, LANE), lambda b: (0, blk(b), 0, 0)), tok],
        out_specs=[pl.BlockSpec((T, 4 * RET_WIDTH), lambda b: (blk(b), 0)), pl.BlockSpec((1, RET_WIDTH), lambda b: (0, 0))],
        out_shape=[jax.ShapeDtypeStruct((S, AB_IN_WIDTH), BF16), jax.ShapeDtypeStruct((1, RET_WIDTH), F32)],
        scratch_shapes=[pltpu.VMEM((RET_HEADS, LANE, LANE), F32)],
        compiler_params=_cp(("arbitrary",)),
    )(z, tb["cos"], tb["sin"], tb["dmat"], tb["aq"], tb["ak"], tb["lam"], gain, opre, states, dcat)


def _pool_counts(t0, rows):
    t = t0 + lax.broadcasted_iota(jnp.int32, (rows, POOL_WIDTH), 0)
    grp = lax.broadcasted_iota(jnp.int32, (rows, POOL_WIDTH), 1) >> 7
    win = jnp.where(grp == 0, POOL_WINDOWS[0], jnp.where(grp == 1, POOL_WINDOWS[1],
                    jnp.where(grp == 2, POOL_WINDOWS[2], POOL_WINDOWS[3])))
    return jnp.maximum(jnp.minimum(t + 1, win), 1).astype(F32), grp


def _window_sums(ext, grp, sign):
    n = ext.shape[0]
    sh = lambda v, k: pltpu.roll(v, k % n if sign > 0 else (n - k) % n, 0)
    s2 = ext + sh(ext, 1)
    s4 = s2 + sh(s2, 2)
    s8 = s4 + sh(s4, 4)
    s16 = s8 + sh(s8, 8)
    return jnp.where(grp == 0, s2, jnp.where(grp == 1, s4, jnp.where(grp == 2, s8, s16)))


def _pool_fwd(name, z, w_pool, scale, cat):
    S = z.shape[0]
    T = min(512, S)
    nb = S // T
    pcol = AB_IN_WIDTH // POOL_WIDTH - 1
    hb = T // POOL_HALO

    def body(p_ref, halo_ref, w_ref, sc_ref, cat_in, out_ref, pooled_ref):
        b = pl.program_id(0)
        cur = p_ref[...].astype(F32)
        halo = jnp.where(b > 0, halo_ref[...].astype(F32), 0.0)
        ext = jnp.concatenate([halo, cur], axis=0)
        cnt, grp = _pool_counts(b * T - POOL_HALO, T + POOL_HALO)
        sums = _window_sums(ext, grp, +1)
        pooled = (sums / cnt)[POOL_HALO:] - cur
        pb = pooled.astype(BF16)
        pooled_ref[...] = pb
        for gi in range(len(POOL_WINDOWS)):
            cs = slice(gi * POOL_GROUP_DIM, (gi + 1) * POOL_GROUP_DIM)
            mixed = _dot(pb[:, cs], w_ref[gi].astype(BF16))
            out_ref[:, cs] = (mixed * sc_ref[:, cs]).astype(out_ref.dtype)

    return pl.pallas_call(
        body,
        name=name,
        grid=(nb,),
        in_specs=[pl.BlockSpec((T, POOL_WIDTH), lambda b: (b, pcol)),
                  pl.BlockSpec((POOL_HALO, POOL_WIDTH), lambda b: (jnp.maximum(b * hb - 1, 0), pcol)),
                  pl.BlockSpec((4, POOL_GROUP_DIM, POOL_GROUP_DIM), lambda b: (0, 0, 0)),
                  pl.BlockSpec((1, POOL_WIDTH), lambda b: (0, 0)), _ANY],
        out_specs=[pl.BlockSpec((T, POOL_WIDTH), lambda b: (b, 1)), pl.BlockSpec((T, POOL_WIDTH), lambda b: (b, 0))],
        out_shape=[jax.ShapeDtypeStruct(cat.shape, cat.dtype), jax.ShapeDtypeStruct((S, POOL_WIDTH), BF16)],
        input_output_aliases={4: 0},
        compiler_params=_cp(("parallel",)),
    )(z, z, w_pool, scale, cat)


def _pool_bwd(name, pooled, w_pool, scale, dcat, dz):
    S = pooled.shape[0]
    T = min(512, S)
    nb = S // T
    hb = T // POOL_HALO
    last_h = S // POOL_HALO - 1
    pcol = AB_IN_WIDTH // POOL_WIDTH - 1

    def body(d_ref, dn_ref, pooled_ref, w_ref, sc_ref, dz_in, dp_ref, dw_ref, dsc_ref):
        b = pl.program_id(0)

        @pl.when(b == 0)
        def _():
            dw_ref[...] = jnp.zeros_like(dw_ref)
            dsc_ref[...] = jnp.zeros_like(dsc_ref)

        sc = sc_ref[...]
        dout = d_ref[...].astype(F32)
        dnext = jnp.where(b < nb - 1, dn_ref[...].astype(F32), 0.0)
        dmix = jnp.concatenate([dout, dnext], axis=0) * sc
        dmb = dmix.astype(BF16)
        pb = pooled_ref[...]
        dpooled = []
        for gi in range(len(POOL_WINDOWS)):
            cs = slice(gi * POOL_GROUP_DIM, (gi + 1) * POOL_GROUP_DIM)
            wb = w_ref[gi].astype(BF16)
            dpooled.append(_dot_nt(dmb[:, cs], wb))
            dw_ref[gi] += _dot_tn(pb[:, cs], dmb[:T, cs])
            mixed = _dot(pb[:, cs], wb)
            dsc_ref[:, cs] += jnp.sum(dout[:, cs] * mixed, axis=0, keepdims=True)
        dpl = jnp.concatenate(dpooled, axis=1)
        cnt, grp = _pool_counts(b * T, T + POOL_HALO)
        sums = _window_sums(dpl / cnt, grp, -1)
        dp_ref[...] = (sums[:T] - dpl[:T]).astype(dp_ref.dtype)

    return pl.pallas_call(
        body,
        name=name,
        grid=(nb,),
        in_specs=[pl.BlockSpec((T, POOL_WIDTH), lambda b: (b, 1)),
                  pl.BlockSpec((POOL_HALO, POOL_WIDTH), lambda b: (jnp.minimum((b + 1) * hb, last_h), 1)),
                  pl.BlockSpec((T, POOL_WIDTH), lambda b: (b, 0)),
                  pl.BlockSpec((4, POOL_GROUP_DIM, POOL_GROUP_DIM), lambda b: (0, 0, 0)),
                  pl.BlockSpec((1, POOL_WIDTH), lambda b: (0, 0)), _ANY],
        out_specs=[pl.BlockSpec((T, POOL_WIDTH), lambda b: (b, pcol)),
                   pl.BlockSpec((4, POOL_GROUP_DIM, POOL_GROUP_DIM), lambda b: (0, 0, 0)),
                   pl.BlockSpec((1, POOL_WIDTH), lambda b: (0, 0))],
        out_shape=[jax.ShapeDtypeStruct(dz.shape, dz.dtype),
                   jax.ShapeDtypeStruct((4, POOL_GROUP_DIM, POOL_GROUP_DIM), F32),
                   jax.ShapeDtypeStruct((1, POOL_WIDTH), F32)],
        input_output_aliases={5: 0},
        compiler_params=_cp(("arbitrary",)),
    )(dcat, dcat, pooled, w_pool, scale, dz)


ATT_STRIP = 32
ATT_Q = 256
ATT_W = ATT_Q + LEFT_CHUNKS * CHUNK


def _rel_index():
    j = np.arange(ATT_W)
    rel = np.clip(LEFT_CHUNKS * CHUNK - j, -REL_CLIP, REL_CLIP) + REL_CLIP
    fwd = np.where(j < BAND, rel, N_REL)
    bwd = np.where(j <= ATT_W - CHUNK, fwd, 2 * REL_CLIP)
    return tuple(jnp.asarray(v.reshape(1, ATT_W), jnp.int32) for v in (fwd, bwd))


def _bias_table(name, rel_bias, rel_idx):
    rb = jnp.concatenate([rel_bias, jnp.full((ATT_HEADS, 1), NEG_INF, F32),
                          jnp.zeros((ATT_HEADS, N_REL_PAD - N_REL - 1), F32)], axis=1)

    def body(rb_ref, idx_ref, o_ref, row0_ref):
        r = lax.broadcasted_iota(jnp.int32, (N_REL_PAD, ATT_W), 0)
        onehot = (r == idx_ref[...]).astype(F32)
        row0_ref[...] = jnp.dot(rb_ref[...], onehot, precision=lax.Precision.HIGHEST, preferred_element_type=F32)
        col = lax.broadcasted_iota(jnp.int32, (CHUNK, ATT_W), 1)
        row = lax.broadcasted_iota(jnp.int32, (CHUNK, ATT_W), 0)
        for h in range(ATT_HEADS):
            same = jnp.broadcast_to(row0_ref[pl.ds(h, 1), :], (CHUNK, ATT_W))
            turned = pltpu.roll(same, 0, 1, stride=1, stride_axis=0)
            o_ref[h] = jnp.where(col >= BAND, NEG_INF, jnp.where(col < row, same, turned))

    return pl.pallas_call(
        body,
        name=name,
        out_shape=jax.ShapeDtypeStruct((ATT_HEADS, CHUNK, ATT_W), F32),
        scratch_shapes=[pltpu.VMEM((ATT_HEADS, ATT_W), F32)],
        compiler_params=pltpu.CompilerParams(vmem_limit_bytes=VMEM_LIMIT),
    )(rb, rel_idx)


def _bias_grad(name, dband, rel_idx):
    def body(d_ref, idx_ref, o_ref, sums_ref):
        row = lax.broadcasted_iota(jnp.int32, (CHUNK, ATT_W), 0)
        for h in range(ATT_HEADS):
            back = d_ref[h]
            for bit in range(CHUNK.bit_length() - 1):
                back = jnp.where(((row >> bit) & 1) == 1, pltpu.roll(back, ATT_W - (1 << bit), 1), back)
            sums_ref[pl.ds(h, 1), :] = jnp.sum(back, axis=0, keepdims=True)
        r = lax.broadcasted_iota(jnp.int32, (N_REL_PAD, ATT_W), 0)
        onehot = (r == idx_ref[...]).astype(F32)
        o_ref[...] = lax.dot_general(sums_ref[...], onehot, (((1,), (1,)), ((), ())),
                                     precision=lax.Precision.HIGHEST, preferred_element_type=F32)

    out = pl.pallas_call(
        body,
        name=name,
        out_shape=jax.ShapeDtypeStruct((ATT_HEADS, N_REL_PAD), F32),
        scratch_shapes=[pltpu.VMEM((ATT_HEADS, ATT_W), F32)],
        compiler_params=pltpu.CompilerParams(vmem_limit_bytes=VMEM_LIMIT),
    )(dband, rel_idx)
    return out[:, :N_REL]


def _attn_unit(q_ref, kw_ref, bias_ref, e, u, lane):
    mine = (lane < ATT_HEAD_DIM) if e == 0 else (lane >= ATT_HEAD_DIM)
    qm = jnp.where(mine, q_ref[u * ATT_Q:(u + 1) * ATT_Q, :] * QSCALE, 0)
    kw = kw_ref[u * ATT_Q:u * ATT_Q + ATT_W, :]
    s = _dot_nt(qm, kw) + bias_ref[u, e]
    p = jnp.exp(s - jnp.max(s, axis=-1, keepdims=True))
    return p, 1.0 / jnp.sum(p, axis=-1, keepdims=True), qm, kw, mine


def _attn_in_specs(nb):
    T = ATT_BLOCK
    hp = ATT_HEADS // 2
    cur = lambda off: pl.BlockSpec((T, LANE), lambda h, b: (jnp.minimum(b, nb - 1), off + h))
    prev = lambda off: pl.BlockSpec((T, LANE), lambda h, b: (jnp.clip(b - 1, 0, nb - 1), off + h))
    return [cur(0), prev(hp), cur(hp), prev(2 * hp), cur(2 * hp),
            pl.BlockSpec((None, 2, CHUNK, ATT_W), lambda h, b: (h, 0, 0, 0))]


def _spread_bias(bias_ref, bm_ref, block):
    col = lax.broadcasted_iota(jnp.int32, (CHUNK, ATT_W), 1)
    for first in (True, False):
        @pl.when(block == (0 if first else 1))
        def _(first=first):
            for u in range(ATT_BLOCK // ATT_Q):
                for e in range(2):
                    for j in range(ATT_Q // CHUNK):
                        rows = pltpu.roll(bias_ref[e], j * CHUNK, 1)
                        if first:
                            rows = jnp.where(col >= ATT_BLOCK - u * ATT_Q, rows, NEG_INF)
                        bm_ref[u, e, j * CHUNK:(j + 1) * CHUNK, :] = rows


def _attn_fwd(name, qkv, bias):
    S = qkv.shape[0]
    T = ATT_BLOCK
    nb = S // T

    def body(q_ref, kp_ref, kc_ref, vp_ref, vc_ref, band_ref, o_ref, kw_ref, vw_ref, bias_ref, s_ref, p_ref, inv_ref):
        _spread_bias(band_ref, bias_ref, pl.program_id(1))
        kw_ref[0:T] = kp_ref[...]
        kw_ref[T:2 * T] = kc_ref[...]
        vw_ref[0:T] = vp_ref[...]
        vw_ref[T:2 * T] = vc_ref[...]
        lane = lax.broadcasted_iota(jnp.int32, (ATT_Q, LANE), 1)
        for u in range(T // ATT_Q):
            vw = vw_ref[u * ATT_Q:u * ATT_Q + ATT_W, :]
            kw = kw_ref[u * ATT_Q:u * ATT_Q + ATT_W, :]
            outs = []
            for e in range(2):
                mine = (lane < ATT_HEAD_DIM) if e == 0 else (lane >= ATT_HEAD_DIM)
                qm = jnp.where(mine, q_ref[u * ATT_Q:(u + 1) * ATT_Q, :] * QSCALE, 0)
                s_ref[e] = _dot_nt(qm, kw)
                for r in range(ATT_Q // ATT_STRIP):
                    rows = slice(r * ATT_STRIP, (r + 1) * ATT_STRIP)
                    s = s_ref[e, rows, :] + bias_ref[u, e, rows, :]
                    p = jnp.exp(s - jnp.max(s, axis=-1, keepdims=True))
                    inv_ref[e, rows, :] = jnp.broadcast_to(1.0 / jnp.sum(p, axis=-1, keepdims=True), (ATT_STRIP, LANE))
                    p_ref[e, rows, :] = p.astype(BF16)
                outs.append(_dot(p_ref[e], vw) * inv_ref[e])
            o_ref[u * ATT_Q:(u + 1) * ATT_Q, :] = jnp.where(lane < ATT_HEAD_DIM, outs[0], outs[1]).astype(o_ref.dtype)

    return pl.pallas_call(
        body,
        name=name,
        grid=(ATT_HEADS // 2, nb),
        in_specs=_attn_in_specs(nb),
        out_specs=pl.BlockSpec((T, LANE), lambda h, b: (b, h)),
        out_shape=jax.ShapeDtypeStruct((S, D_MODEL), BF16),
        scratch_shapes=[pltpu.VMEM((2 * T, LANE), BF16), pltpu.VMEM((2 * T, LANE), BF16),
                        pltpu.VMEM((T // ATT_Q, 2, ATT_Q, ATT_W), F32), pltpu.VMEM((2, ATT_Q, ATT_W), F32),
                        pltpu.VMEM((2, ATT_Q, ATT_W), BF16), pltpu.VMEM((2, ATT_Q, LANE), F32)],
        compiler_params=_cp(("parallel", "arbitrary")),
    )(qkv, qkv, qkv, qkv, qkv, bias)


def _attn_bwd(name, qkv, bias, do):
    S = qkv.shape[0]
    T = ATT_BLOCK
    nb = S // T

    def body(q_ref, kp_ref, kc_ref, vp_ref, vc_ref, band_ref, do_ref,
             dq_ref, dk_ref, dv_ref, dband_ref, kw_ref, vw_ref, dkw_ref, dvw_ref, bias_ref, dbias_ref):
        b = pl.program_id(1)

        _spread_bias(band_ref, bias_ref, b)

        @pl.when(b == 0)
        def _():
            dbias_ref[...] = jnp.zeros_like(dbias_ref)
            dkw_ref[T:2 * T] = jnp.zeros((T, LANE), F32)
            dvw_ref[T:2 * T] = jnp.zeros((T, LANE), F32)

        dkw_ref[0:T] = dkw_ref[T:2 * T]
        dvw_ref[0:T] = dvw_ref[T:2 * T]
        dkw_ref[T:2 * T] = jnp.zeros((T, LANE), F32)
        dvw_ref[T:2 * T] = jnp.zeros((T, LANE), F32)

        @pl.when(b < nb)
        def _():
            kw_ref[0:T] = kp_ref[...]
            kw_ref[T:2 * T] = kc_ref[...]
            vw_ref[0:T] = vp_ref[...]
            vw_ref[T:2 * T] = vc_ref[...]
            lane = lax.broadcasted_iota(jnp.int32, (ATT_Q, LANE), 1)
            for u in range(T // ATT_Q):
                rows = slice(u * ATT_Q, (u + 1) * ATT_Q)
                win = slice(u * ATT_Q, u * ATT_Q + ATT_W)
                vw = vw_ref[win, :]
                do2 = do_ref[rows, :]
                dqs, dk, dv = [], None, None
                for e in range(2):
                    p, inv, qm, kw, mine = _attn_unit(q_ref, kw_ref, bias_ref, e, u, lane)
                    dom = jnp.where(mine, do2, 0)
                    dp = _dot_nt(dom, vw)
                    delta = jnp.sum(p * dp, axis=-1, keepdims=True) * inv
                    ds = p * ((dp - delta) * inv)
                    dbias_ref[e] += ds
                    dsb = ds.astype(BF16)
                    dqs.append(_dot(dsb, kw))
                    dk_e = _dot_tn(dsb, qm)
                    dv_e = _dot_tn((p * inv).astype(BF16), dom)
                    dk = dk_e if dk is None else dk + dk_e
                    dv = dv_e if dv is None else dv + dv_e
                dq_ref[rows, :] = (jnp.where(lane < ATT_HEAD_DIM, dqs[0], dqs[1]) * QSCALE).astype(dq_ref.dtype)
                dkw_ref[win, :] += dk
                dvw_ref[win, :] += dv

        @pl.when(b > 0)
        def _():
            dk_ref[...] = dkw_ref[0:T].astype(dk_ref.dtype)
            dv_ref[...] = dvw_ref[0:T].astype(dv_ref.dtype)

        @pl.when(b == nb)
        def _():
            for e in range(2):
                acc = dbias_ref[e, 0:CHUNK, :]
                for j in range(1, ATT_Q // CHUNK):
                    acc = acc + pltpu.roll(dbias_ref[e, j * CHUNK:(j + 1) * CHUNK, :], ATT_W - j * CHUNK, 1)
                dband_ref[e] = acc

    tok = jax.ShapeDtypeStruct((S, D_MODEL), BF16)
    prev_out = pl.BlockSpec((T, LANE), lambda h, b: (jnp.maximum(b - 1, 0), h))
    return pl.pallas_call(
        body,
        name=name,
        grid=(ATT_HEADS // 2, nb + 1),
        in_specs=_attn_in_specs(nb) + [pl.BlockSpec((T, LANE), lambda h, b: (jnp.minimum(b, nb - 1), h))],
        out_specs=[pl.BlockSpec((T, LANE), lambda h, b: (jnp.minimum(b, nb - 1), h)), prev_out, prev_out,
                   pl.BlockSpec((None, 2, CHUNK, ATT_W), lambda h, b: (h, 0, 0, 0))],
        out_shape=[tok, tok, tok, jax.ShapeDtypeStruct((ATT_HEADS // 2, 2, CHUNK, ATT_W), F32)],
        scratch_shapes=[pltpu.VMEM((2 * T, LANE), BF16), pltpu.VMEM((2 * T, LANE), BF16),
                        pltpu.VMEM((2 * T, LANE), F32), pltpu.VMEM((2 * T, LANE), F32),
                        pltpu.VMEM((T // ATT_Q, 2, ATT_Q, ATT_W), F32), pltpu.VMEM((2, ATT_Q, ATT_W), F32)],
        compiler_params=_cp(("parallel", "arbitrary")),
    )(qkv, qkv, qkv, qkv, qkv, bias, do)


def _local_step(x, target, small, W):
    S = x.shape[0]
    tb = _ret_tables(S)
    rel_fwd, rel_bwd = _rel_index()
    saved = []
    normed = (("tile", BF16),)
    deep = dict(tm=512, tk=D_FF)
    h = _rms_fwd("mix_norm_fwd0", x, small["mix_norm"][0:1])
    for layer in range(DEPTH):
        i = layer // 2
        st = {"x_in": x, "h": h}
        g_ffn = small["ffn_norm"][layer:layer + 1]
        if layer % 2 == 0:
            z = W.mm(f"ab_in_fwd{layer}", "nn", h, W.get("ab_w_in", i), tm=2048, tn=640, out_dtype=BF16)
            gain = small["ab_gn_gain"][i:i + 1]
            cat, opre, states = _ret_fwd(f"ret_fwd{layer}", z, tb, gain)
            cat, pooled = _pool_fwd(f"pool_fwd{layer}", z, small["ab_w_pool"][i], small["ab_pool_scale"][i:i + 1], cat)
            st.update(z=z, opre=opre, states=states, pooled=pooled, cat=cat)
            x, hn = W.mm(f"ab_out_fwd{layer}", "nn", cat, W.get("ab_w_out", i), extras=(x,), aux=(g_ffn,), sides=normed,
                         epi=_epi_residual_norm)
        else:
            qkv = W.mm(f"qkv_fwd{layer}", "nn", h, W.get("c_w_qkv", i), tm=2048, out_dtype=BF16)
            bias = _bias_table(f"bias_table{layer}", small["c_rel_bias"][i], rel_fwd)
            bias = bias.reshape(ATT_HEADS // 2, 2, CHUNK, ATT_W)
            att = _attn_fwd(f"attn_fwd{layer}", qkv, bias)
            st.update(qkv=qkv, bias=bias, att=att)
            x, hn = W.mm(f"c_out_fwd{layer}", "nn", att, W.get("c_w_out", i), extras=(x,), aux=(g_ffn,), sides=normed,
                         epi=_epi_residual_norm)
        st["x_mid"] = x
        u = W.mm(f"ffn_in_fwd{layer}", "nn", hn, W.get("w_ffn_in", layer), out_dtype=BF16, tm=2048)
        if layer + 1 < DEPTH:
            x, h = W.mm(f"ffn_out_fwd{layer}", "nn", u, W.get("w_ffn_out", layer), a_fn=_relu2, extras=(x,),
                        aux=(small["mix_norm"][layer + 1:layer + 2],), sides=normed, epi=_epi_residual_norm, **deep)
        else:
            x = W.mm(f"ffn_out_fwd{layer}", "nn", u, W.get("w_ffn_out", layer), a_fn=_relu2, extras=(x,),
                     epi=_epi_residual, **deep)
        st.update(hn=hn, u=u)
        saved.append(st)

    loss, dx, dxb, d_final = _loss_head(x, small["final_norm"].reshape(1, D_MODEL), target)

    gs = {k: [None] * v.shape[0] for k, v in small.items() if k != "final_norm"}
    gb = {k: None for k in W.n_layers}
    landed = {k: None for k in W.n_layers}
    pending = []

    def host(name, *args, take=1, **kw):
        items = [pending.pop(0) for _ in range(min(take, len(pending)))]
        if not items:
            return _mm(name, *args, **kw)
        riders = [_grad_rider(key, idx, gb[key], landed[key]) for key, idx in items]
        res, outs = _mm(name, *args, rider=_join_riders(riders), **kw)
        for (key, _), out in zip(items, outs):
            landed[key] = out
        return res

    def dw(name, key, idx, a, b, call=_mm, **kw):
        gb[key] = call(name, "tn", a, b, stack=(W.n_layers[key], idx, gb[key]), out_dtype=BF16, **kw)
        pending.append((key, idx))

    gain_sums = (("tile", BF16), ("colsum", F32))
    for layer in reversed(range(DEPTH)):
        i = layer // 2
        st = saved[layer]
        du = host(f"ffn_out_bwd{layer}", "nt", dxb, W.get("w_ffn_out", layer), extras=(st["u"],),
                  epi=lambda acc, u: acc * (2.0 * jnp.maximum(u, 0).astype(F32)), out_dtype=BF16, tm=2048)
        dw(f"ffn_out_dw{layer}", "w_ffn_out", layer, st["u"], dxb, a_fn=_relu2, tk=2048)
        dx, dxb, dgain = host(f"ffn_in_bwd{layer}", "nt", du, W.get("w_ffn_in", layer), extras=(st["x_mid"], dx),
                         aux=(small["ffn_norm"][layer:layer + 1],), sides=gain_sums, epi=_epi_rms_bwd, **deep)
        gs["ffn_norm"][layer] = dgain[0:1]
        dw(f"ffn_in_dw{layer}", "w_ffn_in", layer, st["hn"], du, tk=2048)
        norm_bwd = dict(extras=(st["x_in"], dx), aux=(small["mix_norm"][layer:layer + 1],), sides=gain_sums,
                        epi=_epi_rms_bwd)
        if layer % 2 == 0:
            dcat = _mm(f"ab_out_bwd{layer}", "nt", dxb, W.get("ab_w_out", i), out_dtype=BF16)
            dw(f"ab_out_dw{layer}", "ab_w_out", i, st["cat"], dxb, tk=2048)
            gain = small["ab_gn_gain"][i:i + 1]
            dz, gs["ab_gn_gain"][i] = _ret_bwd(f"ret_bwd{layer}", st["z"], tb, gain, st["opre"], st["states"], dcat)
            dz, gs["ab_w_pool"][i], gs["ab_pool_scale"][i] = _pool_bwd(
                f"pool_bwd{layer}", st["pooled"], small["ab_w_pool"][i], small["ab_pool_scale"][i:i + 1], dcat, dz)
            if layer == 0:
                dw(f"ab_in_dw{layer}", "ab_w_in", i, st["h"], dz, call=host, tn=640, tk=2048)
            dx, dxb, dgain = host(f"ab_in_bwd{layer}", "nt", dz, W.get("ab_w_in", i), tm=512, tk=AB_IN_WIDTH,
                             take=len(pending) if layer == 0 else 1, **norm_bwd)
            if layer > 0:
                dw(f"ab_in_dw{layer}", "ab_w_in", i, st["h"], dz, call=host, tn=640, tk=2048)
        else:
            datt = _mm(f"c_out_bwd{layer}", "nt", dxb, W.get("c_w_out", i), out_dtype=BF16)
            dw(f"c_out_dw{layer}", "c_w_out", i, st["att"], dxb, tk=2048)
            dq, dk, dv, dbias = _attn_bwd(f"attn_bwd{layer}", st["qkv"], st["bias"], datt)
            gs["c_rel_bias"][i] = _bias_grad(f"bias_grad{layer}", dbias.reshape(ATT_HEADS, CHUNK, ATT_W), rel_bwd)
            dqkv = [dq, dk, dv]
            dx, dxb, dgain = host(f"qkv_bwd{layer}", "nt", dqkv, W.get("c_w_qkv", i), tm=512, tk=3 * D_MODEL, **norm_bwd)
            dw(f"qkv_dw{layer}", "c_w_qkv", i, st["h"], dqkv, call=host, tk=2048)
        gs["mix_norm"][layer] = dgain[0:1]
    for key, idx in pending:
        landed[key], = _run_rider(f"grad_exchange_{key}{idx}", _grad_rider(key, idx, gb[key], landed[key]))

    g_small = {
        "mix_norm": jnp.concatenate(gs["mix_norm"], axis=0),
        "ffn_norm": jnp.concatenate(gs["ffn_norm"], axis=0),
        "ab_gn_gain": jnp.concatenate(gs["ab_gn_gain"], axis=0),
        "ab_w_pool": jnp.stack(gs["ab_w_pool"], axis=0),
        "ab_pool_scale": jnp.concatenate(gs["ab_pool_scale"], axis=0),
        "c_rel_bias": jnp.stack(gs["c_rel_bias"], axis=0),
        "final_norm": d_final.reshape(D_MODEL),
    }
    return loss, dx, g_small, gb, landed


_BIG = ("w_ffn_in", "w_ffn_out", "ab_w_in", "ab_w_out", "c_w_qkv", "c_w_out")
_SHARD_AXIS = {"w_ffn_in": 2, "w_ffn_out": 1, "ab_w_in": 2, "ab_w_out": 1, "c_w_qkv": 2, "c_w_out": 1}
_SMALL = ("mix_norm", "ffn_norm", "ab_gn_gain", "ab_w_pool", "ab_pool_scale", "c_rel_bias", "final_norm")


def _place():
    x, y, c = lax.axis_index("x"), lax.axis_index("y"), lax.axis_index("c")
    chips = [(1 - x, y), (x, 1 - y), (1 - x, 1 - y)]
    return x, y, c, chips


def _sub(ref, axis, start, size):
    idx = [slice(None)] * len(ref.shape)
    idx[axis] = pl.ds(pl.multiple_of(start, LANE), size)
    return ref.at[tuple(idx)]


def _gather_rider(items, shards):
    keys = sorted({k for k, _ in items})
    n = len(items)
    axes = [_SHARD_AXIS[k] - 1 for k, _ in items]
    sizes = [shards[k].shape[a + 1] for (k, _), a in zip(items, axes)]
    hsizes = [shards[k].shape[2 - a] // 2 for (k, _), a in zip(items, axes)]

    def views(ins, outs, send_sems, recv_sems):
        x, y, c, chips = _place()
        srcs = [ins[keys.index(k)].at[l] for k, l in items]

        def remote(src, dst, s, to):
            return pltpu.make_async_remote_copy(src_ref=src, dst_ref=dst, send_sem=send_sems.at[s],
                                                recv_sem=recv_sems.at[s], device_id=to, device_id_type=MESH)

        def half(w, chip, core):
            return _sub(_sub(outs[w], axes[w], chip * sizes[w], sizes[w]), 1 - axes[w], core * hsizes[w], hsizes[w])

        me = 2 * x + y
        local = [pltpu.make_async_copy(srcs[w], _sub(outs[w], axes[w], me * sizes[w], sizes[w]), send_sems.at[6 * n + w])
                 for w in range(n)]
        first = [remote(_sub(srcs[w], 1 - axes[w], c * hsizes[w], hsizes[w]), half(w, me, c), w * 6 + k, (px, py, c))
                 for w in range(n) for k, (px, py) in enumerate(chips)]
        return x, y, c, chips, remote, half, local, first

    def start(ins, outs, send_sems, recv_sems):
        *_, local, first = views(ins, outs, send_sems, recv_sems)
        for cp in local + first:
            cp.start()

    def passes(x, y, c, chips, remote, half):
        return [remote(half(w, 2 * px + py, c), half(w, 2 * px + py, c), w * 6 + 3 + k, (x, y, 1 - c))
                for w in range(n) for k, (px, py) in enumerate(chips)]

    def relay(ins, outs, send_sems, recv_sems):
        x, y, c, chips, remote, half, _, _ = views(ins, outs, send_sems, recv_sems)
        for w in range(n):
            for k, (px, py) in enumerate(chips):
                landed = half(w, 2 * px + py, c)
                remote(landed, landed, w * 6 + k, (px, py, c)).wait_recv()
        for cp in passes(x, y, c, chips, remote, half):
            cp.start()

    def finish(ins, outs, send_sems, recv_sems):
        x, y, c, chips, remote, half, local, first = views(ins, outs, send_sems, recv_sems)
        for w in range(n):
            for k, (px, py) in enumerate(chips):
                theirs = half(w, 2 * px + py, 1 - c)
                remote(theirs, theirs, w * 6 + 3 + k, (x, y, 1 - c)).wait_recv()
        for cp in first + passes(x, y, c, chips, remote, half):
            cp.wait_send()
        for cp in local:
            cp.wait()

    def full(k, a):
        shape = list(shards[k].shape[1:])
        shape[a] *= N_CHIPS
        return jax.ShapeDtypeStruct(tuple(shape), shards[k].dtype)

    return _Rider(tuple(shards[k] for k in keys), tuple(full(k, a) for (k, _), a in zip(items, axes)), 7 * n, start, finish,
                  relay=relay)


def _mixer_items(layer):
    names = ("ab_w_in", "ab_w_out") if layer % 2 == 0 else ("c_w_qkv", "c_w_out")
    return [(k, layer // 2) for k in names]


class _Weights:
    def __init__(self, shards):
        self.shards = shards
        self.n_layers = {k: shards[k].shape[0] for k in _BIG}
        self.full = {}
        first, second = _mixer_items(0)
        self._take([first], _run_rider("gather_first", _gather_rider([first], shards)))
        self.plan = {"ab_in_fwd0": [second, ("w_ffn_in", 0)], "ab_out_fwd0": [("w_ffn_out", 0)]}
        for layer in range(1, DEPTH):
            proj = "ab_in" if layer % 2 == 0 else "qkv"
            self.plan[f"ffn_in_fwd{layer - 1}"] = _mixer_items(layer)
            self.plan[f"ffn_out_fwd{layer - 1}"] = [("w_ffn_in", layer)]
            self.plan[f"{proj}_fwd{layer}"] = [("w_ffn_out", layer)]

    def _take(self, items, outs):
        self.full.update(zip(items, outs))

    def get(self, name, layer):
        return self.full[(name, layer)]

    def mm(self, name, *args, **kw):
        items = self.plan.get(name)
        if items is None:
            return _mm(name, *args, **kw)
        res, outs = _mm(name, *args, rider=_gather_rider(items, self.shards), **kw)
        self._take(items, outs)
        return res


def _run_rider(name, rider):
    n_in, n_out = len(rider.operands), len(rider.out_shapes)

    def body(*refs):
        ins, outs, sems = refs[:n_in], refs[n_in:n_in + n_out], refs[n_in + n_out:]
        rider.start(ins, outs, *sems)
        if rider.relay is not None:
            rider.relay(ins, outs, *sems)
        rider.finish(ins, outs, *sems)

    return pl.pallas_call(
        body,
        name=name,
        in_specs=[_ANY] * n_in,
        out_specs=[_ANY] * n_out,
        out_shape=list(rider.out_shapes),
        input_output_aliases=dict(rider.aliases),
        scratch_shapes=[pltpu.SemaphoreType.DMA((rider.n_sems,)), pltpu.SemaphoreType.DMA((rider.n_sems,))],
        compiler_params=pltpu.CompilerParams(has_side_effects=True),
    )(*rider.operands)


def _grad_rider(name, layer, grad, landing):
    axis = _SHARD_AXIS[name] - 1
    L, R, C = grad.shape
    shard = (R // N_CHIPS, C) if axis == 0 else (R, C // N_CHIPS)
    size = shard[axis]

    def copies(ins, outs, send_sems, recv_sems):
        x, y, c, chips = _place()
        return [pltpu.make_async_remote_copy(
            src_ref=_sub(ins[0].at[layer], axis, (2 * px + py) * size, size), dst_ref=outs[0].at[layer, k],
            send_sem=send_sems.at[k], recv_sem=recv_sems.at[k], device_id=(px, py, c), device_id_type=MESH)
            for k, (px, py) in enumerate(chips)]

    def start(ins, outs, send_sems, recv_sems):
        for cp in copies(ins, outs, send_sems, recv_sems):
            cp.start()

    def finish(ins, outs, send_sems, recv_sems):
        cps = copies(ins, outs, send_sems, recv_sems)
        for cp in cps:
            cp.wait_recv()
        for cp in cps:
            cp.wait_send()

    out = jax.ShapeDtypeStruct((L, 3) + shard, grad.dtype)
    if landing is None:
        return _Rider((grad,), (out,), 3, start, finish)
    return _Rider((grad, landing), (out,), 3, start, finish, aliases=((1, 0),))


def _join_riders(riders):
    if len(riders) == 1:
        return riders[0]

    def parts(ins, outs, send_sems, recv_sems):
        i0 = o0 = s0 = 0
        for r in riders:
            ni, no = len(r.operands), len(r.out_shapes)
            yield (r, ins[i0:i0 + ni], outs[o0:o0 + no], send_sems.at[pl.ds(s0, r.n_sems)],
                   recv_sems.at[pl.ds(s0, r.n_sems)])
            i0, o0, s0 = i0 + ni, o0 + no, s0 + r.n_sems

    def start(*refs):
        for r, *own in parts(*refs):
            r.start(*own)

    def finish(*refs):
        for r, *own in parts(*refs):
            r.finish(*own)

    aliases, i0, o0 = [], 0, 0
    for r in riders:
        aliases += [(i0 + src, o0 + dst) for src, dst in r.aliases]
        i0, o0 = i0 + len(r.operands), o0 + len(r.out_shapes)
    return _Rider(tuple(x for r in riders for x in r.operands), tuple(x for r in riders for x in r.out_shapes),
                  sum(r.n_sems for r in riders), start, finish, tuple(aliases))


def _pair_swap(sums):
    n = len(sums)

    def body(*refs):
        ins, outs = refs[:n], refs[n:2 * n]
        send_sems, recv_sems = refs[2 * n:]
        x, y, c, _ = _place()
        cps = [pltpu.make_async_remote_copy(src_ref=ins[w], dst_ref=outs[w], send_sem=send_sems.at[w],
                                            recv_sem=recv_sems.at[w], device_id=(x, y, 1 - c), device_id_type=MESH)
               for w in range(n)]
        for cp in cps:
            cp.start()
        for cp in cps:
            cp.wait_recv()
        for cp in cps:
            cp.wait_send()

    return pl.pallas_call(
        body,
        name="pair_swap",
        in_specs=[_ANY] * n,
        out_specs=[_ANY] * n,
        out_shape=[jax.ShapeDtypeStruct(s.shape, s.dtype) for s in sums],
        scratch_shapes=[pltpu.SemaphoreType.DMA((n,)), pltpu.SemaphoreType.DMA((n,))],
        compiler_params=pltpu.CompilerParams(has_side_effects=True),
    )(*sums)


def _rows_tile(rows, cols):
    tr = rows
    while tr * cols > (1 << 19) and tr % 16 == 0:
        tr //= 2
    return tr


def _chip_sum(name, grad, landed, chip, saxis):
    L = grad.shape[0]
    _, _, R, C = landed.shape
    tr = _rows_tile(R, C)
    nr = R // tr
    if saxis == 2:
        g_idx = lambda l, i, s: (l, i, s[0])
    else:
        g_idx = lambda l, i, s: (l, s[0] * nr + i, 0)

    def body(s_ref, g_ref, l_ref, o_ref):
        tot = ((g_ref[...].astype(F32) + l_ref[0].astype(F32)) + l_ref[1].astype(F32)) + l_ref[2].astype(F32)
        o_ref[...] = tot.astype(o_ref.dtype)

    return pl.pallas_call(
        body,
        name=name,
        grid_spec=pltpu.PrefetchScalarGridSpec(
            num_scalar_prefetch=1,
            grid=(L, nr),
            in_specs=[pl.BlockSpec((None, tr, C), g_idx), pl.BlockSpec((None, 3, tr, C), lambda l, i, s: (l, 0, i, 0))],
            out_specs=pl.BlockSpec((None, tr, C), lambda l, i, s: (l, i, 0)),
        ),
        out_shape=jax.ShapeDtypeStruct((L, R, C), BF16),
        compiler_params=_cp(("parallel", "parallel")),
    )(chip, grad, landed)


def _all_reduce_small(packed):
    R = packed.shape[0]

    def body(p_ref, o_ref, land_ref, send_sems, recv_sems):
        x, y, c, _ = _place()
        me = 4 * x + 2 * y + c
        sends, recvs = [], []
        for r in range(1, N_DEV):
            px, py, pc = x ^ (r >> 2), y ^ ((r >> 1) & 1), c ^ (r & 1)
            cp = pltpu.make_async_remote_copy(src_ref=p_ref, dst_ref=land_ref.at[me], send_sem=send_sems.at[r - 1],
                                              recv_sem=recv_sems.at[r - 1], device_id=(px, py, pc), device_id_type=MESH)
            cp.start()
            sends.append(cp)
            recvs.append(pltpu.make_async_remote_copy(src_ref=p_ref, dst_ref=land_ref.at[4 * px + 2 * py + pc],
                                                      send_sem=send_sems.at[r - 1], recv_sem=recv_sems.at[r - 1],
                                                      device_id=(px, py, pc), device_id_type=MESH))
        land_ref[me] = p_ref[...]
        for cp in recvs:
            cp.wait_recv()
        for cp in sends:
            cp.wait_send()
        acc = land_ref[0]
        for d in range(1, N_DEV):
            acc = acc + land_ref[d]
        o_ref[...] = acc

    vm = pl.BlockSpec(memory_space=pltpu.VMEM)
    return pl.pallas_call(
        body,
        name="all_reduce_small",
        in_specs=[vm],
        out_specs=vm,
        out_shape=jax.ShapeDtypeStruct((R, LANE), F32),
        scratch_shapes=[pltpu.VMEM((N_DEV, R, LANE), F32), pltpu.SemaphoreType.DMA((N_DEV - 1,)),
                        pltpu.SemaphoreType.DMA((N_DEV - 1,))],
        compiler_params=pltpu.CompilerParams(has_side_effects=True, vmem_limit_bytes=VMEM_LIMIT),
    )(packed)


def _adamw(name, w, m, v, grads):
    R, C = w.shape
    tr = _rows_tile(R, C)
    c1 = 1.0 - ADAM_B1 ** ADAM_STEP
    c2 = 1.0 - ADAM_B2 ** ADAM_STEP
    ng = len(grads)

    def body(*refs):
        w_ref, m_ref, v_ref = refs[:3]
        g_refs = refs[3:3 + ng]
        g_ref, d_ref, nm_ref, nv_ref = refs[3 + ng:]
        gv = g_refs[0][...].astype(F32)
        for r in g_refs[1:]:
            gv = gv + r[...].astype(F32)
        g_ref[...] = gv
        nm = ADAM_B1 * m_ref[...] + (1.0 - ADAM_B1) * gv
        nv = ADAM_B2 * v_ref[...] + (1.0 - ADAM_B2) * (gv * gv)
        nm_ref[...] = nm
        nv_ref[...] = nv
        d_ref[...] = -ADAM_LR * ((nm / c1) / (jnp.sqrt(nv / c2) + ADAM_EPS) + ADAM_WD * w_ref[...])

    blk = pl.BlockSpec((tr, C), lambda i: (i, 0))
    out = jax.ShapeDtypeStruct((R, C), F32)
    return pl.pallas_call(
        body,
        name=name,
        grid=(R // tr,),
        in_specs=[blk] * (3 + ng),
        out_specs=[blk] * 4,
        out_shape=[out] * 4,
        compiler_params=_cp(("parallel",)),
    )(w, m, v, *grads)


def _pack(parts):
    rows = []
    for p in parts:
        flat = p.reshape(-1).astype(F32)
        n = -(-flat.shape[0] // (8 * LANE)) * (8 * LANE)
        rows.append(jnp.pad(flat, (0, n - flat.shape[0])).reshape(n // LANE, LANE))
    return jnp.concatenate(rows, axis=0)


def _unpack(packed, like):
    out, r = [], 0
    for p in like:
        size = int(np.prod(p.shape))
        n = -(-size // (8 * LANE)) * 8
        out.append(packed[r:r + n].reshape(-1)[:size].reshape(p.shape))
        r += n
    return out


def kernel(x, mix_norm, ffn_norm, w_ffn_in, w_ffn_out, ab_w_in, ab_gn_gain, ab_w_pool, ab_pool_scale, ab_w_out, c_w_qkv, c_rel_bias, c_w_out, final_norm, loss_target, m_mix_norm, m_ffn_norm, m_w_ffn_in, m_w_ffn_out, m_ab_w_in, m_ab_gn_gain, m_ab_w_pool, m_ab_pool_scale, m_ab_w_out, m_c_w_qkv, m_c_rel_bias, m_c_w_out, m_final_norm, v_mix_norm, v_ffn_norm, v_w_ffn_in, v_w_ffn_out, v_ab_w_in, v_ab_gn_gain, v_ab_w_pool, v_ab_pool_scale, v_ab_w_out, v_c_w_qkv, v_c_rel_bias, v_c_w_out, v_final_norm):
    w = dict(mix_norm=mix_norm, ffn_norm=ffn_norm, w_ffn_in=w_ffn_in, w_ffn_out=w_ffn_out, ab_w_in=ab_w_in,
             ab_gn_gain=ab_gn_gain, ab_w_pool=ab_w_pool, ab_pool_scale=ab_pool_scale, ab_w_out=ab_w_out,
             c_w_qkv=c_w_qkv, c_rel_bias=c_rel_bias, c_w_out=c_w_out, final_norm=final_norm)
    m = dict(mix_norm=m_mix_norm, ffn_norm=m_ffn_norm, w_ffn_in=m_w_ffn_in, w_ffn_out=m_w_ffn_out, ab_w_in=m_ab_w_in,
             ab_gn_gain=m_ab_gn_gain, ab_w_pool=m_ab_w_pool, ab_pool_scale=m_ab_pool_scale, ab_w_out=m_ab_w_out,
             c_w_qkv=m_c_w_qkv, c_rel_bias=m_c_rel_bias, c_w_out=m_c_w_out, final_norm=m_final_norm)
    v = dict(mix_norm=v_mix_norm, ffn_norm=v_ffn_norm, w_ffn_in=v_w_ffn_in, w_ffn_out=v_w_ffn_out, ab_w_in=v_ab_w_in,
             ab_gn_gain=v_ab_gn_gain, ab_w_pool=v_ab_w_pool, ab_pool_scale=v_ab_pool_scale, ab_w_out=v_ab_w_out,
             c_w_qkv=v_c_w_qkv, c_rel_bias=v_c_rel_bias, c_w_out=v_c_w_out, final_norm=v_final_norm)
    S = x.shape[1]
    cx, cy, cc = lax.axis_index("x"), lax.axis_index("y"), lax.axis_index("c")
    chip = jnp.reshape(2 * cx + cy, (1,)).astype(jnp.int32)

    big = _Weights({k: w[k].astype(BF16) for k in _BIG})
    small = {k: w[k] for k in _SMALL}
    loss, grad_x, g_small, g_big, landed = _local_step(x.reshape(S, D_MODEL), loss_target.reshape(S, D_MODEL), small, big)

    sums = [_chip_sum(f"chip_sum_{k}", g_big[k], landed[k], chip, _SHARD_AXIS[k]) for k in _BIG]
    siblings = _pair_swap(sums)

    packed = _all_reduce_small(_pack([g_small[k] for k in _SMALL] + [loss]))
    small_like = [w[k] for k in _SMALL]
    g_red = dict(zip(_SMALL, _unpack(packed, small_like)))
    loss_row = packed.shape[0] - 8
    loss_out = packed[loss_row, 0]

    grad, delta, new_m, new_v = {}, {}, {}, {}
    for k, mine, theirs in zip(_BIG, sums, siblings):
        shp = w[k].shape
        two = (shp[0] * shp[1], shp[2])
        outs = _adamw(f"adamw_{k}", w[k].reshape(two), m[k].reshape(two), v[k].reshape(two),
                      (mine.reshape(two), theirs.reshape(two)))
        grad[k], delta[k], new_m[k], new_v[k] = [o.reshape(shp) for o in outs]
    _, d, nm, nv = _adamw("adamw_small", _pack(small_like), _pack([m[k] for k in _SMALL]), _pack([v[k] for k in _SMALL]),
                          (packed[:loss_row],))
    for k, dk, mk, vk in zip(_SMALL, _unpack(d, small_like), _unpack(nm, small_like), _unpack(nv, small_like)):
        grad[k], delta[k], new_m[k], new_v[k] = g_red[k], dk, mk, vk

    order = ("mix_norm", "ffn_norm", "w_ffn_in", "w_ffn_out", "ab_w_in", "ab_gn_gain", "ab_w_pool", "ab_pool_scale",
             "ab_w_out", "c_w_qkv", "c_rel_bias", "c_w_out", "final_norm")
    return (loss_out, grad_x.reshape(x.shape), *[grad[k] for k in order], *[delta[k] for k in order],
            *[new_m[k] for k in order], *[new_v[k] for k in order])
```

```python
import functools
from typing import Callable, NamedTuple

import numpy as np
import jax
import jax.numpy as jnp
from jax import lax
from jax.experimental import pallas as pl
from jax.experimental.pallas import tpu as pltpu

F32 = jnp.float32
BF16 = jnp.bfloat16

D_MODEL = 1024
D_FF = 4096
DEPTH = 4
CHUNK = 64
RMS_EPS = 1e-6
RET_WIDTH = 512
RET_HEADS = 4
RET_HEAD_DIM = 128
RET_ROPE_BASE = 10000.0
GN_EPS = 1e-5
POOL_WIDTH = 512
POOL_WINDOWS = (2, 4, 8, 16)
POOL_GROUP_DIM = 128
POOL_HALO = 16
AB_IN_WIDTH = 2560
ATT_HEADS = 16
ATT_HEAD_DIM = 64
LEFT_CHUNKS = 8
BAND = (LEFT_CHUNKS + 1) * CHUNK
REL_CLIP = 128
N_REL = 2 * REL_CLIP + 1
N_REL_PAD = 264
NEG_INF = -1e30
KSCALE = RET_HEAD_DIM ** -0.5
QSCALE = ATT_HEAD_DIM ** -0.5

ADAM_LR = 0.001
ADAM_B1 = 0.9
ADAM_B2 = 0.999
ADAM_EPS = 1e-08
ADAM_WD = 0.01
ADAM_STEP = 10

ATT_BLOCK = LEFT_CHUNKS * CHUNK
RET_BLOCK = 512
N_CHIPS = 4
N_DEV = 8
LANE = 128
VMEM_LIMIT = 52 * 1024 * 1024
EPI_ROWS = 256
MESH = pl.DeviceIdType.MESH


def _cp(sem, vmem=VMEM_LIMIT):
    return pltpu.CompilerParams(dimension_semantics=sem, vmem_limit_bytes=vmem)


def _dot(a, b):
    return lax.dot_general(a, b, (((1,), (0,)), ((), ())), preferred_element_type=F32)


def _dot_nt(a, b):
    return lax.dot_general(a, b, (((1,), (1,)), ((), ())), preferred_element_type=F32)


def _dot_tn(a, b):
    return lax.dot_general(a, b, (((0,), (0,)), ((), ())), preferred_element_type=F32)


_ANY = pl.BlockSpec(memory_space=pl.ANY)


class _Rider(NamedTuple):
    operands: tuple
    out_shapes: tuple
    n_sems: int
    start: Callable
    finish: Callable
    aliases: tuple = ()
    relay: Callable = None


def _mm(name, mode, a, b, *, la=None, lb=None, tm=1024, tn=1024, tk=1024, a_fn=None, b_fn=None,
        extras=(), aux=(), sides=(), epi=None, out_dtype=F32, stack=None, rider=None):
    a_parts = list(a) if isinstance(a, (list, tuple)) else [a]
    b_parts = list(b) if isinstance(b, (list, tuple)) else [b]
    na, nbp = len(a_parts), len(b_parts)
    a2, b2 = list(a_parts[0].shape[-2:]), list(b_parts[0].shape[-2:])
    a2[1] *= na
    b2[1] *= nbp
    if mode == "nn":
        (M, K), (K2, N) = a2, b2
    elif mode == "nt":
        (M, K), (N, K2) = a2, b2
    else:
        (K, M), (K2, N) = a2, b2
    assert K == K2, (name, a2, b2)
    tm, tn, tk = min(tm, M), min(tn, N), min(tk, K)
    assert M % tm == 0 and N % tn == 0 and K % tk == 0, (name, M, N, K, tm, tn, tk)
    gm, gn, gk = M // tm, N // tn, K // tk
    fold = mode == "nt" and na > 1 and gk == 1

    def specs(parts, block, idx, lead):
        per = parts[0].shape[-1] // block[1]
        assert parts[0].shape[-1] % block[1] == 0, (name, parts[0].shape, block)
        out = []
        for p in range(len(parts)):
            def f(i, j, k, p=p):
                r, c = idx(i, j, k)
                if len(parts) > 1:
                    c = jnp.clip(c - p * per, 0, per - 1)
                return (r, c) if lead is None else (lead, r, c)
            out.append(pl.BlockSpec(block if lead is None else (None,) + block, f))
        return out, per

    if mode == "nn":
        a_specs, a_per = specs(a_parts, (tm, tk), lambda i, j, k: (i, k), la)
        b_specs, b_per = specs(b_parts, (tk, tn), lambda i, j, k: (k, j), lb)
        a_axis, b_axis, dot = 2, 1, _dot
    elif mode == "nt":
        if fold:
            a_specs, a_per = [pl.BlockSpec((tm, K // na), lambda i, j, k: (i, 0)) for _ in a_parts], 1
        else:
            a_specs, a_per = specs(a_parts, (tm, tk), lambda i, j, k: (i, k), la)
        b_specs, b_per = specs(b_parts, (tn, tk), lambda i, j, k: (j, k), lb)
        a_axis, b_axis, dot = 2, 2, _dot_nt
    else:
        a_specs, a_per = specs(a_parts, (tk, tm), lambda i, j, k: (k, i), la)
        b_specs, b_per = specs(b_parts, (tk, tn), lambda i, j, k: (k, j), lb)
        a_axis, b_axis, dot = 0, 1, _dot_tn
    ex_specs = [pl.BlockSpec((tm, tn), lambda i, j, k: (i, j)) for _ in extras]
    n_ex = len(extras)

    n_aux, n_side = len(aux), len(sides)
    operands = a_parts + b_parts + list(extras) + list(aux)
    in_specs = a_specs + b_specs + ex_specs + [pl.BlockSpec(v.shape, lambda i, j, k, nd=v.ndim: (0,) * nd) for v in aux]
    aliases = {}
    if stack is None:
        out_specs = [pl.BlockSpec((tm, tn), lambda i, j, k: (i, j))]
        out_shapes = [jax.ShapeDtypeStruct((M, N), out_dtype)]
    else:
        n_layers, layer, prev = stack
        out_specs = [pl.BlockSpec((None, tm, tn), lambda i, j, k: (layer, i, j))]
        out_shapes = [jax.ShapeDtypeStruct((n_layers, M, N), out_dtype)]
        if prev is not None:
            aliases = {len(operands): 0}
            operands.append(prev)
            in_specs.append(_ANY)
    for kind, dtype in sides:
        if kind == "tile":
            out_specs.append(pl.BlockSpec((tm, tn), lambda i, j, k: (i, j)))
            out_shapes.append(jax.ShapeDtypeStruct((M, N), dtype))
        else:
            assert gn == 1, name
            out_specs.append(pl.BlockSpec((8, tn), lambda i, j, k: (0, 0)))
            out_shapes.append(jax.ShapeDtypeStruct((8, N), dtype))
    n_prev = len(aliases)
    scratch = [pltpu.VMEM((tm, tn), F32)] if gk > 1 else []
    n_rin = n_rout = 0
    if rider is not None:
        n_rin, n_rout = len(rider.operands), len(rider.out_shapes)
        for src, dst in rider.aliases:
            aliases[len(operands) + src] = 1 + n_side + dst
        operands += list(rider.operands)
        in_specs += [_ANY] * n_rin
        out_specs += [_ANY] * n_rout
        out_shapes += list(rider.out_shapes)
        scratch += [pltpu.SemaphoreType.DMA((rider.n_sems,)), pltpu.SemaphoreType.DMA((rider.n_sems,))]
    assert na == 1 or nbp == 1, name

    def body(*refs):
        a_refs, b_refs = refs[:na], refs[na:na + nbp]
        ex_refs = refs[na + nbp:na + nbp + n_ex + n_aux]
        n_in = na + nbp + n_ex + n_aux + n_prev
        rin = refs[n_in:n_in + n_rin]
        o_ref = refs[n_in + n_rin]
        side_refs = refs[n_in + n_rin + 1:n_in + n_rin + 1 + n_side]
        rout = refs[n_in + n_rin + 1 + n_side:n_in + n_rin + 1 + n_side + n_rout]
        rest = refs[n_in + n_rin + 1 + n_side + n_rout:]
        i, j, k = pl.program_id(0), pl.program_id(1), pl.program_id(2)
        if rider is not None:
            sems = rest[-2:]

            @pl.when(jnp.logical_and(i == 0, jnp.logical_and(j == 0, k == 0)))
            def _():
                rider.start(rin, rout, *sems)

        def finish(acc):
            if epi is None:
                o_ref[...] = acc[...].astype(o_ref.dtype)
                return
            strip = min(tm, EPI_ROWS)
            colsums = [None] * n_side
            for r0 in range(0, tm, strip):
                rows = slice(r0, r0 + strip)
                res = epi(acc[rows, :], *[r[rows, :] for r in ex_refs[:n_ex]], *[r[...] for r in ex_refs[n_ex:]])
                if n_side:
                    res, *side_vals = res
                    for s, ((kind, _), ref, val) in enumerate(zip(sides, side_refs, side_vals)):
                        if kind == "tile":
                            ref[rows, :] = val.astype(ref.dtype)
                        else:
                            colsums[s] = val if colsums[s] is None else colsums[s] + val
                o_ref[rows, :] = res.astype(o_ref.dtype)
            for (kind, _), ref, val in zip(sides, side_refs, colsums):
                if kind == "colsum":
                    @pl.when(i == 0)
                    def _(ref=ref, val=val):
                        ref[...] = val

                    @pl.when(i > 0)
                    def _(ref=ref, val=val):
                        ref[...] += val

                    @pl.when(i == gm - 1)
                    def _(ref=ref):
                        ref[0:1, :] = jnp.sum(ref[...], axis=0, keepdims=True)

        def step(a_ref, b_ref):
            av, bv = a_ref[...], b_ref[...]
            if a_fn is not None:
                av = a_fn(av)
            if b_fn is not None:
                bv = b_fn(bv)
            part = dot(av.astype(BF16), bv.astype(BF16))
            if gk == 1:
                finish(part)
                return
            acc_ref = rest[0]

            @pl.when(k == 0)
            def _():
                acc_ref[...] = part

            @pl.when(k > 0)
            def _():
                acc_ref[...] += part

        if fold:
            kp = K // na
            finish(sum(dot(a_refs[p][...].astype(BF16), b_refs[0][:, p * kp:(p + 1) * kp].astype(BF16))
                       for p in range(na)))
        elif na > 1:
            sel = pl.program_id(a_axis) // a_per
            for p in range(na):
                pl.when(sel == p)(functools.partial(step, a_refs[p], b_refs[0]))
        elif nbp > 1:
            sel = pl.program_id(b_axis) // b_per
            for p in range(nbp):
                pl.when(sel == p)(functools.partial(step, a_refs[0], b_refs[p]))
        else:
            step(a_refs[0], b_refs[0])
        if gk > 1:
            @pl.when(k == gk - 1)
            def _():
                finish(rest[0])

        if rider is not None:
            steps = gm * gn * gk
            step_no = (i * gn + j) * gk + k
            if rider.relay is not None:
                assert steps >= 3, name

                @pl.when(step_no == steps - 2)
                def _():
                    rider.relay(rin, rout, *sems)

            @pl.when(step_no == steps - 1)
            def _():
                rider.finish(rin, rout, *sems)

    sequential = rider is not None or any(kind == "colsum" for kind, _ in sides)
    sem = ("arbitrary",) * 3 if sequential else ("parallel", "parallel", "arbitrary")
    outs = pl.pallas_call(
        body,
        name=name,
        grid=(gm, gn, gk),
        in_specs=in_specs,
        out_specs=out_specs,
        out_shape=out_shapes,
        input_output_aliases=aliases,
        scratch_shapes=scratch,
        compiler_params=_cp(sem),
    )(*operands)
    res = outs[0] if not sides else tuple(outs[:1 + n_side])
    return res if rider is None else (res, list(outs[1 + n_side:]))


def _relu2(u):
    r = jnp.maximum(u, 0)
    return r * r


def _epi_residual(acc, res):
    return acc + res


def _epi_residual_norm(acc, res, g):
    xn = acc + res
    r = lax.rsqrt(jnp.mean(xn * xn, axis=-1, keepdims=True) + RMS_EPS)
    return xn, (xn * r) * g


def _epi_rms_bwd(dh, x, dres, g):
    r = lax.rsqrt(jnp.mean(x * x, axis=-1, keepdims=True) + RMS_EPS)
    xh = x * r
    dxh = dh * g
    dx = dres + r * (dxh - xh * jnp.mean(dxh * xh, axis=-1, keepdims=True))
    return dx, dx, jnp.sum((dh * xh).reshape(dh.shape[0] // 8, 8, dh.shape[1]), axis=0)


def _rms_fwd(name, x, g):
    S, D = x.shape
    tq = min(1024, S)

    def body(x_ref, g_ref, o_ref):
        xv = x_ref[...]
        r = lax.rsqrt(jnp.mean(xv * xv, axis=-1, keepdims=True) + RMS_EPS)
        o_ref[...] = ((xv * r) * g_ref[...]).astype(o_ref.dtype)

    return pl.pallas_call(
        body,
        name=name,
        grid=(S // tq,),
        in_specs=[pl.BlockSpec((tq, D), lambda i: (i, 0)), pl.BlockSpec((1, D), lambda i: (0, 0))],
        out_specs=pl.BlockSpec((tq, D), lambda i: (i, 0)),
        out_shape=jax.ShapeDtypeStruct((S, D), BF16),
        compiler_params=_cp(("parallel",)),
    )(x, g)


def _loss_head(x, g, t):
    S, D = x.shape
    tq = min(512, S)
    n = S // tq

    def body(x_ref, g_ref, t_ref, loss_ref, dx_ref, dxb_ref, dg_ref, lacc_ref, gacc_ref):
        i = pl.program_id(0)
        xv = x_ref[...]
        gv = g_ref[...]
        r = lax.rsqrt(jnp.mean(xv * xv, axis=-1, keepdims=True) + RMS_EPS)
        xh = xv * r
        e = xh * gv - t_ref[...]
        dy = e * (1.0 / D)
        dxh = dy * gv
        dx = r * (dxh - xh * jnp.mean(dxh * xh, axis=-1, keepdims=True))
        dx_ref[...] = dx
        dxb_ref[...] = dx.astype(dxb_ref.dtype)
        lpart = jnp.sum((e * e).reshape(tq // 8, 8, D), axis=0)
        gpart = jnp.sum((dy * xh).reshape(tq // 8, 8, D), axis=0)

        @pl.when(i == 0)
        def _():
            lacc_ref[...] = lpart
            gacc_ref[...] = gpart

        @pl.when(i > 0)
        def _():
            lacc_ref[...] += lpart
            gacc_ref[...] += gpart

        @pl.when(i == n - 1)
        def _():
            dg_ref[...] = jnp.sum(gacc_ref[...], axis=0, keepdims=True)
            tot = jnp.sum(jnp.sum(lacc_ref[...], axis=0, keepdims=True), axis=1, keepdims=True)
            loss_ref[...] = jnp.broadcast_to(tot * (0.5 / D), (1, LANE))

    return pl.pallas_call(
        body,
        name="loss_head",
        grid=(n,),
        in_specs=[pl.BlockSpec((tq, D), lambda i: (i, 0)), pl.BlockSpec((1, D), lambda i: (0, 0)),
                  pl.BlockSpec((tq, D), lambda i: (i, 0))],
        out_specs=[pl.BlockSpec((1, LANE), lambda i: (0, 0)), pl.BlockSpec((tq, D), lambda i: (i, 0)),
                   pl.BlockSpec((tq, D), lambda i: (i, 0)), pl.BlockSpec((1, D), lambda i: (0, 0))],
        out_shape=[jax.ShapeDtypeStruct((1, LANE), F32), jax.ShapeDtypeStruct((S, D), F32),
                   jax.ShapeDtypeStruct((S, D), BF16), jax.ShapeDtypeStruct((1, D), F32)],
        scratch_shapes=[pltpu.VMEM((8, D), F32), pltpu.VMEM((8, D), F32)],
        compiler_params=_cp(("arbitrary",)),
    )(x, g, t)


def _ret_tables(S):
    T = min(RET_BLOCK, S)
    inv_freq = 1.0 / (RET_ROPE_BASE ** jnp.linspace(0.0, 1.0, RET_HEAD_DIM // 2, dtype=F32))
    ang = jnp.arange(S, dtype=F32)[:, None] * inv_freq[None, :]
    cos, sin = jnp.cos(ang), jnp.sin(ang)
    cosf = jnp.repeat(cos, 2, axis=-1)
    sins = jnp.stack([-sin, sin], axis=-1).reshape(S, RET_HEAD_DIM)
    log_g = np.log1p(-np.power(2.0, -5.0 - np.arange(RET_HEADS, dtype=np.float64)))
    pos = np.arange(T, dtype=np.float64)
    diff = pos[:, None] - pos[None, :]
    same = (pos[:, None] // CHUNK) == (pos[None, :] // CHUNK)
    seen = same | (diff > 0)
    dmat = np.where(seen[None], np.exp(np.abs(diff)[None] * log_g[:, None, None]), 0.0)
    aq = np.exp((pos[None, :] + 1.0) * log_g[:, None])
    ak = np.exp((T - 1.0 - pos[None, :]) * log_g[:, None])
    lam = np.exp(T * log_g)
    bc = lambda v: jnp.asarray(np.broadcast_to(v[..., None], v.shape + (LANE,)), F32)
    return dict(cos=cosf, sin=sins, dmat=jnp.asarray(dmat, F32), aq=bc(aq), ak=bc(ak),
                lam=jnp.asarray(np.broadcast_to(lam[:, None, None], (RET_HEADS, 1, LANE)), F32))


def _rot(x, cos, sin_s, even):
    sw = jnp.where(even, pltpu.roll(x, LANE - 1, 1), pltpu.roll(x, 1, 1))
    return x * cos + sw * sin_s


def _rot_t(dy, cos, sin_s, even):
    t = dy * sin_s
    return dy * cos + jnp.where(even, pltpu.roll(t, LANE - 1, 1), pltpu.roll(t, 1, 1))


def _ret_specs(T, rev_nb=None):
    blk = (lambda b: b) if rev_nb is None else (lambda b: rev_nb - 1 - b)
    whole = lambda shape: pl.BlockSpec(shape, lambda b: (0,) * len(shape))
    specs = [pl.BlockSpec((T, AB_IN_WIDTH), lambda b: (blk(b), 0)),
             pl.BlockSpec((T, LANE), lambda b: (blk(b), 0)),
             pl.BlockSpec((T, LANE), lambda b: (blk(b), 0)),
             whole((RET_HEADS, T, T)), whole((RET_HEADS, T, LANE)), whole((RET_HEADS, T, LANE)),
             whole((RET_HEADS, 1, LANE)), whole((1, RET_WIDTH))]
    return specs, blk


def _head_views(h, z_ref, tabs, token_refs, head_refs):
    zs = [z_ref.at[:, (o * RET_HEADS + h) * LANE:(o * RET_HEADS + h + 1) * LANE] for o in range(4)]
    hs = slice(h * LANE, (h + 1) * LANE)
    return zs, [t.at[h] for t in tabs], [r.at[:, hs] for r in token_refs], [r.at[h] for r in head_refs]


def _ret_fwd(name, z, tb, gain):
    S = z.shape[0]
    T = min(RET_BLOCK, S)
    nb = S // T
    specs, blk = _ret_specs(T)

    def body(z_ref, cos_r, sin_r, d_all, aq_all, ak_all, lam_all, gain_all, cat_all, opre_all, st_all, state_all):
        @pl.when(pl.program_id(0) == 0)
        def _():
            state_all[...] = jnp.zeros_like(state_all)

        for h in range(RET_HEADS):
            zs, tabs, toks, heads = _head_views(h, z_ref, (d_all, aq_all, ak_all, lam_all),
                                                (gain_all, cat_all, opre_all), (st_all, state_all))
            head(*zs, cos_r, sin_r, *tabs, *toks, *heads)

    def head(zq, zk, zv, zg, cos_r, sin_r, d_r, aq_r, ak_r, lam_r, gain_r, ret_o, opre_o, st_o, state):
        even = (lax.broadcasted_iota(jnp.int32, (T, LANE), 1) & 1) == 0
        c, s = cos_r[...], sin_r[...]
        q = _rot(zq[...].astype(F32), c, s, even)
        k = _rot(zk[...].astype(F32), c, s, even) * KSCALE
        qb, kb, vb = q.astype(BF16), k.astype(BF16), zv[...].astype(BF16)
        p = (_dot_nt(qb, kb) * d_r[...]).astype(BF16)
        st = state[...]
        st_o[...] = st
        o = _dot(p, vb) + _dot((q * aq_r[...]).astype(BF16), st.astype(BF16))
        state[...] = st * lam_r[...] + _dot_tn((k * ak_r[...]).astype(BF16), vb)
        opre_o[...] = o
        mu = jnp.mean(o, axis=-1, keepdims=True)
        d = o - mu
        y = d * lax.rsqrt(jnp.mean(d * d, axis=-1, keepdims=True) + GN_EPS)
        g = zg[...].astype(F32)
        ret_o[...] = ((g * jax.nn.sigmoid(g)) * (y * gain_r[...])).astype(ret_o.dtype)

    out_blk = pl.BlockSpec((T, RET_WIDTH), lambda b: (b, 0))
    return pl.pallas_call(
        body,
        name=name,
        grid=(nb,),
        in_specs=specs,
        out_specs=[out_blk, out_blk, pl.BlockSpec((RET_HEADS, None, LANE, LANE), lambda b: (0, b, 0, 0))],
        out_shape=[jax.ShapeDtypeStruct((S, D_MODEL), BF16), jax.ShapeDtypeStruct((S, RET_WIDTH), F32),
                   jax.ShapeDtypeStruct((RET_HEADS, nb, LANE, LANE), F32)],
        scratch_shapes=[pltpu.VMEM((RET_HEADS, LANE, LANE), F32)],
        compiler_params=_cp(("arbitrary",)),
    )(z, tb["cos"], tb["sin"], tb["dmat"], tb["aq"], tb["ak"], tb["lam"], gain)


def _ret_bwd(name, z, tb, gain, opre, states, dcat):
    S = z.shape[0]
    T = min(RET_BLOCK, S)
    nb = S // T
    specs, blk = _ret_specs(T, rev_nb=nb)
    tok = pl.BlockSpec((T, RET_WIDTH), lambda b: (blk(b), 0))

    def body(z_ref, cos_r, sin_r, d_all, aq_all, ak_all, lam_all, gain_all, opre_all, st_all, dret_all,
             dz_ref, dgain_all, dstate_all):
        @pl.when(pl.program_id(0) == 0)
        def _():
            dstate_all[...] = jnp.zeros_like(dstate_all)
            dgain_all[...] = jnp.zeros_like(dgain_all)

        for h in range(RET_HEADS):
            zs, tabs, toks, heads = _head_views(h, z_ref, (d_all, aq_all, ak_all, lam_all),
                                                (gain_all, opre_all, dret_all, dgain_all), (st_all, dstate_all))
            dzs, _, _, _ = _head_views(h, dz_ref, (), (), ())
            gain_r, opre_r, dret_r, dgain_o = toks
            head(*zs, cos_r, sin_r, *tabs, gain_r, opre_r, heads[0], dret_r, *dzs, dgain_o, heads[1])

    def head(zq, zk, zv, zg, cos_r, sin_r, d_r, aq_r, ak_r, lam_r, gain_r, opre_r, st_r, dret_r,
             dq_o, dk_o, dv_o, dg_o, dgain_o, dstate):
        even = (lax.broadcasted_iota(jnp.int32, (T, LANE), 1) & 1) == 0
        c, s = cos_r[...], sin_r[...]
        aq, ak, dm = aq_r[...], ak_r[...], d_r[...]
        q = _rot(zq[...].astype(F32), c, s, even)
        k = _rot(zk[...].astype(F32), c, s, even) * KSCALE
        qb, kb, vb = q.astype(BF16), k.astype(BF16), zv[...].astype(BF16)
        pb = (_dot_nt(qb, kb) * dm).astype(BF16)
        g = zg[...].astype(F32)
        sig = jax.nn.sigmoid(g)
        o = opre_r[...]
        mu = jnp.mean(o, axis=-1, keepdims=True)
        d = o - mu
        rstd = lax.rsqrt(jnp.mean(d * d, axis=-1, keepdims=True) + GN_EPS)
        y = d * rstd
        gain_v = gain_r[...]
        dret = dret_r[...].astype(F32)
        dyg = dret * (g * sig)
        dg_o[...] = (dret * (y * gain_v) * (sig * (1.0 + g * (1.0 - sig)))).astype(dg_o.dtype)
        dgain_o[...] += jnp.sum(dyg * y, axis=0, keepdims=True)
        dy = dyg * gain_v
        do = rstd * (dy - jnp.mean(dy, axis=-1, keepdims=True) - y * jnp.mean(dy * y, axis=-1, keepdims=True))
        dob = do.astype(BF16)
        stb = st_r[...].astype(BF16)
        dsn = dstate[...]
        dsnb = dsn.astype(BF16)
        dpb = (_dot_nt(dob, vb) * dm).astype(BF16)
        dq = _dot(dpb, kb) + _dot_nt(dob, stb) * aq
        dk = _dot_tn(qb, dpb).T + _dot_nt(vb, dsnb) * ak
        dv = _dot_tn(dob, pb).T + _dot((k * ak).astype(BF16), dsnb)
        dstate[...] = dsn * lam_r[...] + _dot_tn((q * aq).astype(BF16), dob)
        dq_o[...] = _rot_t(dq, c, s, even).astype(dq_o.dtype)
        dk_o[...] = _rot_t(dk * KSCALE, c, s, even).astype(dk_o.dtype)
        dv_o[...] = dv.astype(dv_o.dtype)

    return pl.pallas_call(
        body,
        name=name,
        grid=(nb,),
        in_specs=specs + [tok, pl.BlockSpec((RET_HEADS, None, LANE, LANE), lambda b: (0, blk(b), 0, 0)), tok],
        out_specs=[pl.BlockSpec((T, 4 * RET_WIDTH), lambda b: (blk(b), 0)), pl.BlockSpec((1, RET_WIDTH), lambda b: (0, 0))],
        out_shape=[jax.ShapeDtypeStruct((S, AB_IN_WIDTH), BF16), jax.ShapeDtypeStruct((1, RET_WIDTH), F32)],
        scratch_shapes=[pltpu.VMEM((RET_HEADS, LANE, LANE), F32)],
        compiler_params=_cp(("arbitrary",)),
    )(z, tb["cos"], tb["sin"], tb["dmat"], tb["aq"], tb["ak"], tb["lam"], gain, opre, states, dcat)


def _pool_counts(t0, rows):
    t = t0 + lax.broadcasted_iota(jnp.int32, (rows, POOL_WIDTH), 0)
    grp = lax.broadcasted_iota(jnp.int32, (rows, POOL_WIDTH), 1) >> 7
    win = jnp.where(grp == 0, POOL_WINDOWS[0], jnp.where(grp == 1, POOL_WINDOWS[1],
                    jnp.where(grp == 2, POOL_WINDOWS[2], POOL_WINDOWS[3])))
    return jnp.maximum(jnp.minimum(t + 1, win), 1).astype(F32), grp


def _window_sums(ext, grp, sign):
    n = ext.shape[0]
    sh = lambda v, k: pltpu.roll(v, k % n if sign > 0 else (n - k) % n, 0)
    s2 = ext + sh(ext, 1)
    s4 = s2 + sh(s2, 2)
    s8 = s4 + sh(s4, 4)
    s16 = s8 + sh(s8, 8)
    return jnp.where(grp == 0, s2, jnp.where(grp == 1, s4, jnp.where(grp == 2, s8, s16)))


def _pool_fwd(name, z, w_pool, scale, cat):
    S = z.shape[0]
    T = min(512, S)
    nb = S // T
    pcol = AB_IN_WIDTH // POOL_WIDTH - 1
    hb = T // POOL_HALO

    def body(p_ref, halo_ref, w_ref, sc_ref, cat_in, out_ref, pooled_ref):
        b = pl.program_id(0)
        cur = p_ref[...].astype(F32)
        halo = jnp.where(b > 0, halo_ref[...].astype(F32), 0.0)
        ext = jnp.concatenate([halo, cur], axis=0)
        cnt, grp = _pool_counts(b * T - POOL_HALO, T + POOL_HALO)
        sums = _window_sums(ext, grp, +1)
        pooled = (sums / cnt)[POOL_HALO:] - cur
        pb = pooled.astype(BF16)
        pooled_ref[...] = pb
        for gi in range(len(POOL_WINDOWS)):
            cs = slice(gi * POOL_GROUP_DIM, (gi + 1) * POOL_GROUP_DIM)
            mixed = _dot(pb[:, cs], w_ref[gi].astype(BF16))
            out_ref[:, cs] = (mixed * sc_ref[:, cs]).astype(out_ref.dtype)

    return pl.pallas_call(
        body,
        name=name,
        grid=(nb,),
        in_specs=[pl.BlockSpec((T, POOL_WIDTH), lambda b: (b, pcol)),
                  pl.BlockSpec((POOL_HALO, POOL_WIDTH), lambda b: (jnp.maximum(b * hb - 1, 0), pcol)),
                  pl.BlockSpec((4, POOL_GROUP_DIM, POOL_GROUP_DIM), lambda b: (0, 0, 0)),
                  pl.BlockSpec((1, POOL_WIDTH), lambda b: (0, 0)), _ANY],
        out_specs=[pl.BlockSpec((T, POOL_WIDTH), lambda b: (b, 1)), pl.BlockSpec((T, POOL_WIDTH), lambda b: (b, 0))],
        out_shape=[jax.ShapeDtypeStruct(cat.shape, cat.dtype), jax.ShapeDtypeStruct((S, POOL_WIDTH), BF16)],
        input_output_aliases={4: 0},
        compiler_params=_cp(("parallel",)),
    )(z, z, w_pool, scale, cat)


def _pool_bwd(name, pooled, w_pool, scale, dcat, dz):
    S = pooled.shape[0]
    T = min(512, S)
    nb = S // T
    hb = T // POOL_HALO
    last_h = S // POOL_HALO - 1
    pcol = AB_IN_WIDTH // POOL_WIDTH - 1

    def body(d_ref, dn_ref, pooled_ref, w_ref, sc_ref, dz_in, dp_ref, dw_ref, dsc_ref):
        b = pl.program_id(0)

        @pl.when(b == 0)
        def _():
            dw_ref[...] = jnp.zeros_like(dw_ref)
            dsc_ref[...] = jnp.zeros_like(dsc_ref)

        sc = sc_ref[...]
        dout = d_ref[...].astype(F32)
        dnext = jnp.where(b < nb - 1, dn_ref[...].astype(F32), 0.0)
        dmix = jnp.concatenate([dout, dnext], axis=0) * sc
        dmb = dmix.astype(BF16)
        pb = pooled_ref[...]
        dpooled = []
        for gi in range(len(POOL_WINDOWS)):
            cs = slice(gi * POOL_GROUP_DIM, (gi + 1) * POOL_GROUP_DIM)
            wb = w_ref[gi].astype(BF16)
            dpooled.append(_dot_nt(dmb[:, cs], wb))
            dw_ref[gi] += _dot_tn(pb[:, cs], dmb[:T, cs])
            mixed = _dot(pb[:, cs], wb)
            dsc_ref[:, cs] += jnp.sum(dout[:, cs] * mixed, axis=0, keepdims=True)
        dpl = jnp.concatenate(dpooled, axis=1)
        cnt, grp = _pool_counts(b * T, T + POOL_HALO)
        sums = _window_sums(dpl / cnt, grp, -1)
        dp_ref[...] = (sums[:T] - dpl[:T]).astype(dp_ref.dtype)

    return pl.pallas_call(
        body,
        name=name,
        grid=(nb,),
        in_specs=[pl.BlockSpec((T, POOL_WIDTH), lambda b: (b, 1)),
                  pl.BlockSpec((POOL_HALO, POOL_WIDTH), lambda b: (jnp.minimum((b + 1) * hb, last_h), 1)),
                  pl.BlockSpec((T, POOL_WIDTH), lambda b: (b, 0)),
                  pl.BlockSpec((4, POOL_GROUP_DIM, POOL_GROUP_DIM), lambda b: (0, 0, 0)),
                  pl.BlockSpec((1, POOL_WIDTH), lambda b: (0, 0)), _ANY],
        out_specs=[pl.BlockSpec((T, POOL_WIDTH), lambda b: (b, pcol)),
                   pl.BlockSpec((4, POOL_GROUP_DIM, POOL_GROUP_DIM), lambda b: (0, 0, 0)),
                   pl.BlockSpec((1, POOL_WIDTH), lambda b: (0, 0))],
        out_shape=[jax.ShapeDtypeStruct(dz.shape, dz.dtype),
                   jax.ShapeDtypeStruct((4, POOL_GROUP_DIM, POOL_GROUP_DIM), F32),
                   jax.ShapeDtypeStruct((1, POOL_WIDTH), F32)],
        input_output_aliases={5: 0},
        compiler_params=_cp(("arbitrary",)),
    )(dcat, dcat, pooled, w_pool, scale, dz)


ATT_STRIP = 32
ATT_Q = 256
ATT_W = ATT_Q + LEFT_CHUNKS * CHUNK


def _rel_index():
    j = np.arange(ATT_W)
    rel = np.clip(LEFT_CHUNKS * CHUNK - j, -REL_CLIP, REL_CLIP) + REL_CLIP
    fwd = np.where(j < BAND, rel, N_REL)
    bwd = np.where(j <= ATT_W - CHUNK, fwd, 2 * REL_CLIP)
    return tuple(jnp.asarray(v.reshape(1, ATT_W), jnp.int32) for v in (fwd, bwd))


def _bias_table(name, rel_bias, rel_idx):
    rb = jnp.concatenate([rel_bias, jnp.full((ATT_HEADS, 1), NEG_INF, F32),
                          jnp.zeros((ATT_HEADS, N_REL_PAD - N_REL - 1), F32)], axis=1)

    def body(rb_ref, idx_ref, o_ref, row0_ref):
        r = lax.broadcasted_iota(jnp.int32, (N_REL_PAD, ATT_W), 0)
        onehot = (r == idx_ref[...]).astype(F32)
        row0_ref[...] = jnp.dot(rb_ref[...], onehot, precision=lax.Precision.HIGHEST, preferred_element_type=F32)
        col = lax.broadcasted_iota(jnp.int32, (CHUNK, ATT_W), 1)
        row = lax.broadcasted_iota(jnp.int32, (CHUNK, ATT_W), 0)
        for h in range(ATT_HEADS):
            same = jnp.broadcast_to(row0_ref[pl.ds(h, 1), :], (CHUNK, ATT_W))
            turned = pltpu.roll(same, 0, 1, stride=1, stride_axis=0)
            o_ref[h] = jnp.where(col >= BAND, NEG_INF, jnp.where(col < row, same, turned))

    return pl.pallas_call(
        body,
        name=name,
        out_shape=jax.ShapeDtypeStruct((ATT_HEADS, CHUNK, ATT_W), F32),
        scratch_shapes=[pltpu.VMEM((ATT_HEADS, ATT_W), F32)],
        compiler_params=pltpu.CompilerParams(vmem_limit_bytes=VMEM_LIMIT),
    )(rb, rel_idx)


def _bias_grad(name, dband, rel_idx):
    def body(d_ref, idx_ref, o_ref, sums_ref):
        row = lax.broadcasted_iota(jnp.int32, (CHUNK, ATT_W), 0)
        for h in range(ATT_HEADS):
            back = d_ref[h]
            for bit in range(CHUNK.bit_length() - 1):
                back = jnp.where(((row >> bit) & 1) == 1, pltpu.roll(back, ATT_W - (1 << bit), 1), back)
            sums_ref[pl.ds(h, 1), :] = jnp.sum(back, axis=0, keepdims=True)
        r = lax.broadcasted_iota(jnp.int32, (N_REL_PAD, ATT_W), 0)
        onehot = (r == idx_ref[...]).astype(F32)
        o_ref[...] = lax.dot_general(sums_ref[...], onehot, (((1,), (1,)), ((), ())),
                                     precision=lax.Precision.HIGHEST, preferred_element_type=F32)

    out = pl.pallas_call(
        body,
        name=name,
        out_shape=jax.ShapeDtypeStruct((ATT_HEADS, N_REL_PAD), F32),
        scratch_shapes=[pltpu.VMEM((ATT_HEADS, ATT_W), F32)],
        compiler_params=pltpu.CompilerParams(vmem_limit_bytes=VMEM_LIMIT),
    )(dband, rel_idx)
    return out[:, :N_REL]


def _attn_unit(q_ref, kw_ref, bias_ref, e, u, lane):
    mine = (lane < ATT_HEAD_DIM) if e == 0 else (lane >= ATT_HEAD_DIM)
    qm = jnp.where(mine, q_ref[u * ATT_Q:(u + 1) * ATT_Q, :] * QSCALE, 0)
    kw = kw_ref[u * ATT_Q:u * ATT_Q + ATT_W, :]
    s = _dot_nt(qm, kw) + bias_ref[u, e]
    p = jnp.exp(s - jnp.max(s, axis=-1, keepdims=True))
    return p, 1.0 / jnp.sum(p, axis=-1, keepdims=True), qm, kw, mine


def _attn_in_specs(nb):
    T = ATT_BLOCK
    hp = ATT_HEADS // 2
    cur = lambda off: pl.BlockSpec((T, LANE), lambda h, b: (jnp.minimum(b, nb - 1), off + h))
    prev = lambda off: pl.BlockSpec((T, LANE), lambda h, b: (jnp.clip(b - 1, 0, nb - 1), off + h))
    return [cur(0), prev(hp), cur(hp), prev(2 * hp), cur(2 * hp),
            pl.BlockSpec((None, 2, CHUNK, ATT_W), lambda h, b: (h, 0, 0, 0))]


def _spread_bias(bias_ref, bm_ref, block):
    col = lax.broadcasted_iota(jnp.int32, (CHUNK, ATT_W), 1)
    for first in (True, False):
        @pl.when(block == (0 if first else 1))
        def _(first=first):
            for u in range(ATT_BLOCK // ATT_Q):
                for e in range(2):
                    for j in range(ATT_Q // CHUNK):
                        rows = pltpu.roll(bias_ref[e], j * CHUNK, 1)
                        if first:
                            rows = jnp.where(col >= ATT_BLOCK - u * ATT_Q, rows, NEG_INF)
                        bm_ref[u, e, j * CHUNK:(j + 1) * CHUNK, :] = rows


def _attn_fwd(name, qkv, bias):
    S = qkv.shape[0]
    T = ATT_BLOCK
    nb = S // T

    def body(q_ref, kp_ref, kc_ref, vp_ref, vc_ref, band_ref, o_ref, kw_ref, vw_ref, bias_ref, s_ref, p_ref, inv_ref):
        _spread_bias(band_ref, bias_ref, pl.program_id(1))
        kw_ref[0:T] = kp_ref[...]
        kw_ref[T:2 * T] = kc_ref[...]
        vw_ref[0:T] = vp_ref[...]
        vw_ref[T:2 * T] = vc_ref[...]
        lane = lax.broadcasted_iota(jnp.int32, (ATT_Q, LANE), 1)
        for u in range(T // ATT_Q):
            vw = vw_ref[u * ATT_Q:u * ATT_Q + ATT_W, :]
            kw = kw_ref[u * ATT_Q:u * ATT_Q + ATT_W, :]
            outs = []
            for e in range(2):
                mine = (lane < ATT_HEAD_DIM) if e == 0 else (lane >= ATT_HEAD_DIM)
                qm = jnp.where(mine, q_ref[u * ATT_Q:(u + 1) * ATT_Q, :] * QSCALE, 0)
                s_ref[e] = _dot_nt(qm, kw)
                for r in range(ATT_Q // ATT_STRIP):
                    rows = slice(r * ATT_STRIP, (r + 1) * ATT_STRIP)
                    s = s_ref[e, rows, :] + bias_ref[u, e, rows, :]
                    p = jnp.exp(s - jnp.max(s, axis=-1, keepdims=True))
                    inv_ref[e, rows, :] = jnp.broadcast_to(1.0 / jnp.sum(p, axis=-1, keepdims=True), (ATT_STRIP, LANE))
                    p_ref[e, rows, :] = p.astype(BF16)
                outs.append(_dot(p_ref[e], vw) * inv_ref[e])
            o_ref[u * ATT_Q:(u + 1) * ATT_Q, :] = jnp.where(lane < ATT_HEAD_DIM, outs[0], outs[1]).astype(o_ref.dtype)

    return pl.pallas_call(
        body,
        name=name,
        grid=(ATT_HEADS // 2, nb),
        in_specs=_attn_in_specs(nb),
        out_specs=pl.BlockSpec((T, LANE), lambda h, b: (b, h)),
        out_shape=jax.ShapeDtypeStruct((S, D_MODEL), BF16),
        scratch_shapes=[pltpu.VMEM((2 * T, LANE), BF16), pltpu.VMEM((2 * T, LANE), BF16),
                        pltpu.VMEM((T // ATT_Q, 2, ATT_Q, ATT_W), F32), pltpu.VMEM((2, ATT_Q, ATT_W), F32),
                        pltpu.VMEM((2, ATT_Q, ATT_W), BF16), pltpu.VMEM((2, ATT_Q, LANE), F32)],
        compiler_params=_cp(("parallel", "arbitrary")),
    )(qkv, qkv, qkv, qkv, qkv, bias)


def _attn_bwd(name, qkv, bias, do):
    S = qkv.shape[0]
    T = ATT_BLOCK
    nb = S // T

    def body(q_ref, kp_ref, kc_ref, vp_ref, vc_ref, band_ref, do_ref,
             dq_ref, dk_ref, dv_ref, dband_ref, kw_ref, vw_ref, dkw_ref, dvw_ref, bias_ref, dbias_ref):
        b = pl.program_id(1)

        _spread_bias(band_ref, bias_ref, b)

        @pl.when(b == 0)
        def _():
            dbias_ref[...] = jnp.zeros_like(dbias_ref)
            dkw_ref[:, T:2 * T] = jnp.zeros((LANE, T), F32)
            dvw_ref[:, T:2 * T] = jnp.zeros((LANE, T), F32)

        dkw_ref[:, 0:T] = dkw_ref[:, T:2 * T]
        dvw_ref[:, 0:T] = dvw_ref[:, T:2 * T]
        dkw_ref[:, T:2 * T] = jnp.zeros((LANE, T), F32)
        dvw_ref[:, T:2 * T] = jnp.zeros((LANE, T), F32)

        @pl.when(b < nb)
        def _():
            kw_ref[0:T] = kp_ref[...]
            kw_ref[T:2 * T] = kc_ref[...]
            vw_ref[0:T] = vp_ref[...]
            vw_ref[T:2 * T] = vc_ref[...]
            lane = lax.broadcasted_iota(jnp.int32, (ATT_Q, LANE), 1)
            for u in range(T // ATT_Q):
                rows = slice(u * ATT_Q, (u + 1) * ATT_Q)
                win = slice(u * ATT_Q, u * ATT_Q + ATT_W)
                vw = vw_ref[win, :]
                do2 = do_ref[rows, :]
                dqs, dk, dv = [], None, None
                for e in range(2):
                    p, inv, qm, kw, mine = _attn_unit(q_ref, kw_ref, bias_ref, e, u, lane)
                    dom = jnp.where(mine, do2, 0)
                    dp = _dot_nt(dom, vw)
                    delta = jnp.sum(p * dp, axis=-1, keepdims=True) * inv
                    ds = p * ((dp - delta) * inv)
                    dbias_ref[e] += ds
                    dsb = ds.astype(BF16)
                    dqs.append(_dot(dsb, kw))
                    dk_e = _dot_tn(qm, dsb)
                    dv_e = _dot_tn(dom, (p * inv).astype(BF16))
                    dk = dk_e if dk is None else dk + dk_e
                    dv = dv_e if dv is None else dv + dv_e
                dq_ref[rows, :] = (jnp.where(lane < ATT_HEAD_DIM, dqs[0], dqs[1]) * QSCALE).astype(dq_ref.dtype)
                dkw_ref[:, win] += dk
                dvw_ref[:, win] += dv

        @pl.when(b > 0)
        def _():
            dk_ref[...] = dkw_ref[:, 0:T].T.astype(dk_ref.dtype)
            dv_ref[...] = dvw_ref[:, 0:T].T.astype(dv_ref.dtype)

        @pl.when(b == nb)
        def _():
            for e in range(2):
                acc = dbias_ref[e, 0:CHUNK, :]
                for j in range(1, ATT_Q // CHUNK):
                    acc = acc + pltpu.roll(dbias_ref[e, j * CHUNK:(j + 1) * CHUNK, :], ATT_W - j * CHUNK, 1)
                dband_ref[e] = acc

    tok = jax.ShapeDtypeStruct((S, D_MODEL), BF16)
    prev_out = pl.BlockSpec((T, LANE), lambda h, b: (jnp.maximum(b - 1, 0), h))
    return pl.pallas_call(
        body,
        name=name,
        grid=(ATT_HEADS // 2, nb + 1),
        in_specs=_attn_in_specs(nb) + [pl.BlockSpec((T, LANE), lambda h, b: (jnp.minimum(b, nb - 1), h))],
        out_specs=[pl.BlockSpec((T, LANE), lambda h, b: (jnp.minimum(b, nb - 1), h)), prev_out, prev_out,
                   pl.BlockSpec((None, 2, CHUNK, ATT_W), lambda h, b: (h, 0, 0, 0))],
        out_shape=[tok, tok, tok, jax.ShapeDtypeStruct((ATT_HEADS // 2, 2, CHUNK, ATT_W), F32)],
        scratch_shapes=[pltpu.VMEM((2 * T, LANE), BF16), pltpu.VMEM((2 * T, LANE), BF16),
                        pltpu.VMEM((LANE, 2 * T), F32), pltpu.VMEM((LANE, 2 * T), F32),
                        pltpu.VMEM((T // ATT_Q, 2, ATT_Q, ATT_W), F32), pltpu.VMEM((2, ATT_Q, ATT_W), F32)],
        compiler_params=_cp(("parallel", "arbitrary")),
    )(qkv, qkv, qkv, qkv, qkv, bias, do)


def _local_step(x, target, small, W):
    S = x.shape[0]
    tb = _ret_tables(S)
    rel_fwd, rel_bwd = _rel_index()
    saved = []
    normed = (("tile", BF16),)
    deep = dict(tm=512, tk=D_FF)
    h = _rms_fwd("mix_norm_fwd0", x, small["mix_norm"][0:1])
    for layer in range(DEPTH):
        i = layer // 2
        st = {"x_in": x, "h": h}
        g_ffn = small["ffn_norm"][layer:layer + 1]
        if layer % 2 == 0:
            z = W.mm(f"ab_in_fwd{layer}", "nn", h, W.get("ab_w_in", i), tm=2048, tn=640, out_dtype=BF16)
            gain = small["ab_gn_gain"][i:i + 1]
            cat, opre, states = _ret_fwd(f"ret_fwd{layer}", z, tb, gain)
            cat, pooled = _pool_fwd(f"pool_fwd{layer}", z, small["ab_w_pool"][i], small["ab_pool_scale"][i:i + 1], cat)
            st.update(z=z, opre=opre, states=states, pooled=pooled, cat=cat)
            x, hn = W.mm(f"ab_out_fwd{layer}", "nn", cat, W.get("ab_w_out", i), extras=(x,), aux=(g_ffn,), sides=normed,
                         epi=_epi_residual_norm)
        else:
            qkv = W.mm(f"qkv_fwd{layer}", "nn", h, W.get("c_w_qkv", i), tm=2048, out_dtype=BF16)
            bias = _bias_table(f"bias_table{layer}", small["c_rel_bias"][i], rel_fwd)
            bias = bias.reshape(ATT_HEADS // 2, 2, CHUNK, ATT_W)
            att = _attn_fwd(f"attn_fwd{layer}", qkv, bias)
            st.update(qkv=qkv, bias=bias, att=att)
            x, hn = W.mm(f"c_out_fwd{layer}", "nn", att, W.get("c_w_out", i), extras=(x,), aux=(g_ffn,), sides=normed,
                         epi=_epi_residual_norm)
        st["x_mid"] = x
        u = W.mm(f"ffn_in_fwd{layer}", "nn", hn, W.get("w_ffn_in", layer), out_dtype=BF16, tm=2048)
        if layer + 1 < DEPTH:
            x, h = W.mm(f"ffn_out_fwd{layer}", "nn", u, W.get("w_ffn_out", layer), a_fn=_relu2, extras=(x,),
                        aux=(small["mix_norm"][layer + 1:layer + 2],), sides=normed, epi=_epi_residual_norm, **deep)
        else:
            x = W.mm(f"ffn_out_fwd{layer}", "nn", u, W.get("w_ffn_out", layer), a_fn=_relu2, extras=(x,),
                     epi=_epi_residual, **deep)
        st.update(hn=hn, u=u)
        saved.append(st)

    loss, dx, dxb, d_final = _loss_head(x, small["final_norm"].reshape(1, D_MODEL), target)

    gs = {k: [None] * v.shape[0] for k, v in small.items() if k != "final_norm"}
    gb = {k: None for k in W.n_layers}
    landed = {k: None for k in W.n_layers}
    pending = []

    def host(name, *args, take=1, **kw):
        items = [pending.pop(0) for _ in range(min(take, len(pending)))]
        if not items:
            return _mm(name, *args, **kw)
        riders = [_grad_rider(key, idx, gb[key], landed[key]) for key, idx in items]
        res, outs = _mm(name, *args, rider=_join_riders(riders), **kw)
        for (key, _), out in zip(items, outs):
            landed[key] = out
        return res

    def dw(name, key, idx, a, b, call=_mm, **kw):
        gb[key] = call(name, "tn", a, b, stack=(W.n_layers[key], idx, gb[key]), out_dtype=BF16, **kw)
        pending.append((key, idx))

    gain_sums = (("tile", BF16), ("colsum", F32))
    for layer in reversed(range(DEPTH)):
        i = layer // 2
        st = saved[layer]
        du = host(f"ffn_out_bwd{layer}", "nt", dxb, W.get("w_ffn_out", layer), extras=(st["u"],),
                  epi=lambda acc, u: acc * (2.0 * jnp.maximum(u, 0).astype(F32)), out_dtype=BF16, tm=2048)
        dw(f"ffn_out_dw{layer}", "w_ffn_out", layer, st["u"], dxb, a_fn=_relu2, tk=2048)
        dx, dxb, dgain = host(f"ffn_in_bwd{layer}", "nt", du, W.get("w_ffn_in", layer), extras=(st["x_mid"], dx),
                         aux=(small["ffn_norm"][layer:layer + 1],), sides=gain_sums, epi=_epi_rms_bwd, **deep)
        gs["ffn_norm"][layer] = dgain[0:1]
        dw(f"ffn_in_dw{layer}", "w_ffn_in", layer, st["hn"], du, tk=2048)
        norm_bwd = dict(extras=(st["x_in"], dx), aux=(small["mix_norm"][layer:layer + 1],), sides=gain_sums,
                        epi=_epi_rms_bwd)
        if layer % 2 == 0:
            dcat = _mm(f"ab_out_bwd{layer}", "nt", dxb, W.get("ab_w_out", i), out_dtype=BF16)
            dw(f"ab_out_dw{layer}", "ab_w_out", i, st["cat"], dxb, tk=2048)
            gain = small["ab_gn_gain"][i:i + 1]
            dz, gs["ab_gn_gain"][i] = _ret_bwd(f"ret_bwd{layer}", st["z"], tb, gain, st["opre"], st["states"], dcat)
            dz, gs["ab_w_pool"][i], gs["ab_pool_scale"][i] = _pool_bwd(
                f"pool_bwd{layer}", st["pooled"], small["ab_w_pool"][i], small["ab_pool_scale"][i:i + 1], dcat, dz)
            if layer == 0:
                dw(f"ab_in_dw{layer}", "ab_w_in", i, st["h"], dz, call=host, tn=640, tk=2048)
            dx, dxb, dgain = host(f"ab_in_bwd{layer}", "nt", dz, W.get("ab_w_in", i), tm=512, tk=AB_IN_WIDTH,
                             take=len(pending) if layer == 0 else 1, **norm_bwd)
            if layer > 0:
                dw(f"ab_in_dw{layer}", "ab_w_in", i, st["h"], dz, call=host, tn=640, tk=2048)
        else:
            datt = _mm(f"c_out_bwd{layer}", "nt", dxb, W.get("c_w_out", i), out_dtype=BF16)
            dw(f"c_out_dw{layer}", "c_w_out", i, st["att"], dxb, tk=2048)
            dq, dk, dv, dbias = _attn_bwd(f"attn_bwd{layer}", st["qkv"], st["bias"], datt)
            gs["c_rel_bias"][i] = _bias_grad(f"bias_grad{layer}", dbias.reshape(ATT_HEADS, CHUNK, ATT_W), rel_bwd)
            dqkv = [dq, dk, dv]
            dx, dxb, dgain = host(f"qkv_bwd{layer}", "nt", dqkv, W.get("c_w_qkv", i), tm=512, tk=3 * D_MODEL, **norm_bwd)
            dw(f"qkv_dw{layer}", "c_w_qkv", i, st["h"], dqkv, call=host, tk=2048)
        gs["mix_norm"][layer] = dgain[0:1]
    for key, idx in pending:
        landed[key], = _run_rider(f"grad_exchange_{key}{idx}", _grad_rider(key, idx, gb[key], landed[key]))

    g_small = {
        "mix_norm": jnp.concatenate(gs["mix_norm"], axis=0),
        "ffn_norm": jnp.concatenate(gs["ffn_norm"], axis=0),
        "ab_gn_gain": jnp.concatenate(gs["ab_gn_gain"], axis=0),
        "ab_w_pool": jnp.stack(gs["ab_w_pool"], axis=0),
        "ab_pool_scale": jnp.concatenate(gs["ab_pool_scale"], axis=0),
        "c_rel_bias": jnp.stack(gs["c_rel_bias"], axis=0),
        "final_norm": d_final.reshape(D_MODEL),
    }
    return loss, dx, g_small, gb, landed


_BIG = ("w_ffn_in", "w_ffn_out", "ab_w_in", "ab_w_out", "c_w_qkv", "c_w_out")
_SHARD_AXIS = {"w_ffn_in": 2, "w_ffn_out": 1, "ab_w_in": 2, "ab_w_out": 1, "c_w_qkv": 2, "c_w_out": 1}
_SMALL = ("mix_norm", "ffn_norm", "ab_gn_gain", "ab_w_pool", "ab_pool_scale", "c_rel_bias", "final_norm")


def _place():
    x, y, c = lax.axis_index("x"), lax.axis_index("y"), lax.axis_index("c")
    chips = [(1 - x, y), (x, 1 - y), (1 - x, 1 - y)]
    return x, y, c, chips


def _sub(ref, axis, start, size):
    idx = [slice(None)] * len(ref.shape)
    idx[axis] = pl.ds(pl.multiple_of(start, LANE), size)
    return ref.at[tuple(idx)]


def _gather_rider(items, shards):
    keys = sorted({k for k, _ in items})
    n = len(items)
    axes = [_SHARD_AXIS[k] - 1 for k, _ in items]
    sizes = [shards[k].shape[a + 1] for (k, _), a in zip(items, axes)]
    hsizes = [shards[k].shape[2 - a] // 2 for (k, _), a in zip(items, axes)]

    def views(ins, outs, send_sems, recv_sems):
        x, y, c, chips = _place()
        srcs = [ins[keys.index(k)].at[l] for k, l in items]

        def remote(src, dst, s, to):
            return pltpu.make_async_remote_copy(src_ref=src, dst_ref=dst, send_sem=send_sems.at[s],
                                                recv_sem=recv_sems.at[s], device_id=to, device_id_type=MESH)

        def half(w, chip, core):
            return _sub(_sub(outs[w], axes[w], chip * sizes[w], sizes[w]), 1 - axes[w], core * hsizes[w], hsizes[w])

        me = 2 * x + y
        local = [pltpu.make_async_copy(srcs[w], _sub(outs[w], axes[w], me * sizes[w], sizes[w]), send_sems.at[6 * n + w])
                 for w in range(n)]
        first = [remote(_sub(srcs[w], 1 - axes[w], c * hsizes[w], hsizes[w]), half(w, me, c), w * 6 + k, (px, py, c))
                 for w in range(n) for k, (px, py) in enumerate(chips)]
        return x, y, c, chips, remote, half, local, first

    def start(ins, outs, send_sems, recv_sems):
        *_, local, first = views(ins, outs, send_sems, recv_sems)
        for cp in local + first:
            cp.start()

    def passes(x, y, c, chips, remote, half):
        return [remote(half(w, 2 * px + py, c), half(w, 2 * px + py, c), w * 6 + 3 + k, (x, y, 1 - c))
                for w in range(n) for k, (px, py) in enumerate(chips)]

    def relay(ins, outs, send_sems, recv_sems):
        x, y, c, chips, remote, half, _, _ = views(ins, outs, send_sems, recv_sems)
        for w in range(n):
            for k, (px, py) in enumerate(chips):
                landed = half(w, 2 * px + py, c)
                remote(landed, landed, w * 6 + k, (px, py, c)).wait_recv()
        for cp in passes(x, y, c, chips, remote, half):
            cp.start()

    def finish(ins, outs, send_sems, recv_sems):
        x, y, c, chips, remote, half, local, first = views(ins, outs, send_sems, recv_sems)
        for w in range(n):
            for k, (px, py) in enumerate(chips):
                theirs = half(w, 2 * px + py, 1 - c)
                remote(theirs, theirs, w * 6 + 3 + k, (x, y, 1 - c)).wait_recv()
        for cp in first + passes(x, y, c, chips, remote, half):
            cp.wait_send()
        for cp in local:
            cp.wait()

    def full(k, a):
        shape = list(shards[k].shape[1:])
        shape[a] *= N_CHIPS
        return jax.ShapeDtypeStruct(tuple(shape), shards[k].dtype)

    return _Rider(tuple(shards[k] for k in keys), tuple(full(k, a) for (k, _), a in zip(items, axes)), 7 * n, start, finish,
                  relay=relay)


def _mixer_items(layer):
    names = ("ab_w_in", "ab_w_out") if layer % 2 == 0 else ("c_w_qkv", "c_w_out")
    return [(k, layer // 2) for k in names]


class _Weights:
    def __init__(self, shards):
        self.shards = shards
        self.n_layers = {k: shards[k].shape[0] for k in _BIG}
        self.full = {}
        first, second = _mixer_items(0)
        self._take([first], _run_rider("gather_first", _gather_rider([first], shards)))
        self.plan = {"ab_in_fwd0": [second, ("w_ffn_in", 0)], "ab_out_fwd0": [("w_ffn_out", 0)]}
        for layer in range(1, DEPTH):
            proj = "ab_in" if layer % 2 == 0 else "qkv"
            self.plan[f"ffn_in_fwd{layer - 1}"] = _mixer_items(layer)
            self.plan[f"ffn_out_fwd{layer - 1}"] = [("w_ffn_in", layer)]
            self.plan[f"{proj}_fwd{layer}"] = [("w_ffn_out", layer)]

    def _take(self, items, outs):
        self.full.update(zip(items, outs))

    def get(self, name, layer):
        return self.full[(name, layer)]

    def mm(self, name, *args, **kw):
        items = self.plan.get(name)
        if items is None:
            return _mm(name, *args, **kw)
        res, outs = _mm(name, *args, rider=_gather_rider(items, self.shards), **kw)
        self._take(items, outs)
        return res


def _run_rider(name, rider):
    n_in, n_out = len(rider.operands), len(rider.out_shapes)

    def body(*refs):
        ins, outs, sems = refs[:n_in], refs[n_in:n_in + n_out], refs[n_in + n_out:]
        rider.start(ins, outs, *sems)
        if rider.relay is not None:
            rider.relay(ins, outs, *sems)
        rider.finish(ins, outs, *sems)

    return pl.pallas_call(
        body,
        name=name,
        in_specs=[_ANY] * n_in,
        out_specs=[_ANY] * n_out,
        out_shape=list(rider.out_shapes),
        input_output_aliases=dict(rider.aliases),
        scratch_shapes=[pltpu.SemaphoreType.DMA((rider.n_sems,)), pltpu.SemaphoreType.DMA((rider.n_sems,))],
        compiler_params=pltpu.CompilerParams(has_side_effects=True),
    )(*rider.operands)


def _grad_rider(name, layer, grad, landing):
    axis = _SHARD_AXIS[name] - 1
    L, R, C = grad.shape
    shard = (R // N_CHIPS, C) if axis == 0 else (R, C // N_CHIPS)
    size = shard[axis]

    def copies(ins, outs, send_sems, recv_sems):
        x, y, c, chips = _place()
        return [pltpu.make_async_remote_copy(
            src_ref=_sub(ins[0].at[layer], axis, (2 * px + py) * size, size), dst_ref=outs[0].at[layer, k],
            send_sem=send_sems.at[k], recv_sem=recv_sems.at[k], device_id=(px, py, c), device_id_type=MESH)
            for k, (px, py) in enumerate(chips)]

    def start(ins, outs, send_sems, recv_sems):
        for cp in copies(ins, outs, send_sems, recv_sems):
            cp.start()

    def finish(ins, outs, send_sems, recv_sems):
        cps = copies(ins, outs, send_sems, recv_sems)
        for cp in cps:
            cp.wait_recv()
        for cp in cps:
            cp.wait_send()

    out = jax.ShapeDtypeStruct((L, 3) + shard, grad.dtype)
    if landing is None:
        return _Rider((grad,), (out,), 3, start, finish)
    return _Rider((grad, landing), (out,), 3, start, finish, aliases=((1, 0),))


def _join_riders(riders):
    if len(riders) == 1:
        return riders[0]

    def parts(ins, outs, send_sems, recv_sems):
        i0 = o0 = s0 = 0
        for r in riders:
            ni, no = len(r.operands), len(r.out_shapes)
            yield (r, ins[i0:i0 + ni], outs[o0:o0 + no], send_sems.at[pl.ds(s0, r.n_sems)],
                   recv_sems.at[pl.ds(s0, r.n_sems)])
            i0, o0, s0 = i0 + ni, o0 + no, s0 + r.n_sems

    def start(*refs):
        for r, *own in parts(*refs):
            r.start(*own)

    def finish(*refs):
        for r, *own in parts(*refs):
            r.finish(*own)

    aliases, i0, o0 = [], 0, 0
    for r in riders:
        aliases += [(i0 + src, o0 + dst) for src, dst in r.aliases]
        i0, o0 = i0 + len(r.operands), o0 + len(r.out_shapes)
    return _Rider(tuple(x for r in riders for x in r.operands), tuple(x for r in riders for x in r.out_shapes),
                  sum(r.n_sems for r in riders), start, finish, tuple(aliases))


def _pair_swap(sums):
    n = len(sums)

    def body(*refs):
        ins, outs = refs[:n], refs[n:2 * n]
        send_sems, recv_sems = refs[2 * n:]
        x, y, c, _ = _place()
        cps = [pltpu.make_async_remote_copy(src_ref=ins[w], dst_ref=outs[w], send_sem=send_sems.at[w],
                                            recv_sem=recv_sems.at[w], device_id=(x, y, 1 - c), device_id_type=MESH)
               for w in range(n)]
        for cp in cps:
            cp.start()
        for cp in cps:
            cp.wait_recv()
        for cp in cps:
            cp.wait_send()

    return pl.pallas_call(
        body,
        name="pair_swap",
        in_specs=[_ANY] * n,
        out_specs=[_ANY] * n,
        out_shape=[jax.ShapeDtypeStruct(s.shape, s.dtype) for s in sums],
        scratch_shapes=[pltpu.SemaphoreType.DMA((n,)), pltpu.SemaphoreType.DMA((n,))],
        compiler_params=pltpu.CompilerParams(has_side_effects=True),
    )(*sums)


def _rows_tile(rows, cols):
    tr = rows
    while tr * cols > (1 << 19) and tr % 16 == 0:
        tr //= 2
    return tr


def _chip_sum(name, grad, landed, chip, saxis):
    L = grad.shape[0]
    _, _, R, C = landed.shape
    tr = _rows_tile(R, C)
    nr = R // tr
    if saxis == 2:
        g_idx = lambda l, i, s: (l, i, s[0])
    else:
        g_idx = lambda l, i, s: (l, s[0] * nr + i, 0)

    def body(s_ref, g_ref, l_ref, o_ref):
        tot = ((g_ref[...].astype(F32) + l_ref[0].astype(F32)) + l_ref[1].astype(F32)) + l_ref[2].astype(F32)
        o_ref[...] = tot.astype(o_ref.dtype)

    return pl.pallas_call(
        body,
        name=name,
        grid_spec=pltpu.PrefetchScalarGridSpec(
            num_scalar_prefetch=1,
            grid=(L, nr),
            in_specs=[pl.BlockSpec((None, tr, C), g_idx), pl.BlockSpec((None, 3, tr, C), lambda l, i, s: (l, 0, i, 0))],
            out_specs=pl.BlockSpec((None, tr, C), lambda l, i, s: (l, i, 0)),
        ),
        out_shape=jax.ShapeDtypeStruct((L, R, C), BF16),
        compiler_params=_cp(("parallel", "parallel")),
    )(chip, grad, landed)


def _all_reduce_small(packed):
    R = packed.shape[0]

    def body(p_ref, o_ref, land_ref, send_sems, recv_sems):
        x, y, c, _ = _place()
        me = 4 * x + 2 * y + c
        sends, recvs = [], []
        for r in range(1, N_DEV):
            px, py, pc = x ^ (r >> 2), y ^ ((r >> 1) & 1), c ^ (r & 1)
            cp = pltpu.make_async_remote_copy(src_ref=p_ref, dst_ref=land_ref.at[me], send_sem=send_sems.at[r - 1],
                                              recv_sem=recv_sems.at[r - 1], device_id=(px, py, pc), device_id_type=MESH)
            cp.start()
            sends.append(cp)
            recvs.append(pltpu.make_async_remote_copy(src_ref=p_ref, dst_ref=land_ref.at[4 * px + 2 * py + pc],
                                                      send_sem=send_sems.at[r - 1], recv_sem=recv_sems.at[r - 1],
                                                      device_id=(px, py, pc), device_id_type=MESH))
        land_ref[me] = p_ref[...]
        for cp in recvs:
            cp.wait_recv()
        for cp in sends:
            cp.wait_send()
        acc = land_ref[0]
        for d in range(1, N_DEV):
            acc = acc + land_ref[d]
        o_ref[...] = acc

    vm = pl.BlockSpec(memory_space=pltpu.VMEM)
    return pl.pallas_call(
        body,
        name="all_reduce_small",
        in_specs=[vm],
        out_specs=vm,
        out_shape=jax.ShapeDtypeStruct((R, LANE), F32),
        scratch_shapes=[pltpu.VMEM((N_DEV, R, LANE), F32), pltpu.SemaphoreType.DMA((N_DEV - 1,)),
                        pltpu.SemaphoreType.DMA((N_DEV - 1,))],
        compiler_params=pltpu.CompilerParams(has_side_effects=True, vmem_limit_bytes=VMEM_LIMIT),
    )(packed)


def _adamw(name, w, m, v, grads):
    R, C = w.shape
    tr = _rows_tile(R, C)
    c1 = 1.0 - ADAM_B1 ** ADAM_STEP
    c2 = 1.0 - ADAM_B2 ** ADAM_STEP
    ng = len(grads)

    def body(*refs):
        w_ref, m_ref, v_ref = refs[:3]
        g_refs = refs[3:3 + ng]
        g_ref, d_ref, nm_ref, nv_ref = refs[3 + ng:]
        gv = g_refs[0][...].astype(F32)
        for r in g_refs[1:]:
            gv = gv + r[...].astype(F32)
        g_ref[...] = gv
        nm = ADAM_B1 * m_ref[...] + (1.0 - ADAM_B1) * gv
        nv = ADAM_B2 * v_ref[...] + (1.0 - ADAM_B2) * (gv * gv)
        nm_ref[...] = nm
        nv_ref[...] = nv
        d_ref[...] = -ADAM_LR * ((nm / c1) / (jnp.sqrt(nv / c2) + ADAM_EPS) + ADAM_WD * w_ref[...])

    blk = pl.BlockSpec((tr, C), lambda i: (i, 0))
    out = jax.ShapeDtypeStruct((R, C), F32)
    return pl.pallas_call(
        body,
        name=name,
        grid=(R // tr,),
        in_specs=[blk] * (3 + ng),
        out_specs=[blk] * 4,
        out_shape=[out] * 4,
        compiler_params=_cp(("parallel",)),
    )(w, m, v, *grads)


def _pack(parts):
    rows = []
    for p in parts:
        flat = p.reshape(-1).astype(F32)
        n = -(-flat.shape[0] // (8 * LANE)) * (8 * LANE)
        rows.append(jnp.pad(flat, (0, n - flat.shape[0])).reshape(n // LANE, LANE))
    return jnp.concatenate(rows, axis=0)


def _unpack(packed, like):
    out, r = [], 0
    for p in like:
        size = int(np.prod(p.shape))
        n = -(-size // (8 * LANE)) * 8
        out.append(packed[r:r + n].reshape(-1)[:size].reshape(p.shape))
        r += n
    return out


def kernel(x, mix_norm, ffn_norm, w_ffn_in, w_ffn_out, ab_w_in, ab_gn_gain, ab_w_pool, ab_pool_scale, ab_w_out, c_w_qkv, c_rel_bias, c_w_out, final_norm, loss_target, m_mix_norm, m_ffn_norm, m_w_ffn_in, m_w_ffn_out, m_ab_w_in, m_ab_gn_gain, m_ab_w_pool, m_ab_pool_scale, m_ab_w_out, m_c_w_qkv, m_c_rel_bias, m_c_w_out, m_final_norm, v_mix_norm, v_ffn_norm, v_w_ffn_in, v_w_ffn_out, v_ab_w_in, v_ab_gn_gain, v_ab_w_pool, v_ab_pool_scale, v_ab_w_out, v_c_w_qkv, v_c_rel_bias, v_c_w_out, v_final_norm):
    w = dict(mix_norm=mix_norm, ffn_norm=ffn_norm, w_ffn_in=w_ffn_in, w_ffn_out=w_ffn_out, ab_w_in=ab_w_in,
             ab_gn_gain=ab_gn_gain, ab_w_pool=ab_w_pool, ab_pool_scale=ab_pool_scale, ab_w_out=ab_w_out,
             c_w_qkv=c_w_qkv, c_rel_bias=c_rel_bias, c_w_out=c_w_out, final_norm=final_norm)
    m = dict(mix_norm=m_mix_norm, ffn_norm=m_ffn_norm, w_ffn_in=m_w_ffn_in, w_ffn_out=m_w_ffn_out, ab_w_in=m_ab_w_in,
             ab_gn_gain=m_ab_gn_gain, ab_w_pool=m_ab_w_pool, ab_pool_scale=m_ab_pool_scale, ab_w_out=m_ab_w_out,
             c_w_qkv=m_c_w_qkv, c_rel_bias=m_c_rel_bias, c_w_out=m_c_w_out, final_norm=m_final_norm)
    v = dict(mix_norm=v_mix_norm, ffn_norm=v_ffn_norm, w_ffn_in=v_w_ffn_in, w_ffn_out=v_w_ffn_out, ab_w_in=v_ab_w_in,
             ab_gn_gain=v_ab_gn_gain, ab_w_pool=v_ab_w_pool, ab_pool_scale=v_ab_pool_scale, ab_w_out=v_ab_w_out,
             c_w_qkv=v_c_w_qkv, c_rel_bias=v_c_rel_bias, c_w_out=v_c_w_out, final_norm=v_final_norm)
    S = x.shape[1]
    cx, cy, cc = lax.axis_index("x"), lax.axis_index("y"), lax.axis_index("c")
    chip = jnp.reshape(2 * cx + cy, (1,)).astype(jnp.int32)

    big = _Weights({k: w[k].astype(BF16) for k in _BIG})
    small = {k: w[k] for k in _SMALL}
    loss, grad_x, g_small, g_big, landed = _local_step(x.reshape(S, D_MODEL), loss_target.reshape(S, D_MODEL), small, big)

    sums = [_chip_sum(f"chip_sum_{k}", g_big[k], landed[k], chip, _SHARD_AXIS[k]) for k in _BIG]
    siblings = _pair_swap(sums)

    packed = _all_reduce_small(_pack([g_small[k] for k in _SMALL] + [loss]))
    small_like = [w[k] for k in _SMALL]
    g_red = dict(zip(_SMALL, _unpack(packed, small_like)))
    loss_row = packed.shape[0] - 8
    loss_out = packed[loss_row, 0]

    grad, delta, new_m, new_v = {}, {}, {}, {}
    for k, mine, theirs in zip(_BIG, sums, siblings):
        shp = w[k].shape
        two = (shp[0] * shp[1], shp[2])
        outs = _adamw(f"adamw_{k}", w[k].reshape(two), m[k].reshape(two), v[k].reshape(two),
                      (mine.reshape(two), theirs.reshape(two)))
        grad[k], delta[k], new_m[k], new_v[k] = [o.reshape(shp) for o in outs]
    _, d, nm, nv = _adamw("adamw_small", _pack(small_like), _pack([m[k] for k in _SMALL]), _pack([v[k] for k in _SMALL]),
                          (packed[:loss_row],))
    for k, dk, mk, vk in zip(_SMALL, _unpack(d, small_like), _unpack(nm, small_like), _unpack(nv, small_like)):
        grad[k], delta[k], new_m[k], new_v[k] = g_red[k], dk, mk, vk

    order = ("mix_norm", "ffn_norm", "w_ffn_in", "w_ffn_out", "ab_w_in", "ab_gn_gain", "ab_w_pool", "ab_pool_scale",
             "ab_w_out", "c_w_qkv", "c_rel_bias", "c_w_out", "final_norm")
    return (loss_out, grad_x.reshape(x.shape), *[grad[k] for k in order], *[delta[k] for k in order],
            *[new_m[k] for k in order], *[new_v[k] for k in order])
```

```python
import functools
from typing import Callable, NamedTuple

import numpy as np
import jax
import jax.numpy as jnp
from jax import lax
from jax.experimental import pallas as pl
from jax.experimental.pallas import tpu as pltpu

F32 = jnp.float32
BF16 = jnp.bfloat16

D_MODEL = 1024
D_FF = 4096
DEPTH = 4
CHUNK = 64
RMS_EPS = 1e-6
RET_WIDTH = 512
RET_HEADS = 4
RET_HEAD_DIM = 128
RET_ROPE_BASE = 10000.0
GN_EPS = 1e-5
POOL_WIDTH = 512
POOL_WINDOWS = (2, 4, 8, 16)
POOL_GROUP_DIM = 128
POOL_HALO = 16
AB_IN_WIDTH = 2560
ATT_HEADS = 16
ATT_HEAD_DIM = 64
LEFT_CHUNKS = 8
BAND = (LEFT_CHUNKS + 1) * CHUNK
REL_CLIP = 128
N_REL = 2 * REL_CLIP + 1
N_REL_PAD = 264
NEG_INF = -1e30
KSCALE = RET_HEAD_DIM ** -0.5
QSCALE = ATT_HEAD_DIM ** -0.5

ADAM_LR = 0.001
ADAM_B1 = 0.9
ADAM_B2 = 0.999
ADAM_EPS = 1e-08
ADAM_WD = 0.01
ADAM_STEP = 10

ATT_BLOCK = LEFT_CHUNKS * CHUNK
RET_BLOCK = 512
N_CHIPS = 4
N_DEV = 8
LANE = 128
VMEM_LIMIT = 52 * 1024 * 1024
EPI_ROWS = 256
MESH = pl.DeviceIdType.MESH


def _cp(sem, vmem=VMEM_LIMIT):
    return pltpu.CompilerParams(dimension_semantics=sem, vmem_limit_bytes=vmem)


def _dot(a, b):
    return lax.dot_general(a, b, (((1,), (0,)), ((), ())), preferred_element_type=F32)


def _dot_nt(a, b):
    return lax.dot_general(a, b, (((1,), (1,)), ((), ())), preferred_element_type=F32)


def _dot_tn(a, b):
    return lax.dot_general(a, b, (((0,), (0,)), ((), ())), preferred_element_type=F32)


_ANY = pl.BlockSpec(memory_space=pl.ANY)


class _Rider(NamedTuple):
    operands: tuple
    out_shapes: tuple
    n_sems: int
    start: Callable
    finish: Callable
    aliases: tuple = ()
    relay: Callable = None


def _mm(name, mode, a, b, *, la=None, lb=None, tm=1024, tn=1024, tk=1024, a_fn=None, b_fn=None,
        extras=(), aux=(), sides=(), epi=None, out_dtype=F32, stack=None, rider=None):
    a_parts = list(a) if isinstance(a, (list, tuple)) else [a]
    b_parts = list(b) if isinstance(b, (list, tuple)) else [b]
    na, nbp = len(a_parts), len(b_parts)
    a2, b2 = list(a_parts[0].shape[-2:]), list(b_parts[0].shape[-2:])
    a2[1] *= na
    b2[1] *= nbp
    if mode == "nn":
        (M, K), (K2, N) = a2, b2
    elif mode == "nt":
        (M, K), (N, K2) = a2, b2
    else:
        (K, M), (K2, N) = a2, b2
    assert K == K2, (name, a2, b2)
    tm, tn, tk = min(tm, M), min(tn, N), min(tk, K)
    assert M % tm == 0 and N % tn == 0 and K % tk == 0, (name, M, N, K, tm, tn, tk)
    gm, gn, gk = M // tm, N // tn, K // tk
    fold = mode == "nt" and na > 1 and gk == 1

    def specs(parts, block, idx, lead):
        per = parts[0].shape[-1] // block[1]
        assert parts[0].shape[-1] % block[1] == 0, (name, parts[0].shape, block)
        out = []
        for p in range(len(parts)):
            def f(i, j, k, p=p):
                r, c = idx(i, j, k)
                if len(parts) > 1:
                    c = jnp.clip(c - p * per, 0, per - 1)
                return (r, c) if lead is None else (lead, r, c)
            out.append(pl.BlockSpec(block if lead is None else (None,) + block, f))
        return out, per

    if mode == "nn":
        a_specs, a_per = specs(a_parts, (tm, tk), lambda i, j, k: (i, k), la)
        b_specs, b_per = specs(b_parts, (tk, tn), lambda i, j, k: (k, j), lb)
        a_axis, b_axis, dot = 2, 1, _dot
    elif mode == "nt":
        if fold:
            a_specs, a_per = [pl.BlockSpec((tm, K // na), lambda i, j, k: (i, 0)) for _ in a_parts], 1
        else:
            a_specs, a_per = specs(a_parts, (tm, tk), lambda i, j, k: (i, k), la)
        b_specs, b_per = specs(b_parts, (tn, tk), lambda i, j, k: (j, k), lb)
        a_axis, b_axis, dot = 2, 2, _dot_nt
    else:
        a_specs, a_per = specs(a_parts, (tk, tm), lambda i, j, k: (k, i), la)
        b_specs, b_per = specs(b_parts, (tk, tn), lambda i, j, k: (k, j), lb)
        a_axis, b_axis, dot = 0, 1, _dot_tn
    ex_specs = [pl.BlockSpec((tm, tn), lambda i, j, k: (i, j)) for _ in extras]
    n_ex = len(extras)

    n_aux, n_side = len(aux), len(sides)
    operands = a_parts + b_parts + list(extras) + list(aux)
    in_specs = a_specs + b_specs + ex_specs + [pl.BlockSpec(v.shape, lambda i, j, k, nd=v.ndim: (0,) * nd) for v in aux]
    aliases = {}
    if stack is None:
        out_specs = [pl.BlockSpec((tm, tn), lambda i, j, k: (i, j))]
        out_shapes = [jax.ShapeDtypeStruct((M, N), out_dtype)]
    else:
        n_layers, layer, prev = stack
        out_specs = [pl.BlockSpec((None, tm, tn), lambda i, j, k: (layer, i, j))]
        out_shapes = [jax.ShapeDtypeStruct((n_layers, M, N), out_dtype)]
        if prev is not None:
            aliases = {len(operands): 0}
            operands.append(prev)
            in_specs.append(_ANY)
    for kind, dtype in sides:
        if kind == "tile":
            out_specs.append(pl.BlockSpec((tm, tn), lambda i, j, k: (i, j)))
            out_shapes.append(jax.ShapeDtypeStruct((M, N), dtype))
        else:
            assert gn == 1, name
            out_specs.append(pl.BlockSpec((8, tn), lambda i, j, k: (0, 0)))
            out_shapes.append(jax.ShapeDtypeStruct((8, N), dtype))
    n_prev = len(aliases)
    scratch = [pltpu.VMEM((tm, tn), F32)] if gk > 1 else []
    n_rin = n_rout = 0
    if rider is not None:
        n_rin, n_rout = len(rider.operands), len(rider.out_shapes)
        for src, dst in rider.aliases:
            aliases[len(operands) + src] = 1 + n_side + dst
        operands += list(rider.operands)
        in_specs += [_ANY] * n_rin
        out_specs += [_ANY] * n_rout
        out_shapes += list(rider.out_shapes)
        scratch += [pltpu.SemaphoreType.DMA((rider.n_sems,)), pltpu.SemaphoreType.DMA((rider.n_sems,))]
    assert na == 1 or nbp == 1, name

    def body(*refs):
        a_refs, b_refs = refs[:na], refs[na:na + nbp]
        ex_refs = refs[na + nbp:na + nbp + n_ex + n_aux]
        n_in = na + nbp + n_ex + n_aux + n_prev
        rin = refs[n_in:n_in + n_rin]
        o_ref = refs[n_in + n_rin]
        side_refs = refs[n_in + n_rin + 1:n_in + n_rin + 1 + n_side]
        rout = refs[n_in + n_rin + 1 + n_side:n_in + n_rin + 1 + n_side + n_rout]
        rest = refs[n_in + n_rin + 1 + n_side + n_rout:]
        i, j, k = pl.program_id(0), pl.program_id(1), pl.program_id(2)
        if rider is not None:
            sems = rest[-2:]

            @pl.when(jnp.logical_and(i == 0, jnp.logical_and(j == 0, k == 0)))
            def _():
                rider.start(rin, rout, *sems)

        def finish(acc):
            if epi is None:
                o_ref[...] = acc[...].astype(o_ref.dtype)
                return
            strip = min(tm, EPI_ROWS)
            colsums = [None] * n_side
            for r0 in range(0, tm, strip):
                rows = slice(r0, r0 + strip)
                res = epi(acc[rows, :], *[r[rows, :] for r in ex_refs[:n_ex]], *[r[...] for r in ex_refs[n_ex:]])
                if n_side:
                    res, *side_vals = res
                    for s, ((kind, _), ref, val) in enumerate(zip(sides, side_refs, side_vals)):
                        if kind == "tile":
                            ref[rows, :] = val.astype(ref.dtype)
                        else:
                            colsums[s] = val if colsums[s] is None else colsums[s] + val
                o_ref[rows, :] = res.astype(o_ref.dtype)
            for (kind, _), ref, val in zip(sides, side_refs, colsums):
                if kind == "colsum":
                    @pl.when(i == 0)
                    def _(ref=ref, val=val):
                        ref[...] = val

                    @pl.when(i > 0)
                    def _(ref=ref, val=val):
                        ref[...] += val

                    @pl.when(i == gm - 1)
                    def _(ref=ref):
                        ref[0:1, :] = jnp.sum(ref[...], axis=0, keepdims=True)

        def step(a_ref, b_ref):
            av, bv = a_ref[...], b_ref[...]
            if a_fn is not None:
                av = a_fn(av)
            if b_fn is not None:
                bv = b_fn(bv)
            part = dot(av.astype(BF16), bv.astype(BF16))
            if gk == 1:
                finish(part)
                return
            acc_ref = rest[0]

            @pl.when(k == 0)
            def _():
                acc_ref[...] = part

            @pl.when(k > 0)
            def _():
                acc_ref[...] += part

        if fold:
            kp = K // na
            finish(sum(dot(a_refs[p][...].astype(BF16), b_refs[0][:, p * kp:(p + 1) * kp].astype(BF16))
                       for p in range(na)))
        elif na > 1:
            sel = pl.program_id(a_axis) // a_per
            for p in range(na):
                pl.when(sel == p)(functools.partial(step, a_refs[p], b_refs[0]))
        elif nbp > 1:
            sel = pl.program_id(b_axis) // b_per
            for p in range(nbp):
                pl.when(sel == p)(functools.partial(step, a_refs[0], b_refs[p]))
        else:
            step(a_refs[0], b_refs[0])
        if gk > 1:
            @pl.when(k == gk - 1)
            def _():
                finish(rest[0])

        if rider is not None:
            steps = gm * gn * gk
            step_no = (i * gn + j) * gk + k
            if rider.relay is not None:
                assert steps >= 3, name

                @pl.when(step_no == steps - 2)
                def _():
                    rider.relay(rin, rout, *sems)

            @pl.when(step_no == steps - 1)
            def _():
                rider.finish(rin, rout, *sems)

    sequential = rider is not None or any(kind == "colsum" for kind, _ in sides)
    sem = ("arbitrary",) * 3 if sequential else ("parallel", "parallel", "arbitrary")
    outs = pl.pallas_call(
        body,
        name=name,
        grid=(gm, gn, gk),
        in_specs=in_specs,
        out_specs=out_specs,
        out_shape=out_shapes,
        input_output_aliases=aliases,
        scratch_shapes=scratch,
        compiler_params=_cp(sem),
    )(*operands)
    res = outs[0] if not sides else tuple(outs[:1 + n_side])
    return res if rider is None else (res, list(outs[1 + n_side:]))


def _relu2(u):
    r = jnp.maximum(u, 0)
    return r * r


def _epi_residual(acc, res):
    return acc + res


def _epi_residual_norm(acc, res, g):
    xn = acc + res
    r = lax.rsqrt(jnp.mean(xn * xn, axis=-1, keepdims=True) + RMS_EPS)
    return xn, (xn * r) * g


def _epi_rms_bwd(dh, x, dres, g):
    r = lax.rsqrt(jnp.mean(x * x, axis=-1, keepdims=True) + RMS_EPS)
    xh = x * r
    dxh = dh * g
    dx = dres + r * (dxh - xh * jnp.mean(dxh * xh, axis=-1, keepdims=True))
    return dx, dx, jnp.sum((dh * xh).reshape(dh.shape[0] // 8, 8, dh.shape[1]), axis=0)


def _rms_fwd(name, x, g):
    S, D = x.shape
    tq = min(1024, S)

    def body(x_ref, g_ref, o_ref):
        xv = x_ref[...]
        r = lax.rsqrt(jnp.mean(xv * xv, axis=-1, keepdims=True) + RMS_EPS)
        o_ref[...] = ((xv * r) * g_ref[...]).astype(o_ref.dtype)

    return pl.pallas_call(
        body,
        name=name,
        grid=(S // tq,),
        in_specs=[pl.BlockSpec((tq, D), lambda i: (i, 0)), pl.BlockSpec((1, D), lambda i: (0, 0))],
        out_specs=pl.BlockSpec((tq, D), lambda i: (i, 0)),
        out_shape=jax.ShapeDtypeStruct((S, D), BF16),
        compiler_params=_cp(("parallel",)),
    )(x, g)


def _loss_head(x, g, t):
    S, D = x.shape
    tq = min(512, S)
    n = S // tq

    def body(x_ref, g_ref, t_ref, loss_ref, dx_ref, dxb_ref, dg_ref, lacc_ref, gacc_ref):
        i = pl.program_id(0)
        xv = x_ref[...]
        gv = g_ref[...]
        r = lax.rsqrt(jnp.mean(xv * xv, axis=-1, keepdims=True) + RMS_EPS)
        xh = xv * r
        e = xh * gv - t_ref[...]
        dy = e * (1.0 / D)
        dxh = dy * gv
        dx = r * (dxh - xh * jnp.mean(dxh * xh, axis=-1, keepdims=True))
        dx_ref[...] = dx
        dxb_ref[...] = dx.astype(dxb_ref.dtype)
        lpart = jnp.sum((e * e).reshape(tq // 8, 8, D), axis=0)
        gpart = jnp.sum((dy * xh).reshape(tq // 8, 8, D), axis=0)

        @pl.when(i == 0)
        def _():
            lacc_ref[...] = lpart
            gacc_ref[...] = gpart

        @pl.when(i > 0)
        def _():
            lacc_ref[...] += lpart
            gacc_ref[...] += gpart

        @pl.when(i == n - 1)
        def _():
            dg_ref[...] = jnp.sum(gacc_ref[...], axis=0, keepdims=True)
            tot = jnp.sum(jnp.sum(lacc_ref[...], axis=0, keepdims=True), axis=1, keepdims=True)
            loss_ref[...] = jnp.broadcast_to(tot * (0.5 / D), (1, LANE))

    return pl.pallas_call(
        body,
        name="loss_head",
        grid=(n,),
        in_specs=[pl.BlockSpec((tq, D), lambda i: (i, 0)), pl.BlockSpec((1, D), lambda i: (0, 0)),
                  pl.BlockSpec((tq, D), lambda i: (i, 0))],
        out_specs=[pl.BlockSpec((1, LANE), lambda i: (0, 0)), pl.BlockSpec((tq, D), lambda i: (i, 0)),
                   pl.BlockSpec((tq, D), lambda i: (i, 0)), pl.BlockSpec((1, D), lambda i: (0, 0))],
        out_shape=[jax.ShapeDtypeStruct((1, LANE), F32), jax.ShapeDtypeStruct((S, D), F32),
                   jax.ShapeDtypeStruct((S, D), BF16), jax.ShapeDtypeStruct((1, D), F32)],
        scratch_shapes=[pltpu.VMEM((8, D), F32), pltpu.VMEM((8, D), F32)],
        compiler_params=_cp(("arbitrary",)),
    )(x, g, t)


def _ret_tables(S):
    T = min(RET_BLOCK, S)
    inv_freq = 1.0 / (RET_ROPE_BASE ** jnp.linspace(0.0, 1.0, RET_HEAD_DIM // 2, dtype=F32))
    ang = jnp.arange(S, dtype=F32)[:, None] * inv_freq[None, :]
    cos, sin = jnp.cos(ang), jnp.sin(ang)
    cosf = jnp.repeat(cos, 2, axis=-1)
    sins = jnp.stack([-sin, sin], axis=-1).reshape(S, RET_HEAD_DIM)
    log_g = np.log1p(-np.power(2.0, -5.0 - np.arange(RET_HEADS, dtype=np.float64)))
    pos = np.arange(T, dtype=np.float64)
    diff = pos[:, None] - pos[None, :]
    same = (pos[:, None] // CHUNK) == (pos[None, :] // CHUNK)
    seen = same | (diff > 0)
    dmat = np.where(seen[None], np.exp(np.abs(diff)[None] * log_g[:, None, None]), 0.0)
    aq = np.exp((pos[None, :] + 1.0) * log_g[:, None])
    ak = np.exp((T - 1.0 - pos[None, :]) * log_g[:, None])
    lam = np.exp(T * log_g)
    bc = lambda v: jnp.asarray(np.broadcast_to(v[..., None], v.shape + (LANE,)), F32)
    return dict(cos=cosf, sin=sins, dmat=jnp.asarray(dmat, F32), aq=bc(aq), ak=bc(ak),
                lam=jnp.asarray(np.broadcast_to(lam[:, None, None], (RET_HEADS, 1, LANE)), F32))


def _rot(x, cos, sin_s, even):
    sw = jnp.where(even, pltpu.roll(x, LANE - 1, 1), pltpu.roll(x, 1, 1))
    return x * cos + sw * sin_s


def _rot_t(dy, cos, sin_s, even):
    t = dy * sin_s
    return dy * cos + jnp.where(even, pltpu.roll(t, LANE - 1, 1), pltpu.roll(t, 1, 1))


def _ret_specs(T, rev_nb=None):
    blk = (lambda b: b) if rev_nb is None else (lambda b: rev_nb - 1 - b)
    whole = lambda shape: pl.BlockSpec(shape, lambda b: (0,) * len(shape))
    specs = [pl.BlockSpec((T, AB_IN_WIDTH), lambda b: (blk(b), 0)),
             pl.BlockSpec((T, LANE), lambda b: (blk(b), 0)),
             pl.BlockSpec((T, LANE), lambda b: (blk(b), 0)),
             whole((RET_HEADS, T, T)), whole((RET_HEADS, T, LANE)), whole((RET_HEADS, T, LANE)),
             whole((RET_HEADS, 1, LANE)), whole((1, RET_WIDTH))]
    return specs, blk


def _head_views(h, z_ref, tabs, token_refs, head_refs):
    zs = [z_ref.at[:, (o * RET_HEADS + h) * LANE:(o * RET_HEADS + h + 1) * LANE] for o in range(4)]
    hs = slice(h * LANE, (h + 1) * LANE)
    return zs, [t.at[h] for t in tabs], [r.at[:, hs] for r in token_refs], [r.at[h] for r in head_refs]


def _ret_fwd(name, z, tb, gain):
    S = z.shape[0]
    T = min(RET_BLOCK, S)
    nb = S // T
    specs, blk = _ret_specs(T)

    def body(z_ref, cos_r, sin_r, d_all, aq_all, ak_all, lam_all, gain_all, cat_all, opre_all, st_all, state_all):
        @pl.when(pl.program_id(0) == 0)
        def _():
            state_all[...] = jnp.zeros_like(state_all)

        for h in range(RET_HEADS):
            zs, tabs, toks, heads = _head_views(h, z_ref, (d_all, aq_all, ak_all, lam_all),
                                                (gain_all, cat_all, opre_all), (st_all, state_all))
            head(*zs, cos_r, sin_r, *tabs, *toks, *heads)

    def head(zq, zk, zv, zg, cos_r, sin_r, d_r, aq_r, ak_r, lam_r, gain_r, ret_o, opre_o, st_o, state):
        even = (lax.broadcasted_iota(jnp.int32, (T, LANE), 1) & 1) == 0
        c, s = cos_r[...], sin_r[...]
        q = _rot(zq[...].astype(F32), c, s, even)
        k = _rot(zk[...].astype(F32), c, s, even) * KSCALE
        qb, kb, vb = q.astype(BF16), k.astype(BF16), zv[...].astype(BF16)
        p = (_dot_nt(qb, kb) * d_r[...]).astype(BF16)
        st = state[...]
        st_o[...] = st
        o = _dot(p, vb) + _dot((q * aq_r[...]).astype(BF16), st.astype(BF16))
        state[...] = st * lam_r[...] + _dot_tn((k * ak_r[...]).astype(BF16), vb)
        opre_o[...] = o
        mu = jnp.mean(o, axis=-1, keepdims=True)
        d = o - mu
        y = d * lax.rsqrt(jnp.mean(d * d, axis=-1, keepdims=True) + GN_EPS)
        g = zg[...].astype(F32)
        ret_o[...] = ((g * jax.nn.sigmoid(g)) * (y * gain_r[...])).astype(ret_o.dtype)

    out_blk = pl.BlockSpec((T, RET_WIDTH), lambda b: (b, 0))
    return pl.pallas_call(
        body,
        name=name,
        grid=(nb,),
        in_specs=specs,
        out_specs=[out_blk, out_blk, pl.BlockSpec((RET_HEADS, None, LANE, LANE), lambda b: (0, b, 0, 0))],
        out_shape=[jax.ShapeDtypeStruct((S, D_MODEL), BF16), jax.ShapeDtypeStruct((S, RET_WIDTH), F32),
                   jax.ShapeDtypeStruct((RET_HEADS, nb, LANE, LANE), F32)],
        scratch_shapes=[pltpu.VMEM((RET_HEADS, LANE, LANE), F32)],
        compiler_params=_cp(("arbitrary",)),
    )(z, tb["cos"], tb["sin"], tb["dmat"], tb["aq"], tb["ak"], tb["lam"], gain)


def _ret_bwd(name, z, tb, gain, opre, states, dcat):
    S = z.shape[0]
    T = min(RET_BLOCK, S)
    nb = S // T
    specs, blk = _ret_specs(T, rev_nb=nb)
    tok = pl.BlockSpec((T, RET_WIDTH), lambda b: (blk(b), 0))

    def body(z_ref, cos_r, sin_r, d_all, aq_all, ak_all, lam_all, gain_all, opre_all, st_all, dret_all,
             dz_ref, dgain_all, dstate_all):
        @pl.when(pl.program_id(0) == 0)
        def _():
            dstate_all[...] = jnp.zeros_like(dstate_all)
            dgain_all[...] = jnp.zeros_like(dgain_all)

        for h in range(RET_HEADS):
            zs, tabs, toks, heads = _head_views(h, z_ref, (d_all, aq_all, ak_all, lam_all),
                                                (gain_all, opre_all, dret_all, dgain_all), (st_all, dstate_all))
            dzs, _, _, _ = _head_views(h, dz_ref, (), (), ())
            gain_r, opre_r, dret_r, dgain_o = toks
            head(*zs, cos_r, sin_r, *tabs, gain_r, opre_r, heads[0], dret_r, *dzs, dgain_o, heads[1])

    def head(zq, zk, zv, zg, cos_r, sin_r, d_r, aq_r, ak_r, lam_r, gain_r, opre_r, st_r, dret_r,
             dq_o, dk_o, dv_o, dg_o, dgain_o, dstate):
        even = (lax.broadcasted_iota(jnp.int32, (T, LANE), 1) & 1) == 0
        c, s = cos_r[...], sin_r[...]
        aq, ak, dm = aq_r[...], ak_r[...], d_r[...]
        q = _rot(zq[...].astype(F32), c, s, even)
        k = _rot(zk[...].astype(F32), c, s, even) * KSCALE
        qb, kb, vb = q.astype(BF16), k.astype(BF16), zv[...].astype(BF16)
        pb = (_dot_nt(qb, kb) * dm).astype(BF16)
        g = zg[...].astype(F32)
        sig = jax.nn.sigmoid(g)
        o = opre_r[...]
        mu = jnp.mean(o, axis=-1, keepdims=True)
        d = o - mu
        rstd = lax.rsqrt(jnp.mean(d * d, axis=-1, keepdims=True) + GN_EPS)
        y = d * rstd
        gain_v = gain_r[...]
        dret = dret_r[...].astype(F32)
        dyg = dret * (g * sig)
        dg_o[...] = (dret * (y * gain_v) * (sig * (1.0 + g * (1.0 - sig)))).astype(dg_o.dtype)
        dgain_o[...] += jnp.sum(dyg * y, axis=0, keepdims=True)
        dy = dyg * gain_v
        do = rstd * (dy - jnp.mean(dy, axis=-1, keepdims=True) - y * jnp.mean(dy * y, axis=-1, keepdims=True))
        dob = do.astype(BF16)
        stb = st_r[...].astype(BF16)
        dsn = dstate[...]
        dsnb = dsn.astype(BF16)
        dpb = (_dot_nt(dob, vb) * dm).astype(BF16)
        dq = _dot(dpb, kb) + _dot_nt(dob, stb) * aq
        dk = _dot_tn(dpb, qb) + _dot_nt(vb, dsnb) * ak
        dv = _dot_tn(pb, dob) + _dot((k * ak).astype(BF16), dsnb)
        dstate[...] = dsn * lam_r[...] + _dot_tn((q * aq).astype(BF16), dob)
        dq_o[...] = _rot_t(dq, c, s, even).astype(dq_o.dtype)
        dk_o[...] = _rot_t(dk * KSCALE, c, s, even).astype(dk_o.dtype)
        dv_o[...] = dv.astype(dv_o.dtype)

    return pl.pallas_call(
        body,
        name=name,
        grid=(nb,),
        in_specs=specs + [tok, pl.BlockSpec((RET_HEADS, None, LANE, LANE), lambda b: (0, blk(b), 0, 0)), tok],
        out_specs=[pl.BlockSpec((T, 4 * RET_WIDTH), lambda b: (blk(b), 0)), pl.BlockSpec((1, RET_WIDTH), lambda b: (0, 0))],
        out_shape=[jax.ShapeDtypeStruct((S, AB_IN_WIDTH), BF16), jax.ShapeDtypeStruct((1, RET_WIDTH), F32)],
        scratch_shapes=[pltpu.VMEM((RET_HEADS, LANE, LANE), F32)],
        compiler_params=_cp(("arbitrary",)),
    )(z, tb["cos"], tb["sin"], tb["dmat"], tb["aq"], tb["ak"], tb["lam"], gain, opre, states, dcat)


def _pool_counts(t0, rows):
    t = t0 + lax.broadcasted_iota(jnp.int32, (rows, POOL_WIDTH), 0)
    grp = lax.broadcasted_iota(jnp.int32, (rows, POOL_WIDTH), 1) >> 7
    win = jnp.where(grp == 0, POOL_WINDOWS[0], jnp.where(grp == 1, POOL_WINDOWS[1],
                    jnp.where(grp == 2, POOL_WINDOWS[2], POOL_WINDOWS[3])))
    return jnp.maximum(jnp.minimum(t + 1, win), 1).astype(F32), grp


def _window_sums(ext, grp, sign):
    n = ext.shape[0]
    sh = lambda v, k: pltpu.roll(v, k % n if sign > 0 else (n - k) % n, 0)
    s2 = ext + sh(ext, 1)
    s4 = s2 + sh(s2, 2)
    s8 = s4 + sh(s4, 4)
    s16 = s8 + sh(s8, 8)
    return jnp.where(grp == 0, s2, jnp.where(grp == 1, s4, jnp.where(grp == 2, s8, s16)))


def _pool_fwd(name, z, w_pool, scale, cat):
    S = z.shape[0]
    T = min(512, S)
    nb = S // T
    pcol = AB_IN_WIDTH // POOL_WIDTH - 1
    hb = T // POOL_HALO

    def body(p_ref, halo_ref, w_ref, sc_ref, cat_in, out_ref, pooled_ref):
        b = pl.program_id(0)
        cur = p_ref[...].astype(F32)
        halo = jnp.where(b > 0, halo_ref[...].astype(F32), 0.0)
        ext = jnp.concatenate([halo, cur], axis=0)
        cnt, grp = _pool_counts(b * T - POOL_HALO, T + POOL_HALO)
        sums = _window_sums(ext, grp, +1)
        pooled = (sums / cnt)[POOL_HALO:] - cur
        pb = pooled.astype(BF16)
        pooled_ref[...] = pb
        for gi in range(len(POOL_WINDOWS)):
            cs = slice(gi * POOL_GROUP_DIM, (gi + 1) * POOL_GROUP_DIM)
            mixed = _dot(pb[:, cs], w_ref[gi].astype(BF16))
            out_ref[:, cs] = (mixed * sc_ref[:, cs]).astype(out_ref.dtype)

    return pl.pallas_call(
        body,
        name=name,
        grid=(nb,),
        in_specs=[pl.BlockSpec((T, POOL_WIDTH), lambda b: (b, pcol)),
                  pl.BlockSpec((POOL_HALO, POOL_WIDTH), lambda b: (jnp.maximum(b * hb - 1, 0), pcol)),
                  pl.BlockSpec((4, POOL_GROUP_DIM, POOL_GROUP_DIM), lambda b: (0, 0, 0)),
                  pl.BlockSpec((1, POOL_WIDTH), lambda b: (0, 0)), _ANY],
        out_specs=[pl.BlockSpec((T, POOL_WIDTH), lambda b: (b, 1)), pl.BlockSpec((T, POOL_WIDTH), lambda b: (b, 0))],
        out_shape=[jax.ShapeDtypeStruct(cat.shape, cat.dtype), jax.ShapeDtypeStruct((S, POOL_WIDTH), BF16)],
        input_output_aliases={4: 0},
        compiler_params=_cp(("parallel",)),
    )(z, z, w_pool, scale, cat)


def _pool_bwd(name, pooled, w_pool, scale, dcat, dz):
    S = pooled.shape[0]
    T = min(512, S)
    nb = S // T
    hb = T // POOL_HALO
    last_h = S // POOL_HALO - 1
    pcol = AB_IN_WIDTH // POOL_WIDTH - 1

    def body(d_ref, dn_ref, pooled_ref, w_ref, sc_ref, dz_in, dp_ref, dw_ref, dsc_ref):
        b = pl.program_id(0)

        @pl.when(b == 0)
        def _():
            dw_ref[...] = jnp.zeros_like(dw_ref)
            dsc_ref[...] = jnp.zeros_like(dsc_ref)

        sc = sc_ref[...]
        dout = d_ref[...].astype(F32)
        dnext = jnp.where(b < nb - 1, dn_ref[...].astype(F32), 0.0)
        dmix = jnp.concatenate([dout, dnext], axis=0) * sc
        dmb = dmix.astype(BF16)
        pb = pooled_ref[...]
        dpooled = []
        for gi in range(len(POOL_WINDOWS)):
            cs = slice(gi * POOL_GROUP_DIM, (gi + 1) * POOL_GROUP_DIM)
            wb = w_ref[gi].astype(BF16)
            dpooled.append(_dot_nt(dmb[:, cs], wb))
            dw_ref[gi] += _dot_tn(pb[:, cs], dmb[:T, cs])
            mixed = _dot(pb[:, cs], wb)
            dsc_ref[:, cs] += jnp.sum(dout[:, cs] * mixed, axis=0, keepdims=True)
        dpl = jnp.concatenate(dpooled, axis=1)
        cnt, grp = _pool_counts(b * T, T + POOL_HALO)
        sums = _window_sums(dpl / cnt, grp, -1)
        dp_ref[...] = (sums[:T] - dpl[:T]).astype(dp_ref.dtype)

    return pl.pallas_call(
        body,
        name=name,
        grid=(nb,),
        in_specs=[pl.BlockSpec((T, POOL_WIDTH), lambda b: (b, 1)),
                  pl.BlockSpec((POOL_HALO, POOL_WIDTH), lambda b: (jnp.minimum((b + 1) * hb, last_h), 1)),
                  pl.BlockSpec((T, POOL_WIDTH), lambda b: (b, 0)),
                  pl.BlockSpec((4, POOL_GROUP_DIM, POOL_GROUP_DIM), lambda b: (0, 0, 0)),
                  pl.BlockSpec((1, POOL_WIDTH), lambda b: (0, 0)), _ANY],
        out_specs=[pl.BlockSpec((T, POOL_WIDTH), lambda b: (b, pcol)),
                   pl.BlockSpec((4, POOL_GROUP_DIM, POOL_GROUP_DIM), lambda b: (0, 0, 0)),
                   pl.BlockSpec((1, POOL_WIDTH), lambda b: (0, 0))],
        out_shape=[jax.ShapeDtypeStruct(dz.shape, dz.dtype),
                   jax.ShapeDtypeStruct((4, POOL_GROUP_DIM, POOL_GROUP_DIM), F32),
                   jax.ShapeDtypeStruct((1, POOL_WIDTH), F32)],
        input_output_aliases={5: 0},
        compiler_params=_cp(("arbitrary",)),
    )(dcat, dcat, pooled, w_pool, scale, dz)


ATT_STRIP = 32
ATT_Q = 256
ATT_W = ATT_Q + LEFT_CHUNKS * CHUNK


def _rel_index():
    j = np.arange(ATT_W)
    rel = np.clip(LEFT_CHUNKS * CHUNK - j, -REL_CLIP, REL_CLIP) + REL_CLIP
    fwd = np.where(j < BAND, rel, N_REL)
    bwd = np.where(j <= ATT_W - CHUNK, fwd, 2 * REL_CLIP)
    return tuple(jnp.asarray(v.reshape(1, ATT_W), jnp.int32) for v in (fwd, bwd))


def _bias_table(name, rel_bias, rel_idx):
    rb = jnp.concatenate([rel_bias, jnp.full((ATT_HEADS, 1), NEG_INF, F32),
                          jnp.zeros((ATT_HEADS, N_REL_PAD - N_REL - 1), F32)], axis=1)

    def body(rb_ref, idx_ref, o_ref, row0_ref):
        r = lax.broadcasted_iota(jnp.int32, (N_REL_PAD, ATT_W), 0)
        onehot = (r == idx_ref[...]).astype(F32)
        row0_ref[...] = jnp.dot(rb_ref[...], onehot, precision=lax.Precision.HIGHEST, preferred_element_type=F32)
        col = lax.broadcasted_iota(jnp.int32, (CHUNK, ATT_W), 1)
        row = lax.broadcasted_iota(jnp.int32, (CHUNK, ATT_W), 0)
        for h in range(ATT_HEADS):
            same = jnp.broadcast_to(row0_ref[pl.ds(h, 1), :], (CHUNK, ATT_W))
            turned = pltpu.roll(same, 0, 1, stride=1, stride_axis=0)
            o_ref[h] = jnp.where(col >= BAND, NEG_INF, jnp.where(col < row, same, turned))

    return pl.pallas_call(
        body,
        name=name,
        out_shape=jax.ShapeDtypeStruct((ATT_HEADS, CHUNK, ATT_W), F32),
        scratch_shapes=[pltpu.VMEM((ATT_HEADS, ATT_W), F32)],
        compiler_params=pltpu.CompilerParams(vmem_limit_bytes=VMEM_LIMIT),
    )(rb, rel_idx)


def _bias_grad(name, dband, rel_idx):
    def body(d_ref, idx_ref, o_ref, sums_ref):
        row = lax.broadcasted_iota(jnp.int32, (CHUNK, ATT_W), 0)
        for h in range(ATT_HEADS):
            back = d_ref[h]
            for bit in range(CHUNK.bit_length() - 1):
                back = jnp.where(((row >> bit) & 1) == 1, pltpu.roll(back, ATT_W - (1 << bit), 1), back)
            sums_ref[pl.ds(h, 1), :] = jnp.sum(back, axis=0, keepdims=True)
        r = lax.broadcasted_iota(jnp.int32, (N_REL_PAD, ATT_W), 0)
        onehot = (r == idx_ref[...]).astype(F32)
        o_ref[...] = lax.dot_general(sums_ref[...], onehot, (((1,), (1,)), ((), ())),
                                     precision=lax.Precision.HIGHEST, preferred_element_type=F32)

    out = pl.pallas_call(
        body,
        name=name,
        out_shape=jax.ShapeDtypeStruct((ATT_HEADS, N_REL_PAD), F32),
        scratch_shapes=[pltpu.VMEM((ATT_HEADS, ATT_W), F32)],
        compiler_params=pltpu.CompilerParams(vmem_limit_bytes=VMEM_LIMIT),
    )(dband, rel_idx)
    return out[:, :N_REL]


def _attn_unit(q_ref, kw_ref, bias_ref, e, u, lane):
    mine = (lane < ATT_HEAD_DIM) if e == 0 else (lane >= ATT_HEAD_DIM)
    qm = jnp.where(mine, q_ref[u * ATT_Q:(u + 1) * ATT_Q, :] * QSCALE, 0)
    kw = kw_ref[u * ATT_Q:u * ATT_Q + ATT_W, :]
    s = _dot_nt(qm, kw) + bias_ref[u, e]
    p = jnp.exp(s - jnp.max(s, axis=-1, keepdims=True))
    return p, 1.0 / jnp.sum(p, axis=-1, keepdims=True), qm, kw, mine


def _attn_in_specs(nb):
    T = ATT_BLOCK
    hp = ATT_HEADS // 2
    cur = lambda off: pl.BlockSpec((T, LANE), lambda h, b: (jnp.minimum(b, nb - 1), off + h))
    prev = lambda off: pl.BlockSpec((T, LANE), lambda h, b: (jnp.clip(b - 1, 0, nb - 1), off + h))
    return [cur(0), prev(hp), cur(hp), prev(2 * hp), cur(2 * hp),
            pl.BlockSpec((None, 2, CHUNK, ATT_W), lambda h, b: (h, 0, 0, 0))]


def _spread_bias(bias_ref, bm_ref, block):
    col = lax.broadcasted_iota(jnp.int32, (CHUNK, ATT_W), 1)
    for first in (True, False):
        @pl.when(block == (0 if first else 1))
        def _(first=first):
            for u in range(ATT_BLOCK // ATT_Q):
                for e in range(2):
                    for j in range(ATT_Q // CHUNK):
                        rows = pltpu.roll(bias_ref[e], j * CHUNK, 1)
                        if first:
                            rows = jnp.where(col >= ATT_BLOCK - u * ATT_Q, rows, NEG_INF)
                        bm_ref[u, e, j * CHUNK:(j + 1) * CHUNK, :] = rows


def _attn_fwd(name, qkv, bias):
    S = qkv.shape[0]
    T = ATT_BLOCK
    nb = S // T

    def body(q_ref, kp_ref, kc_ref, vp_ref, vc_ref, band_ref, o_ref, kw_ref, vw_ref, bias_ref, s_ref, p_ref, inv_ref):
        _spread_bias(band_ref, bias_ref, pl.program_id(1))
        kw_ref[0:T] = kp_ref[...]
        kw_ref[T:2 * T] = kc_ref[...]
        vw_ref[0:T] = vp_ref[...]
        vw_ref[T:2 * T] = vc_ref[...]
        lane = lax.broadcasted_iota(jnp.int32, (ATT_Q, LANE), 1)
        for u in range(T // ATT_Q):
            vw = vw_ref[u * ATT_Q:u * ATT_Q + ATT_W, :]
            kw = kw_ref[u * ATT_Q:u * ATT_Q + ATT_W, :]
            outs = []
            for e in range(2):
                mine = (lane < ATT_HEAD_DIM) if e == 0 else (lane >= ATT_HEAD_DIM)
                qm = jnp.where(mine, q_ref[u * ATT_Q:(u + 1) * ATT_Q, :] * QSCALE, 0)
                s_ref[e] = _dot_nt(qm, kw)
                for r in range(ATT_Q // ATT_STRIP):
                    rows = slice(r * ATT_STRIP, (r + 1) * ATT_STRIP)
                    s = s_ref[e, rows, :] + bias_ref[u, e, rows, :]
                    p = jnp.exp(s - jnp.max(s, axis=-1, keepdims=True))
                    inv_ref[e, rows, :] = jnp.broadcast_to(1.0 / jnp.sum(p, axis=-1, keepdims=True), (ATT_STRIP, LANE))
                    p_ref[e, rows, :] = p.astype(BF16)
                outs.append(_dot(p_ref[e], vw) * inv_ref[e])
            o_ref[u * ATT_Q:(u + 1) * ATT_Q, :] = jnp.where(lane < ATT_HEAD_DIM, outs[0], outs[1]).astype(o_ref.dtype)

    return pl.pallas_call(
        body,
        name=name,
        grid=(ATT_HEADS // 2, nb),
        in_specs=_attn_in_specs(nb),
        out_specs=pl.BlockSpec((T, LANE), lambda h, b: (b, h)),
        out_shape=jax.ShapeDtypeStruct((S, D_MODEL), BF16),
        scratch_shapes=[pltpu.VMEM((2 * T, LANE), BF16), pltpu.VMEM((2 * T, LANE), BF16),
                        pltpu.VMEM((T // ATT_Q, 2, ATT_Q, ATT_W), F32), pltpu.VMEM((2, ATT_Q, ATT_W), F32),
                        pltpu.VMEM((2, ATT_Q, ATT_W), BF16), pltpu.VMEM((2, ATT_Q, LANE), F32)],
        compiler_params=_cp(("parallel", "arbitrary")),
    )(qkv, qkv, qkv, qkv, qkv, bias)


def _attn_bwd(name, qkv, bias, do):
    S = qkv.shape[0]
    T = ATT_BLOCK
    nb = S // T

    def body(q_ref, kp_ref, kc_ref, vp_ref, vc_ref, band_ref, do_ref,
             dq_ref, dk_ref, dv_ref, dband_ref, kw_ref, vw_ref, dkw_ref, dvw_ref, bias_ref, dbias_ref):
        b = pl.program_id(1)

        _spread_bias(band_ref, bias_ref, b)

        @pl.when(b == 0)
        def _():
            dbias_ref[...] = jnp.zeros_like(dbias_ref)
            dkw_ref[:, T:2 * T] = jnp.zeros((LANE, T), F32)
            dvw_ref[:, T:2 * T] = jnp.zeros((LANE, T), F32)

        dkw_ref[:, 0:T] = dkw_ref[:, T:2 * T]
        dvw_ref[:, 0:T] = dvw_ref[:, T:2 * T]
        dkw_ref[:, T:2 * T] = jnp.zeros((LANE, T), F32)
        dvw_ref[:, T:2 * T] = jnp.zeros((LANE, T), F32)

        @pl.when(b < nb)
        def _():
            kw_ref[0:T] = kp_ref[...]
            kw_ref[T:2 * T] = kc_ref[...]
            vw_ref[0:T] = vp_ref[...]
            vw_ref[T:2 * T] = vc_ref[...]
            lane = lax.broadcasted_iota(jnp.int32, (ATT_Q, LANE), 1)
            for u in range(T // ATT_Q):
                rows = slice(u * ATT_Q, (u + 1) * ATT_Q)
                win = slice(u * ATT_Q, u * ATT_Q + ATT_W)
                vw = vw_ref[win, :]
                do2 = do_ref[rows, :]
                dqs, dk, dv = [], None, None
                for e in range(2):
                    p, inv, qm, kw, mine = _attn_unit(q_ref, kw_ref, bias_ref, e, u, lane)
                    dom = jnp.where(mine, do2, 0)
                    dp = _dot_nt(dom, vw)
                    delta = jnp.sum(p * dp, axis=-1, keepdims=True) * inv
                    ds = p * ((dp - delta) * inv)
                    dbias_ref[e] += ds
                    dsb = ds.astype(BF16)
                    dqs.append(_dot(dsb, kw))
                    dk_e = _dot_tn(qm, dsb)
                    dv_e = _dot_tn((dom * inv).astype(BF16), p.astype(BF16))
                    dk = dk_e if dk is None else dk + dk_e
                    dv = dv_e if dv is None else dv + dv_e
                dq_ref[rows, :] = (jnp.where(lane < ATT_HEAD_DIM, dqs[0], dqs[1]) * QSCALE).astype(dq_ref.dtype)
                dkw_ref[:, win] += dk
                dvw_ref[:, win] += dv

        @pl.when(b > 0)
        def _():
            dk_ref[...] = dkw_ref[:, 0:T].T.astype(dk_ref.dtype)
            dv_ref[...] = dvw_ref[:, 0:T].T.astype(dv_ref.dtype)

        @pl.when(b == nb)
        def _():
            for e in range(2):
                acc = dbias_ref[e, 0:CHUNK, :]
                for j in range(1, ATT_Q // CHUNK):
                    acc = acc + pltpu.roll(dbias_ref[e, j * CHUNK:(j + 1) * CHUNK, :], ATT_W - j * CHUNK, 1)
                dband_ref[e] = acc

    tok = jax.ShapeDtypeStruct((S, D_MODEL), BF16)
    prev_out = pl.BlockSpec((T, LANE), lambda h, b: (jnp.maximum(b - 1, 0), h))
    return pl.pallas_call(
        body,
        name=name,
        grid=(ATT_HEADS // 2, nb + 1),
        in_specs=_attn_in_specs(nb) + [pl.BlockSpec((T, LANE), lambda h, b: (jnp.minimum(b, nb - 1), h))],
        out_specs=[pl.BlockSpec((T, LANE), lambda h, b: (jnp.minimum(b, nb - 1), h)), prev_out, prev_out,
                   pl.BlockSpec((None, 2, CHUNK, ATT_W), lambda h, b: (h, 0, 0, 0))],
        out_shape=[tok, tok, tok, jax.ShapeDtypeStruct((ATT_HEADS // 2, 2, CHUNK, ATT_W), F32)],
        scratch_shapes=[pltpu.VMEM((2 * T, LANE), BF16), pltpu.VMEM((2 * T, LANE), BF16),
                        pltpu.VMEM((LANE, 2 * T), F32), pltpu.VMEM((LANE, 2 * T), F32),
                        pltpu.VMEM((T // ATT_Q, 2, ATT_Q, ATT_W), F32), pltpu.VMEM((2, ATT_Q, ATT_W), F32)],
        compiler_params=_cp(("parallel", "arbitrary")),
    )(qkv, qkv, qkv, qkv, qkv, bias, do)


def _local_step(x, target, small, W):
    S = x.shape[0]
    tb = _ret_tables(S)
    rel_fwd, rel_bwd = _rel_index()
    saved = []
    normed = (("tile", BF16),)
    deep = dict(tm=512, tk=D_FF)
    h = _rms_fwd("mix_norm_fwd0", x, small["mix_norm"][0:1])
    for layer in range(DEPTH):
        i = layer // 2
        st = {"x_in": x, "h": h}
        g_ffn = small["ffn_norm"][layer:layer + 1]
        if layer % 2 == 0:
            z = W.mm(f"ab_in_fwd{layer}", "nn", h, W.get("ab_w_in", i), tm=2048, tn=640, out_dtype=BF16)
            gain = small["ab_gn_gain"][i:i + 1]
            cat, opre, states = _ret_fwd(f"ret_fwd{layer}", z, tb, gain)
            cat, pooled = _pool_fwd(f"pool_fwd{layer}", z, small["ab_w_pool"][i], small["ab_pool_scale"][i:i + 1], cat)
            st.update(z=z, opre=opre, states=states, pooled=pooled, cat=cat)
            x, hn = W.mm(f"ab_out_fwd{layer}", "nn", cat, W.get("ab_w_out", i), extras=(x,), aux=(g_ffn,), sides=normed,
                         epi=_epi_residual_norm)
        else:
            qkv = W.mm(f"qkv_fwd{layer}", "nn", h, W.get("c_w_qkv", i), tm=2048, out_dtype=BF16)
            bias = _bias_table(f"bias_table{layer}", small["c_rel_bias"][i], rel_fwd)
            bias = bias.reshape(ATT_HEADS // 2, 2, CHUNK, ATT_W)
            att = _attn_fwd(f"attn_fwd{layer}", qkv, bias)
            st.update(qkv=qkv, bias=bias, att=att)
            x, hn = W.mm(f"c_out_fwd{layer}", "nn", att, W.get("c_w_out", i), extras=(x,), aux=(g_ffn,), sides=normed,
                         epi=_epi_residual_norm)
        st["x_mid"] = x
        u = W.mm(f"ffn_in_fwd{layer}", "nn", hn, W.get("w_ffn_in", layer), out_dtype=BF16, tm=2048)
        if layer + 1 < DEPTH:
            x, h = W.mm(f"ffn_out_fwd{layer}", "nn", u, W.get("w_ffn_out", layer), a_fn=_relu2, extras=(x,),
                        aux=(small["mix_norm"][layer + 1:layer + 2],), sides=normed, epi=_epi_residual_norm, **deep)
        else:
            x = W.mm(f"ffn_out_fwd{layer}", "nn", u, W.get("w_ffn_out", layer), a_fn=_relu2, extras=(x,),
                     epi=_epi_residual, **deep)
        st.update(hn=hn, u=u)
        saved.append(st)

    loss, dx, dxb, d_final = _loss_head(x, small["final_norm"].reshape(1, D_MODEL), target)

    gs = {k: [None] * v.shape[0] for k, v in small.items() if k != "final_norm"}
    gb = {k: None for k in W.n_layers}
    landed = {k: None for k in W.n_layers}
    pending = []

    def host(name, *args, take=1, **kw):
        items = [pending.pop(0) for _ in range(min(take, len(pending)))]
        if not items:
            return _mm(name, *args, **kw)
        riders = [_grad_rider(key, idx, gb[key], landed[key]) for key, idx in items]
        res, outs = _mm(name, *args, rider=_join_riders(riders), **kw)
        for (key, _), out in zip(items, outs):
            landed[key] = out
        return res

    def dw(name, key, idx, a, b, call=_mm, **kw):
        gb[key] = call(name, "tn", a, b, stack=(W.n_layers[key], idx, gb[key]), out_dtype=BF16, **kw)
        pending.append((key, idx))

    gain_sums = (("tile", BF16), ("colsum", F32))
    for layer in reversed(range(DEPTH)):
        i = layer // 2
        st = saved[layer]
        du = host(f"ffn_out_bwd{layer}", "nt", dxb, W.get("w_ffn_out", layer), extras=(st["u"],),
                  epi=lambda acc, u: acc * (2.0 * jnp.maximum(u, 0).astype(F32)), out_dtype=BF16, tm=2048)
        dw(f"ffn_out_dw{layer}", "w_ffn_out", layer, st["u"], dxb, a_fn=_relu2, tk=2048)
        dx, dxb, dgain = host(f"ffn_in_bwd{layer}", "nt", du, W.get("w_ffn_in", layer), extras=(st["x_mid"], dx),
                         aux=(small["ffn_norm"][layer:layer + 1],), sides=gain_sums, epi=_epi_rms_bwd, **deep)
        gs["ffn_norm"][layer] = dgain[0:1]
        dw(f"ffn_in_dw{layer}", "w_ffn_in", layer, st["hn"], du, tk=2048)
        norm_bwd = dict(extras=(st["x_in"], dx), aux=(small["mix_norm"][layer:layer + 1],), sides=gain_sums,
                        epi=_epi_rms_bwd)
        if layer % 2 == 0:
            dcat = _mm(f"ab_out_bwd{layer}", "nt", dxb, W.get("ab_w_out", i), out_dtype=BF16)
            dw(f"ab_out_dw{layer}", "ab_w_out", i, st["cat"], dxb, tk=2048)
            gain = small["ab_gn_gain"][i:i + 1]
            dz, gs["ab_gn_gain"][i] = _ret_bwd(f"ret_bwd{layer}", st["z"], tb, gain, st["opre"], st["states"], dcat)
            dz, gs["ab_w_pool"][i], gs["ab_pool_scale"][i] = _pool_bwd(
                f"pool_bwd{layer}", st["pooled"], small["ab_w_pool"][i], small["ab_pool_scale"][i:i + 1], dcat, dz)
            if layer == 0:
                dw(f"ab_in_dw{layer}", "ab_w_in", i, st["h"], dz, call=host, tn=640, tk=2048)
            dx, dxb, dgain = host(f"ab_in_bwd{layer}", "nt", dz, W.get("ab_w_in", i), tm=512, tk=AB_IN_WIDTH,
                             take=len(pending) if layer == 0 else 1, **norm_bwd)
            if layer > 0:
                dw(f"ab_in_dw{layer}", "ab_w_in", i, st["h"], dz, call=host, tn=640, tk=2048)
        else:
            datt = _mm(f"c_out_bwd{layer}", "nt", dxb, W.get("c_w_out", i), out_dtype=BF16)
            dw(f"c_out_dw{layer}", "c_w_out", i, st["att"], dxb, tk=2048)
            dq, dk, dv, dbias = _attn_bwd(f"attn_bwd{layer}", st["qkv"], st["bias"], datt)
            gs["c_rel_bias"][i] = _bias_grad(f"bias_grad{layer}", dbias.reshape(ATT_HEADS, CHUNK, ATT_W), rel_bwd)
            dqkv = [dq, dk, dv]
            dx, dxb, dgain = host(f"qkv_bwd{layer}", "nt", dqkv, W.get("c_w_qkv", i), tm=512, tk=3 * D_MODEL, **norm_bwd)
            dw(f"qkv_dw{layer}", "c_w_qkv", i, st["h"], dqkv, call=host, tk=2048)
        gs["mix_norm"][layer] = dgain[0:1]
    for key, idx in pending:
        landed[key], = _run_rider(f"grad_exchange_{key}{idx}", _grad_rider(key, idx, gb[key], landed[key]))

    g_small = {
        "mix_norm": jnp.concatenate(gs["mix_norm"], axis=0),
        "ffn_norm": jnp.concatenate(gs["ffn_norm"], axis=0),
        "ab_gn_gain": jnp.concatenate(gs["ab_gn_gain"], axis=0),
        "ab_w_pool": jnp.stack(gs["ab_w_pool"], axis=0),
        "ab_pool_scale": jnp.concatenate(gs["ab_pool_scale"], axis=0),
        "c_rel_bias": jnp.stack(gs["c_rel_bias"], axis=0),
        "final_norm": d_final.reshape(D_MODEL),
    }
    return loss, dx, g_small, gb, landed


_BIG = ("w_ffn_in", "w_ffn_out", "ab_w_in", "ab_w_out", "c_w_qkv", "c_w_out")
_SHARD_AXIS = {"w_ffn_in": 2, "w_ffn_out": 1, "ab_w_in": 2, "ab_w_out": 1, "c_w_qkv": 2, "c_w_out": 1}
_SMALL = ("mix_norm", "ffn_norm", "ab_gn_gain", "ab_w_pool", "ab_pool_scale", "c_rel_bias", "final_norm")


def _place():
    x, y, c = lax.axis_index("x"), lax.axis_index("y"), lax.axis_index("c")
    chips = [(1 - x, y), (x, 1 - y), (1 - x, 1 - y)]
    return x, y, c, chips


def _sub(ref, axis, start, size):
    idx = [slice(None)] * len(ref.shape)
    idx[axis] = pl.ds(pl.multiple_of(start, LANE), size)
    return ref.at[tuple(idx)]


def _gather_rider(items, shards):
    keys = sorted({k for k, _ in items})
    n = len(items)
    axes = [_SHARD_AXIS[k] - 1 for k, _ in items]
    sizes = [shards[k].shape[a + 1] for (k, _), a in zip(items, axes)]
    hsizes = [shards[k].shape[2 - a] // 2 for (k, _), a in zip(items, axes)]

    def views(ins, outs, send_sems, recv_sems):
        x, y, c, chips = _place()
        srcs = [ins[keys.index(k)].at[l] for k, l in items]

        def remote(src, dst, s, to):
            return pltpu.make_async_remote_copy(src_ref=src, dst_ref=dst, send_sem=send_sems.at[s],
                                                recv_sem=recv_sems.at[s], device_id=to, device_id_type=MESH)

        def half(w, chip, core):
            return _sub(_sub(outs[w], axes[w], chip * sizes[w], sizes[w]), 1 - axes[w], core * hsizes[w], hsizes[w])

        me = 2 * x + y
        local = [pltpu.make_async_copy(srcs[w], _sub(outs[w], axes[w], me * sizes[w], sizes[w]), send_sems.at[6 * n + w])
                 for w in range(n)]
        first = [remote(_sub(srcs[w], 1 - axes[w], c * hsizes[w], hsizes[w]), half(w, me, c), w * 6 + k, (px, py, c))
                 for w in range(n) for k, (px, py) in enumerate(chips)]
        return x, y, c, chips, remote, half, local, first

    def start(ins, outs, send_sems, recv_sems):
        *_, local, first = views(ins, outs, send_sems, recv_sems)
        for cp in local + first:
            cp.start()

    def passes(x, y, c, chips, remote, half):
        return [remote(half(w, 2 * px + py, c), half(w, 2 * px + py, c), w * 6 + 3 + k, (x, y, 1 - c))
                for w in range(n) for k, (px, py) in enumerate(chips)]

    def relay(ins, outs, send_sems, recv_sems):
        x, y, c, chips, remote, half, _, _ = views(ins, outs, send_sems, recv_sems)
        for w in range(n):
            for k, (px, py) in enumerate(chips):
                landed = half(w, 2 * px + py, c)
                remote(landed, landed, w * 6 + k, (px, py, c)).wait_recv()
        for cp in passes(x, y, c, chips, remote, half):
            cp.start()

    def finish(ins, outs, send_sems, recv_sems):
        x, y, c, chips, remote, half, local, first = views(ins, outs, send_sems, recv_sems)
        for w in range(n):
            for k, (px, py) in enumerate(chips):
                theirs = half(w, 2 * px + py, 1 - c)
                remote(theirs, theirs, w * 6 + 3 + k, (x, y, 1 - c)).wait_recv()
        for cp in first + passes(x, y, c, chips, remote, half):
            cp.wait_send()
        for cp in local:
            cp.wait()

    def full(k, a):
        shape = list(shards[k].shape[1:])
        shape[a] *= N_CHIPS
        return jax.ShapeDtypeStruct(tuple(shape), shards[k].dtype)

    return _Rider(tuple(shards[k] for k in keys), tuple(full(k, a) for (k, _), a in zip(items, axes)), 7 * n, start, finish,
                  relay=relay)


def _mixer_items(layer):
    names = ("ab_w_in", "ab_w_out") if layer % 2 == 0 else ("c_w_qkv", "c_w_out")
    return [(k, layer // 2) for k in names]


class _Weights:
    def __init__(self, shards):
        self.shards = shards
        self.n_layers = {k: shards[k].shape[0] for k in _BIG}
        self.full = {}
        first, second = _mixer_items(0)
        self._take([first], _run_rider("gather_first", _gather_rider([first], shards)))
        self.plan = {"ab_in_fwd0": [second, ("w_ffn_in", 0)], "ab_out_fwd0": [("w_ffn_out", 0)]}
        for layer in range(1, DEPTH):
            proj = "ab_in" if layer % 2 == 0 else "qkv"
            self.plan[f"ffn_in_fwd{layer - 1}"] = _mixer_items(layer)
            self.plan[f"ffn_out_fwd{layer - 1}"] = [("w_ffn_in", layer)]
            self.plan[f"{proj}_fwd{layer}"] = [("w_ffn_out", layer)]

    def _take(self, items, outs):
        self.full.update(zip(items, outs))

    def get(self, name, layer):
        return self.full[(name, layer)]

    def mm(self, name, *args, **kw):
        items = self.plan.get(name)
        if items is None:
            return _mm(name, *args, **kw)
        res, outs = _mm(name, *args, rider=_gather_rider(items, self.shards), **kw)
        self._take(items, outs)
        return res


def _run_rider(name, rider):
    n_in, n_out = len(rider.operands), len(rider.out_shapes)

    def body(*refs):
        ins, outs, sems = refs[:n_in], refs[n_in:n_in + n_out], refs[n_in + n_out:]
        rider.start(ins, outs, *sems)
        if rider.relay is not None:
            rider.relay(ins, outs, *sems)
        rider.finish(ins, outs, *sems)

    return pl.pallas_call(
        body,
        name=name,
        in_specs=[_ANY] * n_in,
        out_specs=[_ANY] * n_out,
        out_shape=list(rider.out_shapes),
        input_output_aliases=dict(rider.aliases),
        scratch_shapes=[pltpu.SemaphoreType.DMA((rider.n_sems,)), pltpu.SemaphoreType.DMA((rider.n_sems,))],
        compiler_params=pltpu.CompilerParams(has_side_effects=True),
    )(*rider.operands)


def _grad_rider(name, layer, grad, landing):
    axis = _SHARD_AXIS[name] - 1
    L, R, C = grad.shape
    shard = (R // N_CHIPS, C) if axis == 0 else (R, C // N_CHIPS)
    size = shard[axis]

    def copies(ins, outs, send_sems, recv_sems):
        x, y, c, chips = _place()
        return [pltpu.make_async_remote_copy(
            src_ref=_sub(ins[0].at[layer], axis, (2 * px + py) * size, size), dst_ref=outs[0].at[layer, k],
            send_sem=send_sems.at[k], recv_sem=recv_sems.at[k], device_id=(px, py, c), device_id_type=MESH)
            for k, (px, py) in enumerate(chips)]

    def start(ins, outs, send_sems, recv_sems):
        for cp in copies(ins, outs, send_sems, recv_sems):
            cp.start()

    def finish(ins, outs, send_sems, recv_sems):
        cps = copies(ins, outs, send_sems, recv_sems)
        for cp in cps:
            cp.wait_recv()
        for cp in cps:
            cp.wait_send()

    out = jax.ShapeDtypeStruct((L, 3) + shard, grad.dtype)
    if landing is None:
        return _Rider((grad,), (out,), 3, start, finish)
    return _Rider((grad, landing), (out,), 3, start, finish, aliases=((1, 0),))


def _join_riders(riders):
    if len(riders) == 1:
        return riders[0]

    def parts(ins, outs, send_sems, recv_sems):
        i0 = o0 = s0 = 0
        for r in riders:
            ni, no = len(r.operands), len(r.out_shapes)
            yield (r, ins[i0:i0 + ni], outs[o0:o0 + no], send_sems.at[pl.ds(s0, r.n_sems)],
                   recv_sems.at[pl.ds(s0, r.n_sems)])
            i0, o0, s0 = i0 + ni, o0 + no, s0 + r.n_sems

    def start(*refs):
        for r, *own in parts(*refs):
            r.start(*own)

    def finish(*refs):
        for r, *own in parts(*refs):
            r.finish(*own)

    aliases, i0, o0 = [], 0, 0
    for r in riders:
        aliases += [(i0 + src, o0 + dst) for src, dst in r.aliases]
        i0, o0 = i0 + len(r.operands), o0 + len(r.out_shapes)
    return _Rider(tuple(x for r in riders for x in r.operands), tuple(x for r in riders for x in r.out_shapes),
                  sum(r.n_sems for r in riders), start, finish, tuple(aliases))


def _pair_swap(sums):
    n = len(sums)

    def body(*refs):
        ins, outs = refs[:n], refs[n:2 * n]
        send_sems, recv_sems = refs[2 * n:]
        x, y, c, _ = _place()
        cps = [pltpu.make_async_remote_copy(src_ref=ins[w], dst_ref=outs[w], send_sem=send_sems.at[w],
                                            recv_sem=recv_sems.at[w], device_id=(x, y, 1 - c), device_id_type=MESH)
               for w in range(n)]
        for cp in cps:
            cp.start()
        for cp in cps:
            cp.wait_recv()
        for cp in cps:
            cp.wait_send()

    return pl.pallas_call(
        body,
        name="pair_swap",
        in_specs=[_ANY] * n,
        out_specs=[_ANY] * n,
        out_shape=[jax.ShapeDtypeStruct(s.shape, s.dtype) for s in sums],
        scratch_shapes=[pltpu.SemaphoreType.DMA((n,)), pltpu.SemaphoreType.DMA((n,))],
        compiler_params=pltpu.CompilerParams(has_side_effects=True),
    )(*sums)


def _rows_tile(rows, cols):
    tr = rows
    while tr * cols > (1 << 19) and tr % 16 == 0:
        tr //= 2
    return tr


def _chip_sum(name, grad, landed, chip, saxis):
    L = grad.shape[0]
    _, _, R, C = landed.shape
    tr = _rows_tile(R, C)
    nr = R // tr
    if saxis == 2:
        g_idx = lambda l, i, s: (l, i, s[0])
    else:
        g_idx = lambda l, i, s: (l, s[0] * nr + i, 0)

    def body(s_ref, g_ref, l_ref, o_ref):
        tot = ((g_ref[...].astype(F32) + l_ref[0].astype(F32)) + l_ref[1].astype(F32)) + l_ref[2].astype(F32)
        o_ref[...] = tot.astype(o_ref.dtype)

    return pl.pallas_call(
        body,
        name=name,
        grid_spec=pltpu.PrefetchScalarGridSpec(
            num_scalar_prefetch=1,
            grid=(L, nr),
            in_specs=[pl.BlockSpec((None, tr, C), g_idx), pl.BlockSpec((None, 3, tr, C), lambda l, i, s: (l, 0, i, 0))],
            out_specs=pl.BlockSpec((None, tr, C), lambda l, i, s: (l, i, 0)),
        ),
        out_shape=jax.ShapeDtypeStruct((L, R, C), BF16),
        compiler_params=_cp(("parallel", "parallel")),
    )(chip, grad, landed)


def _all_reduce_small(packed):
    R = packed.shape[0]

    def body(p_ref, o_ref, land_ref, send_sems, recv_sems):
        x, y, c, _ = _place()
        me = 4 * x + 2 * y + c
        sends, recvs = [], []
        for r in range(1, N_DEV):
            px, py, pc = x ^ (r >> 2), y ^ ((r >> 1) & 1), c ^ (r & 1)
            cp = pltpu.make_async_remote_copy(src_ref=p_ref, dst_ref=land_ref.at[me], send_sem=send_sems.at[r - 1],
                                              recv_sem=recv_sems.at[r - 1], device_id=(px, py, pc), device_id_type=MESH)
            cp.start()
            sends.append(cp)
            recvs.append(pltpu.make_async_remote_copy(src_ref=p_ref, dst_ref=land_ref.at[4 * px + 2 * py + pc],
                                                      send_sem=send_sems.at[r - 1], recv_sem=recv_sems.at[r - 1],
                                                      device_id=(px, py, pc), device_id_type=MESH))
        land_ref[me] = p_ref[...]
        for cp in recvs:
            cp.wait_recv()
        for cp in sends:
            cp.wait_send()
        acc = land_ref[0]
        for d in range(1, N_DEV):
            acc = acc + land_ref[d]
        o_ref[...] = acc

    vm = pl.BlockSpec(memory_space=pltpu.VMEM)
    return pl.pallas_call(
        body,
        name="all_reduce_small",
        in_specs=[vm],
        out_specs=vm,
        out_shape=jax.ShapeDtypeStruct((R, LANE), F32),
        scratch_shapes=[pltpu.VMEM((N_DEV, R, LANE), F32), pltpu.SemaphoreType.DMA((N_DEV - 1,)),
                        pltpu.SemaphoreType.DMA((N_DEV - 1,))],
        compiler_params=pltpu.CompilerParams(has_side_effects=True, vmem_limit_bytes=VMEM_LIMIT),
    )(packed)


def _adamw(name, w, m, v, grads):
    R, C = w.shape
    tr = _rows_tile(R, C)
    c1 = 1.0 - ADAM_B1 ** ADAM_STEP
    c2 = 1.0 - ADAM_B2 ** ADAM_STEP
    ng = len(grads)

    def body(*refs):
        w_ref, m_ref, v_ref = refs[:3]
        g_refs = refs[3:3 + ng]
        g_ref, d_ref, nm_ref, nv_ref = refs[3 + ng:]
        gv = g_refs[0][...].astype(F32)
        for r in g_refs[1:]:
            gv = gv + r[...].astype(F32)
        g_ref[...] = gv
        nm = ADAM_B1 * m_ref[...] + (1.0 - ADAM_B1) * gv
        nv = ADAM_B2 * v_ref[...] + (1.0 - ADAM_B2) * (gv * gv)
        nm_ref[...] = nm
        nv_ref[...] = nv
        d_ref[...] = -ADAM_LR * ((nm / c1) / (jnp.sqrt(nv / c2) + ADAM_EPS) + ADAM_WD * w_ref[...])

    blk = pl.BlockSpec((tr, C), lambda i: (i, 0))
    out = jax.ShapeDtypeStruct((R, C), F32)
    return pl.pallas_call(
        body,
        name=name,
        grid=(R // tr,),
        in_specs=[blk] * (3 + ng),
        out_specs=[blk] * 4,
        out_shape=[out] * 4,
        compiler_params=_cp(("parallel",)),
    )(w, m, v, *grads)


def _pack(parts):
    rows = []
    for p in parts:
        flat = p.reshape(-1).astype(F32)
        n = -(-flat.shape[0] // (8 * LANE)) * (8 * LANE)
        rows.append(jnp.pad(flat, (0, n - flat.shape[0])).reshape(n // LANE, LANE))
    return jnp.concatenate(rows, axis=0)


def _unpack(packed, like):
    out, r = [], 0
    for p in like:
        size = int(np.prod(p.shape))
        n = -(-size // (8 * LANE)) * 8
        out.append(packed[r:r + n].reshape(-1)[:size].reshape(p.shape))
        r += n
    return out


def kernel(x, mix_norm, ffn_norm, w_ffn_in, w_ffn_out, ab_w_in, ab_gn_gain, ab_w_pool, ab_pool_scale, ab_w_out, c_w_qkv, c_rel_bias, c_w_out, final_norm, loss_target, m_mix_norm, m_ffn_norm, m_w_ffn_in, m_w_ffn_out, m_ab_w_in, m_ab_gn_gain, m_ab_w_pool, m_ab_pool_scale, m_ab_w_out, m_c_w_qkv, m_c_rel_bias, m_c_w_out, m_final_norm, v_mix_norm, v_ffn_norm, v_w_ffn_in, v_w_ffn_out, v_ab_w_in, v_ab_gn_gain, v_ab_w_pool, v_ab_pool_scale, v_ab_w_out, v_c_w_qkv, v_c_rel_bias, v_c_w_out, v_final_norm):
    w = dict(mix_norm=mix_norm, ffn_norm=ffn_norm, w_ffn_in=w_ffn_in, w_ffn_out=w_ffn_out, ab_w_in=ab_w_in,
             ab_gn_gain=ab_gn_gain, ab_w_pool=ab_w_pool, ab_pool_scale=ab_pool_scale, ab_w_out=ab_w_out,
             c_w_qkv=c_w_qkv, c_rel_bias=c_rel_bias, c_w_out=c_w_out, final_norm=final_norm)
    m = dict(mix_norm=m_mix_norm, ffn_norm=m_ffn_norm, w_ffn_in=m_w_ffn_in, w_ffn_out=m_w_ffn_out, ab_w_in=m_ab_w_in,
             ab_gn_gain=m_ab_gn_gain, ab_w_pool=m_ab_w_pool, ab_pool_scale=m_ab_pool_scale, ab_w_out=m_ab_w_out,
             c_w_qkv=m_c_w_qkv, c_rel_bias=m_c_rel_bias, c_w_out=m_c_w_out, final_norm=m_final_norm)
    v = dict(mix_norm=v_mix_norm, ffn_norm=v_ffn_norm, w_ffn_in=v_w_ffn_in, w_ffn_out=v_w_ffn_out, ab_w_in=v_ab_w_in,
             ab_gn_gain=v_ab_gn_gain, ab_w_pool=v_ab_w_pool, ab_pool_scale=v_ab_pool_scale, ab_w_out=v_ab_w_out,
             c_w_qkv=v_c_w_qkv, c_rel_bias=v_c_rel_bias, c_w_out=v_c_w_out, final_norm=v_final_norm)
    S = x.shape[1]
    cx, cy, cc = lax.axis_index("x"), lax.axis_index("y"), lax.axis_index("c")
    chip = jnp.reshape(2 * cx + cy, (1,)).astype(jnp.int32)

    big = _Weights({k: w[k].astype(BF16) for k in _BIG})
    small = {k: w[k] for k in _SMALL}
    loss, grad_x, g_small, g_big, landed = _local_step(x.reshape(S, D_MODEL), loss_target.reshape(S, D_MODEL), small, big)

    sums = [_chip_sum(f"chip_sum_{k}", g_big[k], landed[k], chip, _SHARD_AXIS[k]) for k in _BIG]
    siblings = _pair_swap(sums)

    packed = _all_reduce_small(_pack([g_small[k] for k in _SMALL] + [loss]))
    small_like = [w[k] for k in _SMALL]
    g_red = dict(zip(_SMALL, _unpack(packed, small_like)))
    loss_row = packed.shape[0] - 8
    loss_out = packed[loss_row, 0]

    grad, delta, new_m, new_v = {}, {}, {}, {}
    for k, mine, theirs in zip(_BIG, sums, siblings):
        shp = w[k].shape
        two = (shp[0] * shp[1], shp[2])
        outs = _adamw(f"adamw_{k}", w[k].reshape(two), m[k].reshape(two), v[k].reshape(two),
                      (mine.reshape(two), theirs.reshape(two)))
        grad[k], delta[k], new_m[k], new_v[k] = [o.reshape(shp) for o in outs]
    _, d, nm, nv = _adamw("adamw_small", _pack(small_like), _pack([m[k] for k in _SMALL]), _pack([v[k] for k in _SMALL]),
                          (packed[:loss_row],))
    for k, dk, mk, vk in zip(_SMALL, _unpack(d, small_like), _unpack(nm, small_like), _unpack(nv, small_like)):
        grad[k], delta[k], new_m[k], new_v[k] = g_red[k], dk, mk, vk

    order = ("mix_norm", "ffn_norm", "w_ffn_in", "w_ffn_out", "ab_w_in", "ab_gn_gain", "ab_w_pool", "ab_pool_scale",
             "ab_w_out", "c_w_qkv", "c_rel_bias", "c_w_out", "final_norm")
    return (loss_out, grad_x.reshape(x.shape), *[grad[k] for k in order], *[delta[k] for k in order],
            *[new_m[k] for k in order], *[new_v[k] for k in order])
```

```python
import functools
from typing import Callable, NamedTuple

import numpy as np
import jax
import jax.numpy as jnp
from jax import lax
from jax.experimental import pallas as pl
from jax.experimental.pallas import tpu as pltpu

F32 = jnp.float32
BF16 = jnp.bfloat16

D_MODEL = 1024
D_FF = 4096
DEPTH = 4
CHUNK = 64
RMS_EPS = 1e-6
RET_WIDTH = 512
RET_HEADS = 4
RET_HEAD_DIM = 128
RET_ROPE_BASE = 10000.0
GN_EPS = 1e-5
POOL_WIDTH = 512
POOL_WINDOWS = (2, 4, 8, 16)
POOL_GROUP_DIM = 128
POOL_HALO = 16
AB_IN_WIDTH = 2560
ATT_HEADS = 16
ATT_HEAD_DIM = 64
LEFT_CHUNKS = 8
BAND = (LEFT_CHUNKS + 1) * CHUNK
REL_CLIP = 128
N_REL = 2 * REL_CLIP + 1
N_REL_PAD = 264
NEG_INF = -1e30
KSCALE = RET_HEAD_DIM ** -0.5
QSCALE = ATT_HEAD_DIM ** -0.5

ADAM_LR = 0.001
ADAM_B1 = 0.9
ADAM_B2 = 0.999
ADAM_EPS = 1e-08
ADAM_WD = 0.01
ADAM_STEP = 10

ATT_BLOCK = LEFT_CHUNKS * CHUNK
RET_BLOCK = 512
N_CHIPS = 4
N_DEV = 8
LANE = 128
VMEM_LIMIT = 52 * 1024 * 1024
EPI_ROWS = 256
MESH = pl.DeviceIdType.MESH


def _cp(sem, vmem=VMEM_LIMIT):
    return pltpu.CompilerParams(dimension_semantics=sem, vmem_limit_bytes=vmem)


def _dot(a, b):
    return lax.dot_general(a, b, (((1,), (0,)), ((), ())), preferred_element_type=F32)


def _dot_nt(a, b):
    return lax.dot_general(a, b, (((1,), (1,)), ((), ())), preferred_element_type=F32)


def _dot_tn(a, b):
    return lax.dot_general(a, b, (((0,), (0,)), ((), ())), preferred_element_type=F32)


_ANY = pl.BlockSpec(memory_space=pl.ANY)


class _Rider(NamedTuple):
    operands: tuple
    out_shapes: tuple
    n_sems: int
    start: Callable
    finish: Callable
    aliases: tuple = ()
    relay: Callable = None


def _mm(name, mode, a, b, *, la=None, lb=None, tm=1024, tn=1024, tk=1024, a_fn=None, b_fn=None,
        extras=(), aux=(), sides=(), epi=None, out_dtype=F32, stack=None, rider=None):
    a_parts = list(a) if isinstance(a, (list, tuple)) else [a]
    b_parts = list(b) if isinstance(b, (list, tuple)) else [b]
    na, nbp = len(a_parts), len(b_parts)
    a2, b2 = list(a_parts[0].shape[-2:]), list(b_parts[0].shape[-2:])
    a2[1] *= na
    b2[1] *= nbp
    if mode == "nn":
        (M, K), (K2, N) = a2, b2
    elif mode == "nt":
        (M, K), (N, K2) = a2, b2
    else:
        (K, M), (K2, N) = a2, b2
    assert K == K2, (name, a2, b2)
    tm, tn, tk = min(tm, M), min(tn, N), min(tk, K)
    assert M % tm == 0 and N % tn == 0 and K % tk == 0, (name, M, N, K, tm, tn, tk)
    gm, gn, gk = M // tm, N // tn, K // tk
    fold = mode == "nt" and na > 1 and gk == 1

    def specs(parts, block, idx, lead):
        per = parts[0].shape[-1] // block[1]
        assert parts[0].shape[-1] % block[1] == 0, (name, parts[0].shape, block)
        out = []
        for p in range(len(parts)):
            def f(i, j, k, p=p):
                r, c = idx(i, j, k)
                if len(parts) > 1:
                    c = jnp.clip(c - p * per, 0, per - 1)
                return (r, c) if lead is None else (lead, r, c)
            out.append(pl.BlockSpec(block if lead is None else (None,) + block, f))
        return out, per

    if mode == "nn":
        a_specs, a_per = specs(a_parts, (tm, tk), lambda i, j, k: (i, k), la)
        b_specs, b_per = specs(b_parts, (tk, tn), lambda i, j, k: (k, j), lb)
        a_axis, b_axis, dot = 2, 1, _dot
    elif mode == "nt":
        if fold:
            a_specs, a_per = [pl.BlockSpec((tm, K // na), lambda i, j, k: (i, 0)) for _ in a_parts], 1
        else:
            a_specs, a_per = specs(a_parts, (tm, tk), lambda i, j, k: (i, k), la)
        b_specs, b_per = specs(b_parts, (tn, tk), lambda i, j, k: (j, k), lb)
        a_axis, b_axis, dot = 2, 2, _dot_nt
    else:
        a_specs, a_per = specs(a_parts, (tk, tm), lambda i, j, k: (k, i), la)
        b_specs, b_per = specs(b_parts, (tk, tn), lambda i, j, k: (k, j), lb)
        a_axis, b_axis, dot = 0, 1, _dot_tn
    ex_specs = [pl.BlockSpec((tm, tn), lambda i, j, k: (i, j)) for _ in extras]
    n_ex = len(extras)

    n_aux, n_side = len(aux), len(sides)
    operands = a_parts + b_parts + list(extras) + list(aux)
    in_specs = a_specs + b_specs + ex_specs + [pl.BlockSpec(v.shape, lambda i, j, k, nd=v.ndim: (0,) * nd) for v in aux]
    aliases = {}
    if stack is None:
        out_specs = [pl.BlockSpec((tm, tn), lambda i, j, k: (i, j))]
        out_shapes = [jax.ShapeDtypeStruct((M, N), out_dtype)]
    else:
        n_layers, layer, prev = stack
        out_specs = [pl.BlockSpec((None, tm, tn), lambda i, j, k: (layer, i, j))]
        out_shapes = [jax.ShapeDtypeStruct((n_layers, M, N), out_dtype)]
        if prev is not None:
            aliases = {len(operands): 0}
            operands.append(prev)
            in_specs.append(_ANY)
    for kind, dtype in sides:
        if kind == "tile":
            out_specs.append(pl.BlockSpec((tm, tn), lambda i, j, k: (i, j)))
            out_shapes.append(jax.ShapeDtypeStruct((M, N), dtype))
        else:
            assert gn == 1, name
            out_specs.append(pl.BlockSpec((8, tn), lambda i, j, k: (0, 0)))
            out_shapes.append(jax.ShapeDtypeStruct((8, N), dtype))
    n_prev = len(aliases)
    scratch = [pltpu.VMEM((tm, tn), F32)] if gk > 1 else []
    n_rin = n_rout = 0
    if rider is not None:
        n_rin, n_rout = len(rider.operands), len(rider.out_shapes)
        for src, dst in rider.aliases:
            aliases[len(operands) + src] = 1 + n_side + dst
        operands += list(rider.operands)
        in_specs += [_ANY] * n_rin
        out_specs += [_ANY] * n_rout
        out_shapes += list(rider.out_shapes)
        scratch += [pltpu.SemaphoreType.DMA((rider.n_sems,)), pltpu.SemaphoreType.DMA((rider.n_sems,))]
    assert na == 1 or nbp == 1, name

    def body(*refs):
        a_refs, b_refs = refs[:na], refs[na:na + nbp]
        ex_refs = refs[na + nbp:na + nbp + n_ex + n_aux]
        n_in = na + nbp + n_ex + n_aux + n_prev
        rin = refs[n_in:n_in + n_rin]
        o_ref = refs[n_in + n_rin]
        side_refs = refs[n_in + n_rin + 1:n_in + n_rin + 1 + n_side]
        rout = refs[n_in + n_rin + 1 + n_side:n_in + n_rin + 1 + n_side + n_rout]
        rest = refs[n_in + n_rin + 1 + n_side + n_rout:]
        i, j, k = pl.program_id(0), pl.program_id(1), pl.program_id(2)
        if rider is not None:
            sems = rest[-2:]

            @pl.when(jnp.logical_and(i == 0, jnp.logical_and(j == 0, k == 0)))
            def _():
                rider.start(rin, rout, *sems)

        def finish(acc):
            if epi is None:
                o_ref[...] = acc[...].astype(o_ref.dtype)
                return
            strip = min(tm, EPI_ROWS)
            colsums = [None] * n_side
            for r0 in range(0, tm, strip):
                rows = slice(r0, r0 + strip)
                res = epi(acc[rows, :], *[r[rows, :] for r in ex_refs[:n_ex]], *[r[...] for r in ex_refs[n_ex:]])
                if n_side:
                    res, *side_vals = res
                    for s, ((kind, _), ref, val) in enumerate(zip(sides, side_refs, side_vals)):
                        if kind == "tile":
                            ref[rows, :] = val.astype(ref.dtype)
                        else:
                            colsums[s] = val if colsums[s] is None else colsums[s] + val
                o_ref[rows, :] = res.astype(o_ref.dtype)
            for (kind, _), ref, val in zip(sides, side_refs, colsums):
                if kind == "colsum":
                    @pl.when(i == 0)
                    def _(ref=ref, val=val):
                        ref[...] = val

                    @pl.when(i > 0)
                    def _(ref=ref, val=val):
                        ref[...] += val

                    @pl.when(i == gm - 1)
                    def _(ref=ref):
                        ref[0:1, :] = jnp.sum(ref[...], axis=0, keepdims=True)

        def step(a_ref, b_ref):
            av, bv = a_ref[...], b_ref[...]
            if a_fn is not None:
                av = a_fn(av)
            if b_fn is not None:
                bv = b_fn(bv)
            part = dot(av.astype(BF16), bv.astype(BF16))
            if gk == 1:
                finish(part)
                return
            acc_ref = rest[0]

            @pl.when(k == 0)
            def _():
                acc_ref[...] = part

            @pl.when(k > 0)
            def _():
                acc_ref[...] += part

        if fold:
            kp = K // na
            finish(sum(dot(a_refs[p][...].astype(BF16), b_refs[0][:, p * kp:(p + 1) * kp].astype(BF16))
                       for p in range(na)))
        elif na > 1:
            sel = pl.program_id(a_axis) // a_per
            for p in range(na):
                pl.when(sel == p)(functools.partial(step, a_refs[p], b_refs[0]))
        elif nbp > 1:
            sel = pl.program_id(b_axis) // b_per
            for p in range(nbp):
                pl.when(sel == p)(functools.partial(step, a_refs[0], b_refs[p]))
        else:
            step(a_refs[0], b_refs[0])
        if gk > 1:
            @pl.when(k == gk - 1)
            def _():
                finish(rest[0])

        if rider is not None:
            steps = gm * gn * gk
            step_no = (i * gn + j) * gk + k
            if rider.relay is not None:
                assert steps >= 3, name

                @pl.when(step_no == steps - 2)
                def _():
                    rider.relay(rin, rout, *sems)

            @pl.when(step_no == steps - 1)
            def _():
                rider.finish(rin, rout, *sems)

    sequential = rider is not None or any(kind == "colsum" for kind, _ in sides)
    sem = ("arbitrary",) * 3 if sequential else ("parallel", "parallel", "arbitrary")
    outs = pl.pallas_call(
        body,
        name=name,
        grid=(gm, gn, gk),
        in_specs=in_specs,
        out_specs=out_specs,
        out_shape=out_shapes,
        input_output_aliases=aliases,
        scratch_shapes=scratch,
        compiler_params=_cp(sem),
    )(*operands)
    res = outs[0] if not sides else tuple(outs[:1 + n_side])
    return res if rider is None else (res, list(outs[1 + n_side:]))


def _relu2(u):
    r = jnp.maximum(u, 0)
    return r * r


def _epi_residual(acc, res):
    return acc + res


def _epi_residual_norm(acc, res, g):
    xn = acc + res
    r = lax.rsqrt(jnp.mean(xn * xn, axis=-1, keepdims=True) + RMS_EPS)
    return xn, (xn * r) * g


def _epi_rms_bwd(dh, x, dres, g):
    r = lax.rsqrt(jnp.mean(x * x, axis=-1, keepdims=True) + RMS_EPS)
    xh = x * r
    dxh = dh * g
    dx = dres + r * (dxh - xh * jnp.mean(dxh * xh, axis=-1, keepdims=True))
    return dx, dx, jnp.sum((dh * xh).reshape(dh.shape[0] // 8, 8, dh.shape[1]), axis=0)


def _rms_fwd(name, x, g):
    S, D = x.shape
    tq = min(1024, S)

    def body(x_ref, g_ref, o_ref):
        xv = x_ref[...]
        r = lax.rsqrt(jnp.mean(xv * xv, axis=-1, keepdims=True) + RMS_EPS)
        o_ref[...] = ((xv * r) * g_ref[...]).astype(o_ref.dtype)

    return pl.pallas_call(
        body,
        name=name,
        grid=(S // tq,),
        in_specs=[pl.BlockSpec((tq, D), lambda i: (i, 0)), pl.BlockSpec((1, D), lambda i: (0, 0))],
        out_specs=pl.BlockSpec((tq, D), lambda i: (i, 0)),
        out_shape=jax.ShapeDtypeStruct((S, D), BF16),
        compiler_params=_cp(("parallel",)),
    )(x, g)


def _loss_head(x, g, t):
    S, D = x.shape
    tq = min(512, S)
    n = S // tq

    def body(x_ref, g_ref, t_ref, loss_ref, dx_ref, dxb_ref, dg_ref, lacc_ref, gacc_ref):
        i = pl.program_id(0)
        xv = x_ref[...]
        gv = g_ref[...]
        r = lax.rsqrt(jnp.mean(xv * xv, axis=-1, keepdims=True) + RMS_EPS)
        xh = xv * r
        e = xh * gv - t_ref[...]
        dy = e * (1.0 / D)
        dxh = dy * gv
        dx = r * (dxh - xh * jnp.mean(dxh * xh, axis=-1, keepdims=True))
        dx_ref[...] = dx
        dxb_ref[...] = dx.astype(dxb_ref.dtype)
        lpart = jnp.sum((e * e).reshape(tq // 8, 8, D), axis=0)
        gpart = jnp.sum((dy * xh).reshape(tq // 8, 8, D), axis=0)

        @pl.when(i == 0)
        def _():
            lacc_ref[...] = lpart
            gacc_ref[...] = gpart

        @pl.when(i > 0)
        def _():
            lacc_ref[...] += lpart
            gacc_ref[...] += gpart

        @pl.when(i == n - 1)
        def _():
            dg_ref[...] = jnp.sum(gacc_ref[...], axis=0, keepdims=True)
            tot = jnp.sum(jnp.sum(lacc_ref[...], axis=0, keepdims=True), axis=1, keepdims=True)
            loss_ref[...] = jnp.broadcast_to(tot * (0.5 / D), (1, LANE))

    return pl.pallas_call(
        body,
        name="loss_head",
        grid=(n,),
        in_specs=[pl.BlockSpec((tq, D), lambda i: (i, 0)), pl.BlockSpec((1, D), lambda i: (0, 0)),
                  pl.BlockSpec((tq, D), lambda i: (i, 0))],
        out_specs=[pl.BlockSpec((1, LANE), lambda i: (0, 0)), pl.BlockSpec((tq, D), lambda i: (i, 0)),
                   pl.BlockSpec((tq, D), lambda i: (i, 0)), pl.BlockSpec((1, D), lambda i: (0, 0))],
        out_shape=[jax.ShapeDtypeStruct((1, LANE), F32), jax.ShapeDtypeStruct((S, D), F32),
                   jax.ShapeDtypeStruct((S, D), BF16), jax.ShapeDtypeStruct((1, D), F32)],
        scratch_shapes=[pltpu.VMEM((8, D), F32), pltpu.VMEM((8, D), F32)],
        compiler_params=_cp(("arbitrary",)),
    )(x, g, t)


def _ret_tables(S):
    T = min(RET_BLOCK, S)
    inv_freq = 1.0 / (RET_ROPE_BASE ** jnp.linspace(0.0, 1.0, RET_HEAD_DIM // 2, dtype=F32))
    ang = jnp.arange(S, dtype=F32)[:, None] * inv_freq[None, :]
    cos, sin = jnp.cos(ang), jnp.sin(ang)
    cosf = jnp.repeat(cos, 2, axis=-1)
    sins = jnp.stack([-sin, sin], axis=-1).reshape(S, RET_HEAD_DIM)
    log_g = np.log1p(-np.power(2.0, -5.0 - np.arange(RET_HEADS, dtype=np.float64)))
    pos = np.arange(T, dtype=np.float64)
    diff = pos[:, None] - pos[None, :]
    same = (pos[:, None] // CHUNK) == (pos[None, :] // CHUNK)
    seen = same | (diff > 0)
    dmat = np.where(seen[None], np.exp(np.abs(diff)[None] * log_g[:, None, None]), 0.0)
    aq = np.exp((pos[None, :] + 1.0) * log_g[:, None])
    ak = np.exp((T - 1.0 - pos[None, :]) * log_g[:, None])
    lam = np.exp(T * log_g)
    bc = lambda v: jnp.asarray(np.broadcast_to(v[..., None], v.shape + (LANE,)), F32)
    return dict(cos=cosf, sin=sins, dmat=jnp.asarray(dmat, F32), aq=bc(aq), ak=bc(ak),
                lam=jnp.asarray(np.broadcast_to(lam[:, None, None], (RET_HEADS, 1, LANE)), F32))


def _rot(x, cos, sin_s, even):
    sw = jnp.where(even, pltpu.roll(x, LANE - 1, 1), pltpu.roll(x, 1, 1))
    return x * cos + sw * sin_s


def _rot_t(dy, cos, sin_s, even):
    t = dy * sin_s
    return dy * cos + jnp.where(even, pltpu.roll(t, LANE - 1, 1), pltpu.roll(t, 1, 1))


def _ret_specs(T, rev_nb=None):
    blk = (lambda b: b) if rev_nb is None else (lambda b: rev_nb - 1 - b)
    whole = lambda shape: pl.BlockSpec(shape, lambda b: (0,) * len(shape))
    specs = [pl.BlockSpec((T, AB_IN_WIDTH), lambda b: (blk(b), 0)),
             pl.BlockSpec((T, LANE), lambda b: (blk(b), 0)),
             pl.BlockSpec((T, LANE), lambda b: (blk(b), 0)),
             whole((RET_HEADS, T, T)), whole((RET_HEADS, T, LANE)), whole((RET_HEADS, T, LANE)),
             whole((RET_HEADS, 1, LANE)), whole((1, RET_WIDTH))]
    return specs, blk


def _head_views(h, z_ref, tabs, token_refs, head_refs):
    zs = [z_ref.at[:, (o * RET_HEADS + h) * LANE:(o * RET_HEADS + h + 1) * LANE] for o in range(4)]
    hs = slice(h * LANE, (h + 1) * LANE)
    return zs, [t.at[h] for t in tabs], [r.at[:, hs] for r in token_refs], [r.at[h] for r in head_refs]


def _ret_fwd(name, z, tb, gain):
    S = z.shape[0]
    T = min(RET_BLOCK, S)
    nb = S // T
    specs, blk = _ret_specs(T)

    def body(z_ref, cos_r, sin_r, d_all, aq_all, ak_all, lam_all, gain_all, cat_all, opre_all, st_all, state_all):
        @pl.when(pl.program_id(0) == 0)
        def _():
            state_all[...] = jnp.zeros_like(state_all)

        for h in range(RET_HEADS):
            zs, tabs, toks, heads = _head_views(h, z_ref, (d_all, aq_all, ak_all, lam_all),
                                                (gain_all, cat_all, opre_all), (st_all, state_all))
            head(*zs, cos_r, sin_r, *tabs, *toks, *heads)

    def head(zq, zk, zv, zg, cos_r, sin_r, d_r, aq_r, ak_r, lam_r, gain_r, ret_o, opre_o, st_o, state):
        even = (lax.broadcasted_iota(jnp.int32, (T, LANE), 1) & 1) == 0
        c, s = cos_r[...], sin_r[...]
        q = _rot(zq[...].astype(F32), c, s, even)
        k = _rot(zk[...].astype(F32), c, s, even) * KSCALE
        qb, kb, vb = q.astype(BF16), k.astype(BF16), zv[...].astype(BF16)
        p = (_dot_nt(qb, kb) * d_r[...]).astype(BF16)
        st = state[...]
        st_o[...] = st
        o = _dot(p, vb) + _dot((q * aq_r[...]).astype(BF16), st.astype(BF16))
        state[...] = st * lam_r[...] + _dot_tn((k * ak_r[...]).astype(BF16), vb)
        opre_o[...] = o
        mu = jnp.mean(o, axis=-1, keepdims=True)
        d = o - mu
        y = d * lax.rsqrt(jnp.mean(d * d, axis=-1, keepdims=True) + GN_EPS)
        g = zg[...].astype(F32)
        ret_o[...] = ((g * jax.nn.sigmoid(g)) * (y * gain_r[...])).astype(ret_o.dtype)

    out_blk = pl.BlockSpec((T, RET_WIDTH), lambda b: (b, 0))
    return pl.pallas_call(
        body,
        name=name,
        grid=(nb,),
        in_specs=specs,
        out_specs=[out_blk, out_blk, pl.BlockSpec((RET_HEADS, None, LANE, LANE), lambda b: (0, b, 0, 0))],
        out_shape=[jax.ShapeDtypeStruct((S, D_MODEL), BF16), jax.ShapeDtypeStruct((S, RET_WIDTH), F32),
                   jax.ShapeDtypeStruct((RET_HEADS, nb, LANE, LANE), F32)],
        scratch_shapes=[pltpu.VMEM((RET_HEADS, LANE, LANE), F32)],
        compiler_params=_cp(("arbitrary",)),
    )(z, tb["cos"], tb["sin"], tb["dmat"], tb["aq"], tb["ak"], tb["lam"], gain)


def _ret_bwd(name, z, tb, gain, opre, states, dcat):
    S = z.shape[0]
    T = min(RET_BLOCK, S)
    nb = S // T
    specs, blk = _ret_specs(T, rev_nb=nb)
    tok = pl.BlockSpec((T, RET_WIDTH), lambda b: (blk(b), 0))

    def body(z_ref, cos_r, sin_r, d_all, aq_all, ak_all, lam_all, gain_all, opre_all, st_all, dret_all,
             dz_ref, dgain_all, dstate_all):
        @pl.when(pl.program_id(0) == 0)
        def _():
            dstate_all[...] = jnp.zeros_like(dstate_all)
            dgain_all[...] = jnp.zeros_like(dgain_all)

        for h in range(RET_HEADS):
            zs, tabs, toks, heads = _head_views(h, z_ref, (d_all, aq_all, ak_all, lam_all),
                                                (gain_all, opre_all, dret_all, dgain_all), (st_all, dstate_all))
            dzs, _, _, _ = _head_views(h, dz_ref, (), (), ())
            gain_r, opre_r, dret_r, dgain_o = toks
            head(*zs, cos_r, sin_r, *tabs, gain_r, opre_r, heads[0], dret_r, *dzs, dgain_o, heads[1])

    def head(zq, zk, zv, zg, cos_r, sin_r, d_r, aq_r, ak_r, lam_r, gain_r, opre_r, st_r, dret_r,
             dq_o, dk_o, dv_o, dg_o, dgain_o, dstate):
        even = (lax.broadcasted_iota(jnp.int32, (T, LANE), 1) & 1) == 0
        c, s = cos_r[...], sin_r[...]
        aq, ak, dm = aq_r[...], ak_r[...], d_r[...]
        q = _rot(zq[...].astype(F32), c, s, even)
        k = _rot(zk[...].astype(F32), c, s, even) * KSCALE
        qb, kb, vb = q.astype(BF16), k.astype(BF16), zv[...].astype(BF16)
        pb = (_dot_nt(qb, kb) * dm).astype(BF16)
        g = zg[...].astype(F32)
        sig = jax.nn.sigmoid(g)
        o = opre_r[...]
        mu = jnp.mean(o, axis=-1, keepdims=True)
        d = o - mu
        rstd = lax.rsqrt(jnp.mean(d * d, axis=-1, keepdims=True) + GN_EPS)
        y = d * rstd
        gain_v = gain_r[...]
        dret = dret_r[...].astype(F32)
        dyg = dret * (g * sig)
        dg_o[...] = (dret * (y * gain_v) * (sig * (1.0 + g * (1.0 - sig)))).astype(dg_o.dtype)
        dgain_o[...] += jnp.sum(dyg * y, axis=0, keepdims=True)
        dy = dyg * gain_v
        do = rstd * (dy - jnp.mean(dy, axis=-1, keepdims=True) - y * jnp.mean(dy * y, axis=-1, keepdims=True))
        dob = do.astype(BF16)
        stb = st_r[...].astype(BF16)
        dsn = dstate[...]
        dsnb = dsn.astype(BF16)
        dpb = (_dot_nt(dob, vb) * dm).astype(BF16)
        dq = _dot(dpb, kb) + _dot_nt(dob, stb) * aq
        dk = _dot_tn(dpb, qb) + _dot_nt(vb, dsnb) * ak
        dv = _dot_tn(pb, dob) + _dot((k * ak).astype(BF16), dsnb)
        dstate[...] = dsn * lam_r[...] + _dot_tn((q * aq).astype(BF16), dob)
        dq_o[...] = _rot_t(dq, c, s, even).astype(dq_o.dtype)
        dk_o[...] = _rot_t(dk * KSCALE, c, s, even).astype(dk_o.dtype)
        dv_o[...] = dv.astype(dv_o.dtype)

    return pl.pallas_call(
        body,
        name=name,
        grid=(nb,),
        in_specs=specs + [tok, pl.BlockSpec((RET_HEADS, None, LANE, LANE), lambda b: (0, blk(b), 0, 0)), tok],
        out_specs=[pl.BlockSpec((T, 4 * RET_WIDTH), lambda b: (blk(b), 0)), pl.BlockSpec((1, RET_WIDTH), lambda b: (0, 0))],
        out_shape=[jax.ShapeDtypeStruct((S, AB_IN_WIDTH), BF16), jax.ShapeDtypeStruct((1, RET_WIDTH), F32)],
        scratch_shapes=[pltpu.VMEM((RET_HEADS, LANE, LANE), F32)],
        compiler_params=_cp(("arbitrary",)),
    )(z, tb["cos"], tb["sin"], tb["dmat"], tb["aq"], tb["ak"], tb["lam"], gain, opre, states, dcat)


def _pool_counts(t0, rows):
    t = t0 + lax.broadcasted_iota(jnp.int32, (rows, POOL_WIDTH), 0)
    grp = lax.broadcasted_iota(jnp.int32, (rows, POOL_WIDTH), 1) >> 7
    win = jnp.where(grp == 0, POOL_WINDOWS[0], jnp.where(grp == 1, POOL_WINDOWS[1],
                    jnp.where(grp == 2, POOL_WINDOWS[2], POOL_WINDOWS[3])))
    return jnp.maximum(jnp.minimum(t + 1, win), 1).astype(F32), grp


def _window_sums(ext, grp, sign):
    n = ext.shape[0]
    sh = lambda v, k: pltpu.roll(v, k % n if sign > 0 else (n - k) % n, 0)
    s2 = ext + sh(ext, 1)
    s4 = s2 + sh(s2, 2)
    s8 = s4 + sh(s4, 4)
    s16 = s8 + sh(s8, 8)
    return jnp.where(grp == 0, s2, jnp.where(grp == 1, s4, jnp.where(grp == 2, s8, s16)))


def _pool_fwd(name, z, w_pool, scale, cat):
    S = z.shape[0]
    T = min(512, S)
    nb = S // T
    pcol = AB_IN_WIDTH // POOL_WIDTH - 1
    hb = T // POOL_HALO

    def body(p_ref, halo_ref, w_ref, sc_ref, cat_in, out_ref, pooled_ref):
        b = pl.program_id(0)
        cur = p_ref[...].astype(F32)
        halo = jnp.where(b > 0, halo_ref[...].astype(F32), 0.0)
        ext = jnp.concatenate([halo, cur], axis=0)
        cnt, grp = _pool_counts(b * T - POOL_HALO, T + POOL_HALO)
        sums = _window_sums(ext, grp, +1)
        pooled = (sums / cnt)[POOL_HALO:] - cur
        pb = pooled.astype(BF16)
        pooled_ref[...] = pb
        for gi in range(len(POOL_WINDOWS)):
            cs = slice(gi * POOL_GROUP_DIM, (gi + 1) * POOL_GROUP_DIM)
            mixed = _dot(pb[:, cs], w_ref[gi].astype(BF16))
            out_ref[:, cs] = (mixed * sc_ref[:, cs]).astype(out_ref.dtype)

    return pl.pallas_call(
        body,
        name=name,
        grid=(nb,),
        in_specs=[pl.BlockSpec((T, POOL_WIDTH), lambda b: (b, pcol)),
                  pl.BlockSpec((POOL_HALO, POOL_WIDTH), lambda b: (jnp.maximum(b * hb - 1, 0), pcol)),
                  pl.BlockSpec((4, POOL_GROUP_DIM, POOL_GROUP_DIM), lambda b: (0, 0, 0)),
                  pl.BlockSpec((1, POOL_WIDTH), lambda b: (0, 0)), _ANY],
        out_specs=[pl.BlockSpec((T, POOL_WIDTH), lambda b: (b, 1)), pl.BlockSpec((T, POOL_WIDTH), lambda b: (b, 0))],
        out_shape=[jax.ShapeDtypeStruct(cat.shape, cat.dtype), jax.ShapeDtypeStruct((S, POOL_WIDTH), BF16)],
        input_output_aliases={4: 0},
        compiler_params=_cp(("parallel",)),
    )(z, z, w_pool, scale, cat)


def _pool_bwd(name, pooled, w_pool, scale, dcat, dz):
    S = pooled.shape[0]
    T = min(512, S)
    nb = S // T
    hb = T // POOL_HALO
    last_h = S // POOL_HALO - 1
    pcol = AB_IN_WIDTH // POOL_WIDTH - 1

    def body(d_ref, dn_ref, pooled_ref, w_ref, sc_ref, dz_in, dp_ref, dw_ref, dsc_ref):
        b = pl.program_id(0)

        @pl.when(b == 0)
        def _():
            dw_ref[...] = jnp.zeros_like(dw_ref)
            dsc_ref[...] = jnp.zeros_like(dsc_ref)

        sc = sc_ref[...]
        dout = d_ref[...].astype(F32)
        dnext = jnp.where(b < nb - 1, dn_ref[...].astype(F32), 0.0)
        dmix = jnp.concatenate([dout, dnext], axis=0) * sc
        dmb = dmix.astype(BF16)
        pb = pooled_ref[...]
        dpooled = []
        for gi in range(len(POOL_WINDOWS)):
            cs = slice(gi * POOL_GROUP_DIM, (gi + 1) * POOL_GROUP_DIM)
            wb = w_ref[gi].astype(BF16)
            dpooled.append(_dot_nt(dmb[:, cs], wb))
            dw_ref[gi] += _dot_tn(pb[:, cs], dmb[:T, cs])
            mixed = _dot(pb[:, cs], wb)
            dsc_ref[:, cs] += jnp.sum(dout[:, cs] * mixed, axis=0, keepdims=True)
        dpl = jnp.concatenate(dpooled, axis=1)
        cnt, grp = _pool_counts(b * T, T + POOL_HALO)
        sums = _window_sums(dpl / cnt, grp, -1)
        dp_ref[...] = (sums[:T] - dpl[:T]).astype(dp_ref.dtype)

    return pl.pallas_call(
        body,
        name=name,
        grid=(nb,),
        in_specs=[pl.BlockSpec((T, POOL_WIDTH), lambda b: (b, 1)),
                  pl.BlockSpec((POOL_HALO, POOL_WIDTH), lambda b: (jnp.minimum((b + 1) * hb, last_h), 1)),
                  pl.BlockSpec((T, POOL_WIDTH), lambda b: (b, 0)),
                  pl.BlockSpec((4, POOL_GROUP_DIM, POOL_GROUP_DIM), lambda b: (0, 0, 0)),
                  pl.BlockSpec((1, POOL_WIDTH), lambda b: (0, 0)), _ANY],
        out_specs=[pl.BlockSpec((T, POOL_WIDTH), lambda b: (b, pcol)),
                   pl.BlockSpec((4, POOL_GROUP_DIM, POOL_GROUP_DIM), lambda b: (0, 0, 0)),
                   pl.BlockSpec((1, POOL_WIDTH), lambda b: (0, 0))],
        out_shape=[jax.ShapeDtypeStruct(dz.shape, dz.dtype),
                   jax.ShapeDtypeStruct((4, POOL_GROUP_DIM, POOL_GROUP_DIM), F32),
                   jax.ShapeDtypeStruct((1, POOL_WIDTH), F32)],
        input_output_aliases={5: 0},
        compiler_params=_cp(("arbitrary",)),
    )(dcat, dcat, pooled, w_pool, scale, dz)


ATT_STRIP = 32
ATT_Q = 256
ATT_W = ATT_Q + LEFT_CHUNKS * CHUNK


def _rel_index():
    j = np.arange(ATT_W)
    rel = np.clip(LEFT_CHUNKS * CHUNK - j, -REL_CLIP, REL_CLIP) + REL_CLIP
    fwd = np.where(j < BAND, rel, N_REL)
    bwd = np.where(j <= ATT_W - CHUNK, fwd, 2 * REL_CLIP)
    return tuple(jnp.asarray(v.reshape(1, ATT_W), jnp.int32) for v in (fwd, bwd))


def _bias_table(name, rel_bias, rel_idx):
    rb = jnp.concatenate([rel_bias, jnp.full((ATT_HEADS, 1), NEG_INF, F32),
                          jnp.zeros((ATT_HEADS, N_REL_PAD - N_REL - 1), F32)], axis=1)

    def body(rb_ref, idx_ref, o_ref, row0_ref):
        r = lax.broadcasted_iota(jnp.int32, (N_REL_PAD, ATT_W), 0)
        onehot = (r == idx_ref[...]).astype(F32)
        row0_ref[...] = jnp.dot(rb_ref[...], onehot, precision=lax.Precision.HIGHEST, preferred_element_type=F32)
        col = lax.broadcasted_iota(jnp.int32, (CHUNK, ATT_W), 1)
        row = lax.broadcasted_iota(jnp.int32, (CHUNK, ATT_W), 0)
        for h in range(ATT_HEADS):
            same = jnp.broadcast_to(row0_ref[pl.ds(h, 1), :], (CHUNK, ATT_W))
            turned = pltpu.roll(same, 0, 1, stride=1, stride_axis=0)
            o_ref[h] = jnp.where(col >= BAND, NEG_INF, jnp.where(col < row, same, turned))

    return pl.pallas_call(
        body,
        name=name,
        out_shape=jax.ShapeDtypeStruct((ATT_HEADS, CHUNK, ATT_W), F32),
        scratch_shapes=[pltpu.VMEM((ATT_HEADS, ATT_W), F32)],
        compiler_params=pltpu.CompilerParams(vmem_limit_bytes=VMEM_LIMIT),
    )(rb, rel_idx)


def _bias_grad(name, dband, rel_idx):
    def body(d_ref, idx_ref, o_ref, sums_ref):
        row = lax.broadcasted_iota(jnp.int32, (CHUNK, ATT_W), 0)
        for h in range(ATT_HEADS):
            back = d_ref[h]
            for bit in range(CHUNK.bit_length() - 1):
                back = jnp.where(((row >> bit) & 1) == 1, pltpu.roll(back, ATT_W - (1 << bit), 1), back)
            sums_ref[pl.ds(h, 1), :] = jnp.sum(back, axis=0, keepdims=True)
        r = lax.broadcasted_iota(jnp.int32, (N_REL_PAD, ATT_W), 0)
        onehot = (r == idx_ref[...]).astype(F32)
        o_ref[...] = lax.dot_general(sums_ref[...], onehot, (((1,), (1,)), ((), ())),
                                     precision=lax.Precision.HIGHEST, preferred_element_type=F32)

    out = pl.pallas_call(
        body,
        name=name,
        out_shape=jax.ShapeDtypeStruct((ATT_HEADS, N_REL_PAD), F32),
        scratch_shapes=[pltpu.VMEM((ATT_HEADS, ATT_W), F32)],
        compiler_params=pltpu.CompilerParams(vmem_limit_bytes=VMEM_LIMIT),
    )(dband, rel_idx)
    return out[:, :N_REL]


def _attn_unit(q_ref, kw_ref, bias_ref, e, u, lane):
    mine = (lane < ATT_HEAD_DIM) if e == 0 else (lane >= ATT_HEAD_DIM)
    qm = jnp.where(mine, q_ref[u * ATT_Q:(u + 1) * ATT_Q, :] * QSCALE, 0)
    kw = kw_ref[u * ATT_Q:u * ATT_Q + ATT_W, :]
    s = _dot_nt(qm, kw) + bias_ref[u, e]
    p = jnp.exp(s - jnp.max(s, axis=-1, keepdims=True))
    return p, 1.0 / jnp.sum(p, axis=-1, keepdims=True), qm, kw, mine


def _attn_in_specs(nb):
    T = ATT_BLOCK
    hp = ATT_HEADS // 2
    cur = lambda off: pl.BlockSpec((T, LANE), lambda h, b: (jnp.minimum(b, nb - 1), off + h))
    prev = lambda off: pl.BlockSpec((T, LANE), lambda h, b: (jnp.clip(b - 1, 0, nb - 1), off + h))
    return [cur(0), prev(hp), cur(hp), prev(2 * hp), cur(2 * hp),
            pl.BlockSpec((None, 2, CHUNK, ATT_W), lambda h, b: (h, 0, 0, 0))]


def _spread_bias(bias_ref, bm_ref, block):
    col = lax.broadcasted_iota(jnp.int32, (CHUNK, ATT_W), 1)
    for first in (True, False):
        @pl.when(block == (0 if first else 1))
        def _(first=first):
            for u in range(ATT_BLOCK // ATT_Q):
                for e in range(2):
                    for j in range(ATT_Q // CHUNK):
                        rows = pltpu.roll(bias_ref[e], j * CHUNK, 1)
                        if first:
                            rows = jnp.where(col >= ATT_BLOCK - u * ATT_Q, rows, NEG_INF)
                        bm_ref[u, e, j * CHUNK:(j + 1) * CHUNK, :] = rows


def _riding(rider, body, n_in, n_out, grid):
    if rider is None:
        return body, [], [], [], [], {}
    n_rin, n_rout = len(rider.operands), len(rider.out_shapes)
    steps = int(np.prod(grid))

    def riding(*refs):
        ins, rin = refs[:n_in], refs[n_in:n_in + n_rin]
        outs = refs[n_in + n_rin:n_in + n_rin + n_out]
        rout = refs[n_in + n_rin + n_out:n_in + n_rin + n_out + n_rout]
        rest = refs[n_in + n_rin + n_out + n_rout:]
        scratch, sems = rest[:-2], rest[-2:]
        step_no = pl.program_id(0)
        for axis in range(1, len(grid)):
            step_no = step_no * grid[axis] + pl.program_id(axis)

        @pl.when(step_no == 0)
        def _():
            rider.start(rin, rout, *sems)

        body(*ins, *outs, *scratch)
        if rider.relay is not None:
            assert steps >= 3

            @pl.when(step_no == steps - 2)
            def _():
                rider.relay(rin, rout, *sems)

        @pl.when(step_no == steps - 1)
        def _():
            rider.finish(rin, rout, *sems)

    sems = [pltpu.SemaphoreType.DMA((rider.n_sems,)), pltpu.SemaphoreType.DMA((rider.n_sems,))]
    aliases = {n_in + src: n_out + dst for src, dst in rider.aliases}
    return riding, [_ANY] * n_rin, [_ANY] * n_rout, list(rider.out_shapes), sems, aliases


def _attn_fwd(name, qkv, bias, rider=None):
    S = qkv.shape[0]
    T = ATT_BLOCK
    nb = S // T

    def body(q_ref, kp_ref, kc_ref, vp_ref, vc_ref, band_ref, o_ref, kw_ref, vw_ref, bias_ref, s_ref, p_ref, inv_ref):
        _spread_bias(band_ref, bias_ref, pl.program_id(1))
        kw_ref[0:T] = kp_ref[...]
        kw_ref[T:2 * T] = kc_ref[...]
        vw_ref[0:T] = vp_ref[...]
        vw_ref[T:2 * T] = vc_ref[...]
        lane = lax.broadcasted_iota(jnp.int32, (ATT_Q, LANE), 1)
        for u in range(T // ATT_Q):
            vw = vw_ref[u * ATT_Q:u * ATT_Q + ATT_W, :]
            kw = kw_ref[u * ATT_Q:u * ATT_Q + ATT_W, :]
            outs = []
            for e in range(2):
                mine = (lane < ATT_HEAD_DIM) if e == 0 else (lane >= ATT_HEAD_DIM)
                qm = jnp.where(mine, q_ref[u * ATT_Q:(u + 1) * ATT_Q, :] * QSCALE, 0)
                s_ref[e] = _dot_nt(qm, kw)
                for r in range(ATT_Q // ATT_STRIP):
                    rows = slice(r * ATT_STRIP, (r + 1) * ATT_STRIP)
                    s = s_ref[e, rows, :] + bias_ref[u, e, rows, :]
                    p = jnp.exp(s - jnp.max(s, axis=-1, keepdims=True))
                    inv_ref[e, rows, :] = jnp.broadcast_to(1.0 / jnp.sum(p, axis=-1, keepdims=True), (ATT_STRIP, LANE))
                    p_ref[e, rows, :] = p.astype(BF16)
                outs.append(_dot(p_ref[e], vw) * inv_ref[e])
            o_ref[u * ATT_Q:(u + 1) * ATT_Q, :] = jnp.where(lane < ATT_HEAD_DIM, outs[0], outs[1]).astype(o_ref.dtype)

    grid = (ATT_HEADS // 2, nb)
    body, r_in, r_out, r_shapes, r_sems, aliases = _riding(rider, body, 6, 1, grid)
    outs = pl.pallas_call(
        body,
        name=name,
        grid=grid,
        in_specs=_attn_in_specs(nb) + r_in,
        out_specs=[pl.BlockSpec((T, LANE), lambda h, b: (b, h))] + r_out,
        out_shape=[jax.ShapeDtypeStruct((S, D_MODEL), BF16)] + r_shapes,
        input_output_aliases=aliases,
        scratch_shapes=[pltpu.VMEM((2 * T, LANE), BF16), pltpu.VMEM((2 * T, LANE), BF16),
                        pltpu.VMEM((T // ATT_Q, 2, ATT_Q, ATT_W), F32), pltpu.VMEM((2, ATT_Q, ATT_W), F32),
                        pltpu.VMEM((2, ATT_Q, ATT_W), BF16), pltpu.VMEM((2, ATT_Q, LANE), F32)] + r_sems,
        compiler_params=_cp(("parallel" if rider is None else "arbitrary", "arbitrary")),
    )(qkv, qkv, qkv, qkv, qkv, bias, *(rider.operands if rider is not None else ()))
    return outs[0] if rider is None else (outs[0], list(outs[1:]))


def _attn_bwd(name, qkv, bias, do, rider=None):
    S = qkv.shape[0]
    T = ATT_BLOCK
    nb = S // T

    def body(q_ref, kp_ref, kc_ref, vp_ref, vc_ref, band_ref, do_ref,
             dq_ref, dk_ref, dv_ref, dband_ref, kw_ref, vw_ref, dkw_ref, dvw_ref, bias_ref, dbias_ref):
        b = pl.program_id(1)

        _spread_bias(band_ref, bias_ref, b)

        @pl.when(b == 0)
        def _():
            dbias_ref[...] = jnp.zeros_like(dbias_ref)
            dkw_ref[:, T:2 * T] = jnp.zeros((LANE, T), F32)
            dvw_ref[:, T:2 * T] = jnp.zeros((LANE, T), F32)

        dkw_ref[:, 0:T] = dkw_ref[:, T:2 * T]
        dvw_ref[:, 0:T] = dvw_ref[:, T:2 * T]
        dkw_ref[:, T:2 * T] = jnp.zeros((LANE, T), F32)
        dvw_ref[:, T:2 * T] = jnp.zeros((LANE, T), F32)

        @pl.when(b < nb)
        def _():
            kw_ref[0:T] = kp_ref[...]
            kw_ref[T:2 * T] = kc_ref[...]
            vw_ref[0:T] = vp_ref[...]
            vw_ref[T:2 * T] = vc_ref[...]
            lane = lax.broadcasted_iota(jnp.int32, (ATT_Q, LANE), 1)
            for u in range(T // ATT_Q):
                rows = slice(u * ATT_Q, (u + 1) * ATT_Q)
                win = slice(u * ATT_Q, u * ATT_Q + ATT_W)
                vw = vw_ref[win, :]
                do2 = do_ref[rows, :]
                dqs, dk, dv = [], None, None
                for e in range(2):
                    p, inv, qm, kw, mine = _attn_unit(q_ref, kw_ref, bias_ref, e, u, lane)
                    dom = jnp.where(mine, do2, 0)
                    dp = _dot_nt(dom, vw)
                    delta = jnp.sum(p * dp, axis=-1, keepdims=True) * inv
                    ds = p * ((dp - delta) * inv)
                    dbias_ref[e] += ds
                    dsb = ds.astype(BF16)
                    dqs.append(_dot(dsb, kw))
                    dk_e = _dot_tn(qm, dsb)
                    dv_e = _dot_tn((dom * inv).astype(BF16), p.astype(BF16))
                    dk = dk_e if dk is None else dk + dk_e
                    dv = dv_e if dv is None else dv + dv_e
                dq_ref[rows, :] = (jnp.where(lane < ATT_HEAD_DIM, dqs[0], dqs[1]) * QSCALE).astype(dq_ref.dtype)
                dkw_ref[:, win] += dk
                dvw_ref[:, win] += dv

        @pl.when(b > 0)
        def _():
            dk_ref[...] = dkw_ref[:, 0:T].T.astype(dk_ref.dtype)
            dv_ref[...] = dvw_ref[:, 0:T].T.astype(dv_ref.dtype)

        @pl.when(b == nb)
        def _():
            for e in range(2):
                acc = dbias_ref[e, 0:CHUNK, :]
                for j in range(1, ATT_Q // CHUNK):
                    acc = acc + pltpu.roll(dbias_ref[e, j * CHUNK:(j + 1) * CHUNK, :], ATT_W - j * CHUNK, 1)
                dband_ref[e] = acc

    tok = jax.ShapeDtypeStruct((S, D_MODEL), BF16)
    prev_out = pl.BlockSpec((T, LANE), lambda h, b: (jnp.maximum(b - 1, 0), h))
    grid = (ATT_HEADS // 2, nb + 1)
    body, r_in, r_out, r_shapes, r_sems, aliases = _riding(rider, body, 7, 4, grid)
    outs = pl.pallas_call(
        body,
        name=name,
        grid=grid,
        in_specs=_attn_in_specs(nb) + [pl.BlockSpec((T, LANE), lambda h, b: (jnp.minimum(b, nb - 1), h))] + r_in,
        out_specs=[pl.BlockSpec((T, LANE), lambda h, b: (jnp.minimum(b, nb - 1), h)), prev_out, prev_out,
                   pl.BlockSpec((None, 2, CHUNK, ATT_W), lambda h, b: (h, 0, 0, 0))] + r_out,
        out_shape=[tok, tok, tok, jax.ShapeDtypeStruct((ATT_HEADS // 2, 2, CHUNK, ATT_W), F32)] + r_shapes,
        input_output_aliases=aliases,
        scratch_shapes=[pltpu.VMEM((2 * T, LANE), BF16), pltpu.VMEM((2 * T, LANE), BF16),
                        pltpu.VMEM((LANE, 2 * T), F32), pltpu.VMEM((LANE, 2 * T), F32),
                        pltpu.VMEM((T // ATT_Q, 2, ATT_Q, ATT_W), F32), pltpu.VMEM((2, ATT_Q, ATT_W), F32)] + r_sems,
        compiler_params=_cp(("parallel" if rider is None else "arbitrary", "arbitrary")),
    )(qkv, qkv, qkv, qkv, qkv, bias, do, *(rider.operands if rider is not None else ()))
    return tuple(outs[:4]) if rider is None else (tuple(outs[:4]), list(outs[4:]))


def _local_step(x, target, small, W):
    S = x.shape[0]
    tb = _ret_tables(S)
    rel_fwd, rel_bwd = _rel_index()
    saved = []
    normed = (("tile", BF16),)
    deep = dict(tm=512, tk=D_FF)
    h = _rms_fwd("mix_norm_fwd0", x, small["mix_norm"][0:1])
    for layer in range(DEPTH):
        i = layer // 2
        st = {"x_in": x, "h": h}
        g_ffn = small["ffn_norm"][layer:layer + 1]
        if layer % 2 == 0:
            z = W.mm(f"ab_in_fwd{layer}", "nn", h, W.get("ab_w_in", i), tm=2048, tn=640, out_dtype=BF16)
            gain = small["ab_gn_gain"][i:i + 1]
            cat, opre, states = _ret_fwd(f"ret_fwd{layer}", z, tb, gain)
            cat, pooled = _pool_fwd(f"pool_fwd{layer}", z, small["ab_w_pool"][i], small["ab_pool_scale"][i:i + 1], cat)
            st.update(z=z, opre=opre, states=states, pooled=pooled, cat=cat)
            x, hn = W.mm(f"ab_out_fwd{layer}", "nn", cat, W.get("ab_w_out", i), extras=(x,), aux=(g_ffn,), sides=normed,
                         epi=_epi_residual_norm)
        else:
            qkv = W.mm(f"qkv_fwd{layer}", "nn", h, W.get("c_w_qkv", i), tm=2048, out_dtype=BF16)
            bias = _bias_table(f"bias_table{layer}", small["c_rel_bias"][i], rel_fwd)
            bias = bias.reshape(ATT_HEADS // 2, 2, CHUNK, ATT_W)
            att = W.hosted(f"attn_fwd{layer}", lambda rider: _attn_fwd(f"attn_fwd{layer}", qkv, bias, rider=rider))
            st.update(qkv=qkv, bias=bias, att=att)
            x, hn = W.mm(f"c_out_fwd{layer}", "nn", att, W.get("c_w_out", i), extras=(x,), aux=(g_ffn,), sides=normed,
                         epi=_epi_residual_norm)
        st["x_mid"] = x
        u = W.mm(f"ffn_in_fwd{layer}", "nn", hn, W.get("w_ffn_in", layer), out_dtype=BF16, tm=2048)
        if layer + 1 < DEPTH:
            x, h = W.mm(f"ffn_out_fwd{layer}", "nn", u, W.get("w_ffn_out", layer), a_fn=_relu2, extras=(x,),
                        aux=(small["mix_norm"][layer + 1:layer + 2],), sides=normed, epi=_epi_residual_norm, **deep)
        else:
            x = W.mm(f"ffn_out_fwd{layer}", "nn", u, W.get("w_ffn_out", layer), a_fn=_relu2, extras=(x,),
                     epi=_epi_residual, **deep)
        st.update(hn=hn, u=u)
        saved.append(st)

    loss, dx, dxb, d_final = _loss_head(x, small["final_norm"].reshape(1, D_MODEL), target)

    gs = {k: [None] * v.shape[0] for k, v in small.items() if k != "final_norm"}
    gb = {k: None for k in W.n_layers}
    landed = {k: None for k in W.n_layers}
    pending = []

    def carry(call, take=1):
        items = [pending.pop(0) for _ in range(min(take, len(pending)))]
        if not items:
            return call(None)
        riders = [_grad_rider(key, idx, gb[key], landed[key]) for key, idx in items]
        res, outs = call(_join_riders(riders))
        for (key, _), out in zip(items, outs):
            landed[key] = out
        return res

    def host(name, *args, take=1, **kw):
        return carry(lambda rider: _mm(name, *args, rider=rider, **kw), take)

    def dw(name, key, idx, a, b, call=_mm, **kw):
        gb[key] = call(name, "tn", a, b, stack=(W.n_layers[key], idx, gb[key]), out_dtype=BF16, **kw)
        pending.append((key, idx))

    gain_sums = (("tile", BF16), ("colsum", F32))
    for layer in reversed(range(DEPTH)):
        i = layer // 2
        st = saved[layer]
        du = host(f"ffn_out_bwd{layer}", "nt", dxb, W.get("w_ffn_out", layer), extras=(st["u"],),
                  epi=lambda acc, u: acc * (2.0 * jnp.maximum(u, 0).astype(F32)), out_dtype=BF16, tm=2048)
        dw(f"ffn_out_dw{layer}", "w_ffn_out", layer, st["u"], dxb, a_fn=_relu2, tk=2048)
        dx, dxb, dgain = host(f"ffn_in_bwd{layer}", "nt", du, W.get("w_ffn_in", layer), extras=(st["x_mid"], dx),
                         aux=(small["ffn_norm"][layer:layer + 1],), sides=gain_sums, epi=_epi_rms_bwd, **deep)
        gs["ffn_norm"][layer] = dgain[0:1]
        dw(f"ffn_in_dw{layer}", "w_ffn_in", layer, st["hn"], du, tk=2048)
        norm_bwd = dict(extras=(st["x_in"], dx), aux=(small["mix_norm"][layer:layer + 1],), sides=gain_sums,
                        epi=_epi_rms_bwd)
        if layer % 2 == 0:
            dcat = _mm(f"ab_out_bwd{layer}", "nt", dxb, W.get("ab_w_out", i), out_dtype=BF16)
            dw(f"ab_out_dw{layer}", "ab_w_out", i, st["cat"], dxb, tk=2048)
            gain = small["ab_gn_gain"][i:i + 1]
            dz, gs["ab_gn_gain"][i] = _ret_bwd(f"ret_bwd{layer}", st["z"], tb, gain, st["opre"], st["states"], dcat)
            dz, gs["ab_w_pool"][i], gs["ab_pool_scale"][i] = _pool_bwd(
                f"pool_bwd{layer}", st["pooled"], small["ab_w_pool"][i], small["ab_pool_scale"][i:i + 1], dcat, dz)
            if layer == 0:
                dw(f"ab_in_dw{layer}", "ab_w_in", i, st["h"], dz, call=host, tn=640, tk=2048)
            dx, dxb, dgain = host(f"ab_in_bwd{layer}", "nt", dz, W.get("ab_w_in", i), tm=512, tk=AB_IN_WIDTH,
                             take=len(pending) if layer == 0 else 1, **norm_bwd)
            if layer > 0:
                dw(f"ab_in_dw{layer}", "ab_w_in", i, st["h"], dz, call=host, tn=640, tk=2048)
        else:
            datt = _mm(f"c_out_bwd{layer}", "nt", dxb, W.get("c_w_out", i), out_dtype=BF16)
            dw(f"c_out_dw{layer}", "c_w_out", i, st["att"], dxb, tk=2048)
            dq, dk, dv, dbias = carry(lambda rider: _attn_bwd(f"attn_bwd{layer}", st["qkv"], st["bias"], datt, rider=rider),
                                      take=len(pending))
            gs["c_rel_bias"][i] = _bias_grad(f"bias_grad{layer}", dbias.reshape(ATT_HEADS, CHUNK, ATT_W), rel_bwd)
            dqkv = [dq, dk, dv]
            dx, dxb, dgain = host(f"qkv_bwd{layer}", "nt", dqkv, W.get("c_w_qkv", i), tm=512, tk=3 * D_MODEL, **norm_bwd)
            dw(f"qkv_dw{layer}", "c_w_qkv", i, st["h"], dqkv, call=host, tk=2048)
        gs["mix_norm"][layer] = dgain[0:1]
    for key, idx in pending:
        landed[key], = _run_rider(f"grad_exchange_{key}{idx}", _grad_rider(key, idx, gb[key], landed[key]))

    g_small = {
        "mix_norm": jnp.concatenate(gs["mix_norm"], axis=0),
        "ffn_norm": jnp.concatenate(gs["ffn_norm"], axis=0),
        "ab_gn_gain": jnp.concatenate(gs["ab_gn_gain"], axis=0),
        "ab_w_pool": jnp.stack(gs["ab_w_pool"], axis=0),
        "ab_pool_scale": jnp.concatenate(gs["ab_pool_scale"], axis=0),
        "c_rel_bias": jnp.stack(gs["c_rel_bias"], axis=0),
        "final_norm": d_final.reshape(D_MODEL),
    }
    return loss, dx, g_small, gb, landed


_BIG = ("w_ffn_in", "w_ffn_out", "ab_w_in", "ab_w_out", "c_w_qkv", "c_w_out")
_SHARD_AXIS = {"w_ffn_in": 2, "w_ffn_out": 1, "ab_w_in": 2, "ab_w_out": 1, "c_w_qkv": 2, "c_w_out": 1}
_SMALL = ("mix_norm", "ffn_norm", "ab_gn_gain", "ab_w_pool", "ab_pool_scale", "c_rel_bias", "final_norm")


def _place():
    x, y, c = lax.axis_index("x"), lax.axis_index("y"), lax.axis_index("c")
    chips = [(1 - x, y), (x, 1 - y), (1 - x, 1 - y)]
    return x, y, c, chips


def _sub(ref, axis, start, size):
    idx = [slice(None)] * len(ref.shape)
    idx[axis] = pl.ds(pl.multiple_of(start, LANE), size)
    return ref.at[tuple(idx)]


def _gather_rider(items, shards):
    keys = sorted({k for k, _ in items})
    n = len(items)
    axes = [_SHARD_AXIS[k] - 1 for k, _ in items]
    sizes = [shards[k].shape[a + 1] for (k, _), a in zip(items, axes)]
    hsizes = [shards[k].shape[2 - a] // 2 for (k, _), a in zip(items, axes)]

    def views(ins, outs, send_sems, recv_sems):
        x, y, c, chips = _place()
        srcs = [ins[keys.index(k)].at[l] for k, l in items]

        def remote(src, dst, s, to):
            return pltpu.make_async_remote_copy(src_ref=src, dst_ref=dst, send_sem=send_sems.at[s],
                                                recv_sem=recv_sems.at[s], device_id=to, device_id_type=MESH)

        def half(w, chip, core):
            return _sub(_sub(outs[w], axes[w], chip * sizes[w], sizes[w]), 1 - axes[w], core * hsizes[w], hsizes[w])

        me = 2 * x + y
        local = [pltpu.make_async_copy(srcs[w], _sub(outs[w], axes[w], me * sizes[w], sizes[w]), send_sems.at[6 * n + w])
                 for w in range(n)]
        first = [remote(_sub(srcs[w], 1 - axes[w], c * hsizes[w], hsizes[w]), half(w, me, c), w * 6 + k, (px, py, c))
                 for w in range(n) for k, (px, py) in enumerate(chips)]
        return x, y, c, chips, remote, half, local, first

    def start(ins, outs, send_sems, recv_sems):
        *_, local, first = views(ins, outs, send_sems, recv_sems)
        for cp in local + first:
            cp.start()

    def passes(x, y, c, chips, remote, half):
        return [remote(half(w, 2 * px + py, c), half(w, 2 * px + py, c), w * 6 + 3 + k, (x, y, 1 - c))
                for w in range(n) for k, (px, py) in enumerate(chips)]

    def relay(ins, outs, send_sems, recv_sems):
        x, y, c, chips, remote, half, _, _ = views(ins, outs, send_sems, recv_sems)
        for w in range(n):
            for k, (px, py) in enumerate(chips):
                landed = half(w, 2 * px + py, c)
                remote(landed, landed, w * 6 + k, (px, py, c)).wait_recv()
        for cp in passes(x, y, c, chips, remote, half):
            cp.start()

    def finish(ins, outs, send_sems, recv_sems):
        x, y, c, chips, remote, half, local, first = views(ins, outs, send_sems, recv_sems)
        for w in range(n):
            for k, (px, py) in enumerate(chips):
                theirs = half(w, 2 * px + py, 1 - c)
                remote(theirs, theirs, w * 6 + 3 + k, (x, y, 1 - c)).wait_recv()
        for cp in first + passes(x, y, c, chips, remote, half):
            cp.wait_send()
        for cp in local:
            cp.wait()

    def full(k, a):
        shape = list(shards[k].shape[1:])
        shape[a] *= N_CHIPS
        return jax.ShapeDtypeStruct(tuple(shape), shards[k].dtype)

    return _Rider(tuple(shards[k] for k in keys), tuple(full(k, a) for (k, _), a in zip(items, axes)), 7 * n, start, finish,
                  relay=relay)


def _mixer_items(layer):
    names = ("ab_w_in", "ab_w_out") if layer % 2 == 0 else ("c_w_qkv", "c_w_out")
    return [(k, layer // 2) for k in names]


class _Weights:
    def __init__(self, shards):
        self.shards = shards
        self.n_layers = {k: shards[k].shape[0] for k in _BIG}
        self.full = {}
        first, second = _mixer_items(0)
        self._take([first], _run_rider("gather_first", _gather_rider([first], shards)))
        self.plan = {"ab_in_fwd0": [second, ("w_ffn_in", 0)], "ab_out_fwd0": [("w_ffn_out", 0)]}
        for layer in range(1, DEPTH):
            if layer % 2 == 0:
                self.plan[f"attn_fwd{layer - 1}"] = _mixer_items(layer) + [("w_ffn_in", layer), ("w_ffn_out", layer)]
            else:
                self.plan[f"ffn_in_fwd{layer - 1}"] = _mixer_items(layer)
                self.plan[f"ffn_out_fwd{layer - 1}"] = [("w_ffn_in", layer)]
                self.plan[f"qkv_fwd{layer}"] = [("w_ffn_out", layer)]

    def _take(self, items, outs):
        self.full.update(zip(items, outs))

    def get(self, name, layer):
        return self.full[(name, layer)]

    def hosted(self, name, call):
        items = self.plan.get(name)
        if items is None:
            return call(None)
        res, outs = call(_gather_rider(items, self.shards))
        self._take(items, outs)
        return res

    def mm(self, name, *args, **kw):
        return self.hosted(name, lambda rider: _mm(name, *args, rider=rider, **kw))


def _run_rider(name, rider):
    n_in, n_out = len(rider.operands), len(rider.out_shapes)

    def body(*refs):
        ins, outs, sems = refs[:n_in], refs[n_in:n_in + n_out], refs[n_in + n_out:]
        rider.start(ins, outs, *sems)
        if rider.relay is not None:
            rider.relay(ins, outs, *sems)
        rider.finish(ins, outs, *sems)

    return pl.pallas_call(
        body,
        name=name,
        in_specs=[_ANY] * n_in,
        out_specs=[_ANY] * n_out,
        out_shape=list(rider.out_shapes),
        input_output_aliases=dict(rider.aliases),
        scratch_shapes=[pltpu.SemaphoreType.DMA((rider.n_sems,)), pltpu.SemaphoreType.DMA((rider.n_sems,))],
        compiler_params=pltpu.CompilerParams(has_side_effects=True),
    )(*rider.operands)


def _grad_rider(name, layer, grad, landing):
    axis = _SHARD_AXIS[name] - 1
    L, R, C = grad.shape
    shard = (R // N_CHIPS, C) if axis == 0 else (R, C // N_CHIPS)
    size = shard[axis]

    def copies(ins, outs, send_sems, recv_sems):
        x, y, c, chips = _place()
        return [pltpu.make_async_remote_copy(
            src_ref=_sub(ins[0].at[layer], axis, (2 * px + py) * size, size), dst_ref=outs[0].at[layer, k],
            send_sem=send_sems.at[k], recv_sem=recv_sems.at[k], device_id=(px, py, c), device_id_type=MESH)
            for k, (px, py) in enumerate(chips)]

    def start(ins, outs, send_sems, recv_sems):
        for cp in copies(ins, outs, send_sems, recv_sems):
            cp.start()

    def finish(ins, outs, send_sems, recv_sems):
        cps = copies(ins, outs, send_sems, recv_sems)
        for cp in cps:
            cp.wait_recv()
        for cp in cps:
            cp.wait_send()

    out = jax.ShapeDtypeStruct((L, 3) + shard, grad.dtype)
    if landing is None:
        return _Rider((grad,), (out,), 3, start, finish)
    return _Rider((grad, landing), (out,), 3, start, finish, aliases=((1, 0),))


def _join_riders(riders):
    if len(riders) == 1:
        return riders[0]

    def parts(ins, outs, send_sems, recv_sems):
        i0 = o0 = s0 = 0
        for r in riders:
            ni, no = len(r.operands), len(r.out_shapes)
            yield (r, ins[i0:i0 + ni], outs[o0:o0 + no], send_sems.at[pl.ds(s0, r.n_sems)],
                   recv_sems.at[pl.ds(s0, r.n_sems)])
            i0, o0, s0 = i0 + ni, o0 + no, s0 + r.n_sems

    def start(*refs):
        for r, *own in parts(*refs):
            r.start(*own)

    def finish(*refs):
        for r, *own in parts(*refs):
            r.finish(*own)

    aliases, i0, o0 = [], 0, 0
    for r in riders:
        aliases += [(i0 + src, o0 + dst) for src, dst in r.aliases]
        i0, o0 = i0 + len(r.operands), o0 + len(r.out_shapes)
    return _Rider(tuple(x for r in riders for x in r.operands), tuple(x for r in riders for x in r.out_shapes),
                  sum(r.n_sems for r in riders), start, finish, tuple(aliases))


def _pair_swap(sums):
    n = len(sums)

    def body(*refs):
        ins, outs = refs[:n], refs[n:2 * n]
        send_sems, recv_sems = refs[2 * n:]
        x, y, c, _ = _place()
        cps = [pltpu.make_async_remote_copy(src_ref=ins[w], dst_ref=outs[w], send_sem=send_sems.at[w],
                                            recv_sem=recv_sems.at[w], device_id=(x, y, 1 - c), device_id_type=MESH)
               for w in range(n)]
        for cp in cps:
            cp.start()
        for cp in cps:
            cp.wait_recv()
        for cp in cps:
            cp.wait_send()

    return pl.pallas_call(
        body,
        name="pair_swap",
        in_specs=[_ANY] * n,
        out_specs=[_ANY] * n,
        out_shape=[jax.ShapeDtypeStruct(s.shape, s.dtype) for s in sums],
        scratch_shapes=[pltpu.SemaphoreType.DMA((n,)), pltpu.SemaphoreType.DMA((n,))],
        compiler_params=pltpu.CompilerParams(has_side_effects=True),
    )(*sums)


def _rows_tile(rows, cols):
    tr = rows
    while tr * cols > (1 << 19) and tr % 16 == 0:
        tr //= 2
    return tr


def _chip_sum(name, grad, landed, chip, saxis):
    L = grad.shape[0]
    _, _, R, C = landed.shape
    tr = _rows_tile(R, C)
    nr = R // tr
    if saxis == 2:
        g_idx = lambda l, i, s: (l, i, s[0])
    else:
        g_idx = lambda l, i, s: (l, s[0] * nr + i, 0)

    def body(s_ref, g_ref, l_ref, o_ref):
        tot = ((g_ref[...].astype(F32) + l_ref[0].astype(F32)) + l_ref[1].astype(F32)) + l_ref[2].astype(F32)
        o_ref[...] = tot.astype(o_ref.dtype)

    return pl.pallas_call(
        body,
        name=name,
        grid_spec=pltpu.PrefetchScalarGridSpec(
            num_scalar_prefetch=1,
            grid=(L, nr),
            in_specs=[pl.BlockSpec((None, tr, C), g_idx), pl.BlockSpec((None, 3, tr, C), lambda l, i, s: (l, 0, i, 0))],
            out_specs=pl.BlockSpec((None, tr, C), lambda l, i, s: (l, i, 0)),
        ),
        out_shape=jax.ShapeDtypeStruct((L, R, C), BF16),
        compiler_params=_cp(("parallel", "parallel")),
    )(chip, grad, landed)


def _all_reduce_small(packed):
    R = packed.shape[0]

    def body(p_ref, o_ref, land_ref, send_sems, recv_sems):
        x, y, c, _ = _place()
        me = 4 * x + 2 * y + c
        sends, recvs = [], []
        for r in range(1, N_DEV):
            px, py, pc = x ^ (r >> 2), y ^ ((r >> 1) & 1), c ^ (r & 1)
            cp = pltpu.make_async_remote_copy(src_ref=p_ref, dst_ref=land_ref.at[me], send_sem=send_sems.at[r - 1],
                                              recv_sem=recv_sems.at[r - 1], device_id=(px, py, pc), device_id_type=MESH)
            cp.start()
            sends.append(cp)
            recvs.append(pltpu.make_async_remote_copy(src_ref=p_ref, dst_ref=land_ref.at[4 * px + 2 * py + pc],
                                                      send_sem=send_sems.at[r - 1], recv_sem=recv_sems.at[r - 1],
                                                      device_id=(px, py, pc), device_id_type=MESH))
        land_ref[me] = p_ref[...]
        for cp in recvs:
            cp.wait_recv()
        for cp in sends:
            cp.wait_send()
        acc = land_ref[0]
        for d in range(1, N_DEV):
            acc = acc + land_ref[d]
        o_ref[...] = acc

    vm = pl.BlockSpec(memory_space=pltpu.VMEM)
    return pl.pallas_call(
        body,
        name="all_reduce_small",
        in_specs=[vm],
        out_specs=vm,
        out_shape=jax.ShapeDtypeStruct((R, LANE), F32),
        scratch_shapes=[pltpu.VMEM((N_DEV, R, LANE), F32), pltpu.SemaphoreType.DMA((N_DEV - 1,)),
                        pltpu.SemaphoreType.DMA((N_DEV - 1,))],
        compiler_params=pltpu.CompilerParams(has_side_effects=True, vmem_limit_bytes=VMEM_LIMIT),
    )(packed)


def _adamw(name, w, m, v, grads):
    R, C = w.shape
    tr = _rows_tile(R, C)
    c1 = 1.0 - ADAM_B1 ** ADAM_STEP
    c2 = 1.0 - ADAM_B2 ** ADAM_STEP
    ng = len(grads)

    def body(*refs):
        w_ref, m_ref, v_ref = refs[:3]
        g_refs = refs[3:3 + ng]
        g_ref, d_ref, nm_ref, nv_ref = refs[3 + ng:]
        gv = g_refs[0][...].astype(F32)
        for r in g_refs[1:]:
            gv = gv + r[...].astype(F32)
        g_ref[...] = gv
        nm = ADAM_B1 * m_ref[...] + (1.0 - ADAM_B1) * gv
        nv = ADAM_B2 * v_ref[...] + (1.0 - ADAM_B2) * (gv * gv)
        nm_ref[...] = nm
        nv_ref[...] = nv
        d_ref[...] = -ADAM_LR * ((nm / c1) / (jnp.sqrt(nv / c2) + ADAM_EPS) + ADAM_WD * w_ref[...])

    blk = pl.BlockSpec((tr, C), lambda i: (i, 0))
    out = jax.ShapeDtypeStruct((R, C), F32)
    return pl.pallas_call(
        body,
        name=name,
        grid=(R // tr,),
        in_specs=[blk] * (3 + ng),
        out_specs=[blk] * 4,
        out_shape=[out] * 4,
        compiler_params=_cp(("parallel",)),
    )(w, m, v, *grads)


def _pack(parts):
    rows = []
    for p in parts:
        flat = p.reshape(-1).astype(F32)
        n = -(-flat.shape[0] // (8 * LANE)) * (8 * LANE)
        rows.append(jnp.pad(flat, (0, n - flat.shape[0])).reshape(n // LANE, LANE))
    return jnp.concatenate(rows, axis=0)


def _unpack(packed, like):
    out, r = [], 0
    for p in like:
        size = int(np.prod(p.shape))
        n = -(-size // (8 * LANE)) * 8
        out.append(packed[r:r + n].reshape(-1)[:size].reshape(p.shape))
        r += n
    return out


def kernel(x, mix_norm, ffn_norm, w_ffn_in, w_ffn_out, ab_w_in, ab_gn_gain, ab_w_pool, ab_pool_scale, ab_w_out, c_w_qkv, c_rel_bias, c_w_out, final_norm, loss_target, m_mix_norm, m_ffn_norm, m_w_ffn_in, m_w_ffn_out, m_ab_w_in, m_ab_gn_gain, m_ab_w_pool, m_ab_pool_scale, m_ab_w_out, m_c_w_qkv, m_c_rel_bias, m_c_w_out, m_final_norm, v_mix_norm, v_ffn_norm, v_w_ffn_in, v_w_ffn_out, v_ab_w_in, v_ab_gn_gain, v_ab_w_pool, v_ab_pool_scale, v_ab_w_out, v_c_w_qkv, v_c_rel_bias, v_c_w_out, v_final_norm):
    w = dict(mix_norm=mix_norm, ffn_norm=ffn_norm, w_ffn_in=w_ffn_in, w_ffn_out=w_ffn_out, ab_w_in=ab_w_in,
             ab_gn_gain=ab_gn_gain, ab_w_pool=ab_w_pool, ab_pool_scale=ab_pool_scale, ab_w_out=ab_w_out,
             c_w_qkv=c_w_qkv, c_rel_bias=c_rel_bias, c_w_out=c_w_out, final_norm=final_norm)
    m = dict(mix_norm=m_mix_norm, ffn_norm=m_ffn_norm, w_ffn_in=m_w_ffn_in, w_ffn_out=m_w_ffn_out, ab_w_in=m_ab_w_in,
             ab_gn_gain=m_ab_gn_gain, ab_w_pool=m_ab_w_pool, ab_pool_scale=m_ab_pool_scale, ab_w_out=m_ab_w_out,
             c_w_qkv=m_c_w_qkv, c_rel_bias=m_c_rel_bias, c_w_out=m_c_w_out, final_norm=m_final_norm)
    v = dict(mix_norm=v_mix_norm, ffn_norm=v_ffn_norm, w_ffn_in=v_w_ffn_in, w_ffn_out=v_w_ffn_out, ab_w_in=v_ab_w_in,
             ab_gn_gain=v_ab_gn_gain, ab_w_pool=v_ab_w_pool, ab_pool_scale=v_ab_pool_scale, ab_w_out=v_ab_w_out,
             c_w_qkv=v_c_w_qkv, c_rel_bias=v_c_rel_bias, c_w_out=v_c_w_out, final_norm=v_final_norm)
    S = x.shape[1]
    cx, cy, cc = lax.axis_index("x"), lax.axis_index("y"), lax.axis_index("c")
    chip = jnp.reshape(2 * cx + cy, (1,)).astype(jnp.int32)

    big = _Weights({k: w[k].astype(BF16) for k in _BIG})
    small = {k: w[k] for k in _SMALL}
    loss, grad_x, g_small, g_big, landed = _local_step(x.reshape(S, D_MODEL), loss_target.reshape(S, D_MODEL), small, big)

    sums = [_chip_sum(f"chip_sum_{k}", g_big[k], landed[k], chip, _SHARD_AXIS[k]) for k in _BIG]
    siblings = _pair_swap(sums)

    packed = _all_reduce_small(_pack([g_small[k] for k in _SMALL] + [loss]))
    small_like = [w[k] for k in _SMALL]
    g_red = dict(zip(_SMALL, _unpack(packed, small_like)))
    loss_row = packed.shape[0] - 8
    loss_out = packed[loss_row, 0]

    grad, delta, new_m, new_v = {}, {}, {}, {}
    for k, mine, theirs in zip(_BIG, sums, siblings):
        shp = w[k].shape
        two = (shp[0] * shp[1], shp[2])
        outs = _adamw(f"adamw_{k}", w[k].reshape(two), m[k].reshape(two), v[k].reshape(two),
                      (mine.reshape(two), theirs.reshape(two)))
        grad[k], delta[k], new_m[k], new_v[k] = [o.reshape(shp) for o in outs]
    _, d, nm, nv = _adamw("adamw_small", _pack(small_like), _pack([m[k] for k in _SMALL]), _pack([v[k] for k in _SMALL]),
                          (packed[:loss_row],))
    for k, dk, mk, vk in zip(_SMALL, _unpack(d, small_like), _unpack(nm, small_like), _unpack(nv, small_like)):
        grad[k], delta[k], new_m[k], new_v[k] = g_red[k], dk, mk, vk

    order = ("mix_norm", "ffn_norm", "w_ffn_in", "w_ffn_out", "ab_w_in", "ab_gn_gain", "ab_w_pool", "ab_pool_scale",
             "ab_w_out", "c_w_qkv", "c_rel_bias", "c_w_out", "final_norm")
    return (loss_out, grad_x.reshape(x.shape), *[grad[k] for k in order], *[delta[k] for k in order],
            *[new_m[k] for k in order], *[new_v[k] for k in order])
```

```python
import functools
from typing import Callable, NamedTuple

import numpy as np
import jax
import jax.numpy as jnp
from jax import lax
from jax.experimental import pallas as pl
from jax.experimental.pallas import tpu as pltpu

F32 = jnp.float32
BF16 = jnp.bfloat16

D_MODEL = 1024
D_FF = 4096
DEPTH = 4
CHUNK = 64
RMS_EPS = 1e-6
RET_WIDTH = 512
RET_HEADS = 4
RET_HEAD_DIM = 128
RET_ROPE_BASE = 10000.0
GN_EPS = 1e-5
POOL_WIDTH = 512
POOL_WINDOWS = (2, 4, 8, 16)
POOL_GROUP_DIM = 128
POOL_HALO = 16
AB_IN_WIDTH = 2560
ATT_HEADS = 16
ATT_HEAD_DIM = 64
LEFT_CHUNKS = 8
BAND = (LEFT_CHUNKS + 1) * CHUNK
REL_CLIP = 128
N_REL = 2 * REL_CLIP + 1
N_REL_PAD = 264
NEG_INF = -1e30
KSCALE = RET_HEAD_DIM ** -0.5
QSCALE = ATT_HEAD_DIM ** -0.5

ADAM_LR = 0.001
ADAM_B1 = 0.9
ADAM_B2 = 0.999
ADAM_EPS = 1e-08
ADAM_WD = 0.01
ADAM_STEP = 10

ATT_BLOCK = LEFT_CHUNKS * CHUNK
RET_BLOCK = 512
N_CHIPS = 4
N_DEV = 8
LANE = 128
VMEM_LIMIT = 52 * 1024 * 1024
EPI_ROWS = 256
MESH = pl.DeviceIdType.MESH


def _cp(sem, vmem=VMEM_LIMIT):
    return pltpu.CompilerParams(dimension_semantics=sem, vmem_limit_bytes=vmem)


def _dot(a, b):
    return lax.dot_general(a, b, (((1,), (0,)), ((), ())), preferred_element_type=F32)


def _dot_nt(a, b):
    return lax.dot_general(a, b, (((1,), (1,)), ((), ())), preferred_element_type=F32)


def _dot_tn(a, b):
    return lax.dot_general(a, b, (((0,), (0,)), ((), ())), preferred_element_type=F32)


_ANY = pl.BlockSpec(memory_space=pl.ANY)


class _Rider(NamedTuple):
    operands: tuple
    out_shapes: tuple
    n_sems: int
    start: Callable
    finish: Callable
    aliases: tuple = ()
    relay: Callable = None


def _mm(name, mode, a, b, *, la=None, lb=None, tm=1024, tn=1024, tk=1024, a_fn=None, b_fn=None,
        extras=(), aux=(), sides=(), epi=None, out_dtype=F32, stack=None, rider=None):
    a_parts = list(a) if isinstance(a, (list, tuple)) else [a]
    b_parts = list(b) if isinstance(b, (list, tuple)) else [b]
    na, nbp = len(a_parts), len(b_parts)
    a2, b2 = list(a_parts[0].shape[-2:]), list(b_parts[0].shape[-2:])
    a2[1] *= na
    b2[1] *= nbp
    if mode == "nn":
        (M, K), (K2, N) = a2, b2
    elif mode == "nt":
        (M, K), (N, K2) = a2, b2
    else:
        (K, M), (K2, N) = a2, b2
    assert K == K2, (name, a2, b2)
    tm, tn, tk = min(tm, M), min(tn, N), min(tk, K)
    assert M % tm == 0 and N % tn == 0 and K % tk == 0, (name, M, N, K, tm, tn, tk)
    gm, gn, gk = M // tm, N // tn, K // tk
    fold = mode == "nt" and na > 1 and gk == 1

    def specs(parts, block, idx, lead):
        per = parts[0].shape[-1] // block[1]
        assert parts[0].shape[-1] % block[1] == 0, (name, parts[0].shape, block)
        out = []
        for p in range(len(parts)):
            def f(i, j, k, p=p):
                r, c = idx(i, j, k)
                if len(parts) > 1:
                    c = jnp.clip(c - p * per, 0, per - 1)
                return (r, c) if lead is None else (lead, r, c)
            out.append(pl.BlockSpec(block if lead is None else (None,) + block, f))
        return out, per

    if mode == "nn":
        a_specs, a_per = specs(a_parts, (tm, tk), lambda i, j, k: (i, k), la)
        b_specs, b_per = specs(b_parts, (tk, tn), lambda i, j, k: (k, j), lb)
        a_axis, b_axis, dot = 2, 1, _dot
    elif mode == "nt":
        if fold:
            a_specs, a_per = [pl.BlockSpec((tm, K // na), lambda i, j, k: (i, 0)) for _ in a_parts], 1
        else:
            a_specs, a_per = specs(a_parts, (tm, tk), lambda i, j, k: (i, k), la)
        b_specs, b_per = specs(b_parts, (tn, tk), lambda i, j, k: (j, k), lb)
        a_axis, b_axis, dot = 2, 2, _dot_nt
    else:
        a_specs, a_per = specs(a_parts, (tk, tm), lambda i, j, k: (k, i), la)
        b_specs, b_per = specs(b_parts, (tk, tn), lambda i, j, k: (k, j), lb)
        a_axis, b_axis, dot = 0, 1, _dot_tn
    ex_specs = [pl.BlockSpec((tm, tn), lambda i, j, k: (i, j)) for _ in extras]
    n_ex = len(extras)

    n_aux, n_side = len(aux), len(sides)
    operands = a_parts + b_parts + list(extras) + list(aux)
    in_specs = a_specs + b_specs + ex_specs + [pl.BlockSpec(v.shape, lambda i, j, k, nd=v.ndim: (0,) * nd) for v in aux]
    aliases = {}
    if stack is None:
        out_specs = [pl.BlockSpec((tm, tn), lambda i, j, k: (i, j))]
        out_shapes = [jax.ShapeDtypeStruct((M, N), out_dtype)]
    else:
        n_layers, layer, prev = stack
        out_specs = [pl.BlockSpec((None, tm, tn), lambda i, j, k: (layer, i, j))]
        out_shapes = [jax.ShapeDtypeStruct((n_layers, M, N), out_dtype)]
        if prev is not None:
            aliases = {len(operands): 0}
            operands.append(prev)
            in_specs.append(_ANY)
    for kind, dtype in sides:
        if kind == "tile":
            out_specs.append(pl.BlockSpec((tm, tn), lambda i, j, k: (i, j)))
            out_shapes.append(jax.ShapeDtypeStruct((M, N), dtype))
        else:
            assert gn == 1, name
            out_specs.append(pl.BlockSpec((8, tn), lambda i, j, k: (0, 0)))
            out_shapes.append(jax.ShapeDtypeStruct((8, N), dtype))
    n_prev = len(aliases)
    scratch = [pltpu.VMEM((tm, tn), F32)] if gk > 1 else []
    n_rin = n_rout = 0
    if rider is not None:
        n_rin, n_rout = len(rider.operands), len(rider.out_shapes)
        for src, dst in rider.aliases:
            aliases[len(operands) + src] = 1 + n_side + dst
        operands += list(rider.operands)
        in_specs += [_ANY] * n_rin
        out_specs += [_ANY] * n_rout
        out_shapes += list(rider.out_shapes)
        scratch += [pltpu.SemaphoreType.DMA((rider.n_sems,)), pltpu.SemaphoreType.DMA((rider.n_sems,))]
    assert na == 1 or nbp == 1, name

    def body(*refs):
        a_refs, b_refs = refs[:na], refs[na:na + nbp]
        ex_refs = refs[na + nbp:na + nbp + n_ex + n_aux]
        n_in = na + nbp + n_ex + n_aux + n_prev
        rin = refs[n_in:n_in + n_rin]
        o_ref = refs[n_in + n_rin]
        side_refs = refs[n_in + n_rin + 1:n_in + n_rin + 1 + n_side]
        rout = refs[n_in + n_rin + 1 + n_side:n_in + n_rin + 1 + n_side + n_rout]
        rest = refs[n_in + n_rin + 1 + n_side + n_rout:]
        i, j, k = pl.program_id(0), pl.program_id(1), pl.program_id(2)
        if rider is not None:
            sems = rest[-2:]

            @pl.when(jnp.logical_and(i == 0, jnp.logical_and(j == 0, k == 0)))
            def _():
                rider.start(rin, rout, *sems)

        def finish(acc):
            if epi is None:
                o_ref[...] = acc[...].astype(o_ref.dtype)
                return
            strip = min(tm, EPI_ROWS)
            colsums = [None] * n_side
            for r0 in range(0, tm, strip):
                rows = slice(r0, r0 + strip)
                res = epi(acc[rows, :], *[r[rows, :] for r in ex_refs[:n_ex]], *[r[...] for r in ex_refs[n_ex:]])
                if n_side:
                    res, *side_vals = res
                    for s, ((kind, _), ref, val) in enumerate(zip(sides, side_refs, side_vals)):
                        if kind == "tile":
                            ref[rows, :] = val.astype(ref.dtype)
                        else:
                            colsums[s] = val if colsums[s] is None else colsums[s] + val
                o_ref[rows, :] = res.astype(o_ref.dtype)
            for (kind, _), ref, val in zip(sides, side_refs, colsums):
                if kind == "colsum":
                    @pl.when(i == 0)
                    def _(ref=ref, val=val):
                        ref[...] = val

                    @pl.when(i > 0)
                    def _(ref=ref, val=val):
                        ref[...] += val

                    @pl.when(i == gm - 1)
                    def _(ref=ref):
                        ref[0:1, :] = jnp.sum(ref[...], axis=0, keepdims=True)

        def step(a_ref, b_ref):
            av, bv = a_ref[...], b_ref[...]
            if a_fn is not None:
                av = a_fn(av)
            if b_fn is not None:
                bv = b_fn(bv)
            part = dot(av.astype(BF16), bv.astype(BF16))
            if gk == 1:
                finish(part)
                return
            acc_ref = rest[0]

            @pl.when(k == 0)
            def _():
                acc_ref[...] = part

            @pl.when(k > 0)
            def _():
                acc_ref[...] += part

        if fold:
            kp = K // na
            finish(sum(dot(a_refs[p][...].astype(BF16), b_refs[0][:, p * kp:(p + 1) * kp].astype(BF16))
                       for p in range(na)))
        elif na > 1:
            sel = pl.program_id(a_axis) // a_per
            for p in range(na):
                pl.when(sel == p)(functools.partial(step, a_refs[p], b_refs[0]))
        elif nbp > 1:
            sel = pl.program_id(b_axis) // b_per
            for p in range(nbp):
                pl.when(sel == p)(functools.partial(step, a_refs[0], b_refs[p]))
        else:
            step(a_refs[0], b_refs[0])
        if gk > 1:
            @pl.when(k == gk - 1)
            def _():
                finish(rest[0])

        if rider is not None:
            steps = gm * gn * gk
            step_no = (i * gn + j) * gk + k
            if rider.relay is not None:
                assert steps >= 3, name

                @pl.when(step_no == steps - 2)
                def _():
                    rider.relay(rin, rout, *sems)

            @pl.when(step_no == steps - 1)
            def _():
                rider.finish(rin, rout, *sems)

    sequential = rider is not None or any(kind == "colsum" for kind, _ in sides)
    sem = ("arbitrary",) * 3 if sequential else ("parallel", "parallel", "arbitrary")
    outs = pl.pallas_call(
        body,
        name=name,
        grid=(gm, gn, gk),
        in_specs=in_specs,
        out_specs=out_specs,
        out_shape=out_shapes,
        input_output_aliases=aliases,
        scratch_shapes=scratch,
        compiler_params=_cp(sem),
    )(*operands)
    res = outs[0] if not sides else tuple(outs[:1 + n_side])
    return res if rider is None else (res, list(outs[1 + n_side:]))


def _relu2(u):
    r = jnp.maximum(u, 0)
    return r * r


def _epi_residual(acc, res):
    return acc + res


def _epi_residual_norm(acc, res, g):
    xn = acc + res
    r = lax.rsqrt(jnp.mean(xn * xn, axis=-1, keepdims=True) + RMS_EPS)
    return xn, (xn * r) * g


def _epi_rms_bwd(dh, x, dres, g):
    r = lax.rsqrt(jnp.mean(x * x, axis=-1, keepdims=True) + RMS_EPS)
    xh = x * r
    dxh = dh * g
    dx = dres + r * (dxh - xh * jnp.mean(dxh * xh, axis=-1, keepdims=True))
    return dx, dx, jnp.sum((dh * xh).reshape(dh.shape[0] // 8, 8, dh.shape[1]), axis=0)


def _rms_fwd(name, x, g, rider=None):
    S, D = x.shape
    tq = min(1024, S)

    def body(x_ref, g_ref, o_ref):
        xv = x_ref[...]
        r = lax.rsqrt(jnp.mean(xv * xv, axis=-1, keepdims=True) + RMS_EPS)
        o_ref[...] = ((xv * r) * g_ref[...]).astype(o_ref.dtype)

    grid = (S // tq,)
    body, r_in, r_out, r_shapes, r_sems, aliases = _riding(rider, body, 2, 1, grid)
    outs = pl.pallas_call(
        body,
        name=name,
        grid=grid,
        in_specs=[pl.BlockSpec((tq, D), lambda i: (i, 0)), pl.BlockSpec((1, D), lambda i: (0, 0))] + r_in,
        out_specs=[pl.BlockSpec((tq, D), lambda i: (i, 0))] + r_out,
        out_shape=[jax.ShapeDtypeStruct((S, D), BF16)] + r_shapes,
        input_output_aliases=aliases,
        scratch_shapes=r_sems,
        compiler_params=_cp(("parallel" if rider is None else "arbitrary",)),
    )(x, g, *(rider.operands if rider is not None else ()))
    return outs[0] if rider is None else (outs[0], list(outs[1:]))


def _loss_head(x, g, t):
    S, D = x.shape
    tq = min(512, S)
    n = S // tq

    def body(x_ref, g_ref, t_ref, loss_ref, dx_ref, dxb_ref, dg_ref, lacc_ref, gacc_ref):
        i = pl.program_id(0)
        xv = x_ref[...]
        gv = g_ref[...]
        r = lax.rsqrt(jnp.mean(xv * xv, axis=-1, keepdims=True) + RMS_EPS)
        xh = xv * r
        e = xh * gv - t_ref[...]
        dy = e * (1.0 / D)
        dxh = dy * gv
        dx = r * (dxh - xh * jnp.mean(dxh * xh, axis=-1, keepdims=True))
        dx_ref[...] = dx
        dxb_ref[...] = dx.astype(dxb_ref.dtype)
        lpart = jnp.sum((e * e).reshape(tq // 8, 8, D), axis=0)
        gpart = jnp.sum((dy * xh).reshape(tq // 8, 8, D), axis=0)

        @pl.when(i == 0)
        def _():
            lacc_ref[...] = lpart
            gacc_ref[...] = gpart

        @pl.when(i > 0)
        def _():
            lacc_ref[...] += lpart
            gacc_ref[...] += gpart

        @pl.when(i == n - 1)
        def _():
            dg_ref[...] = jnp.sum(gacc_ref[...], axis=0, keepdims=True)
            tot = jnp.sum(jnp.sum(lacc_ref[...], axis=0, keepdims=True), axis=1, keepdims=True)
            loss_ref[...] = jnp.broadcast_to(tot * (0.5 / D), (1, LANE))

    return pl.pallas_call(
        body,
        name="loss_head",
        grid=(n,),
        in_specs=[pl.BlockSpec((tq, D), lambda i: (i, 0)), pl.BlockSpec((1, D), lambda i: (0, 0)),
                  pl.BlockSpec((tq, D), lambda i: (i, 0))],
        out_specs=[pl.BlockSpec((1, LANE), lambda i: (0, 0)), pl.BlockSpec((tq, D), lambda i: (i, 0)),
                   pl.BlockSpec((tq, D), lambda i: (i, 0)), pl.BlockSpec((1, D), lambda i: (0, 0))],
        out_shape=[jax.ShapeDtypeStruct((1, LANE), F32), jax.ShapeDtypeStruct((S, D), F32),
                   jax.ShapeDtypeStruct((S, D), BF16), jax.ShapeDtypeStruct((1, D), F32)],
        scratch_shapes=[pltpu.VMEM((8, D), F32), pltpu.VMEM((8, D), F32)],
        compiler_params=_cp(("arbitrary",)),
    )(x, g, t)


def _ret_tables(S):
    T = min(RET_BLOCK, S)
    inv_freq = 1.0 / (RET_ROPE_BASE ** jnp.linspace(0.0, 1.0, RET_HEAD_DIM // 2, dtype=F32))
    ang = jnp.arange(S, dtype=F32)[:, None] * inv_freq[None, :]
    cos, sin = jnp.cos(ang), jnp.sin(ang)
    cosf = jnp.repeat(cos, 2, axis=-1)
    sins = jnp.stack([-sin, sin], axis=-1).reshape(S, RET_HEAD_DIM)
    log_g = np.log1p(-np.power(2.0, -5.0 - np.arange(RET_HEADS, dtype=np.float64)))
    pos = np.arange(T, dtype=np.float64)
    diff = pos[:, None] - pos[None, :]
    same = (pos[:, None] // CHUNK) == (pos[None, :] // CHUNK)
    seen = same | (diff > 0)
    dmat = np.where(seen[None], np.exp(np.abs(diff)[None] * log_g[:, None, None]), 0.0)
    aq = np.exp((pos[None, :] + 1.0) * log_g[:, None])
    ak = np.exp((T - 1.0 - pos[None, :]) * log_g[:, None])
    lam = np.exp(T * log_g)
    bc = lambda v: jnp.asarray(np.broadcast_to(v[..., None], v.shape + (LANE,)), F32)
    return dict(cos=cosf, sin=sins, dmat=jnp.asarray(dmat, F32), aq=bc(aq), ak=bc(ak),
                lam=jnp.asarray(np.broadcast_to(lam[:, None, None], (RET_HEADS, 1, LANE)), F32))


def _rot(x, cos, sin_s, even):
    sw = jnp.where(even, pltpu.roll(x, LANE - 1, 1), pltpu.roll(x, 1, 1))
    return x * cos + sw * sin_s


def _rot_t(dy, cos, sin_s, even):
    t = dy * sin_s
    return dy * cos + jnp.where(even, pltpu.roll(t, LANE - 1, 1), pltpu.roll(t, 1, 1))


def _ret_specs(T, rev_nb=None):
    blk = (lambda b: b) if rev_nb is None else (lambda b: rev_nb - 1 - b)
    whole = lambda shape: pl.BlockSpec(shape, lambda b: (0,) * len(shape))
    specs = [pl.BlockSpec((T, AB_IN_WIDTH), lambda b: (blk(b), 0)),
             pl.BlockSpec((T, LANE), lambda b: (blk(b), 0)),
             pl.BlockSpec((T, LANE), lambda b: (blk(b), 0)),
             whole((RET_HEADS, T, T)), whole((RET_HEADS, T, LANE)), whole((RET_HEADS, T, LANE)),
             whole((RET_HEADS, 1, LANE)), whole((1, RET_WIDTH))]
    return specs, blk


def _head_views(h, z_ref, tabs, token_refs, head_refs):
    zs = [z_ref.at[:, (o * RET_HEADS + h) * LANE:(o * RET_HEADS + h + 1) * LANE] for o in range(4)]
    hs = slice(h * LANE, (h + 1) * LANE)
    return zs, [t.at[h] for t in tabs], [r.at[:, hs] for r in token_refs], [r.at[h] for r in head_refs]


def _ret_fwd(name, z, tb, gain):
    S = z.shape[0]
    T = min(RET_BLOCK, S)
    nb = S // T
    specs, blk = _ret_specs(T)

    def body(z_ref, cos_r, sin_r, d_all, aq_all, ak_all, lam_all, gain_all, cat_all, opre_all, st_all, state_all):
        @pl.when(pl.program_id(0) == 0)
        def _():
            state_all[...] = jnp.zeros_like(state_all)

        for h in range(RET_HEADS):
            zs, tabs, toks, heads = _head_views(h, z_ref, (d_all, aq_all, ak_all, lam_all),
                                                (gain_all, cat_all, opre_all), (st_all, state_all))
            head(*zs, cos_r, sin_r, *tabs, *toks, *heads)

    def head(zq, zk, zv, zg, cos_r, sin_r, d_r, aq_r, ak_r, lam_r, gain_r, ret_o, opre_o, st_o, state):
        even = (lax.broadcasted_iota(jnp.int32, (T, LANE), 1) & 1) == 0
        c, s = cos_r[...], sin_r[...]
        q = _rot(zq[...].astype(F32), c, s, even)
        k = _rot(zk[...].astype(F32), c, s, even) * KSCALE
        qb, kb, vb = q.astype(BF16), k.astype(BF16), zv[...].astype(BF16)
        p = (_dot_nt(qb, kb) * d_r[...]).astype(BF16)
        st = state[...]
        st_o[...] = st
        o = _dot(p, vb) + _dot((q * aq_r[...]).astype(BF16), st.astype(BF16))
        state[...] = st * lam_r[...] + _dot_tn((k * ak_r[...]).astype(BF16), vb)
        opre_o[...] = o
        mu = jnp.mean(o, axis=-1, keepdims=True)
        d = o - mu
        y = d * lax.rsqrt(jnp.mean(d * d, axis=-1, keepdims=True) + GN_EPS)
        g = zg[...].astype(F32)
        ret_o[...] = ((g * jax.nn.sigmoid(g)) * (y * gain_r[...])).astype(ret_o.dtype)

    out_blk = pl.BlockSpec((T, RET_WIDTH), lambda b: (b, 0))
    return pl.pallas_call(
        body,
        name=name,
        grid=(nb,),
        in_specs=specs,
        out_specs=[out_blk, out_blk, pl.BlockSpec((RET_HEADS, None, LANE, LANE), lambda b: (0, b, 0, 0))],
        out_shape=[jax.ShapeDtypeStruct((S, D_MODEL), BF16), jax.ShapeDtypeStruct((S, RET_WIDTH), F32),
                   jax.ShapeDtypeStruct((RET_HEADS, nb, LANE, LANE), F32)],
        scratch_shapes=[pltpu.VMEM((RET_HEADS, LANE, LANE), F32)],
        compiler_params=_cp(("arbitrary",)),
    )(z, tb["cos"], tb["sin"], tb["dmat"], tb["aq"], tb["ak"], tb["lam"], gain)


def _ret_bwd(name, z, tb, gain, opre, states, dcat, rider=None):
    S = z.shape[0]
    T = min(RET_BLOCK, S)
    nb = S // T
    specs, blk = _ret_specs(T, rev_nb=nb)
    tok = pl.BlockSpec((T, RET_WIDTH), lambda b: (blk(b), 0))

    def body(z_ref, cos_r, sin_r, d_all, aq_all, ak_all, lam_all, gain_all, opre_all, st_all, dret_all,
             dz_ref, dgain_all, dstate_all):
        @pl.when(pl.program_id(0) == 0)
        def _():
            dstate_all[...] = jnp.zeros_like(dstate_all)
            dgain_all[...] = jnp.zeros_like(dgain_all)

        for h in range(RET_HEADS):
            zs, tabs, toks, heads = _head_views(h, z_ref, (d_all, aq_all, ak_all, lam_all),
                                                (gain_all, opre_all, dret_all, dgain_all), (st_all, dstate_all))
            dzs, _, _, _ = _head_views(h, dz_ref, (), (), ())
            gain_r, opre_r, dret_r, dgain_o = toks
            head(*zs, cos_r, sin_r, *tabs, gain_r, opre_r, heads[0], dret_r, *dzs, dgain_o, heads[1])

    def head(zq, zk, zv, zg, cos_r, sin_r, d_r, aq_r, ak_r, lam_r, gain_r, opre_r, st_r, dret_r,
             dq_o, dk_o, dv_o, dg_o, dgain_o, dstate):
        even = (lax.broadcasted_iota(jnp.int32, (T, LANE), 1) & 1) == 0
        c, s = cos_r[...], sin_r[...]
        aq, ak, dm = aq_r[...], ak_r[...], d_r[...]
        q = _rot(zq[...].astype(F32), c, s, even)
        k = _rot(zk[...].astype(F32), c, s, even) * KSCALE
        qb, kb, vb = q.astype(BF16), k.astype(BF16), zv[...].astype(BF16)
        pb = (_dot_nt(qb, kb) * dm).astype(BF16)
        g = zg[...].astype(F32)
        sig = jax.nn.sigmoid(g)
        o = opre_r[...]
        mu = jnp.mean(o, axis=-1, keepdims=True)
        d = o - mu
        rstd = lax.rsqrt(jnp.mean(d * d, axis=-1, keepdims=True) + GN_EPS)
        y = d * rstd
        gain_v = gain_r[...]
        dret = dret_r[...].astype(F32)
        dyg = dret * (g * sig)
        dg_o[...] = (dret * (y * gain_v) * (sig * (1.0 + g * (1.0 - sig)))).astype(dg_o.dtype)
        dgain_o[...] += jnp.sum(dyg * y, axis=0, keepdims=True)
        dy = dyg * gain_v
        do = rstd * (dy - jnp.mean(dy, axis=-1, keepdims=True) - y * jnp.mean(dy * y, axis=-1, keepdims=True))
        dob = do.astype(BF16)
        stb = st_r[...].astype(BF16)
        dsn = dstate[...]
        dsnb = dsn.astype(BF16)
        dpb = (_dot_nt(dob, vb) * dm).astype(BF16)
        dq = _dot(dpb, kb) + _dot_nt(dob, stb) * aq
        dk = _dot_tn(dpb, qb) + _dot_nt(vb, dsnb) * ak
        dv = _dot_tn(pb, dob) + _dot((k * ak).astype(BF16), dsnb)
        dstate[...] = dsn * lam_r[...] + _dot_tn((q * aq).astype(BF16), dob)
        dq_o[...] = _rot_t(dq, c, s, even).astype(dq_o.dtype)
        dk_o[...] = _rot_t(dk * KSCALE, c, s, even).astype(dk_o.dtype)
        dv_o[...] = dv.astype(dv_o.dtype)

    grid = (nb,)
    body, r_in, r_out, r_shapes, r_sems, aliases = _riding(rider, body, 11, 2, grid)
    outs = pl.pallas_call(
        body,
        name=name,
        grid=grid,
        in_specs=specs + [tok, pl.BlockSpec((RET_HEADS, None, LANE, LANE), lambda b: (0, blk(b), 0, 0)), tok] + r_in,
        out_specs=[pl.BlockSpec((T, 4 * RET_WIDTH), lambda b: (blk(b), 0)),
                   pl.BlockSpec((1, RET_WIDTH), lambda b: (0, 0))] + r_out,
        out_shape=[jax.ShapeDtypeStruct((S, AB_IN_WIDTH), BF16), jax.ShapeDtypeStruct((1, RET_WIDTH), F32)] + r_shapes,
        input_output_aliases=aliases,
        scratch_shapes=[pltpu.VMEM((RET_HEADS, LANE, LANE), F32)] + r_sems,
        compiler_params=_cp(("arbitrary",)),
    )(z, tb["cos"], tb["sin"], tb["dmat"], tb["aq"], tb["ak"], tb["lam"], gain, opre, states, dcat,
      *(rider.operands if rider is not None else ()))
    return tuple(outs[:2]) if rider is None else (tuple(outs[:2]), list(outs[2:]))


def _pool_counts(t0, rows):
    t = t0 + lax.broadcasted_iota(jnp.int32, (rows, POOL_WIDTH), 0)
    grp = lax.broadcasted_iota(jnp.int32, (rows, POOL_WIDTH), 1) >> 7
    win = jnp.where(grp == 0, POOL_WINDOWS[0], jnp.where(grp == 1, POOL_WINDOWS[1],
                    jnp.where(grp == 2, POOL_WINDOWS[2], POOL_WINDOWS[3])))
    return jnp.maximum(jnp.minimum(t + 1, win), 1).astype(F32), grp


def _window_sums(ext, grp, sign):
    n = ext.shape[0]
    sh = lambda v, k: pltpu.roll(v, k % n if sign > 0 else (n - k) % n, 0)
    s2 = ext + sh(ext, 1)
    s4 = s2 + sh(s2, 2)
    s8 = s4 + sh(s4, 4)
    s16 = s8 + sh(s8, 8)
    return jnp.where(grp == 0, s2, jnp.where(grp == 1, s4, jnp.where(grp == 2, s8, s16)))


def _pool_fwd(name, z, w_pool, scale, cat):
    S = z.shape[0]
    T = min(512, S)
    nb = S // T
    pcol = AB_IN_WIDTH // POOL_WIDTH - 1
    hb = T // POOL_HALO

    def body(p_ref, halo_ref, w_ref, sc_ref, cat_in, out_ref, pooled_ref):
        b = pl.program_id(0)
        cur = p_ref[...].astype(F32)
        halo = jnp.where(b > 0, halo_ref[...].astype(F32), 0.0)
        ext = jnp.concatenate([halo, cur], axis=0)
        cnt, grp = _pool_counts(b * T - POOL_HALO, T + POOL_HALO)
        sums = _window_sums(ext, grp, +1)
        pooled = (sums / cnt)[POOL_HALO:] - cur
        pb = pooled.astype(BF16)
        pooled_ref[...] = pb
        for gi in range(len(POOL_WINDOWS)):
            cs = slice(gi * POOL_GROUP_DIM, (gi + 1) * POOL_GROUP_DIM)
            mixed = _dot(pb[:, cs], w_ref[gi].astype(BF16))
            out_ref[:, cs] = (mixed * sc_ref[:, cs]).astype(out_ref.dtype)

    return pl.pallas_call(
        body,
        name=name,
        grid=(nb,),
        in_specs=[pl.BlockSpec((T, POOL_WIDTH), lambda b: (b, pcol)),
                  pl.BlockSpec((POOL_HALO, POOL_WIDTH), lambda b: (jnp.maximum(b * hb - 1, 0), pcol)),
                  pl.BlockSpec((4, POOL_GROUP_DIM, POOL_GROUP_DIM), lambda b: (0, 0, 0)),
                  pl.BlockSpec((1, POOL_WIDTH), lambda b: (0, 0)), _ANY],
        out_specs=[pl.BlockSpec((T, POOL_WIDTH), lambda b: (b, 1)), pl.BlockSpec((T, POOL_WIDTH), lambda b: (b, 0))],
        out_shape=[jax.ShapeDtypeStruct(cat.shape, cat.dtype), jax.ShapeDtypeStruct((S, POOL_WIDTH), BF16)],
        input_output_aliases={4: 0},
        compiler_params=_cp(("parallel",)),
    )(z, z, w_pool, scale, cat)


def _pool_bwd(name, pooled, w_pool, scale, dcat, dz):
    S = pooled.shape[0]
    T = min(512, S)
    nb = S // T
    hb = T // POOL_HALO
    last_h = S // POOL_HALO - 1
    pcol = AB_IN_WIDTH // POOL_WIDTH - 1

    def body(d_ref, dn_ref, pooled_ref, w_ref, sc_ref, dz_in, dp_ref, dw_ref, dsc_ref):
        b = pl.program_id(0)

        @pl.when(b == 0)
        def _():
            dw_ref[...] = jnp.zeros_like(dw_ref)
            dsc_ref[...] = jnp.zeros_like(dsc_ref)

        sc = sc_ref[...]
        dout = d_ref[...].astype(F32)
        dnext = jnp.where(b < nb - 1, dn_ref[...].astype(F32), 0.0)
        dmix = jnp.concatenate([dout, dnext], axis=0) * sc
        dmb = dmix.astype(BF16)
        pb = pooled_ref[...]
        dpooled = []
        for gi in range(len(POOL_WINDOWS)):
            cs = slice(gi * POOL_GROUP_DIM, (gi + 1) * POOL_GROUP_DIM)
            wb = w_ref[gi].astype(BF16)
            dpooled.append(_dot_nt(dmb[:, cs], wb))
            dw_ref[gi] += _dot_tn(pb[:, cs], dmb[:T, cs])
            mixed = _dot(pb[:, cs], wb)
            dsc_ref[:, cs] += jnp.sum(dout[:, cs] * mixed, axis=0, keepdims=True)
        dpl = jnp.concatenate(dpooled, axis=1)
        cnt, grp = _pool_counts(b * T, T + POOL_HALO)
        sums = _window_sums(dpl / cnt, grp, -1)
        dp_ref[...] = (sums[:T] - dpl[:T]).astype(dp_ref.dtype)

    return pl.pallas_call(
        body,
        name=name,
        grid=(nb,),
        in_specs=[pl.BlockSpec((T, POOL_WIDTH), lambda b: (b, 1)),
                  pl.BlockSpec((POOL_HALO, POOL_WIDTH), lambda b: (jnp.minimum((b + 1) * hb, last_h), 1)),
                  pl.BlockSpec((T, POOL_WIDTH), lambda b: (b, 0)),
                  pl.BlockSpec((4, POOL_GROUP_DIM, POOL_GROUP_DIM), lambda b: (0, 0, 0)),
                  pl.BlockSpec((1, POOL_WIDTH), lambda b: (0, 0)), _ANY],
        out_specs=[pl.BlockSpec((T, POOL_WIDTH), lambda b: (b, pcol)),
                   pl.BlockSpec((4, POOL_GROUP_DIM, POOL_GROUP_DIM), lambda b: (0, 0, 0)),
                   pl.BlockSpec((1, POOL_WIDTH), lambda b: (0, 0))],
        out_shape=[jax.ShapeDtypeStruct(dz.shape, dz.dtype),
                   jax.ShapeDtypeStruct((4, POOL_GROUP_DIM, POOL_GROUP_DIM), F32),
                   jax.ShapeDtypeStruct((1, POOL_WIDTH), F32)],
        input_output_aliases={5: 0},
        compiler_params=_cp(("arbitrary",)),
    )(dcat, dcat, pooled, w_pool, scale, dz)


ATT_STRIP = 32
ATT_Q = 256
ATT_W = ATT_Q + LEFT_CHUNKS * CHUNK


def _rel_index():
    j = np.arange(ATT_W)
    rel = np.clip(LEFT_CHUNKS * CHUNK - j, -REL_CLIP, REL_CLIP) + REL_CLIP
    fwd = np.where(j < BAND, rel, N_REL)
    bwd = np.where(j <= ATT_W - CHUNK, fwd, 2 * REL_CLIP)
    return tuple(jnp.asarray(v.reshape(1, ATT_W), jnp.int32) for v in (fwd, bwd))


def _bias_table(name, rel_bias, rel_idx):
    rb = jnp.concatenate([rel_bias, jnp.full((ATT_HEADS, 1), NEG_INF, F32),
                          jnp.zeros((ATT_HEADS, N_REL_PAD - N_REL - 1), F32)], axis=1)

    def body(rb_ref, idx_ref, o_ref, row0_ref):
        r = lax.broadcasted_iota(jnp.int32, (N_REL_PAD, ATT_W), 0)
        onehot = (r == idx_ref[...]).astype(F32)
        row0_ref[...] = jnp.dot(rb_ref[...], onehot, precision=lax.Precision.HIGHEST, preferred_element_type=F32)
        col = lax.broadcasted_iota(jnp.int32, (CHUNK, ATT_W), 1)
        row = lax.broadcasted_iota(jnp.int32, (CHUNK, ATT_W), 0)
        for h in range(ATT_HEADS):
            same = jnp.broadcast_to(row0_ref[pl.ds(h, 1), :], (CHUNK, ATT_W))
            turned = pltpu.roll(same, 0, 1, stride=1, stride_axis=0)
            o_ref[h] = jnp.where(col >= BAND, NEG_INF, jnp.where(col < row, same, turned))

    return pl.pallas_call(
        body,
        name=name,
        out_shape=jax.ShapeDtypeStruct((ATT_HEADS, CHUNK, ATT_W), F32),
        scratch_shapes=[pltpu.VMEM((ATT_HEADS, ATT_W), F32)],
        compiler_params=pltpu.CompilerParams(vmem_limit_bytes=VMEM_LIMIT),
    )(rb, rel_idx)


def _bias_grad(name, dband, rel_idx):
    def body(d_ref, idx_ref, o_ref, sums_ref):
        row = lax.broadcasted_iota(jnp.int32, (CHUNK, ATT_W), 0)
        for h in range(ATT_HEADS):
            back = d_ref[h]
            for bit in range(CHUNK.bit_length() - 1):
                back = jnp.where(((row >> bit) & 1) == 1, pltpu.roll(back, ATT_W - (1 << bit), 1), back)
            sums_ref[pl.ds(h, 1), :] = jnp.sum(back, axis=0, keepdims=True)
        r = lax.broadcasted_iota(jnp.int32, (N_REL_PAD, ATT_W), 0)
        onehot = (r == idx_ref[...]).astype(F32)
        o_ref[...] = lax.dot_general(sums_ref[...], onehot, (((1,), (1,)), ((), ())),
                                     precision=lax.Precision.HIGHEST, preferred_element_type=F32)

    out = pl.pallas_call(
        body,
        name=name,
        out_shape=jax.ShapeDtypeStruct((ATT_HEADS, N_REL_PAD), F32),
        scratch_shapes=[pltpu.VMEM((ATT_HEADS, ATT_W), F32)],
        compiler_params=pltpu.CompilerParams(vmem_limit_bytes=VMEM_LIMIT),
    )(dband, rel_idx)
    return out[:, :N_REL]


def _attn_unit(q_ref, kw_ref, bias_ref, e, u, lane):
    mine = (lane < ATT_HEAD_DIM) if e == 0 else (lane >= ATT_HEAD_DIM)
    qm = jnp.where(mine, q_ref[u * ATT_Q:(u + 1) * ATT_Q, :] * QSCALE, 0)
    kw = kw_ref[u * ATT_Q:u * ATT_Q + ATT_W, :]
    s = _dot_nt(qm, kw) + bias_ref[u, e]
    p = jnp.exp(s - jnp.max(s, axis=-1, keepdims=True))
    return p, 1.0 / jnp.sum(p, axis=-1, keepdims=True), qm, kw, mine


def _attn_in_specs(nb):
    T = ATT_BLOCK
    hp = ATT_HEADS // 2
    cur = lambda off: pl.BlockSpec((T, LANE), lambda h, b: (jnp.minimum(b, nb - 1), off + h))
    prev = lambda off: pl.BlockSpec((T, LANE), lambda h, b: (jnp.clip(b - 1, 0, nb - 1), off + h))
    return [cur(0), prev(hp), cur(hp), prev(2 * hp), cur(2 * hp),
            pl.BlockSpec((None, 2, CHUNK, ATT_W), lambda h, b: (h, 0, 0, 0))]


def _spread_bias(bias_ref, bm_ref, block):
    col = lax.broadcasted_iota(jnp.int32, (CHUNK, ATT_W), 1)
    for first in (True, False):
        @pl.when(block == (0 if first else 1))
        def _(first=first):
            for u in range(ATT_BLOCK // ATT_Q):
                for e in range(2):
                    for j in range(ATT_Q // CHUNK):
                        rows = pltpu.roll(bias_ref[e], j * CHUNK, 1)
                        if first:
                            rows = jnp.where(col >= ATT_BLOCK - u * ATT_Q, rows, NEG_INF)
                        bm_ref[u, e, j * CHUNK:(j + 1) * CHUNK, :] = rows


def _riding(rider, body, n_in, n_out, grid):
    if rider is None:
        return body, [], [], [], [], {}
    n_rin, n_rout = len(rider.operands), len(rider.out_shapes)
    steps = int(np.prod(grid))

    def riding(*refs):
        ins, rin = refs[:n_in], refs[n_in:n_in + n_rin]
        outs = refs[n_in + n_rin:n_in + n_rin + n_out]
        rout = refs[n_in + n_rin + n_out:n_in + n_rin + n_out + n_rout]
        rest = refs[n_in + n_rin + n_out + n_rout:]
        scratch, sems = rest[:-2], rest[-2:]
        step_no = pl.program_id(0)
        for axis in range(1, len(grid)):
            step_no = step_no * grid[axis] + pl.program_id(axis)

        @pl.when(step_no == 0)
        def _():
            rider.start(rin, rout, *sems)

        body(*ins, *outs, *scratch)
        if rider.relay is not None:
            assert steps >= 3

            @pl.when(step_no == steps - 2)
            def _():
                rider.relay(rin, rout, *sems)

        @pl.when(step_no == steps - 1)
        def _():
            rider.finish(rin, rout, *sems)

    sems = [pltpu.SemaphoreType.DMA((rider.n_sems,)), pltpu.SemaphoreType.DMA((rider.n_sems,))]
    aliases = {n_in + src: n_out + dst for src, dst in rider.aliases}
    return riding, [_ANY] * n_rin, [_ANY] * n_rout, list(rider.out_shapes), sems, aliases


def _attn_fwd(name, qkv, bias, rider=None):
    S = qkv.shape[0]
    T = ATT_BLOCK
    nb = S // T

    def body(q_ref, kp_ref, kc_ref, vp_ref, vc_ref, band_ref, o_ref, kw_ref, vw_ref, bias_ref, s_ref, p_ref, inv_ref):
        _spread_bias(band_ref, bias_ref, pl.program_id(1))
        kw_ref[0:T] = kp_ref[...]
        kw_ref[T:2 * T] = kc_ref[...]
        vw_ref[0:T] = vp_ref[...]
        vw_ref[T:2 * T] = vc_ref[...]
        lane = lax.broadcasted_iota(jnp.int32, (ATT_Q, LANE), 1)
        for u in range(T // ATT_Q):
            vw = vw_ref[u * ATT_Q:u * ATT_Q + ATT_W, :]
            kw = kw_ref[u * ATT_Q:u * ATT_Q + ATT_W, :]
            outs = []
            for e in range(2):
                mine = (lane < ATT_HEAD_DIM) if e == 0 else (lane >= ATT_HEAD_DIM)
                qm = jnp.where(mine, q_ref[u * ATT_Q:(u + 1) * ATT_Q, :] * QSCALE, 0)
                s_ref[e] = _dot_nt(qm, kw)
                for r in range(ATT_Q // ATT_STRIP):
                    rows = slice(r * ATT_STRIP, (r + 1) * ATT_STRIP)
                    s = s_ref[e, rows, :] + bias_ref[u, e, rows, :]
                    p = jnp.exp(s - jnp.max(s, axis=-1, keepdims=True))
                    inv_ref[e, rows, :] = jnp.broadcast_to(1.0 / jnp.sum(p, axis=-1, keepdims=True), (ATT_STRIP, LANE))
                    p_ref[e, rows, :] = p.astype(BF16)
                outs.append(_dot(p_ref[e], vw) * inv_ref[e])
            o_ref[u * ATT_Q:(u + 1) * ATT_Q, :] = jnp.where(lane < ATT_HEAD_DIM, outs[0], outs[1]).astype(o_ref.dtype)

    grid = (ATT_HEADS // 2, nb)
    body, r_in, r_out, r_shapes, r_sems, aliases = _riding(rider, body, 6, 1, grid)
    outs = pl.pallas_call(
        body,
        name=name,
        grid=grid,
        in_specs=_attn_in_specs(nb) + r_in,
        out_specs=[pl.BlockSpec((T, LANE), lambda h, b: (b, h))] + r_out,
        out_shape=[jax.ShapeDtypeStruct((S, D_MODEL), BF16)] + r_shapes,
        input_output_aliases=aliases,
        scratch_shapes=[pltpu.VMEM((2 * T, LANE), BF16), pltpu.VMEM((2 * T, LANE), BF16),
                        pltpu.VMEM((T // ATT_Q, 2, ATT_Q, ATT_W), F32), pltpu.VMEM((2, ATT_Q, ATT_W), F32),
                        pltpu.VMEM((2, ATT_Q, ATT_W), BF16), pltpu.VMEM((2, ATT_Q, LANE), F32)] + r_sems,
        compiler_params=_cp(("parallel" if rider is None else "arbitrary", "arbitrary")),
    )(qkv, qkv, qkv, qkv, qkv, bias, *(rider.operands if rider is not None else ()))
    return outs[0] if rider is None else (outs[0], list(outs[1:]))


def _attn_bwd(name, qkv, bias, do, rider=None):
    S = qkv.shape[0]
    T = ATT_BLOCK
    nb = S // T

    def body(q_ref, kp_ref, kc_ref, vp_ref, vc_ref, band_ref, do_ref,
             dq_ref, dk_ref, dv_ref, dband_ref, kw_ref, vw_ref, dkw_ref, dvw_ref, bias_ref, dbias_ref):
        b = pl.program_id(1)

        _spread_bias(band_ref, bias_ref, b)

        @pl.when(b == 0)
        def _():
            dbias_ref[...] = jnp.zeros_like(dbias_ref)
            dkw_ref[:, T:2 * T] = jnp.zeros((LANE, T), F32)
            dvw_ref[:, T:2 * T] = jnp.zeros((LANE, T), F32)

        dkw_ref[:, 0:T] = dkw_ref[:, T:2 * T]
        dvw_ref[:, 0:T] = dvw_ref[:, T:2 * T]
        dkw_ref[:, T:2 * T] = jnp.zeros((LANE, T), F32)
        dvw_ref[:, T:2 * T] = jnp.zeros((LANE, T), F32)

        @pl.when(b < nb)
        def _():
            kw_ref[0:T] = kp_ref[...]
            kw_ref[T:2 * T] = kc_ref[...]
            vw_ref[0:T] = vp_ref[...]
            vw_ref[T:2 * T] = vc_ref[...]
            lane = lax.broadcasted_iota(jnp.int32, (ATT_Q, LANE), 1)
            for u in range(T // ATT_Q):
                rows = slice(u * ATT_Q, (u + 1) * ATT_Q)
                win = slice(u * ATT_Q, u * ATT_Q + ATT_W)
                vw = vw_ref[win, :]
                do2 = do_ref[rows, :]
                dqs, dk, dv = [], None, None
                for e in range(2):
                    p, inv, qm, kw, mine = _attn_unit(q_ref, kw_ref, bias_ref, e, u, lane)
                    dom = jnp.where(mine, do2, 0)
                    dp = _dot_nt(dom, vw)
                    delta = jnp.sum(p * dp, axis=-1, keepdims=True) * inv
                    ds = p * ((dp - delta) * inv)
                    dbias_ref[e] += ds
                    dsb = ds.astype(BF16)
                    dqs.append(_dot(dsb, kw))
                    dk_e = _dot_tn(qm, dsb)
                    dv_e = _dot_tn((dom * inv).astype(BF16), p.astype(BF16))
                    dk = dk_e if dk is None else dk + dk_e
                    dv = dv_e if dv is None else dv + dv_e
                dq_ref[rows, :] = (jnp.where(lane < ATT_HEAD_DIM, dqs[0], dqs[1]) * QSCALE).astype(dq_ref.dtype)
                dkw_ref[:, win] += dk
                dvw_ref[:, win] += dv

        @pl.when(b > 0)
        def _():
            dk_ref[...] = dkw_ref[:, 0:T].T.astype(dk_ref.dtype)
            dv_ref[...] = dvw_ref[:, 0:T].T.astype(dv_ref.dtype)

        @pl.when(b == nb)
        def _():
            for e in range(2):
                acc = dbias_ref[e, 0:CHUNK, :]
                for j in range(1, ATT_Q // CHUNK):
                    acc = acc + pltpu.roll(dbias_ref[e, j * CHUNK:(j + 1) * CHUNK, :], ATT_W - j * CHUNK, 1)
                dband_ref[e] = acc

    tok = jax.ShapeDtypeStruct((S, D_MODEL), BF16)
    prev_out = pl.BlockSpec((T, LANE), lambda h, b: (jnp.maximum(b - 1, 0), h))
    grid = (ATT_HEADS // 2, nb + 1)
    body, r_in, r_out, r_shapes, r_sems, aliases = _riding(rider, body, 7, 4, grid)
    outs = pl.pallas_call(
        body,
        name=name,
        grid=grid,
        in_specs=_attn_in_specs(nb) + [pl.BlockSpec((T, LANE), lambda h, b: (jnp.minimum(b, nb - 1), h))] + r_in,
        out_specs=[pl.BlockSpec((T, LANE), lambda h, b: (jnp.minimum(b, nb - 1), h)), prev_out, prev_out,
                   pl.BlockSpec((None, 2, CHUNK, ATT_W), lambda h, b: (h, 0, 0, 0))] + r_out,
        out_shape=[tok, tok, tok, jax.ShapeDtypeStruct((ATT_HEADS // 2, 2, CHUNK, ATT_W), F32)] + r_shapes,
        input_output_aliases=aliases,
        scratch_shapes=[pltpu.VMEM((2 * T, LANE), BF16), pltpu.VMEM((2 * T, LANE), BF16),
                        pltpu.VMEM((LANE, 2 * T), F32), pltpu.VMEM((LANE, 2 * T), F32),
                        pltpu.VMEM((T // ATT_Q, 2, ATT_Q, ATT_W), F32), pltpu.VMEM((2, ATT_Q, ATT_W), F32)] + r_sems,
        compiler_params=_cp(("parallel" if rider is None else "arbitrary", "arbitrary")),
    )(qkv, qkv, qkv, qkv, qkv, bias, do, *(rider.operands if rider is not None else ()))
    return tuple(outs[:4]) if rider is None else (tuple(outs[:4]), list(outs[4:]))


def _local_step(x, target, small, W):
    S = x.shape[0]
    tb = _ret_tables(S)
    rel_fwd, rel_bwd = _rel_index()
    saved = []
    normed = (("tile", BF16),)
    deep = dict(tm=512, tk=D_FF)
    h = W.hosted("mix_norm_fwd0", lambda rider: _rms_fwd("mix_norm_fwd0", x, small["mix_norm"][0:1], rider=rider))
    for layer in range(DEPTH):
        i = layer // 2
        st = {"x_in": x, "h": h}
        g_ffn = small["ffn_norm"][layer:layer + 1]
        if layer % 2 == 0:
            z = W.mm(f"ab_in_fwd{layer}", "nn", h, W.get("ab_w_in", i), tm=2048, tn=640, out_dtype=BF16)
            gain = small["ab_gn_gain"][i:i + 1]
            cat, opre, states = _ret_fwd(f"ret_fwd{layer}", z, tb, gain)
            cat, pooled = _pool_fwd(f"pool_fwd{layer}", z, small["ab_w_pool"][i], small["ab_pool_scale"][i:i + 1], cat)
            st.update(z=z, opre=opre, states=states, pooled=pooled, cat=cat)
            x, hn = W.mm(f"ab_out_fwd{layer}", "nn", cat, W.get("ab_w_out", i), extras=(x,), aux=(g_ffn,), sides=normed,
                         epi=_epi_residual_norm)
        else:
            qkv = W.mm(f"qkv_fwd{layer}", "nn", h, W.get("c_w_qkv", i), tm=2048, out_dtype=BF16)
            bias = _bias_table(f"bias_table{layer}", small["c_rel_bias"][i], rel_fwd)
            bias = bias.reshape(ATT_HEADS // 2, 2, CHUNK, ATT_W)
            att = W.hosted(f"attn_fwd{layer}", lambda rider: _attn_fwd(f"attn_fwd{layer}", qkv, bias, rider=rider))
            st.update(qkv=qkv, bias=bias, att=att)
            x, hn = W.mm(f"c_out_fwd{layer}", "nn", att, W.get("c_w_out", i), extras=(x,), aux=(g_ffn,), sides=normed,
                         epi=_epi_residual_norm)
        st["x_mid"] = x
        u = W.mm(f"ffn_in_fwd{layer}", "nn", hn, W.get("w_ffn_in", layer), out_dtype=BF16, tm=2048)
        if layer + 1 < DEPTH:
            x, h = W.mm(f"ffn_out_fwd{layer}", "nn", u, W.get("w_ffn_out", layer), a_fn=_relu2, extras=(x,),
                        aux=(small["mix_norm"][layer + 1:layer + 2],), sides=normed, epi=_epi_residual_norm, **deep)
        else:
            x = W.mm(f"ffn_out_fwd{layer}", "nn", u, W.get("w_ffn_out", layer), a_fn=_relu2, extras=(x,),
                     epi=_epi_residual, **deep)
        st.update(hn=hn, u=u)
        saved.append(st)

    loss, dx, dxb, d_final = _loss_head(x, small["final_norm"].reshape(1, D_MODEL), target)

    gs = {k: [None] * v.shape[0] for k, v in small.items() if k != "final_norm"}
    gb = {k: None for k in W.n_layers}
    landed = {k: None for k in W.n_layers}
    pending = []

    def carry(call, take=1):
        items = [pending.pop(0) for _ in range(min(take, len(pending)))]
        if not items:
            return call(None)
        riders = [_grad_rider(key, idx, gb[key], landed[key]) for key, idx in items]
        res, outs = call(_join_riders(riders))
        for (key, _), out in zip(items, outs):
            landed[key] = out
        return res

    def host(name, *args, take=1, **kw):
        return carry(lambda rider: _mm(name, *args, rider=rider, **kw), take)

    def dw(name, key, idx, a, b, call=_mm, **kw):
        gb[key] = call(name, "tn", a, b, stack=(W.n_layers[key], idx, gb[key]), out_dtype=BF16, **kw)
        pending.append((key, idx))

    gain_sums = (("tile", BF16), ("colsum", F32))
    for layer in reversed(range(DEPTH)):
        i = layer // 2
        st = saved[layer]
        du = host(f"ffn_out_bwd{layer}", "nt", dxb, W.get("w_ffn_out", layer), extras=(st["u"],),
                  epi=lambda acc, u: acc * (2.0 * jnp.maximum(u, 0).astype(F32)), out_dtype=BF16, tm=2048)
        dw(f"ffn_out_dw{layer}", "w_ffn_out", layer, st["u"], dxb, a_fn=_relu2, tk=2048)
        dx, dxb, dgain = host(f"ffn_in_bwd{layer}", "nt", du, W.get("w_ffn_in", layer), extras=(st["x_mid"], dx),
                         aux=(small["ffn_norm"][layer:layer + 1],), sides=gain_sums, epi=_epi_rms_bwd, **deep)
        gs["ffn_norm"][layer] = dgain[0:1]
        dw(f"ffn_in_dw{layer}", "w_ffn_in", layer, st["hn"], du, tk=2048)
        norm_bwd = dict(extras=(st["x_in"], dx), aux=(small["mix_norm"][layer:layer + 1],), sides=gain_sums,
                        epi=_epi_rms_bwd)
        if layer % 2 == 0:
            dcat = _mm(f"ab_out_bwd{layer}", "nt", dxb, W.get("ab_w_out", i), out_dtype=BF16)
            dw(f"ab_out_dw{layer}", "ab_w_out", i, st["cat"], dxb, tk=2048)
            gain = small["ab_gn_gain"][i:i + 1]
            dz, gs["ab_gn_gain"][i] = carry(lambda rider: _ret_bwd(f"ret_bwd{layer}", st["z"], tb, gain, st["opre"],
                                                                   st["states"], dcat, rider=rider), take=len(pending))
            dz, gs["ab_w_pool"][i], gs["ab_pool_scale"][i] = _pool_bwd(
                f"pool_bwd{layer}", st["pooled"], small["ab_w_pool"][i], small["ab_pool_scale"][i:i + 1], dcat, dz)
            if layer == 0:
                dw(f"ab_in_dw{layer}", "ab_w_in", i, st["h"], dz, call=host, tn=640, tk=2048)
            dx, dxb, dgain = host(f"ab_in_bwd{layer}", "nt", dz, W.get("ab_w_in", i), tm=512, tk=AB_IN_WIDTH,
                             take=len(pending) if layer == 0 else 1, **norm_bwd)
            if layer > 0:
                dw(f"ab_in_dw{layer}", "ab_w_in", i, st["h"], dz, call=host, tn=640, tk=2048)
        else:
            datt = _mm(f"c_out_bwd{layer}", "nt", dxb, W.get("c_w_out", i), out_dtype=BF16)
            dw(f"c_out_dw{layer}", "c_w_out", i, st["att"], dxb, tk=2048)
            dq, dk, dv, dbias = carry(lambda rider: _attn_bwd(f"attn_bwd{layer}", st["qkv"], st["bias"], datt, rider=rider),
                                      take=len(pending))
            gs["c_rel_bias"][i] = _bias_grad(f"bias_grad{layer}", dbias.reshape(ATT_HEADS, CHUNK, ATT_W), rel_bwd)
            dqkv = [dq, dk, dv]
            dx, dxb, dgain = host(f"qkv_bwd{layer}", "nt", dqkv, W.get("c_w_qkv", i), tm=512, tk=3 * D_MODEL, **norm_bwd)
            dw(f"qkv_dw{layer}", "c_w_qkv", i, st["h"], dqkv, call=host, tk=2048)
        gs["mix_norm"][layer] = dgain[0:1]
    for key, idx in pending:
        landed[key], = _run_rider(f"grad_exchange_{key}{idx}", _grad_rider(key, idx, gb[key], landed[key]))

    g_small = {
        "mix_norm": jnp.concatenate(gs["mix_norm"], axis=0),
        "ffn_norm": jnp.concatenate(gs["ffn_norm"], axis=0),
        "ab_gn_gain": jnp.concatenate(gs["ab_gn_gain"], axis=0),
        "ab_w_pool": jnp.stack(gs["ab_w_pool"], axis=0),
        "ab_pool_scale": jnp.concatenate(gs["ab_pool_scale"], axis=0),
        "c_rel_bias": jnp.stack(gs["c_rel_bias"], axis=0),
        "final_norm": d_final.reshape(D_MODEL),
    }
    return loss, dx, g_small, gb, landed


_BIG = ("w_ffn_in", "w_ffn_out", "ab_w_in", "ab_w_out", "c_w_qkv", "c_w_out")
_SHARD_AXIS = {"w_ffn_in": 2, "w_ffn_out": 1, "ab_w_in": 2, "ab_w_out": 1, "c_w_qkv": 2, "c_w_out": 1}
_SMALL = ("mix_norm", "ffn_norm", "ab_gn_gain", "ab_w_pool", "ab_pool_scale", "c_rel_bias", "final_norm")


def _place():
    x, y, c = lax.axis_index("x"), lax.axis_index("y"), lax.axis_index("c")
    chips = [(1 - x, y), (x, 1 - y), (1 - x, 1 - y)]
    return x, y, c, chips


def _sub(ref, axis, start, size):
    idx = [slice(None)] * len(ref.shape)
    idx[axis] = pl.ds(pl.multiple_of(start, LANE), size)
    return ref.at[tuple(idx)]


def _gather_rider(items, shards):
    keys = sorted({k for k, _ in items})
    n = len(items)
    axes = [_SHARD_AXIS[k] - 1 for k, _ in items]
    sizes = [shards[k].shape[a + 1] for (k, _), a in zip(items, axes)]
    hsizes = [shards[k].shape[2 - a] // 2 for (k, _), a in zip(items, axes)]

    def views(ins, outs, send_sems, recv_sems):
        x, y, c, chips = _place()
        srcs = [ins[keys.index(k)].at[l] for k, l in items]

        def remote(src, dst, s, to):
            return pltpu.make_async_remote_copy(src_ref=src, dst_ref=dst, send_sem=send_sems.at[s],
                                                recv_sem=recv_sems.at[s], device_id=to, device_id_type=MESH)

        def half(w, chip, core):
            return _sub(_sub(outs[w], axes[w], chip * sizes[w], sizes[w]), 1 - axes[w], core * hsizes[w], hsizes[w])

        me = 2 * x + y
        local = [pltpu.make_async_copy(srcs[w], _sub(outs[w], axes[w], me * sizes[w], sizes[w]), send_sems.at[6 * n + w])
                 for w in range(n)]
        first = [remote(_sub(srcs[w], 1 - axes[w], c * hsizes[w], hsizes[w]), half(w, me, c), w * 6 + k, (px, py, c))
                 for w in range(n) for k, (px, py) in enumerate(chips)]
        return x, y, c, chips, remote, half, local, first

    def start(ins, outs, send_sems, recv_sems):
        *_, local, first = views(ins, outs, send_sems, recv_sems)
        for cp in local + first:
            cp.start()

    def passes(x, y, c, chips, remote, half):
        return [remote(half(w, 2 * px + py, c), half(w, 2 * px + py, c), w * 6 + 3 + k, (x, y, 1 - c))
                for w in range(n) for k, (px, py) in enumerate(chips)]

    def relay(ins, outs, send_sems, recv_sems):
        x, y, c, chips, remote, half, _, _ = views(ins, outs, send_sems, recv_sems)
        for w in range(n):
            for k, (px, py) in enumerate(chips):
                landed = half(w, 2 * px + py, c)
                remote(landed, landed, w * 6 + k, (px, py, c)).wait_recv()
        for cp in passes(x, y, c, chips, remote, half):
            cp.start()

    def finish(ins, outs, send_sems, recv_sems):
        x, y, c, chips, remote, half, local, first = views(ins, outs, send_sems, recv_sems)
        for w in range(n):
            for k, (px, py) in enumerate(chips):
                theirs = half(w, 2 * px + py, 1 - c)
                remote(theirs, theirs, w * 6 + 3 + k, (x, y, 1 - c)).wait_recv()
        for cp in first + passes(x, y, c, chips, remote, half):
            cp.wait_send()
        for cp in local:
            cp.wait()

    def full(k, a):
        shape = list(shards[k].shape[1:])
        shape[a] *= N_CHIPS
        return jax.ShapeDtypeStruct(tuple(shape), shards[k].dtype)

    return _Rider(tuple(shards[k] for k in keys), tuple(full(k, a) for (k, _), a in zip(items, axes)), 7 * n, start, finish,
                  relay=relay)


def _mixer_items(layer):
    names = ("ab_w_in", "ab_w_out") if layer % 2 == 0 else ("c_w_qkv", "c_w_out")
    return [(k, layer // 2) for k in names]


class _Weights:
    def __init__(self, shards):
        self.shards = shards
        self.n_layers = {k: shards[k].shape[0] for k in _BIG}
        self.full = {}
        first, second = _mixer_items(0)
        self.plan = {"mix_norm_fwd0": [first], "ab_in_fwd0": [second, ("w_ffn_in", 0)], "ab_out_fwd0": [("w_ffn_out", 0)]}
        for layer in range(1, DEPTH):
            if layer % 2 == 0:
                self.plan[f"attn_fwd{layer - 1}"] = _mixer_items(layer) + [("w_ffn_in", layer), ("w_ffn_out", layer)]
            else:
                self.plan[f"ffn_in_fwd{layer - 1}"] = _mixer_items(layer)
                self.plan[f"ffn_out_fwd{layer - 1}"] = [("w_ffn_in", layer)]
                self.plan[f"qkv_fwd{layer}"] = [("w_ffn_out", layer)]

    def _take(self, items, outs):
        self.full.update(zip(items, outs))

    def get(self, name, layer):
        return self.full[(name, layer)]

    def hosted(self, name, call):
        items = self.plan.get(name)
        if items is None:
            return call(None)
        res, outs = call(_gather_rider(items, self.shards))
        self._take(items, outs)
        return res

    def mm(self, name, *args, **kw):
        return self.hosted(name, lambda rider: _mm(name, *args, rider=rider, **kw))


def _run_rider(name, rider):
    n_in, n_out = len(rider.operands), len(rider.out_shapes)

    def body(*refs):
        ins, outs, sems = refs[:n_in], refs[n_in:n_in + n_out], refs[n_in + n_out:]
        rider.start(ins, outs, *sems)
        if rider.relay is not None:
            rider.relay(ins, outs, *sems)
        rider.finish(ins, outs, *sems)

    return pl.pallas_call(
        body,
        name=name,
        in_specs=[_ANY] * n_in,
        out_specs=[_ANY] * n_out,
        out_shape=list(rider.out_shapes),
        input_output_aliases=dict(rider.aliases),
        scratch_shapes=[pltpu.SemaphoreType.DMA((rider.n_sems,)), pltpu.SemaphoreType.DMA((rider.n_sems,))],
        compiler_params=pltpu.CompilerParams(has_side_effects=True),
    )(*rider.operands)


def _grad_rider(name, layer, grad, landing):
    axis = _SHARD_AXIS[name] - 1
    L, R, C = grad.shape
    shard = (R // N_CHIPS, C) if axis == 0 else (R, C // N_CHIPS)
    size = shard[axis]

    def copies(ins, outs, send_sems, recv_sems):
        x, y, c, chips = _place()
        return [pltpu.make_async_remote_copy(
            src_ref=_sub(ins[0].at[layer], axis, (2 * px + py) * size, size), dst_ref=outs[0].at[layer, k],
            send_sem=send_sems.at[k], recv_sem=recv_sems.at[k], device_id=(px, py, c), device_id_type=MESH)
            for k, (px, py) in enumerate(chips)]

    def start(ins, outs, send_sems, recv_sems):
        for cp in copies(ins, outs, send_sems, recv_sems):
            cp.start()

    def finish(ins, outs, send_sems, recv_sems):
        cps = copies(ins, outs, send_sems, recv_sems)
        for cp in cps:
            cp.wait_recv()
        for cp in cps:
            cp.wait_send()

    out = jax.ShapeDtypeStruct((L, 3) + shard, grad.dtype)
    if landing is None:
        return _Rider((grad,), (out,), 3, start, finish)
    return _Rider((grad, landing), (out,), 3, start, finish, aliases=((1, 0),))


def _join_riders(riders):
    if len(riders) == 1:
        return riders[0]

    def parts(ins, outs, send_sems, recv_sems):
        i0 = o0 = s0 = 0
        for r in riders:
            ni, no = len(r.operands), len(r.out_shapes)
            yield (r, ins[i0:i0 + ni], outs[o0:o0 + no], send_sems.at[pl.ds(s0, r.n_sems)],
                   recv_sems.at[pl.ds(s0, r.n_sems)])
            i0, o0, s0 = i0 + ni, o0 + no, s0 + r.n_sems

    def start(*refs):
        for r, *own in parts(*refs):
            r.start(*own)

    def finish(*refs):
        for r, *own in parts(*refs):
            r.finish(*own)

    aliases, i0, o0 = [], 0, 0
    for r in riders:
        aliases += [(i0 + src, o0 + dst) for src, dst in r.aliases]
        i0, o0 = i0 + len(r.operands), o0 + len(r.out_shapes)
    return _Rider(tuple(x for r in riders for x in r.operands), tuple(x for r in riders for x in r.out_shapes),
                  sum(r.n_sems for r in riders), start, finish, tuple(aliases))


def _rows_tile(rows, cols):
    tr = rows
    while tr * cols > (1 << 19) and tr % 16 == 0:
        tr //= 2
    return tr


def _chip_sum(name, grad, landed, chip, saxis):
    L = grad.shape[0]
    _, _, R, C = landed.shape
    tr = _rows_tile(R, C)
    nr = R // tr
    if saxis == 2:
        g_idx = lambda l, i, s: (l, i, s[0])
    else:
        g_idx = lambda l, i, s: (l, s[0] * nr + i, 0)

    def body(s_ref, g_ref, l_ref, o_ref):
        tot = ((g_ref[...].astype(F32) + l_ref[0].astype(F32)) + l_ref[1].astype(F32)) + l_ref[2].astype(F32)
        o_ref[...] = tot.astype(o_ref.dtype)

    return pl.pallas_call(
        body,
        name=name,
        grid_spec=pltpu.PrefetchScalarGridSpec(
            num_scalar_prefetch=1,
            grid=(L, nr),
            in_specs=[pl.BlockSpec((None, tr, C), g_idx), pl.BlockSpec((None, 3, tr, C), lambda l, i, s: (l, 0, i, 0))],
            out_specs=pl.BlockSpec((None, tr, C), lambda l, i, s: (l, i, 0)),
        ),
        out_shape=jax.ShapeDtypeStruct((L, R, C), BF16),
        compiler_params=_cp(("parallel", "parallel")),
    )(chip, grad, landed)


def _all_reduce_small(packed, sums):
    R = packed.shape[0]
    n = len(sums)

    def body(p_ref, *refs):
        s_in, o_ref, s_out = refs[:n], refs[n], refs[n + 1:2 * n + 1]
        land_ref, send_sems, recv_sems = refs[2 * n + 1:]
        x, y, c, _ = _place()
        me = 4 * x + 2 * y + c
        swaps = [pltpu.make_async_remote_copy(src_ref=s_in[w], dst_ref=s_out[w], send_sem=send_sems.at[N_DEV - 1 + w],
                                              recv_sem=recv_sems.at[N_DEV - 1 + w], device_id=(x, y, 1 - c),
                                              device_id_type=MESH) for w in range(n)]
        for cp in swaps:
            cp.start()
        sends, recvs = [], []
        for r in range(1, N_DEV):
            px, py, pc = x ^ (r >> 2), y ^ ((r >> 1) & 1), c ^ (r & 1)
            cp = pltpu.make_async_remote_copy(src_ref=p_ref, dst_ref=land_ref.at[me], send_sem=send_sems.at[r - 1],
                                              recv_sem=recv_sems.at[r - 1], device_id=(px, py, pc), device_id_type=MESH)
            cp.start()
            sends.append(cp)
            recvs.append(pltpu.make_async_remote_copy(src_ref=p_ref, dst_ref=land_ref.at[4 * px + 2 * py + pc],
                                                      send_sem=send_sems.at[r - 1], recv_sem=recv_sems.at[r - 1],
                                                      device_id=(px, py, pc), device_id_type=MESH))
        land_ref[me] = p_ref[...]
        for cp in recvs:
            cp.wait_recv()
        for cp in sends:
            cp.wait_send()
        acc = land_ref[0]
        for d in range(1, N_DEV):
            acc = acc + land_ref[d]
        o_ref[...] = acc
        for cp in swaps:
            cp.wait_recv()
        for cp in swaps:
            cp.wait_send()

    vm = pl.BlockSpec(memory_space=pltpu.VMEM)
    outs = pl.pallas_call(
        body,
        name="all_reduce_small",
        in_specs=[vm] + [_ANY] * n,
        out_specs=[vm] + [_ANY] * n,
        out_shape=[jax.ShapeDtypeStruct((R, LANE), F32)] + [jax.ShapeDtypeStruct(s.shape, s.dtype) for s in sums],
        scratch_shapes=[pltpu.VMEM((N_DEV, R, LANE), F32), pltpu.SemaphoreType.DMA((N_DEV - 1 + n,)),
                        pltpu.SemaphoreType.DMA((N_DEV - 1 + n,))],
        compiler_params=pltpu.CompilerParams(has_side_effects=True, vmem_limit_bytes=VMEM_LIMIT),
    )(packed, *sums)
    return outs[0], list(outs[1:])


def _adamw(name, w, m, v, grads):
    R, C = w.shape
    tr = _rows_tile(R, C)
    c1 = 1.0 - ADAM_B1 ** ADAM_STEP
    c2 = 1.0 - ADAM_B2 ** ADAM_STEP
    ng = len(grads)

    def body(*refs):
        w_ref, m_ref, v_ref = refs[:3]
        g_refs = refs[3:3 + ng]
        g_ref, d_ref, nm_ref, nv_ref = refs[3 + ng:]
        gv = g_refs[0][...].astype(F32)
        for r in g_refs[1:]:
            gv = gv + r[...].astype(F32)
        g_ref[...] = gv
        nm = ADAM_B1 * m_ref[...] + (1.0 - ADAM_B1) * gv
        nv = ADAM_B2 * v_ref[...] + (1.0 - ADAM_B2) * (gv * gv)
        nm_ref[...] = nm
        nv_ref[...] = nv
        d_ref[...] = -ADAM_LR * ((nm / c1) / (jnp.sqrt(nv / c2) + ADAM_EPS) + ADAM_WD * w_ref[...])

    blk = pl.BlockSpec((tr, C), lambda i: (i, 0))
    out = jax.ShapeDtypeStruct((R, C), F32)
    return pl.pallas_call(
        body,
        name=name,
        grid=(R // tr,),
        in_specs=[blk] * (3 + ng),
        out_specs=[blk] * 4,
        out_shape=[out] * 4,
        compiler_params=_cp(("parallel",)),
    )(w, m, v, *grads)


def _pack(parts):
    rows = []
    for p in parts:
        flat = p.reshape(-1).astype(F32)
        n = -(-flat.shape[0] // (8 * LANE)) * (8 * LANE)
        rows.append(jnp.pad(flat, (0, n - flat.shape[0])).reshape(n // LANE, LANE))
    return jnp.concatenate(rows, axis=0)


def _unpack(packed, like):
    out, r = [], 0
    for p in like:
        size = int(np.prod(p.shape))
        n = -(-size // (8 * LANE)) * 8
        out.append(packed[r:r + n].reshape(-1)[:size].reshape(p.shape))
        r += n
    return out


def kernel(x, mix_norm, ffn_norm, w_ffn_in, w_ffn_out, ab_w_in, ab_gn_gain, ab_w_pool, ab_pool_scale, ab_w_out, c_w_qkv, c_rel_bias, c_w_out, final_norm, loss_target, m_mix_norm, m_ffn_norm, m_w_ffn_in, m_w_ffn_out, m_ab_w_in, m_ab_gn_gain, m_ab_w_pool, m_ab_pool_scale, m_ab_w_out, m_c_w_qkv, m_c_rel_bias, m_c_w_out, m_final_norm, v_mix_norm, v_ffn_norm, v_w_ffn_in, v_w_ffn_out, v_ab_w_in, v_ab_gn_gain, v_ab_w_pool, v_ab_pool_scale, v_ab_w_out, v_c_w_qkv, v_c_rel_bias, v_c_w_out, v_final_norm):
    w = dict(mix_norm=mix_norm, ffn_norm=ffn_norm, w_ffn_in=w_ffn_in, w_ffn_out=w_ffn_out, ab_w_in=ab_w_in,
             ab_gn_gain=ab_gn_gain, ab_w_pool=ab_w_pool, ab_pool_scale=ab_pool_scale, ab_w_out=ab_w_out,
             c_w_qkv=c_w_qkv, c_rel_bias=c_rel_bias, c_w_out=c_w_out, final_norm=final_norm)
    m = dict(mix_norm=m_mix_norm, ffn_norm=m_ffn_norm, w_ffn_in=m_w_ffn_in, w_ffn_out=m_w_ffn_out, ab_w_in=m_ab_w_in,
             ab_gn_gain=m_ab_gn_gain, ab_w_pool=m_ab_w_pool, ab_pool_scale=m_ab_pool_scale, ab_w_out=m_ab_w_out,
             c_w_qkv=m_c_w_qkv, c_rel_bias=m_c_rel_bias, c_w_out=m_c_w_out, final_norm=m_final_norm)
    v = dict(mix_norm=v_mix_norm, ffn_norm=v_ffn_norm, w_ffn_in=v_w_ffn_in, w_ffn_out=v_w_ffn_out, ab_w_in=v_ab_w_in,
             ab_gn_gain=v_ab_gn_gain, ab_w_pool=v_ab_w_pool, ab_pool_scale=v_ab_pool_scale, ab_w_out=v_ab_w_out,
             c_w_qkv=v_c_w_qkv, c_rel_bias=v_c_rel_bias, c_w_out=v_c_w_out, final_norm=v_final_norm)
    S = x.shape[1]
    cx, cy, cc = lax.axis_index("x"), lax.axis_index("y"), lax.axis_index("c")
    chip = jnp.reshape(2 * cx + cy, (1,)).astype(jnp.int32)

    big = _Weights({k: w[k].astype(BF16) for k in _BIG})
    small = {k: w[k] for k in _SMALL}
    loss, grad_x, g_small, g_big, landed = _local_step(x.reshape(S, D_MODEL), loss_target.reshape(S, D_MODEL), small, big)

    sums = [_chip_sum(f"chip_sum_{k}", g_big[k], landed[k], chip, _SHARD_AXIS[k]) for k in _BIG]

    packed, siblings = _all_reduce_small(_pack([g_small[k] for k in _SMALL] + [loss]), sums)
    small_like = [w[k] for k in _SMALL]
    g_red = dict(zip(_SMALL, _unpack(packed, small_like)))
    loss_row = packed.shape[0] - 8
    loss_out = packed[loss_row, 0]

    grad, delta, new_m, new_v = {}, {}, {}, {}
    for k, mine, theirs in zip(_BIG, sums, siblings):
        shp = w[k].shape
        two = (shp[0] * shp[1], shp[2])
        outs = _adamw(f"adamw_{k}", w[k].reshape(two), m[k].reshape(two), v[k].reshape(two),
                      (mine.reshape(two), theirs.reshape(two)))
        grad[k], delta[k], new_m[k], new_v[k] = [o.reshape(shp) for o in outs]
    _, d, nm, nv = _adamw("adamw_small", _pack(small_like), _pack([m[k] for k in _SMALL]), _pack([v[k] for k in _SMALL]),
                          (packed[:loss_row],))
    for k, dk, mk, vk in zip(_SMALL, _unpack(d, small_like), _unpack(nm, small_like), _unpack(nv, small_like)):
        grad[k], delta[k], new_m[k], new_v[k] = g_red[k], dk, mk, vk

    order = ("mix_norm", "ffn_norm", "w_ffn_in", "w_ffn_out", "ab_w_in", "ab_gn_gain", "ab_w_pool", "ab_pool_scale",
             "ab_w_out", "c_w_qkv", "c_rel_bias", "c_w_out", "final_norm")
    return (loss_out, grad_x.reshape(x.shape), *[grad[k] for k in order], *[delta[k] for k in order],
            *[new_m[k] for k in order], *[new_v[k] for k in order])
```

```python
import functools
from typing import Callable, NamedTuple

import numpy as np
import jax
import jax.numpy as jnp
from jax import lax
from jax.experimental import pallas as pl
from jax.experimental.pallas import tpu as pltpu

F32 = jnp.float32
BF16 = jnp.bfloat16

D_MODEL = 1024
D_FF = 4096
DEPTH = 4
CHUNK = 64
RMS_EPS = 1e-6
RET_WIDTH = 512
RET_HEADS = 4
RET_HEAD_DIM = 128
RET_ROPE_BASE = 10000.0
GN_EPS = 1e-5
POOL_WIDTH = 512
POOL_WINDOWS = (2, 4, 8, 16)
POOL_GROUP_DIM = 128
POOL_HALO = 16
AB_IN_WIDTH = 2560
ATT_HEADS = 16
ATT_HEAD_DIM = 64
LEFT_CHUNKS = 8
BAND = (LEFT_CHUNKS + 1) * CHUNK
REL_CLIP = 128
N_REL = 2 * REL_CLIP + 1
N_REL_PAD = 264
NEG_INF = -1e30
KSCALE = RET_HEAD_DIM ** -0.5
QSCALE = ATT_HEAD_DIM ** -0.5

ADAM_LR = 0.001
ADAM_B1 = 0.9
ADAM_B2 = 0.999
ADAM_EPS = 1e-08
ADAM_WD = 0.01
ADAM_STEP = 10

ATT_BLOCK = LEFT_CHUNKS * CHUNK
RET_BLOCK = 512
N_CHIPS = 4
N_DEV = 8
LANE = 128
VMEM_LIMIT = 52 * 1024 * 1024
EPI_ROWS = 256
MESH = pl.DeviceIdType.MESH


def _cp(sem, vmem=VMEM_LIMIT):
    return pltpu.CompilerParams(dimension_semantics=sem, vmem_limit_bytes=vmem)


def _dot(a, b):
    return lax.dot_general(a, b, (((1,), (0,)), ((), ())), preferred_element_type=F32)


def _dot_nt(a, b):
    return lax.dot_general(a, b, (((1,), (1,)), ((), ())), preferred_element_type=F32)


def _dot_tn(a, b):
    return lax.dot_general(a, b, (((0,), (0,)), ((), ())), preferred_element_type=F32)


_ANY = pl.BlockSpec(memory_space=pl.ANY)


class _Rider(NamedTuple):
    operands: tuple
    out_shapes: tuple
    n_sems: int
    start: Callable
    finish: Callable
    aliases: tuple = ()
    relay: Callable = None


def _mm(name, mode, a, b, *, la=None, lb=None, tm=1024, tn=1024, tk=1024, a_fn=None, b_fn=None,
        extras=(), aux=(), sides=(), epi=None, out_dtype=F32, stack=None, rider=None):
    a_parts = list(a) if isinstance(a, (list, tuple)) else [a]
    b_parts = list(b) if isinstance(b, (list, tuple)) else [b]
    na, nbp = len(a_parts), len(b_parts)
    a2, b2 = list(a_parts[0].shape[-2:]), list(b_parts[0].shape[-2:])
    a2[1] *= na
    b2[1] *= nbp
    if mode == "nn":
        (M, K), (K2, N) = a2, b2
    elif mode == "nt":
        (M, K), (N, K2) = a2, b2
    else:
        (K, M), (K2, N) = a2, b2
    assert K == K2, (name, a2, b2)
    tm, tn, tk = min(tm, M), min(tn, N), min(tk, K)
    assert M % tm == 0 and N % tn == 0 and K % tk == 0, (name, M, N, K, tm, tn, tk)
    gm, gn, gk = M // tm, N // tn, K // tk
    fold = mode == "nt" and na > 1 and gk == 1

    def specs(parts, block, idx, lead):
        per = parts[0].shape[-1] // block[1]
        assert parts[0].shape[-1] % block[1] == 0, (name, parts[0].shape, block)
        out = []
        for p in range(len(parts)):
            def f(i, j, k, p=p):
                r, c = idx(i, j, k)
                if len(parts) > 1:
                    c = jnp.clip(c - p * per, 0, per - 1)
                return (r, c) if lead is None else (lead, r, c)
            out.append(pl.BlockSpec(block if lead is None else (None,) + block, f))
        return out, per

    if mode == "nn":
        a_specs, a_per = specs(a_parts, (tm, tk), lambda i, j, k: (i, k), la)
        b_specs, b_per = specs(b_parts, (tk, tn), lambda i, j, k: (k, j), lb)
        a_axis, b_axis, dot = 2, 1, _dot
    elif mode == "nt":
        if fold:
            a_specs, a_per = [pl.BlockSpec((tm, K // na), lambda i, j, k: (i, 0)) for _ in a_parts], 1
        else:
            a_specs, a_per = specs(a_parts, (tm, tk), lambda i, j, k: (i, k), la)
        b_specs, b_per = specs(b_parts, (tn, tk), lambda i, j, k: (j, k), lb)
        a_axis, b_axis, dot = 2, 2, _dot_nt
    else:
        a_specs, a_per = specs(a_parts, (tk, tm), lambda i, j, k: (k, i), la)
        b_specs, b_per = specs(b_parts, (tk, tn), lambda i, j, k: (k, j), lb)
        a_axis, b_axis, dot = 0, 1, _dot_tn
    ex_specs = [pl.BlockSpec((tm, tn), lambda i, j, k: (i, j)) for _ in extras]
    n_ex = len(extras)

    n_aux, n_side = len(aux), len(sides)
    operands = a_parts + b_parts + list(extras) + list(aux)
    in_specs = a_specs + b_specs + ex_specs + [pl.BlockSpec(v.shape, lambda i, j, k, nd=v.ndim: (0,) * nd) for v in aux]
    aliases = {}
    if stack is None:
        out_specs = [pl.BlockSpec((tm, tn), lambda i, j, k: (i, j))]
        out_shapes = [jax.ShapeDtypeStruct((M, N), out_dtype)]
    else:
        n_layers, layer, prev = stack
        out_specs = [pl.BlockSpec((None, tm, tn), lambda i, j, k: (layer, i, j))]
        out_shapes = [jax.ShapeDtypeStruct((n_layers, M, N), out_dtype)]
        if prev is not None:
            aliases = {len(operands): 0}
            operands.append(prev)
            in_specs.append(_ANY)
    for kind, dtype in sides:
        if kind == "tile":
            out_specs.append(pl.BlockSpec((tm, tn), lambda i, j, k: (i, j)))
            out_shapes.append(jax.ShapeDtypeStruct((M, N), dtype))
        else:
            assert gn == 1, name
            out_specs.append(pl.BlockSpec((8, tn), lambda i, j, k: (0, 0)))
            out_shapes.append(jax.ShapeDtypeStruct((8, N), dtype))
    n_prev = len(aliases)
    scratch = [pltpu.VMEM((tm, tn), F32)] if gk > 1 else []
    n_rin = n_rout = 0
    if rider is not None:
        n_rin, n_rout = len(rider.operands), len(rider.out_shapes)
        for src, dst in rider.aliases:
            aliases[len(operands) + src] = 1 + n_side + dst
        operands += list(rider.operands)
        in_specs += [_ANY] * n_rin
        out_specs += [_ANY] * n_rout
        out_shapes += list(rider.out_shapes)
        scratch += [pltpu.SemaphoreType.DMA((rider.n_sems,)), pltpu.SemaphoreType.DMA((rider.n_sems,))]
    assert na == 1 or nbp == 1, name

    def body(*refs):
        a_refs, b_refs = refs[:na], refs[na:na + nbp]
        ex_refs = refs[na + nbp:na + nbp + n_ex + n_aux]
        n_in = na + nbp + n_ex + n_aux + n_prev
        rin = refs[n_in:n_in + n_rin]
        o_ref = refs[n_in + n_rin]
        side_refs = refs[n_in + n_rin + 1:n_in + n_rin + 1 + n_side]
        rout = refs[n_in + n_rin + 1 + n_side:n_in + n_rin + 1 + n_side + n_rout]
        rest = refs[n_in + n_rin + 1 + n_side + n_rout:]
        i, j, k = pl.program_id(0), pl.program_id(1), pl.program_id(2)
        if rider is not None:
            sems = rest[-2:]

            @pl.when(jnp.logical_and(i == 0, jnp.logical_and(j == 0, k == 0)))
            def _():
                rider.start(rin, rout, *sems)

        def finish(acc):
            if epi is None:
                o_ref[...] = acc[...].astype(o_ref.dtype)
                return
            strip = min(tm, EPI_ROWS)
            colsums = [None] * n_side
            for r0 in range(0, tm, strip):
                rows = slice(r0, r0 + strip)
                res = epi(acc[rows, :], *[r[rows, :] for r in ex_refs[:n_ex]], *[r[...] for r in ex_refs[n_ex:]])
                if n_side:
                    res, *side_vals = res
                    for s, ((kind, _), ref, val) in enumerate(zip(sides, side_refs, side_vals)):
                        if kind == "tile":
                            ref[rows, :] = val.astype(ref.dtype)
                        else:
                            colsums[s] = val if colsums[s] is None else colsums[s] + val
                o_ref[rows, :] = res.astype(o_ref.dtype)
            for (kind, _), ref, val in zip(sides, side_refs, colsums):
                if kind == "colsum":
                    @pl.when(i == 0)
                    def _(ref=ref, val=val):
                        ref[...] = val

                    @pl.when(i > 0)
                    def _(ref=ref, val=val):
                        ref[...] += val

                    @pl.when(i == gm - 1)
                    def _(ref=ref):
                        ref[0:1, :] = jnp.sum(ref[...], axis=0, keepdims=True)

        def step(a_ref, b_ref):
            av, bv = a_ref[...], b_ref[...]
            if a_fn is not None:
                av = a_fn(av)
            if b_fn is not None:
                bv = b_fn(bv)
            part = dot(av.astype(BF16), bv.astype(BF16))
            if gk == 1:
                finish(part)
                return
            acc_ref = rest[0]

            @pl.when(k == 0)
            def _():
                acc_ref[...] = part

            @pl.when(k > 0)
            def _():
                acc_ref[...] += part

        if fold:
            kp = K // na
            finish(sum(dot(a_refs[p][...].astype(BF16), b_refs[0][:, p * kp:(p + 1) * kp].astype(BF16))
                       for p in range(na)))
        elif na > 1:
            sel = pl.program_id(a_axis) // a_per
            for p in range(na):
                pl.when(sel == p)(functools.partial(step, a_refs[p], b_refs[0]))
        elif nbp > 1:
            sel = pl.program_id(b_axis) // b_per
            for p in range(nbp):
                pl.when(sel == p)(functools.partial(step, a_refs[0], b_refs[p]))
        else:
            step(a_refs[0], b_refs[0])
        if gk > 1:
            @pl.when(k == gk - 1)
            def _():
                finish(rest[0])

        if rider is not None:
            steps = gm * gn * gk
            step_no = (i * gn + j) * gk + k
            if rider.relay is not None:
                assert steps >= 3, name

                @pl.when(step_no == steps - 2)
                def _():
                    rider.relay(rin, rout, *sems)

            @pl.when(step_no == steps - 1)
            def _():
                rider.finish(rin, rout, *sems)

    sequential = rider is not None or any(kind == "colsum" for kind, _ in sides)
    sem = ("arbitrary",) * 3 if sequential else ("parallel", "parallel", "arbitrary")
    outs = pl.pallas_call(
        body,
        name=name,
        grid=(gm, gn, gk),
        in_specs=in_specs,
        out_specs=out_specs,
        out_shape=out_shapes,
        input_output_aliases=aliases,
        scratch_shapes=scratch,
        compiler_params=_cp(sem),
    )(*operands)
    res = outs[0] if not sides else tuple(outs[:1 + n_side])
    return res if rider is None else (res, list(outs[1 + n_side:]))


def _relu2(u):
    r = jnp.maximum(u, 0)
    return r * r


def _epi_residual(acc, res):
    return acc + res


def _epi_residual_norm(acc, res, g):
    xn = acc + res
    r = lax.rsqrt(jnp.mean(xn * xn, axis=-1, keepdims=True) + RMS_EPS)
    return xn, (xn * r) * g


def _epi_rms_bwd(dh, x, dres, g):
    r = lax.rsqrt(jnp.mean(x * x, axis=-1, keepdims=True) + RMS_EPS)
    xh = x * r
    dxh = dh * g
    dx = dres + r * (dxh - xh * jnp.mean(dxh * xh, axis=-1, keepdims=True))
    return dx, dx, jnp.sum((dh * xh).reshape(dh.shape[0] // 8, 8, dh.shape[1]), axis=0)


def _rms_fwd(name, x, g, rider=None):
    S, D = x.shape
    tq = min(1024, S)

    def body(x_ref, g_ref, o_ref):
        xv = x_ref[...]
        r = lax.rsqrt(jnp.mean(xv * xv, axis=-1, keepdims=True) + RMS_EPS)
        o_ref[...] = ((xv * r) * g_ref[...]).astype(o_ref.dtype)

    grid = (S // tq,)
    body, r_in, r_out, r_shapes, r_sems, aliases = _riding(rider, body, 2, 1, grid)
    outs = pl.pallas_call(
        body,
        name=name,
        grid=grid,
        in_specs=[pl.BlockSpec((tq, D), lambda i: (i, 0)), pl.BlockSpec((1, D), lambda i: (0, 0))] + r_in,
        out_specs=[pl.BlockSpec((tq, D), lambda i: (i, 0))] + r_out,
        out_shape=[jax.ShapeDtypeStruct((S, D), BF16)] + r_shapes,
        input_output_aliases=aliases,
        scratch_shapes=r_sems,
        compiler_params=_cp(("parallel" if rider is None else "arbitrary",)),
    )(x, g, *(rider.operands if rider is not None else ()))
    return outs[0] if rider is None else (outs[0], list(outs[1:]))


def _loss_head(x, g, t):
    S, D = x.shape
    tq = min(512, S)
    n = S // tq

    def body(x_ref, g_ref, t_ref, loss_ref, dx_ref, dxb_ref, dg_ref, lacc_ref, gacc_ref):
        i = pl.program_id(0)
        xv = x_ref[...]
        gv = g_ref[...]
        r = lax.rsqrt(jnp.mean(xv * xv, axis=-1, keepdims=True) + RMS_EPS)
        xh = xv * r
        e = xh * gv - t_ref[...]
        dy = e * (1.0 / D)
        dxh = dy * gv
        dx = r * (dxh - xh * jnp.mean(dxh * xh, axis=-1, keepdims=True))
        dx_ref[...] = dx
        dxb_ref[...] = dx.astype(dxb_ref.dtype)
        lpart = jnp.sum((e * e).reshape(tq // 8, 8, D), axis=0)
        gpart = jnp.sum((dy * xh).reshape(tq // 8, 8, D), axis=0)

        @pl.when(i == 0)
        def _():
            lacc_ref[...] = lpart
            gacc_ref[...] = gpart

        @pl.when(i > 0)
        def _():
            lacc_ref[...] += lpart
            gacc_ref[...] += gpart

        @pl.when(i == n - 1)
        def _():
            dg_ref[...] = jnp.sum(gacc_ref[...], axis=0, keepdims=True)
            tot = jnp.sum(jnp.sum(lacc_ref[...], axis=0, keepdims=True), axis=1, keepdims=True)
            loss_ref[...] = jnp.broadcast_to(tot * (0.5 / D), (1, LANE))

    return pl.pallas_call(
        body,
        name="loss_head",
        grid=(n,),
        in_specs=[pl.BlockSpec((tq, D), lambda i: (i, 0)), pl.BlockSpec((1, D), lambda i: (0, 0)),
                  pl.BlockSpec((tq, D), lambda i: (i, 0))],
        out_specs=[pl.BlockSpec((1, LANE), lambda i: (0, 0)), pl.BlockSpec((tq, D), lambda i: (i, 0)),
                   pl.BlockSpec((tq, D), lambda i: (i, 0)), pl.BlockSpec((1, D), lambda i: (0, 0))],
        out_shape=[jax.ShapeDtypeStruct((1, LANE), F32), jax.ShapeDtypeStruct((S, D), F32),
                   jax.ShapeDtypeStruct((S, D), BF16), jax.ShapeDtypeStruct((1, D), F32)],
        scratch_shapes=[pltpu.VMEM((8, D), F32), pltpu.VMEM((8, D), F32)],
        compiler_params=_cp(("arbitrary",)),
    )(x, g, t)


def _ret_tables(S):
    T = min(RET_BLOCK, S)
    inv_freq = 1.0 / (RET_ROPE_BASE ** jnp.linspace(0.0, 1.0, RET_HEAD_DIM // 2, dtype=F32))
    ang = jnp.arange(S, dtype=F32)[:, None] * jnp.repeat(inv_freq, 2)[None, :]
    cosf = jnp.cos(ang)
    sins = jnp.sin(ang) * jnp.asarray(np.tile([-1.0, 1.0], RET_HEAD_DIM // 2), F32)[None, :]
    log_g = np.log1p(-np.power(2.0, -5.0 - np.arange(RET_HEADS, dtype=np.float64)))
    pos = np.arange(T, dtype=np.float64)
    diff = pos[:, None] - pos[None, :]
    same = (pos[:, None] // CHUNK) == (pos[None, :] // CHUNK)
    seen = same | (diff > 0)
    dmat = np.where(seen[None], np.exp(np.abs(diff)[None] * log_g[:, None, None]), 0.0)
    aq = np.exp((pos[None, :] + 1.0) * log_g[:, None])
    ak = np.exp((T - 1.0 - pos[None, :]) * log_g[:, None])
    lam = np.exp(T * log_g)
    bc = lambda v: jnp.asarray(np.broadcast_to(v[..., None], v.shape + (LANE,)), F32)
    return dict(cos=cosf, sin=sins, dmat=jnp.asarray(dmat, F32), aq=bc(aq), ak=bc(ak),
                lam=jnp.asarray(np.broadcast_to(lam[:, None, None], (RET_HEADS, 1, LANE)), F32))


def _rot(x, cos, sin_s, even):
    sw = jnp.where(even, pltpu.roll(x, LANE - 1, 1), pltpu.roll(x, 1, 1))
    return x * cos + sw * sin_s


def _rot_t(dy, cos, sin_s, even):
    t = dy * sin_s
    return dy * cos + jnp.where(even, pltpu.roll(t, LANE - 1, 1), pltpu.roll(t, 1, 1))


def _ret_specs(T, rev_nb=None):
    blk = (lambda b: b) if rev_nb is None else (lambda b: rev_nb - 1 - b)
    whole = lambda shape: pl.BlockSpec(shape, lambda b: (0,) * len(shape))
    specs = [pl.BlockSpec((T, AB_IN_WIDTH), lambda b: (blk(b), 0)),
             pl.BlockSpec((T, LANE), lambda b: (blk(b), 0)),
             pl.BlockSpec((T, LANE), lambda b: (blk(b), 0)),
             whole((RET_HEADS, T, T)), whole((RET_HEADS, T, LANE)), whole((RET_HEADS, T, LANE)),
             whole((RET_HEADS, 1, LANE)), whole((1, RET_WIDTH))]
    return specs, blk


def _head_views(h, z_ref, tabs, token_refs, head_refs):
    zs = [z_ref.at[:, (o * RET_HEADS + h) * LANE:(o * RET_HEADS + h + 1) * LANE] for o in range(4)]
    hs = slice(h * LANE, (h + 1) * LANE)
    return zs, [t.at[h] for t in tabs], [r.at[:, hs] for r in token_refs], [r.at[h] for r in head_refs]


def _ret_fwd(name, z, tb, gain, rider=None):
    S = z.shape[0]
    T = min(RET_BLOCK, S)
    nb = S // T
    specs, blk = _ret_specs(T)

    def body(z_ref, cos_r, sin_r, d_all, aq_all, ak_all, lam_all, gain_all, cat_all, opre_all, st_all, state_all):
        @pl.when(pl.program_id(0) == 0)
        def _():
            state_all[...] = jnp.zeros_like(state_all)

        for h in range(RET_HEADS):
            zs, tabs, toks, heads = _head_views(h, z_ref, (d_all, aq_all, ak_all, lam_all),
                                                (gain_all, cat_all, opre_all), (st_all, state_all))
            head(*zs, cos_r, sin_r, *tabs, *toks, *heads)

    def head(zq, zk, zv, zg, cos_r, sin_r, d_r, aq_r, ak_r, lam_r, gain_r, ret_o, opre_o, st_o, state):
        even = (lax.broadcasted_iota(jnp.int32, (T, LANE), 1) & 1) == 0
        c, s = cos_r[...], sin_r[...]
        q = _rot(zq[...].astype(F32), c, s, even)
        k = _rot(zk[...].astype(F32), c, s, even) * KSCALE
        qb, kb, vb = q.astype(BF16), k.astype(BF16), zv[...].astype(BF16)
        p = (_dot_nt(qb, kb) * d_r[...]).astype(BF16)
        st = state[...]
        st_o[...] = st
        o = _dot(p, vb) + _dot((q * aq_r[...]).astype(BF16), st.astype(BF16))
        state[...] = st * lam_r[...] + _dot_tn((k * ak_r[...]).astype(BF16), vb)
        opre_o[...] = o
        mu = jnp.mean(o, axis=-1, keepdims=True)
        d = o - mu
        y = d * lax.rsqrt(jnp.mean(d * d, axis=-1, keepdims=True) + GN_EPS)
        g = zg[...].astype(F32)
        ret_o[...] = ((g * jax.nn.sigmoid(g)) * (y * gain_r[...])).astype(ret_o.dtype)

    out_blk = pl.BlockSpec((T, RET_WIDTH), lambda b: (b, 0))
    grid = (nb,)
    body, r_in, r_out, r_shapes, r_sems, aliases = _riding(rider, body, 8, 3, grid)
    outs = pl.pallas_call(
        body,
        name=name,
        grid=grid,
        in_specs=specs + r_in,
        out_specs=[out_blk, out_blk, pl.BlockSpec((RET_HEADS, None, LANE, LANE), lambda b: (0, b, 0, 0))] + r_out,
        out_shape=[jax.ShapeDtypeStruct((S, D_MODEL), BF16), jax.ShapeDtypeStruct((S, RET_WIDTH), F32),
                   jax.ShapeDtypeStruct((RET_HEADS, nb, LANE, LANE), F32)] + r_shapes,
        input_output_aliases=aliases,
        scratch_shapes=[pltpu.VMEM((RET_HEADS, LANE, LANE), F32)] + r_sems,
        compiler_params=_cp(("arbitrary",)),
    )(z, tb["cos"], tb["sin"], tb["dmat"], tb["aq"], tb["ak"], tb["lam"], gain,
      *(rider.operands if rider is not None else ()))
    return tuple(outs[:3]) if rider is None else (tuple(outs[:3]), list(outs[3:]))


def _ret_bwd(name, z, tb, gain, opre, states, dcat, rider=None):
    S = z.shape[0]
    T = min(RET_BLOCK, S)
    nb = S // T
    specs, blk = _ret_specs(T, rev_nb=nb)
    tok = pl.BlockSpec((T, RET_WIDTH), lambda b: (blk(b), 0))

    def body(z_ref, cos_r, sin_r, d_all, aq_all, ak_all, lam_all, gain_all, opre_all, st_all, dret_all,
             dz_ref, dgain_all, dstate_all):
        @pl.when(pl.program_id(0) == 0)
        def _():
            dstate_all[...] = jnp.zeros_like(dstate_all)
            dgain_all[...] = jnp.zeros_like(dgain_all)

        for h in range(RET_HEADS):
            zs, tabs, toks, heads = _head_views(h, z_ref, (d_all, aq_all, ak_all, lam_all),
                                                (gain_all, opre_all, dret_all, dgain_all), (st_all, dstate_all))
            dzs, _, _, _ = _head_views(h, dz_ref, (), (), ())
            gain_r, opre_r, dret_r, dgain_o = toks
            head(*zs, cos_r, sin_r, *tabs, gain_r, opre_r, heads[0], dret_r, *dzs, dgain_o, heads[1])

    def head(zq, zk, zv, zg, cos_r, sin_r, d_r, aq_r, ak_r, lam_r, gain_r, opre_r, st_r, dret_r,
             dq_o, dk_o, dv_o, dg_o, dgain_o, dstate):
        even = (lax.broadcasted_iota(jnp.int32, (T, LANE), 1) & 1) == 0
        c, s = cos_r[...], sin_r[...]
        aq, ak, dm = aq_r[...], ak_r[...], d_r[...]
        q = _rot(zq[...].astype(F32), c, s, even)
        k = _rot(zk[...].astype(F32), c, s, even) * KSCALE
        qb, kb, vb = q.astype(BF16), k.astype(BF16), zv[...].astype(BF16)
        pb = (_dot_nt(qb, kb) * dm).astype(BF16)
        g = zg[...].astype(F32)
        sig = jax.nn.sigmoid(g)
        o = opre_r[...]
        mu = jnp.mean(o, axis=-1, keepdims=True)
        d = o - mu
        rstd = lax.rsqrt(jnp.mean(d * d, axis=-1, keepdims=True) + GN_EPS)
        y = d * rstd
        gain_v = gain_r[...]
        dret = dret_r[...].astype(F32)
        dyg = dret * (g * sig)
        dg_o[...] = (dret * (y * gain_v) * (sig * (1.0 + g * (1.0 - sig)))).astype(dg_o.dtype)
        dgain_o[...] += jnp.sum(dyg * y, axis=0, keepdims=True)
        dy = dyg * gain_v
        do = rstd * (dy - jnp.mean(dy, axis=-1, keepdims=True) - y * jnp.mean(dy * y, axis=-1, keepdims=True))
        dob = do.astype(BF16)
        stb = st_r[...].astype(BF16)
        dsn = dstate[...]
        dsnb = dsn.astype(BF16)
        dpb = (_dot_nt(dob, vb) * dm).astype(BF16)
        dq = _dot(dpb, kb) + _dot_nt(dob, stb) * aq
        dk = _dot_tn(dpb, qb) + _dot_nt(vb, dsnb) * ak
        dv = _dot_tn(pb, dob) + _dot((k * ak).astype(BF16), dsnb)
        dstate[...] = dsn * lam_r[...] + _dot_tn((q * aq).astype(BF16), dob)
        dq_o[...] = _rot_t(dq, c, s, even).astype(dq_o.dtype)
        dk_o[...] = _rot_t(dk * KSCALE, c, s, even).astype(dk_o.dtype)
        dv_o[...] = dv.astype(dv_o.dtype)

    grid = (nb,)
    body, r_in, r_out, r_shapes, r_sems, aliases = _riding(rider, body, 11, 2, grid)
    outs = pl.pallas_call(
        body,
        name=name,
        grid=grid,
        in_specs=specs + [tok, pl.BlockSpec((RET_HEADS, None, LANE, LANE), lambda b: (0, blk(b), 0, 0)), tok] + r_in,
        out_specs=[pl.BlockSpec((T, 4 * RET_WIDTH), lambda b: (blk(b), 0)),
                   pl.BlockSpec((1, RET_WIDTH), lambda b: (0, 0))] + r_out,
        out_shape=[jax.ShapeDtypeStruct((S, AB_IN_WIDTH), BF16), jax.ShapeDtypeStruct((1, RET_WIDTH), F32)] + r_shapes,
        input_output_aliases=aliases,
        scratch_shapes=[pltpu.VMEM((RET_HEADS, LANE, LANE), F32)] + r_sems,
        compiler_params=_cp(("arbitrary",)),
    )(z, tb["cos"], tb["sin"], tb["dmat"], tb["aq"], tb["ak"], tb["lam"], gain, opre, states, dcat,
      *(rider.operands if rider is not None else ()))
    return tuple(outs[:2]) if rider is None else (tuple(outs[:2]), list(outs[2:]))


def _pool_counts(t0, rows):
    t = t0 + lax.broadcasted_iota(jnp.int32, (rows, POOL_WIDTH), 0)
    grp = lax.broadcasted_iota(jnp.int32, (rows, POOL_WIDTH), 1) >> 7
    win = jnp.where(grp == 0, POOL_WINDOWS[0], jnp.where(grp == 1, POOL_WINDOWS[1],
                    jnp.where(grp == 2, POOL_WINDOWS[2], POOL_WINDOWS[3])))
    return jnp.maximum(jnp.minimum(t + 1, win), 1).astype(F32), grp


def _window_sums(ext, grp, sign):
    n = ext.shape[0]
    sh = lambda v, k: pltpu.roll(v, k % n if sign > 0 else (n - k) % n, 0)
    s2 = ext + sh(ext, 1)
    s4 = s2 + sh(s2, 2)
    s8 = s4 + sh(s4, 4)
    s16 = s8 + sh(s8, 8)
    return jnp.where(grp == 0, s2, jnp.where(grp == 1, s4, jnp.where(grp == 2, s8, s16)))


def _pool_fwd(name, z, w_pool, scale, cat):
    S = z.shape[0]
    T = min(512, S)
    nb = S // T
    pcol = AB_IN_WIDTH // POOL_WIDTH - 1
    hb = T // POOL_HALO

    def body(p_ref, halo_ref, w_ref, sc_ref, cat_in, out_ref, pooled_ref):
        b = pl.program_id(0)
        cur = p_ref[...].astype(F32)
        halo = jnp.where(b > 0, halo_ref[...].astype(F32), 0.0)
        ext = jnp.concatenate([halo, cur], axis=0)
        cnt, grp = _pool_counts(b * T - POOL_HALO, T + POOL_HALO)
        sums = _window_sums(ext, grp, +1)
        pooled = (sums / cnt)[POOL_HALO:] - cur
        pb = pooled.astype(BF16)
        pooled_ref[...] = pb
        for gi in range(len(POOL_WINDOWS)):
            cs = slice(gi * POOL_GROUP_DIM, (gi + 1) * POOL_GROUP_DIM)
            mixed = _dot(pb[:, cs], w_ref[gi].astype(BF16))
            out_ref[:, cs] = (mixed * sc_ref[:, cs]).astype(out_ref.dtype)

    return pl.pallas_call(
        body,
        name=name,
        grid=(nb,),
        in_specs=[pl.BlockSpec((T, POOL_WIDTH), lambda b: (b, pcol)),
                  pl.BlockSpec((POOL_HALO, POOL_WIDTH), lambda b: (jnp.maximum(b * hb - 1, 0), pcol)),
                  pl.BlockSpec((4, POOL_GROUP_DIM, POOL_GROUP_DIM), lambda b: (0, 0, 0)),
                  pl.BlockSpec((1, POOL_WIDTH), lambda b: (0, 0)), _ANY],
        out_specs=[pl.BlockSpec((T, POOL_WIDTH), lambda b: (b, 1)), pl.BlockSpec((T, POOL_WIDTH), lambda b: (b, 0))],
        out_shape=[jax.ShapeDtypeStruct(cat.shape, cat.dtype), jax.ShapeDtypeStruct((S, POOL_WIDTH), BF16)],
        input_output_aliases={4: 0},
        compiler_params=_cp(("parallel",)),
    )(z, z, w_pool, scale, cat)


def _pool_bwd(name, pooled, w_pool, scale, dcat, dz):
    S = pooled.shape[0]
    T = min(512, S)
    nb = S // T
    hb = T // POOL_HALO
    last_h = S // POOL_HALO - 1
    pcol = AB_IN_WIDTH // POOL_WIDTH - 1

    def body(d_ref, dn_ref, pooled_ref, w_ref, sc_ref, dz_in, dp_ref, dw_ref, dsc_ref):
        b = pl.program_id(0)

        @pl.when(b == 0)
        def _():
            dw_ref[...] = jnp.zeros_like(dw_ref)
            dsc_ref[...] = jnp.zeros_like(dsc_ref)

        sc = sc_ref[...]
        dout = d_ref[...].astype(F32)
        dnext = jnp.where(b < nb - 1, dn_ref[...].astype(F32), 0.0)
        dmix = jnp.concatenate([dout, dnext], axis=0) * sc
        dmb = dmix.astype(BF16)
        pb = pooled_ref[...]
        dpooled = []
        for gi in range(len(POOL_WINDOWS)):
            cs = slice(gi * POOL_GROUP_DIM, (gi + 1) * POOL_GROUP_DIM)
            wb = w_ref[gi].astype(BF16)
            dpooled.append(_dot_nt(dmb[:, cs], wb))
            dw_ref[gi] += _dot_tn(pb[:, cs], dmb[:T, cs])
            mixed = _dot(pb[:, cs], wb)
            dsc_ref[:, cs] += jnp.sum(dout[:, cs] * mixed, axis=0, keepdims=True)
        dpl = jnp.concatenate(dpooled, axis=1)
        cnt, grp = _pool_counts(b * T, T + POOL_HALO)
        sums = _window_sums(dpl / cnt, grp, -1)
        dp_ref[...] = (sums[:T] - dpl[:T]).astype(dp_ref.dtype)

    return pl.pallas_call(
        body,
        name=name,
        grid=(nb,),
        in_specs=[pl.BlockSpec((T, POOL_WIDTH), lambda b: (b, 1)),
                  pl.BlockSpec((POOL_HALO, POOL_WIDTH), lambda b: (jnp.minimum((b + 1) * hb, last_h), 1)),
                  pl.BlockSpec((T, POOL_WIDTH), lambda b: (b, 0)),
                  pl.BlockSpec((4, POOL_GROUP_DIM, POOL_GROUP_DIM), lambda b: (0, 0, 0)),
                  pl.BlockSpec((1, POOL_WIDTH), lambda b: (0, 0)), _ANY],
        out_specs=[pl.BlockSpec((T, POOL_WIDTH), lambda b: (b, pcol)),
                   pl.BlockSpec((4, POOL_GROUP_DIM, POOL_GROUP_DIM), lambda b: (0, 0, 0)),
                   pl.BlockSpec((1, POOL_WIDTH), lambda b: (0, 0))],
        out_shape=[jax.ShapeDtypeStruct(dz.shape, dz.dtype),
                   jax.ShapeDtypeStruct((4, POOL_GROUP_DIM, POOL_GROUP_DIM), F32),
                   jax.ShapeDtypeStruct((1, POOL_WIDTH), F32)],
        input_output_aliases={5: 0},
        compiler_params=_cp(("arbitrary",)),
    )(dcat, dcat, pooled, w_pool, scale, dz)


ATT_STRIP = 32
ATT_Q = 256
ATT_W = ATT_Q + LEFT_CHUNKS * CHUNK


def _rel_index():
    j = np.arange(ATT_W)
    rel = np.clip(LEFT_CHUNKS * CHUNK - j, -REL_CLIP, REL_CLIP) + REL_CLIP
    fwd = np.where(j < BAND, rel, N_REL)
    bwd = np.where(j <= ATT_W - CHUNK, fwd, 2 * REL_CLIP)
    return tuple(jnp.asarray(v.reshape(1, ATT_W), jnp.int32) for v in (fwd, bwd))


def _bias_table(name, rel_bias, rel_idx):
    rb = jnp.concatenate([rel_bias, jnp.full((ATT_HEADS, 1), NEG_INF, F32),
                          jnp.zeros((ATT_HEADS, N_REL_PAD - N_REL - 1), F32)], axis=1)

    def body(rb_ref, idx_ref, o_ref, row0_ref):
        r = lax.broadcasted_iota(jnp.int32, (N_REL_PAD, ATT_W), 0)
        onehot = (r == idx_ref[...]).astype(F32)
        row0_ref[...] = jnp.dot(rb_ref[...], onehot, precision=lax.Precision.HIGHEST, preferred_element_type=F32)
        col = lax.broadcasted_iota(jnp.int32, (CHUNK, ATT_W), 1)
        row = lax.broadcasted_iota(jnp.int32, (CHUNK, ATT_W), 0)
        for h in range(ATT_HEADS):
            same = jnp.broadcast_to(row0_ref[pl.ds(h, 1), :], (CHUNK, ATT_W))
            turned = pltpu.roll(same, 0, 1, stride=1, stride_axis=0)
            o_ref[h] = jnp.where(col >= BAND, NEG_INF, jnp.where(col < row, same, turned))

    return pl.pallas_call(
        body,
        name=name,
        out_shape=jax.ShapeDtypeStruct((ATT_HEADS, CHUNK, ATT_W), F32),
        scratch_shapes=[pltpu.VMEM((ATT_HEADS, ATT_W), F32)],
        compiler_params=pltpu.CompilerParams(vmem_limit_bytes=VMEM_LIMIT),
    )(rb, rel_idx)


def _bias_grad(name, dband, rel_idx):
    def body(d_ref, idx_ref, o_ref, sums_ref):
        row = lax.broadcasted_iota(jnp.int32, (CHUNK, ATT_W), 0)
        for h in range(ATT_HEADS):
            back = d_ref[h]
            for bit in range(CHUNK.bit_length() - 1):
                back = jnp.where(((row >> bit) & 1) == 1, pltpu.roll(back, ATT_W - (1 << bit), 1), back)
            sums_ref[pl.ds(h, 1), :] = jnp.sum(back, axis=0, keepdims=True)
        r = lax.broadcasted_iota(jnp.int32, (N_REL_PAD, ATT_W), 0)
        onehot = (r == idx_ref[...]).astype(F32)
        o_ref[...] = lax.dot_general(sums_ref[...], onehot, (((1,), (1,)), ((), ())),
                                     precision=lax.Precision.HIGHEST, preferred_element_type=F32)

    out = pl.pallas_call(
        body,
        name=name,
        out_shape=jax.ShapeDtypeStruct((ATT_HEADS, N_REL_PAD), F32),
        scratch_shapes=[pltpu.VMEM((ATT_HEADS, ATT_W), F32)],
        compiler_params=pltpu.CompilerParams(vmem_limit_bytes=VMEM_LIMIT),
    )(dband, rel_idx)
    return out[:, :N_REL]


def _attn_unit(q_ref, kw_ref, bias_ref, e, u, lane):
    mine = (lane < ATT_HEAD_DIM) if e == 0 else (lane >= ATT_HEAD_DIM)
    qm = jnp.where(mine, q_ref[u * ATT_Q:(u + 1) * ATT_Q, :] * QSCALE, 0)
    kw = kw_ref[u * ATT_Q:u * ATT_Q + ATT_W, :]
    s = _dot_nt(qm, kw) + bias_ref[u, e]
    p = jnp.exp(s - jnp.max(s, axis=-1, keepdims=True))
    return p, 1.0 / jnp.sum(p, axis=-1, keepdims=True), qm, kw, mine


def _attn_in_specs(nb):
    T = ATT_BLOCK
    hp = ATT_HEADS // 2
    cur = lambda off: pl.BlockSpec((T, LANE), lambda h, b: (jnp.minimum(b, nb - 1), off + h))
    prev = lambda off: pl.BlockSpec((T, LANE), lambda h, b: (jnp.clip(b - 1, 0, nb - 1), off + h))
    return [cur(0), prev(hp), cur(hp), prev(2 * hp), cur(2 * hp),
            pl.BlockSpec((None, 2, CHUNK, ATT_W), lambda h, b: (h, 0, 0, 0))]


def _spread_bias(bias_ref, bm_ref, block):
    col = lax.broadcasted_iota(jnp.int32, (CHUNK, ATT_W), 1)
    for first in (True, False):
        @pl.when(block == (0 if first else 1))
        def _(first=first):
            for u in range(ATT_BLOCK // ATT_Q):
                for e in range(2):
                    for j in range(ATT_Q // CHUNK):
                        rows = pltpu.roll(bias_ref[e], j * CHUNK, 1)
                        if first:
                            rows = jnp.where(col >= ATT_BLOCK - u * ATT_Q, rows, NEG_INF)
                        bm_ref[u, e, j * CHUNK:(j + 1) * CHUNK, :] = rows


def _riding(rider, body, n_in, n_out, grid):
    if rider is None:
        return body, [], [], [], [], {}
    n_rin, n_rout = len(rider.operands), len(rider.out_shapes)
    steps = int(np.prod(grid))

    def riding(*refs):
        ins, rin = refs[:n_in], refs[n_in:n_in + n_rin]
        outs = refs[n_in + n_rin:n_in + n_rin + n_out]
        rout = refs[n_in + n_rin + n_out:n_in + n_rin + n_out + n_rout]
        rest = refs[n_in + n_rin + n_out + n_rout:]
        scratch, sems = rest[:-2], rest[-2:]
        step_no = pl.program_id(0)
        for axis in range(1, len(grid)):
            step_no = step_no * grid[axis] + pl.program_id(axis)

        @pl.when(step_no == 0)
        def _():
            rider.start(rin, rout, *sems)

        body(*ins, *outs, *scratch)
        if rider.relay is not None:
            assert steps >= 3

            @pl.when(step_no == steps - 2)
            def _():
                rider.relay(rin, rout, *sems)

        @pl.when(step_no == steps - 1)
        def _():
            rider.finish(rin, rout, *sems)

    sems = [pltpu.SemaphoreType.DMA((rider.n_sems,)), pltpu.SemaphoreType.DMA((rider.n_sems,))]
    aliases = {n_in + src: n_out + dst for src, dst in rider.aliases}
    return riding, [_ANY] * n_rin, [_ANY] * n_rout, list(rider.out_shapes), sems, aliases


def _attn_fwd(name, qkv, bias, rider=None):
    S = qkv.shape[0]
    T = ATT_BLOCK
    nb = S // T

    def body(q_ref, kp_ref, kc_ref, vp_ref, vc_ref, band_ref, o_ref, kw_ref, vw_ref, bias_ref, s_ref, p_ref, inv_ref):
        _spread_bias(band_ref, bias_ref, pl.program_id(1))
        kw_ref[0:T] = kp_ref[...]
        kw_ref[T:2 * T] = kc_ref[...]
        vw_ref[0:T] = vp_ref[...]
        vw_ref[T:2 * T] = vc_ref[...]
        lane = lax.broadcasted_iota(jnp.int32, (ATT_Q, LANE), 1)
        for u in range(T // ATT_Q):
            vw = vw_ref[u * ATT_Q:u * ATT_Q + ATT_W, :]
            kw = kw_ref[u * ATT_Q:u * ATT_Q + ATT_W, :]
            outs = []
            for e in range(2):
                mine = (lane < ATT_HEAD_DIM) if e == 0 else (lane >= ATT_HEAD_DIM)
                qm = jnp.where(mine, q_ref[u * ATT_Q:(u + 1) * ATT_Q, :] * QSCALE, 0)
                s_ref[e] = _dot_nt(qm, kw)
                for r in range(ATT_Q // ATT_STRIP):
                    rows = slice(r * ATT_STRIP, (r + 1) * ATT_STRIP)
                    s = s_ref[e, rows, :] + bias_ref[u, e, rows, :]
                    p = jnp.exp(s - jnp.max(s, axis=-1, keepdims=True))
                    inv_ref[e, rows, :] = jnp.broadcast_to(1.0 / jnp.sum(p, axis=-1, keepdims=True), (ATT_STRIP, LANE))
                    p_ref[e, rows, :] = p.astype(BF16)
                outs.append(_dot(p_ref[e], vw) * inv_ref[e])
            o_ref[u * ATT_Q:(u + 1) * ATT_Q, :] = jnp.where(lane < ATT_HEAD_DIM, outs[0], outs[1]).astype(o_ref.dtype)

    grid = (ATT_HEADS // 2, nb)
    body, r_in, r_out, r_shapes, r_sems, aliases = _riding(rider, body, 6, 1, grid)
    outs = pl.pallas_call(
        body,
        name=name,
        grid=grid,
        in_specs=_attn_in_specs(nb) + r_in,
        out_specs=[pl.BlockSpec((T, LANE), lambda h, b: (b, h))] + r_out,
        out_shape=[jax.ShapeDtypeStruct((S, D_MODEL), BF16)] + r_shapes,
        input_output_aliases=aliases,
        scratch_shapes=[pltpu.VMEM((2 * T, LANE), BF16), pltpu.VMEM((2 * T, LANE), BF16),
                        pltpu.VMEM((T // ATT_Q, 2, ATT_Q, ATT_W), F32), pltpu.VMEM((2, ATT_Q, ATT_W), F32),
                        pltpu.VMEM((2, ATT_Q, ATT_W), BF16), pltpu.VMEM((2, ATT_Q, LANE), F32)] + r_sems,
        compiler_params=_cp(("parallel" if rider is None else "arbitrary", "arbitrary")),
    )(qkv, qkv, qkv, qkv, qkv, bias, *(rider.operands if rider is not None else ()))
    return outs[0] if rider is None else (outs[0], list(outs[1:]))


def _attn_bwd(name, qkv, bias, do, rider=None):
    S = qkv.shape[0]
    T = ATT_BLOCK
    nb = S // T

    def body(q_ref, kp_ref, kc_ref, vp_ref, vc_ref, band_ref, do_ref,
             dq_ref, dk_ref, dv_ref, dband_ref, kw_ref, vw_ref, dkw_ref, dvw_ref, bias_ref, dbias_ref):
        b = pl.program_id(1)

        _spread_bias(band_ref, bias_ref, b)

        @pl.when(b == 0)
        def _():
            dbias_ref[...] = jnp.zeros_like(dbias_ref)
            dkw_ref[:, T:2 * T] = jnp.zeros((LANE, T), F32)
            dvw_ref[:, T:2 * T] = jnp.zeros((LANE, T), F32)

        dkw_ref[:, 0:T] = dkw_ref[:, T:2 * T]
        dvw_ref[:, 0:T] = dvw_ref[:, T:2 * T]
        dkw_ref[:, T:2 * T] = jnp.zeros((LANE, T), F32)
        dvw_ref[:, T:2 * T] = jnp.zeros((LANE, T), F32)

        @pl.when(b < nb)
        def _():
            kw_ref[0:T] = kp_ref[...]
            kw_ref[T:2 * T] = kc_ref[...]
            vw_ref[0:T] = vp_ref[...]
            vw_ref[T:2 * T] = vc_ref[...]
            lane = lax.broadcasted_iota(jnp.int32, (ATT_Q, LANE), 1)
            for u in range(T // ATT_Q):
                rows = slice(u * ATT_Q, (u + 1) * ATT_Q)
                win = slice(u * ATT_Q, u * ATT_Q + ATT_W)
                vw = vw_ref[win, :]
                do2 = do_ref[rows, :]
                dqs, dk, dv = [], None, None
                for e in range(2):
                    p, inv, qm, kw, mine = _attn_unit(q_ref, kw_ref, bias_ref, e, u, lane)
                    dom = jnp.where(mine, do2, 0)
                    dp = _dot_nt(dom, vw)
                    delta = jnp.sum(p * dp, axis=-1, keepdims=True) * inv
                    ds = p * ((dp - delta) * inv)
                    dbias_ref[e] += ds
                    dsb = ds.astype(BF16)
                    dqs.append(_dot(dsb, kw))
                    dk_e = _dot_tn(qm, dsb)
                    dv_e = _dot_tn((dom * inv).astype(BF16), p.astype(BF16))
                    dk = dk_e if dk is None else dk + dk_e
                    dv = dv_e if dv is None else dv + dv_e
                dq_ref[rows, :] = (jnp.where(lane < ATT_HEAD_DIM, dqs[0], dqs[1]) * QSCALE).astype(dq_ref.dtype)
                dkw_ref[:, win] += dk
                dvw_ref[:, win] += dv

        @pl.when(b > 0)
        def _():
            dk_ref[...] = dkw_ref[:, 0:T].T.astype(dk_ref.dtype)
            dv_ref[...] = dvw_ref[:, 0:T].T.astype(dv_ref.dtype)

        @pl.when(b == nb)
        def _():
            for e in range(2):
                acc = dbias_ref[e, 0:CHUNK, :]
                for j in range(1, ATT_Q // CHUNK):
                    acc = acc + pltpu.roll(dbias_ref[e, j * CHUNK:(j + 1) * CHUNK, :], ATT_W - j * CHUNK, 1)
                dband_ref[e] = acc

    tok = jax.ShapeDtypeStruct((S, D_MODEL), BF16)
    prev_out = pl.BlockSpec((T, LANE), lambda h, b: (jnp.maximum(b - 1, 0), h))
    grid = (ATT_HEADS // 2, nb + 1)
    body, r_in, r_out, r_shapes, r_sems, aliases = _riding(rider, body, 7, 4, grid)
    outs = pl.pallas_call(
        body,
        name=name,
        grid=grid,
        in_specs=_attn_in_specs(nb) + [pl.BlockSpec((T, LANE), lambda h, b: (jnp.minimum(b, nb - 1), h))] + r_in,
        out_specs=[pl.BlockSpec((T, LANE), lambda h, b: (jnp.minimum(b, nb - 1), h)), prev_out, prev_out,
                   pl.BlockSpec((None, 2, CHUNK, ATT_W), lambda h, b: (h, 0, 0, 0))] + r_out,
        out_shape=[tok, tok, tok, jax.ShapeDtypeStruct((ATT_HEADS // 2, 2, CHUNK, ATT_W), F32)] + r_shapes,
        input_output_aliases=aliases,
        scratch_shapes=[pltpu.VMEM((2 * T, LANE), BF16), pltpu.VMEM((2 * T, LANE), BF16),
                        pltpu.VMEM((LANE, 2 * T), F32), pltpu.VMEM((LANE, 2 * T), F32),
                        pltpu.VMEM((T // ATT_Q, 2, ATT_Q, ATT_W), F32), pltpu.VMEM((2, ATT_Q, ATT_W), F32)] + r_sems,
        compiler_params=_cp(("parallel" if rider is None else "arbitrary", "arbitrary")),
    )(qkv, qkv, qkv, qkv, qkv, bias, do, *(rider.operands if rider is not None else ()))
    return tuple(outs[:4]) if rider is None else (tuple(outs[:4]), list(outs[4:]))


def _local_step(x, target, small, W):
    S = x.shape[0]
    tb = _ret_tables(S)
    rel_fwd, rel_bwd = _rel_index()
    saved = []
    normed = (("tile", BF16),)
    deep = dict(tm=512, tk=D_FF)
    h = W.hosted("mix_norm_fwd0", lambda rider: _rms_fwd("mix_norm_fwd0", x, small["mix_norm"][0:1], rider=rider))
    for layer in range(DEPTH):
        i = layer // 2
        st = {"x_in": x, "h": h}
        g_ffn = small["ffn_norm"][layer:layer + 1]
        if layer % 2 == 0:
            z = W.mm(f"ab_in_fwd{layer}", "nn", h, W.get("ab_w_in", i), tm=2048, tn=640, out_dtype=BF16)
            gain = small["ab_gn_gain"][i:i + 1]
            cat, opre, states = W.hosted(f"ret_fwd{layer}", lambda rider: _ret_fwd(f"ret_fwd{layer}", z, tb, gain, rider=rider))
            cat, pooled = _pool_fwd(f"pool_fwd{layer}", z, small["ab_w_pool"][i], small["ab_pool_scale"][i:i + 1], cat)
            st.update(z=z, opre=opre, states=states, pooled=pooled, cat=cat)
            x, hn = W.mm(f"ab_out_fwd{layer}", "nn", cat, W.get("ab_w_out", i), extras=(x,), aux=(g_ffn,), sides=normed,
                         epi=_epi_residual_norm)
        else:
            qkv = W.mm(f"qkv_fwd{layer}", "nn", h, W.get("c_w_qkv", i), tm=2048, out_dtype=BF16)
            bias = _bias_table(f"bias_table{layer}", small["c_rel_bias"][i], rel_fwd)
            bias = bias.reshape(ATT_HEADS // 2, 2, CHUNK, ATT_W)
            att = W.hosted(f"attn_fwd{layer}", lambda rider: _attn_fwd(f"attn_fwd{layer}", qkv, bias, rider=rider))
            st.update(qkv=qkv, bias=bias, att=att)
            x, hn = W.mm(f"c_out_fwd{layer}", "nn", att, W.get("c_w_out", i), extras=(x,), aux=(g_ffn,), sides=normed,
                         epi=_epi_residual_norm)
        st["x_mid"] = x
        u = W.mm(f"ffn_in_fwd{layer}", "nn", hn, W.get("w_ffn_in", layer), out_dtype=BF16, tm=2048)
        if layer + 1 < DEPTH:
            x, h = W.mm(f"ffn_out_fwd{layer}", "nn", u, W.get("w_ffn_out", layer), a_fn=_relu2, extras=(x,),
                        aux=(small["mix_norm"][layer + 1:layer + 2],), sides=normed, epi=_epi_residual_norm, **deep)
        else:
            x = W.mm(f"ffn_out_fwd{layer}", "nn", u, W.get("w_ffn_out", layer), a_fn=_relu2, extras=(x,),
                     epi=_epi_residual, **deep)
        st.update(hn=hn, u=u)
        saved.append(st)

    loss, dx, dxb, d_final = _loss_head(x, small["final_norm"].reshape(1, D_MODEL), target)

    gs = {k: [None] * v.shape[0] for k, v in small.items() if k != "final_norm"}
    gb = {k: None for k in W.n_layers}
    landed = {k: None for k in W.n_layers}
    pending = []

    def carry(call, take=1):
        items = [pending.pop(0) for _ in range(min(take, len(pending)))]
        if not items:
            return call(None)
        riders = [_grad_rider(key, idx, gb[key], landed[key]) for key, idx in items]
        res, outs = call(_join_riders(riders))
        for (key, _), out in zip(items, outs):
            landed[key] = out
        return res

    def host(name, *args, take=1, **kw):
        return carry(lambda rider: _mm(name, *args, rider=rider, **kw), take)

    def dw(name, key, idx, a, b, call=_mm, **kw):
        gb[key] = call(name, "tn", a, b, stack=(W.n_layers[key], idx, gb[key]), out_dtype=BF16, **kw)
        pending.append((key, idx))

    gain_sums = (("tile", BF16), ("colsum", F32))
    for layer in reversed(range(DEPTH)):
        i = layer // 2
        st = saved[layer]
        du = host(f"ffn_out_bwd{layer}", "nt", dxb, W.get("w_ffn_out", layer), extras=(st["u"],),
                  epi=lambda acc, u: acc * (2.0 * jnp.maximum(u, 0).astype(F32)), out_dtype=BF16, tm=2048)
        dw(f"ffn_out_dw{layer}", "w_ffn_out", layer, st["u"], dxb, a_fn=_relu2, tk=2048)
        dx, dxb, dgain = host(f"ffn_in_bwd{layer}", "nt", du, W.get("w_ffn_in", layer), extras=(st["x_mid"], dx),
                         aux=(small["ffn_norm"][layer:layer + 1],), sides=gain_sums, epi=_epi_rms_bwd, **deep)
        gs["ffn_norm"][layer] = dgain[0:1]
        dw(f"ffn_in_dw{layer}", "w_ffn_in", layer, st["hn"], du, tk=2048)
        norm_bwd = dict(extras=(st["x_in"], dx), aux=(small["mix_norm"][layer:layer + 1],), sides=gain_sums,
                        epi=_epi_rms_bwd)
        if layer % 2 == 0:
            dcat = _mm(f"ab_out_bwd{layer}", "nt", dxb, W.get("ab_w_out", i), out_dtype=BF16)
            dw(f"ab_out_dw{layer}", "ab_w_out", i, st["cat"], dxb, tk=2048)
            gain = small["ab_gn_gain"][i:i + 1]
            dz, gs["ab_gn_gain"][i] = carry(lambda rider: _ret_bwd(f"ret_bwd{layer}", st["z"], tb, gain, st["opre"],
                                                                   st["states"], dcat, rider=rider), take=len(pending))
            dz, gs["ab_w_pool"][i], gs["ab_pool_scale"][i] = _pool_bwd(
                f"pool_bwd{layer}", st["pooled"], small["ab_w_pool"][i], small["ab_pool_scale"][i:i + 1], dcat, dz)
            if layer == 0:
                dw(f"ab_in_dw{layer}", "ab_w_in", i, st["h"], dz, call=host, tn=640, tk=2048)
            dx, dxb, dgain = host(f"ab_in_bwd{layer}", "nt", dz, W.get("ab_w_in", i), tm=512, tk=AB_IN_WIDTH,
                             take=len(pending) if layer == 0 else 1, **norm_bwd)
            if layer > 0:
                dw(f"ab_in_dw{layer}", "ab_w_in", i, st["h"], dz, call=host, tn=640, tk=2048)
        else:
            datt = _mm(f"c_out_bwd{layer}", "nt", dxb, W.get("c_w_out", i), out_dtype=BF16)
            dw(f"c_out_dw{layer}", "c_w_out", i, st["att"], dxb, tk=2048)
            dq, dk, dv, dbias = carry(lambda rider: _attn_bwd(f"attn_bwd{layer}", st["qkv"], st["bias"], datt, rider=rider),
                                      take=len(pending))
            gs["c_rel_bias"][i] = _bias_grad(f"bias_grad{layer}", dbias.reshape(ATT_HEADS, CHUNK, ATT_W), rel_bwd)
            dqkv = [dq, dk, dv]
            dx, dxb, dgain = host(f"qkv_bwd{layer}", "nt", dqkv, W.get("c_w_qkv", i), tm=512, tk=3 * D_MODEL, **norm_bwd)
            dw(f"qkv_dw{layer}", "c_w_qkv", i, st["h"], dqkv, call=host, tk=2048)
        gs["mix_norm"][layer] = dgain[0:1]
    for key, idx in pending:
        landed[key], = _run_rider(f"grad_exchange_{key}{idx}", _grad_rider(key, idx, gb[key], landed[key]))

    g_small = {
        "mix_norm": jnp.concatenate(gs["mix_norm"], axis=0),
        "ffn_norm": jnp.concatenate(gs["ffn_norm"], axis=0),
        "ab_gn_gain": jnp.concatenate(gs["ab_gn_gain"], axis=0),
        "ab_w_pool": jnp.stack(gs["ab_w_pool"], axis=0),
        "ab_pool_scale": jnp.concatenate(gs["ab_pool_scale"], axis=0),
        "c_rel_bias": jnp.stack(gs["c_rel_bias"], axis=0),
        "final_norm": d_final.reshape(D_MODEL),
    }
    return loss, dx, g_small, gb, landed


_BIG = ("w_ffn_in", "w_ffn_out", "ab_w_in", "ab_w_out", "c_w_qkv", "c_w_out")
_SHARD_AXIS = {"w_ffn_in": 2, "w_ffn_out": 1, "ab_w_in": 2, "ab_w_out": 1, "c_w_qkv": 2, "c_w_out": 1}
_SMALL = ("mix_norm", "ffn_norm", "ab_gn_gain", "ab_w_pool", "ab_pool_scale", "c_rel_bias", "final_norm")


def _place():
    x, y, c = lax.axis_index("x"), lax.axis_index("y"), lax.axis_index("c")
    chips = [(1 - x, y), (x, 1 - y), (1 - x, 1 - y)]
    return x, y, c, chips


def _sub(ref, axis, start, size):
    idx = [slice(None)] * len(ref.shape)
    idx[axis] = pl.ds(pl.multiple_of(start, LANE), size)
    return ref.at[tuple(idx)]


def _gather_rider(items, shards):
    keys = sorted({k for k, _ in items})
    n = len(items)
    axes = [_SHARD_AXIS[k] - 1 for k, _ in items]
    sizes = [shards[k].shape[a + 1] for (k, _), a in zip(items, axes)]
    hsizes = [shards[k].shape[2 - a] // 2 for (k, _), a in zip(items, axes)]

    def views(ins, outs, send_sems, recv_sems):
        x, y, c, chips = _place()
        srcs = [ins[keys.index(k)].at[l] for k, l in items]

        def remote(src, dst, s, to):
            return pltpu.make_async_remote_copy(src_ref=src, dst_ref=dst, send_sem=send_sems.at[s],
                                                recv_sem=recv_sems.at[s], device_id=to, device_id_type=MESH)

        def half(w, chip, core):
            return _sub(_sub(outs[w], axes[w], chip * sizes[w], sizes[w]), 1 - axes[w], core * hsizes[w], hsizes[w])

        me = 2 * x + y
        local = [pltpu.make_async_copy(srcs[w], _sub(outs[w], axes[w], me * sizes[w], sizes[w]), send_sems.at[6 * n + w])
                 for w in range(n)]
        first = [remote(_sub(srcs[w], 1 - axes[w], c * hsizes[w], hsizes[w]), half(w, me, c), w * 6 + k, (px, py, c))
                 for w in range(n) for k, (px, py) in enumerate(chips)]
        return x, y, c, chips, remote, half, local, first

    def start(ins, outs, send_sems, recv_sems):
        *_, local, first = views(ins, outs, send_sems, recv_sems)
        for cp in local + first:
            cp.start()

    def passes(x, y, c, chips, remote, half):
        return [remote(half(w, 2 * px + py, c), half(w, 2 * px + py, c), w * 6 + 3 + k, (x, y, 1 - c))
                for w in range(n) for k, (px, py) in enumerate(chips)]

    def relay(ins, outs, send_sems, recv_sems):
        x, y, c, chips, remote, half, _, _ = views(ins, outs, send_sems, recv_sems)
        for w in range(n):
            for k, (px, py) in enumerate(chips):
                landed = half(w, 2 * px + py, c)
                remote(landed, landed, w * 6 + k, (px, py, c)).wait_recv()
        for cp in passes(x, y, c, chips, remote, half):
            cp.start()

    def finish(ins, outs, send_sems, recv_sems):
        x, y, c, chips, remote, half, local, first = views(ins, outs, send_sems, recv_sems)
        for w in range(n):
            for k, (px, py) in enumerate(chips):
                theirs = half(w, 2 * px + py, 1 - c)
                remote(theirs, theirs, w * 6 + 3 + k, (x, y, 1 - c)).wait_recv()
        for cp in first + passes(x, y, c, chips, remote, half):
            cp.wait_send()
        for cp in local:
            cp.wait()

    def full(k, a):
        shape = list(shards[k].shape[1:])
        shape[a] *= N_CHIPS
        return jax.ShapeDtypeStruct(tuple(shape), shards[k].dtype)

    return _Rider(tuple(shards[k] for k in keys), tuple(full(k, a) for (k, _), a in zip(items, axes)), 7 * n, start, finish,
                  relay=relay)


def _mixer_items(layer):
    names = ("ab_w_in", "ab_w_out") if layer % 2 == 0 else ("c_w_qkv", "c_w_out")
    return [(k, layer // 2) for k in names]


class _Weights:
    def __init__(self, shards):
        self.shards = shards
        self.n_layers = {k: shards[k].shape[0] for k in _BIG}
        self.full = {}
        first, second = _mixer_items(0)
        self.plan = {"mix_norm_fwd0": [first], "ab_in_fwd0": [second, ("w_ffn_in", 0)], "ret_fwd0": [("w_ffn_out", 0)]}
        for layer in range(1, DEPTH):
            if layer % 2 == 0:
                self.plan[f"attn_fwd{layer - 1}"] = _mixer_items(layer) + [("w_ffn_in", layer), ("w_ffn_out", layer)]
            else:
                self.plan[f"ffn_in_fwd{layer - 1}"] = _mixer_items(layer)
                self.plan[f"ffn_out_fwd{layer - 1}"] = [("w_ffn_in", layer)]
                self.plan[f"qkv_fwd{layer}"] = [("w_ffn_out", layer)]

    def _take(self, items, outs):
        self.full.update(zip(items, outs))

    def get(self, name, layer):
        return self.full[(name, layer)]

    def hosted(self, name, call):
        items = self.plan.get(name)
        if items is None:
            return call(None)
        res, outs = call(_gather_rider(items, self.shards))
        self._take(items, outs)
        return res

    def mm(self, name, *args, **kw):
        return self.hosted(name, lambda rider: _mm(name, *args, rider=rider, **kw))


def _run_rider(name, rider):
    n_in, n_out = len(rider.operands), len(rider.out_shapes)

    def body(*refs):
        ins, outs, sems = refs[:n_in], refs[n_in:n_in + n_out], refs[n_in + n_out:]
        rider.start(ins, outs, *sems)
        if rider.relay is not None:
            rider.relay(ins, outs, *sems)
        rider.finish(ins, outs, *sems)

    return pl.pallas_call(
        body,
        name=name,
        in_specs=[_ANY] * n_in,
        out_specs=[_ANY] * n_out,
        out_shape=list(rider.out_shapes),
        input_output_aliases=dict(rider.aliases),
        scratch_shapes=[pltpu.SemaphoreType.DMA((rider.n_sems,)), pltpu.SemaphoreType.DMA((rider.n_sems,))],
        compiler_params=pltpu.CompilerParams(has_side_effects=True),
    )(*rider.operands)


def _grad_rider(name, layer, grad, landing):
    axis = _SHARD_AXIS[name] - 1
    L, R, C = grad.shape
    shard = (R // N_CHIPS, C) if axis == 0 else (R, C // N_CHIPS)
    size = shard[axis]

    def copies(ins, outs, send_sems, recv_sems):
        x, y, c, chips = _place()
        return [pltpu.make_async_remote_copy(
            src_ref=_sub(ins[0].at[layer], axis, (2 * px + py) * size, size), dst_ref=outs[0].at[layer, k],
            send_sem=send_sems.at[k], recv_sem=recv_sems.at[k], device_id=(px, py, c), device_id_type=MESH)
            for k, (px, py) in enumerate(chips)]

    def start(ins, outs, send_sems, recv_sems):
        for cp in copies(ins, outs, send_sems, recv_sems):
            cp.start()

    def finish(ins, outs, send_sems, recv_sems):
        cps = copies(ins, outs, send_sems, recv_sems)
        for cp in cps:
            cp.wait_recv()
        for cp in cps:
            cp.wait_send()

    out = jax.ShapeDtypeStruct((L, 3) + shard, grad.dtype)
    if landing is None:
        return _Rider((grad,), (out,), 3, start, finish)
    return _Rider((grad, landing), (out,), 3, start, finish, aliases=((1, 0),))


def _join_riders(riders):
    if len(riders) == 1:
        return riders[0]

    def parts(ins, outs, send_sems, recv_sems):
        i0 = o0 = s0 = 0
        for r in riders:
            ni, no = len(r.operands), len(r.out_shapes)
            yield (r, ins[i0:i0 + ni], outs[o0:o0 + no], send_sems.at[pl.ds(s0, r.n_sems)],
                   recv_sems.at[pl.ds(s0, r.n_sems)])
            i0, o0, s0 = i0 + ni, o0 + no, s0 + r.n_sems

    def start(*refs):
        for r, *own in parts(*refs):
            r.start(*own)

    def finish(*refs):
        for r, *own in parts(*refs):
            r.finish(*own)

    aliases, i0, o0 = [], 0, 0
    for r in riders:
        aliases += [(i0 + src, o0 + dst) for src, dst in r.aliases]
        i0, o0 = i0 + len(r.operands), o0 + len(r.out_shapes)
    return _Rider(tuple(x for r in riders for x in r.operands), tuple(x for r in riders for x in r.out_shapes),
                  sum(r.n_sems for r in riders), start, finish, tuple(aliases))


def _rows_tile(rows, cols):
    tr = rows
    while tr * cols > (1 << 19) and tr % 16 == 0:
        tr //= 2
    return tr


def _chip_sum(name, grad, landed, chip, saxis):
    L = grad.shape[0]
    _, _, R, C = landed.shape
    tr = _rows_tile(R, C)
    nr = R // tr
    if saxis == 2:
        g_idx = lambda l, i, s: (l, i, s[0])
    else:
        g_idx = lambda l, i, s: (l, s[0] * nr + i, 0)

    def body(s_ref, g_ref, l_ref, o_ref):
        tot = ((g_ref[...].astype(F32) + l_ref[0].astype(F32)) + l_ref[1].astype(F32)) + l_ref[2].astype(F32)
        o_ref[...] = tot.astype(o_ref.dtype)

    return pl.pallas_call(
        body,
        name=name,
        grid_spec=pltpu.PrefetchScalarGridSpec(
            num_scalar_prefetch=1,
            grid=(L, nr),
            in_specs=[pl.BlockSpec((None, tr, C), g_idx), pl.BlockSpec((None, 3, tr, C), lambda l, i, s: (l, 0, i, 0))],
            out_specs=pl.BlockSpec((None, tr, C), lambda l, i, s: (l, i, 0)),
        ),
        out_shape=jax.ShapeDtypeStruct((L, R, C), BF16),
        compiler_params=_cp(("parallel", "parallel")),
    )(chip, grad, landed)


def _all_reduce_small(packed, sums):
    R = packed.shape[0]
    n = len(sums)

    def body(p_ref, *refs):
        s_in, o_ref, s_out = refs[:n], refs[n], refs[n + 1:2 * n + 1]
        land_ref, send_sems, recv_sems = refs[2 * n + 1:]
        x, y, c, _ = _place()
        me = 4 * x + 2 * y + c
        swaps = [pltpu.make_async_remote_copy(src_ref=s_in[w], dst_ref=s_out[w], send_sem=send_sems.at[N_DEV - 1 + w],
                                              recv_sem=recv_sems.at[N_DEV - 1 + w], device_id=(x, y, 1 - c),
                                              device_id_type=MESH) for w in range(n)]
        for cp in swaps:
            cp.start()
        sends, recvs = [], []
        for r in range(1, N_DEV):
            px, py, pc = x ^ (r >> 2), y ^ ((r >> 1) & 1), c ^ (r & 1)
            cp = pltpu.make_async_remote_copy(src_ref=p_ref, dst_ref=land_ref.at[me], send_sem=send_sems.at[r - 1],
                                              recv_sem=recv_sems.at[r - 1], device_id=(px, py, pc), device_id_type=MESH)
            cp.start()
            sends.append(cp)
            recvs.append(pltpu.make_async_remote_copy(src_ref=p_ref, dst_ref=land_ref.at[4 * px + 2 * py + pc],
                                                      send_sem=send_sems.at[r - 1], recv_sem=recv_sems.at[r - 1],
                                                      device_id=(px, py, pc), device_id_type=MESH))
        land_ref[me] = p_ref[...]
        for cp in recvs:
            cp.wait_recv()
        for cp in sends:
            cp.wait_send()
        acc = land_ref[0]
        for d in range(1, N_DEV):
            acc = acc + land_ref[d]
        o_ref[...] = acc
        for cp in swaps:
            cp.wait_recv()
        for cp in swaps:
            cp.wait_send()

    vm = pl.BlockSpec(memory_space=pltpu.VMEM)
    outs = pl.pallas_call(
        body,
        name="all_reduce_small",
        in_specs=[vm] + [_ANY] * n,
        out_specs=[vm] + [_ANY] * n,
        out_shape=[jax.ShapeDtypeStruct((R, LANE), F32)] + [jax.ShapeDtypeStruct(s.shape, s.dtype) for s in sums],
        scratch_shapes=[pltpu.VMEM((N_DEV, R, LANE), F32), pltpu.SemaphoreType.DMA((N_DEV - 1 + n,)),
                        pltpu.SemaphoreType.DMA((N_DEV - 1 + n,))],
        compiler_params=pltpu.CompilerParams(has_side_effects=True, vmem_limit_bytes=VMEM_LIMIT),
    )(packed, *sums)
    return outs[0], list(outs[1:])


def _adamw(name, w, m, v, grads):
    R, C = w.shape
    tr = _rows_tile(R, C)
    c1 = 1.0 - ADAM_B1 ** ADAM_STEP
    c2 = 1.0 - ADAM_B2 ** ADAM_STEP
    ng = len(grads)

    def body(*refs):
        w_ref, m_ref, v_ref = refs[:3]
        g_refs = refs[3:3 + ng]
        g_ref, d_ref, nm_ref, nv_ref = refs[3 + ng:]
        gv = g_refs[0][...].astype(F32)
        for r in g_refs[1:]:
            gv = gv + r[...].astype(F32)
        g_ref[...] = gv
        nm = ADAM_B1 * m_ref[...] + (1.0 - ADAM_B1) * gv
        nv = ADAM_B2 * v_ref[...] + (1.0 - ADAM_B2) * (gv * gv)
        nm_ref[...] = nm
        nv_ref[...] = nv
        d_ref[...] = -ADAM_LR * ((nm / c1) / (jnp.sqrt(nv / c2) + ADAM_EPS) + ADAM_WD * w_ref[...])

    blk = pl.BlockSpec((tr, C), lambda i: (i, 0))
    out = jax.ShapeDtypeStruct((R, C), F32)
    return pl.pallas_call(
        body,
        name=name,
        grid=(R // tr,),
        in_specs=[blk] * (3 + ng),
        out_specs=[blk] * 4,
        out_shape=[out] * 4,
        compiler_params=_cp(("parallel",)),
    )(w, m, v, *grads)


def _pack(parts):
    rows = []
    for p in parts:
        flat = p.reshape(-1).astype(F32)
        n = -(-flat.shape[0] // (8 * LANE)) * (8 * LANE)
        rows.append(jnp.pad(flat, (0, n - flat.shape[0])).reshape(n // LANE, LANE))
    return jnp.concatenate(rows, axis=0)


def _unpack(packed, like):
    out, r = [], 0
    for p in like:
        size = int(np.prod(p.shape))
        n = -(-size // (8 * LANE)) * 8
        out.append(packed[r:r + n].reshape(-1)[:size].reshape(p.shape))
        r += n
    return out


def kernel(x, mix_norm, ffn_norm, w_ffn_in, w_ffn_out, ab_w_in, ab_gn_gain, ab_w_pool, ab_pool_scale, ab_w_out, c_w_qkv, c_rel_bias, c_w_out, final_norm, loss_target, m_mix_norm, m_ffn_norm, m_w_ffn_in, m_w_ffn_out, m_ab_w_in, m_ab_gn_gain, m_ab_w_pool, m_ab_pool_scale, m_ab_w_out, m_c_w_qkv, m_c_rel_bias, m_c_w_out, m_final_norm, v_mix_norm, v_ffn_norm, v_w_ffn_in, v_w_ffn_out, v_ab_w_in, v_ab_gn_gain, v_ab_w_pool, v_ab_pool_scale, v_ab_w_out, v_c_w_qkv, v_c_rel_bias, v_c_w_out, v_final_norm):
    w = dict(mix_norm=mix_norm, ffn_norm=ffn_norm, w_ffn_in=w_ffn_in, w_ffn_out=w_ffn_out, ab_w_in=ab_w_in,
             ab_gn_gain=ab_gn_gain, ab_w_pool=ab_w_pool, ab_pool_scale=ab_pool_scale, ab_w_out=ab_w_out,
             c_w_qkv=c_w_qkv, c_rel_bias=c_rel_bias, c_w_out=c_w_out, final_norm=final_norm)
    m = dict(mix_norm=m_mix_norm, ffn_norm=m_ffn_norm, w_ffn_in=m_w_ffn_in, w_ffn_out=m_w_ffn_out, ab_w_in=m_ab_w_in,
             ab_gn_gain=m_ab_gn_gain, ab_w_pool=m_ab_w_pool, ab_pool_scale=m_ab_pool_scale, ab_w_out=m_ab_w_out,
             c_w_qkv=m_c_w_qkv, c_rel_bias=m_c_rel_bias, c_w_out=m_c_w_out, final_norm=m_final_norm)
    v = dict(mix_norm=v_mix_norm, ffn_norm=v_ffn_norm, w_ffn_in=v_w_ffn_in, w_ffn_out=v_w_ffn_out, ab_w_in=v_ab_w_in,
             ab_gn_gain=v_ab_gn_gain, ab_w_pool=v_ab_w_pool, ab_pool_scale=v_ab_pool_scale, ab_w_out=v_ab_w_out,
             c_w_qkv=v_c_w_qkv, c_rel_bias=v_c_rel_bias, c_w_out=v_c_w_out, final_norm=v_final_norm)
    S = x.shape[1]
    cx, cy, cc = lax.axis_index("x"), lax.axis_index("y"), lax.axis_index("c")
    chip = jnp.reshape(2 * cx + cy, (1,)).astype(jnp.int32)

    big = _Weights({k: w[k].astype(BF16) for k in _BIG})
    small = {k: w[k] for k in _SMALL}
    loss, grad_x, g_small, g_big, landed = _local_step(x.reshape(S, D_MODEL), loss_target.reshape(S, D_MODEL), small, big)

    sums = [_chip_sum(f"chip_sum_{k}", g_big[k], landed[k], chip, _SHARD_AXIS[k]) for k in _BIG]

    packed, siblings = _all_reduce_small(_pack([g_small[k] for k in _SMALL] + [loss]), sums)
    small_like = [w[k] for k in _SMALL]
    g_red = dict(zip(_SMALL, _unpack(packed, small_like)))
    loss_row = packed.shape[0] - 8
    loss_out = packed[loss_row, 0]

    grad, delta, new_m, new_v = {}, {}, {}, {}
    for k, mine, theirs in zip(_BIG, sums, siblings):
        shp = w[k].shape
        two = (shp[0] * shp[1], shp[2])
        outs = _adamw(f"adamw_{k}", w[k].reshape(two), m[k].reshape(two), v[k].reshape(two),
                      (mine.reshape(two), theirs.reshape(two)))
        grad[k], delta[k], new_m[k], new_v[k] = [o.reshape(shp) for o in outs]
    _, d, nm, nv = _adamw("adamw_small", _pack(small_like), _pack([m[k] for k in _SMALL]), _pack([v[k] for k in _SMALL]),
                          (packed[:loss_row],))
    for k, dk, mk, vk in zip(_SMALL, _unpack(d, small_like), _unpack(nm, small_like), _unpack(nv, small_like)):
        grad[k], delta[k], new_m[k], new_v[k] = g_red[k], dk, mk, vk

    order = ("mix_norm", "ffn_norm", "w_ffn_in", "w_ffn_out", "ab_w_in", "ab_gn_gain", "ab_w_pool", "ab_pool_scale",
             "ab_w_out", "c_w_qkv", "c_rel_bias", "c_w_out", "final_norm")
    return (loss_out, grad_x.reshape(x.shape), *[grad[k] for k in order], *[delta[k] for k in order],
            *[new_m[k] for k in order], *[new_v[k] for k in order])
```

```python
import functools
from typing import Callable, NamedTuple

import numpy as np
import jax
import jax.numpy as jnp
from jax import lax
from jax.experimental import pallas as pl
from jax.experimental.pallas import tpu as pltpu

F32 = jnp.float32
BF16 = jnp.bfloat16

D_MODEL = 1024
D_FF = 4096
DEPTH = 4
CHUNK = 64
RMS_EPS = 1e-6
RET_WIDTH = 512
RET_HEADS = 4
RET_HEAD_DIM = 128
RET_ROPE_BASE = 10000.0
GN_EPS = 1e-5
POOL_WIDTH = 512
POOL_WINDOWS = (2, 4, 8, 16)
POOL_GROUP_DIM = 128
POOL_HALO = 16
AB_IN_WIDTH = 2560
ATT_HEADS = 16
ATT_HEAD_DIM = 64
LEFT_CHUNKS = 8
BAND = (LEFT_CHUNKS + 1) * CHUNK
REL_CLIP = 128
N_REL = 2 * REL_CLIP + 1
N_REL_PAD = 264
NEG_INF = -1e30
KSCALE = RET_HEAD_DIM ** -0.5
QSCALE = ATT_HEAD_DIM ** -0.5

ADAM_LR = 0.001
ADAM_B1 = 0.9
ADAM_B2 = 0.999
ADAM_EPS = 1e-08
ADAM_WD = 0.01
ADAM_STEP = 10

ATT_BLOCK = LEFT_CHUNKS * CHUNK
RET_BLOCK = 256
N_CHIPS = 4
N_DEV = 8
LANE = 128
VMEM_LIMIT = 52 * 1024 * 1024
EPI_ROWS = 256
MESH = pl.DeviceIdType.MESH


def _cp(sem, vmem=VMEM_LIMIT):
    return pltpu.CompilerParams(dimension_semantics=sem, vmem_limit_bytes=vmem)


def _dot(a, b):
    return lax.dot_general(a, b, (((1,), (0,)), ((), ())), preferred_element_type=F32)


def _dot_nt(a, b):
    return lax.dot_general(a, b, (((1,), (1,)), ((), ())), preferred_element_type=F32)


def _dot_tn(a, b):
    return lax.dot_general(a, b, (((0,), (0,)), ((), ())), preferred_element_type=F32)


_ANY = pl.BlockSpec(memory_space=pl.ANY)


class _Rider(NamedTuple):
    operands: tuple
    out_shapes: tuple
    n_sems: int
    start: Callable
    finish: Callable
    aliases: tuple = ()
    relay: Callable = None


def _mm(name, mode, a, b, *, la=None, lb=None, tm=1024, tn=1024, tk=1024, a_fn=None, b_fn=None,
        extras=(), aux=(), sides=(), epi=None, out_dtype=F32, stack=None, rider=None):
    a_parts = list(a) if isinstance(a, (list, tuple)) else [a]
    b_parts = list(b) if isinstance(b, (list, tuple)) else [b]
    na, nbp = len(a_parts), len(b_parts)
    a2, b2 = list(a_parts[0].shape[-2:]), list(b_parts[0].shape[-2:])
    a2[1] *= na
    b2[1] *= nbp
    if mode == "nn":
        (M, K), (K2, N) = a2, b2
    elif mode == "nt":
        (M, K), (N, K2) = a2, b2
    else:
        (K, M), (K2, N) = a2, b2
    assert K == K2, (name, a2, b2)
    tm, tn, tk = min(tm, M), min(tn, N), min(tk, K)
    assert M % tm == 0 and N % tn == 0 and K % tk == 0, (name, M, N, K, tm, tn, tk)
    gm, gn, gk = M // tm, N // tn, K // tk
    fold = mode == "nt" and na > 1 and gk == 1

    def specs(parts, block, idx, lead):
        per = parts[0].shape[-1] // block[1]
        assert parts[0].shape[-1] % block[1] == 0, (name, parts[0].shape, block)
        out = []
        for p in range(len(parts)):
            def f(i, j, k, p=p):
                r, c = idx(i, j, k)
                if len(parts) > 1:
                    c = jnp.clip(c - p * per, 0, per - 1)
                return (r, c) if lead is None else (lead, r, c)
            out.append(pl.BlockSpec(block if lead is None else (None,) + block, f))
        return out, per

    if mode == "nn":
        a_specs, a_per = specs(a_parts, (tm, tk), lambda i, j, k: (i, k), la)
        b_specs, b_per = specs(b_parts, (tk, tn), lambda i, j, k: (k, j), lb)
        a_axis, b_axis, dot = 2, 1, _dot
    elif mode == "nt":
        if fold:
            a_specs, a_per = [pl.BlockSpec((tm, K // na), lambda i, j, k: (i, 0)) for _ in a_parts], 1
        else:
            a_specs, a_per = specs(a_parts, (tm, tk), lambda i, j, k: (i, k), la)
        b_specs, b_per = specs(b_parts, (tn, tk), lambda i, j, k: (j, k), lb)
        a_axis, b_axis, dot = 2, 2, _dot_nt
    else:
        a_specs, a_per = specs(a_parts, (tk, tm), lambda i, j, k: (k, i), la)
        b_specs, b_per = specs(b_parts, (tk, tn), lambda i, j, k: (k, j), lb)
        a_axis, b_axis, dot = 0, 1, _dot_tn
    ex_specs = [pl.BlockSpec((tm, tn), lambda i, j, k: (i, j)) for _ in extras]
    n_ex = len(extras)

    n_aux, n_side = len(aux), len(sides)
    operands = a_parts + b_parts + list(extras) + list(aux)
    in_specs = a_specs + b_specs + ex_specs + [pl.BlockSpec(v.shape, lambda i, j, k, nd=v.ndim: (0,) * nd) for v in aux]
    aliases = {}
    if stack is None:
        out_specs = [pl.BlockSpec((tm, tn), lambda i, j, k: (i, j))]
        out_shapes = [jax.ShapeDtypeStruct((M, N), out_dtype)]
    else:
        n_layers, layer, prev = stack
        out_specs = [pl.BlockSpec((None, tm, tn), lambda i, j, k: (layer, i, j))]
        out_shapes = [jax.ShapeDtypeStruct((n_layers, M, N), out_dtype)]
        if prev is not None:
            aliases = {len(operands): 0}
            operands.append(prev)
            in_specs.append(_ANY)
    for kind, dtype in sides:
        if kind == "tile":
            out_specs.append(pl.BlockSpec((tm, tn), lambda i, j, k: (i, j)))
            out_shapes.append(jax.ShapeDtypeStruct((M, N), dtype))
        else:
            assert gn == 1, name
            out_specs.append(pl.BlockSpec((8, tn), lambda i, j, k: (0, 0)))
            out_shapes.append(jax.ShapeDtypeStruct((8, N), dtype))
    n_prev = len(aliases)
    scratch = [pltpu.VMEM((tm, tn), F32)] if gk > 1 else []
    n_rin = n_rout = 0
    if rider is not None:
        n_rin, n_rout = len(rider.operands), len(rider.out_shapes)
        for src, dst in rider.aliases:
            aliases[len(operands) + src] = 1 + n_side + dst
        operands += list(rider.operands)
        in_specs += [_ANY] * n_rin
        out_specs += [_ANY] * n_rout
        out_shapes += list(rider.out_shapes)
        scratch += [pltpu.SemaphoreType.DMA((rider.n_sems,)), pltpu.SemaphoreType.DMA((rider.n_sems,))]
    assert na == 1 or nbp == 1, name

    def body(*refs):
        a_refs, b_refs = refs[:na], refs[na:na + nbp]
        ex_refs = refs[na + nbp:na + nbp + n_ex + n_aux]
        n_in = na + nbp + n_ex + n_aux + n_prev
        rin = refs[n_in:n_in + n_rin]
        o_ref = refs[n_in + n_rin]
        side_refs = refs[n_in + n_rin + 1:n_in + n_rin + 1 + n_side]
        rout = refs[n_in + n_rin + 1 + n_side:n_in + n_rin + 1 + n_side + n_rout]
        rest = refs[n_in + n_rin + 1 + n_side + n_rout:]
        i, j, k = pl.program_id(0), pl.program_id(1), pl.program_id(2)
        if rider is not None:
            sems = rest[-2:]

            @pl.when(jnp.logical_and(i == 0, jnp.logical_and(j == 0, k == 0)))
            def _():
                rider.start(rin, rout, *sems)

        def finish(acc):
            if epi is None:
                o_ref[...] = acc[...].astype(o_ref.dtype)
                return
            strip = min(tm, EPI_ROWS)
            colsums = [None] * n_side
            for r0 in range(0, tm, strip):
                rows = slice(r0, r0 + strip)
                res = epi(acc[rows, :], *[r[rows, :] for r in ex_refs[:n_ex]], *[r[...] for r in ex_refs[n_ex:]])
                if n_side:
                    res, *side_vals = res
                    for s, ((kind, _), ref, val) in enumerate(zip(sides, side_refs, side_vals)):
                        if kind == "tile":
                            ref[rows, :] = val.astype(ref.dtype)
                        else:
                            colsums[s] = val if colsums[s] is None else colsums[s] + val
                o_ref[rows, :] = res.astype(o_ref.dtype)
            for (kind, _), ref, val in zip(sides, side_refs, colsums):
                if kind == "colsum":
                    @pl.when(i == 0)
                    def _(ref=ref, val=val):
                        ref[...] = val

                    @pl.when(i > 0)
                    def _(ref=ref, val=val):
                        ref[...] += val

                    @pl.when(i == gm - 1)
                    def _(ref=ref):
                        ref[0:1, :] = jnp.sum(ref[...], axis=0, keepdims=True)

        def step(a_ref, b_ref):
            av, bv = a_ref[...], b_ref[...]
            if a_fn is not None:
                av = a_fn(av)
            if b_fn is not None:
                bv = b_fn(bv)
            part = dot(av.astype(BF16), bv.astype(BF16))
            if gk == 1:
                finish(part)
                return
            acc_ref = rest[0]

            @pl.when(k == 0)
            def _():
                acc_ref[...] = part

            @pl.when(k > 0)
            def _():
                acc_ref[...] += part

        if fold:
            kp = K // na
            finish(sum(dot(a_refs[p][...].astype(BF16), b_refs[0][:, p * kp:(p + 1) * kp].astype(BF16))
                       for p in range(na)))
        elif na > 1:
            sel = pl.program_id(a_axis) // a_per
            for p in range(na):
                pl.when(sel == p)(functools.partial(step, a_refs[p], b_refs[0]))
        elif nbp > 1:
            sel = pl.program_id(b_axis) // b_per
            for p in range(nbp):
                pl.when(sel == p)(functools.partial(step, a_refs[0], b_refs[p]))
        else:
            step(a_refs[0], b_refs[0])
        if gk > 1:
            @pl.when(k == gk - 1)
            def _():
                finish(rest[0])

        if rider is not None:
            steps = gm * gn * gk
            step_no = (i * gn + j) * gk + k
            if rider.relay is not None:
                assert steps >= 3, name

                @pl.when(step_no == steps - 2)
                def _():
                    rider.relay(rin, rout, *sems)

            @pl.when(step_no == steps - 1)
            def _():
                rider.finish(rin, rout, *sems)

    sequential = rider is not None or any(kind == "colsum" for kind, _ in sides)
    sem = ("arbitrary",) * 3 if sequential else ("parallel", "parallel", "arbitrary")
    outs = pl.pallas_call(
        body,
        name=name,
        grid=(gm, gn, gk),
        in_specs=in_specs,
        out_specs=out_specs,
        out_shape=out_shapes,
        input_output_aliases=aliases,
        scratch_shapes=scratch,
        compiler_params=_cp(sem),
    )(*operands)
    res = outs[0] if not sides else tuple(outs[:1 + n_side])
    return res if rider is None else (res, list(outs[1 + n_side:]))


def _relu2(u):
    r = jnp.maximum(u, 0)
    return r * r


def _epi_residual(acc, res):
    return acc + res


def _epi_residual_norm(acc, res, g):
    xn = acc + res
    r = lax.rsqrt(jnp.mean(xn * xn, axis=-1, keepdims=True) + RMS_EPS)
    return xn, (xn * r) * g


def _epi_rms_bwd(dh, x, dres, g):
    r = lax.rsqrt(jnp.mean(x * x, axis=-1, keepdims=True) + RMS_EPS)
    xh = x * r
    dxh = dh * g
    dx = dres + r * (dxh - xh * jnp.mean(dxh * xh, axis=-1, keepdims=True))
    return dx, dx, jnp.sum((dh * xh).reshape(dh.shape[0] // 8, 8, dh.shape[1]), axis=0)


def _rms_fwd(name, x, g, rider=None):
    S, D = x.shape
    tq = min(1024, S)

    def body(x_ref, g_ref, o_ref):
        xv = x_ref[...]
        r = lax.rsqrt(jnp.mean(xv * xv, axis=-1, keepdims=True) + RMS_EPS)
        o_ref[...] = ((xv * r) * g_ref[...]).astype(o_ref.dtype)

    grid = (S // tq,)
    body, r_in, r_out, r_shapes, r_sems, aliases = _riding(rider, body, 2, 1, grid)
    outs = pl.pallas_call(
        body,
        name=name,
        grid=grid,
        in_specs=[pl.BlockSpec((tq, D), lambda i: (i, 0)), pl.BlockSpec((1, D), lambda i: (0, 0))] + r_in,
        out_specs=[pl.BlockSpec((tq, D), lambda i: (i, 0))] + r_out,
        out_shape=[jax.ShapeDtypeStruct((S, D), BF16)] + r_shapes,
        input_output_aliases=aliases,
        scratch_shapes=r_sems,
        compiler_params=_cp(("parallel" if rider is None else "arbitrary",)),
    )(x, g, *(rider.operands if rider is not None else ()))
    return outs[0] if rider is None else (outs[0], list(outs[1:]))


def _loss_head(x, g, t):
    S, D = x.shape
    tq = min(512, S)
    n = S // tq

    def body(x_ref, g_ref, t_ref, loss_ref, dx_ref, dxb_ref, dg_ref, lacc_ref, gacc_ref):
        i = pl.program_id(0)
        xv = x_ref[...]
        gv = g_ref[...]
        r = lax.rsqrt(jnp.mean(xv * xv, axis=-1, keepdims=True) + RMS_EPS)
        xh = xv * r
        e = xh * gv - t_ref[...]
        dy = e * (1.0 / D)
        dxh = dy * gv
        dx = r * (dxh - xh * jnp.mean(dxh * xh, axis=-1, keepdims=True))
        dx_ref[...] = dx
        dxb_ref[...] = dx.astype(dxb_ref.dtype)
        lpart = jnp.sum((e * e).reshape(tq // 8, 8, D), axis=0)
        gpart = jnp.sum((dy * xh).reshape(tq // 8, 8, D), axis=0)

        @pl.when(i == 0)
        def _():
            lacc_ref[...] = lpart
            gacc_ref[...] = gpart

        @pl.when(i > 0)
        def _():
            lacc_ref[...] += lpart
            gacc_ref[...] += gpart

        @pl.when(i == n - 1)
        def _():
            dg_ref[...] = jnp.sum(gacc_ref[...], axis=0, keepdims=True)
            tot = jnp.sum(jnp.sum(lacc_ref[...], axis=0, keepdims=True), axis=1, keepdims=True)
            loss_ref[...] = jnp.broadcast_to(tot * (0.5 / D), (1, LANE))

    return pl.pallas_call(
        body,
        name="loss_head",
        grid=(n,),
        in_specs=[pl.BlockSpec((tq, D), lambda i: (i, 0)), pl.BlockSpec((1, D), lambda i: (0, 0)),
                  pl.BlockSpec((tq, D), lambda i: (i, 0))],
        out_specs=[pl.BlockSpec((1, LANE), lambda i: (0, 0)), pl.BlockSpec((tq, D), lambda i: (i, 0)),
                   pl.BlockSpec((tq, D), lambda i: (i, 0)), pl.BlockSpec((1, D), lambda i: (0, 0))],
        out_shape=[jax.ShapeDtypeStruct((1, LANE), F32), jax.ShapeDtypeStruct((S, D), F32),
                   jax.ShapeDtypeStruct((S, D), BF16), jax.ShapeDtypeStruct((1, D), F32)],
        scratch_shapes=[pltpu.VMEM((8, D), F32), pltpu.VMEM((8, D), F32)],
        compiler_params=_cp(("arbitrary",)),
    )(x, g, t)


def _ret_tables(S):
    T = min(RET_BLOCK, S)
    inv_freq = 1.0 / (RET_ROPE_BASE ** jnp.linspace(0.0, 1.0, RET_HEAD_DIM // 2, dtype=F32))
    ang = jnp.arange(S, dtype=F32)[:, None] * jnp.repeat(inv_freq, 2)[None, :]
    cosf = jnp.cos(ang)
    sins = jnp.sin(ang) * jnp.asarray(np.tile([-1.0, 1.0], RET_HEAD_DIM // 2), F32)[None, :]
    log_g = np.log1p(-np.power(2.0, -5.0 - np.arange(RET_HEADS, dtype=np.float64)))
    pos = np.arange(T, dtype=np.float64)
    diff = pos[:, None] - pos[None, :]
    same = (pos[:, None] // CHUNK) == (pos[None, :] // CHUNK)
    seen = same | (diff > 0)
    dmat = np.where(seen[None], np.exp(np.abs(diff)[None] * log_g[:, None, None]), 0.0)
    aq = np.exp((pos[None, :] + 1.0) * log_g[:, None])
    ak = np.exp((T - 1.0 - pos[None, :]) * log_g[:, None])
    lam = np.exp(T * log_g)
    bc = lambda v: jnp.asarray(np.broadcast_to(v[..., None], v.shape + (LANE,)), F32)
    return dict(cos=cosf, sin=sins, dmat=jnp.asarray(dmat, F32), aq=bc(aq), ak=bc(ak),
                lam=jnp.asarray(np.broadcast_to(lam[:, None, None], (RET_HEADS, 1, LANE)), F32))


def _rot(x, cos, sin_s, even):
    sw = jnp.where(even, pltpu.roll(x, LANE - 1, 1), pltpu.roll(x, 1, 1))
    return x * cos + sw * sin_s


def _rot_t(dy, cos, sin_s, even):
    t = dy * sin_s
    return dy * cos + jnp.where(even, pltpu.roll(t, LANE - 1, 1), pltpu.roll(t, 1, 1))


def _ret_specs(T, rev_nb=None):
    blk = (lambda b: b) if rev_nb is None else (lambda b: rev_nb - 1 - b)
    whole = lambda shape: pl.BlockSpec(shape, lambda b: (0,) * len(shape))
    specs = [pl.BlockSpec((T, AB_IN_WIDTH), lambda b: (blk(b), 0)),
             pl.BlockSpec((T, LANE), lambda b: (blk(b), 0)),
             pl.BlockSpec((T, LANE), lambda b: (blk(b), 0)),
             whole((RET_HEADS, T, T)), whole((RET_HEADS, T, LANE)), whole((RET_HEADS, T, LANE)),
             whole((RET_HEADS, 1, LANE)), whole((1, RET_WIDTH))]
    return specs, blk


def _head_views(h, z_ref, tabs, token_refs, head_refs):
    zs = [z_ref.at[:, (o * RET_HEADS + h) * LANE:(o * RET_HEADS + h + 1) * LANE] for o in range(4)]
    hs = slice(h * LANE, (h + 1) * LANE)
    return zs, [t.at[h] for t in tabs], [r.at[:, hs] for r in token_refs], [r.at[h] for r in head_refs]


def _ret_fwd(name, z, tb, gain, rider=None):
    S = z.shape[0]
    T = min(RET_BLOCK, S)
    nb = S // T
    specs, blk = _ret_specs(T)

    def body(z_ref, cos_r, sin_r, d_all, aq_all, ak_all, lam_all, gain_all, cat_all, opre_all, st_all, state_all):
        @pl.when(pl.program_id(0) == 0)
        def _():
            state_all[...] = jnp.zeros_like(state_all)

        for h in range(RET_HEADS):
            zs, tabs, toks, heads = _head_views(h, z_ref, (d_all, aq_all, ak_all, lam_all),
                                                (gain_all, cat_all, opre_all), (st_all, state_all))
            head(*zs, cos_r, sin_r, *tabs, *toks, *heads)

    def head(zq, zk, zv, zg, cos_r, sin_r, d_r, aq_r, ak_r, lam_r, gain_r, ret_o, opre_o, st_o, state):
        even = (lax.broadcasted_iota(jnp.int32, (T, LANE), 1) & 1) == 0
        c, s = cos_r[...], sin_r[...]
        q = _rot(zq[...].astype(F32), c, s, even)
        k = _rot(zk[...].astype(F32), c, s, even) * KSCALE
        qb, kb, vb = q.astype(BF16), k.astype(BF16), zv[...].astype(BF16)
        p = (_dot_nt(qb, kb) * d_r[...]).astype(BF16)
        st = state[...]
        st_o[...] = st
        o = _dot(p, vb) + _dot((q * aq_r[...]).astype(BF16), st.astype(BF16))
        state[...] = st * lam_r[...] + _dot_tn((k * ak_r[...]).astype(BF16), vb)
        opre_o[...] = o
        mu = jnp.mean(o, axis=-1, keepdims=True)
        d = o - mu
        y = d * lax.rsqrt(jnp.mean(d * d, axis=-1, keepdims=True) + GN_EPS)
        g = zg[...].astype(F32)
        ret_o[...] = ((g * jax.nn.sigmoid(g)) * (y * gain_r[...])).astype(ret_o.dtype)

    out_blk = pl.BlockSpec((T, RET_WIDTH), lambda b: (b, 0))
    grid = (nb,)
    body, r_in, r_out, r_shapes, r_sems, aliases = _riding(rider, body, 8, 3, grid)
    outs = pl.pallas_call(
        body,
        name=name,
        grid=grid,
        in_specs=specs + r_in,
        out_specs=[out_blk, out_blk, pl.BlockSpec((RET_HEADS, None, LANE, LANE), lambda b: (0, b, 0, 0))] + r_out,
        out_shape=[jax.ShapeDtypeStruct((S, D_MODEL), BF16), jax.ShapeDtypeStruct((S, RET_WIDTH), F32),
                   jax.ShapeDtypeStruct((RET_HEADS, nb, LANE, LANE), F32)] + r_shapes,
        input_output_aliases=aliases,
        scratch_shapes=[pltpu.VMEM((RET_HEADS, LANE, LANE), F32)] + r_sems,
        compiler_params=_cp(("arbitrary",)),
    )(z, tb["cos"], tb["sin"], tb["dmat"], tb["aq"], tb["ak"], tb["lam"], gain,
      *(rider.operands if rider is not None else ()))
    return tuple(outs[:3]) if rider is None else (tuple(outs[:3]), list(outs[3:]))


def _ret_bwd(name, z, tb, gain, opre, states, dcat, rider=None):
    S = z.shape[0]
    T = min(RET_BLOCK, S)
    nb = S // T
    specs, blk = _ret_specs(T, rev_nb=nb)
    tok = pl.BlockSpec((T, RET_WIDTH), lambda b: (blk(b), 0))

    def body(z_ref, cos_r, sin_r, d_all, aq_all, ak_all, lam_all, gain_all, opre_all, st_all, dret_all,
             dz_ref, dgain_all, dstate_all):
        @pl.when(pl.program_id(0) == 0)
        def _():
            dstate_all[...] = jnp.zeros_like(dstate_all)
            dgain_all[...] = jnp.zeros_like(dgain_all)

        for h in range(RET_HEADS):
            zs, tabs, toks, heads = _head_views(h, z_ref, (d_all, aq_all, ak_all, lam_all),
                                                (gain_all, opre_all, dret_all, dgain_all), (st_all, dstate_all))
            dzs, _, _, _ = _head_views(h, dz_ref, (), (), ())
            gain_r, opre_r, dret_r, dgain_o = toks
            head(*zs, cos_r, sin_r, *tabs, gain_r, opre_r, heads[0], dret_r, *dzs, dgain_o, heads[1])

    def head(zq, zk, zv, zg, cos_r, sin_r, d_r, aq_r, ak_r, lam_r, gain_r, opre_r, st_r, dret_r,
             dq_o, dk_o, dv_o, dg_o, dgain_o, dstate):
        even = (lax.broadcasted_iota(jnp.int32, (T, LANE), 1) & 1) == 0
        c, s = cos_r[...], sin_r[...]
        aq, ak, dm = aq_r[...], ak_r[...], d_r[...]
        q = _rot(zq[...].astype(F32), c, s, even)
        k = _rot(zk[...].astype(F32), c, s, even) * KSCALE
        qb, kb, vb = q.astype(BF16), k.astype(BF16), zv[...].astype(BF16)
        pb = (_dot_nt(qb, kb) * dm).astype(BF16)
        g = zg[...].astype(F32)
        sig = jax.nn.sigmoid(g)
        o = opre_r[...]
        mu = jnp.mean(o, axis=-1, keepdims=True)
        d = o - mu
        rstd = lax.rsqrt(jnp.mean(d * d, axis=-1, keepdims=True) + GN_EPS)
        y = d * rstd
        gain_v = gain_r[...]
        dret = dret_r[...].astype(F32)
        dyg = dret * (g * sig)
        dg_o[...] = (dret * (y * gain_v) * (sig * (1.0 + g * (1.0 - sig)))).astype(dg_o.dtype)
        dgain_o[...] += jnp.sum(dyg * y, axis=0, keepdims=True)
        dy = dyg * gain_v
        do = rstd * (dy - jnp.mean(dy, axis=-1, keepdims=True) - y * jnp.mean(dy * y, axis=-1, keepdims=True))
        dob = do.astype(BF16)
        stb = st_r[...].astype(BF16)
        dsn = dstate[...]
        dsnb = dsn.astype(BF16)
        dpb = (_dot_nt(dob, vb) * dm).astype(BF16)
        dq = _dot(dpb, kb) + _dot_nt(dob, stb) * aq
        dk = _dot_tn(dpb, qb) + _dot_nt(vb, dsnb) * ak
        dv = _dot_tn(pb, dob) + _dot((k * ak).astype(BF16), dsnb)
        dstate[...] = dsn * lam_r[...] + _dot_tn((q * aq).astype(BF16), dob)
        dq_o[...] = _rot_t(dq, c, s, even).astype(dq_o.dtype)
        dk_o[...] = _rot_t(dk * KSCALE, c, s, even).astype(dk_o.dtype)
        dv_o[...] = dv.astype(dv_o.dtype)

    grid = (nb,)
    body, r_in, r_out, r_shapes, r_sems, aliases = _riding(rider, body, 11, 2, grid)
    outs = pl.pallas_call(
        body,
        name=name,
        grid=grid,
        in_specs=specs + [tok, pl.BlockSpec((RET_HEADS, None, LANE, LANE), lambda b: (0, blk(b), 0, 0)), tok] + r_in,
        out_specs=[pl.BlockSpec((T, 4 * RET_WIDTH), lambda b: (blk(b), 0)),
                   pl.BlockSpec((1, RET_WIDTH), lambda b: (0, 0))] + r_out,
        out_shape=[jax.ShapeDtypeStruct((S, AB_IN_WIDTH), BF16), jax.ShapeDtypeStruct((1, RET_WIDTH), F32)] + r_shapes,
        input_output_aliases=aliases,
        scratch_shapes=[pltpu.VMEM((RET_HEADS, LANE, LANE), F32)] + r_sems,
        compiler_params=_cp(("arbitrary",)),
    )(z, tb["cos"], tb["sin"], tb["dmat"], tb["aq"], tb["ak"], tb["lam"], gain, opre, states, dcat,
      *(rider.operands if rider is not None else ()))
    return tuple(outs[:2]) if rider is None else (tuple(outs[:2]), list(outs[2:]))


def _pool_counts(t0, rows):
    t = t0 + lax.broadcasted_iota(jnp.int32, (rows, POOL_WIDTH), 0)
    grp = lax.broadcasted_iota(jnp.int32, (rows, POOL_WIDTH), 1) >> 7
    win = jnp.where(grp == 0, POOL_WINDOWS[0], jnp.where(grp == 1, POOL_WINDOWS[1],
                    jnp.where(grp == 2, POOL_WINDOWS[2], POOL_WINDOWS[3])))
    return jnp.maximum(jnp.minimum(t + 1, win), 1).astype(F32), grp


def _window_sums(ext, grp, sign):
    n = ext.shape[0]
    sh = lambda v, k: pltpu.roll(v, k % n if sign > 0 else (n - k) % n, 0)
    s2 = ext + sh(ext, 1)
    s4 = s2 + sh(s2, 2)
    s8 = s4 + sh(s4, 4)
    s16 = s8 + sh(s8, 8)
    return jnp.where(grp == 0, s2, jnp.where(grp == 1, s4, jnp.where(grp == 2, s8, s16)))


def _pool_fwd(name, z, w_pool, scale, cat):
    S = z.shape[0]
    T = min(512, S)
    nb = S // T
    pcol = AB_IN_WIDTH // POOL_WIDTH - 1
    hb = T // POOL_HALO

    def body(p_ref, halo_ref, w_ref, sc_ref, cat_in, out_ref, pooled_ref):
        b = pl.program_id(0)
        cur = p_ref[...].astype(F32)
        halo = jnp.where(b > 0, halo_ref[...].astype(F32), 0.0)
        ext = jnp.concatenate([halo, cur], axis=0)
        cnt, grp = _pool_counts(b * T - POOL_HALO, T + POOL_HALO)
        sums = _window_sums(ext, grp, +1)
        pooled = (sums / cnt)[POOL_HALO:] - cur
        pb = pooled.astype(BF16)
        pooled_ref[...] = pb
        for gi in range(len(POOL_WINDOWS)):
            cs = slice(gi * POOL_GROUP_DIM, (gi + 1) * POOL_GROUP_DIM)
            mixed = _dot(pb[:, cs], w_ref[gi].astype(BF16))
            out_ref[:, cs] = (mixed * sc_ref[:, cs]).astype(out_ref.dtype)

    return pl.pallas_call(
        body,
        name=name,
        grid=(nb,),
        in_specs=[pl.BlockSpec((T, POOL_WIDTH), lambda b: (b, pcol)),
                  pl.BlockSpec((POOL_HALO, POOL_WIDTH), lambda b: (jnp.maximum(b * hb - 1, 0), pcol)),
                  pl.BlockSpec((4, POOL_GROUP_DIM, POOL_GROUP_DIM), lambda b: (0, 0, 0)),
                  pl.BlockSpec((1, POOL_WIDTH), lambda b: (0, 0)), _ANY],
        out_specs=[pl.BlockSpec((T, POOL_WIDTH), lambda b: (b, 1)), pl.BlockSpec((T, POOL_WIDTH), lambda b: (b, 0))],
        out_shape=[jax.ShapeDtypeStruct(cat.shape, cat.dtype), jax.ShapeDtypeStruct((S, POOL_WIDTH), BF16)],
        input_output_aliases={4: 0},
        compiler_params=_cp(("parallel",)),
    )(z, z, w_pool, scale, cat)


def _pool_bwd(name, pooled, w_pool, scale, dcat, dz):
    S = pooled.shape[0]
    T = min(512, S)
    nb = S // T
    hb = T // POOL_HALO
    last_h = S // POOL_HALO - 1
    pcol = AB_IN_WIDTH // POOL_WIDTH - 1

    def body(d_ref, dn_ref, pooled_ref, w_ref, sc_ref, dz_in, dp_ref, dw_ref, dsc_ref):
        b = pl.program_id(0)

        @pl.when(b == 0)
        def _():
            dw_ref[...] = jnp.zeros_like(dw_ref)
            dsc_ref[...] = jnp.zeros_like(dsc_ref)

        sc = sc_ref[...]
        dout = d_ref[...].astype(F32)
        dnext = jnp.where(b < nb - 1, dn_ref[...].astype(F32), 0.0)
        dmix = jnp.concatenate([dout, dnext], axis=0) * sc
        dmb = dmix.astype(BF16)
        pb = pooled_ref[...]
        dpooled = []
        for gi in range(len(POOL_WINDOWS)):
            cs = slice(gi * POOL_GROUP_DIM, (gi + 1) * POOL_GROUP_DIM)
            wb = w_ref[gi].astype(BF16)
            dpooled.append(_dot_nt(dmb[:, cs], wb))
            dw_ref[gi] += _dot_tn(pb[:, cs], dmb[:T, cs])
            mixed = _dot(pb[:, cs], wb)
            dsc_ref[:, cs] += jnp.sum(dout[:, cs] * mixed, axis=0, keepdims=True)
        dpl = jnp.concatenate(dpooled, axis=1)
        cnt, grp = _pool_counts(b * T, T + POOL_HALO)
        sums = _window_sums(dpl / cnt, grp, -1)
        dp_ref[...] = (sums[:T] - dpl[:T]).astype(dp_ref.dtype)

    return pl.pallas_call(
        body,
        name=name,
        grid=(nb,),
        in_specs=[pl.BlockSpec((T, POOL_WIDTH), lambda b: (b, 1)),
                  pl.BlockSpec((POOL_HALO, POOL_WIDTH), lambda b: (jnp.minimum((b + 1) * hb, last_h), 1)),
                  pl.BlockSpec((T, POOL_WIDTH), lambda b: (b, 0)),
                  pl.BlockSpec((4, POOL_GROUP_DIM, POOL_GROUP_DIM), lambda b: (0, 0, 0)),
                  pl.BlockSpec((1, POOL_WIDTH), lambda b: (0, 0)), _ANY],
        out_specs=[pl.BlockSpec((T, POOL_WIDTH), lambda b: (b, pcol)),
                   pl.BlockSpec((4, POOL_GROUP_DIM, POOL_GROUP_DIM), lambda b: (0, 0, 0)),
                   pl.BlockSpec((1, POOL_WIDTH), lambda b: (0, 0))],
        out_shape=[jax.ShapeDtypeStruct(dz.shape, dz.dtype),
                   jax.ShapeDtypeStruct((4, POOL_GROUP_DIM, POOL_GROUP_DIM), F32),
                   jax.ShapeDtypeStruct((1, POOL_WIDTH), F32)],
        input_output_aliases={5: 0},
        compiler_params=_cp(("arbitrary",)),
    )(dcat, dcat, pooled, w_pool, scale, dz)


ATT_STRIP = 32
ATT_Q = 256
ATT_W = ATT_Q + LEFT_CHUNKS * CHUNK


def _rel_index():
    j = np.arange(ATT_W)
    rel = np.clip(LEFT_CHUNKS * CHUNK - j, -REL_CLIP, REL_CLIP) + REL_CLIP
    fwd = np.where(j < BAND, rel, N_REL)
    bwd = np.where(j <= ATT_W - CHUNK, fwd, 2 * REL_CLIP)
    return tuple(jnp.asarray(v.reshape(1, ATT_W), jnp.int32) for v in (fwd, bwd))


def _bias_table(name, rel_bias, rel_idx):
    rb = jnp.concatenate([rel_bias, jnp.full((ATT_HEADS, 1), NEG_INF, F32),
                          jnp.zeros((ATT_HEADS, N_REL_PAD - N_REL - 1), F32)], axis=1)

    def body(rb_ref, idx_ref, o_ref, row0_ref):
        r = lax.broadcasted_iota(jnp.int32, (N_REL_PAD, ATT_W), 0)
        onehot = (r == idx_ref[...]).astype(F32)
        row0_ref[...] = jnp.dot(rb_ref[...], onehot, precision=lax.Precision.HIGHEST, preferred_element_type=F32)
        col = lax.broadcasted_iota(jnp.int32, (CHUNK, ATT_W), 1)
        row = lax.broadcasted_iota(jnp.int32, (CHUNK, ATT_W), 0)
        for h in range(ATT_HEADS):
            same = jnp.broadcast_to(row0_ref[pl.ds(h, 1), :], (CHUNK, ATT_W))
            turned = pltpu.roll(same, 0, 1, stride=1, stride_axis=0)
            o_ref[h] = jnp.where(col >= BAND, NEG_INF, jnp.where(col < row, same, turned))

    return pl.pallas_call(
        body,
        name=name,
        out_shape=jax.ShapeDtypeStruct((ATT_HEADS, CHUNK, ATT_W), F32),
        scratch_shapes=[pltpu.VMEM((ATT_HEADS, ATT_W), F32)],
        compiler_params=pltpu.CompilerParams(vmem_limit_bytes=VMEM_LIMIT),
    )(rb, rel_idx)


def _bias_grad(name, dband, rel_idx):
    def body(d_ref, idx_ref, o_ref, sums_ref):
        row = lax.broadcasted_iota(jnp.int32, (CHUNK, ATT_W), 0)
        for h in range(ATT_HEADS):
            back = d_ref[h]
            for bit in range(CHUNK.bit_length() - 1):
                back = jnp.where(((row >> bit) & 1) == 1, pltpu.roll(back, ATT_W - (1 << bit), 1), back)
            sums_ref[pl.ds(h, 1), :] = jnp.sum(back, axis=0, keepdims=True)
        r = lax.broadcasted_iota(jnp.int32, (N_REL_PAD, ATT_W), 0)
        onehot = (r == idx_ref[...]).astype(F32)
        o_ref[...] = lax.dot_general(sums_ref[...], onehot, (((1,), (1,)), ((), ())),
                                     precision=lax.Precision.HIGHEST, preferred_element_type=F32)

    out = pl.pallas_call(
        body,
        name=name,
        out_shape=jax.ShapeDtypeStruct((ATT_HEADS, N_REL_PAD), F32),
        scratch_shapes=[pltpu.VMEM((ATT_HEADS, ATT_W), F32)],
        compiler_params=pltpu.CompilerParams(vmem_limit_bytes=VMEM_LIMIT),
    )(dband, rel_idx)
    return out[:, :N_REL]


def _attn_unit(q_ref, kw_ref, bias_ref, e, u, lane):
    mine = (lane < ATT_HEAD_DIM) if e == 0 else (lane >= ATT_HEAD_DIM)
    qm = jnp.where(mine, q_ref[u * ATT_Q:(u + 1) * ATT_Q, :] * QSCALE, 0)
    kw = kw_ref[u * ATT_Q:u * ATT_Q + ATT_W, :]
    s = _dot_nt(qm, kw) + bias_ref[u, e]
    p = jnp.exp(s - jnp.max(s, axis=-1, keepdims=True))
    return p, 1.0 / jnp.sum(p, axis=-1, keepdims=True), qm, kw, mine


def _attn_in_specs(nb):
    T = ATT_BLOCK
    hp = ATT_HEADS // 2
    cur = lambda off: pl.BlockSpec((T, LANE), lambda h, b: (jnp.minimum(b, nb - 1), off + h))
    prev = lambda off: pl.BlockSpec((T, LANE), lambda h, b: (jnp.clip(b - 1, 0, nb - 1), off + h))
    return [cur(0), prev(hp), cur(hp), prev(2 * hp), cur(2 * hp),
            pl.BlockSpec((None, 2, CHUNK, ATT_W), lambda h, b: (h, 0, 0, 0))]


def _spread_bias(bias_ref, bm_ref, block):
    col = lax.broadcasted_iota(jnp.int32, (CHUNK, ATT_W), 1)
    for first in (True, False):
        @pl.when(block == (0 if first else 1))
        def _(first=first):
            for u in range(ATT_BLOCK // ATT_Q):
                for e in range(2):
                    for j in range(ATT_Q // CHUNK):
                        rows = pltpu.roll(bias_ref[e], j * CHUNK, 1)
                        if first:
                            rows = jnp.where(col >= ATT_BLOCK - u * ATT_Q, rows, NEG_INF)
                        bm_ref[u, e, j * CHUNK:(j + 1) * CHUNK, :] = rows


def _riding(rider, body, n_in, n_out, grid):
    if rider is None:
        return body, [], [], [], [], {}
    n_rin, n_rout = len(rider.operands), len(rider.out_shapes)
    steps = int(np.prod(grid))

    def riding(*refs):
        ins, rin = refs[:n_in], refs[n_in:n_in + n_rin]
        outs = refs[n_in + n_rin:n_in + n_rin + n_out]
        rout = refs[n_in + n_rin + n_out:n_in + n_rin + n_out + n_rout]
        rest = refs[n_in + n_rin + n_out + n_rout:]
        scratch, sems = rest[:-2], rest[-2:]
        step_no = pl.program_id(0)
        for axis in range(1, len(grid)):
            step_no = step_no * grid[axis] + pl.program_id(axis)

        @pl.when(step_no == 0)
        def _():
            rider.start(rin, rout, *sems)

        body(*ins, *outs, *scratch)
        if rider.relay is not None:
            assert steps >= 3

            @pl.when(step_no == steps - 2)
            def _():
                rider.relay(rin, rout, *sems)

        @pl.when(step_no == steps - 1)
        def _():
            rider.finish(rin, rout, *sems)

    sems = [pltpu.SemaphoreType.DMA((rider.n_sems,)), pltpu.SemaphoreType.DMA((rider.n_sems,))]
    aliases = {n_in + src: n_out + dst for src, dst in rider.aliases}
    return riding, [_ANY] * n_rin, [_ANY] * n_rout, list(rider.out_shapes), sems, aliases


def _attn_fwd(name, qkv, bias, rider=None):
    S = qkv.shape[0]
    T = ATT_BLOCK
    nb = S // T

    def body(q_ref, kp_ref, kc_ref, vp_ref, vc_ref, band_ref, o_ref, kw_ref, vw_ref, bias_ref, s_ref, p_ref, inv_ref):
        _spread_bias(band_ref, bias_ref, pl.program_id(1))
        kw_ref[0:T] = kp_ref[...]
        kw_ref[T:2 * T] = kc_ref[...]
        vw_ref[0:T] = vp_ref[...]
        vw_ref[T:2 * T] = vc_ref[...]
        lane = lax.broadcasted_iota(jnp.int32, (ATT_Q, LANE), 1)
        for u in range(T // ATT_Q):
            vw = vw_ref[u * ATT_Q:u * ATT_Q + ATT_W, :]
            kw = kw_ref[u * ATT_Q:u * ATT_Q + ATT_W, :]
            outs = []
            for e in range(2):
                mine = (lane < ATT_HEAD_DIM) if e == 0 else (lane >= ATT_HEAD_DIM)
                qm = jnp.where(mine, q_ref[u * ATT_Q:(u + 1) * ATT_Q, :] * QSCALE, 0)
                s_ref[e] = _dot_nt(qm, kw)
                for r in range(ATT_Q // ATT_STRIP):
                    rows = slice(r * ATT_STRIP, (r + 1) * ATT_STRIP)
                    s = s_ref[e, rows, :] + bias_ref[u, e, rows, :]
                    p = jnp.exp(s - jnp.max(s, axis=-1, keepdims=True))
                    inv_ref[e, rows, :] = jnp.broadcast_to(1.0 / jnp.sum(p, axis=-1, keepdims=True), (ATT_STRIP, LANE))
                    p_ref[e, rows, :] = p.astype(BF16)
                outs.append(_dot(p_ref[e], vw) * inv_ref[e])
            o_ref[u * ATT_Q:(u + 1) * ATT_Q, :] = jnp.where(lane < ATT_HEAD_DIM, outs[0], outs[1]).astype(o_ref.dtype)

    grid = (ATT_HEADS // 2, nb)
    body, r_in, r_out, r_shapes, r_sems, aliases = _riding(rider, body, 6, 1, grid)
    outs = pl.pallas_call(
        body,
        name=name,
        grid=grid,
        in_specs=_attn_in_specs(nb) + r_in,
        out_specs=[pl.BlockSpec((T, LANE), lambda h, b: (b, h))] + r_out,
        out_shape=[jax.ShapeDtypeStruct((S, D_MODEL), BF16)] + r_shapes,
        input_output_aliases=aliases,
        scratch_shapes=[pltpu.VMEM((2 * T, LANE), BF16), pltpu.VMEM((2 * T, LANE), BF16),
                        pltpu.VMEM((T // ATT_Q, 2, ATT_Q, ATT_W), F32), pltpu.VMEM((2, ATT_Q, ATT_W), F32),
                        pltpu.VMEM((2, ATT_Q, ATT_W), BF16), pltpu.VMEM((2, ATT_Q, LANE), F32)] + r_sems,
        compiler_params=_cp(("parallel" if rider is None else "arbitrary", "arbitrary")),
    )(qkv, qkv, qkv, qkv, qkv, bias, *(rider.operands if rider is not None else ()))
    return outs[0] if rider is None else (outs[0], list(outs[1:]))


def _attn_bwd(name, qkv, bias, do, rider=None):
    S = qkv.shape[0]
    T = ATT_BLOCK
    nb = S // T

    def body(q_ref, kp_ref, kc_ref, vp_ref, vc_ref, band_ref, do_ref,
             dq_ref, dk_ref, dv_ref, dband_ref, kw_ref, vw_ref, dkw_ref, dvw_ref, bias_ref, dbias_ref):
        b = pl.program_id(1)

        _spread_bias(band_ref, bias_ref, b)

        @pl.when(b == 0)
        def _():
            dbias_ref[...] = jnp.zeros_like(dbias_ref)
            dkw_ref[:, T:2 * T] = jnp.zeros((LANE, T), F32)
            dvw_ref[:, T:2 * T] = jnp.zeros((LANE, T), F32)

        dkw_ref[:, 0:T] = dkw_ref[:, T:2 * T]
        dvw_ref[:, 0:T] = dvw_ref[:, T:2 * T]
        dkw_ref[:, T:2 * T] = jnp.zeros((LANE, T), F32)
        dvw_ref[:, T:2 * T] = jnp.zeros((LANE, T), F32)

        @pl.when(b < nb)
        def _():
            kw_ref[0:T] = kp_ref[...]
            kw_ref[T:2 * T] = kc_ref[...]
            vw_ref[0:T] = vp_ref[...]
            vw_ref[T:2 * T] = vc_ref[...]
            lane = lax.broadcasted_iota(jnp.int32, (ATT_Q, LANE), 1)
            for u in range(T // ATT_Q):
                rows = slice(u * ATT_Q, (u + 1) * ATT_Q)
                win = slice(u * ATT_Q, u * ATT_Q + ATT_W)
                vw = vw_ref[win, :]
                do2 = do_ref[rows, :]
                dqs, dk, dv = [], None, None
                for e in range(2):
                    p, inv, qm, kw, mine = _attn_unit(q_ref, kw_ref, bias_ref, e, u, lane)
                    dom = jnp.where(mine, do2, 0)
                    dp = _dot_nt(dom, vw)
                    delta = jnp.sum(p * dp, axis=-1, keepdims=True) * inv
                    ds = p * ((dp - delta) * inv)
                    dbias_ref[e] += ds
                    dsb = ds.astype(BF16)
                    dqs.append(_dot(dsb, kw))
                    dk_e = _dot_tn(qm, dsb)
                    dv_e = _dot_tn((dom * inv).astype(BF16), p.astype(BF16))
                    dk = dk_e if dk is None else dk + dk_e
                    dv = dv_e if dv is None else dv + dv_e
                dq_ref[rows, :] = (jnp.where(lane < ATT_HEAD_DIM, dqs[0], dqs[1]) * QSCALE).astype(dq_ref.dtype)
                dkw_ref[:, win] += dk
                dvw_ref[:, win] += dv

        @pl.when(b > 0)
        def _():
            dk_ref[...] = dkw_ref[:, 0:T].T.astype(dk_ref.dtype)
            dv_ref[...] = dvw_ref[:, 0:T].T.astype(dv_ref.dtype)

        @pl.when(b == nb)
        def _():
            for e in range(2):
                acc = dbias_ref[e, 0:CHUNK, :]
                for j in range(1, ATT_Q // CHUNK):
                    acc = acc + pltpu.roll(dbias_ref[e, j * CHUNK:(j + 1) * CHUNK, :], ATT_W - j * CHUNK, 1)
                dband_ref[e] = acc

    tok = jax.ShapeDtypeStruct((S, D_MODEL), BF16)
    prev_out = pl.BlockSpec((T, LANE), lambda h, b: (jnp.maximum(b - 1, 0), h))
    grid = (ATT_HEADS // 2, nb + 1)
    body, r_in, r_out, r_shapes, r_sems, aliases = _riding(rider, body, 7, 4, grid)
    outs = pl.pallas_call(
        body,
        name=name,
        grid=grid,
        in_specs=_attn_in_specs(nb) + [pl.BlockSpec((T, LANE), lambda h, b: (jnp.minimum(b, nb - 1), h))] + r_in,
        out_specs=[pl.BlockSpec((T, LANE), lambda h, b: (jnp.minimum(b, nb - 1), h)), prev_out, prev_out,
                   pl.BlockSpec((None, 2, CHUNK, ATT_W), lambda h, b: (h, 0, 0, 0))] + r_out,
        out_shape=[tok, tok, tok, jax.ShapeDtypeStruct((ATT_HEADS // 2, 2, CHUNK, ATT_W), F32)] + r_shapes,
        input_output_aliases=aliases,
        scratch_shapes=[pltpu.VMEM((2 * T, LANE), BF16), pltpu.VMEM((2 * T, LANE), BF16),
                        pltpu.VMEM((LANE, 2 * T), F32), pltpu.VMEM((LANE, 2 * T), F32),
                        pltpu.VMEM((T // ATT_Q, 2, ATT_Q, ATT_W), F32), pltpu.VMEM((2, ATT_Q, ATT_W), F32)] + r_sems,
        compiler_params=_cp(("parallel" if rider is None else "arbitrary", "arbitrary")),
    )(qkv, qkv, qkv, qkv, qkv, bias, do, *(rider.operands if rider is not None else ()))
    return tuple(outs[:4]) if rider is None else (tuple(outs[:4]), list(outs[4:]))


def _local_step(x, target, small, W):
    S = x.shape[0]
    tb = _ret_tables(S)
    rel_fwd, rel_bwd = _rel_index()
    saved = []
    normed = (("tile", BF16),)
    deep = dict(tm=512, tk=D_FF)
    h = W.hosted("mix_norm_fwd0", lambda rider: _rms_fwd("mix_norm_fwd0", x, small["mix_norm"][0:1], rider=rider))
    for layer in range(DEPTH):
        i = layer // 2
        st = {"x_in": x, "h": h}
        g_ffn = small["ffn_norm"][layer:layer + 1]
        if layer % 2 == 0:
            z = W.mm(f"ab_in_fwd{layer}", "nn", h, W.get("ab_w_in", i), tm=2048, tn=640, out_dtype=BF16)
            gain = small["ab_gn_gain"][i:i + 1]
            cat, opre, states = W.hosted(f"ret_fwd{layer}", lambda rider: _ret_fwd(f"ret_fwd{layer}", z, tb, gain, rider=rider))
            cat, pooled = _pool_fwd(f"pool_fwd{layer}", z, small["ab_w_pool"][i], small["ab_pool_scale"][i:i + 1], cat)
            st.update(z=z, opre=opre, states=states, pooled=pooled, cat=cat)
            x, hn = W.mm(f"ab_out_fwd{layer}", "nn", cat, W.get("ab_w_out", i), extras=(x,), aux=(g_ffn,), sides=normed,
                         epi=_epi_residual_norm)
        else:
            qkv = W.mm(f"qkv_fwd{layer}", "nn", h, W.get("c_w_qkv", i), tm=2048, out_dtype=BF16)
            bias = _bias_table(f"bias_table{layer}", small["c_rel_bias"][i], rel_fwd)
            bias = bias.reshape(ATT_HEADS // 2, 2, CHUNK, ATT_W)
            att = W.hosted(f"attn_fwd{layer}", lambda rider: _attn_fwd(f"attn_fwd{layer}", qkv, bias, rider=rider))
            st.update(qkv=qkv, bias=bias, att=att)
            x, hn = W.mm(f"c_out_fwd{layer}", "nn", att, W.get("c_w_out", i), extras=(x,), aux=(g_ffn,), sides=normed,
                         epi=_epi_residual_norm)
        st["x_mid"] = x
        u = W.mm(f"ffn_in_fwd{layer}", "nn", hn, W.get("w_ffn_in", layer), out_dtype=BF16, tm=2048)
        if layer + 1 < DEPTH:
            x, h = W.mm(f"ffn_out_fwd{layer}", "nn", u, W.get("w_ffn_out", layer), a_fn=_relu2, extras=(x,),
                        aux=(small["mix_norm"][layer + 1:layer + 2],), sides=normed, epi=_epi_residual_norm, **deep)
        else:
            x = W.mm(f"ffn_out_fwd{layer}", "nn", u, W.get("w_ffn_out", layer), a_fn=_relu2, extras=(x,),
                     epi=_epi_residual, **deep)
        st.update(hn=hn, u=u)
        saved.append(st)

    loss, dx, dxb, d_final = _loss_head(x, small["final_norm"].reshape(1, D_MODEL), target)

    gs = {k: [None] * v.shape[0] for k, v in small.items() if k != "final_norm"}
    gb = {k: None for k in W.n_layers}
    landed = {k: None for k in W.n_layers}
    pending = []

    def carry(call, take=1):
        items = [pending.pop(0) for _ in range(min(take, len(pending)))]
        if not items:
            return call(None)
        riders = [_grad_rider(key, idx, gb[key], landed[key]) for key, idx in items]
        res, outs = call(_join_riders(riders))
        for (key, _), out in zip(items, outs):
            landed[key] = out
        return res

    def host(name, *args, take=1, **kw):
        return carry(lambda rider: _mm(name, *args, rider=rider, **kw), take)

    def dw(name, key, idx, a, b, call=_mm, **kw):
        gb[key] = call(name, "tn", a, b, stack=(W.n_layers[key], idx, gb[key]), out_dtype=BF16, **kw)
        pending.append((key, idx))

    gain_sums = (("tile", BF16), ("colsum", F32))
    for layer in reversed(range(DEPTH)):
        i = layer // 2
        st = saved[layer]
        du = host(f"ffn_out_bwd{layer}", "nt", dxb, W.get("w_ffn_out", layer), extras=(st["u"],),
                  epi=lambda acc, u: acc * (2.0 * jnp.maximum(u, 0).astype(F32)), out_dtype=BF16, tm=2048)
        dw(f"ffn_out_dw{layer}", "w_ffn_out", layer, st["u"], dxb, a_fn=_relu2, tk=2048)
        dx, dxb, dgain = host(f"ffn_in_bwd{layer}", "nt", du, W.get("w_ffn_in", layer), extras=(st["x_mid"], dx),
                         aux=(small["ffn_norm"][layer:layer + 1],), sides=gain_sums, epi=_epi_rms_bwd, **deep)
        gs["ffn_norm"][layer] = dgain[0:1]
        dw(f"ffn_in_dw{layer}", "w_ffn_in", layer, st["hn"], du, tk=2048)
        norm_bwd = dict(extras=(st["x_in"], dx), aux=(small["mix_norm"][layer:layer + 1],), sides=gain_sums,
                        epi=_epi_rms_bwd)
        if layer % 2 == 0:
            dcat = _mm(f"ab_out_bwd{layer}", "nt", dxb, W.get("ab_w_out", i), out_dtype=BF16)
            dw(f"ab_out_dw{layer}", "ab_w_out", i, st["cat"], dxb, tk=2048)
            gain = small["ab_gn_gain"][i:i + 1]
            dz, gs["ab_gn_gain"][i] = carry(lambda rider: _ret_bwd(f"ret_bwd{layer}", st["z"], tb, gain, st["opre"],
                                                                   st["states"], dcat, rider=rider), take=len(pending))
            dz, gs["ab_w_pool"][i], gs["ab_pool_scale"][i] = _pool_bwd(
                f"pool_bwd{layer}", st["pooled"], small["ab_w_pool"][i], small["ab_pool_scale"][i:i + 1], dcat, dz)
            if layer == 0:
                dw(f"ab_in_dw{layer}", "ab_w_in", i, st["h"], dz, call=host, tn=640, tk=2048)
            dx, dxb, dgain = host(f"ab_in_bwd{layer}", "nt", dz, W.get("ab_w_in", i), tm=512, tk=AB_IN_WIDTH,
                             take=len(pending) if layer == 0 else 1, **norm_bwd)
            if layer > 0:
                dw(f"ab_in_dw{layer}", "ab_w_in", i, st["h"], dz, call=host, tn=640, tk=2048)
        else:
            datt = _mm(f"c_out_bwd{layer}", "nt", dxb, W.get("c_w_out", i), out_dtype=BF16)
            dw(f"c_out_dw{layer}", "c_w_out", i, st["att"], dxb, tk=2048)
            dq, dk, dv, dbias = carry(lambda rider: _attn_bwd(f"attn_bwd{layer}", st["qkv"], st["bias"], datt, rider=rider),
                                      take=len(pending))
            gs["c_rel_bias"][i] = _bias_grad(f"bias_grad{layer}", dbias.reshape(ATT_HEADS, CHUNK, ATT_W), rel_bwd)
            dqkv = [dq, dk, dv]
            dx, dxb, dgain = host(f"qkv_bwd{layer}", "nt", dqkv, W.get("c_w_qkv", i), tm=512, tk=3 * D_MODEL, **norm_bwd)
            dw(f"qkv_dw{layer}", "c_w_qkv", i, st["h"], dqkv, call=host, tk=2048)
        gs["mix_norm"][layer] = dgain[0:1]
    for key, idx in pending:
        landed[key], = _run_rider(f"grad_exchange_{key}{idx}", _grad_rider(key, idx, gb[key], landed[key]))

    g_small = {
        "mix_norm": jnp.concatenate(gs["mix_norm"], axis=0),
        "ffn_norm": jnp.concatenate(gs["ffn_norm"], axis=0),
        "ab_gn_gain": jnp.concatenate(gs["ab_gn_gain"], axis=0),
        "ab_w_pool": jnp.stack(gs["ab_w_pool"], axis=0),
        "ab_pool_scale": jnp.concatenate(gs["ab_pool_scale"], axis=0),
        "c_rel_bias": jnp.stack(gs["c_rel_bias"], axis=0),
        "final_norm": d_final.reshape(D_MODEL),
    }
    return loss, dx, g_small, gb, landed


_BIG = ("w_ffn_in", "w_ffn_out", "ab_w_in", "ab_w_out", "c_w_qkv", "c_w_out")
_SHARD_AXIS = {"w_ffn_in": 2, "w_ffn_out": 1, "ab_w_in": 2, "ab_w_out": 1, "c_w_qkv": 2, "c_w_out": 1}
_SMALL = ("mix_norm", "ffn_norm", "ab_gn_gain", "ab_w_pool", "ab_pool_scale", "c_rel_bias", "final_norm")


def _place():
    x, y, c = lax.axis_index("x"), lax.axis_index("y"), lax.axis_index("c")
    chips = [(1 - x, y), (x, 1 - y), (1 - x, 1 - y)]
    return x, y, c, chips


def _sub(ref, axis, start, size):
    idx = [slice(None)] * len(ref.shape)
    idx[axis] = pl.ds(pl.multiple_of(start, LANE), size)
    return ref.at[tuple(idx)]


def _gather_rider(items, shards):
    keys = sorted({k for k, _ in items})
    n = len(items)
    axes = [_SHARD_AXIS[k] - 1 for k, _ in items]
    sizes = [shards[k].shape[a + 1] for (k, _), a in zip(items, axes)]
    hsizes = [shards[k].shape[2 - a] // 2 for (k, _), a in zip(items, axes)]

    def views(ins, outs, send_sems, recv_sems):
        x, y, c, chips = _place()
        srcs = [ins[keys.index(k)].at[l] for k, l in items]

        def remote(src, dst, s, to):
            return pltpu.make_async_remote_copy(src_ref=src, dst_ref=dst, send_sem=send_sems.at[s],
                                                recv_sem=recv_sems.at[s], device_id=to, device_id_type=MESH)

        def half(w, chip, core):
            return _sub(_sub(outs[w], axes[w], chip * sizes[w], sizes[w]), 1 - axes[w], core * hsizes[w], hsizes[w])

        me = 2 * x + y
        local = [pltpu.make_async_copy(srcs[w], _sub(outs[w], axes[w], me * sizes[w], sizes[w]), send_sems.at[6 * n + w])
                 for w in range(n)]
        first = [remote(_sub(srcs[w], 1 - axes[w], c * hsizes[w], hsizes[w]), half(w, me, c), w * 6 + k, (px, py, c))
                 for w in range(n) for k, (px, py) in enumerate(chips)]
        return x, y, c, chips, remote, half, local, first

    def start(ins, outs, send_sems, recv_sems):
        *_, local, first = views(ins, outs, send_sems, recv_sems)
        for cp in local + first:
            cp.start()

    def passes(x, y, c, chips, remote, half):
        return [remote(half(w, 2 * px + py, c), half(w, 2 * px + py, c), w * 6 + 3 + k, (x, y, 1 - c))
                for w in range(n) for k, (px, py) in enumerate(chips)]

    def relay(ins, outs, send_sems, recv_sems):
        x, y, c, chips, remote, half, _, _ = views(ins, outs, send_sems, recv_sems)
        for w in range(n):
            for k, (px, py) in enumerate(chips):
                landed = half(w, 2 * px + py, c)
                remote(landed, landed, w * 6 + k, (px, py, c)).wait_recv()
        for cp in passes(x, y, c, chips, remote, half):
            cp.start()

    def finish(ins, outs, send_sems, recv_sems):
        x, y, c, chips, remote, half, local, first = views(ins, outs, send_sems, recv_sems)
        for w in range(n):
            for k, (px, py) in enumerate(chips):
                theirs = half(w, 2 * px + py, 1 - c)
                remote(theirs, theirs, w * 6 + 3 + k, (x, y, 1 - c)).wait_recv()
        for cp in first + passes(x, y, c, chips, remote, half):
            cp.wait_send()
        for cp in local:
            cp.wait()

    def full(k, a):
        shape = list(shards[k].shape[1:])
        shape[a] *= N_CHIPS
        return jax.ShapeDtypeStruct(tuple(shape), shards[k].dtype)

    return _Rider(tuple(shards[k] for k in keys), tuple(full(k, a) for (k, _), a in zip(items, axes)), 7 * n, start, finish,
                  relay=relay)


def _mixer_items(layer):
    names = ("ab_w_in", "ab_w_out") if layer % 2 == 0 else ("c_w_qkv", "c_w_out")
    return [(k, layer // 2) for k in names]


class _Weights:
    def __init__(self, shards):
        self.shards = shards
        self.n_layers = {k: shards[k].shape[0] for k in _BIG}
        self.full = {}
        first, second = _mixer_items(0)
        self.plan = {"mix_norm_fwd0": [first], "ab_in_fwd0": [second, ("w_ffn_in", 0)], "ret_fwd0": [("w_ffn_out", 0)]}
        for layer in range(1, DEPTH):
            if layer % 2 == 0:
                self.plan[f"attn_fwd{layer - 1}"] = _mixer_items(layer) + [("w_ffn_in", layer), ("w_ffn_out", layer)]
            else:
                self.plan[f"ffn_in_fwd{layer - 1}"] = _mixer_items(layer)
                self.plan[f"ffn_out_fwd{layer - 1}"] = [("w_ffn_in", layer)]
                self.plan[f"qkv_fwd{layer}"] = [("w_ffn_out", layer)]

    def _take(self, items, outs):
        self.full.update(zip(items, outs))

    def get(self, name, layer):
        return self.full[(name, layer)]

    def hosted(self, name, call):
        items = self.plan.get(name)
        if items is None:
            return call(None)
        res, outs = call(_gather_rider(items, self.shards))
        self._take(items, outs)
        return res

    def mm(self, name, *args, **kw):
        return self.hosted(name, lambda rider: _mm(name, *args, rider=rider, **kw))


def _run_rider(name, rider):
    n_in, n_out = len(rider.operands), len(rider.out_shapes)

    def body(*refs):
        ins, outs, sems = refs[:n_in], refs[n_in:n_in + n_out], refs[n_in + n_out:]
        rider.start(ins, outs, *sems)
        if rider.relay is not None:
            rider.relay(ins, outs, *sems)
        rider.finish(ins, outs, *sems)

    return pl.pallas_call(
        body,
        name=name,
        in_specs=[_ANY] * n_in,
        out_specs=[_ANY] * n_out,
        out_shape=list(rider.out_shapes),
        input_output_aliases=dict(rider.aliases),
        scratch_shapes=[pltpu.SemaphoreType.DMA((rider.n_sems,)), pltpu.SemaphoreType.DMA((rider.n_sems,))],
        compiler_params=pltpu.CompilerParams(has_side_effects=True),
    )(*rider.operands)


def _grad_rider(name, layer, grad, landing):
    axis = _SHARD_AXIS[name] - 1
    L, R, C = grad.shape
    shard = (R // N_CHIPS, C) if axis == 0 else (R, C // N_CHIPS)
    size = shard[axis]

    def copies(ins, outs, send_sems, recv_sems):
        x, y, c, chips = _place()
        return [pltpu.make_async_remote_copy(
            src_ref=_sub(ins[0].at[layer], axis, (2 * px + py) * size, size), dst_ref=outs[0].at[layer, k],
            send_sem=send_sems.at[k], recv_sem=recv_sems.at[k], device_id=(px, py, c), device_id_type=MESH)
            for k, (px, py) in enumerate(chips)]

    def start(ins, outs, send_sems, recv_sems):
        for cp in copies(ins, outs, send_sems, recv_sems):
            cp.start()

    def finish(ins, outs, send_sems, recv_sems):
        cps = copies(ins, outs, send_sems, recv_sems)
        for cp in cps:
            cp.wait_recv()
        for cp in cps:
            cp.wait_send()

    out = jax.ShapeDtypeStruct((L, 3) + shard, grad.dtype)
    if landing is None:
        return _Rider((grad,), (out,), 3, start, finish)
    return _Rider((grad, landing), (out,), 3, start, finish, aliases=((1, 0),))


def _join_riders(riders):
    if len(riders) == 1:
        return riders[0]

    def parts(ins, outs, send_sems, recv_sems):
        i0 = o0 = s0 = 0
        for r in riders:
            ni, no = len(r.operands), len(r.out_shapes)
            yield (r, ins[i0:i0 + ni], outs[o0:o0 + no], send_sems.at[pl.ds(s0, r.n_sems)],
                   recv_sems.at[pl.ds(s0, r.n_sems)])
            i0, o0, s0 = i0 + ni, o0 + no, s0 + r.n_sems

    def start(*refs):
        for r, *own in parts(*refs):
            r.start(*own)

    def finish(*refs):
        for r, *own in parts(*refs):
            r.finish(*own)

    aliases, i0, o0 = [], 0, 0
    for r in riders:
        aliases += [(i0 + src, o0 + dst) for src, dst in r.aliases]
        i0, o0 = i0 + len(r.operands), o0 + len(r.out_shapes)
    return _Rider(tuple(x for r in riders for x in r.operands), tuple(x for r in riders for x in r.out_shapes),
                  sum(r.n_sems for r in riders), start, finish, tuple(aliases))


def _rows_tile(rows, cols):
    tr = rows
    while tr * cols > (1 << 19) and tr % 16 == 0:
        tr //= 2
    return tr


def _chip_sum(name, grad, landed, chip, saxis):
    L = grad.shape[0]
    _, _, R, C = landed.shape
    tr = _rows_tile(R, C)
    nr = R // tr
    if saxis == 2:
        g_idx = lambda l, i, s: (l, i, s[0])
    else:
        g_idx = lambda l, i, s: (l, s[0] * nr + i, 0)

    def body(s_ref, g_ref, l_ref, o_ref):
        tot = ((g_ref[...].astype(F32) + l_ref[0].astype(F32)) + l_ref[1].astype(F32)) + l_ref[2].astype(F32)
        o_ref[...] = tot.astype(o_ref.dtype)

    return pl.pallas_call(
        body,
        name=name,
        grid_spec=pltpu.PrefetchScalarGridSpec(
            num_scalar_prefetch=1,
            grid=(L, nr),
            in_specs=[pl.BlockSpec((None, tr, C), g_idx), pl.BlockSpec((None, 3, tr, C), lambda l, i, s: (l, 0, i, 0))],
            out_specs=pl.BlockSpec((None, tr, C), lambda l, i, s: (l, i, 0)),
        ),
        out_shape=jax.ShapeDtypeStruct((L, R, C), BF16),
        compiler_params=_cp(("parallel", "parallel")),
    )(chip, grad, landed)


def _all_reduce_small(packed, sums):
    R = packed.shape[0]
    n = len(sums)

    def body(p_ref, *refs):
        s_in, o_ref, s_out = refs[:n], refs[n], refs[n + 1:2 * n + 1]
        land_ref, send_sems, recv_sems = refs[2 * n + 1:]
        x, y, c, _ = _place()
        me = 4 * x + 2 * y + c
        swaps = [pltpu.make_async_remote_copy(src_ref=s_in[w], dst_ref=s_out[w], send_sem=send_sems.at[N_DEV - 1 + w],
                                              recv_sem=recv_sems.at[N_DEV - 1 + w], device_id=(x, y, 1 - c),
                                              device_id_type=MESH) for w in range(n)]
        for cp in swaps:
            cp.start()
        sends, recvs = [], []
        for r in range(1, N_DEV):
            px, py, pc = x ^ (r >> 2), y ^ ((r >> 1) & 1), c ^ (r & 1)
            cp = pltpu.make_async_remote_copy(src_ref=p_ref, dst_ref=land_ref.at[me], send_sem=send_sems.at[r - 1],
                                              recv_sem=recv_sems.at[r - 1], device_id=(px, py, pc), device_id_type=MESH)
            cp.start()
            sends.append(cp)
            recvs.append(pltpu.make_async_remote_copy(src_ref=p_ref, dst_ref=land_ref.at[4 * px + 2 * py + pc],
                                                      send_sem=send_sems.at[r - 1], recv_sem=recv_sems.at[r - 1],
                                                      device_id=(px, py, pc), device_id_type=MESH))
        land_ref[me] = p_ref[...]
        for cp in recvs:
            cp.wait_recv()
        for cp in sends:
            cp.wait_send()
        acc = land_ref[0]
        for d in range(1, N_DEV):
            acc = acc + land_ref[d]
        o_ref[...] = acc
        for cp in swaps:
            cp.wait_recv()
        for cp in swaps:
            cp.wait_send()

    vm = pl.BlockSpec(memory_space=pltpu.VMEM)
    outs = pl.pallas_call(
        body,
        name="all_reduce_small",
        in_specs=[vm] + [_ANY] * n,
        out_specs=[vm] + [_ANY] * n,
        out_shape=[jax.ShapeDtypeStruct((R, LANE), F32)] + [jax.ShapeDtypeStruct(s.shape, s.dtype) for s in sums],
        scratch_shapes=[pltpu.VMEM((N_DEV, R, LANE), F32), pltpu.SemaphoreType.DMA((N_DEV - 1 + n,)),
                        pltpu.SemaphoreType.DMA((N_DEV - 1 + n,))],
        compiler_params=pltpu.CompilerParams(has_side_effects=True, vmem_limit_bytes=VMEM_LIMIT),
    )(packed, *sums)
    return outs[0], list(outs[1:])


def _adamw(name, w, m, v, grads):
    R, C = w.shape
    tr = _rows_tile(R, C)
    c1 = 1.0 - ADAM_B1 ** ADAM_STEP
    c2 = 1.0 - ADAM_B2 ** ADAM_STEP
    ng = len(grads)

    def body(*refs):
        w_ref, m_ref, v_ref = refs[:3]
        g_refs = refs[3:3 + ng]
        g_ref, d_ref, nm_ref, nv_ref = refs[3 + ng:]
        gv = g_refs[0][...].astype(F32)
        for r in g_refs[1:]:
            gv = gv + r[...].astype(F32)
        g_ref[...] = gv
        nm = ADAM_B1 * m_ref[...] + (1.0 - ADAM_B1) * gv
        nv = ADAM_B2 * v_ref[...] + (1.0 - ADAM_B2) * (gv * gv)
        nm_ref[...] = nm
        nv_ref[...] = nv
        d_ref[...] = -ADAM_LR * ((nm / c1) / (jnp.sqrt(nv / c2) + ADAM_EPS) + ADAM_WD * w_ref[...])

    blk = pl.BlockSpec((tr, C), lambda i: (i, 0))
    out = jax.ShapeDtypeStruct((R, C), F32)
    return pl.pallas_call(
        body,
        name=name,
        grid=(R // tr,),
        in_specs=[blk] * (3 + ng),
        out_specs=[blk] * 4,
        out_shape=[out] * 4,
        compiler_params=_cp(("parallel",)),
    )(w, m, v, *grads)


def _pack(parts):
    rows = []
    for p in parts:
        flat = p.reshape(-1).astype(F32)
        n = -(-flat.shape[0] // (8 * LANE)) * (8 * LANE)
        rows.append(jnp.pad(flat, (0, n - flat.shape[0])).reshape(n // LANE, LANE))
    return jnp.concatenate(rows, axis=0)


def _unpack(packed, like):
    out, r = [], 0
    for p in like:
        size = int(np.prod(p.shape))
        n = -(-size // (8 * LANE)) * 8
        out.append(packed[r:r + n].reshape(-1)[:size].reshape(p.shape))
        r += n
    return out


def kernel(x, mix_norm, ffn_norm, w_ffn_in, w_ffn_out, ab_w_in, ab_gn_gain, ab_w_pool, ab_pool_scale, ab_w_out, c_w_qkv, c_rel_bias, c_w_out, final_norm, loss_target, m_mix_norm, m_ffn_norm, m_w_ffn_in, m_w_ffn_out, m_ab_w_in, m_ab_gn_gain, m_ab_w_pool, m_ab_pool_scale, m_ab_w_out, m_c_w_qkv, m_c_rel_bias, m_c_w_out, m_final_norm, v_mix_norm, v_ffn_norm, v_w_ffn_in, v_w_ffn_out, v_ab_w_in, v_ab_gn_gain, v_ab_w_pool, v_ab_pool_scale, v_ab_w_out, v_c_w_qkv, v_c_rel_bias, v_c_w_out, v_final_norm):
    w = dict(mix_norm=mix_norm, ffn_norm=ffn_norm, w_ffn_in=w_ffn_in, w_ffn_out=w_ffn_out, ab_w_in=ab_w_in,
             ab_gn_gain=ab_gn_gain, ab_w_pool=ab_w_pool, ab_pool_scale=ab_pool_scale, ab_w_out=ab_w_out,
             c_w_qkv=c_w_qkv, c_rel_bias=c_rel_bias, c_w_out=c_w_out, final_norm=final_norm)
    m = dict(mix_norm=m_mix_norm, ffn_norm=m_ffn_norm, w_ffn_in=m_w_ffn_in, w_ffn_out=m_w_ffn_out, ab_w_in=m_ab_w_in,
             ab_gn_gain=m_ab_gn_gain, ab_w_pool=m_ab_w_pool, ab_pool_scale=m_ab_pool_scale, ab_w_out=m_ab_w_out,
             c_w_qkv=m_c_w_qkv, c_rel_bias=m_c_rel_bias, c_w_out=m_c_w_out, final_norm=m_final_norm)
    v = dict(mix_norm=v_mix_norm, ffn_norm=v_ffn_norm, w_ffn_in=v_w_ffn_in, w_ffn_out=v_w_ffn_out, ab_w_in=v_ab_w_in,
             ab_gn_gain=v_ab_gn_gain, ab_w_pool=v_ab_w_pool, ab_pool_scale=v_ab_pool_scale, ab_w_out=v_ab_w_out,
             c_w_qkv=v_c_w_qkv, c_rel_bias=v_c_rel_bias, c_w_out=v_c_w_out, final_norm=v_final_norm)
    S = x.shape[1]
    cx, cy, cc = lax.axis_index("x"), lax.axis_index("y"), lax.axis_index("c")
    chip = jnp.reshape(2 * cx + cy, (1,)).astype(jnp.int32)

    big = _Weights({k: w[k].astype(BF16) for k in _BIG})
    small = {k: w[k] for k in _SMALL}
    loss, grad_x, g_small, g_big, landed = _local_step(x.reshape(S, D_MODEL), loss_target.reshape(S, D_MODEL), small, big)

    sums = [_chip_sum(f"chip_sum_{k}", g_big[k], landed[k], chip, _SHARD_AXIS[k]) for k in _BIG]

    packed, siblings = _all_reduce_small(_pack([g_small[k] for k in _SMALL] + [loss]), sums)
    small_like = [w[k] for k in _SMALL]
    g_red = dict(zip(_SMALL, _unpack(packed, small_like)))
    loss_row = packed.shape[0] - 8
    loss_out = packed[loss_row, 0]

    grad, delta, new_m, new_v = {}, {}, {}, {}
    for k, mine, theirs in zip(_BIG, sums, siblings):
        shp = w[k].shape
        two = (shp[0] * shp[1], shp[2])
        outs = _adamw(f"adamw_{k}", w[k].reshape(two), m[k].reshape(two), v[k].reshape(two),
                      (mine.reshape(two), theirs.reshape(two)))
        grad[k], delta[k], new_m[k], new_v[k] = [o.reshape(shp) for o in outs]
    _, d, nm, nv = _adamw("adamw_small", _pack(small_like), _pack([m[k] for k in _SMALL]), _pack([v[k] for k in _SMALL]),
                          (packed[:loss_row],))
    for k, dk, mk, vk in zip(_SMALL, _unpack(d, small_like), _unpack(nm, small_like), _unpack(nv, small_like)):
        grad[k], delta[k], new_m[k], new_v[k] = g_red[k], dk, mk, vk

    order = ("mix_norm", "ffn_norm", "w_ffn_in", "w_ffn_out", "ab_w_in", "ab_gn_gain", "ab_w_pool", "ab_pool_scale",
             "ab_w_out", "c_w_qkv", "c_rel_bias", "c_w_out", "final_norm")
    return (loss_out, grad_x.reshape(x.shape), *[grad[k] for k in order], *[delta[k] for k in order],
            *[new_m[k] for k in order], *[new_v[k] for k in order])
```

```python
import functools
from typing import Callable, NamedTuple

import numpy as np
import jax
import jax.numpy as jnp
from jax import lax
from jax.experimental import pallas as pl
from jax.experimental.pallas import tpu as pltpu

F32 = jnp.float32
BF16 = jnp.bfloat16

D_MODEL = 1024
D_FF = 4096
DEPTH = 4
CHUNK = 64
RMS_EPS = 1e-6
RET_WIDTH = 512
RET_HEADS = 4
RET_HEAD_DIM = 128
RET_ROPE_BASE = 10000.0
GN_EPS = 1e-5
POOL_WIDTH = 512
POOL_WINDOWS = (2, 4, 8, 16)
POOL_GROUP_DIM = 128
POOL_HALO = 16
AB_IN_WIDTH = 2560
ATT_HEADS = 16
ATT_HEAD_DIM = 64
LEFT_CHUNKS = 8
BAND = (LEFT_CHUNKS + 1) * CHUNK
REL_CLIP = 128
N_REL = 2 * REL_CLIP + 1
N_REL_PAD = 264
NEG_INF = -1e30
KSCALE = RET_HEAD_DIM ** -0.5
QSCALE = ATT_HEAD_DIM ** -0.5

ADAM_LR = 0.001
ADAM_B1 = 0.9
ADAM_B2 = 0.999
ADAM_EPS = 1e-08
ADAM_WD = 0.01
ADAM_STEP = 10

ATT_BLOCK = LEFT_CHUNKS * CHUNK
RET_BLOCK = 256
N_CHIPS = 4
N_DEV = 8
LANE = 128
VMEM_LIMIT = 52 * 1024 * 1024
EPI_ROWS = 256
MESH = pl.DeviceIdType.MESH


def _cp(sem, vmem=VMEM_LIMIT):
    return pltpu.CompilerParams(dimension_semantics=sem, vmem_limit_bytes=vmem)


def _dot(a, b):
    return lax.dot_general(a, b, (((1,), (0,)), ((), ())), preferred_element_type=F32)


def _dot_nt(a, b):
    return lax.dot_general(a, b, (((1,), (1,)), ((), ())), preferred_element_type=F32)


def _dot_tn(a, b):
    return lax.dot_general(a, b, (((0,), (0,)), ((), ())), preferred_element_type=F32)


_ANY = pl.BlockSpec(memory_space=pl.ANY)


class _Rider(NamedTuple):
    operands: tuple
    out_shapes: tuple
    n_sems: int
    start: Callable
    finish: Callable
    aliases: tuple = ()
    relay: Callable = None


def _mm(name, mode, a, b, *, la=None, lb=None, tm=1024, tn=1024, tk=1024, a_fn=None, b_fn=None,
        extras=(), aux=(), sides=(), epi=None, out_dtype=F32, stack=None, rider=None):
    a_parts = list(a) if isinstance(a, (list, tuple)) else [a]
    b_parts = list(b) if isinstance(b, (list, tuple)) else [b]
    na, nbp = len(a_parts), len(b_parts)
    a2, b2 = list(a_parts[0].shape[-2:]), list(b_parts[0].shape[-2:])
    a2[1] *= na
    b2[1] *= nbp
    if mode == "nn":
        (M, K), (K2, N) = a2, b2
    elif mode == "nt":
        (M, K), (N, K2) = a2, b2
    else:
        (K, M), (K2, N) = a2, b2
    assert K == K2, (name, a2, b2)
    tm, tn, tk = min(tm, M), min(tn, N), min(tk, K)
    assert M % tm == 0 and N % tn == 0 and K % tk == 0, (name, M, N, K, tm, tn, tk)
    gm, gn, gk = M // tm, N // tn, K // tk
    fold = mode == "nt" and na > 1 and gk == 1

    def specs(parts, block, idx, lead):
        per = parts[0].shape[-1] // block[1]
        assert parts[0].shape[-1] % block[1] == 0, (name, parts[0].shape, block)
        out = []
        for p in range(len(parts)):
            def f(i, j, k, p=p):
                r, c = idx(i, j, k)
                if len(parts) > 1:
                    c = jnp.clip(c - p * per, 0, per - 1)
                return (r, c) if lead is None else (lead, r, c)
            out.append(pl.BlockSpec(block if lead is None else (None,) + block, f))
        return out, per

    if mode == "nn":
        a_specs, a_per = specs(a_parts, (tm, tk), lambda i, j, k: (i, k), la)
        b_specs, b_per = specs(b_parts, (tk, tn), lambda i, j, k: (k, j), lb)
        a_axis, b_axis, dot = 2, 1, _dot
    elif mode == "nt":
        if fold:
            a_specs, a_per = [pl.BlockSpec((tm, K // na), lambda i, j, k: (i, 0)) for _ in a_parts], 1
        else:
            a_specs, a_per = specs(a_parts, (tm, tk), lambda i, j, k: (i, k), la)
        b_specs, b_per = specs(b_parts, (tn, tk), lambda i, j, k: (j, k), lb)
        a_axis, b_axis, dot = 2, 2, _dot_nt
    else:
        a_specs, a_per = specs(a_parts, (tk, tm), lambda i, j, k: (k, i), la)
        b_specs, b_per = specs(b_parts, (tk, tn), lambda i, j, k: (k, j), lb)
        a_axis, b_axis, dot = 0, 1, _dot_tn
    ex_specs = [pl.BlockSpec((tm, tn), lambda i, j, k: (i, j)) for _ in extras]
    n_ex = len(extras)

    n_aux, n_side = len(aux), len(sides)
    operands = a_parts + b_parts + list(extras) + list(aux)
    in_specs = a_specs + b_specs + ex_specs + [pl.BlockSpec(v.shape, lambda i, j, k, nd=v.ndim: (0,) * nd) for v in aux]
    aliases = {}
    if stack is None:
        out_specs = [pl.BlockSpec((tm, tn), lambda i, j, k: (i, j))]
        out_shapes = [jax.ShapeDtypeStruct((M, N), out_dtype)]
    else:
        n_layers, layer, prev = stack
        out_specs = [pl.BlockSpec((None, tm, tn), lambda i, j, k: (layer, i, j))]
        out_shapes = [jax.ShapeDtypeStruct((n_layers, M, N), out_dtype)]
        if prev is not None:
            aliases = {len(operands): 0}
            operands.append(prev)
            in_specs.append(_ANY)
    for kind, dtype in sides:
        if kind == "tile":
            out_specs.append(pl.BlockSpec((tm, tn), lambda i, j, k: (i, j)))
            out_shapes.append(jax.ShapeDtypeStruct((M, N), dtype))
        else:
            assert gn == 1, name
            out_specs.append(pl.BlockSpec((8, tn), lambda i, j, k: (0, 0)))
            out_shapes.append(jax.ShapeDtypeStruct((8, N), dtype))
    n_prev = len(aliases)
    scratch = [pltpu.VMEM((tm, tn), F32)] if gk > 1 else []
    n_rin = n_rout = 0
    if rider is not None:
        n_rin, n_rout = len(rider.operands), len(rider.out_shapes)
        for src, dst in rider.aliases:
            aliases[len(operands) + src] = 1 + n_side + dst
        operands += list(rider.operands)
        in_specs += [_ANY] * n_rin
        out_specs += [_ANY] * n_rout
        out_shapes += list(rider.out_shapes)
        scratch += [pltpu.SemaphoreType.DMA((rider.n_sems,)), pltpu.SemaphoreType.DMA((rider.n_sems,))]
    assert na == 1 or nbp == 1, name

    def body(*refs):
        a_refs, b_refs = refs[:na], refs[na:na + nbp]
        ex_refs = refs[na + nbp:na + nbp + n_ex + n_aux]
        n_in = na + nbp + n_ex + n_aux + n_prev
        rin = refs[n_in:n_in + n_rin]
        o_ref = refs[n_in + n_rin]
        side_refs = refs[n_in + n_rin + 1:n_in + n_rin + 1 + n_side]
        rout = refs[n_in + n_rin + 1 + n_side:n_in + n_rin + 1 + n_side + n_rout]
        rest = refs[n_in + n_rin + 1 + n_side + n_rout:]
        i, j, k = pl.program_id(0), pl.program_id(1), pl.program_id(2)
        if rider is not None:
            sems = rest[-2:]

            @pl.when(jnp.logical_and(i == 0, jnp.logical_and(j == 0, k == 0)))
            def _():
                rider.start(rin, rout, *sems)

        def finish(acc):
            if epi is None:
                o_ref[...] = acc[...].astype(o_ref.dtype)
                return
            strip = min(tm, EPI_ROWS)
            colsums = [None] * n_side
            for r0 in range(0, tm, strip):
                rows = slice(r0, r0 + strip)
                res = epi(acc[rows, :], *[r[rows, :] for r in ex_refs[:n_ex]], *[r[...] for r in ex_refs[n_ex:]])
                if n_side:
                    res, *side_vals = res
                    for s, ((kind, _), ref, val) in enumerate(zip(sides, side_refs, side_vals)):
                        if kind == "tile":
                            ref[rows, :] = val.astype(ref.dtype)
                        else:
                            colsums[s] = val if colsums[s] is None else colsums[s] + val
                o_ref[rows, :] = res.astype(o_ref.dtype)
            for (kind, _), ref, val in zip(sides, side_refs, colsums):
                if kind == "colsum":
                    @pl.when(i == 0)
                    def _(ref=ref, val=val):
                        ref[...] = val

                    @pl.when(i > 0)
                    def _(ref=ref, val=val):
                        ref[...] += val

                    @pl.when(i == gm - 1)
                    def _(ref=ref):
                        ref[0:1, :] = jnp.sum(ref[...], axis=0, keepdims=True)

        def step(a_ref, b_ref):
            av, bv = a_ref[...], b_ref[...]
            if a_fn is not None:
                av = a_fn(av)
            if b_fn is not None:
                bv = b_fn(bv)
            part = dot(av.astype(BF16), bv.astype(BF16))
            if gk == 1:
                finish(part)
                return
            acc_ref = rest[0]

            @pl.when(k == 0)
            def _():
                acc_ref[...] = part

            @pl.when(k > 0)
            def _():
                acc_ref[...] += part

        if fold:
            kp = K // na
            finish(sum(dot(a_refs[p][...].astype(BF16), b_refs[0][:, p * kp:(p + 1) * kp].astype(BF16))
                       for p in range(na)))
        elif na > 1:
            sel = pl.program_id(a_axis) // a_per
            for p in range(na):
                pl.when(sel == p)(functools.partial(step, a_refs[p], b_refs[0]))
        elif nbp > 1:
            sel = pl.program_id(b_axis) // b_per
            for p in range(nbp):
                pl.when(sel == p)(functools.partial(step, a_refs[0], b_refs[p]))
        else:
            step(a_refs[0], b_refs[0])
        if gk > 1:
            @pl.when(k == gk - 1)
            def _():
                finish(rest[0])

        if rider is not None:
            steps = gm * gn * gk
            step_no = (i * gn + j) * gk + k
            if rider.relay is not None:
                assert steps >= 3, name

                @pl.when(step_no == steps - 2)
                def _():
                    rider.relay(rin, rout, *sems)

            @pl.when(step_no == steps - 1)
            def _():
                rider.finish(rin, rout, *sems)

    sequential = rider is not None or any(kind == "colsum" for kind, _ in sides)
    sem = ("arbitrary",) * 3 if sequential else ("parallel", "parallel", "arbitrary")
    outs = pl.pallas_call(
        body,
        name=name,
        grid=(gm, gn, gk),
        in_specs=in_specs,
        out_specs=out_specs,
        out_shape=out_shapes,
        input_output_aliases=aliases,
        scratch_shapes=scratch,
        compiler_params=_cp(sem),
    )(*operands)
    res = outs[0] if not sides else tuple(outs[:1 + n_side])
    return res if rider is None else (res, list(outs[1 + n_side:]))


def _relu2(u):
    r = jnp.maximum(u, 0)
    return r * r


def _epi_residual(acc, res):
    return acc + res


def _epi_residual_norm(acc, res, g):
    xn = acc + res
    r = lax.rsqrt(jnp.mean(xn * xn, axis=-1, keepdims=True) + RMS_EPS)
    return xn, (xn * r) * g


def _epi_rms_bwd(dh, x, dres, g):
    r = lax.rsqrt(jnp.mean(x * x, axis=-1, keepdims=True) + RMS_EPS)
    xh = x * r
    dxh = dh * g
    dx = dres + r * (dxh - xh * jnp.mean(dxh * xh, axis=-1, keepdims=True))
    return dx, dx, jnp.sum((dh * xh).reshape(dh.shape[0] // 8, 8, dh.shape[1]), axis=0)


def _rms_fwd(name, x, g, rider=None):
    S, D = x.shape
    tq = min(1024, S)

    def body(x_ref, g_ref, o_ref):
        xv = x_ref[...]
        r = lax.rsqrt(jnp.mean(xv * xv, axis=-1, keepdims=True) + RMS_EPS)
        o_ref[...] = ((xv * r) * g_ref[...]).astype(o_ref.dtype)

    grid = (S // tq,)
    body, r_in, r_out, r_shapes, r_sems, aliases = _riding(rider, body, 2, 1, grid)
    outs = pl.pallas_call(
        body,
        name=name,
        grid=grid,
        in_specs=[pl.BlockSpec((tq, D), lambda i: (i, 0)), pl.BlockSpec((1, D), lambda i: (0, 0))] + r_in,
        out_specs=[pl.BlockSpec((tq, D), lambda i: (i, 0))] + r_out,
        out_shape=[jax.ShapeDtypeStruct((S, D), BF16)] + r_shapes,
        input_output_aliases=aliases,
        scratch_shapes=r_sems,
        compiler_params=_cp(("parallel" if rider is None else "arbitrary",)),
    )(x, g, *(rider.operands if rider is not None else ()))
    return outs[0] if rider is None else (outs[0], list(outs[1:]))


def _loss_head(x, g, t):
    S, D = x.shape
    tq = min(512, S)
    n = S // tq

    def body(x_ref, g_ref, t_ref, loss_ref, dx_ref, dxb_ref, dg_ref, lacc_ref, gacc_ref):
        i = pl.program_id(0)
        xv = x_ref[...]
        gv = g_ref[...]
        r = lax.rsqrt(jnp.mean(xv * xv, axis=-1, keepdims=True) + RMS_EPS)
        xh = xv * r
        e = xh * gv - t_ref[...]
        dy = e * (1.0 / D)
        dxh = dy * gv
        dx = r * (dxh - xh * jnp.mean(dxh * xh, axis=-1, keepdims=True))
        dx_ref[...] = dx
        dxb_ref[...] = dx.astype(dxb_ref.dtype)
        lpart = jnp.sum((e * e).reshape(tq // 8, 8, D), axis=0)
        gpart = jnp.sum((dy * xh).reshape(tq // 8, 8, D), axis=0)

        @pl.when(i == 0)
        def _():
            lacc_ref[...] = lpart
            gacc_ref[...] = gpart

        @pl.when(i > 0)
        def _():
            lacc_ref[...] += lpart
            gacc_ref[...] += gpart

        @pl.when(i == n - 1)
        def _():
            dg_ref[...] = jnp.sum(gacc_ref[...], axis=0, keepdims=True)
            tot = jnp.sum(jnp.sum(lacc_ref[...], axis=0, keepdims=True), axis=1, keepdims=True)
            loss_ref[...] = jnp.broadcast_to(tot * (0.5 / D), (1, LANE))

    return pl.pallas_call(
        body,
        name="loss_head",
        grid=(n,),
        in_specs=[pl.BlockSpec((tq, D), lambda i: (i, 0)), pl.BlockSpec((1, D), lambda i: (0, 0)),
                  pl.BlockSpec((tq, D), lambda i: (i, 0))],
        out_specs=[pl.BlockSpec((1, LANE), lambda i: (0, 0)), pl.BlockSpec((tq, D), lambda i: (i, 0)),
                   pl.BlockSpec((tq, D), lambda i: (i, 0)), pl.BlockSpec((1, D), lambda i: (0, 0))],
        out_shape=[jax.ShapeDtypeStruct((1, LANE), F32), jax.ShapeDtypeStruct((S, D), F32),
                   jax.ShapeDtypeStruct((S, D), BF16), jax.ShapeDtypeStruct((1, D), F32)],
        scratch_shapes=[pltpu.VMEM((8, D), F32), pltpu.VMEM((8, D), F32)],
        compiler_params=_cp(("arbitrary",)),
    )(x, g, t)


def _ret_tables(S):
    T = min(RET_BLOCK, S)
    inv_freq = 1.0 / (RET_ROPE_BASE ** jnp.linspace(0.0, 1.0, RET_HEAD_DIM // 2, dtype=F32))
    ang = jnp.arange(S, dtype=F32)[:, None] * jnp.repeat(inv_freq, 2)[None, :]
    cosf = jnp.cos(ang)
    sins = jnp.sin(ang) * jnp.asarray(np.tile([-1.0, 1.0], RET_HEAD_DIM // 2), F32)[None, :]
    log_g = np.log1p(-np.power(2.0, -5.0 - np.arange(RET_HEADS, dtype=np.float64)))
    pos = np.arange(T, dtype=np.float64)
    diff = pos[:, None] - pos[None, :]
    same = (pos[:, None] // CHUNK) == (pos[None, :] // CHUNK)
    seen = same | (diff > 0)
    dmat = np.where(seen[None], np.exp(np.abs(diff)[None] * log_g[:, None, None]), 0.0)
    aq = np.exp((pos[None, :] + 1.0) * log_g[:, None])
    ak = np.exp((T - 1.0 - pos[None, :]) * log_g[:, None])
    lam = np.exp(T * log_g)
    bc = lambda v: jnp.asarray(np.broadcast_to(v[..., None], v.shape + (LANE,)), F32)
    return dict(cos=cosf, sin=sins, dmat=jnp.asarray(dmat, F32), aq=bc(aq), ak=bc(ak),
                lam=jnp.asarray(np.broadcast_to(lam[:, None, None], (RET_HEADS, 1, LANE)), F32))


def _rot(x, cos, sin_s, even):
    sw = jnp.where(even, pltpu.roll(x, LANE - 1, 1), pltpu.roll(x, 1, 1))
    return x * cos + sw * sin_s


def _rot_t(dy, cos, sin_s, even):
    t = dy * sin_s
    return dy * cos + jnp.where(even, pltpu.roll(t, LANE - 1, 1), pltpu.roll(t, 1, 1))


def _ret_specs(T, rev_nb=None):
    blk = (lambda b: b) if rev_nb is None else (lambda b: rev_nb - 1 - b)
    whole = lambda shape: pl.BlockSpec(shape, lambda b: (0,) * len(shape))
    specs = [pl.BlockSpec((T, AB_IN_WIDTH), lambda b: (blk(b), 0)),
             pl.BlockSpec((T, LANE), lambda b: (blk(b), 0)),
             pl.BlockSpec((T, LANE), lambda b: (blk(b), 0)),
             whole((RET_HEADS, T, T)), whole((RET_HEADS, T, LANE)), whole((RET_HEADS, T, LANE)),
             whole((RET_HEADS, 1, LANE)), whole((1, RET_WIDTH))]
    return specs, blk


def _head_views(h, z_ref, tabs, token_refs, head_refs):
    zs = [z_ref.at[:, (o * RET_HEADS + h) * LANE:(o * RET_HEADS + h + 1) * LANE] for o in range(4)]
    hs = slice(h * LANE, (h + 1) * LANE)
    return zs, [t.at[h] for t in tabs], [r.at[:, hs] for r in token_refs], [r.at[h] for r in head_refs]


def _ret_fwd(name, z, tb, gain, rider=None):
    S = z.shape[0]
    T = min(RET_BLOCK, S)
    nb = S // T
    specs, blk = _ret_specs(T)

    def body(z_ref, cos_r, sin_r, d_all, aq_all, ak_all, lam_all, gain_all, cat_all, opre_all, st_all, state_all):
        @pl.when(pl.program_id(0) == 0)
        def _():
            state_all[...] = jnp.zeros_like(state_all)

        for h in range(RET_HEADS):
            zs, tabs, toks, heads = _head_views(h, z_ref, (d_all, aq_all, ak_all, lam_all),
                                                (gain_all, cat_all, opre_all), (st_all, state_all))
            head(*zs, cos_r, sin_r, *tabs, *toks, *heads)

    def head(zq, zk, zv, zg, cos_r, sin_r, d_r, aq_r, ak_r, lam_r, gain_r, ret_o, opre_o, st_o, state):
        even = (lax.broadcasted_iota(jnp.int32, (T, LANE), 1) & 1) == 0
        c, s = cos_r[...], sin_r[...]
        q = _rot(zq[...].astype(F32), c, s, even)
        k = _rot(zk[...].astype(F32), c, s, even) * KSCALE
        qb, kb, vb = q.astype(BF16), k.astype(BF16), zv[...].astype(BF16)
        p = (_dot_nt(qb, kb) * d_r[...]).astype(BF16)
        st = state[...]
        st_o[...] = st
        o = _dot(p, vb) + _dot((q * aq_r[...]).astype(BF16), st.astype(BF16))
        state[...] = st * lam_r[...] + _dot_tn((k * ak_r[...]).astype(BF16), vb)
        opre_o[...] = o
        mu = jnp.mean(o, axis=-1, keepdims=True)
        d = o - mu
        y = d * lax.rsqrt(jnp.mean(d * d, axis=-1, keepdims=True) + GN_EPS)
        g = zg[...].astype(F32)
        ret_o[...] = ((g * jax.nn.sigmoid(g)) * (y * gain_r[...])).astype(ret_o.dtype)

    out_blk = pl.BlockSpec((T, RET_WIDTH), lambda b: (b, 0))
    grid = (nb,)
    body, r_in, r_out, r_shapes, r_sems, aliases = _riding(rider, body, 8, 3, grid)
    outs = pl.pallas_call(
        body,
        name=name,
        grid=grid,
        in_specs=specs + r_in,
        out_specs=[out_blk, out_blk, pl.BlockSpec((RET_HEADS, None, LANE, LANE), lambda b: (0, b, 0, 0))] + r_out,
        out_shape=[jax.ShapeDtypeStruct((S, D_MODEL), BF16), jax.ShapeDtypeStruct((S, RET_WIDTH), F32),
                   jax.ShapeDtypeStruct((RET_HEADS, nb, LANE, LANE), F32)] + r_shapes,
        input_output_aliases=aliases,
        scratch_shapes=[pltpu.VMEM((RET_HEADS, LANE, LANE), F32)] + r_sems,
        compiler_params=_cp(("arbitrary",)),
    )(z, tb["cos"], tb["sin"], tb["dmat"], tb["aq"], tb["ak"], tb["lam"], gain,
      *(rider.operands if rider is not None else ()))
    return tuple(outs[:3]) if rider is None else (tuple(outs[:3]), list(outs[3:]))


def _ret_bwd(name, z, tb, gain, opre, states, dcat, rider=None):
    S = z.shape[0]
    T = min(RET_BLOCK, S)
    nb = S // T
    specs, blk = _ret_specs(T, rev_nb=nb)
    tok = pl.BlockSpec((T, RET_WIDTH), lambda b: (blk(b), 0))

    def body(z_ref, cos_r, sin_r, d_all, aq_all, ak_all, lam_all, gain_all, opre_all, st_all, dret_all,
             dz_ref, dgain_all, dstate_all):
        @pl.when(pl.program_id(0) == 0)
        def _():
            dstate_all[...] = jnp.zeros_like(dstate_all)
            dgain_all[...] = jnp.zeros_like(dgain_all)

        for h in range(RET_HEADS):
            zs, tabs, toks, heads = _head_views(h, z_ref, (d_all, aq_all, ak_all, lam_all),
                                                (gain_all, opre_all, dret_all, dgain_all), (st_all, dstate_all))
            dzs, _, _, _ = _head_views(h, dz_ref, (), (), ())
            gain_r, opre_r, dret_r, dgain_o = toks
            head(*zs, cos_r, sin_r, *tabs, gain_r, opre_r, heads[0], dret_r, *dzs, dgain_o, heads[1])

    def head(zq, zk, zv, zg, cos_r, sin_r, d_r, aq_r, ak_r, lam_r, gain_r, opre_r, st_r, dret_r,
             dq_o, dk_o, dv_o, dg_o, dgain_o, dstate):
        even = (lax.broadcasted_iota(jnp.int32, (T, LANE), 1) & 1) == 0
        c, s = cos_r[...], sin_r[...]
        aq, ak, dm = aq_r[...], ak_r[...], d_r[...]
        q = _rot(zq[...].astype(F32), c, s, even)
        k = _rot(zk[...].astype(F32), c, s, even) * KSCALE
        qb, kb, vb = q.astype(BF16), k.astype(BF16), zv[...].astype(BF16)
        pb = (_dot_nt(qb, kb) * dm).astype(BF16)
        g = zg[...].astype(F32)
        sig = jax.nn.sigmoid(g)
        o = opre_r[...]
        mu = jnp.mean(o, axis=-1, keepdims=True)
        d = o - mu
        rstd = lax.rsqrt(jnp.mean(d * d, axis=-1, keepdims=True) + GN_EPS)
        y = d * rstd
        gain_v = gain_r[...]
        dret = dret_r[...].astype(F32)
        dyg = dret * (g * sig)
        dg_o[...] = (dret * (y * gain_v) * (sig * (1.0 + g * (1.0 - sig)))).astype(dg_o.dtype)
        dgain_o[...] += jnp.sum(dyg * y, axis=0, keepdims=True)
        dy = dyg * gain_v
        do = rstd * (dy - jnp.mean(dy, axis=-1, keepdims=True) - y * jnp.mean(dy * y, axis=-1, keepdims=True))
        dob = do.astype(BF16)
        stb = st_r[...].astype(BF16)
        dsn = dstate[...]
        dsnb = dsn.astype(BF16)
        dpb = (_dot_nt(dob, vb) * dm).astype(BF16)
        dq = _dot(dpb, kb) + _dot_nt(dob, stb) * aq
        dk = _dot_tn(dpb, qb) + _dot_nt(vb, dsnb) * ak
        dv = _dot_tn(pb, dob) + _dot((k * ak).astype(BF16), dsnb)
        dstate[...] = dsn * lam_r[...] + _dot_tn((q * aq).astype(BF16), dob)
        dq_o[...] = _rot_t(dq, c, s, even).astype(dq_o.dtype)
        dk_o[...] = _rot_t(dk * KSCALE, c, s, even).astype(dk_o.dtype)
        dv_o[...] = dv.astype(dv_o.dtype)

    grid = (nb,)
    body, r_in, r_out, r_shapes, r_sems, aliases = _riding(rider, body, 11, 2, grid)
    outs = pl.pallas_call(
        body,
        name=name,
        grid=grid,
        in_specs=specs + [tok, pl.BlockSpec((RET_HEADS, None, LANE, LANE), lambda b: (0, blk(b), 0, 0)), tok] + r_in,
        out_specs=[pl.BlockSpec((T, 4 * RET_WIDTH), lambda b: (blk(b), 0)),
                   pl.BlockSpec((1, RET_WIDTH), lambda b: (0, 0))] + r_out,
        out_shape=[jax.ShapeDtypeStruct((S, AB_IN_WIDTH), BF16), jax.ShapeDtypeStruct((1, RET_WIDTH), F32)] + r_shapes,
        input_output_aliases=aliases,
        scratch_shapes=[pltpu.VMEM((RET_HEADS, LANE, LANE), F32)] + r_sems,
        compiler_params=_cp(("arbitrary",)),
    )(z, tb["cos"], tb["sin"], tb["dmat"], tb["aq"], tb["ak"], tb["lam"], gain, opre, states, dcat,
      *(rider.operands if rider is not None else ()))
    return tuple(outs[:2]) if rider is None else (tuple(outs[:2]), list(outs[2:]))


def _pool_counts(t0, rows):
    t = t0 + lax.broadcasted_iota(jnp.int32, (rows, POOL_WIDTH), 0)
    grp = lax.broadcasted_iota(jnp.int32, (rows, POOL_WIDTH), 1) >> 7
    win = jnp.where(grp == 0, POOL_WINDOWS[0], jnp.where(grp == 1, POOL_WINDOWS[1],
                    jnp.where(grp == 2, POOL_WINDOWS[2], POOL_WINDOWS[3])))
    return jnp.maximum(jnp.minimum(t + 1, win), 1).astype(F32), grp


def _window_sums(ext, grp, sign):
    n = ext.shape[0]
    sh = lambda v, k: pltpu.roll(v, k % n if sign > 0 else (n - k) % n, 0)
    s2 = ext + sh(ext, 1)
    s4 = s2 + sh(s2, 2)
    s8 = s4 + sh(s4, 4)
    s16 = s8 + sh(s8, 8)
    return jnp.where(grp == 0, s2, jnp.where(grp == 1, s4, jnp.where(grp == 2, s8, s16)))


def _pool_fwd(name, z, w_pool, scale, cat):
    S = z.shape[0]
    T = min(512, S)
    nb = S // T
    pcol = AB_IN_WIDTH // POOL_WIDTH - 1
    hb = T // POOL_HALO

    def body(p_ref, halo_ref, w_ref, sc_ref, cat_in, out_ref, pooled_ref):
        b = pl.program_id(0)
        cur = p_ref[...].astype(F32)
        halo = jnp.where(b > 0, halo_ref[...].astype(F32), 0.0)
        ext = jnp.concatenate([halo, cur], axis=0)
        cnt, grp = _pool_counts(b * T - POOL_HALO, T + POOL_HALO)
        sums = _window_sums(ext, grp, +1)
        pooled = (sums / cnt)[POOL_HALO:] - cur
        pb = pooled.astype(BF16)
        pooled_ref[...] = pb
        for gi in range(len(POOL_WINDOWS)):
            cs = slice(gi * POOL_GROUP_DIM, (gi + 1) * POOL_GROUP_DIM)
            mixed = _dot(pb[:, cs], w_ref[gi].astype(BF16))
            out_ref[:, cs] = (mixed * sc_ref[:, cs]).astype(out_ref.dtype)

    return pl.pallas_call(
        body,
        name=name,
        grid=(nb,),
        in_specs=[pl.BlockSpec((T, POOL_WIDTH), lambda b: (b, pcol)),
                  pl.BlockSpec((POOL_HALO, POOL_WIDTH), lambda b: (jnp.maximum(b * hb - 1, 0), pcol)),
                  pl.BlockSpec((4, POOL_GROUP_DIM, POOL_GROUP_DIM), lambda b: (0, 0, 0)),
                  pl.BlockSpec((1, POOL_WIDTH), lambda b: (0, 0)), _ANY],
        out_specs=[pl.BlockSpec((T, POOL_WIDTH), lambda b: (b, 1)), pl.BlockSpec((T, POOL_WIDTH), lambda b: (b, 0))],
        out_shape=[jax.ShapeDtypeStruct(cat.shape, cat.dtype), jax.ShapeDtypeStruct((S, POOL_WIDTH), BF16)],
        input_output_aliases={4: 0},
        compiler_params=_cp(("parallel",)),
    )(z, z, w_pool, scale, cat)


def _pool_bwd(name, pooled, w_pool, scale, dcat, dz):
    S = pooled.shape[0]
    T = min(512, S)
    nb = S // T
    hb = T // POOL_HALO
    last_h = S // POOL_HALO - 1
    pcol = AB_IN_WIDTH // POOL_WIDTH - 1

    def body(d_ref, dn_ref, pooled_ref, w_ref, sc_ref, dz_in, dp_ref, dw_ref, dsc_ref):
        b = pl.program_id(0)

        @pl.when(b == 0)
        def _():
            dw_ref[...] = jnp.zeros_like(dw_ref)
            dsc_ref[...] = jnp.zeros_like(dsc_ref)

        sc = sc_ref[...]
        dout = d_ref[...].astype(F32)
        dnext = jnp.where(b < nb - 1, dn_ref[...].astype(F32), 0.0)
        dmix = jnp.concatenate([dout, dnext], axis=0) * sc
        dmb = dmix.astype(BF16)
        pb = pooled_ref[...]
        dpooled = []
        for gi in range(len(POOL_WINDOWS)):
            cs = slice(gi * POOL_GROUP_DIM, (gi + 1) * POOL_GROUP_DIM)
            wb = w_ref[gi].astype(BF16)
            dpooled.append(_dot_nt(dmb[:, cs], wb))
            dw_ref[gi] += _dot_tn(pb[:, cs], dmb[:T, cs])
            mixed = _dot(pb[:, cs], wb)
            dsc_ref[:, cs] += jnp.sum(dout[:, cs] * mixed, axis=0, keepdims=True)
        dpl = jnp.concatenate(dpooled, axis=1)
        cnt, grp = _pool_counts(b * T, T + POOL_HALO)
        sums = _window_sums(dpl / cnt, grp, -1)
        dp_ref[...] = (sums[:T] - dpl[:T]).astype(dp_ref.dtype)

    return pl.pallas_call(
        body,
        name=name,
        grid=(nb,),
        in_specs=[pl.BlockSpec((T, POOL_WIDTH), lambda b: (b, 1)),
                  pl.BlockSpec((POOL_HALO, POOL_WIDTH), lambda b: (jnp.minimum((b + 1) * hb, last_h), 1)),
                  pl.BlockSpec((T, POOL_WIDTH), lambda b: (b, 0)),
                  pl.BlockSpec((4, POOL_GROUP_DIM, POOL_GROUP_DIM), lambda b: (0, 0, 0)),
                  pl.BlockSpec((1, POOL_WIDTH), lambda b: (0, 0)), _ANY],
        out_specs=[pl.BlockSpec((T, POOL_WIDTH), lambda b: (b, pcol)),
                   pl.BlockSpec((4, POOL_GROUP_DIM, POOL_GROUP_DIM), lambda b: (0, 0, 0)),
                   pl.BlockSpec((1, POOL_WIDTH), lambda b: (0, 0))],
        out_shape=[jax.ShapeDtypeStruct(dz.shape, dz.dtype),
                   jax.ShapeDtypeStruct((4, POOL_GROUP_DIM, POOL_GROUP_DIM), F32),
                   jax.ShapeDtypeStruct((1, POOL_WIDTH), F32)],
        input_output_aliases={5: 0},
        compiler_params=_cp(("arbitrary",)),
    )(dcat, dcat, pooled, w_pool, scale, dz)


ATT_STRIP = 32
ATT_Q = 256
ATT_W = ATT_Q + LEFT_CHUNKS * CHUNK


def _rel_index():
    j = np.arange(ATT_W)
    rel = np.clip(LEFT_CHUNKS * CHUNK - j, -REL_CLIP, REL_CLIP) + REL_CLIP
    fwd = np.where(j < BAND, rel, N_REL)
    bwd = np.where(j <= ATT_W - CHUNK, fwd, 2 * REL_CLIP)
    return tuple(jnp.asarray(v.reshape(1, ATT_W), jnp.int32) for v in (fwd, bwd))


def _bias_table(name, rel_bias, rel_idx):
    rb = jnp.concatenate([rel_bias, jnp.full((ATT_HEADS, 1), NEG_INF, F32),
                          jnp.zeros((ATT_HEADS, N_REL_PAD - N_REL - 1), F32)], axis=1)

    def body(rb_ref, idx_ref, o_ref, row0_ref):
        r = lax.broadcasted_iota(jnp.int32, (N_REL_PAD, ATT_W), 0)
        onehot = (r == idx_ref[...]).astype(F32)
        row0_ref[...] = jnp.dot(rb_ref[...], onehot, precision=lax.Precision.HIGHEST, preferred_element_type=F32)
        col = lax.broadcasted_iota(jnp.int32, (CHUNK, ATT_W), 1)
        row = lax.broadcasted_iota(jnp.int32, (CHUNK, ATT_W), 0)
        for h in range(ATT_HEADS):
            same = jnp.broadcast_to(row0_ref[pl.ds(h, 1), :], (CHUNK, ATT_W))
            turned = pltpu.roll(same, 0, 1, stride=1, stride_axis=0)
            o_ref[h] = jnp.where(col >= BAND, NEG_INF, jnp.where(col < row, same, turned))

    return pl.pallas_call(
        body,
        name=name,
        out_shape=jax.ShapeDtypeStruct((ATT_HEADS, CHUNK, ATT_W), F32),
        scratch_shapes=[pltpu.VMEM((ATT_HEADS, ATT_W), F32)],
        compiler_params=pltpu.CompilerParams(vmem_limit_bytes=VMEM_LIMIT),
    )(rb, rel_idx)


def _bias_grad(name, dband, rel_idx):
    def body(d_ref, idx_ref, o_ref, sums_ref):
        row = lax.broadcasted_iota(jnp.int32, (CHUNK, ATT_W), 0)
        for h in range(ATT_HEADS):
            back = d_ref[h]
            for bit in range(CHUNK.bit_length() - 1):
                back = jnp.where(((row >> bit) & 1) == 1, pltpu.roll(back, ATT_W - (1 << bit), 1), back)
            sums_ref[pl.ds(h, 1), :] = jnp.sum(back, axis=0, keepdims=True)
        r = lax.broadcasted_iota(jnp.int32, (N_REL_PAD, ATT_W), 0)
        onehot = (r == idx_ref[...]).astype(F32)
        o_ref[...] = lax.dot_general(sums_ref[...], onehot, (((1,), (1,)), ((), ())),
                                     precision=lax.Precision.HIGHEST, preferred_element_type=F32)

    out = pl.pallas_call(
        body,
        name=name,
        out_shape=jax.ShapeDtypeStruct((ATT_HEADS, N_REL_PAD), F32),
        scratch_shapes=[pltpu.VMEM((ATT_HEADS, ATT_W), F32)],
        compiler_params=pltpu.CompilerParams(vmem_limit_bytes=VMEM_LIMIT),
    )(dband, rel_idx)
    return out[:, :N_REL]


def _attn_unit(q_ref, kw_ref, bias_ref, e, u, lane):
    mine = (lane < ATT_HEAD_DIM) if e == 0 else (lane >= ATT_HEAD_DIM)
    qm = jnp.where(mine, q_ref[u * ATT_Q:(u + 1) * ATT_Q, :] * QSCALE, 0)
    kw = kw_ref[u * ATT_Q:u * ATT_Q + ATT_W, :]
    s = _dot_nt(qm, kw) + bias_ref[u, e]
    p = jnp.exp(s - jnp.max(s, axis=-1, keepdims=True))
    return p, 1.0 / jnp.sum(p, axis=-1, keepdims=True), qm, kw, mine


def _attn_in_specs(nb):
    T = ATT_BLOCK
    hp = ATT_HEADS // 2
    cur = lambda off: pl.BlockSpec((T, LANE), lambda h, b: (jnp.minimum(b, nb - 1), off + h))
    prev = lambda off: pl.BlockSpec((T, LANE), lambda h, b: (jnp.clip(b - 1, 0, nb - 1), off + h))
    return [cur(0), prev(hp), cur(hp), prev(2 * hp), cur(2 * hp),
            pl.BlockSpec((None, 2, CHUNK, ATT_W), lambda h, b: (h, 0, 0, 0))]


def _spread_bias(bias_ref, bm_ref, block):
    col = lax.broadcasted_iota(jnp.int32, (CHUNK, ATT_W), 1)
    for first in (True, False):
        @pl.when(block == (0 if first else 1))
        def _(first=first):
            for u in range(ATT_BLOCK // ATT_Q):
                for e in range(2):
                    for j in range(ATT_Q // CHUNK):
                        rows = pltpu.roll(bias_ref[e], j * CHUNK, 1)
                        if first:
                            rows = jnp.where(col >= ATT_BLOCK - u * ATT_Q, rows, NEG_INF)
                        bm_ref[u, e, j * CHUNK:(j + 1) * CHUNK, :] = rows


def _riding(rider, body, n_in, n_out, grid):
    if rider is None:
        return body, [], [], [], [], {}
    n_rin, n_rout = len(rider.operands), len(rider.out_shapes)
    steps = int(np.prod(grid))

    def riding(*refs):
        ins, rin = refs[:n_in], refs[n_in:n_in + n_rin]
        outs = refs[n_in + n_rin:n_in + n_rin + n_out]
        rout = refs[n_in + n_rin + n_out:n_in + n_rin + n_out + n_rout]
        rest = refs[n_in + n_rin + n_out + n_rout:]
        scratch, sems = rest[:-2], rest[-2:]
        step_no = pl.program_id(0)
        for axis in range(1, len(grid)):
            step_no = step_no * grid[axis] + pl.program_id(axis)

        @pl.when(step_no == 0)
        def _():
            rider.start(rin, rout, *sems)

        body(*ins, *outs, *scratch)
        if rider.relay is not None:
            assert steps >= 3

            @pl.when(step_no == steps - 2)
            def _():
                rider.relay(rin, rout, *sems)

        @pl.when(step_no == steps - 1)
        def _():
            rider.finish(rin, rout, *sems)

    sems = [pltpu.SemaphoreType.DMA((rider.n_sems,)), pltpu.SemaphoreType.DMA((rider.n_sems,))]
    aliases = {n_in + src: n_out + dst for src, dst in rider.aliases}
    return riding, [_ANY] * n_rin, [_ANY] * n_rout, list(rider.out_shapes), sems, aliases


def _attn_fwd(name, qkv, bias, rider=None):
    S = qkv.shape[0]
    T = ATT_BLOCK
    nb = S // T

    def body(q_ref, kp_ref, kc_ref, vp_ref, vc_ref, band_ref, o_ref, kw_ref, vw_ref, bias_ref, s_ref, p_ref, inv_ref):
        _spread_bias(band_ref, bias_ref, pl.program_id(1))
        kw_ref[0:T] = kp_ref[...]
        kw_ref[T:2 * T] = kc_ref[...]
        vw_ref[0:T] = vp_ref[...]
        vw_ref[T:2 * T] = vc_ref[...]
        lane = lax.broadcasted_iota(jnp.int32, (ATT_Q, LANE), 1)
        for u in range(T // ATT_Q):
            vw = vw_ref[u * ATT_Q:u * ATT_Q + ATT_W, :]
            kw = kw_ref[u * ATT_Q:u * ATT_Q + ATT_W, :]
            outs = []
            for e in range(2):
                mine = (lane < ATT_HEAD_DIM) if e == 0 else (lane >= ATT_HEAD_DIM)
                qm = jnp.where(mine, q_ref[u * ATT_Q:(u + 1) * ATT_Q, :] * QSCALE, 0)
                s_ref[e] = _dot_nt(qm, kw)
                for r in range(ATT_Q // ATT_STRIP):
                    rows = slice(r * ATT_STRIP, (r + 1) * ATT_STRIP)
                    s = s_ref[e, rows, :] + bias_ref[u, e, rows, :]
                    p = jnp.exp(s - jnp.max(s, axis=-1, keepdims=True))
                    inv_ref[e, rows, :] = jnp.broadcast_to(1.0 / jnp.sum(p, axis=-1, keepdims=True), (ATT_STRIP, LANE))
                    p_ref[e, rows, :] = p.astype(BF16)
                outs.append(_dot(p_ref[e], vw) * inv_ref[e])
            o_ref[u * ATT_Q:(u + 1) * ATT_Q, :] = jnp.where(lane < ATT_HEAD_DIM, outs[0], outs[1]).astype(o_ref.dtype)

    grid = (ATT_HEADS // 2, nb)
    body, r_in, r_out, r_shapes, r_sems, aliases = _riding(rider, body, 6, 1, grid)
    outs = pl.pallas_call(
        body,
        name=name,
        grid=grid,
        in_specs=_attn_in_specs(nb) + r_in,
        out_specs=[pl.BlockSpec((T, LANE), lambda h, b: (b, h))] + r_out,
        out_shape=[jax.ShapeDtypeStruct((S, D_MODEL), BF16)] + r_shapes,
        input_output_aliases=aliases,
        scratch_shapes=[pltpu.VMEM((2 * T, LANE), BF16), pltpu.VMEM((2 * T, LANE), BF16),
                        pltpu.VMEM((T // ATT_Q, 2, ATT_Q, ATT_W), F32), pltpu.VMEM((2, ATT_Q, ATT_W), F32),
                        pltpu.VMEM((2, ATT_Q, ATT_W), BF16), pltpu.VMEM((2, ATT_Q, LANE), F32)] + r_sems,
        compiler_params=_cp(("parallel" if rider is None else "arbitrary", "arbitrary")),
    )(qkv, qkv, qkv, qkv, qkv, bias, *(rider.operands if rider is not None else ()))
    return outs[0] if rider is None else (outs[0], list(outs[1:]))


def _attn_bwd(name, qkv, bias, do, rider=None):
    S = qkv.shape[0]
    T = ATT_BLOCK
    nb = S // T

    def body(q_ref, kp_ref, kc_ref, vp_ref, vc_ref, band_ref, do_ref,
             dq_ref, dk_ref, dv_ref, dband_ref, kw_ref, vw_ref, dkw_ref, dvw_ref, bias_ref, dbias_ref):
        b = pl.program_id(1)

        _spread_bias(band_ref, bias_ref, b)

        @pl.when(b == 0)
        def _():
            dbias_ref[...] = jnp.zeros_like(dbias_ref)
            dkw_ref[:, T:2 * T] = jnp.zeros((LANE, T), F32)
            dvw_ref[:, T:2 * T] = jnp.zeros((LANE, T), F32)

        dkw_ref[:, 0:T] = dkw_ref[:, T:2 * T]
        dvw_ref[:, 0:T] = dvw_ref[:, T:2 * T]
        dkw_ref[:, T:2 * T] = jnp.zeros((LANE, T), F32)
        dvw_ref[:, T:2 * T] = jnp.zeros((LANE, T), F32)

        @pl.when(b < nb)
        def _():
            kw_ref[0:T] = kp_ref[...]
            kw_ref[T:2 * T] = kc_ref[...]
            vw_ref[0:T] = vp_ref[...]
            vw_ref[T:2 * T] = vc_ref[...]
            lane = lax.broadcasted_iota(jnp.int32, (ATT_Q, LANE), 1)
            for u in range(T // ATT_Q):
                rows = slice(u * ATT_Q, (u + 1) * ATT_Q)
                win = slice(u * ATT_Q, u * ATT_Q + ATT_W)
                vw = vw_ref[win, :]
                do2 = do_ref[rows, :]
                dqs, dk, dv = [], None, None
                for e in range(2):
                    p, inv, qm, kw, mine = _attn_unit(q_ref, kw_ref, bias_ref, e, u, lane)
                    dom = jnp.where(mine, do2, 0)
                    dp = _dot_nt(dom, vw)
                    delta = jnp.sum(p * dp, axis=-1, keepdims=True) * inv
                    ds = p * ((dp - delta) * inv)
                    dbias_ref[e] += ds
                    dsb = ds.astype(BF16)
                    dqs.append(_dot(dsb, kw))
                    dk_e = _dot_tn(qm, dsb)
                    dv_e = _dot_tn((dom * inv).astype(BF16), p.astype(BF16))
                    dk = dk_e if dk is None else dk + dk_e
                    dv = dv_e if dv is None else dv + dv_e
                dq_ref[rows, :] = (jnp.where(lane < ATT_HEAD_DIM, dqs[0], dqs[1]) * QSCALE).astype(dq_ref.dtype)
                dkw_ref[:, win] += dk
                dvw_ref[:, win] += dv

        @pl.when(b > 0)
        def _():
            dk_ref[...] = dkw_ref[:, 0:T].T.astype(dk_ref.dtype)
            dv_ref[...] = dvw_ref[:, 0:T].T.astype(dv_ref.dtype)

        @pl.when(b == nb)
        def _():
            for e in range(2):
                acc = dbias_ref[e, 0:CHUNK, :]
                for j in range(1, ATT_Q // CHUNK):
                    acc = acc + pltpu.roll(dbias_ref[e, j * CHUNK:(j + 1) * CHUNK, :], ATT_W - j * CHUNK, 1)
                dband_ref[e] = acc

    tok = jax.ShapeDtypeStruct((S, D_MODEL), BF16)
    prev_out = pl.BlockSpec((T, LANE), lambda h, b: (jnp.maximum(b - 1, 0), h))
    grid = (ATT_HEADS // 2, nb + 1)
    body, r_in, r_out, r_shapes, r_sems, aliases = _riding(rider, body, 7, 4, grid)
    outs = pl.pallas_call(
        body,
        name=name,
        grid=grid,
        in_specs=_attn_in_specs(nb) + [pl.BlockSpec((T, LANE), lambda h, b: (jnp.minimum(b, nb - 1), h))] + r_in,
        out_specs=[pl.BlockSpec((T, LANE), lambda h, b: (jnp.minimum(b, nb - 1), h)), prev_out, prev_out,
                   pl.BlockSpec((None, 2, CHUNK, ATT_W), lambda h, b: (h, 0, 0, 0))] + r_out,
        out_shape=[tok, tok, tok, jax.ShapeDtypeStruct((ATT_HEADS // 2, 2, CHUNK, ATT_W), F32)] + r_shapes,
        input_output_aliases=aliases,
        scratch_shapes=[pltpu.VMEM((2 * T, LANE), BF16), pltpu.VMEM((2 * T, LANE), BF16),
                        pltpu.VMEM((LANE, 2 * T), F32), pltpu.VMEM((LANE, 2 * T), F32),
                        pltpu.VMEM((T // ATT_Q, 2, ATT_Q, ATT_W), F32), pltpu.VMEM((2, ATT_Q, ATT_W), F32)] + r_sems,
        compiler_params=_cp(("parallel" if rider is None else "arbitrary", "arbitrary")),
    )(qkv, qkv, qkv, qkv, qkv, bias, do, *(rider.operands if rider is not None else ()))
    return tuple(outs[:4]) if rider is None else (tuple(outs[:4]), list(outs[4:]))


def _local_step(x, target, small, W):
    S = x.shape[0]
    tb = _ret_tables(S)
    rel_fwd, rel_bwd = _rel_index()
    saved = []
    normed = (("tile", BF16),)
    deep = dict(tm=512, tk=D_FF)
    h = W.hosted("mix_norm_fwd0", lambda rider: _rms_fwd("mix_norm_fwd0", x, small["mix_norm"][0:1], rider=rider))
    for layer in range(DEPTH):
        i = layer // 2
        st = {"x_in": x, "h": h}
        g_ffn = small["ffn_norm"][layer:layer + 1]
        if layer % 2 == 0:
            z = W.mm(f"ab_in_fwd{layer}", "nn", h, W.get("ab_w_in", i), tm=2048, tn=640, out_dtype=BF16)
            gain = small["ab_gn_gain"][i:i + 1]
            cat, opre, states = W.hosted(f"ret_fwd{layer}", lambda rider: _ret_fwd(f"ret_fwd{layer}", z, tb, gain, rider=rider))
            cat, pooled = _pool_fwd(f"pool_fwd{layer}", z, small["ab_w_pool"][i], small["ab_pool_scale"][i:i + 1], cat)
            st.update(z=z, opre=opre, states=states, pooled=pooled, cat=cat)
            x, hn = W.mm(f"ab_out_fwd{layer}", "nn", cat, W.get("ab_w_out", i), extras=(x,), aux=(g_ffn,), sides=normed,
                         epi=_epi_residual_norm)
        else:
            qkv = W.mm(f"qkv_fwd{layer}", "nn", h, W.get("c_w_qkv", i), tm=2048, out_dtype=BF16)
            bias = _bias_table(f"bias_table{layer}", small["c_rel_bias"][i], rel_fwd)
            bias = bias.reshape(ATT_HEADS // 2, 2, CHUNK, ATT_W)
            att = W.hosted(f"attn_fwd{layer}", lambda rider: _attn_fwd(f"attn_fwd{layer}", qkv, bias, rider=rider))
            st.update(qkv=qkv, bias=bias, att=att)
            x, hn = W.mm(f"c_out_fwd{layer}", "nn", att, W.get("c_w_out", i), extras=(x,), aux=(g_ffn,), sides=normed,
                         epi=_epi_residual_norm)
        st["x_mid"] = x
        u = W.mm(f"ffn_in_fwd{layer}", "nn", hn, W.get("w_ffn_in", layer), out_dtype=BF16, tm=2048)
        if layer + 1 < DEPTH:
            x, h = W.mm(f"ffn_out_fwd{layer}", "nn", u, W.get("w_ffn_out", layer), a_fn=_relu2, extras=(x,),
                        aux=(small["mix_norm"][layer + 1:layer + 2],), sides=normed, epi=_epi_residual_norm, **deep)
        else:
            x = W.mm(f"ffn_out_fwd{layer}", "nn", u, W.get("w_ffn_out", layer), a_fn=_relu2, extras=(x,),
                     epi=_epi_residual, **deep)
        st.update(hn=hn, u=u)
        saved.append(st)

    loss, dx, dxb, d_final = _loss_head(x, small["final_norm"].reshape(1, D_MODEL), target)

    gs = {k: [None] * v.shape[0] for k, v in small.items() if k != "final_norm"}
    gb = {k: None for k in W.n_layers}
    landed = {k: None for k in W.n_layers}
    pending = []

    def carry(call, take=1):
        items = [pending.pop(0) for _ in range(min(take, len(pending)))]
        if not items:
            return call(None)
        riders = [_grad_rider(key, idx, gb[key], landed[key]) for key, idx in items]
        res, outs = call(_join_riders(riders))
        for (key, _), out in zip(items, outs):
            landed[key] = out
        return res

    def host(name, *args, take=1, **kw):
        return carry(lambda rider: _mm(name, *args, rider=rider, **kw), take)

    def dw(name, key, idx, a, b, call=_mm, **kw):
        gb[key] = call(name, "tn", a, b, stack=(W.n_layers[key], idx, gb[key]), out_dtype=BF16, **kw)
        pending.append((key, idx))

    gain_sums = (("tile", BF16), ("colsum", F32))
    for layer in reversed(range(DEPTH)):
        i = layer // 2
        st = saved[layer]
        ffn_mm = host if layer % 2 == 0 else _mm
        du = ffn_mm(f"ffn_out_bwd{layer}", "nt", dxb, W.get("w_ffn_out", layer), extras=(st["u"],),
                    epi=lambda acc, u: acc * (2.0 * jnp.maximum(u, 0).astype(F32)), out_dtype=BF16, tm=2048)
        dw(f"ffn_out_dw{layer}", "w_ffn_out", layer, st["u"], dxb, a_fn=_relu2, tk=2048)
        dx, dxb, dgain = ffn_mm(f"ffn_in_bwd{layer}", "nt", du, W.get("w_ffn_in", layer), extras=(st["x_mid"], dx),
                         aux=(small["ffn_norm"][layer:layer + 1],), sides=gain_sums, epi=_epi_rms_bwd, **deep)
        gs["ffn_norm"][layer] = dgain[0:1]
        dw(f"ffn_in_dw{layer}", "w_ffn_in", layer, st["hn"], du, tk=2048)
        norm_bwd = dict(extras=(st["x_in"], dx), aux=(small["mix_norm"][layer:layer + 1],), sides=gain_sums,
                        epi=_epi_rms_bwd)
        if layer % 2 == 0:
            dcat = _mm(f"ab_out_bwd{layer}", "nt", dxb, W.get("ab_w_out", i), out_dtype=BF16)
            dw(f"ab_out_dw{layer}", "ab_w_out", i, st["cat"], dxb, tk=2048)
            gain = small["ab_gn_gain"][i:i + 1]
            dz, gs["ab_gn_gain"][i] = carry(lambda rider: _ret_bwd(f"ret_bwd{layer}", st["z"], tb, gain, st["opre"],
                                                                   st["states"], dcat, rider=rider), take=len(pending))
            dz, gs["ab_w_pool"][i], gs["ab_pool_scale"][i] = _pool_bwd(
                f"pool_bwd{layer}", st["pooled"], small["ab_w_pool"][i], small["ab_pool_scale"][i:i + 1], dcat, dz)
            if layer == 0:
                dw(f"ab_in_dw{layer}", "ab_w_in", i, st["h"], dz, call=host, tn=640, tk=2048)
            dx, dxb, dgain = host(f"ab_in_bwd{layer}", "nt", dz, W.get("ab_w_in", i), tm=512, tk=AB_IN_WIDTH,
                             take=len(pending) if layer == 0 else 1, **norm_bwd)
            if layer > 0:
                dw(f"ab_in_dw{layer}", "ab_w_in", i, st["h"], dz, call=host, tn=640, tk=2048)
        else:
            datt = _mm(f"c_out_bwd{layer}", "nt", dxb, W.get("c_w_out", i), out_dtype=BF16)
            dw(f"c_out_dw{layer}", "c_w_out", i, st["att"], dxb, tk=2048)
            dq, dk, dv, dbias = carry(lambda rider: _attn_bwd(f"attn_bwd{layer}", st["qkv"], st["bias"], datt, rider=rider),
                                      take=len(pending))
            gs["c_rel_bias"][i] = _bias_grad(f"bias_grad{layer}", dbias.reshape(ATT_HEADS, CHUNK, ATT_W), rel_bwd)
            dqkv = [dq, dk, dv]
            dx, dxb, dgain = host(f"qkv_bwd{layer}", "nt", dqkv, W.get("c_w_qkv", i), tm=512, tk=3 * D_MODEL, **norm_bwd)
            dw(f"qkv_dw{layer}", "c_w_qkv", i, st["h"], dqkv, call=host, tk=2048)
        gs["mix_norm"][layer] = dgain[0:1]
    for key, idx in pending:
        landed[key], = _run_rider(f"grad_exchange_{key}{idx}", _grad_rider(key, idx, gb[key], landed[key]))

    g_small = {
        "mix_norm": jnp.concatenate(gs["mix_norm"], axis=0),
        "ffn_norm": jnp.concatenate(gs["ffn_norm"], axis=0),
        "ab_gn_gain": jnp.concatenate(gs["ab_gn_gain"], axis=0),
        "ab_w_pool": jnp.stack(gs["ab_w_pool"], axis=0),
        "ab_pool_scale": jnp.concatenate(gs["ab_pool_scale"], axis=0),
        "c_rel_bias": jnp.stack(gs["c_rel_bias"], axis=0),
        "final_norm": d_final.reshape(D_MODEL),
    }
    return loss, dx, g_small, gb, landed


_BIG = ("w_ffn_in", "w_ffn_out", "ab_w_in", "ab_w_out", "c_w_qkv", "c_w_out")
_SHARD_AXIS = {"w_ffn_in": 2, "w_ffn_out": 1, "ab_w_in": 2, "ab_w_out": 1, "c_w_qkv": 2, "c_w_out": 1}
_SMALL = ("mix_norm", "ffn_norm", "ab_gn_gain", "ab_w_pool", "ab_pool_scale", "c_rel_bias", "final_norm")


def _place():
    x, y, c = lax.axis_index("x"), lax.axis_index("y"), lax.axis_index("c")
    chips = [(1 - x, y), (x, 1 - y), (1 - x, 1 - y)]
    return x, y, c, chips


def _sub(ref, axis, start, size):
    idx = [slice(None)] * len(ref.shape)
    idx[axis] = pl.ds(pl.multiple_of(start, LANE), size)
    return ref.at[tuple(idx)]


def _gather_rider(items, shards):
    keys = sorted({k for k, _ in items})
    n = len(items)
    axes = [_SHARD_AXIS[k] - 1 for k, _ in items]
    sizes = [shards[k].shape[a + 1] for (k, _), a in zip(items, axes)]
    hsizes = [shards[k].shape[2 - a] // 2 for (k, _), a in zip(items, axes)]

    def views(ins, outs, send_sems, recv_sems):
        x, y, c, chips = _place()
        srcs = [ins[keys.index(k)].at[l] for k, l in items]

        def remote(src, dst, s, to):
            return pltpu.make_async_remote_copy(src_ref=src, dst_ref=dst, send_sem=send_sems.at[s],
                                                recv_sem=recv_sems.at[s], device_id=to, device_id_type=MESH)

        def half(w, chip, core):
            return _sub(_sub(outs[w], axes[w], chip * sizes[w], sizes[w]), 1 - axes[w], core * hsizes[w], hsizes[w])

        me = 2 * x + y
        local = [pltpu.make_async_copy(srcs[w], _sub(outs[w], axes[w], me * sizes[w], sizes[w]), send_sems.at[6 * n + w])
                 for w in range(n)]
        first = [remote(_sub(srcs[w], 1 - axes[w], c * hsizes[w], hsizes[w]), half(w, me, c), w * 6 + k, (px, py, c))
                 for w in range(n) for k, (px, py) in enumerate(chips)]
        return x, y, c, chips, remote, half, local, first

    def start(ins, outs, send_sems, recv_sems):
        *_, local, first = views(ins, outs, send_sems, recv_sems)
        for cp in local + first:
            cp.start()

    def passes(x, y, c, chips, remote, half):
        return [remote(half(w, 2 * px + py, c), half(w, 2 * px + py, c), w * 6 + 3 + k, (x, y, 1 - c))
                for w in range(n) for k, (px, py) in enumerate(chips)]

    def relay(ins, outs, send_sems, recv_sems):
        x, y, c, chips, remote, half, _, _ = views(ins, outs, send_sems, recv_sems)
        for w in range(n):
            for k, (px, py) in enumerate(chips):
                landed = half(w, 2 * px + py, c)
                remote(landed, landed, w * 6 + k, (px, py, c)).wait_recv()
        for cp in passes(x, y, c, chips, remote, half):
            cp.start()

    def finish(ins, outs, send_sems, recv_sems):
        x, y, c, chips, remote, half, local, first = views(ins, outs, send_sems, recv_sems)
        for w in range(n):
            for k, (px, py) in enumerate(chips):
                theirs = half(w, 2 * px + py, 1 - c)
                remote(theirs, theirs, w * 6 + 3 + k, (x, y, 1 - c)).wait_recv()
        for cp in first + passes(x, y, c, chips, remote, half):
            cp.wait_send()
        for cp in local:
            cp.wait()

    def full(k, a):
        shape = list(shards[k].shape[1:])
        shape[a] *= N_CHIPS
        return jax.ShapeDtypeStruct(tuple(shape), shards[k].dtype)

    return _Rider(tuple(shards[k] for k in keys), tuple(full(k, a) for (k, _), a in zip(items, axes)), 7 * n, start, finish,
                  relay=relay)


def _mixer_items(layer):
    names = ("ab_w_in", "ab_w_out") if layer % 2 == 0 else ("c_w_qkv", "c_w_out")
    return [(k, layer // 2) for k in names]


class _Weights:
    def __init__(self, shards):
        self.shards = shards
        self.n_layers = {k: shards[k].shape[0] for k in _BIG}
        self.full = {}
        first, second = _mixer_items(0)
        self.plan = {"mix_norm_fwd0": [first], "ab_in_fwd0": [second, ("w_ffn_in", 0)], "ret_fwd0": [("w_ffn_out", 0)]}
        for layer in range(1, DEPTH):
            if layer % 2 == 0:
                self.plan[f"attn_fwd{layer - 1}"] = _mixer_items(layer) + [("w_ffn_in", layer), ("w_ffn_out", layer)]
            else:
                self.plan[f"ffn_in_fwd{layer - 1}"] = _mixer_items(layer)
                self.plan[f"ffn_out_fwd{layer - 1}"] = [("w_ffn_in", layer)]
                self.plan[f"qkv_fwd{layer}"] = [("w_ffn_out", layer)]

    def _take(self, items, outs):
        self.full.update(zip(items, outs))

    def get(self, name, layer):
        return self.full[(name, layer)]

    def hosted(self, name, call):
        items = self.plan.get(name)
        if items is None:
            return call(None)
        res, outs = call(_gather_rider(items, self.shards))
        self._take(items, outs)
        return res

    def mm(self, name, *args, **kw):
        return self.hosted(name, lambda rider: _mm(name, *args, rider=rider, **kw))


def _run_rider(name, rider):
    n_in, n_out = len(rider.operands), len(rider.out_shapes)

    def body(*refs):
        ins, outs, sems = refs[:n_in], refs[n_in:n_in + n_out], refs[n_in + n_out:]
        rider.start(ins, outs, *sems)
        if rider.relay is not None:
            rider.relay(ins, outs, *sems)
        rider.finish(ins, outs, *sems)

    return pl.pallas_call(
        body,
        name=name,
        in_specs=[_ANY] * n_in,
        out_specs=[_ANY] * n_out,
        out_shape=list(rider.out_shapes),
        input_output_aliases=dict(rider.aliases),
        scratch_shapes=[pltpu.SemaphoreType.DMA((rider.n_sems,)), pltpu.SemaphoreType.DMA((rider.n_sems,))],
        compiler_params=pltpu.CompilerParams(has_side_effects=True),
    )(*rider.operands)


def _grad_rider(name, layer, grad, landing):
    axis = _SHARD_AXIS[name] - 1
    L, R, C = grad.shape
    shard = (R // N_CHIPS, C) if axis == 0 else (R, C // N_CHIPS)
    size = shard[axis]

    def copies(ins, outs, send_sems, recv_sems):
        x, y, c, chips = _place()
        return [pltpu.make_async_remote_copy(
            src_ref=_sub(ins[0].at[layer], axis, (2 * px + py) * size, size), dst_ref=outs[0].at[layer, k],
            send_sem=send_sems.at[k], recv_sem=recv_sems.at[k], device_id=(px, py, c), device_id_type=MESH)
            for k, (px, py) in enumerate(chips)]

    def start(ins, outs, send_sems, recv_sems):
        for cp in copies(ins, outs, send_sems, recv_sems):
            cp.start()

    def finish(ins, outs, send_sems, recv_sems):
        cps = copies(ins, outs, send_sems, recv_sems)
        for cp in cps:
            cp.wait_recv()
        for cp in cps:
            cp.wait_send()

    out = jax.ShapeDtypeStruct((L, 3) + shard, grad.dtype)
    if landing is None:
        return _Rider((grad,), (out,), 3, start, finish)
    return _Rider((grad, landing), (out,), 3, start, finish, aliases=((1, 0),))


def _join_riders(riders):
    if len(riders) == 1:
        return riders[0]

    def parts(ins, outs, send_sems, recv_sems):
        i0 = o0 = s0 = 0
        for r in riders:
            ni, no = len(r.operands), len(r.out_shapes)
            yield (r, ins[i0:i0 + ni], outs[o0:o0 + no], send_sems.at[pl.ds(s0, r.n_sems)],
                   recv_sems.at[pl.ds(s0, r.n_sems)])
            i0, o0, s0 = i0 + ni, o0 + no, s0 + r.n_sems

    def start(*refs):
        for r, *own in parts(*refs):
            r.start(*own)

    def finish(*refs):
        for r, *own in parts(*refs):
            r.finish(*own)

    aliases, i0, o0 = [], 0, 0
    for r in riders:
        aliases += [(i0 + src, o0 + dst) for src, dst in r.aliases]
        i0, o0 = i0 + len(r.operands), o0 + len(r.out_shapes)
    return _Rider(tuple(x for r in riders for x in r.operands), tuple(x for r in riders for x in r.out_shapes),
                  sum(r.n_sems for r in riders), start, finish, tuple(aliases))


def _rows_tile(rows, cols):
    tr = rows
    while tr * cols > (1 << 19) and tr % 16 == 0:
        tr //= 2
    return tr


def _chip_sum(name, grad, landed, chip, saxis):
    L = grad.shape[0]
    _, _, R, C = landed.shape
    tr = _rows_tile(R, C)
    nr = R // tr
    if saxis == 2:
        g_idx = lambda l, i, s: (l, i, s[0])
    else:
        g_idx = lambda l, i, s: (l, s[0] * nr + i, 0)

    def body(s_ref, g_ref, l_ref, o_ref):
        tot = ((g_ref[...].astype(F32) + l_ref[0].astype(F32)) + l_ref[1].astype(F32)) + l_ref[2].astype(F32)
        o_ref[...] = tot.astype(o_ref.dtype)

    return pl.pallas_call(
        body,
        name=name,
        grid_spec=pltpu.PrefetchScalarGridSpec(
            num_scalar_prefetch=1,
            grid=(L, nr),
            in_specs=[pl.BlockSpec((None, tr, C), g_idx), pl.BlockSpec((None, 3, tr, C), lambda l, i, s: (l, 0, i, 0))],
            out_specs=pl.BlockSpec((None, tr, C), lambda l, i, s: (l, i, 0)),
        ),
        out_shape=jax.ShapeDtypeStruct((L, R, C), BF16),
        compiler_params=_cp(("parallel", "parallel")),
    )(chip, grad, landed)


def _all_reduce_small(packed, sums):
    R = packed.shape[0]
    n = len(sums)

    def body(p_ref, *refs):
        s_in, o_ref, s_out = refs[:n], refs[n], refs[n + 1:2 * n + 1]
        land_ref, send_sems, recv_sems = refs[2 * n + 1:]
        x, y, c, _ = _place()
        me = 4 * x + 2 * y + c
        swaps = [pltpu.make_async_remote_copy(src_ref=s_in[w], dst_ref=s_out[w], send_sem=send_sems.at[N_DEV - 1 + w],
                                              recv_sem=recv_sems.at[N_DEV - 1 + w], device_id=(x, y, 1 - c),
                                              device_id_type=MESH) for w in range(n)]
        for cp in swaps:
            cp.start()
        sends, recvs = [], []
        for r in range(1, N_DEV):
            px, py, pc = x ^ (r >> 2), y ^ ((r >> 1) & 1), c ^ (r & 1)
            cp = pltpu.make_async_remote_copy(src_ref=p_ref, dst_ref=land_ref.at[me], send_sem=send_sems.at[r - 1],
                                              recv_sem=recv_sems.at[r - 1], device_id=(px, py, pc), device_id_type=MESH)
            cp.start()
            sends.append(cp)
            recvs.append(pltpu.make_async_remote_copy(src_ref=p_ref, dst_ref=land_ref.at[4 * px + 2 * py + pc],
                                                      send_sem=send_sems.at[r - 1], recv_sem=recv_sems.at[r - 1],
                                                      device_id=(px, py, pc), device_id_type=MESH))
        land_ref[me] = p_ref[...]
        for cp in recvs:
            cp.wait_recv()
        for cp in sends:
            cp.wait_send()
        acc = land_ref[0]
        for d in range(1, N_DEV):
            acc = acc + land_ref[d]
        o_ref[...] = acc
        for cp in swaps:
            cp.wait_recv()
        for cp in swaps:
            cp.wait_send()

    vm = pl.BlockSpec(memory_space=pltpu.VMEM)
    outs = pl.pallas_call(
        body,
        name="all_reduce_small",
        in_specs=[vm] + [_ANY] * n,
        out_specs=[vm] + [_ANY] * n,
        out_shape=[jax.ShapeDtypeStruct((R, LANE), F32)] + [jax.ShapeDtypeStruct(s.shape, s.dtype) for s in sums],
        scratch_shapes=[pltpu.VMEM((N_DEV, R, LANE), F32), pltpu.SemaphoreType.DMA((N_DEV - 1 + n,)),
                        pltpu.SemaphoreType.DMA((N_DEV - 1 + n,))],
        compiler_params=pltpu.CompilerParams(has_side_effects=True, vmem_limit_bytes=VMEM_LIMIT),
    )(packed, *sums)
    return outs[0], list(outs[1:])


def _adamw(name, w, m, v, grads):
    R, C = w.shape
    tr = _rows_tile(R, C)
    c1 = 1.0 - ADAM_B1 ** ADAM_STEP
    c2 = 1.0 - ADAM_B2 ** ADAM_STEP
    ng = len(grads)

    def body(*refs):
        w_ref, m_ref, v_ref = refs[:3]
        g_refs = refs[3:3 + ng]
        g_ref, d_ref, nm_ref, nv_ref = refs[3 + ng:]
        gv = g_refs[0][...].astype(F32)
        for r in g_refs[1:]:
            gv = gv + r[...].astype(F32)
        g_ref[...] = gv
        nm = ADAM_B1 * m_ref[...] + (1.0 - ADAM_B1) * gv
        nv = ADAM_B2 * v_ref[...] + (1.0 - ADAM_B2) * (gv * gv)
        nm_ref[...] = nm
        nv_ref[...] = nv
        d_ref[...] = -ADAM_LR * ((nm / c1) / (jnp.sqrt(nv / c2) + ADAM_EPS) + ADAM_WD * w_ref[...])

    blk = pl.BlockSpec((tr, C), lambda i: (i, 0))
    out = jax.ShapeDtypeStruct((R, C), F32)
    return pl.pallas_call(
        body,
        name=name,
        grid=(R // tr,),
        in_specs=[blk] * (3 + ng),
        out_specs=[blk] * 4,
        out_shape=[out] * 4,
        compiler_params=_cp(("parallel",)),
    )(w, m, v, *grads)


def _pack(parts):
    rows = []
    for p in parts:
        flat = p.reshape(-1).astype(F32)
        n = -(-flat.shape[0] // (8 * LANE)) * (8 * LANE)
        rows.append(jnp.pad(flat, (0, n - flat.shape[0])).reshape(n // LANE, LANE))
    return jnp.concatenate(rows, axis=0)


def _unpack(packed, like):
    out, r = [], 0
    for p in like:
        size = int(np.prod(p.shape))
        n = -(-size // (8 * LANE)) * 8
        out.append(packed[r:r + n].reshape(-1)[:size].reshape(p.shape))
        r += n
    return out


def kernel(x, mix_norm, ffn_norm, w_ffn_in, w_ffn_out, ab_w_in, ab_gn_gain, ab_w_pool, ab_pool_scale, ab_w_out, c_w_qkv, c_rel_bias, c_w_out, final_norm, loss_target, m_mix_norm, m_ffn_norm, m_w_ffn_in, m_w_ffn_out, m_ab_w_in, m_ab_gn_gain, m_ab_w_pool, m_ab_pool_scale, m_ab_w_out, m_c_w_qkv, m_c_rel_bias, m_c_w_out, m_final_norm, v_mix_norm, v_ffn_norm, v_w_ffn_in, v_w_ffn_out, v_ab_w_in, v_ab_gn_gain, v_ab_w_pool, v_ab_pool_scale, v_ab_w_out, v_c_w_qkv, v_c_rel_bias, v_c_w_out, v_final_norm):
    w = dict(mix_norm=mix_norm, ffn_norm=ffn_norm, w_ffn_in=w_ffn_in, w_ffn_out=w_ffn_out, ab_w_in=ab_w_in,
             ab_gn_gain=ab_gn_gain, ab_w_pool=ab_w_pool, ab_pool_scale=ab_pool_scale, ab_w_out=ab_w_out,
             c_w_qkv=c_w_qkv, c_rel_bias=c_rel_bias, c_w_out=c_w_out, final_norm=final_norm)
    m = dict(mix_norm=m_mix_norm, ffn_norm=m_ffn_norm, w_ffn_in=m_w_ffn_in, w_ffn_out=m_w_ffn_out, ab_w_in=m_ab_w_in,
             ab_gn_gain=m_ab_gn_gain, ab_w_pool=m_ab_w_pool, ab_pool_scale=m_ab_pool_scale, ab_w_out=m_ab_w_out,
             c_w_qkv=m_c_w_qkv, c_rel_bias=m_c_rel_bias, c_w_out=m_c_w_out, final_norm=m_final_norm)
    v = dict(mix_norm=v_mix_norm, ffn_norm=v_ffn_norm, w_ffn_in=v_w_ffn_in, w_ffn_out=v_w_ffn_out, ab_w_in=v_ab_w_in,
             ab_gn_gain=v_ab_gn_gain, ab_w_pool=v_ab_w_pool, ab_pool_scale=v_ab_pool_scale, ab_w_out=v_ab_w_out,
             c_w_qkv=v_c_w_qkv, c_rel_bias=v_c_rel_bias, c_w_out=v_c_w_out, final_norm=v_final_norm)
    S = x.shape[1]
    cx, cy, cc = lax.axis_index("x"), lax.axis_index("y"), lax.axis_index("c")
    chip = jnp.reshape(2 * cx + cy, (1,)).astype(jnp.int32)

    big = _Weights({k: w[k].astype(BF16) for k in _BIG})
    small = {k: w[k] for k in _SMALL}
    loss, grad_x, g_small, g_big, landed = _local_step(x.reshape(S, D_MODEL), loss_target.reshape(S, D_MODEL), small, big)

    sums = [_chip_sum(f"chip_sum_{k}", g_big[k], landed[k], chip, _SHARD_AXIS[k]) for k in _BIG]

    packed, siblings = _all_reduce_small(_pack([g_small[k] for k in _SMALL] + [loss]), sums)
    small_like = [w[k] for k in _SMALL]
    g_red = dict(zip(_SMALL, _unpack(packed, small_like)))
    loss_row = packed.shape[0] - 8
    loss_out = packed[loss_row, 0]

    grad, delta, new_m, new_v = {}, {}, {}, {}
    for k, mine, theirs in zip(_BIG, sums, siblings):
        shp = w[k].shape
        two = (shp[0] * shp[1], shp[2])
        outs = _adamw(f"adamw_{k}", w[k].reshape(two), m[k].reshape(two), v[k].reshape(two),
                      (mine.reshape(two), theirs.reshape(two)))
        grad[k], delta[k], new_m[k], new_v[k] = [o.reshape(shp) for o in outs]
    _, d, nm, nv = _adamw("adamw_small", _pack(small_like), _pack([m[k] for k in _SMALL]), _pack([v[k] for k in _SMALL]),
                          (packed[:loss_row],))
    for k, dk, mk, vk in zip(_SMALL, _unpack(d, small_like), _unpack(nm, small_like), _unpack(nv, small_like)):
        grad[k], delta[k], new_m[k], new_v[k] = g_red[k], dk, mk, vk

    order = ("mix_norm", "ffn_norm", "w_ffn_in", "w_ffn_out", "ab_w_in", "ab_gn_gain", "ab_w_pool", "ab_pool_scale",
             "ab_w_out", "c_w_qkv", "c_rel_bias", "c_w_out", "final_norm")
    return (loss_out, grad_x.reshape(x.shape), *[grad[k] for k in order], *[delta[k] for k in order],
            *[new_m[k] for k in order], *[new_v[k] for k in order])
```

```python
import functools
from typing import Callable, NamedTuple

import numpy as np
import jax
import jax.numpy as jnp
from jax import lax
from jax.experimental import pallas as pl
from jax.experimental.pallas import tpu as pltpu

F32 = jnp.float32
BF16 = jnp.bfloat16

D_MODEL = 1024
D_FF = 4096
DEPTH = 4
CHUNK = 64
RMS_EPS = 1e-6
RET_WIDTH = 512
RET_HEADS = 4
RET_HEAD_DIM = 128
RET_ROPE_BASE = 10000.0
GN_EPS = 1e-5
POOL_WIDTH = 512
POOL_WINDOWS = (2, 4, 8, 16)
POOL_GROUP_DIM = 128
POOL_HALO = 16
AB_IN_WIDTH = 2560
ATT_HEADS = 16
ATT_HEAD_DIM = 64
LEFT_CHUNKS = 8
BAND = (LEFT_CHUNKS + 1) * CHUNK
REL_CLIP = 128
N_REL = 2 * REL_CLIP + 1
N_REL_PAD = 264
NEG_INF = -1e30
KSCALE = RET_HEAD_DIM ** -0.5
QSCALE = ATT_HEAD_DIM ** -0.5

ADAM_LR = 0.001
ADAM_B1 = 0.9
ADAM_B2 = 0.999
ADAM_EPS = 1e-08
ADAM_WD = 0.01
ADAM_STEP = 10

ATT_BLOCK = LEFT_CHUNKS * CHUNK
RET_BLOCK = 256
N_CHIPS = 4
N_DEV = 8
LANE = 128
VMEM_LIMIT = 52 * 1024 * 1024
EPI_ROWS = 256
MESH = pl.DeviceIdType.MESH


def _cp(sem, vmem=VMEM_LIMIT):
    return pltpu.CompilerParams(dimension_semantics=sem, vmem_limit_bytes=vmem)


def _dot(a, b):
    return lax.dot_general(a, b, (((1,), (0,)), ((), ())), preferred_element_type=F32)


def _dot_nt(a, b):
    return lax.dot_general(a, b, (((1,), (1,)), ((), ())), preferred_element_type=F32)


def _dot_tn(a, b):
    return lax.dot_general(a, b, (((0,), (0,)), ((), ())), preferred_element_type=F32)


_ANY = pl.BlockSpec(memory_space=pl.ANY)


class _Rider(NamedTuple):
    operands: tuple
    out_shapes: tuple
    n_sems: int
    start: Callable
    finish: Callable
    aliases: tuple = ()
    relay: Callable = None


def _mm(name, mode, a, b, *, la=None, lb=None, tm=1024, tn=1024, tk=1024, a_fn=None, b_fn=None,
        extras=(), aux=(), sides=(), epi=None, out_dtype=F32, stack=None, rider=None):
    a_parts = list(a) if isinstance(a, (list, tuple)) else [a]
    b_parts = list(b) if isinstance(b, (list, tuple)) else [b]
    na, nbp = len(a_parts), len(b_parts)
    a2, b2 = list(a_parts[0].shape[-2:]), list(b_parts[0].shape[-2:])
    a2[1] *= na
    b2[1] *= nbp
    if mode == "nn":
        (M, K), (K2, N) = a2, b2
    elif mode == "nt":
        (M, K), (N, K2) = a2, b2
    else:
        (K, M), (K2, N) = a2, b2
    assert K == K2, (name, a2, b2)
    tm, tn, tk = min(tm, M), min(tn, N), min(tk, K)
    assert M % tm == 0 and N % tn == 0 and K % tk == 0, (name, M, N, K, tm, tn, tk)
    gm, gn, gk = M // tm, N // tn, K // tk
    fold = mode == "nt" and na > 1 and gk == 1

    def specs(parts, block, idx, lead):
        per = parts[0].shape[-1] // block[1]
        assert parts[0].shape[-1] % block[1] == 0, (name, parts[0].shape, block)
        out = []
        for p in range(len(parts)):
            def f(i, j, k, p=p):
                r, c = idx(i, j, k)
                if len(parts) > 1:
                    c = jnp.clip(c - p * per, 0, per - 1)
                return (r, c) if lead is None else (lead, r, c)
            out.append(pl.BlockSpec(block if lead is None else (None,) + block, f))
        return out, per

    if mode == "nn":
        a_specs, a_per = specs(a_parts, (tm, tk), lambda i, j, k: (i, k), la)
        b_specs, b_per = specs(b_parts, (tk, tn), lambda i, j, k: (k, j), lb)
        a_axis, b_axis, dot = 2, 1, _dot
    elif mode == "nt":
        if fold:
            a_specs, a_per = [pl.BlockSpec((tm, K // na), lambda i, j, k: (i, 0)) for _ in a_parts], 1
        else:
            a_specs, a_per = specs(a_parts, (tm, tk), lambda i, j, k: (i, k), la)
        b_specs, b_per = specs(b_parts, (tn, tk), lambda i, j, k: (j, k), lb)
        a_axis, b_axis, dot = 2, 2, _dot_nt
    else:
        a_specs, a_per = specs(a_parts, (tk, tm), lambda i, j, k: (k, i), la)
        b_specs, b_per = specs(b_parts, (tk, tn), lambda i, j, k: (k, j), lb)
        a_axis, b_axis, dot = 0, 1, _dot_tn
    ex_specs = [pl.BlockSpec((tm, tn), lambda i, j, k: (i, j)) for _ in extras]
    n_ex = len(extras)

    n_aux, n_side = len(aux), len(sides)
    operands = a_parts + b_parts + list(extras) + list(aux)
    in_specs = a_specs + b_specs + ex_specs + [pl.BlockSpec(v.shape, lambda i, j, k, nd=v.ndim: (0,) * nd) for v in aux]
    aliases = {}
    if stack is None:
        out_specs = [pl.BlockSpec((tm, tn), lambda i, j, k: (i, j))]
        out_shapes = [jax.ShapeDtypeStruct((M, N), out_dtype)]
    else:
        n_layers, layer, prev = stack
        out_specs = [pl.BlockSpec((None, tm, tn), lambda i, j, k: (layer, i, j))]
        out_shapes = [jax.ShapeDtypeStruct((n_layers, M, N), out_dtype)]
        if prev is not None:
            aliases = {len(operands): 0}
            operands.append(prev)
            in_specs.append(_ANY)
    for kind, dtype in sides:
        if kind == "tile":
            out_specs.append(pl.BlockSpec((tm, tn), lambda i, j, k: (i, j)))
            out_shapes.append(jax.ShapeDtypeStruct((M, N), dtype))
        else:
            assert gn == 1, name
            out_specs.append(pl.BlockSpec((8, tn), lambda i, j, k: (0, 0)))
            out_shapes.append(jax.ShapeDtypeStruct((8, N), dtype))
    n_prev = len(aliases)
    scratch = [pltpu.VMEM((tm, tn), F32)] if gk > 1 else []
    n_rin = n_rout = 0
    if rider is not None:
        n_rin, n_rout = len(rider.operands), len(rider.out_shapes)
        for src, dst in rider.aliases:
            aliases[len(operands) + src] = 1 + n_side + dst
        operands += list(rider.operands)
        in_specs += [_ANY] * n_rin
        out_specs += [_ANY] * n_rout
        out_shapes += list(rider.out_shapes)
        scratch += [pltpu.SemaphoreType.DMA((rider.n_sems,)), pltpu.SemaphoreType.DMA((rider.n_sems,))]
    assert na == 1 or nbp == 1, name

    def body(*refs):
        a_refs, b_refs = refs[:na], refs[na:na + nbp]
        ex_refs = refs[na + nbp:na + nbp + n_ex + n_aux]
        n_in = na + nbp + n_ex + n_aux + n_prev
        rin = refs[n_in:n_in + n_rin]
        o_ref = refs[n_in + n_rin]
        side_refs = refs[n_in + n_rin + 1:n_in + n_rin + 1 + n_side]
        rout = refs[n_in + n_rin + 1 + n_side:n_in + n_rin + 1 + n_side + n_rout]
        rest = refs[n_in + n_rin + 1 + n_side + n_rout:]
        i, j, k = pl.program_id(0), pl.program_id(1), pl.program_id(2)
        if rider is not None:
            sems = rest[-2:]

            @pl.when(jnp.logical_and(i == 0, jnp.logical_and(j == 0, k == 0)))
            def _():
                rider.start(rin, rout, *sems)

        def finish(acc):
            if epi is None:
                o_ref[...] = acc[...].astype(o_ref.dtype)
                return
            strip = min(tm, EPI_ROWS)
            colsums = [None] * n_side
            for r0 in range(0, tm, strip):
                rows = slice(r0, r0 + strip)
                res = epi(acc[rows, :], *[r[rows, :] for r in ex_refs[:n_ex]], *[r[...] for r in ex_refs[n_ex:]])
                if n_side:
                    res, *side_vals = res
                    for s, ((kind, _), ref, val) in enumerate(zip(sides, side_refs, side_vals)):
                        if kind == "tile":
                            ref[rows, :] = val.astype(ref.dtype)
                        else:
                            colsums[s] = val if colsums[s] is None else colsums[s] + val
                o_ref[rows, :] = res.astype(o_ref.dtype)
            for (kind, _), ref, val in zip(sides, side_refs, colsums):
                if kind == "colsum":
                    @pl.when(i == 0)
                    def _(ref=ref, val=val):
                        ref[...] = val

                    @pl.when(i > 0)
                    def _(ref=ref, val=val):
                        ref[...] += val

                    @pl.when(i == gm - 1)
                    def _(ref=ref):
                        ref[0:1, :] = jnp.sum(ref[...], axis=0, keepdims=True)

        def step(a_ref, b_ref):
            av, bv = a_ref[...], b_ref[...]
            if a_fn is not None:
                av = a_fn(av)
            if b_fn is not None:
                bv = b_fn(bv)
            part = dot(av.astype(BF16), bv.astype(BF16))
            if gk == 1:
                finish(part)
                return
            acc_ref = rest[0]

            @pl.when(k == 0)
            def _():
                acc_ref[...] = part

            @pl.when(k > 0)
            def _():
                acc_ref[...] += part

        if fold:
            kp = K // na
            finish(sum(dot(a_refs[p][...].astype(BF16), b_refs[0][:, p * kp:(p + 1) * kp].astype(BF16))
                       for p in range(na)))
        elif na > 1:
            sel = pl.program_id(a_axis) // a_per
            for p in range(na):
                pl.when(sel == p)(functools.partial(step, a_refs[p], b_refs[0]))
        elif nbp > 1:
            sel = pl.program_id(b_axis) // b_per
            for p in range(nbp):
                pl.when(sel == p)(functools.partial(step, a_refs[0], b_refs[p]))
        else:
            step(a_refs[0], b_refs[0])
        if gk > 1:
            @pl.when(k == gk - 1)
            def _():
                finish(rest[0])

        if rider is not None:
            steps = gm * gn * gk
            step_no = (i * gn + j) * gk + k
            if rider.relay is not None:
                assert steps >= 3, name

                @pl.when(step_no == steps - 2)
                def _():
                    rider.relay(rin, rout, *sems)

            @pl.when(step_no == steps - 1)
            def _():
                rider.finish(rin, rout, *sems)

    sequential = rider is not None or any(kind == "colsum" for kind, _ in sides)
    sem = ("arbitrary",) * 3 if sequential else ("parallel", "parallel", "arbitrary")
    outs = pl.pallas_call(
        body,
        name=name,
        grid=(gm, gn, gk),
        in_specs=in_specs,
        out_specs=out_specs,
        out_shape=out_shapes,
        input_output_aliases=aliases,
        scratch_shapes=scratch,
        compiler_params=_cp(sem),
    )(*operands)
    res = outs[0] if not sides else tuple(outs[:1 + n_side])
    return res if rider is None else (res, list(outs[1 + n_side:]))


def _relu2(u):
    r = jnp.maximum(u, 0)
    return r * r


def _epi_residual(acc, res):
    return acc + res


def _epi_residual_norm(acc, res, g):
    xn = acc + res
    r = lax.rsqrt(jnp.mean(xn * xn, axis=-1, keepdims=True) + RMS_EPS)
    return xn, (xn * r) * g


def _epi_rms_bwd(dh, x, dres, g):
    r = lax.rsqrt(jnp.mean(x * x, axis=-1, keepdims=True) + RMS_EPS)
    xh = x * r
    dxh = dh * g
    dx = dres + r * (dxh - xh * jnp.mean(dxh * xh, axis=-1, keepdims=True))
    return dx, dx, jnp.sum((dh * xh).reshape(dh.shape[0] // 8, 8, dh.shape[1]), axis=0)


def _rms_fwd(name, x, g, rider=None):
    S, D = x.shape
    tq = min(1024, S)

    def body(x_ref, g_ref, o_ref):
        xv = x_ref[...]
        r = lax.rsqrt(jnp.mean(xv * xv, axis=-1, keepdims=True) + RMS_EPS)
        o_ref[...] = ((xv * r) * g_ref[...]).astype(o_ref.dtype)

    grid = (S // tq,)
    body, r_in, r_out, r_shapes, r_sems, aliases = _riding(rider, body, 2, 1, grid)
    outs = pl.pallas_call(
        body,
        name=name,
        grid=grid,
        in_specs=[pl.BlockSpec((tq, D), lambda i: (i, 0)), pl.BlockSpec((1, D), lambda i: (0, 0))] + r_in,
        out_specs=[pl.BlockSpec((tq, D), lambda i: (i, 0))] + r_out,
        out_shape=[jax.ShapeDtypeStruct((S, D), BF16)] + r_shapes,
        input_output_aliases=aliases,
        scratch_shapes=r_sems,
        compiler_params=_cp(("parallel" if rider is None else "arbitrary",)),
    )(x, g, *(rider.operands if rider is not None else ()))
    return outs[0] if rider is None else (outs[0], list(outs[1:]))


def _loss_head(x, g, t):
    S, D = x.shape
    tq = min(512, S)
    n = S // tq

    def body(x_ref, g_ref, t_ref, loss_ref, dx_ref, dxb_ref, dg_ref, lacc_ref, gacc_ref):
        i = pl.program_id(0)
        xv = x_ref[...]
        gv = g_ref[...]
        r = lax.rsqrt(jnp.mean(xv * xv, axis=-1, keepdims=True) + RMS_EPS)
        xh = xv * r
        e = xh * gv - t_ref[...]
        dy = e * (1.0 / D)
        dxh = dy * gv
        dx = r * (dxh - xh * jnp.mean(dxh * xh, axis=-1, keepdims=True))
        dx_ref[...] = dx
        dxb_ref[...] = dx.astype(dxb_ref.dtype)
        lpart = jnp.sum((e * e).reshape(tq // 8, 8, D), axis=0)
        gpart = jnp.sum((dy * xh).reshape(tq // 8, 8, D), axis=0)

        @pl.when(i == 0)
        def _():
            lacc_ref[...] = lpart
            gacc_ref[...] = gpart

        @pl.when(i > 0)
        def _():
            lacc_ref[...] += lpart
            gacc_ref[...] += gpart

        @pl.when(i == n - 1)
        def _():
            dg_ref[...] = jnp.sum(gacc_ref[...], axis=0, keepdims=True)
            tot = jnp.sum(jnp.sum(lacc_ref[...], axis=0, keepdims=True), axis=1, keepdims=True)
            loss_ref[...] = jnp.broadcast_to(tot * (0.5 / D), (1, LANE))

    return pl.pallas_call(
        body,
        name="loss_head",
        grid=(n,),
        in_specs=[pl.BlockSpec((tq, D), lambda i: (i, 0)), pl.BlockSpec((1, D), lambda i: (0, 0)),
                  pl.BlockSpec((tq, D), lambda i: (i, 0))],
        out_specs=[pl.BlockSpec((1, LANE), lambda i: (0, 0)), pl.BlockSpec((tq, D), lambda i: (i, 0)),
                   pl.BlockSpec((tq, D), lambda i: (i, 0)), pl.BlockSpec((1, D), lambda i: (0, 0))],
        out_shape=[jax.ShapeDtypeStruct((1, LANE), F32), jax.ShapeDtypeStruct((S, D), F32),
                   jax.ShapeDtypeStruct((S, D), BF16), jax.ShapeDtypeStruct((1, D), F32)],
        scratch_shapes=[pltpu.VMEM((8, D), F32), pltpu.VMEM((8, D), F32)],
        compiler_params=_cp(("arbitrary",)),
    )(x, g, t)


def _ret_tables(S):
    T = min(RET_BLOCK, S)
    inv_freq = 1.0 / (RET_ROPE_BASE ** jnp.linspace(0.0, 1.0, RET_HEAD_DIM // 2, dtype=F32))
    ang = jnp.arange(S, dtype=F32)[:, None] * jnp.repeat(inv_freq, 2)[None, :]
    cosf = jnp.cos(ang)
    sins = jnp.sin(ang) * jnp.asarray(np.tile([-1.0, 1.0], RET_HEAD_DIM // 2), F32)[None, :]
    log_g = np.log1p(-np.power(2.0, -5.0 - np.arange(RET_HEADS, dtype=np.float64)))
    pos = np.arange(T, dtype=np.float64)
    diff = pos[:, None] - pos[None, :]
    same = (pos[:, None] // CHUNK) == (pos[None, :] // CHUNK)
    seen = same | (diff > 0)
    dmat = np.where(seen[None], np.exp(np.abs(diff)[None] * log_g[:, None, None]), 0.0)
    aq = np.exp((pos[None, :] + 1.0) * log_g[:, None])
    ak = np.exp((T - 1.0 - pos[None, :]) * log_g[:, None])
    lam = np.exp(T * log_g)
    bc = lambda v: jnp.asarray(np.broadcast_to(v[..., None], v.shape + (LANE,)), F32)
    return dict(cos=cosf, sin=sins, dmat=jnp.asarray(dmat, F32), aq=bc(aq), ak=bc(ak),
                lam=jnp.asarray(np.broadcast_to(lam[:, None, None], (RET_HEADS, 1, LANE)), F32))


def _rot(x, cos, sin_s, even):
    sw = jnp.where(even, pltpu.roll(x, LANE - 1, 1), pltpu.roll(x, 1, 1))
    return x * cos + sw * sin_s


def _rot_t(dy, cos, sin_s, even):
    t = dy * sin_s
    return dy * cos + jnp.where(even, pltpu.roll(t, LANE - 1, 1), pltpu.roll(t, 1, 1))


def _ret_specs(T, rev_nb=None):
    blk = (lambda b: b) if rev_nb is None else (lambda b: rev_nb - 1 - b)
    whole = lambda shape: pl.BlockSpec(shape, lambda b: (0,) * len(shape))
    specs = [pl.BlockSpec((T, AB_IN_WIDTH), lambda b: (blk(b), 0)),
             pl.BlockSpec((T, LANE), lambda b: (blk(b), 0)),
             pl.BlockSpec((T, LANE), lambda b: (blk(b), 0)),
             whole((RET_HEADS, T, T)), whole((RET_HEADS, T, LANE)), whole((RET_HEADS, T, LANE)),
             whole((RET_HEADS, 1, LANE)), whole((1, RET_WIDTH))]
    return specs, blk


def _head_views(h, z_ref, tabs, token_refs, head_refs):
    zs = [z_ref.at[:, (o * RET_HEADS + h) * LANE:(o * RET_HEADS + h + 1) * LANE] for o in range(4)]
    hs = slice(h * LANE, (h + 1) * LANE)
    return zs, [t.at[h] for t in tabs], [r.at[:, hs] for r in token_refs], [r.at[h] for r in head_refs]


def _ret_fwd(name, z, tb, gain, rider=None):
    S = z.shape[0]
    T = min(RET_BLOCK, S)
    nb = S // T
    specs, blk = _ret_specs(T)

    def body(z_ref, cos_r, sin_r, d_all, aq_all, ak_all, lam_all, gain_all, cat_all, opre_all, st_all, state_all):
        @pl.when(pl.program_id(0) == 0)
        def _():
            state_all[...] = jnp.zeros_like(state_all)

        for h in range(RET_HEADS):
            zs, tabs, toks, heads = _head_views(h, z_ref, (d_all, aq_all, ak_all, lam_all),
                                                (gain_all, cat_all, opre_all), (st_all, state_all))
            head(*zs, cos_r, sin_r, *tabs, *toks, *heads)

    def head(zq, zk, zv, zg, cos_r, sin_r, d_r, aq_r, ak_r, lam_r, gain_r, ret_o, opre_o, st_o, state):
        even = (lax.broadcasted_iota(jnp.int32, (T, LANE), 1) & 1) == 0
        c, s = cos_r[...], sin_r[...]
        q = _rot(zq[...].astype(F32), c, s, even)
        k = _rot(zk[...].astype(F32), c, s, even) * KSCALE
        qb, kb, vb = q.astype(BF16), k.astype(BF16), zv[...].astype(BF16)
        p = (_dot_nt(qb, kb) * d_r[...]).astype(BF16)
        st = state[...]
        st_o[...] = st
        o = _dot(p, vb) + _dot((q * aq_r[...]).astype(BF16), st.astype(BF16))
        state[...] = st * lam_r[...] + _dot_tn((k * ak_r[...]).astype(BF16), vb)
        opre_o[...] = o
        mu = jnp.mean(o, axis=-1, keepdims=True)
        d = o - mu
        y = d * lax.rsqrt(jnp.mean(d * d, axis=-1, keepdims=True) + GN_EPS)
        g = zg[...].astype(F32)
        ret_o[...] = ((g * jax.nn.sigmoid(g)) * (y * gain_r[...])).astype(ret_o.dtype)

    out_blk = pl.BlockSpec((T, RET_WIDTH), lambda b: (b, 0))
    grid = (nb,)
    body, r_in, r_out, r_shapes, r_sems, aliases = _riding(rider, body, 8, 3, grid)
    outs = pl.pallas_call(
        body,
        name=name,
        grid=grid,
        in_specs=specs + r_in,
        out_specs=[out_blk, out_blk, pl.BlockSpec((RET_HEADS, None, LANE, LANE), lambda b: (0, b, 0, 0))] + r_out,
        out_shape=[jax.ShapeDtypeStruct((S, D_MODEL), BF16), jax.ShapeDtypeStruct((S, RET_WIDTH), F32),
                   jax.ShapeDtypeStruct((RET_HEADS, nb, LANE, LANE), F32)] + r_shapes,
        input_output_aliases=aliases,
        scratch_shapes=[pltpu.VMEM((RET_HEADS, LANE, LANE), F32)] + r_sems,
        compiler_params=_cp(("arbitrary",)),
    )(z, tb["cos"], tb["sin"], tb["dmat"], tb["aq"], tb["ak"], tb["lam"], gain,
      *(rider.operands if rider is not None else ()))
    return tuple(outs[:3]) if rider is None else (tuple(outs[:3]), list(outs[3:]))


def _ret_bwd(name, z, tb, gain, opre, states, dcat, rider=None):
    S = z.shape[0]
    T = min(RET_BLOCK, S)
    nb = S // T
    specs, blk = _ret_specs(T, rev_nb=nb)
    tok = pl.BlockSpec((T, RET_WIDTH), lambda b: (blk(b), 0))

    def body(z_ref, cos_r, sin_r, d_all, aq_all, ak_all, lam_all, gain_all, opre_all, st_all, dret_all,
             dz_ref, dgain_all, dstate_all):
        @pl.when(pl.program_id(0) == 0)
        def _():
            dstate_all[...] = jnp.zeros_like(dstate_all)
            dgain_all[...] = jnp.zeros_like(dgain_all)

        for h in range(RET_HEADS):
            zs, tabs, toks, heads = _head_views(h, z_ref, (d_all, aq_all, ak_all, lam_all),
                                                (gain_all, opre_all, dret_all, dgain_all), (st_all, dstate_all))
            dzs, _, _, _ = _head_views(h, dz_ref, (), (), ())
            gain_r, opre_r, dret_r, dgain_o = toks
            head(*zs, cos_r, sin_r, *tabs, gain_r, opre_r, heads[0], dret_r, *dzs, dgain_o, heads[1])

    def head(zq, zk, zv, zg, cos_r, sin_r, d_r, aq_r, ak_r, lam_r, gain_r, opre_r, st_r, dret_r,
             dq_o, dk_o, dv_o, dg_o, dgain_o, dstate):
        even = (lax.broadcasted_iota(jnp.int32, (T, LANE), 1) & 1) == 0
        c, s = cos_r[...], sin_r[...]
        aq, ak, dm = aq_r[...], ak_r[...], d_r[...]
        q = _rot(zq[...].astype(F32), c, s, even)
        k = _rot(zk[...].astype(F32), c, s, even) * KSCALE
        qb, kb, vb = q.astype(BF16), k.astype(BF16), zv[...].astype(BF16)
        pb = (_dot_nt(qb, kb) * dm).astype(BF16)
        g = zg[...].astype(F32)
        sig = jax.nn.sigmoid(g)
        o = opre_r[...]
        mu = jnp.mean(o, axis=-1, keepdims=True)
        d = o - mu
        rstd = lax.rsqrt(jnp.mean(d * d, axis=-1, keepdims=True) + GN_EPS)
        y = d * rstd
        gain_v = gain_r[...]
        dret = dret_r[...].astype(F32)
        dyg = dret * (g * sig)
        dg_o[...] = (dret * (y * gain_v) * (sig * (1.0 + g * (1.0 - sig)))).astype(dg_o.dtype)
        dgain_o[...] += jnp.sum(dyg * y, axis=0, keepdims=True)
        dy = dyg * gain_v
        do = rstd * (dy - jnp.mean(dy, axis=-1, keepdims=True) - y * jnp.mean(dy * y, axis=-1, keepdims=True))
        dob = do.astype(BF16)
        stb = st_r[...].astype(BF16)
        dsn = dstate[...]
        dsnb = dsn.astype(BF16)
        dpb = (_dot_nt(dob, vb) * dm).astype(BF16)
        dq = _dot(dpb, kb) + _dot_nt(dob, stb) * aq
        dk = _dot_tn(dpb, qb) + _dot_nt(vb, dsnb) * ak
        dv = _dot_tn(pb, dob) + _dot((k * ak).astype(BF16), dsnb)
        dstate[...] = dsn * lam_r[...] + _dot_tn((q * aq).astype(BF16), dob)
        dq_o[...] = _rot_t(dq, c, s, even).astype(dq_o.dtype)
        dk_o[...] = _rot_t(dk * KSCALE, c, s, even).astype(dk_o.dtype)
        dv_o[...] = dv.astype(dv_o.dtype)

    grid = (nb,)
    body, r_in, r_out, r_shapes, r_sems, aliases = _riding(rider, body, 11, 2, grid)
    outs = pl.pallas_call(
        body,
        name=name,
        grid=grid,
        in_specs=specs + [tok, pl.BlockSpec((RET_HEADS, None, LANE, LANE), lambda b: (0, blk(b), 0, 0)), tok] + r_in,
        out_specs=[pl.BlockSpec((T, 4 * RET_WIDTH), lambda b: (blk(b), 0)),
                   pl.BlockSpec((1, RET_WIDTH), lambda b: (0, 0))] + r_out,
        out_shape=[jax.ShapeDtypeStruct((S, AB_IN_WIDTH), BF16), jax.ShapeDtypeStruct((1, RET_WIDTH), F32)] + r_shapes,
        input_output_aliases=aliases,
        scratch_shapes=[pltpu.VMEM((RET_HEADS, LANE, LANE), F32)] + r_sems,
        compiler_params=_cp(("arbitrary",)),
    )(z, tb["cos"], tb["sin"], tb["dmat"], tb["aq"], tb["ak"], tb["lam"], gain, opre, states, dcat,
      *(rider.operands if rider is not None else ()))
    return tuple(outs[:2]) if rider is None else (tuple(outs[:2]), list(outs[2:]))


def _pool_counts(t0, rows):
    t = t0 + lax.broadcasted_iota(jnp.int32, (rows, POOL_WIDTH), 0)
    grp = lax.broadcasted_iota(jnp.int32, (rows, POOL_WIDTH), 1) >> 7
    win = jnp.where(grp == 0, POOL_WINDOWS[0], jnp.where(grp == 1, POOL_WINDOWS[1],
                    jnp.where(grp == 2, POOL_WINDOWS[2], POOL_WINDOWS[3])))
    return jnp.maximum(jnp.minimum(t + 1, win), 1).astype(F32), grp


def _window_sums(ext, grp, sign):
    n = ext.shape[0]
    sh = lambda v, k: pltpu.roll(v, k % n if sign > 0 else (n - k) % n, 0)
    s2 = ext + sh(ext, 1)
    s4 = s2 + sh(s2, 2)
    s8 = s4 + sh(s4, 4)
    s16 = s8 + sh(s8, 8)
    return jnp.where(grp == 0, s2, jnp.where(grp == 1, s4, jnp.where(grp == 2, s8, s16)))


def _pool_fwd(name, z, w_pool, scale, cat):
    S = z.shape[0]
    T = min(512, S)
    nb = S // T
    pcol = AB_IN_WIDTH // POOL_WIDTH - 1
    hb = T // POOL_HALO

    def body(p_ref, halo_ref, w_ref, sc_ref, cat_in, out_ref, pooled_ref):
        b = pl.program_id(0)
        cur = p_ref[...].astype(F32)
        halo = jnp.where(b > 0, halo_ref[...].astype(F32), 0.0)
        ext = jnp.concatenate([halo, cur], axis=0)
        cnt, grp = _pool_counts(b * T - POOL_HALO, T + POOL_HALO)
        sums = _window_sums(ext, grp, +1)
        pooled = (sums / cnt)[POOL_HALO:] - cur
        pb = pooled.astype(BF16)
        pooled_ref[...] = pb
        for gi in range(len(POOL_WINDOWS)):
            cs = slice(gi * POOL_GROUP_DIM, (gi + 1) * POOL_GROUP_DIM)
            mixed = _dot(pb[:, cs], w_ref[gi].astype(BF16))
            out_ref[:, cs] = (mixed * sc_ref[:, cs]).astype(out_ref.dtype)

    return pl.pallas_call(
        body,
        name=name,
        grid=(nb,),
        in_specs=[pl.BlockSpec((T, POOL_WIDTH), lambda b: (b, pcol)),
                  pl.BlockSpec((POOL_HALO, POOL_WIDTH), lambda b: (jnp.maximum(b * hb - 1, 0), pcol)),
                  pl.BlockSpec((4, POOL_GROUP_DIM, POOL_GROUP_DIM), lambda b: (0, 0, 0)),
                  pl.BlockSpec((1, POOL_WIDTH), lambda b: (0, 0)), _ANY],
        out_specs=[pl.BlockSpec((T, POOL_WIDTH), lambda b: (b, 1)), pl.BlockSpec((T, POOL_WIDTH), lambda b: (b, 0))],
        out_shape=[jax.ShapeDtypeStruct(cat.shape, cat.dtype), jax.ShapeDtypeStruct((S, POOL_WIDTH), BF16)],
        input_output_aliases={4: 0},
        compiler_params=_cp(("parallel",)),
    )(z, z, w_pool, scale, cat)


def _pool_bwd(name, pooled, w_pool, scale, dcat, dz):
    S = pooled.shape[0]
    T = min(512, S)
    nb = S // T
    hb = T // POOL_HALO
    last_h = S // POOL_HALO - 1
    pcol = AB_IN_WIDTH // POOL_WIDTH - 1

    def body(d_ref, dn_ref, pooled_ref, w_ref, sc_ref, dz_in, dp_ref, dw_ref, dsc_ref):
        b = pl.program_id(0)

        @pl.when(b == 0)
        def _():
            dw_ref[...] = jnp.zeros_like(dw_ref)
            dsc_ref[...] = jnp.zeros_like(dsc_ref)

        sc = sc_ref[...]
        dout = d_ref[...].astype(F32)
        dnext = jnp.where(b < nb - 1, dn_ref[...].astype(F32), 0.0)
        dmix = jnp.concatenate([dout, dnext], axis=0) * sc
        dmb = dmix.astype(BF16)
        pb = pooled_ref[...]
        dpooled = []
        for gi in range(len(POOL_WINDOWS)):
            cs = slice(gi * POOL_GROUP_DIM, (gi + 1) * POOL_GROUP_DIM)
            wb = w_ref[gi].astype(BF16)
            dpooled.append(_dot_nt(dmb[:, cs], wb))
            dw_ref[gi] += _dot_tn(pb[:, cs], dmb[:T, cs])
            mixed = _dot(pb[:, cs], wb)
            dsc_ref[:, cs] += jnp.sum(dout[:, cs] * mixed, axis=0, keepdims=True)
        dpl = jnp.concatenate(dpooled, axis=1)
        cnt, grp = _pool_counts(b * T, T + POOL_HALO)
        sums = _window_sums(dpl / cnt, grp, -1)
        dp_ref[...] = (sums[:T] - dpl[:T]).astype(dp_ref.dtype)

    return pl.pallas_call(
        body,
        name=name,
        grid=(nb,),
        in_specs=[pl.BlockSpec((T, POOL_WIDTH), lambda b: (b, 1)),
                  pl.BlockSpec((POOL_HALO, POOL_WIDTH), lambda b: (jnp.minimum((b + 1) * hb, last_h), 1)),
                  pl.BlockSpec((T, POOL_WIDTH), lambda b: (b, 0)),
                  pl.BlockSpec((4, POOL_GROUP_DIM, POOL_GROUP_DIM), lambda b: (0, 0, 0)),
                  pl.BlockSpec((1, POOL_WIDTH), lambda b: (0, 0)), _ANY],
        out_specs=[pl.BlockSpec((T, POOL_WIDTH), lambda b: (b, pcol)),
                   pl.BlockSpec((4, POOL_GROUP_DIM, POOL_GROUP_DIM), lambda b: (0, 0, 0)),
                   pl.BlockSpec((1, POOL_WIDTH), lambda b: (0, 0))],
        out_shape=[jax.ShapeDtypeStruct(dz.shape, dz.dtype),
                   jax.ShapeDtypeStruct((4, POOL_GROUP_DIM, POOL_GROUP_DIM), F32),
                   jax.ShapeDtypeStruct((1, POOL_WIDTH), F32)],
        input_output_aliases={5: 0},
        compiler_params=_cp(("arbitrary",)),
    )(dcat, dcat, pooled, w_pool, scale, dz)


ATT_STRIP = 32
ATT_Q = 256
ATT_W = ATT_Q + LEFT_CHUNKS * CHUNK


def _rel_index():
    j = np.arange(ATT_W)
    rel = np.clip(LEFT_CHUNKS * CHUNK - j, -REL_CLIP, REL_CLIP) + REL_CLIP
    fwd = np.where(j < BAND, rel, N_REL)
    bwd = np.where(j <= ATT_W - CHUNK, fwd, 2 * REL_CLIP)
    return tuple(jnp.asarray(v.reshape(1, ATT_W), jnp.int32) for v in (fwd, bwd))


def _bias_table(name, rel_bias, rel_idx):
    rb = jnp.concatenate([rel_bias, jnp.full((ATT_HEADS, 1), NEG_INF, F32),
                          jnp.zeros((ATT_HEADS, N_REL_PAD - N_REL - 1), F32)], axis=1)

    def body(rb_ref, idx_ref, o_ref, row0_ref):
        r = lax.broadcasted_iota(jnp.int32, (N_REL_PAD, ATT_W), 0)
        onehot = (r == idx_ref[...]).astype(F32)
        row0_ref[...] = jnp.dot(rb_ref[...], onehot, precision=lax.Precision.HIGHEST, preferred_element_type=F32)
        col = lax.broadcasted_iota(jnp.int32, (CHUNK, ATT_W), 1)
        row = lax.broadcasted_iota(jnp.int32, (CHUNK, ATT_W), 0)
        for h in range(ATT_HEADS):
            same = jnp.broadcast_to(row0_ref[pl.ds(h, 1), :], (CHUNK, ATT_W))
            turned = pltpu.roll(same, 0, 1, stride=1, stride_axis=0)
            o_ref[h] = jnp.where(col >= BAND, NEG_INF, jnp.where(col < row, same, turned))

    return pl.pallas_call(
        body,
        name=name,
        out_shape=jax.ShapeDtypeStruct((ATT_HEADS, CHUNK, ATT_W), F32),
        scratch_shapes=[pltpu.VMEM((ATT_HEADS, ATT_W), F32)],
        compiler_params=pltpu.CompilerParams(vmem_limit_bytes=VMEM_LIMIT),
    )(rb, rel_idx)


def _bias_grad(name, dband, rel_idx):
    def body(d_ref, idx_ref, o_ref, sums_ref):
        row = lax.broadcasted_iota(jnp.int32, (CHUNK, ATT_W), 0)
        for h in range(ATT_HEADS):
            back = d_ref[h]
            for bit in range(CHUNK.bit_length() - 1):
                back = jnp.where(((row >> bit) & 1) == 1, pltpu.roll(back, ATT_W - (1 << bit), 1), back)
            sums_ref[pl.ds(h, 1), :] = jnp.sum(back, axis=0, keepdims=True)
        r = lax.broadcasted_iota(jnp.int32, (N_REL_PAD, ATT_W), 0)
        onehot = (r == idx_ref[...]).astype(F32)
        o_ref[...] = lax.dot_general(sums_ref[...], onehot, (((1,), (1,)), ((), ())),
                                     precision=lax.Precision.HIGHEST, preferred_element_type=F32)

    out = pl.pallas_call(
        body,
        name=name,
        out_shape=jax.ShapeDtypeStruct((ATT_HEADS, N_REL_PAD), F32),
        scratch_shapes=[pltpu.VMEM((ATT_HEADS, ATT_W), F32)],
        compiler_params=pltpu.CompilerParams(vmem_limit_bytes=VMEM_LIMIT),
    )(dband, rel_idx)
    return out[:, :N_REL]


def _attn_unit(q_ref, kw_ref, bias_ref, e, u, lane):
    mine = (lane < ATT_HEAD_DIM) if e == 0 else (lane >= ATT_HEAD_DIM)
    qm = jnp.where(mine, q_ref[u * ATT_Q:(u + 1) * ATT_Q, :] * QSCALE, 0)
    kw = kw_ref[u * ATT_Q:u * ATT_Q + ATT_W, :]
    s = _dot_nt(qm, kw) + bias_ref[u, e]
    p = jnp.exp(s - jnp.max(s, axis=-1, keepdims=True))
    return p, 1.0 / jnp.sum(p, axis=-1, keepdims=True), qm, kw, mine


def _attn_in_specs(nb):
    T = ATT_BLOCK
    hp = ATT_HEADS // 2
    cur = lambda off: pl.BlockSpec((T, LANE), lambda h, b: (jnp.minimum(b, nb - 1), off + h))
    prev = lambda off: pl.BlockSpec((T, LANE), lambda h, b: (jnp.clip(b - 1, 0, nb - 1), off + h))
    return [cur(0), prev(hp), cur(hp), prev(2 * hp), cur(2 * hp),
            pl.BlockSpec((None, 2, CHUNK, ATT_W), lambda h, b: (h, 0, 0, 0))]


def _spread_bias(bias_ref, bm_ref, block):
    col = lax.broadcasted_iota(jnp.int32, (CHUNK, ATT_W), 1)
    for first in (True, False):
        @pl.when(block == (0 if first else 1))
        def _(first=first):
            for u in range(ATT_BLOCK // ATT_Q):
                for e in range(2):
                    for j in range(ATT_Q // CHUNK):
                        rows = pltpu.roll(bias_ref[e], j * CHUNK, 1)
                        if first:
                            rows = jnp.where(col >= ATT_BLOCK - u * ATT_Q, rows, NEG_INF)
                        bm_ref[u, e, j * CHUNK:(j + 1) * CHUNK, :] = rows


def _riding(rider, body, n_in, n_out, grid):
    if rider is None:
        return body, [], [], [], [], {}
    n_rin, n_rout = len(rider.operands), len(rider.out_shapes)
    steps = int(np.prod(grid))

    def riding(*refs):
        ins, rin = refs[:n_in], refs[n_in:n_in + n_rin]
        outs = refs[n_in + n_rin:n_in + n_rin + n_out]
        rout = refs[n_in + n_rin + n_out:n_in + n_rin + n_out + n_rout]
        rest = refs[n_in + n_rin + n_out + n_rout:]
        scratch, sems = rest[:-2], rest[-2:]
        step_no = pl.program_id(0)
        for axis in range(1, len(grid)):
            step_no = step_no * grid[axis] + pl.program_id(axis)

        @pl.when(step_no == 0)
        def _():
            rider.start(rin, rout, *sems)

        body(*ins, *outs, *scratch)
        if rider.relay is not None:
            assert steps >= 3

            @pl.when(step_no == steps - 2)
            def _():
                rider.relay(rin, rout, *sems)

        @pl.when(step_no == steps - 1)
        def _():
            rider.finish(rin, rout, *sems)

    sems = [pltpu.SemaphoreType.DMA((rider.n_sems,)), pltpu.SemaphoreType.DMA((rider.n_sems,))]
    aliases = {n_in + src: n_out + dst for src, dst in rider.aliases}
    return riding, [_ANY] * n_rin, [_ANY] * n_rout, list(rider.out_shapes), sems, aliases


def _attn_fwd(name, qkv, bias, rider=None):
    S = qkv.shape[0]
    T = ATT_BLOCK
    nb = S // T

    def body(q_ref, kp_ref, kc_ref, vp_ref, vc_ref, band_ref, o_ref, kw_ref, vw_ref, bias_ref, s_ref, p_ref, inv_ref):
        _spread_bias(band_ref, bias_ref, pl.program_id(1))
        kw_ref[0:T] = kp_ref[...]
        kw_ref[T:2 * T] = kc_ref[...]
        vw_ref[0:T] = vp_ref[...]
        vw_ref[T:2 * T] = vc_ref[...]
        lane = lax.broadcasted_iota(jnp.int32, (ATT_Q, LANE), 1)
        for u in range(T // ATT_Q):
            vw = vw_ref[u * ATT_Q:u * ATT_Q + ATT_W, :]
            kw = kw_ref[u * ATT_Q:u * ATT_Q + ATT_W, :]
            outs = []
            for e in range(2):
                mine = (lane < ATT_HEAD_DIM) if e == 0 else (lane >= ATT_HEAD_DIM)
                qm = jnp.where(mine, q_ref[u * ATT_Q:(u + 1) * ATT_Q, :] * QSCALE, 0)
                s_ref[e] = _dot_nt(qm, kw)
                for r in range(ATT_Q // ATT_STRIP):
                    rows = slice(r * ATT_STRIP, (r + 1) * ATT_STRIP)
                    s = s_ref[e, rows, :] + bias_ref[u, e, rows, :]
                    p = jnp.exp(s - jnp.max(s, axis=-1, keepdims=True))
                    inv_ref[e, rows, :] = jnp.broadcast_to(1.0 / jnp.sum(p, axis=-1, keepdims=True), (ATT_STRIP, LANE))
                    p_ref[e, rows, :] = p.astype(BF16)
                outs.append(_dot(p_ref[e], vw) * inv_ref[e])
            o_ref[u * ATT_Q:(u + 1) * ATT_Q, :] = jnp.where(lane < ATT_HEAD_DIM, outs[0], outs[1]).astype(o_ref.dtype)

    grid = (ATT_HEADS // 2, nb)
    body, r_in, r_out, r_shapes, r_sems, aliases = _riding(rider, body, 6, 1, grid)
    outs = pl.pallas_call(
        body,
        name=name,
        grid=grid,
        in_specs=_attn_in_specs(nb) + r_in,
        out_specs=[pl.BlockSpec((T, LANE), lambda h, b: (b, h))] + r_out,
        out_shape=[jax.ShapeDtypeStruct((S, D_MODEL), BF16)] + r_shapes,
        input_output_aliases=aliases,
        scratch_shapes=[pltpu.VMEM((2 * T, LANE), BF16), pltpu.VMEM((2 * T, LANE), BF16),
                        pltpu.VMEM((T // ATT_Q, 2, ATT_Q, ATT_W), F32), pltpu.VMEM((2, ATT_Q, ATT_W), F32),
                        pltpu.VMEM((2, ATT_Q, ATT_W), BF16), pltpu.VMEM((2, ATT_Q, LANE), F32)] + r_sems,
        compiler_params=_cp(("parallel" if rider is None else "arbitrary", "arbitrary")),
    )(qkv, qkv, qkv, qkv, qkv, bias, *(rider.operands if rider is not None else ()))
    return outs[0] if rider is None else (outs[0], list(outs[1:]))


def _attn_bwd(name, qkv, bias, do, rider=None):
    S = qkv.shape[0]
    T = ATT_BLOCK
    nb = S // T

    def body(q_ref, kp_ref, kc_ref, vp_ref, vc_ref, band_ref, do_ref,
             dq_ref, dk_ref, dv_ref, dband_ref, kw_ref, vw_ref, dkw_ref, dvw_ref, bias_ref, dbias_ref,
             s_ref, dp_ref, dsb_ref, pb_ref, inv_ref):
        b = pl.program_id(1)

        _spread_bias(band_ref, bias_ref, b)

        @pl.when(b == 0)
        def _():
            dbias_ref[...] = jnp.zeros_like(dbias_ref)
            dkw_ref[:, T:2 * T] = jnp.zeros((LANE, T), F32)
            dvw_ref[:, T:2 * T] = jnp.zeros((LANE, T), F32)

        dkw_ref[:, 0:T] = dkw_ref[:, T:2 * T]
        dvw_ref[:, 0:T] = dvw_ref[:, T:2 * T]
        dkw_ref[:, T:2 * T] = jnp.zeros((LANE, T), F32)
        dvw_ref[:, T:2 * T] = jnp.zeros((LANE, T), F32)

        @pl.when(b < nb)
        def _():
            kw_ref[0:T] = kp_ref[...]
            kw_ref[T:2 * T] = kc_ref[...]
            vw_ref[0:T] = vp_ref[...]
            vw_ref[T:2 * T] = vc_ref[...]
            lane = lax.broadcasted_iota(jnp.int32, (ATT_Q, LANE), 1)
            for u in range(T // ATT_Q):
                rows = slice(u * ATT_Q, (u + 1) * ATT_Q)
                win = slice(u * ATT_Q, u * ATT_Q + ATT_W)
                vw = vw_ref[win, :]
                do2 = do_ref[rows, :]
                dqs, dk, dv = [], None, None
                kw = kw_ref[win, :]
                for e in range(2):
                    mine = (lane < ATT_HEAD_DIM) if e == 0 else (lane >= ATT_HEAD_DIM)
                    qm = jnp.where(mine, q_ref[rows, :] * QSCALE, 0)
                    dom = jnp.where(mine, do2, 0)
                    s_ref[e] = _dot_nt(qm, kw)
                    dp_ref[e] = _dot_nt(dom, vw)
                    for r in range(ATT_Q // ATT_STRIP):
                        st = slice(r * ATT_STRIP, (r + 1) * ATT_STRIP)
                        s = s_ref[e, st, :] + bias_ref[u, e, st, :]
                        p = jnp.exp(s - jnp.max(s, axis=-1, keepdims=True))
                        inv = 1.0 / jnp.sum(p, axis=-1, keepdims=True)
                        dp = dp_ref[e, st, :]
                        ds = p * ((dp - jnp.sum(p * dp, axis=-1, keepdims=True) * inv) * inv)
                        dbias_ref[e, st, :] += ds
                        dsb_ref[e, st, :] = ds.astype(BF16)
                        pb_ref[e, st, :] = p.astype(BF16)
                        inv_ref[e, st, :] = jnp.broadcast_to(inv, (ATT_STRIP, LANE))
                    dsb = dsb_ref[e]
                    dqs.append(_dot(dsb, kw))
                    dk_e = _dot_tn(qm, dsb)
                    dv_e = _dot_tn((dom * inv_ref[e]).astype(BF16), pb_ref[e])
                    dk = dk_e if dk is None else dk + dk_e
                    dv = dv_e if dv is None else dv + dv_e
                dq_ref[rows, :] = (jnp.where(lane < ATT_HEAD_DIM, dqs[0], dqs[1]) * QSCALE).astype(dq_ref.dtype)
                dkw_ref[:, win] += dk
                dvw_ref[:, win] += dv

        @pl.when(b > 0)
        def _():
            dk_ref[...] = dkw_ref[:, 0:T].T.astype(dk_ref.dtype)
            dv_ref[...] = dvw_ref[:, 0:T].T.astype(dv_ref.dtype)

        @pl.when(b == nb)
        def _():
            for e in range(2):
                acc = dbias_ref[e, 0:CHUNK, :]
                for j in range(1, ATT_Q // CHUNK):
                    acc = acc + pltpu.roll(dbias_ref[e, j * CHUNK:(j + 1) * CHUNK, :], ATT_W - j * CHUNK, 1)
                dband_ref[e] = acc

    tok = jax.ShapeDtypeStruct((S, D_MODEL), BF16)
    prev_out = pl.BlockSpec((T, LANE), lambda h, b: (jnp.maximum(b - 1, 0), h))
    grid = (ATT_HEADS // 2, nb + 1)
    body, r_in, r_out, r_shapes, r_sems, aliases = _riding(rider, body, 7, 4, grid)
    outs = pl.pallas_call(
        body,
        name=name,
        grid=grid,
        in_specs=_attn_in_specs(nb) + [pl.BlockSpec((T, LANE), lambda h, b: (jnp.minimum(b, nb - 1), h))] + r_in,
        out_specs=[pl.BlockSpec((T, LANE), lambda h, b: (jnp.minimum(b, nb - 1), h)), prev_out, prev_out,
                   pl.BlockSpec((None, 2, CHUNK, ATT_W), lambda h, b: (h, 0, 0, 0))] + r_out,
        out_shape=[tok, tok, tok, jax.ShapeDtypeStruct((ATT_HEADS // 2, 2, CHUNK, ATT_W), F32)] + r_shapes,
        input_output_aliases=aliases,
        scratch_shapes=[pltpu.VMEM((2 * T, LANE), BF16), pltpu.VMEM((2 * T, LANE), BF16),
                        pltpu.VMEM((LANE, 2 * T), F32), pltpu.VMEM((LANE, 2 * T), F32),
                        pltpu.VMEM((T // ATT_Q, 2, ATT_Q, ATT_W), F32), pltpu.VMEM((2, ATT_Q, ATT_W), F32),
                        pltpu.VMEM((2, ATT_Q, ATT_W), F32), pltpu.VMEM((2, ATT_Q, ATT_W), F32),
                        pltpu.VMEM((2, ATT_Q, ATT_W), BF16), pltpu.VMEM((2, ATT_Q, ATT_W), BF16),
                        pltpu.VMEM((2, ATT_Q, LANE), F32)] + r_sems,
        compiler_params=_cp(("parallel" if rider is None else "arbitrary", "arbitrary")),
    )(qkv, qkv, qkv, qkv, qkv, bias, do, *(rider.operands if rider is not None else ()))
    return tuple(outs[:4]) if rider is None else (tuple(outs[:4]), list(outs[4:]))


def _local_step(x, target, small, W):
    S = x.shape[0]
    tb = _ret_tables(S)
    rel_fwd, rel_bwd = _rel_index()
    saved = []
    normed = (("tile", BF16),)
    deep = dict(tm=512, tk=D_FF)
    h = W.hosted("mix_norm_fwd0", lambda rider: _rms_fwd("mix_norm_fwd0", x, small["mix_norm"][0:1], rider=rider))
    for layer in range(DEPTH):
        i = layer // 2
        st = {"x_in": x, "h": h}
        g_ffn = small["ffn_norm"][layer:layer + 1]
        if layer % 2 == 0:
            z = W.mm(f"ab_in_fwd{layer}", "nn", h, W.get("ab_w_in", i), tm=2048, tn=640, out_dtype=BF16)
            gain = small["ab_gn_gain"][i:i + 1]
            cat, opre, states = W.hosted(f"ret_fwd{layer}", lambda rider: _ret_fwd(f"ret_fwd{layer}", z, tb, gain, rider=rider))
            cat, pooled = _pool_fwd(f"pool_fwd{layer}", z, small["ab_w_pool"][i], small["ab_pool_scale"][i:i + 1], cat)
            st.update(z=z, opre=opre, states=states, pooled=pooled, cat=cat)
            x, hn = W.mm(f"ab_out_fwd{layer}", "nn", cat, W.get("ab_w_out", i), extras=(x,), aux=(g_ffn,), sides=normed,
                         epi=_epi_residual_norm)
        else:
            qkv = W.mm(f"qkv_fwd{layer}", "nn", h, W.get("c_w_qkv", i), tm=2048, out_dtype=BF16)
            bias = _bias_table(f"bias_table{layer}", small["c_rel_bias"][i], rel_fwd)
            bias = bias.reshape(ATT_HEADS // 2, 2, CHUNK, ATT_W)
            att = W.hosted(f"attn_fwd{layer}", lambda rider: _attn_fwd(f"attn_fwd{layer}", qkv, bias, rider=rider))
            st.update(qkv=qkv, bias=bias, att=att)
            x, hn = W.mm(f"c_out_fwd{layer}", "nn", att, W.get("c_w_out", i), extras=(x,), aux=(g_ffn,), sides=normed,
                         epi=_epi_residual_norm)
        st["x_mid"] = x
        u = W.mm(f"ffn_in_fwd{layer}", "nn", hn, W.get("w_ffn_in", layer), out_dtype=BF16, tm=2048)
        if layer + 1 < DEPTH:
            x, h = W.mm(f"ffn_out_fwd{layer}", "nn", u, W.get("w_ffn_out", layer), a_fn=_relu2, extras=(x,),
                        aux=(small["mix_norm"][layer + 1:layer + 2],), sides=normed, epi=_epi_residual_norm, **deep)
        else:
            x = W.mm(f"ffn_out_fwd{layer}", "nn", u, W.get("w_ffn_out", layer), a_fn=_relu2, extras=(x,),
                     epi=_epi_residual, **deep)
        st.update(hn=hn, u=u)
        saved.append(st)

    loss, dx, dxb, d_final = _loss_head(x, small["final_norm"].reshape(1, D_MODEL), target)

    gs = {k: [None] * v.shape[0] for k, v in small.items() if k != "final_norm"}
    gb = {k: None for k in W.n_layers}
    landed = {k: None for k in W.n_layers}
    pending = []

    def carry(call, take=1):
        items = [pending.pop(0) for _ in range(min(take, len(pending)))]
        if not items:
            return call(None)
        riders = [_grad_rider(key, idx, gb[key], landed[key]) for key, idx in items]
        res, outs = call(_join_riders(riders))
        for (key, _), out in zip(items, outs):
            landed[key] = out
        return res

    def host(name, *args, take=1, **kw):
        return carry(lambda rider: _mm(name, *args, rider=rider, **kw), take)

    def dw(name, key, idx, a, b, call=_mm, **kw):
        gb[key] = call(name, "tn", a, b, stack=(W.n_layers[key], idx, gb[key]), out_dtype=BF16, **kw)
        pending.append((key, idx))

    gain_sums = (("tile", BF16), ("colsum", F32))
    for layer in reversed(range(DEPTH)):
        i = layer // 2
        st = saved[layer]
        ffn_mm = host if layer % 2 == 0 else _mm
        du = ffn_mm(f"ffn_out_bwd{layer}", "nt", dxb, W.get("w_ffn_out", layer), extras=(st["u"],),
                    epi=lambda acc, u: acc * (2.0 * jnp.maximum(u, 0).astype(F32)), out_dtype=BF16, tm=2048)
        dw(f"ffn_out_dw{layer}", "w_ffn_out", layer, st["u"], dxb, a_fn=_relu2, tk=2048)
        dx, dxb, dgain = ffn_mm(f"ffn_in_bwd{layer}", "nt", du, W.get("w_ffn_in", layer), extras=(st["x_mid"], dx),
                         aux=(small["ffn_norm"][layer:layer + 1],), sides=gain_sums, epi=_epi_rms_bwd, **deep)
        gs["ffn_norm"][layer] = dgain[0:1]
        dw(f"ffn_in_dw{layer}", "w_ffn_in", layer, st["hn"], du, tk=2048)
        norm_bwd = dict(extras=(st["x_in"], dx), aux=(small["mix_norm"][layer:layer + 1],), sides=gain_sums,
                        epi=_epi_rms_bwd)
        if layer % 2 == 0:
            dcat = _mm(f"ab_out_bwd{layer}", "nt", dxb, W.get("ab_w_out", i), out_dtype=BF16)
            dw(f"ab_out_dw{layer}", "ab_w_out", i, st["cat"], dxb, tk=2048)
            gain = small["ab_gn_gain"][i:i + 1]
            dz, gs["ab_gn_gain"][i] = carry(lambda rider: _ret_bwd(f"ret_bwd{layer}", st["z"], tb, gain, st["opre"],
                                                                   st["states"], dcat, rider=rider), take=len(pending))
            dz, gs["ab_w_pool"][i], gs["ab_pool_scale"][i] = _pool_bwd(
                f"pool_bwd{layer}", st["pooled"], small["ab_w_pool"][i], small["ab_pool_scale"][i:i + 1], dcat, dz)
            if layer == 0:
                dw(f"ab_in_dw{layer}", "ab_w_in", i, st["h"], dz, call=host, tn=640, tk=2048)
            dx, dxb, dgain = host(f"ab_in_bwd{layer}", "nt", dz, W.get("ab_w_in", i), tm=512, tk=AB_IN_WIDTH,
                             take=len(pending) if layer == 0 else 1, **norm_bwd)
            if layer > 0:
                dw(f"ab_in_dw{layer}", "ab_w_in", i, st["h"], dz, call=host, tn=640, tk=2048)
        else:
            datt = _mm(f"c_out_bwd{layer}", "nt", dxb, W.get("c_w_out", i), out_dtype=BF16)
            dw(f"c_out_dw{layer}", "c_w_out", i, st["att"], dxb, tk=2048)
            dq, dk, dv, dbias = carry(lambda rider: _attn_bwd(f"attn_bwd{layer}", st["qkv"], st["bias"], datt, rider=rider),
                                      take=len(pending))
            gs["c_rel_bias"][i] = _bias_grad(f"bias_grad{layer}", dbias.reshape(ATT_HEADS, CHUNK, ATT_W), rel_bwd)
            dqkv = [dq, dk, dv]
            dx, dxb, dgain = host(f"qkv_bwd{layer}", "nt", dqkv, W.get("c_w_qkv", i), tm=512, tk=3 * D_MODEL, **norm_bwd)
            dw(f"qkv_dw{layer}", "c_w_qkv", i, st["h"], dqkv, call=host, tk=2048)
        gs["mix_norm"][layer] = dgain[0:1]
    for key, idx in pending:
        landed[key], = _run_rider(f"grad_exchange_{key}{idx}", _grad_rider(key, idx, gb[key], landed[key]))

    g_small = {
        "mix_norm": jnp.concatenate(gs["mix_norm"], axis=0),
        "ffn_norm": jnp.concatenate(gs["ffn_norm"], axis=0),
        "ab_gn_gain": jnp.concatenate(gs["ab_gn_gain"], axis=0),
        "ab_w_pool": jnp.stack(gs["ab_w_pool"], axis=0),
        "ab_pool_scale": jnp.concatenate(gs["ab_pool_scale"], axis=0),
        "c_rel_bias": jnp.stack(gs["c_rel_bias"], axis=0),
        "final_norm": d_final.reshape(D_MODEL),
    }
    return loss, dx, g_small, gb, landed


_BIG = ("w_ffn_in", "w_ffn_out", "ab_w_in", "ab_w_out", "c_w_qkv", "c_w_out")
_SHARD_AXIS = {"w_ffn_in": 2, "w_ffn_out": 1, "ab_w_in": 2, "ab_w_out": 1, "c_w_qkv": 2, "c_w_out": 1}
_SMALL = ("mix_norm", "ffn_norm", "ab_gn_gain", "ab_w_pool", "ab_pool_scale", "c_rel_bias", "final_norm")


def _place():
    x, y, c = lax.axis_index("x"), lax.axis_index("y"), lax.axis_index("c")
    chips = [(1 - x, y), (x, 1 - y), (1 - x, 1 - y)]
    return x, y, c, chips


def _sub(ref, axis, start, size):
    idx = [slice(None)] * len(ref.shape)
    idx[axis] = pl.ds(pl.multiple_of(start, LANE), size)
    return ref.at[tuple(idx)]


def _gather_rider(items, shards):
    keys = sorted({k for k, _ in items})
    n = len(items)
    axes = [_SHARD_AXIS[k] - 1 for k, _ in items]
    sizes = [shards[k].shape[a + 1] for (k, _), a in zip(items, axes)]
    hsizes = [shards[k].shape[2 - a] // 2 for (k, _), a in zip(items, axes)]

    def views(ins, outs, send_sems, recv_sems):
        x, y, c, chips = _place()
        srcs = [ins[keys.index(k)].at[l] for k, l in items]

        def remote(src, dst, s, to):
            return pltpu.make_async_remote_copy(src_ref=src, dst_ref=dst, send_sem=send_sems.at[s],
                                                recv_sem=recv_sems.at[s], device_id=to, device_id_type=MESH)

        def half(w, chip, core):
            return _sub(_sub(outs[w], axes[w], chip * sizes[w], sizes[w]), 1 - axes[w], core * hsizes[w], hsizes[w])

        me = 2 * x + y
        local = [pltpu.make_async_copy(srcs[w], _sub(outs[w], axes[w], me * sizes[w], sizes[w]), send_sems.at[6 * n + w])
                 for w in range(n)]
        first = [remote(_sub(srcs[w], 1 - axes[w], c * hsizes[w], hsizes[w]), half(w, me, c), w * 6 + k, (px, py, c))
                 for w in range(n) for k, (px, py) in enumerate(chips)]
        return x, y, c, chips, remote, half, local, first

    def start(ins, outs, send_sems, recv_sems):
        *_, local, first = views(ins, outs, send_sems, recv_sems)
        for cp in local + first:
            cp.start()

    def passes(x, y, c, chips, remote, half):
        return [remote(half(w, 2 * px + py, c), half(w, 2 * px + py, c), w * 6 + 3 + k, (x, y, 1 - c))
                for w in range(n) for k, (px, py) in enumerate(chips)]

    def relay(ins, outs, send_sems, recv_sems):
        x, y, c, chips, remote, half, _, _ = views(ins, outs, send_sems, recv_sems)
        for w in range(n):
            for k, (px, py) in enumerate(chips):
                landed = half(w, 2 * px + py, c)
                remote(landed, landed, w * 6 + k, (px, py, c)).wait_recv()
        for cp in passes(x, y, c, chips, remote, half):
            cp.start()

    def finish(ins, outs, send_sems, recv_sems):
        x, y, c, chips, remote, half, local, first = views(ins, outs, send_sems, recv_sems)
        for w in range(n):
            for k, (px, py) in enumerate(chips):
                theirs = half(w, 2 * px + py, 1 - c)
                remote(theirs, theirs, w * 6 + 3 + k, (x, y, 1 - c)).wait_recv()
        for cp in first + passes(x, y, c, chips, remote, half):
            cp.wait_send()
        for cp in local:
            cp.wait()

    def full(k, a):
        shape = list(shards[k].shape[1:])
        shape[a] *= N_CHIPS
        return jax.ShapeDtypeStruct(tuple(shape), shards[k].dtype)

    return _Rider(tuple(shards[k] for k in keys), tuple(full(k, a) for (k, _), a in zip(items, axes)), 7 * n, start, finish,
                  relay=relay)


def _mixer_items(layer):
    names = ("ab_w_in", "ab_w_out") if layer % 2 == 0 else ("c_w_qkv", "c_w_out")
    return [(k, layer // 2) for k in names]


class _Weights:
    def __init__(self, shards):
        self.shards = shards
        self.n_layers = {k: shards[k].shape[0] for k in _BIG}
        self.full = {}
        first, second = _mixer_items(0)
        self.plan = {"mix_norm_fwd0": [first], "ab_in_fwd0": [second, ("w_ffn_in", 0)], "ret_fwd0": [("w_ffn_out", 0)]}
        for layer in range(1, DEPTH):
            if layer % 2 == 0:
                self.plan[f"attn_fwd{layer - 1}"] = _mixer_items(layer) + [("w_ffn_in", layer), ("w_ffn_out", layer)]
            else:
                self.plan[f"ffn_in_fwd{layer - 1}"] = _mixer_items(layer)
                self.plan[f"ffn_out_fwd{layer - 1}"] = [("w_ffn_in", layer)]
                self.plan[f"qkv_fwd{layer}"] = [("w_ffn_out", layer)]

    def _take(self, items, outs):
        self.full.update(zip(items, outs))

    def get(self, name, layer):
        return self.full[(name, layer)]

    def hosted(self, name, call):
        items = self.plan.get(name)
        if items is None:
            return call(None)
        res, outs = call(_gather_rider(items, self.shards))
        self._take(items, outs)
        return res

    def mm(self, name, *args, **kw):
        return self.hosted(name, lambda rider: _mm(name, *args, rider=rider, **kw))


def _run_rider(name, rider):
    n_in, n_out = len(rider.operands), len(rider.out_shapes)

    def body(*refs):
        ins, outs, sems = refs[:n_in], refs[n_in:n_in + n_out], refs[n_in + n_out:]
        rider.start(ins, outs, *sems)
        if rider.relay is not None:
            rider.relay(ins, outs, *sems)
        rider.finish(ins, outs, *sems)

    return pl.pallas_call(
        body,
        name=name,
        in_specs=[_ANY] * n_in,
        out_specs=[_ANY] * n_out,
        out_shape=list(rider.out_shapes),
        input_output_aliases=dict(rider.aliases),
        scratch_shapes=[pltpu.SemaphoreType.DMA((rider.n_sems,)), pltpu.SemaphoreType.DMA((rider.n_sems,))],
        compiler_params=pltpu.CompilerParams(has_side_effects=True),
    )(*rider.operands)


def _grad_rider(name, layer, grad, landing):
    axis = _SHARD_AXIS[name] - 1
    L, R, C = grad.shape
    shard = (R // N_CHIPS, C) if axis == 0 else (R, C // N_CHIPS)
    size = shard[axis]

    def copies(ins, outs, send_sems, recv_sems):
        x, y, c, chips = _place()
        return [pltpu.make_async_remote_copy(
            src_ref=_sub(ins[0].at[layer], axis, (2 * px + py) * size, size), dst_ref=outs[0].at[layer, k],
            send_sem=send_sems.at[k], recv_sem=recv_sems.at[k], device_id=(px, py, c), device_id_type=MESH)
            for k, (px, py) in enumerate(chips)]

    def start(ins, outs, send_sems, recv_sems):
        for cp in copies(ins, outs, send_sems, recv_sems):
            cp.start()

    def finish(ins, outs, send_sems, recv_sems):
        cps = copies(ins, outs, send_sems, recv_sems)
        for cp in cps:
            cp.wait_recv()
        for cp in cps:
            cp.wait_send()

    out = jax.ShapeDtypeStruct((L, 3) + shard, grad.dtype)
    if landing is None:
        return _Rider((grad,), (out,), 3, start, finish)
    return _Rider((grad, landing), (out,), 3, start, finish, aliases=((1, 0),))


def _join_riders(riders):
    if len(riders) == 1:
        return riders[0]

    def parts(ins, outs, send_sems, recv_sems):
        i0 = o0 = s0 = 0
        for r in riders:
            ni, no = len(r.operands), len(r.out_shapes)
            yield (r, ins[i0:i0 + ni], outs[o0:o0 + no], send_sems.at[pl.ds(s0, r.n_sems)],
                   recv_sems.at[pl.ds(s0, r.n_sems)])
            i0, o0, s0 = i0 + ni, o0 + no, s0 + r.n_sems

    def start(*refs):
        for r, *own in parts(*refs):
            r.start(*own)

    def finish(*refs):
        for r, *own in parts(*refs):
            r.finish(*own)

    aliases, i0, o0 = [], 0, 0
    for r in riders:
        aliases += [(i0 + src, o0 + dst) for src, dst in r.aliases]
        i0, o0 = i0 + len(r.operands), o0 + len(r.out_shapes)
    return _Rider(tuple(x for r in riders for x in r.operands), tuple(x for r in riders for x in r.out_shapes),
                  sum(r.n_sems for r in riders), start, finish, tuple(aliases))


def _rows_tile(rows, cols):
    tr = rows
    while tr * cols > (1 << 19) and tr % 16 == 0:
        tr //= 2
    return tr


def _chip_sum(name, grad, landed, chip, saxis):
    L = grad.shape[0]
    _, _, R, C = landed.shape
    tr = _rows_tile(R, C)
    nr = R // tr
    if saxis == 2:
        g_idx = lambda l, i, s: (l, i, s[0])
    else:
        g_idx = lambda l, i, s: (l, s[0] * nr + i, 0)

    def body(s_ref, g_ref, l_ref, o_ref):
        tot = ((g_ref[...].astype(F32) + l_ref[0].astype(F32)) + l_ref[1].astype(F32)) + l_ref[2].astype(F32)
        o_ref[...] = tot.astype(o_ref.dtype)

    return pl.pallas_call(
        body,
        name=name,
        grid_spec=pltpu.PrefetchScalarGridSpec(
            num_scalar_prefetch=1,
            grid=(L, nr),
            in_specs=[pl.BlockSpec((None, tr, C), g_idx), pl.BlockSpec((None, 3, tr, C), lambda l, i, s: (l, 0, i, 0))],
            out_specs=pl.BlockSpec((None, tr, C), lambda l, i, s: (l, i, 0)),
        ),
        out_shape=jax.ShapeDtypeStruct((L, R, C), BF16),
        compiler_params=_cp(("parallel", "parallel")),
    )(chip, grad, landed)


def _all_reduce_small(packed, sums):
    R = packed.shape[0]
    n = len(sums)

    def body(p_ref, *refs):
        s_in, o_ref, s_out = refs[:n], refs[n], refs[n + 1:2 * n + 1]
        land_ref, send_sems, recv_sems = refs[2 * n + 1:]
        x, y, c, _ = _place()
        me = 4 * x + 2 * y + c
        swaps = [pltpu.make_async_remote_copy(src_ref=s_in[w], dst_ref=s_out[w], send_sem=send_sems.at[N_DEV - 1 + w],
                                              recv_sem=recv_sems.at[N_DEV - 1 + w], device_id=(x, y, 1 - c),
                                              device_id_type=MESH) for w in range(n)]
        for cp in swaps:
            cp.start()
        sends, recvs = [], []
        for r in range(1, N_DEV):
            px, py, pc = x ^ (r >> 2), y ^ ((r >> 1) & 1), c ^ (r & 1)
            cp = pltpu.make_async_remote_copy(src_ref=p_ref, dst_ref=land_ref.at[me], send_sem=send_sems.at[r - 1],
                                              recv_sem=recv_sems.at[r - 1], device_id=(px, py, pc), device_id_type=MESH)
            cp.start()
            sends.append(cp)
            recvs.append(pltpu.make_async_remote_copy(src_ref=p_ref, dst_ref=land_ref.at[4 * px + 2 * py + pc],
                                                      send_sem=send_sems.at[r - 1], recv_sem=recv_sems.at[r - 1],
                                                      device_id=(px, py, pc), device_id_type=MESH))
        land_ref[me] = p_ref[...]
        for cp in recvs:
            cp.wait_recv()
        for cp in sends:
            cp.wait_send()
        acc = land_ref[0]
        for d in range(1, N_DEV):
            acc = acc + land_ref[d]
        o_ref[...] = acc
        for cp in swaps:
            cp.wait_recv()
        for cp in swaps:
            cp.wait_send()

    vm = pl.BlockSpec(memory_space=pltpu.VMEM)
    outs = pl.pallas_call(
        body,
        name="all_reduce_small",
        in_specs=[vm] + [_ANY] * n,
        out_specs=[vm] + [_ANY] * n,
        out_shape=[jax.ShapeDtypeStruct((R, LANE), F32)] + [jax.ShapeDtypeStruct(s.shape, s.dtype) for s in sums],
        scratch_shapes=[pltpu.VMEM((N_DEV, R, LANE), F32), pltpu.SemaphoreType.DMA((N_DEV - 1 + n,)),
                        pltpu.SemaphoreType.DMA((N_DEV - 1 + n,))],
        compiler_params=pltpu.CompilerParams(has_side_effects=True, vmem_limit_bytes=VMEM_LIMIT),
    )(packed, *sums)
    return outs[0], list(outs[1:])


def _adamw(name, w, m, v, grads):
    R, C = w.shape
    tr = _rows_tile(R, C)
    c1 = 1.0 - ADAM_B1 ** ADAM_STEP
    c2 = 1.0 - ADAM_B2 ** ADAM_STEP
    ng = len(grads)

    def body(*refs):
        w_ref, m_ref, v_ref = refs[:3]
        g_refs = refs[3:3 + ng]
        g_ref, d_ref, nm_ref, nv_ref = refs[3 + ng:]
        gv = g_refs[0][...].astype(F32)
        for r in g_refs[1:]:
            gv = gv + r[...].astype(F32)
        g_ref[...] = gv
        nm = ADAM_B1 * m_ref[...] + (1.0 - ADAM_B1) * gv
        nv = ADAM_B2 * v_ref[...] + (1.0 - ADAM_B2) * (gv * gv)
        nm_ref[...] = nm
        nv_ref[...] = nv
        d_ref[...] = -ADAM_LR * ((nm / c1) / (jnp.sqrt(nv / c2) + ADAM_EPS) + ADAM_WD * w_ref[...])

    blk = pl.BlockSpec((tr, C), lambda i: (i, 0))
    out = jax.ShapeDtypeStruct((R, C), F32)
    return pl.pallas_call(
        body,
        name=name,
        grid=(R // tr,),
        in_specs=[blk] * (3 + ng),
        out_specs=[blk] * 4,
        out_shape=[out] * 4,
        compiler_params=_cp(("parallel",)),
    )(w, m, v, *grads)


def _pack(parts):
    rows = []
    for p in parts:
        flat = p.reshape(-1).astype(F32)
        n = -(-flat.shape[0] // (8 * LANE)) * (8 * LANE)
        rows.append(jnp.pad(flat, (0, n - flat.shape[0])).reshape(n // LANE, LANE))
    return jnp.concatenate(rows, axis=0)


def _unpack(packed, like):
    out, r = [], 0
    for p in like:
        size = int(np.prod(p.shape))
        n = -(-size // (8 * LANE)) * 8
        out.append(packed[r:r + n].reshape(-1)[:size].reshape(p.shape))
        r += n
    return out


def kernel(x, mix_norm, ffn_norm, w_ffn_in, w_ffn_out, ab_w_in, ab_gn_gain, ab_w_pool, ab_pool_scale, ab_w_out, c_w_qkv, c_rel_bias, c_w_out, final_norm, loss_target, m_mix_norm, m_ffn_norm, m_w_ffn_in, m_w_ffn_out, m_ab_w_in, m_ab_gn_gain, m_ab_w_pool, m_ab_pool_scale, m_ab_w_out, m_c_w_qkv, m_c_rel_bias, m_c_w_out, m_final_norm, v_mix_norm, v_ffn_norm, v_w_ffn_in, v_w_ffn_out, v_ab_w_in, v_ab_gn_gain, v_ab_w_pool, v_ab_pool_scale, v_ab_w_out, v_c_w_qkv, v_c_rel_bias, v_c_w_out, v_final_norm):
    w = dict(mix_norm=mix_norm, ffn_norm=ffn_norm, w_ffn_in=w_ffn_in, w_ffn_out=w_ffn_out, ab_w_in=ab_w_in,
             ab_gn_gain=ab_gn_gain, ab_w_pool=ab_w_pool, ab_pool_scale=ab_pool_scale, ab_w_out=ab_w_out,
             c_w_qkv=c_w_qkv, c_rel_bias=c_rel_bias, c_w_out=c_w_out, final_norm=final_norm)
    m = dict(mix_norm=m_mix_norm, ffn_norm=m_ffn_norm, w_ffn_in=m_w_ffn_in, w_ffn_out=m_w_ffn_out, ab_w_in=m_ab_w_in,
             ab_gn_gain=m_ab_gn_gain, ab_w_pool=m_ab_w_pool, ab_pool_scale=m_ab_pool_scale, ab_w_out=m_ab_w_out,
             c_w_qkv=m_c_w_qkv, c_rel_bias=m_c_rel_bias, c_w_out=m_c_w_out, final_norm=m_final_norm)
    v = dict(mix_norm=v_mix_norm, ffn_norm=v_ffn_norm, w_ffn_in=v_w_ffn_in, w_ffn_out=v_w_ffn_out, ab_w_in=v_ab_w_in,
             ab_gn_gain=v_ab_gn_gain, ab_w_pool=v_ab_w_pool, ab_pool_scale=v_ab_pool_scale, ab_w_out=v_ab_w_out,
             c_w_qkv=v_c_w_qkv, c_rel_bias=v_c_rel_bias, c_w_out=v_c_w_out, final_norm=v_final_norm)
    S = x.shape[1]
    cx, cy, cc = lax.axis_index("x"), lax.axis_index("y"), lax.axis_index("c")
    chip = jnp.reshape(2 * cx + cy, (1,)).astype(jnp.int32)

    big = _Weights({k: w[k].astype(BF16) for k in _BIG})
    small = {k: w[k] for k in _SMALL}
    loss, grad_x, g_small, g_big, landed = _local_step(x.reshape(S, D_MODEL), loss_target.reshape(S, D_MODEL), small, big)

    sums = [_chip_sum(f"chip_sum_{k}", g_big[k], landed[k], chip, _SHARD_AXIS[k]) for k in _BIG]

    packed, siblings = _all_reduce_small(_pack([g_small[k] for k in _SMALL] + [loss]), sums)
    small_like = [w[k] for k in _SMALL]
    g_red = dict(zip(_SMALL, _unpack(packed, small_like)))
    loss_row = packed.shape[0] - 8
    loss_out = packed[loss_row, 0]

    grad, delta, new_m, new_v = {}, {}, {}, {}
    for k, mine, theirs in zip(_BIG, sums, siblings):
        shp = w[k].shape
        two = (shp[0] * shp[1], shp[2])
        outs = _adamw(f"adamw_{k}", w[k].reshape(two), m[k].reshape(two), v[k].reshape(two),
                      (mine.reshape(two), theirs.reshape(two)))
        grad[k], delta[k], new_m[k], new_v[k] = [o.reshape(shp) for o in outs]
    _, d, nm, nv = _adamw("adamw_small", _pack(small_like), _pack([m[k] for k in _SMALL]), _pack([v[k] for k in _SMALL]),
                          (packed[:loss_row],))
    for k, dk, mk, vk in zip(_SMALL, _unpack(d, small_like), _unpack(nm, small_like), _unpack(nv, small_like)):
        grad[k], delta[k], new_m[k], new_v[k] = g_red[k], dk, mk, vk

    order = ("mix_norm", "ffn_norm", "w_ffn_in", "w_ffn_out", "ab_w_in", "ab_gn_gain", "ab_w_pool", "ab_pool_scale",
             "ab_w_out", "c_w_qkv", "c_rel_bias", "c_w_out", "final_norm")
    return (loss_out, grad_x.reshape(x.shape), *[grad[k] for k in order], *[delta[k] for k in order],
            *[new_m[k] for k in order], *[new_v[k] for k in order])
```
